```python
import math
import jax, jax.numpy as jnp
from jax import lax
import numpy as np

D_MODEL = 1024
BATCH = 8
SEQ = 8192
DEPTH = 4

N_HEADS = 16
N_KV_HEADS = 4
HEAD_DIM = 64
GROUP = N_HEADS // N_KV_HEADS
WINDOW = 128
BLOCK = 128
ROPE_THETA = 10000.0
D_RNN = 1024
RNN_BLOCKS = 4
RNN_BLOCK_W = D_RNN // RNN_BLOCKS
CONV_W = 4
LRU_C = 8.0
D_FF = 2816
DEEPNORM_ALPHA = (2.0 * DEPTH) ** 0.25
DEEPNORM_BETA = (8.0 * DEPTH) ** -0.25
LN_EPS = 1e-5
N_MIXERS = 2
N_ATTN_LAYERS = (DEPTH + 1) // 2
N_LRU_LAYERS = DEPTH // 2
QKV_COLS = (N_HEADS + 2 * N_KV_HEADS) * HEAD_DIM

kernel_name = "hybrid_swa_sink_rglru_macaron_deepnorm"


def _layernorm(x, g, b):
    xf = x.astype(jnp.float32)
    mu = jnp.mean(xf, axis=-1, keepdims=True)
    xc = xf - mu
    var = jnp.mean(xc * xc, axis=-1, keepdims=True)
    y = xc * lax.rsqrt(var + LN_EPS) * g.astype(jnp.float32) + b.astype(jnp.float32)
    return y.astype(x.dtype)


def _swiglu(x, w_in, w_out):
    g, u = jnp.split(x @ w_in, 2, axis=-1)
    return (jax.nn.silu(g) * u) @ w_out


def _rope(t, cos, sin):
    t1, t2 = jnp.split(t, 2, axis=-1)
    c = cos[None, :, None, :]
    s = sin[None, :, None, :]
    out = jnp.concatenate([t1 * c - t2 * s, t2 * c + t1 * s], axis=-1)
    return out.astype(t.dtype)


def _band(t):
    prev = jnp.pad(t[:, :-1], ((0, 0), (1, 0), (0, 0), (0, 0), (0, 0)))
    return jnp.concatenate([prev, t], axis=2)


def _swa_sink_attention(x, w_qkv, sinks, w_o, cos, sin):
    B, S, _ = x.shape
    nb = S // BLOCK
    qkv = x @ w_qkv
    q, k, v = jnp.split(qkv, [N_HEADS * HEAD_DIM, (N_HEADS + N_KV_HEADS) * HEAD_DIM], axis=-1)
    q = _rope(q.reshape(B, S, N_HEADS, HEAD_DIM), cos, sin)
    k = _rope(k.reshape(B, S, N_KV_HEADS, HEAD_DIM), cos, sin)
    v = v.reshape(B, S, N_KV_HEADS, HEAD_DIM)
    q = q.reshape(B, nb, BLOCK, N_KV_HEADS, GROUP, HEAD_DIM)
    kb = _band(k.reshape(B, nb, BLOCK, N_KV_HEADS, HEAD_DIM))
    vb = _band(v.reshape(B, nb, BLOCK, N_KV_HEADS, HEAD_DIM))
    s = jnp.einsum('bnqkgd,bnjkd->bnkgqj', q, kb).astype(jnp.float32) * (HEAD_DIM ** -0.5)
    qi = jnp.arange(BLOCK)[:, None]
    kj = jnp.arange(2 * BLOCK)[None, :]
    dist = qi + BLOCK - kj
    in_window = (dist >= 0) & (dist < WINDOW)
    k_pos = jnp.arange(nb)[:, None] * BLOCK - BLOCK + jnp.arange(2 * BLOCK)[None, :]
    mask = in_window[None, :, :] & (k_pos >= 0)[:, None, :]
    s = jnp.where(mask[None, :, None, None, :, :], s, jnp.finfo(jnp.float32).min)
    sink = sinks.astype(jnp.float32).reshape(N_KV_HEADS, GROUP)[None, None, :, :, None, None]
    m = jnp.maximum(jnp.max(s, axis=-1, keepdims=True), sink)
    p = jnp.exp(s - m)
    denom = jnp.sum(p, axis=-1, keepdims=True) + jnp.exp(sink - m)
    p = (p / denom).astype(x.dtype)
    o = jnp.einsum('bnkgqj,bnjkd->bnqkgd', p, vb).reshape(B, S, N_HEADS * HEAD_DIM)
    return o @ w_o


def _lru_combine(left, right):
    a1, b1 = left
    a2, b2 = right
    return a1 * a2, a2 * b1 + b2


def _rglru_block(x, w_in, conv_w, conv_b, w_ra, b_ra, w_rx, b_rx, lam, w_out):
    B, S, _ = x.shape
    xb, gb = jnp.split(x @ w_in, 2, axis=-1)
    gate = jax.nn.gelu(gb)
    xc = lax.conv_general_dilated(
        xb, conv_w[:, None, :].astype(xb.dtype), window_strides=(1,), padding=[(CONV_W - 1, 0)],
        dimension_numbers=('NWC', 'WIO', 'NWC'), feature_group_count=D_RNN) + conv_b
    xr = xc.reshape(B, S, RNN_BLOCKS, RNN_BLOCK_W)
    r = jax.nn.sigmoid(jnp.einsum('bsnc,ncd->bsnd', xr, w_ra).reshape(B, S, D_RNN) + b_ra)
    i = jax.nn.sigmoid(jnp.einsum('bsnc,ncd->bsnd', xr, w_rx).reshape(B, S, D_RNN) + b_rx)
    log_a = LRU_C * r.astype(jnp.float32) * jax.nn.log_sigmoid(lam.astype(jnp.float32))
    a = jnp.exp(log_a)
    b = jnp.sqrt(-jnp.expm1(2.0 * log_a)) * (i * xc).astype(jnp.float32)
    _, h = lax.associative_scan(_lru_combine, (a, b), axis=1)
    y = h.astype(x.dtype) * gate
    return y @ w_out


def _fwd_setup_inputs(seed: int = 0) -> dict:
    key = jax.random.key(seed)
    ks = jax.random.split(key, 24)
    f32 = jnp.float32
    nrm = lambda k, shape, scale: jax.random.normal(k, shape, f32) * scale
    x = jax.random.normal(ks[0], (BATCH, SEQ, D_MODEL), f32)
    ffn1_w_in = nrm(ks[1], (DEPTH, D_MODEL, 2 * D_FF), D_MODEL ** -0.5)
    ffn1_w_out = nrm(ks[2], (DEPTH, D_FF, D_MODEL), D_FF ** -0.5 * DEEPNORM_BETA)
    ffn2_w_in = nrm(ks[3], (DEPTH, D_MODEL, 2 * D_FF), D_MODEL ** -0.5)
    ffn2_w_out = nrm(ks[4], (DEPTH, D_FF, D_MODEL), D_FF ** -0.5 * DEEPNORM_BETA)
    ln_g = 1.0 + nrm(ks[5], (DEPTH, 3, D_MODEL), 0.02)
    ln_b = nrm(ks[6], (DEPTH, 3, D_MODEL), 0.02)
    attn_w_qkv = nrm(ks[7], (N_ATTN_LAYERS, D_MODEL, QKV_COLS), D_MODEL ** -0.5)
    attn_sinks = nrm(ks[8], (N_ATTN_LAYERS, N_HEADS), 0.5)
    attn_w_o = nrm(ks[9], (N_ATTN_LAYERS, N_HEADS * HEAD_DIM, D_MODEL), (N_HEADS * HEAD_DIM) ** -0.5 * DEEPNORM_BETA)
    lru_w_in = nrm(ks[10], (N_LRU_LAYERS, D_MODEL, 2 * D_RNN), D_MODEL ** -0.5)
    lru_conv_w = nrm(ks[11], (N_LRU_LAYERS, CONV_W, D_RNN), CONV_W ** -0.5)
    lru_conv_b = nrm(ks[12], (N_LRU_LAYERS, D_RNN), 0.01)
    lru_w_ra = nrm(ks[13], (N_LRU_LAYERS, RNN_BLOCKS, RNN_BLOCK_W, RNN_BLOCK_W), RNN_BLOCK_W ** -0.5)
    lru_b_ra = nrm(ks[14], (N_LRU_LAYERS, D_RNN), 0.01)
    lru_w_rx = nrm(ks[15], (N_LRU_LAYERS, RNN_BLOCKS, RNN_BLOCK_W, RNN_BLOCK_W), RNN_BLOCK_W ** -0.5)
    lru_b_rx = nrm(ks[16], (N_LRU_LAYERS, D_RNN), 0.01)
    a_c = jax.random.uniform(ks[17], (N_LRU_LAYERS, D_RNN), f32, 0.9, 0.999)
    sig = a_c ** (1.0 / LRU_C)
    lru_lambda = jnp.log(sig) - jnp.log1p(-sig)
    lru_w_out = nrm(ks[18], (N_LRU_LAYERS, D_RNN, D_MODEL), D_RNN ** -0.5 * DEEPNORM_BETA)
    return {"x": x, "ffn1_w_in": ffn1_w_in, "ffn1_w_out": ffn1_w_out,
            "ffn2_w_in": ffn2_w_in, "ffn2_w_out": ffn2_w_out, "ln_g": ln_g, "ln_b": ln_b,
            "attn_w_qkv": attn_w_qkv, "attn_sinks": attn_sinks, "attn_w_o": attn_w_o,
            "lru_w_in": lru_w_in, "lru_conv_w": lru_conv_w, "lru_conv_b": lru_conv_b,
            "lru_w_ra": lru_w_ra, "lru_b_ra": lru_b_ra, "lru_w_rx": lru_w_rx, "lru_b_rx": lru_b_rx,
            "lru_lambda": lru_lambda, "lru_w_out": lru_w_out}


def _fwd_reference(x, ffn1_w_in, ffn1_w_out, ffn2_w_in, ffn2_w_out, ln_g, ln_b,
              attn_w_qkv, attn_sinks, attn_w_o,
              lru_w_in, lru_conv_w, lru_conv_b, lru_w_ra, lru_b_ra, lru_w_rx, lru_b_rx,
              lru_lambda, lru_w_out):
    S = x.shape[1]
    pos = jnp.arange(S, dtype=jnp.float32)
    inv_freq = ROPE_THETA ** (-jnp.arange(0, HEAD_DIM, 2, dtype=jnp.float32) / HEAD_DIM)
    ang = pos[:, None] * inv_freq[None, :]
    cos, sin = jnp.cos(ang), jnp.sin(ang)
    h = x
    for i in range(DEPTH):
        h = _layernorm(DEEPNORM_ALPHA * h + 0.5 * _swiglu(h, ffn1_w_in[i], ffn1_w_out[i]), ln_g[i, 0], ln_b[i, 0])
        j = i // N_MIXERS
        if i % N_MIXERS == 0:
            mix = _swa_sink_attention(h, attn_w_qkv[j], attn_sinks[j], attn_w_o[j], cos, sin)
        else:
            mix = _rglru_block(h, lru_w_in[j], lru_conv_w[j], lru_conv_b[j], lru_w_ra[j], lru_b_ra[j],
                               lru_w_rx[j], lru_b_rx[j], lru_lambda[j], lru_w_out[j])
        h = _layernorm(DEEPNORM_ALPHA * h + mix, ln_g[i, 1], ln_b[i, 1])
        h = _layernorm(DEEPNORM_ALPHA * h + 0.5 * _swiglu(h, ffn2_w_in[i], ffn2_w_out[i]), ln_g[i, 2], ln_b[i, 2])
    return h


import jax as _jax
import jax.numpy as _jnp

TWIN_FORMAT = 'train_step'
FWD_PARAMS = ['x', 'ffn1_w_in', 'ffn1_w_out', 'ffn2_w_in', 'ffn2_w_out', 'ln_g', 'ln_b', 'attn_w_qkv', 'attn_sinks', 'attn_w_o', 'lru_w_in', 'lru_conv_w', 'lru_conv_b', 'lru_w_ra', 'lru_b_ra', 'lru_w_rx', 'lru_b_rx', 'lru_lambda', 'lru_w_out']
TWIN_WEIGHTS = ['ffn1_w_in', 'ffn1_w_out', 'ffn2_w_in', 'ffn2_w_out', 'ln_g', 'ln_b', 'attn_w_qkv', 'attn_sinks', 'attn_w_o', 'lru_w_in', 'lru_conv_w', 'lru_conv_b', 'lru_w_ra', 'lru_b_ra', 'lru_w_rx', 'lru_b_rx', 'lru_lambda', 'lru_w_out']
TWIN_DIFF_INPUT = 'x'
TWIN_INPUTS = ['x', 'ffn1_w_in', 'ffn1_w_out', 'ffn2_w_in', 'ffn2_w_out', 'ln_g', 'ln_b', 'attn_w_qkv', 'attn_sinks', 'attn_w_o', 'lru_w_in', 'lru_conv_w', 'lru_conv_b', 'lru_w_ra', 'lru_b_ra', 'lru_w_rx', 'lru_b_rx', 'lru_lambda', 'lru_w_out', 'loss_target', 'm_ffn1_w_in', 'm_ffn1_w_out', 'm_ffn2_w_in', 'm_ffn2_w_out', 'm_ln_g', 'm_ln_b', 'm_attn_w_qkv', 'm_attn_sinks', 'm_attn_w_o', 'm_lru_w_in', 'm_lru_conv_w', 'm_lru_conv_b', 'm_lru_w_ra', 'm_lru_b_ra', 'm_lru_w_rx', 'm_lru_b_rx', 'm_lru_lambda', 'm_lru_w_out', 'v_ffn1_w_in', 'v_ffn1_w_out', 'v_ffn2_w_in', 'v_ffn2_w_out', 'v_ln_g', 'v_ln_b', 'v_attn_w_qkv', 'v_attn_sinks', 'v_attn_w_o', 'v_lru_w_in', 'v_lru_conv_w', 'v_lru_conv_b', 'v_lru_w_ra', 'v_lru_b_ra', 'v_lru_w_rx', 'v_lru_b_rx', 'v_lru_lambda', 'v_lru_w_out']
TWIN_OUTPUTS = ['loss', 'grad_x', 'grad_ffn1_w_in', 'grad_ffn1_w_out', 'grad_ffn2_w_in', 'grad_ffn2_w_out', 'grad_ln_g', 'grad_ln_b', 'grad_attn_w_qkv', 'grad_attn_sinks', 'grad_attn_w_o', 'grad_lru_w_in', 'grad_lru_conv_w', 'grad_lru_conv_b', 'grad_lru_w_ra', 'grad_lru_b_ra', 'grad_lru_w_rx', 'grad_lru_b_rx', 'grad_lru_lambda', 'grad_lru_w_out', 'delta_ffn1_w_in', 'delta_ffn1_w_out', 'delta_ffn2_w_in', 'delta_ffn2_w_out', 'delta_ln_g', 'delta_ln_b', 'delta_attn_w_qkv', 'delta_attn_sinks', 'delta_attn_w_o', 'delta_lru_w_in', 'delta_lru_conv_w', 'delta_lru_conv_b', 'delta_lru_w_ra', 'delta_lru_b_ra', 'delta_lru_w_rx', 'delta_lru_b_rx', 'delta_lru_lambda', 'delta_lru_w_out', 'new_m_ffn1_w_in', 'new_m_ffn1_w_out', 'new_m_ffn2_w_in', 'new_m_ffn2_w_out', 'new_m_ln_g', 'new_m_ln_b', 'new_m_attn_w_qkv', 'new_m_attn_sinks', 'new_m_attn_w_o', 'new_m_lru_w_in', 'new_m_lru_conv_w', 'new_m_lru_conv_b', 'new_m_lru_w_ra', 'new_m_lru_b_ra', 'new_m_lru_w_rx', 'new_m_lru_b_rx', 'new_m_lru_lambda', 'new_m_lru_w_out', 'new_v_ffn1_w_in', 'new_v_ffn1_w_out', 'new_v_ffn2_w_in', 'new_v_ffn2_w_out', 'new_v_ln_g', 'new_v_ln_b', 'new_v_attn_w_qkv', 'new_v_attn_sinks', 'new_v_attn_w_o', 'new_v_lru_w_in', 'new_v_lru_conv_w', 'new_v_lru_conv_b', 'new_v_lru_w_ra', 'new_v_lru_b_ra', 'new_v_lru_w_rx', 'new_v_lru_b_rx', 'new_v_lru_lambda', 'new_v_lru_w_out']
TWIN_LEAF_KINDS = {'loss': 'loss', 'grad_x': 'grad_x', 'grad_ffn1_w_in': 'grad_w', 'grad_ffn1_w_out': 'grad_w', 'grad_ffn2_w_in': 'grad_w', 'grad_ffn2_w_out': 'grad_w', 'grad_ln_g': 'grad_w', 'grad_ln_b': 'grad_w', 'grad_attn_w_qkv': 'grad_w', 'grad_attn_sinks': 'grad_w', 'grad_attn_w_o': 'grad_w', 'grad_lru_w_in': 'grad_w', 'grad_lru_conv_w': 'grad_w', 'grad_lru_conv_b': 'grad_w', 'grad_lru_w_ra': 'grad_w', 'grad_lru_b_ra': 'grad_w', 'grad_lru_w_rx': 'grad_w', 'grad_lru_b_rx': 'grad_w', 'grad_lru_lambda': 'grad_w', 'grad_lru_w_out': 'grad_w', 'delta_ffn1_w_in': 'delta_w', 'delta_ffn1_w_out': 'delta_w', 'delta_ffn2_w_in': 'delta_w', 'delta_ffn2_w_out': 'delta_w', 'delta_ln_g': 'delta_w', 'delta_ln_b': 'delta_w', 'delta_attn_w_qkv': 'delta_w', 'delta_attn_sinks': 'delta_w', 'delta_attn_w_o': 'delta_w', 'delta_lru_w_in': 'delta_w', 'delta_lru_conv_w': 'delta_w', 'delta_lru_conv_b': 'delta_w', 'delta_lru_w_ra': 'delta_w', 'delta_lru_b_ra': 'delta_w', 'delta_lru_w_rx': 'delta_w', 'delta_lru_b_rx': 'delta_w', 'delta_lru_lambda': 'delta_w', 'delta_lru_w_out': 'delta_w', 'new_m_ffn1_w_in': 'new_m', 'new_m_ffn1_w_out': 'new_m', 'new_m_ffn2_w_in': 'new_m', 'new_m_ffn2_w_out': 'new_m', 'new_m_ln_g': 'new_m', 'new_m_ln_b': 'new_m', 'new_m_attn_w_qkv': 'new_m', 'new_m_attn_sinks': 'new_m', 'new_m_attn_w_o': 'new_m', 'new_m_lru_w_in': 'new_m', 'new_m_lru_conv_w': 'new_m', 'new_m_lru_conv_b': 'new_m', 'new_m_lru_w_ra': 'new_m', 'new_m_lru_b_ra': 'new_m', 'new_m_lru_w_rx': 'new_m', 'new_m_lru_b_rx': 'new_m', 'new_m_lru_lambda': 'new_m', 'new_m_lru_w_out': 'new_m', 'new_v_ffn1_w_in': 'new_v', 'new_v_ffn1_w_out': 'new_v', 'new_v_ffn2_w_in': 'new_v', 'new_v_ffn2_w_out': 'new_v', 'new_v_ln_g': 'new_v', 'new_v_ln_b': 'new_v', 'new_v_attn_w_qkv': 'new_v', 'new_v_attn_sinks': 'new_v', 'new_v_attn_w_o': 'new_v', 'new_v_lru_w_in': 'new_v', 'new_v_lru_conv_w': 'new_v', 'new_v_lru_conv_b': 'new_v', 'new_v_lru_w_ra': 'new_v', 'new_v_lru_b_ra': 'new_v', 'new_v_lru_w_rx': 'new_v', 'new_v_lru_b_rx': 'new_v', 'new_v_lru_lambda': 'new_v', 'new_v_lru_w_out': 'new_v'}


def _forward(args):
    return _fwd_reference(*[args[k] for k in FWD_PARAMS])


def _output_shape():
    def fwd():
        inp = _fwd_setup_inputs(0)
        return _fwd_reference(*[inp[k] for k in FWD_PARAMS])
    out = _jax.eval_shape(fwd)
    return out.shape, out.dtype

N_MICROBATCH = 1
ADAM_LR = 0.001
ADAM_B1 = 0.9
ADAM_B2 = 0.999
ADAM_EPS = 1e-08
ADAM_WD = 0.01
ADAM_STEP = 10
PER_EXAMPLE_BATCH_AXIS = {'x': 0, 'loss_target': 0}
SHARED_INPUTS = []
_WEIGHT_DTYPES = {'ffn1_w_in': _jnp.float32, 'ffn1_w_out': _jnp.float32, 'ffn2_w_in': _jnp.float32, 'ffn2_w_out': _jnp.float32, 'ln_g': _jnp.float32, 'ln_b': _jnp.float32, 'attn_w_qkv': _jnp.float32, 'attn_sinks': _jnp.float32, 'attn_w_o': _jnp.float32, 'lru_w_in': _jnp.float32, 'lru_conv_w': _jnp.float32, 'lru_conv_b': _jnp.float32, 'lru_w_ra': _jnp.float32, 'lru_b_ra': _jnp.float32, 'lru_w_rx': _jnp.float32, 'lru_b_rx': _jnp.float32, 'lru_lambda': _jnp.float32, 'lru_w_out': _jnp.float32}
MOMENT_SCALE = {'ffn1_w_in': 1.165144e-02, 'ffn1_w_out': 4.527482e-02, 'ffn2_w_in': 1.159241e-02, 'ffn2_w_out': 4.504414e-02, 'ln_g': 1.869717e+01, 'ln_b': 1.750740e+00, 'attn_w_qkv': 1.799182e-02, 'attn_sinks': 9.060459e-03, 'attn_w_o': 3.531556e-02, 'lru_w_in': 3.484651e-02, 'lru_conv_w': 3.813115e-02, 'lru_conv_b': 6.250273e-01, 'lru_w_ra': 9.979134e-03, 'lru_b_ra': 1.039443e-02, 'lru_w_rx': 1.828487e-02, 'lru_b_rx': 1.443629e-02, 'lru_lambda': 2.121933e-02, 'lru_w_out': 9.202092e-02}


def _to_microbatches(a, axis):
    t = _jnp.moveaxis(a, axis, 0)
    t = t.reshape((N_MICROBATCH, t.shape[0] // N_MICROBATCH) + t.shape[1:])
    return _jnp.moveaxis(t, 1, axis + 1)


def setup_inputs(seed: int = 0) -> dict:
    inp = _fwd_setup_inputs(seed)
    key = _jax.random.fold_in(_jax.random.key(seed), 7919)
    shape, _ = _output_shape()
    out = dict(inp)
    out["loss_target"] = _jax.random.normal(_jax.random.fold_in(key, 0), shape, _jnp.float32)
    for i, name in enumerate(TWIN_WEIGHTS):
        w = inp[name].astype(_jnp.float32)
        if MOMENT_SCALE is None:
            s = _jnp.sqrt(_jnp.mean(_jnp.square(w)) + 1e-30)
        else:
            s = MOMENT_SCALE[name]
        km, kv = _jax.random.split(_jax.random.fold_in(key, i + 1))
        out[name] = w
        out["m_" + name] = s * _jax.random.normal(km, w.shape, _jnp.float32)
        out["v_" + name] = (s * s) * _jax.random.uniform(kv, w.shape, _jnp.float32, 0.5, 1.5)
    if N_MICROBATCH > 1:
        for name, axis in PER_EXAMPLE_BATCH_AXIS.items():
            out[name] = _to_microbatches(out[name], axis)
    return {'x': out['x'], 'ffn1_w_in': out['ffn1_w_in'], 'ffn1_w_out': out['ffn1_w_out'], 'ffn2_w_in': out['ffn2_w_in'], 'ffn2_w_out': out['ffn2_w_out'], 'ln_g': out['ln_g'], 'ln_b': out['ln_b'], 'attn_w_qkv': out['attn_w_qkv'], 'attn_sinks': out['attn_sinks'], 'attn_w_o': out['attn_w_o'], 'lru_w_in': out['lru_w_in'], 'lru_conv_w': out['lru_conv_w'], 'lru_conv_b': out['lru_conv_b'], 'lru_w_ra': out['lru_w_ra'], 'lru_b_ra': out['lru_b_ra'], 'lru_w_rx': out['lru_w_rx'], 'lru_b_rx': out['lru_b_rx'], 'lru_lambda': out['lru_lambda'], 'lru_w_out': out['lru_w_out'], 'loss_target': out['loss_target'], 'm_ffn1_w_in': out['m_ffn1_w_in'], 'm_ffn1_w_out': out['m_ffn1_w_out'], 'm_ffn2_w_in': out['m_ffn2_w_in'], 'm_ffn2_w_out': out['m_ffn2_w_out'], 'm_ln_g': out['m_ln_g'], 'm_ln_b': out['m_ln_b'], 'm_attn_w_qkv': out['m_attn_w_qkv'], 'm_attn_sinks': out['m_attn_sinks'], 'm_attn_w_o': out['m_attn_w_o'], 'm_lru_w_in': out['m_lru_w_in'], 'm_lru_conv_w': out['m_lru_conv_w'], 'm_lru_conv_b': out['m_lru_conv_b'], 'm_lru_w_ra': out['m_lru_w_ra'], 'm_lru_b_ra': out['m_lru_b_ra'], 'm_lru_w_rx': out['m_lru_w_rx'], 'm_lru_b_rx': out['m_lru_b_rx'], 'm_lru_lambda': out['m_lru_lambda'], 'm_lru_w_out': out['m_lru_w_out'], 'v_ffn1_w_in': out['v_ffn1_w_in'], 'v_ffn1_w_out': out['v_ffn1_w_out'], 'v_ffn2_w_in': out['v_ffn2_w_in'], 'v_ffn2_w_out': out['v_ffn2_w_out'], 'v_ln_g': out['v_ln_g'], 'v_ln_b': out['v_ln_b'], 'v_attn_w_qkv': out['v_attn_w_qkv'], 'v_attn_sinks': out['v_attn_sinks'], 'v_attn_w_o': out['v_attn_w_o'], 'v_lru_w_in': out['v_lru_w_in'], 'v_lru_conv_w': out['v_lru_conv_w'], 'v_lru_conv_b': out['v_lru_conv_b'], 'v_lru_w_ra': out['v_lru_w_ra'], 'v_lru_b_ra': out['v_lru_b_ra'], 'v_lru_w_rx': out['v_lru_w_rx'], 'v_lru_b_rx': out['v_lru_b_rx'], 'v_lru_lambda': out['v_lru_lambda'], 'v_lru_w_out': out['v_lru_w_out']}


def _loss(weights, diff, rest, loss_target):
    with _jax.named_scope("forward"):
        args = {**rest, TWIN_DIFF_INPUT: diff, **{k: w.astype(_WEIGHT_DTYPES[k]) for k, w in weights.items()}}
        y = _forward(args)
    with _jax.named_scope("loss_head"):
        err = _jnp.square(y.astype(_jnp.float32) - loss_target)
        return 0.5 * _jnp.sum(_jnp.mean(err, axis=-1)) if err.ndim else 0.5 * err


def _adamw(w, g, m, v):
    m = ADAM_B1 * m + (1.0 - ADAM_B1) * g
    v = ADAM_B2 * v + (1.0 - ADAM_B2) * _jnp.square(g)
    m_hat = m / (1.0 - ADAM_B1 ** ADAM_STEP)
    v_hat = v / (1.0 - ADAM_B2 ** ADAM_STEP)
    delta = -ADAM_LR * (m_hat / (_jnp.sqrt(v_hat) + ADAM_EPS) + ADAM_WD * w)
    return delta, m, v


def reference(x, ffn1_w_in, ffn1_w_out, ffn2_w_in, ffn2_w_out, ln_g, ln_b, attn_w_qkv, attn_sinks, attn_w_o, lru_w_in, lru_conv_w, lru_conv_b, lru_w_ra, lru_b_ra, lru_w_rx, lru_b_rx, lru_lambda, lru_w_out, loss_target, m_ffn1_w_in, m_ffn1_w_out, m_ffn2_w_in, m_ffn2_w_out, m_ln_g, m_ln_b, m_attn_w_qkv, m_attn_sinks, m_attn_w_o, m_lru_w_in, m_lru_conv_w, m_lru_conv_b, m_lru_w_ra, m_lru_b_ra, m_lru_w_rx, m_lru_b_rx, m_lru_lambda, m_lru_w_out, v_ffn1_w_in, v_ffn1_w_out, v_ffn2_w_in, v_ffn2_w_out, v_ln_g, v_ln_b, v_attn_w_qkv, v_attn_sinks, v_attn_w_o, v_lru_w_in, v_lru_conv_w, v_lru_conv_b, v_lru_w_ra, v_lru_b_ra, v_lru_w_rx, v_lru_b_rx, v_lru_lambda, v_lru_w_out):
    given = dict(x=x, ffn1_w_in=ffn1_w_in, ffn1_w_out=ffn1_w_out, ffn2_w_in=ffn2_w_in, ffn2_w_out=ffn2_w_out, ln_g=ln_g, ln_b=ln_b, attn_w_qkv=attn_w_qkv, attn_sinks=attn_sinks, attn_w_o=attn_w_o, lru_w_in=lru_w_in, lru_conv_w=lru_conv_w, lru_conv_b=lru_conv_b, lru_w_ra=lru_w_ra, lru_b_ra=lru_b_ra, lru_w_rx=lru_w_rx, lru_b_rx=lru_b_rx, lru_lambda=lru_lambda, lru_w_out=lru_w_out, loss_target=loss_target, m_ffn1_w_in=m_ffn1_w_in, m_ffn1_w_out=m_ffn1_w_out, m_ffn2_w_in=m_ffn2_w_in, m_ffn2_w_out=m_ffn2_w_out, m_ln_g=m_ln_g, m_ln_b=m_ln_b, m_attn_w_qkv=m_attn_w_qkv, m_attn_sinks=m_attn_sinks, m_attn_w_o=m_attn_w_o, m_lru_w_in=m_lru_w_in, m_lru_conv_w=m_lru_conv_w, m_lru_conv_b=m_lru_conv_b, m_lru_w_ra=m_lru_w_ra, m_lru_b_ra=m_lru_b_ra, m_lru_w_rx=m_lru_w_rx, m_lru_b_rx=m_lru_b_rx, m_lru_lambda=m_lru_lambda, m_lru_w_out=m_lru_w_out, v_ffn1_w_in=v_ffn1_w_in, v_ffn1_w_out=v_ffn1_w_out, v_ffn2_w_in=v_ffn2_w_in, v_ffn2_w_out=v_ffn2_w_out, v_ln_g=v_ln_g, v_ln_b=v_ln_b, v_attn_w_qkv=v_attn_w_qkv, v_attn_sinks=v_attn_sinks, v_attn_w_o=v_attn_w_o, v_lru_w_in=v_lru_w_in, v_lru_conv_w=v_lru_conv_w, v_lru_conv_b=v_lru_conv_b, v_lru_w_ra=v_lru_w_ra, v_lru_b_ra=v_lru_b_ra, v_lru_w_rx=v_lru_w_rx, v_lru_b_rx=v_lru_b_rx, v_lru_lambda=v_lru_lambda, v_lru_w_out=v_lru_w_out)
    weights = {n: given[n] for n in TWIN_WEIGHTS}
    shared = {n: given[n] for n in SHARED_INPUTS}
    per_example = {n: given[n] for n in ['x']}
    grad_fn = _jax.value_and_grad(_loss, argnums=(0, 1))

    def one_microbatch(ex, loss_target):
        ex = dict(ex)
        diff = ex.pop(TWIN_DIFF_INPUT)
        return grad_fn(weights, diff, {**shared, **ex}, loss_target)

    if N_MICROBATCH == 1:
        loss, (grad_w, grad_x) = one_microbatch(per_example, given["loss_target"])
    else:
        def body(carry, xs):
            loss_sum, grad_sum = carry
            l_k, (gw_k, gx_k) = one_microbatch(xs[0], xs[1])
            with _jax.named_scope("update"):
                return (loss_sum + l_k, _jax.tree.map(_jnp.add, grad_sum, gw_k)), gx_k

        init = (_jnp.zeros((), _jnp.float32), _jax.tree.map(_jnp.zeros_like, weights))
        (loss, grad_w), grad_x = _jax.lax.scan(body, init, (per_example, given["loss_target"]))
    with _jax.named_scope("update"):
        delta_w, new_m, new_v = {}, {}, {}
        for n in TWIN_WEIGHTS:
            delta_w[n], new_m[n], new_v[n] = _adamw(weights[n], grad_w[n], given["m_" + n], given["v_" + n])
    return (loss, grad_x, *[grad_w[n] for n in TWIN_WEIGHTS], *[delta_w[n] for n in TWIN_WEIGHTS],
            *[new_m[n] for n in TWIN_WEIGHTS], *[new_v[n] for n in TWIN_WEIGHTS])
```

```python
import math

import jax
import jax.numpy as jnp
import numpy as np
from jax import lax
from jax.experimental import pallas as pl
from jax.experimental.pallas import tpu as pltpu

F32 = jnp.float32
BF16 = jnp.bfloat16

D_MODEL = 1024
DEPTH = 4
N_HEADS = 16
N_KV = 4
HEAD_DIM = 64
GROUP = 4
BLOCK = 128
ROPE_THETA = 10000.0
D_RNN = 1024
RNN_BLOCKS = 4
RNN_W = 256
CONV_W = 4
LRU_C = 8.0
D_FF = 2816
ALPHA = (2.0 * DEPTH) ** 0.25
LN_EPS = 1e-5
QKV = (N_HEADS + 2 * N_KV) * HEAD_DIM
N_DEV = 8

ADAM_LR = 0.001
ADAM_B1 = 0.9
ADAM_B2 = 0.999
ADAM_EPS = 1e-08
ADAM_WD = 0.01
ADAM_STEP = 10

VMEM_LIMIT = 52 * 1024 * 1024
NEG = float(np.finfo(np.float32).min)

MESH = pl.DeviceIdType.MESH
ANY = pl.BlockSpec(memory_space=pl.ANY)

FF_SHARD = D_FF // N_DEV
MIX_SHARD = D_MODEL // N_DEV
R_FFN2 = DEPTH * FF_SHARD
R_ATTN = 2 * DEPTH * FF_SHARD
R_LRU = R_ATTN + 2 * MIX_SHARD
R_ROWS = R_LRU + 2 * MIX_SHARD
C256_RA = 2 * D_MODEL
C256_RX = C256_RA + 2 * RNN_BLOCKS * (RNN_W // N_DEV)
C256_ROWS = C256_RX + 2 * RNN_BLOCKS * (RNN_W // N_DEV)

SMALL = (
    ("ln_g", (4, 3, 128), 2),
    ("ln_b", (4, 3, 128), 2),
    ("lru_conv_w", (2, 4, 128), 2),
    ("lru_conv_b", (2, 128), 1),
    ("lru_b_ra", (2, 128), 1),
    ("lru_b_rx", (2, 128), 1),
    ("lru_lambda", (2, 128), 1),
)
WEIGHTS = ("ffn1_w_in", "ffn1_w_out", "ffn2_w_in", "ffn2_w_out", "ln_g", "ln_b", "attn_w_qkv", "attn_sinks",
           "attn_w_o", "lru_w_in", "lru_conv_w", "lru_conv_b", "lru_w_ra", "lru_b_ra", "lru_w_rx", "lru_b_rx",
           "lru_lambda", "lru_w_out")
SMALL_N = sum(math.prod(s) for _, s, _ in SMALL)
SINK_OFF = SMALL_N
LOSS_OFF = SMALL_N + 32
SM_ROWS = 8


def _cp(*sem):
    return pltpu.CompilerParams(dimension_semantics=sem, vmem_limit_bytes=VMEM_LIMIT)


def _dot(a, b):
    return jnp.dot(a, b, preferred_element_type=F32)


def _dot_tn(a, b):
    return lax.dot_general(a, b, (((0,), (0,)), ((), ())), preferred_element_type=F32)


def _dot_nt(a, b):
    return lax.dot_general(a, b, (((1,), (1,)), ((), ())), preferred_element_type=F32)


def _ln(z, g, b):
    mu = jnp.mean(z, axis=-1, keepdims=True)
    xc = z - mu
    var = jnp.mean(xc * xc, axis=-1, keepdims=True)
    return xc * lax.rsqrt(var + LN_EPS) * g + b


def _rows_view(pack, rows_per_dev, row0):
    return (pack, (N_DEV, rows_per_dev, D_MODEL), (0, row0 // rows_per_dev, 0), (N_DEV * rows_per_dev, D_MODEL))


def _plane_view(arr, j, cols=None, cblk=0):
    _, k, n = arr.shape
    cols = n if cols is None else cols
    return (arr, (None, k, cols), (j, 0, cblk), (k, cols))


def _vspec(view):
    _, bshape, bidx, _ = view
    return pl.BlockSpec(bshape, lambda *_: bidx)


def _vload(view, ref):
    return ref[...].reshape(view[3])


def mm_plain(a, wv, out_dtype, name, nt=False, tm=512):
    S, K = a.shape
    N = wv[3][0] if nt else wv[3][1]
    tm = min(tm, S)
    dot = _dot_nt if nt else _dot

    def body(a_ref, w_ref, o_ref):
        o_ref[...] = dot(a_ref[...], _vload(wv, w_ref)).astype(out_dtype)

    return pl.pallas_call(
        body, grid=(S // tm,),
        in_specs=[pl.BlockSpec((tm, K), lambda i: (i, 0)), _vspec(wv)],
        out_specs=pl.BlockSpec((tm, N), lambda i: (i, 0)),
        out_shape=jax.ShapeDtypeStruct((S, N), out_dtype),
        compiler_params=_cp("parallel"), name=name)(a, wv[0])


def mm_res_nt(a, wv, r, alpha, name, tm=256):
    S, K = a.shape
    N = wv[3][0]
    tm = min(tm, S)

    def body(a_ref, w_ref, r_ref, o_ref):
        o_ref[...] = _dot_nt(a_ref[...], _vload(wv, w_ref)) + alpha * r_ref[...]

    return pl.pallas_call(
        body, grid=(S // tm,),
        in_specs=[pl.BlockSpec((tm, K), lambda i: (i, 0)), _vspec(wv), pl.BlockSpec((tm, N), lambda i: (i, 0))],
        out_specs=pl.BlockSpec((tm, N), lambda i: (i, 0)),
        out_shape=jax.ShapeDtypeStruct((S, N), F32),
        compiler_params=_cp("parallel"), name=name)(a, wv[0], r)


def mm_res2_nt(a3, wv0, wv1, r, alpha, name, tm=256):
    _, S, K = a3.shape
    N = wv0[3][0]
    tm = min(tm, S)

    def body(a_ref, w0_ref, w1_ref, r_ref, o_ref):
        o_ref[...] = (_dot_nt(a_ref[0], _vload(wv0, w0_ref)) + _dot_nt(a_ref[1], _vload(wv1, w1_ref))
                      + alpha * r_ref[...])

    return pl.pallas_call(
        body, grid=(S // tm,),
        in_specs=[pl.BlockSpec((2, tm, K), lambda i: (0, i, 0)), _vspec(wv0), _vspec(wv1),
                  pl.BlockSpec((tm, N), lambda i: (i, 0))],
        out_specs=pl.BlockSpec((tm, N), lambda i: (i, 0)),
        out_shape=jax.ShapeDtypeStruct((S, N), F32),
        compiler_params=_cp("parallel"), name=name)(a3, wv0[0], wv1[0], r)


def mm_ln(a, wv, h, g, b, scale, name, tm=256):
    S, K = a.shape
    tm = min(tm, S)

    def body(a_ref, w_ref, h_ref, g_ref, b_ref, z_ref, y_ref, yb_ref):
        z = ALPHA * h_ref[...] + scale * _dot(a_ref[...], _vload(wv, w_ref))
        y = _ln(z, g_ref[...], b_ref[...])
        z_ref[...] = z
        y_ref[...] = y
        yb_ref[...] = y.astype(BF16)

    row = pl.BlockSpec((tm, D_MODEL), lambda i: (i, 0))
    vec = pl.BlockSpec((1, D_MODEL), lambda i: (0, 0))
    return pl.pallas_call(
        body, grid=(S // tm,),
        in_specs=[pl.BlockSpec((tm, K), lambda i: (i, 0)), _vspec(wv), row, vec, vec],
        out_specs=[row, row, row],
        out_shape=[jax.ShapeDtypeStruct((S, D_MODEL), F32), jax.ShapeDtypeStruct((S, D_MODEL), F32),
                   jax.ShapeDtypeStruct((S, D_MODEL), BF16)],
        compiler_params=_cp("parallel"), name=name)(a, wv[0], h, g.reshape(1, -1), b.reshape(1, -1))


def ffn_up(xb, w_in, fl, name, tm=512, tn=1408):
    S = xb.shape[0]
    tm = min(tm, S)
    nj = D_FF // tn

    def body(x_ref, wg_ref, wu_ref, gu_ref, a_ref):
        x = x_ref[...]
        g = _dot(x, wg_ref[...])
        u = _dot(x, wu_ref[...])
        gu_ref[0] = g.astype(BF16)
        gu_ref[1] = u.astype(BF16)
        a_ref[...] = (g * jax.nn.sigmoid(g) * u).astype(BF16)

    return pl.pallas_call(
        body, grid=(nj, S // tm),
        in_specs=[pl.BlockSpec((tm, D_MODEL), lambda j, i: (i, 0)),
                  pl.BlockSpec((None, D_MODEL, tn), lambda j, i: (fl, 0, j)),
                  pl.BlockSpec((None, D_MODEL, tn), lambda j, i: (fl, 0, nj + j))],
        out_specs=[pl.BlockSpec((2, tm, tn), lambda j, i: (0, i, j)), pl.BlockSpec((tm, tn), lambda j, i: (i, j))],
        out_shape=[jax.ShapeDtypeStruct((2, S, D_FF), BF16), jax.ShapeDtypeStruct((S, D_FF), BF16)],
        compiler_params=_cp("parallel", "parallel"), name=name)(xb, w_in, w_in)


def ffn_mid_bwd(dfb, r_pack, row0, gu, name, tm=512):
    S = dfb.shape[0]
    tm = min(tm, S)
    tn = 4 * FF_SHARD

    def body(df_ref, w_ref, gu_ref, dh_ref):
        da = _dot_nt(df_ref[...], w_ref[...].reshape(tn, D_MODEL))
        g = gu_ref[0].astype(F32)
        u = gu_ref[1].astype(F32)
        sg = jax.nn.sigmoid(g)
        dh_ref[0] = (da * u * (sg * (1.0 + g * (1.0 - sg)))).astype(BF16)
        dh_ref[1] = (da * (g * sg)).astype(BF16)

    gspec = pl.BlockSpec((2, tm, tn), lambda j, i: (0, i, j))
    return pl.pallas_call(
        body, grid=(2, S // tm),
        in_specs=[pl.BlockSpec((tm, D_MODEL), lambda j, i: (i, 0)),
                  pl.BlockSpec((4, FF_SHARD, D_MODEL), lambda j, i: (j, row0 // FF_SHARD, 0)), gspec],
        out_specs=gspec,
        out_shape=jax.ShapeDtypeStruct((2, S, D_FF), BF16),
        compiler_params=_cp("parallel", "parallel"), name=name)(dfb, r_pack, gu)


def mm_tn(a, b, name, tm, tn, ts=1024, planes=1):
    S, M = a.shape
    N = b.shape[-1] * planes
    ts = min(ts, S)
    per = b.shape[-1] // tn

    def body(a_ref, b_ref, o_ref):
        @pl.when(pl.program_id(2) == 0)
        def _():
            o_ref[...] = jnp.zeros_like(o_ref)

        o_ref[...] += _dot_tn(a_ref[...], b_ref[...])

    if planes == 1:
        bspec = pl.BlockSpec((ts, tn), lambda i, j, s: (s, j))
    else:
        bspec = pl.BlockSpec((None, ts, tn), lambda i, j, s: (j // per, s, j % per))
    return pl.pallas_call(
        body, grid=(M // tm, N // tn, S // ts),
        in_specs=[pl.BlockSpec((ts, tm), lambda i, j, s: (s, i)), bspec],
        out_specs=pl.BlockSpec((tm, tn), lambda i, j, s: (i, j)),
        out_shape=jax.ShapeDtypeStruct((M, N), F32),
        compiler_params=_cp("parallel", "parallel", "arbitrary"), name=name)(a, b)


def ln_bwd(dy, z, g, out_scale, name, tm=512):
    S = dy.shape[0]
    tm = min(tm, S)

    def body(dy_ref, z_ref, g_ref, dz_ref, dzb_ref, dg_ref, db_ref):
        @pl.when(pl.program_id(0) == 0)
        def _():
            dg_ref[...] = jnp.zeros_like(dg_ref)
            db_ref[...] = jnp.zeros_like(db_ref)

        z = z_ref[...]
        dy_ = dy_ref[...]
        mu = jnp.mean(z, axis=-1, keepdims=True)
        xc = z - mu
        var = jnp.mean(xc * xc, axis=-1, keepdims=True)
        rstd = lax.rsqrt(var + LN_EPS)
        xh = xc * rstd
        dxh = dy_ * g_ref[...]
        m1 = jnp.mean(dxh, axis=-1, keepdims=True)
        m2 = jnp.mean(dxh * xh, axis=-1, keepdims=True)
        dz = rstd * (dxh - m1 - xh * m2)
        dz_ref[...] = dz
        dzb_ref[...] = (out_scale * dz).astype(BF16)
        dg_ref[...] += jnp.sum(dy_ * xh, axis=0, keepdims=True)
        db_ref[...] += jnp.sum(dy_, axis=0, keepdims=True)

    row = pl.BlockSpec((tm, D_MODEL), lambda i: (i, 0))
    vec = pl.BlockSpec((1, D_MODEL), lambda i: (0, 0))
    return pl.pallas_call(
        body, grid=(S // tm,),
        in_specs=[row, row, vec],
        out_specs=[row, row, vec, vec],
        out_shape=[jax.ShapeDtypeStruct((S, D_MODEL), F32), jax.ShapeDtypeStruct((S, D_MODEL), BF16),
                   jax.ShapeDtypeStruct((1, D_MODEL), F32), jax.ShapeDtypeStruct((1, D_MODEL), F32)],
        compiler_params=_cp("arbitrary"), name=name)(dy, z, g.reshape(1, -1))


def loss_head(y, t, name, tm=512):
    S = y.shape[0]
    tm = min(tm, S)
    nt = S // tm

    def body(y_ref, t_ref, dy_ref, l_ref):
        i = pl.program_id(0)

        @pl.when(i == 0)
        def _():
            l_ref[...] = jnp.zeros_like(l_ref)

        e = y_ref[...] - t_ref[...]
        dy_ref[...] = e * (1.0 / D_MODEL)
        l_ref[...] += jnp.sum(e * e, axis=0, keepdims=True)

        @pl.when(i == nt - 1)
        def _():
            tot = jnp.sum(l_ref[...], axis=1, keepdims=True) * (0.5 / D_MODEL)
            l_ref[...] = jnp.broadcast_to(tot, l_ref.shape)

    row = pl.BlockSpec((tm, D_MODEL), lambda i: (i, 0))
    vec = pl.BlockSpec((1, D_MODEL), lambda i: (0, 0))
    return pl.pallas_call(
        body, grid=(nt,), in_specs=[row, row], out_specs=[row, vec],
        out_shape=[jax.ShapeDtypeStruct((S, D_MODEL), F32), jax.ShapeDtypeStruct((1, D_MODEL), F32)],
        compiler_params=_cp("arbitrary"), name=name)(y, t)


def _rope_tables(S):
    pos = jnp.arange(S, dtype=F32)
    inv_freq = ROPE_THETA ** (-jnp.arange(0, HEAD_DIM, 2, dtype=F32) / HEAD_DIM)
    ang = pos[:, None] * inv_freq[None, :]
    cos, sin = jnp.cos(ang), jnp.sin(ang)
    cosf = jnp.concatenate([cos, cos, cos, cos], axis=1)
    sinf = jnp.concatenate([-sin, sin, -sin, sin], axis=1)
    return cosf, sinf


def _rot(t, c, s, first):
    sw = jnp.where(first, pltpu.roll(t, 96, 1), pltpu.roll(t, 32, 1))
    return t * c + sw * s


def rope_fwd(qkv, cosf, sinf, name, tm=512):
    S = qkv.shape[0]
    tm = min(tm, S)

    def body(x_ref, c_ref, s_ref, q_ref, k_ref, v_ref):
        c = c_ref[...]
        s = s_ref[...]
        first = (lax.broadcasted_iota(jnp.int32, (tm, 128), 1) % HEAD_DIM) < (HEAD_DIM // 2)
        for j in range(8):
            q_ref[:, 128 * j:128 * (j + 1)] = _rot(x_ref[:, 128 * j:128 * (j + 1)], c, s, first).astype(BF16)
        for j in range(2):
            k_ref[:, 128 * j:128 * (j + 1)] = _rot(x_ref[:, 1024 + 128 * j:1024 + 128 * (j + 1)], c, s, first).astype(BF16)
        v_ref[...] = x_ref[:, 1280:1536].astype(BF16)

    tab = pl.BlockSpec((tm, 128), lambda i: (i, 0))
    return pl.pallas_call(
        body, grid=(S // tm,),
        in_specs=[pl.BlockSpec((tm, QKV), lambda i: (i, 0)), tab, tab],
        out_specs=[pl.BlockSpec((tm, 1024), lambda i: (i, 0)), pl.BlockSpec((tm, 256), lambda i: (i, 0)),
                   pl.BlockSpec((tm, 256), lambda i: (i, 0))],
        out_shape=[jax.ShapeDtypeStruct((S, 1024), BF16), jax.ShapeDtypeStruct((S, 256), BF16),
                   jax.ShapeDtypeStruct((S, 256), BF16)],
        compiler_params=_cp("parallel"), name=name)(qkv, cosf, sinf)


def rope_bwd(dq, dkc, dkp, dvc, dvp, cosf, sinf, name, tq):
    S = dq.shape[0]
    nt = S // tq

    def body(dq_ref, dkc_ref, dkp_ref, dvc_ref, dvp_ref, c_ref, s_ref, o_ref):
        i = pl.program_id(0)
        c = c_ref[...]
        s = -s_ref[...]
        first = (lax.broadcasted_iota(jnp.int32, (tq, 128), 1) % HEAD_DIM) < (HEAD_DIM // 2)
        for j in range(8):
            o_ref[:, 128 * j:128 * (j + 1)] = _rot(dq_ref[:, 128 * j:128 * (j + 1)], c, s, first).astype(BF16)
        has_next = i < nt - 1
        rows = lax.broadcasted_iota(jnp.int32, (tq, 256), 0)
        pad = jnp.zeros((tq - BLOCK, 256), F32)
        halo_k = jnp.concatenate([pad, dkp_ref[...]], axis=0)
        halo_v = jnp.concatenate([pad, dvp_ref[...]], axis=0)
        use = jnp.logical_and(has_next, rows >= tq - BLOCK)
        dk = dkc_ref[...] + jnp.where(use, halo_k, 0.0)
        dv = dvc_ref[...] + jnp.where(use, halo_v, 0.0)
        for j in range(2):
            o_ref[:, 1024 + 128 * j:1024 + 128 * (j + 1)] = _rot(dk[:, 128 * j:128 * (j + 1)], c, s, first).astype(BF16)
        o_ref[:, 1280:1536] = dv.astype(BF16)

    tab = pl.BlockSpec((tq, 128), lambda i: (i, 0))
    cur = pl.BlockSpec((tq, 256), lambda i: (i, 0))
    nxt = pl.BlockSpec((BLOCK, 256), lambda i: (jnp.minimum(i + 1, nt - 1), 0))
    return pl.pallas_call(
        body, grid=(nt,),
        in_specs=[pl.BlockSpec((tq, 1024), lambda i: (i, 0)), cur, nxt, cur, nxt, tab, tab],
        out_specs=pl.BlockSpec((tq, QKV), lambda i: (i, 0)),
        out_shape=jax.ShapeDtypeStruct((S, QKV), BF16),
        compiler_params=_cp("parallel"), name=name)(dq, dkc, dkp, dvc, dvp, cosf, sinf)


def _attn_probs(qs, kw, sink_col, first_block):
    s = _dot_nt(qs, kw) * (HEAD_DIM ** -0.5)
    r = lax.broadcasted_iota(jnp.int32, s.shape, 0) % BLOCK
    c = lax.broadcasted_iota(jnp.int32, s.shape, 1)
    ok = jnp.logical_and(c > r, c <= r + BLOCK)
    ok = jnp.logical_and(ok, jnp.logical_or(c >= BLOCK, jnp.logical_not(first_block)))
    s = jnp.where(ok, s, NEG)
    m = jnp.maximum(jnp.max(s, axis=1, keepdims=True), sink_col)
    p = jnp.exp(s - m)
    es = jnp.exp(sink_col - m)
    den = jnp.sum(p, axis=1, keepdims=True) + es
    return p / den, es / den


def _sink_col(sink_ref, g):
    rid = lax.broadcasted_iota(jnp.int32, (GROUP * BLOCK, 1), 0) // BLOCK
    col = jnp.zeros((GROUP * BLOCK, 1), F32)
    for j in range(GROUP):
        col = jnp.where(rid == j, sink_ref[GROUP * g + j], col)
    return col


def _kv_window(kc_ref, kp_ref, b):
    if b == 0:
        return jnp.concatenate([kp_ref[...], kc_ref[0:BLOCK, :]], axis=0)
    return kc_ref[BLOCK * (b - 1):BLOCK * (b + 1), :]


def _attn_specs(tq):
    nsub = tq // BLOCK
    qspec = pl.BlockSpec((GROUP, tq, HEAD_DIM), lambda g, n: (g, n, 0))
    cur = pl.BlockSpec((None, tq, HEAD_DIM), lambda g, n: (g, n, 0))
    prev = pl.BlockSpec((None, BLOCK, HEAD_DIM), lambda g, n: (g, jnp.maximum(n * nsub - 1, 0), 0))
    return qspec, cur, prev


def attn_fwd(q, k, v, sinks, name, tq):
    S = q.shape[1]
    nsub = tq // BLOCK

    def body(sink_ref, q_ref, kc_ref, kp_ref, vc_ref, vp_ref, o_ref):
        g = pl.program_id(0)
        n = pl.program_id(1)
        sink_col = _sink_col(sink_ref, g)
        for b in range(nsub):
            lo = BLOCK * b
            qs = jnp.concatenate([q_ref[j, lo:lo + BLOCK, :] for j in range(GROUP)], axis=0)
            pn, _ = _attn_probs(qs, _kv_window(kc_ref, kp_ref, b), sink_col, jnp.logical_and(n == 0, b == 0))
            o = _dot(pn.astype(BF16), _kv_window(vc_ref, vp_ref, b))
            for j in range(GROUP):
                o_ref[j, lo:lo + BLOCK, :] = o[BLOCK * j:BLOCK * (j + 1)].astype(BF16)

    qspec, cur, prev = _attn_specs(tq)
    return pl.pallas_call(
        body, grid=(N_KV, S // tq),
        in_specs=[pl.BlockSpec(memory_space=pltpu.SMEM), qspec, cur, prev, cur, prev],
        out_specs=qspec,
        out_shape=jax.ShapeDtypeStruct((N_HEADS, S, HEAD_DIM), BF16),
        compiler_params=_cp("parallel", "parallel"), name=name)(sinks, q, k, k, v, v)


def attn_bwd(q, k, v, do, sinks, name, tq):
    S = q.shape[1]
    nsub = tq // BLOCK
    nt = S // tq

    def body(sink_ref, q_ref, kc_ref, kp_ref, vc_ref, vp_ref, do_ref, dq_ref, dkc_ref, dkp_ref, dvc_ref, dvp_ref, ds_ref):
        g = pl.program_id(0)
        n = pl.program_id(1)

        @pl.when(n == 0)
        def _():
            ds_ref[...] = jnp.zeros_like(ds_ref)

        dkc_ref[...] = jnp.zeros_like(dkc_ref)
        dvc_ref[...] = jnp.zeros_like(dvc_ref)
        sink_col = _sink_col(sink_ref, g)
        rid = lax.broadcasted_iota(jnp.int32, (GROUP * BLOCK, 1), 0) // BLOCK
        sub = lax.broadcasted_iota(jnp.int32, (8, 128), 0)
        dsink = jnp.zeros((8, 128), F32)
        for b in range(nsub):
            lo = BLOCK * b
            qs = jnp.concatenate([q_ref[j, lo:lo + BLOCK, :] for j in range(GROUP)], axis=0)
            dos = jnp.concatenate([do_ref[j, lo:lo + BLOCK, :] for j in range(GROUP)], axis=0)
            kw = _kv_window(kc_ref, kp_ref, b)
            vw = _kv_window(vc_ref, vp_ref, b)
            pn, ps = _attn_probs(qs, kw, sink_col, jnp.logical_and(n == 0, b == 0))
            dp = _dot_nt(dos, vw)
            delta = jnp.sum(pn * dp, axis=1, keepdims=True)
            dsb = (pn * (dp - delta) * (HEAD_DIM ** -0.5)).astype(BF16)
            dq = _dot(dsb, kw)
            for j in range(GROUP):
                dq_ref[j, lo:lo + BLOCK, :] = dq[BLOCK * j:BLOCK * (j + 1)]
            dkw = _dot_tn(dsb, qs)
            dvw = _dot_tn(pn.astype(BF16), dos)
            if b == 0:
                dkp_ref[...] = dkw[0:BLOCK]
                dvp_ref[...] = dvw[0:BLOCK]
            else:
                dkc_ref[lo - BLOCK:lo, :] += dkw[0:BLOCK]
                dvc_ref[lo - BLOCK:lo, :] += dvw[0:BLOCK]
            dkc_ref[lo:lo + BLOCK, :] += dkw[BLOCK:2 * BLOCK]
            dvc_ref[lo:lo + BLOCK, :] += dvw[BLOCK:2 * BLOCK]
            sd = ps * delta
            for j in range(GROUP):
                tot = -jnp.sum(jnp.where(rid == j, sd, 0.0))
                dsink = dsink + jnp.where(sub == j, tot, 0.0)
        ds_ref[...] += dsink

    qspec, cur, prev = _attn_specs(tq)
    halo = pl.BlockSpec((None, BLOCK, HEAD_DIM), lambda g, n: (g, n, 0))
    return pl.pallas_call(
        body, grid=(N_KV, nt),
        in_specs=[pl.BlockSpec(memory_space=pltpu.SMEM), qspec, cur, prev, cur, prev, qspec],
        out_specs=[qspec, cur, halo, cur, halo, pl.BlockSpec((None, 8, 128), lambda g, n: (g, 0, 0))],
        out_shape=[jax.ShapeDtypeStruct((N_HEADS, S, HEAD_DIM), F32),
                   jax.ShapeDtypeStruct((N_KV, S, HEAD_DIM), F32), jax.ShapeDtypeStruct((N_KV, nt * BLOCK, HEAD_DIM), F32),
                   jax.ShapeDtypeStruct((N_KV, S, HEAD_DIM), F32), jax.ShapeDtypeStruct((N_KV, nt * BLOCK, HEAD_DIM), F32),
                   jax.ShapeDtypeStruct((N_KV, 8, 128), F32)],
        compiler_params=_cp("parallel", "arbitrary"), name=name)(sinks, q, k, k, v, v, do)


def _rows_before(cur, prev8, k):
    if k == 0:
        return cur
    n = cur.shape[0]
    ext = jnp.concatenate([prev8, cur], axis=0)
    return ext[8 - k:8 - k + n]


def _rows_after(cur, next8, k):
    if k == 0:
        return cur
    n = cur.shape[0]
    ext = jnp.concatenate([cur, next8], axis=0)
    return ext[k:k + n]


def _gelu(x):
    c = math.sqrt(2.0 / math.pi)
    t = jnp.tanh(c * (x + 0.044715 * (x * x * x)))
    return 0.5 * (1.0 + t), t


def _neg_expm1(u):
    ser = 1.0 + u * (1.0 / 9.0)
    for k in range(8, 1, -1):
        ser = 1.0 + (u * (1.0 / k)) * ser
    return jnp.where(u > -0.5, -(u * ser), 1.0 - jnp.exp(u))


def _block_diag(xb16, w_ref):
    return jnp.concatenate([_dot(xb16[:, RNN_W * n:RNN_W * (n + 1)], w_ref[n]) for n in range(RNN_BLOCKS)], axis=1)


def _lru_gates(xb, prev8, cw_ref, cb_ref, wra_ref, wrx_ref, bra_ref, brx_ref, lsl_ref):
    xc = cb_ref[...] + cw_ref[3:4, :] * xb
    for w in range(CONV_W - 1):
        xc = xc + cw_ref[w:w + 1, :] * _rows_before(xb, prev8, CONV_W - 1 - w)
    xcb = xc.astype(BF16)
    r = jax.nn.sigmoid(_block_diag(xcb, wra_ref) + bra_ref[...])
    ig = jax.nn.sigmoid(_block_diag(xcb, wrx_ref) + brx_ref[...])
    la = LRU_C * r * lsl_ref[...]
    a = jnp.exp(la)
    sq = jnp.sqrt(_neg_expm1(2.0 * la))
    return xc, xcb, r, ig, a, sq


def lru_fwd(xg, p, name, tm=256):
    S = xg.shape[0]
    tm = min(tm, S)

    def body(xg_ref, xp_ref, cw_ref, cb_ref, wra_ref, wrx_ref, bra_ref, brx_ref, lsl_ref, y_ref, h_ref, hc_ref, a_s, b_s):
        i = pl.program_id(0)
        xb = xg_ref[:, 0:D_RNN]
        gb = xg_ref[:, D_RNN:2 * D_RNN]
        prev8 = jnp.where(i > 0, xp_ref[:, 0:D_RNN], 0.0)
        xc, _, r, ig, a, sq = _lru_gates(xb, prev8, cw_ref, cb_ref, wra_ref, wrx_ref, bra_ref, brx_ref, lsl_ref)
        a_s[...] = a
        b_s[...] = sq * (ig * xc)

        @pl.when(i == 0)
        def _():
            hc_ref[...] = jnp.zeros_like(hc_ref)

        def chunk(c, h):
            o = pl.multiple_of(c * 8, 8)
            av = a_s[pl.ds(o, 8), :]
            bv = b_s[pl.ds(o, 8), :]
            rows = []
            for t in range(8):
                h = av[t:t + 1, :] * h + bv[t:t + 1, :]
                rows.append(h)
            h_ref[pl.ds(o, 8), :] = jnp.concatenate(rows, axis=0)
            return h

        h_last = lax.fori_loop(0, tm // 8, chunk, hc_ref[0:1, :])
        hc_ref[0:1, :] = h_last
        cdf, _ = _gelu(gb)
        y_ref[...] = (h_ref[...] * (gb * cdf)).astype(BF16)

    vec = pl.BlockSpec((1, D_RNN), lambda i: (0, 0))
    wsp = pl.BlockSpec((RNN_BLOCKS, RNN_W, RNN_W), lambda i: (0, 0, 0))
    return pl.pallas_call(
        body, grid=(S // tm,),
        in_specs=[pl.BlockSpec((tm, 2 * D_RNN), lambda i: (i, 0)),
                  pl.BlockSpec((8, 2 * D_RNN), lambda i: (jnp.maximum(i * (tm // 8) - 1, 0), 0)),
                  pl.BlockSpec((CONV_W, D_RNN), lambda i: (0, 0)), vec, wsp, wsp, vec, vec, vec],
        out_specs=[pl.BlockSpec((tm, D_RNN), lambda i: (i, 0)), pl.BlockSpec((tm, D_RNN), lambda i: (i, 0))],
        out_shape=[jax.ShapeDtypeStruct((S, D_RNN), BF16), jax.ShapeDtypeStruct((S, D_RNN), F32)],
        scratch_shapes=[pltpu.VMEM((8, D_RNN), F32), pltpu.VMEM((tm, D_RNN), F32), pltpu.VMEM((tm, D_RNN), F32)],
        compiler_params=_cp("arbitrary"), name=name)(
            xg, xg, p["conv_w"], p["conv_b"], p["w_ra"], p["w_rx"], p["b_ra"], p["b_rx"], p["lsl"])


def lru_bwd(dy, xg, h, p, name, tm=256):
    S = xg.shape[0]
    tm = min(tm, S)
    nt = S // tm

    def body(dy_ref, xg_ref, xp_ref, h_ref, hp_ref, cw_ref, cb_ref, wra_ref, wrx_ref, bra_ref, brx_ref,
             lsl_ref, dxg_ref, dcw_ref, dcb_ref, dwra_ref, dwrx_ref, dbra_ref, dbrx_ref, dlam_ref,
             lc_ref, nx_ref, a_s, g_s, l_s):
        i = pl.program_id(0)
        ti = nt - 1 - i

        @pl.when(i == 0)
        def _():
            lc_ref[...] = jnp.zeros_like(lc_ref)
            nx_ref[...] = jnp.zeros_like(nx_ref)
            for ref in (dcw_ref, dcb_ref, dwra_ref, dwrx_ref, dbra_ref, dbrx_ref, dlam_ref):
                ref[...] = jnp.zeros_like(ref)

        xb = xg_ref[:, 0:D_RNN]
        gb = xg_ref[:, D_RNN:2 * D_RNN]
        prev8 = jnp.where(ti > 0, xp_ref[:, 0:D_RNN], 0.0)
        xc, xcb, r, ig, a, sq = _lru_gates(xb, prev8, cw_ref, cb_ref, wra_ref, wrx_ref, bra_ref, brx_ref, lsl_ref)
        hh = h_ref[...]
        hprev = _rows_before(hh, jnp.where(ti > 0, hp_ref[...], 0.0), 1)
        dy_ = dy_ref[...]
        cdf, th = _gelu(gb)
        c0 = math.sqrt(2.0 / math.pi)
        dgate = cdf + gb * (0.5 * (1.0 - th * th) * c0 * (1.0 + 3.0 * 0.044715 * gb * gb))
        dgb = dy_ * hh * dgate
        a_s[...] = a
        g_s[...] = dy_ * (gb * cdf)

        def chunk(cc, carry):
            o = pl.multiple_of((tm // 8 - 1 - cc) * 8, 8)
            av = a_s[pl.ds(o, 8), :]
            gv = g_s[pl.ds(o, 8), :]
            rows = [None] * 8
            for t in range(7, -1, -1):
                lam_t = gv[t:t + 1, :] + carry
                rows[t] = lam_t
                carry = av[t:t + 1, :] * lam_t
            l_s[pl.ds(o, 8), :] = jnp.concatenate(rows, axis=0)
            return carry

        carry = lax.fori_loop(0, tm // 8, chunk, lc_ref[0:1, :])
        lc_ref[0:1, :] = carry
        lam = l_s[...]
        da = lam * hprev
        dixc = lam * sq
        di = dixc * xc
        dxc = dixc * ig
        dsq = lam * (ig * xc)
        dla = da * a - dsq * (a * a / sq)
        dr = dla * (LRU_C * lsl_ref[...])
        dlam_ref[...] += jnp.sum(dla * (LRU_C * r), axis=0, keepdims=True)
        dpr = dr * r * (1.0 - r)
        dpi = di * ig * (1.0 - ig)
        dbra_ref[...] += jnp.sum(dpr, axis=0, keepdims=True)
        dbrx_ref[...] += jnp.sum(dpi, axis=0, keepdims=True)
        dprb = dpr.astype(BF16)
        dpib = dpi.astype(BF16)
        back = []
        for n in range(RNN_BLOCKS):
            sl = slice(RNN_W * n, RNN_W * (n + 1))
            dwra_ref[n] += _dot_tn(xcb[:, sl], dprb[:, sl])
            dwrx_ref[n] += _dot_tn(xcb[:, sl], dpib[:, sl])
            back.append(_dot_nt(dprb[:, sl], wra_ref[n]) + _dot_nt(dpib[:, sl], wrx_ref[n]))
        dxc = dxc + jnp.concatenate(back, axis=1)
        dcb_ref[...] += jnp.sum(dxc, axis=0, keepdims=True)
        next8 = nx_ref[...]
        dxb = cw_ref[3:4, :] * dxc
        dcw_ref[3:4, :] += jnp.sum(dxc * xb, axis=0, keepdims=True)
        for w in range(CONV_W - 1):
            k = CONV_W - 1 - w
            dcw_ref[w:w + 1, :] += jnp.sum(dxc * _rows_before(xb, prev8, k), axis=0, keepdims=True)
            dxb = dxb + cw_ref[w:w + 1, :] * _rows_after(dxc, next8, k)
        nx_ref[...] = dxc[0:8, :]
        dxg_ref[:, 0:D_RNN] = dxb.astype(BF16)
        dxg_ref[:, D_RNN:2 * D_RNN] = dgb.astype(BF16)

    rev = lambda i: (nt - 1 - i, 0)
    before = lambda i: (jnp.maximum((nt - 1 - i) * (tm // 8) - 1, 0), 0)
    vec = pl.BlockSpec((1, D_RNN), lambda i: (0, 0))
    wsp = pl.BlockSpec((RNN_BLOCKS, RNN_W, RNN_W), lambda i: (0, 0, 0))
    cwsp = pl.BlockSpec((CONV_W, D_RNN), lambda i: (0, 0))
    return pl.pallas_call(
        body, grid=(nt,),
        in_specs=[pl.BlockSpec((tm, D_RNN), rev), pl.BlockSpec((tm, 2 * D_RNN), rev), pl.BlockSpec((8, 2 * D_RNN), before),
                  pl.BlockSpec((tm, D_RNN), rev), pl.BlockSpec((8, D_RNN), before),
                  cwsp, vec, wsp, wsp, vec, vec, vec],
        out_specs=[pl.BlockSpec((tm, 2 * D_RNN), rev), cwsp, vec, wsp, wsp, vec, vec, vec],
        out_shape=[jax.ShapeDtypeStruct((S, 2 * D_RNN), BF16), jax.ShapeDtypeStruct((CONV_W, D_RNN), F32),
                   jax.ShapeDtypeStruct((1, D_RNN), F32), jax.ShapeDtypeStruct((RNN_BLOCKS, RNN_W, RNN_W), F32),
                   jax.ShapeDtypeStruct((RNN_BLOCKS, RNN_W, RNN_W), F32), jax.ShapeDtypeStruct((1, D_RNN), F32),
                   jax.ShapeDtypeStruct((1, D_RNN), F32), jax.ShapeDtypeStruct((1, D_RNN), F32)],
        scratch_shapes=[pltpu.VMEM((8, D_RNN), F32), pltpu.VMEM((8, D_RNN), F32), pltpu.VMEM((tm, D_RNN), F32),
                        pltpu.VMEM((tm, D_RNN), F32), pltpu.VMEM((tm, D_RNN), F32)],
        compiler_params=_cp("arbitrary"), name=name)(
            dy, xg, xg, h, h, p["conv_w"], p["conv_b"], p["w_ra"], p["w_rx"], p["b_ra"], p["b_rx"], p["lsl"])


def _place():
    x, y, c = lax.axis_index("x"), lax.axis_index("y"), lax.axis_index("c")
    chips = [(1 - x, y), (x, 1 - y), (1 - x, 1 - y)]
    return x, y, c, chips


def all_gather_packs(packs):
    n = len(packs)

    def body(*refs):
        ins, outs = refs[:n], refs[n:2 * n]
        send_sems, recv_sems, local_sems = refs[2 * n:]
        x, y, c, chips = _place()
        me, sibling = (x, y, c), (x, y, 1 - c)

        def rows(a, px, py, pc):
            return outs[a].at[4 * px + 2 * py + pc]

        def copy(a, k, block, to, own=False):
            return pltpu.make_async_remote_copy(
                src_ref=ins[a] if own else rows(a, *block), dst_ref=rows(a, *block),
                send_sem=send_sems.at[k, a], recv_sem=recv_sems.at[k, a], device_id=to, device_id_type=MESH)

        mine = [pltpu.make_async_copy(ins[a], rows(a, *me), local_sems.at[a]) for a in range(n)]
        for cp in mine:
            cp.start()
        first = [copy(a, 1 + j, me, (*chip, c), own=True) for j, chip in enumerate(chips) for a in range(n)]
        first += [copy(a, 0, me, sibling, own=True) for a in range(n)]
        for cp in first:
            cp.start()
        passed = []
        for j, chip in enumerate(chips):
            for a in range(n):
                copy(a, 1 + j, (*chip, c), me).wait_recv()
                fwd = copy(a, 4 + j, (*chip, c), sibling)
                fwd.start()
                passed.append(fwd)
        for a in range(n):
            copy(a, 0, sibling, me).wait_recv()
        for j, chip in enumerate(chips):
            for a in range(n):
                copy(a, 4 + j, (*chip, 1 - c), me).wait_recv()
        for cp in first + passed:
            cp.wait_send()
        for cp in mine:
            cp.wait()

    return pl.pallas_call(
        body, out_shape=[jax.ShapeDtypeStruct((N_DEV,) + p.shape, p.dtype) for p in packs],
        in_specs=[ANY] * n, out_specs=[ANY] * n,
        scratch_shapes=[pltpu.SemaphoreType.DMA((7, n)), pltpu.SemaphoreType.DMA((7, n)), pltpu.SemaphoreType.DMA((n,))],
        name="all_gather_packs")(*packs)


def pair_exchange(sends):
    n = len(sends)

    def body(*refs):
        ins, outs = refs[:n], refs[n:2 * n]
        send_sems, recv_sems = refs[2 * n:]
        x, y, c, _ = _place()
        cps = [pltpu.make_async_remote_copy(
            src_ref=ins[a].at[k, 1 - c], dst_ref=outs[a].at[k], send_sem=send_sems.at[k, a], recv_sem=recv_sems.at[k, a],
            device_id=(x, y, 1 - c), device_id_type=MESH) for k in range(4) for a in range(n)]
        for cp in cps:
            cp.start()
        for cp in cps:
            cp.wait()

    return pl.pallas_call(
        body, out_shape=[jax.ShapeDtypeStruct((4,) + s.shape[2:], s.dtype) for s in sends],
        in_specs=[ANY] * n, out_specs=[ANY] * n,
        scratch_shapes=[pltpu.SemaphoreType.DMA((4, n)), pltpu.SemaphoreType.DMA((4, n))],
        name="pair_exchange")(*sends)


def pair_sum(send, got, out_dtype, tr, name):
    _, _, R, C = send.shape
    c = lax.axis_index("c").astype(jnp.int32).reshape(1)

    def body(c_ref, s_ref, g_ref, o_ref):
        o_ref[...] = (s_ref[...] + g_ref[...]).astype(out_dtype)

    return pl.pallas_call(
        body,
        grid_spec=pltpu.PrefetchScalarGridSpec(
            num_scalar_prefetch=1, grid=(4, R // tr),
            in_specs=[pl.BlockSpec((None, None, tr, C), lambda k, i, cr: (k, cr[0], i, 0)),
                      pl.BlockSpec((None, tr, C), lambda k, i, cr: (k, i, 0))],
            out_specs=pl.BlockSpec((None, tr, C), lambda k, i, cr: (k, i, 0))),
        out_shape=jax.ShapeDtypeStruct((4, R, C), out_dtype),
        compiler_params=_cp("parallel", "parallel"), name=name)(c, send, got)


def chip_exchange(parts):
    n = len(parts)

    def body(*refs):
        ins, outs = refs[:n], refs[n:2 * n]
        send_sems, recv_sems, local_sems = refs[2 * n:]
        x, y, c, chips = _place()
        mychip = 2 * x + y
        mine = [pltpu.make_async_copy(ins[a].at[mychip], outs[a].at[mychip], local_sems.at[a]) for a in range(n)]
        for cp in mine:
            cp.start()
        cps = [pltpu.make_async_remote_copy(
            src_ref=ins[a].at[2 * px + py], dst_ref=outs[a].at[mychip], send_sem=send_sems.at[j, a],
            recv_sem=recv_sems.at[j, a], device_id=(px, py, c), device_id_type=MESH)
            for j, (px, py) in enumerate(chips) for a in range(n)]
        for cp in cps:
            cp.start()
        for cp in cps:
            cp.wait()
        for cp in mine:
            cp.wait()

    return pl.pallas_call(
        body, out_shape=[jax.ShapeDtypeStruct(p.shape, p.dtype) for p in parts],
        in_specs=[ANY] * n, out_specs=[ANY] * n,
        scratch_shapes=[pltpu.SemaphoreType.DMA((3, n)), pltpu.SemaphoreType.DMA((3, n)), pltpu.SemaphoreType.DMA((n,))],
        name="chip_exchange")(*parts)


def adamw(parts, row0, w, m, v, tr, name):
    C = w.shape[-1]
    rows = w.size // C
    off = row0 // tr

    def body(p_ref, w_ref, m_ref, v_ref, g_ref, d_ref, nm_ref, nv_ref):
        g = ((p_ref[0].astype(F32) + p_ref[1].astype(F32)) + p_ref[2].astype(F32)) + p_ref[3].astype(F32)
        m2 = ADAM_B1 * m_ref[...] + (1.0 - ADAM_B1) * g
        v2 = ADAM_B2 * v_ref[...] + (1.0 - ADAM_B2) * (g * g)
        mh = m2 / (1.0 - ADAM_B1 ** ADAM_STEP)
        vh = v2 / (1.0 - ADAM_B2 ** ADAM_STEP)
        g_ref[...] = g
        d_ref[...] = -ADAM_LR * (mh / (jnp.sqrt(vh) + ADAM_EPS) + ADAM_WD * w_ref[...])
        nm_ref[...] = m2
        nv_ref[...] = v2

    row = pl.BlockSpec((tr, C), lambda i: (i, 0))
    shp = jax.ShapeDtypeStruct((rows, C), F32)
    outs = pl.pallas_call(
        body, grid=(rows // tr,),
        in_specs=[pl.BlockSpec((4, tr, C), lambda i: (0, off + i, 0)), row, row, row],
        out_specs=[row, row, row, row], out_shape=[shp, shp, shp, shp],
        compiler_params=_cp("parallel"), name=name)(parts, w.reshape(rows, C), m.reshape(rows, C), v.reshape(rows, C))
    return [o.reshape(w.shape) for o in outs]


def _full_from_gathered(flat, shape, axis):
    t = jnp.moveaxis(flat.reshape((N_DEV,) + shape), 0, axis)
    return t.reshape(shape[:axis] + (N_DEV * shape[axis],) + shape[axis + 1:])


def _shards_of_full(full, shape, axis):
    t = full.reshape(shape[:axis] + (N_DEV, shape[axis]) + shape[axis + 1:])
    return jnp.moveaxis(t, axis, 0).reshape(N_DEV, -1)


def _small_pack(args, prefix):
    flat = jnp.concatenate([args[prefix + n].reshape(-1) for n, _, _ in SMALL] + [args[prefix + "attn_sinks"].reshape(-1)])
    return jnp.pad(flat, (0, SM_ROWS * 1024 - flat.shape[0])).reshape(SM_ROWS, 1024)


def _small_unpack(pack):
    flat = pack.reshape(-1)
    out, off = {}, 0
    for n, shape, _ in SMALL:
        size = math.prod(shape)
        out[n] = flat[off:off + size].reshape(shape)
        off += size
    out["attn_sinks"] = flat[SINK_OFF:SINK_OFF + 32].reshape(2, 16)
    return out


def _tn_tile(n):
    return next(t for t in (1408, 1024, 768, 512, 256, 128) if n % t == 0)


def _dw(a, b, name, planes=1):
    return mm_tn(a, b, name, _tn_tile(a.shape[1]), _tn_tile(b.shape[-1]), planes=planes)


def _local_step(x, target, W, P, sinks):
    S = x.shape[0]
    tq = min(512, S)
    cosf, sinf = _rope_tables(S)
    rp = W["R"]
    saved = []
    h, hb = x, x.astype(BF16)
    for l in range(DEPTH):
        j = l // 2
        sv = {"h0b": hb}
        sv["gu1"], a1 = ffn_up(hb, W["w_in"], l, f"ffn1_up_{l}")
        sv["a1"] = a1
        sv["z1"], h, hb = mm_ln(a1, _rows_view(rp, FF_SHARD, l * FF_SHARD), h, P["ln_g"][l, 0], P["ln_b"][l, 0], 0.5,
                                f"ffn1_down_ln_{l}")
        sv["h1b"] = hb
        if l % 2 == 0:
            qkv = mm_plain(hb, _plane_view(W["w_qkv"], j), F32, f"attn_qkv_{l}")
            qr, kr, vv = rope_fwd(qkv, cosf, sinf, f"rope_{l}")
            qh = qr.reshape(S, N_HEADS, HEAD_DIM).transpose(1, 0, 2)
            kh = kr.reshape(S, N_KV, HEAD_DIM).transpose(1, 0, 2)
            vh = vv.reshape(S, N_KV, HEAD_DIM).transpose(1, 0, 2)
            oh = attn_fwd(qh, kh, vh, sinks[j], f"attn_core_{l}", tq)
            mix_in = oh.transpose(1, 0, 2).reshape(S, N_HEADS * HEAD_DIM)
            sv.update(qh=qh, kh=kh, vh=vh)
            w_mix = _rows_view(rp, MIX_SHARD, R_ATTN + j * MIX_SHARD)
        else:
            xg = mm_plain(hb, _plane_view(W["lru_w_in"], j), F32, f"lru_in_{l}")
            mix_in, hstate = lru_fwd(xg, P["lru"][j], f"lru_core_{l}")
            sv.update(xg=xg, hstate=hstate)
            w_mix = _rows_view(rp, MIX_SHARD, R_LRU + j * MIX_SHARD)
        sv["mix_in"], sv["w_mix"] = mix_in, w_mix
        sv["z2"], h, hb = mm_ln(mix_in, w_mix, h, P["ln_g"][l, 1], P["ln_b"][l, 1], 1.0, f"mix_out_ln_{l}")
        sv["h2b"] = hb
        sv["gu2"], a2 = ffn_up(hb, W["w_in"], DEPTH + l, f"ffn2_up_{l}")
        sv["a2"] = a2
        sv["z3"], h, hb = mm_ln(a2, _rows_view(rp, FF_SHARD, R_FFN2 + l * FF_SHARD), h, P["ln_g"][l, 2], P["ln_b"][l, 2],
                                0.5, f"ffn2_down_ln_{l}")
        saved.append(sv)

    dy, lvec = loss_head(h, target, "loss_head")
    loss = lvec[0, 0]

    def ffn_bwd(dy, z, g, gu, a, xin_b, row0, fl, tag):
        dz, dzb, dg, db = ln_bwd(dy, z, g, 0.5, f"ln_bwd_{tag}")
        dh = ffn_mid_bwd(dzb, rp, row0, gu, f"ffn_mid_bwd_{tag}")
        d_wout = _dw(a, dzb, f"dw_out_{tag}")
        d_win = _dw(xin_b, dh, f"dw_in_{tag}", planes=2)
        dx = mm_res2_nt(dh, _plane_view(W["w_in"], fl, D_FF, 0), _plane_view(W["w_in"], fl, D_FF, 1), dz, ALPHA,
                        f"ffn_dx_{tag}")
        return dx, dg, db, d_wout, d_win

    g_ln_g = [[None] * 3 for _ in range(DEPTH)]
    g_ln_b = [[None] * 3 for _ in range(DEPTH)]
    lists = {k: [None] * DEPTH for k in ("ffn1_w_in", "ffn1_w_out", "ffn2_w_in", "ffn2_w_out")}
    mixg = {k: [None] * 2 for k in ("attn_w_qkv", "attn_w_o", "attn_sinks", "lru_w_in", "lru_conv_w", "lru_conv_b",
                                     "lru_w_ra", "lru_b_ra", "lru_w_rx", "lru_b_rx", "lru_lambda", "lru_w_out")}
    for l in reversed(range(DEPTH)):
        j = l // 2
        sv = saved[l]
        dy, g_ln_g[l][2], g_ln_b[l][2], lists["ffn2_w_out"][l], lists["ffn2_w_in"][l] = ffn_bwd(
            dy, sv["z3"], P["ln_g"][l, 2], sv["gu2"], sv["a2"], sv["h2b"], R_FFN2 + l * FF_SHARD, DEPTH + l, f"2_{l}")
        dz, dzb, g_ln_g[l][1], g_ln_b[l][1] = ln_bwd(dy, sv["z2"], P["ln_g"][l, 1], 1.0, f"ln_bwd_mix_{l}")
        if l % 2 == 0:
            mixg["attn_w_o"][j] = _dw(sv["mix_in"], dzb, f"dw_o_{l}")
            do = mm_plain(dzb, sv["w_mix"], BF16, f"attn_do_{l}", nt=True)
            doh = do.reshape(S, N_HEADS, HEAD_DIM).transpose(1, 0, 2)
            dq, dkc, dkp, dvc, dvp, dsk = attn_bwd(sv["qh"], sv["kh"], sv["vh"], doh, sinks[j], f"attn_core_bwd_{l}", tq)
            mixg["attn_sinks"][j] = dsk[:, :GROUP, 0].reshape(N_HEADS)
            tok = lambda t: t.transpose(1, 0, 2).reshape(t.shape[1], -1)
            dmid = rope_bwd(tok(dq), tok(dkc), tok(dkp), tok(dvc), tok(dvp), cosf, sinf, f"rope_bwd_{l}", tq)
            mixg["attn_w_qkv"][j] = _dw(sv["h1b"], dmid, f"dw_qkv_{l}")
            w_in_v = _plane_view(W["w_qkv"], j)
        else:
            mixg["lru_w_out"][j] = _dw(sv["mix_in"], dzb, f"dw_lru_out_{l}")
            dyl = mm_plain(dzb, sv["w_mix"], F32, f"lru_dy_{l}", nt=True)
            dmid, dcw, dcb, dwra, dwrx, dbra, dbrx, dlam = lru_bwd(dyl, sv["xg"], sv["hstate"], P["lru"][j], f"lru_core_bwd_{l}")
            mixg["lru_conv_w"][j], mixg["lru_conv_b"][j] = dcw, dcb[0]
            mixg["lru_w_ra"][j], mixg["lru_w_rx"][j] = dwra, dwrx
            mixg["lru_b_ra"][j], mixg["lru_b_rx"][j] = dbra[0], dbrx[0]
            mixg["lru_lambda"][j] = dlam[0] * P["lru"][j]["sig_neg"]
            mixg["lru_w_in"][j] = _dw(sv["h1b"], dmid, f"dw_lru_in_{l}")
            w_in_v = _plane_view(W["lru_w_in"], j)
        dy = mm_res_nt(dmid, w_in_v, dz, ALPHA, f"mix_dx_{l}")
        dy, g_ln_g[l][0], g_ln_b[l][0], lists["ffn1_w_out"][l], lists["ffn1_w_in"][l] = ffn_bwd(
            dy, sv["z1"], P["ln_g"][l, 0], sv["gu1"], sv["a1"], sv["h0b"], l * FF_SHARD, l, f"1_{l}")

    G = dict(lists)
    G.update(mixg)
    G["ln_g"] = jnp.stack([jnp.concatenate(r, axis=0) for r in g_ln_g])
    G["ln_b"] = jnp.stack([jnp.concatenate(r, axis=0) for r in g_ln_b])
    return loss, dy, G


def _lru_params(full, j):
    lam = full["lru_lambda"][j]
    return {
        "conv_w": full["lru_conv_w"][j], "conv_b": full["lru_conv_b"][j].reshape(1, -1),
        "w_ra": full["lru_w_ra"][j], "w_rx": full["lru_w_rx"][j],
        "b_ra": full["lru_b_ra"][j].reshape(1, -1), "b_rx": full["lru_b_rx"][j].reshape(1, -1),
        "lsl": jax.nn.log_sigmoid(lam).reshape(1, -1), "sig_neg": jax.nn.sigmoid(-lam),
    }


def _col_shards(full_list, cols):
    return jnp.stack([f.reshape(f.shape[0], N_DEV, cols).transpose(1, 0, 2) for f in full_list], axis=1)


def _row_shards(full_list, rows):
    return jnp.concatenate([f.reshape(N_DEV, rows, f.shape[1]) for f in full_list], axis=1)


def kernel(x, ffn1_w_in, ffn1_w_out, ffn2_w_in, ffn2_w_out, ln_g, ln_b, attn_w_qkv, attn_sinks, attn_w_o, lru_w_in, lru_conv_w, lru_conv_b, lru_w_ra, lru_b_ra, lru_w_rx, lru_b_rx, lru_lambda, lru_w_out, loss_target, m_ffn1_w_in, m_ffn1_w_out, m_ffn2_w_in, m_ffn2_w_out, m_ln_g, m_ln_b, m_attn_w_qkv, m_attn_sinks, m_attn_w_o, m_lru_w_in, m_lru_conv_w, m_lru_conv_b, m_lru_w_ra, m_lru_b_ra, m_lru_w_rx, m_lru_b_rx, m_lru_lambda, m_lru_w_out, v_ffn1_w_in, v_ffn1_w_out, v_ffn2_w_in, v_ffn2_w_out, v_ln_g, v_ln_b, v_attn_w_qkv, v_attn_sinks, v_attn_w_o, v_lru_w_in, v_lru_conv_w, v_lru_conv_b, v_lru_w_ra, v_lru_b_ra, v_lru_w_rx, v_lru_b_rx, v_lru_lambda, v_lru_w_out):
    args = dict(locals())
    b16 = lambda a: a.astype(BF16)

    r_pack = jnp.concatenate([b16(ffn1_w_out).reshape(-1, D_MODEL), b16(ffn2_w_out).reshape(-1, D_MODEL),
                              b16(attn_w_o).reshape(-1, D_MODEL), b16(lru_w_out).reshape(-1, D_MODEL)], axis=0)
    c704 = jnp.concatenate([b16(ffn1_w_in), b16(ffn2_w_in)], axis=0)
    c192 = b16(attn_w_qkv)
    c256 = jnp.concatenate([b16(lru_w_in).reshape(-1, RNN_W), b16(lru_w_ra).reshape(-1, RNN_W),
                            b16(lru_w_rx).reshape(-1, RNN_W)], axis=0)
    sm = _small_pack(args, "")
    g_r, g704, g192, g256, g_sm = all_gather_packs([r_pack, c704, c192, c256, sm])

    W = {
        "R": g_r,
        "w_in": g704.transpose(1, 2, 0, 3).reshape(2 * DEPTH, D_MODEL, 2 * D_FF),
        "w_qkv": g192.transpose(1, 2, 0, 3).reshape(2, D_MODEL, QKV),
        "lru_w_in": g256[:, :C256_RA].reshape(N_DEV, 2, D_MODEL, RNN_W).transpose(1, 2, 0, 3).reshape(2, D_MODEL, 2 * D_RNN),
    }
    gate = lambda lo: (g256[:, lo:lo + C256_RX - C256_RA].reshape(N_DEV, 2, RNN_BLOCKS, RNN_W // N_DEV, RNN_W)
                       .transpose(1, 2, 0, 3, 4).reshape(2, RNN_BLOCKS, RNN_W, RNN_W))
    full = {"lru_w_ra": gate(C256_RA), "lru_w_rx": gate(C256_RX)}
    gflat = g_sm.reshape(N_DEV, SM_ROWS * 1024)
    off = 0
    for n, shape, axis in SMALL:
        size = math.prod(shape)
        full[n] = _full_from_gathered(gflat[:, off:off + size], shape, axis)
        off += size
    P = {"ln_g": full["ln_g"], "ln_b": full["ln_b"], "lru": [_lru_params(full, j) for j in range(2)]}

    loss, dx, G = _local_step(x[0], loss_target[0], W, P, attn_sinks)

    s_r = jnp.concatenate([_row_shards(G["ffn1_w_out"], FF_SHARD), _row_shards(G["ffn2_w_out"], FF_SHARD),
                           _row_shards(G["attn_w_o"], MIX_SHARD), _row_shards(G["lru_w_out"], MIX_SHARD)], axis=1)
    s704 = _col_shards(G["ffn1_w_in"] + G["ffn2_w_in"], 704).reshape(N_DEV, 2 * DEPTH * D_MODEL, 704)
    s192 = _col_shards(G["attn_w_qkv"], 192).reshape(N_DEV, 2 * D_MODEL, 192)
    gates = lambda gs: jnp.stack(gs).reshape(2, RNN_BLOCKS, N_DEV, RNN_W // N_DEV, RNN_W).transpose(2, 0, 1, 3, 4).reshape(
        N_DEV, C256_RX - C256_RA, RNN_W)
    s256 = jnp.concatenate([_col_shards(G["lru_w_in"], RNN_W).reshape(N_DEV, C256_RA, RNN_W), gates(G["lru_w_ra"]),
                            gates(G["lru_w_rx"])], axis=1)
    small_g = {n: jnp.stack(G[n]) if isinstance(G[n], list) else G[n] for n, _, _ in SMALL}
    tail = jnp.concatenate([jnp.stack(G["attn_sinks"]).reshape(-1), loss.reshape(1)])
    tail = jnp.pad(tail, (0, SM_ROWS * 1024 - SMALL_N - tail.shape[0]))
    s_sm = jnp.concatenate([_shards_of_full(small_g[n], shape, axis) for n, shape, axis in SMALL]
                           + [jnp.broadcast_to(tail, (N_DEV, tail.shape[0]))], axis=1).reshape(N_DEV, SM_ROWS, 1024)
    sends = [s.reshape((4, 2) + s.shape[1:]) for s in (s_r, s704, s192, s256, s_sm)]
    gots = pair_exchange(sends)
    tiles = (256, 256, 512, 512, SM_ROWS)
    names = ("r", "c704", "c192", "c256", "sm")
    parts = chip_exchange([pair_sum(s, g, F32 if nm == "sm" else BF16, t, "pair_sum_" + nm)
                           for s, g, t, nm in zip(sends, gots, tiles, names)])
    p_r, p704, p192, p256, p_sm = parts

    fam = lambda n: (args[n], args["m_" + n], args["v_" + n])
    res = {
        "ffn1_w_out": adamw(p_r, 0, *fam("ffn1_w_out"), FF_SHARD, "adamw_ffn1_w_out"),
        "ffn2_w_out": adamw(p_r, R_FFN2, *fam("ffn2_w_out"), FF_SHARD, "adamw_ffn2_w_out"),
        "attn_w_o": adamw(p_r, R_ATTN, *fam("attn_w_o"), 2 * MIX_SHARD, "adamw_attn_w_o"),
        "lru_w_out": adamw(p_r, R_LRU, *fam("lru_w_out"), 2 * MIX_SHARD, "adamw_lru_w_out"),
        "ffn1_w_in": adamw(p704, 0, *fam("ffn1_w_in"), 512, "adamw_ffn1_w_in"),
        "ffn2_w_in": adamw(p704, DEPTH * D_MODEL, *fam("ffn2_w_in"), 512, "adamw_ffn2_w_in"),
        "attn_w_qkv": adamw(p192, 0, *fam("attn_w_qkv"), 512, "adamw_attn_w_qkv"),
        "lru_w_in": adamw(p256, 0, *fam("lru_w_in"), 512, "adamw_lru_w_in"),
        "lru_w_ra": adamw(p256, C256_RA, *fam("lru_w_ra"), 256, "adamw_lru_w_ra"),
        "lru_w_rx": adamw(p256, C256_RX, *fam("lru_w_rx"), 256, "adamw_lru_w_rx"),
    }
    sm_out = adamw(p_sm, 0, sm, _small_pack(args, "m_"), _small_pack(args, "v_"), SM_ROWS, "adamw_small")
    for k, pack in enumerate(sm_out):
        for n, val in _small_unpack(pack).items():
            res.setdefault(n, [None] * 4)[k] = val
    loss_total = sm_out[0].reshape(-1)[LOSS_OFF]
    out = [loss_total, dx[None]]
    for k in range(4):
        out += [res[n][k] for n in WEIGHTS]
    return tuple(out)
```

```python
import math

import jax
import jax.numpy as jnp
import numpy as np
from jax import lax
from jax.experimental import pallas as pl
from jax.experimental.pallas import tpu as pltpu

F32 = jnp.float32
BF16 = jnp.bfloat16

D_MODEL = 1024
DEPTH = 4
N_HEADS = 16
N_KV = 4
HEAD_DIM = 64
GROUP = 4
BLOCK = 128
ROPE_THETA = 10000.0
D_RNN = 1024
RNN_BLOCKS = 4
RNN_W = 256
CONV_W = 4
LRU_C = 8.0
D_FF = 2816
ALPHA = (2.0 * DEPTH) ** 0.25
LN_EPS = 1e-5
QKV = (N_HEADS + 2 * N_KV) * HEAD_DIM
N_DEV = 8

ADAM_LR = 0.001
ADAM_B1 = 0.9
ADAM_B2 = 0.999
ADAM_EPS = 1e-08
ADAM_WD = 0.01
ADAM_STEP = 10

VMEM_LIMIT = 52 * 1024 * 1024
NEG = float(np.finfo(np.float32).min)

MESH = pl.DeviceIdType.MESH
ANY = pl.BlockSpec(memory_space=pl.ANY)

FF_SHARD = D_FF // N_DEV
MIX_SHARD = D_MODEL // N_DEV
R_FFN2 = DEPTH * FF_SHARD
R_ATTN = 2 * DEPTH * FF_SHARD
R_LRU = R_ATTN + 2 * MIX_SHARD
R_ROWS = R_LRU + 2 * MIX_SHARD
C256_RA = 2 * D_MODEL
C256_RX = C256_RA + 2 * RNN_BLOCKS * (RNN_W // N_DEV)
C256_ROWS = C256_RX + 2 * RNN_BLOCKS * (RNN_W // N_DEV)

SMALL = (
    ("ln_g", (4, 3, 128), 2),
    ("ln_b", (4, 3, 128), 2),
    ("lru_conv_w", (2, 4, 128), 2),
    ("lru_conv_b", (2, 128), 1),
    ("lru_b_ra", (2, 128), 1),
    ("lru_b_rx", (2, 128), 1),
    ("lru_lambda", (2, 128), 1),
)
WEIGHTS = ("ffn1_w_in", "ffn1_w_out", "ffn2_w_in", "ffn2_w_out", "ln_g", "ln_b", "attn_w_qkv", "attn_sinks",
           "attn_w_o", "lru_w_in", "lru_conv_w", "lru_conv_b", "lru_w_ra", "lru_b_ra", "lru_w_rx", "lru_b_rx",
           "lru_lambda", "lru_w_out")
SMALL_N = sum(math.prod(s) for _, s, _ in SMALL)
SINK_OFF = SMALL_N
LOSS_OFF = SMALL_N + 32
SM_ROWS = 8


def _cp(*sem):
    return pltpu.CompilerParams(dimension_semantics=sem, vmem_limit_bytes=VMEM_LIMIT)


def _dot(a, b):
    return jnp.dot(a, b, preferred_element_type=F32)


def _dot_tn(a, b):
    return lax.dot_general(a, b, (((0,), (0,)), ((), ())), preferred_element_type=F32)


def _dot_nt(a, b):
    return lax.dot_general(a, b, (((1,), (1,)), ((), ())), preferred_element_type=F32)


def _ln(z, g, b):
    mu = jnp.mean(z, axis=-1, keepdims=True)
    xc = z - mu
    var = jnp.mean(xc * xc, axis=-1, keepdims=True)
    return xc * lax.rsqrt(var + LN_EPS) * g + b


def _rows_view(pack, rows_per_dev, row0):
    return (pack, (N_DEV, rows_per_dev, D_MODEL), (0, row0 // rows_per_dev, 0), (N_DEV * rows_per_dev, D_MODEL))


def _mat_view(arr, cols=None, cblk=0):
    k, n = arr.shape
    cols = n if cols is None else cols
    return (arr, (k, cols), (0, cblk), (k, cols))


def _vspec(view):
    _, bshape, bidx, _ = view
    return pl.BlockSpec(bshape, lambda *_: bidx)


def _vload(view, ref):
    return ref[...].reshape(view[3])


class Comm:
    def __init__(self, ins, out_shapes, sems, start, finish, aliases=None):
        self.ins, self.out_shapes, self.sems = list(ins), list(out_shapes), list(sems)
        self.start, self.finish, self.aliases = start, finish, dict(aliases or {})


def _call(body, *, grid, in_specs, out_specs, out_shape, operands, name, sem, scratch=(), comm=None):
    n_in, n_out, n_scr = len(in_specs), len(out_specs), len(scratch)
    if comm is None:
        return pl.pallas_call(body, grid=grid, in_specs=list(in_specs), out_specs=list(out_specs),
                              out_shape=list(out_shape), scratch_shapes=list(scratch), compiler_params=_cp(*sem),
                              name=name)(*operands), []
    nci, nco = len(comm.ins), len(comm.out_shapes)

    def hosted(*refs):
        ins, cins = refs[:n_in], refs[n_in:n_in + nci]
        o0 = n_in + nci
        outs, couts = refs[o0:o0 + n_out], refs[o0 + n_out:o0 + n_out + nco]
        s0 = o0 + n_out + nco
        scr, csems = refs[s0:s0 + n_scr], refs[s0 + n_scr:]
        first = last = None
        for ax, size in enumerate(grid):
            pid = pl.program_id(ax)
            f, e = pid == 0, pid == size - 1
            first = f if first is None else jnp.logical_and(first, f)
            last = e if last is None else jnp.logical_and(last, e)

        @pl.when(first)
        def _():
            comm.start(cins, couts, csems)

        body(*ins, *outs, *scr)

        @pl.when(last)
        def _():
            comm.finish(cins, couts, csems)

    res = pl.pallas_call(
        hosted, grid=grid, in_specs=list(in_specs) + [ANY] * nci, out_specs=list(out_specs) + [ANY] * nco,
        out_shape=list(out_shape) + comm.out_shapes, scratch_shapes=list(scratch) + comm.sems,
        input_output_aliases={n_in + i: n_out + o for i, o in comm.aliases.items()},
        compiler_params=_cp(*(("arbitrary",) * len(grid))), name=name)(*operands, *comm.ins)
    return res[:n_out], res[n_out:]


def run_comm(comm, name):
    nci, nco = len(comm.ins), len(comm.out_shapes)

    def body(*refs):
        cins, couts, csems = refs[:nci], refs[nci:nci + nco], refs[nci + nco:]
        comm.start(cins, couts, csems)
        comm.finish(cins, couts, csems)

    return pl.pallas_call(
        body, in_specs=[ANY] * nci, out_specs=[ANY] * nco, out_shape=comm.out_shapes, scratch_shapes=comm.sems,
        input_output_aliases=dict(comm.aliases), name=name)(*comm.ins)


def mm_plain(a, wv, out_dtype, name, nt=False, tm=512):
    S, K = a.shape
    N = wv[3][0] if nt else wv[3][1]
    tm = min(tm, S)
    dot = _dot_nt if nt else _dot

    def body(a_ref, w_ref, o_ref):
        o_ref[...] = dot(a_ref[...], _vload(wv, w_ref)).astype(out_dtype)

    return pl.pallas_call(
        body, grid=(S // tm,),
        in_specs=[pl.BlockSpec((tm, K), lambda i: (i, 0)), _vspec(wv)],
        out_specs=pl.BlockSpec((tm, N), lambda i: (i, 0)),
        out_shape=jax.ShapeDtypeStruct((S, N), out_dtype),
        compiler_params=_cp("parallel"), name=name)(a, wv[0])


def mm_res_nt(a, wv, r, alpha, name, tm=256):
    S, K = a.shape
    N = wv[3][0]
    tm = min(tm, S)

    def body(a_ref, w_ref, r_ref, o_ref):
        o_ref[...] = _dot_nt(a_ref[...], _vload(wv, w_ref)) + alpha * r_ref[...]

    return pl.pallas_call(
        body, grid=(S // tm,),
        in_specs=[pl.BlockSpec((tm, K), lambda i: (i, 0)), _vspec(wv), pl.BlockSpec((tm, N), lambda i: (i, 0))],
        out_specs=pl.BlockSpec((tm, N), lambda i: (i, 0)),
        out_shape=jax.ShapeDtypeStruct((S, N), F32),
        compiler_params=_cp("parallel"), name=name)(a, wv[0], r)


def mm_res2_nt(a3, wv0, wv1, r, alpha, name, tm=256, comm=None):
    _, S, K = a3.shape
    N = wv0[3][0]
    tm = min(tm, S)

    def body(a_ref, w0_ref, w1_ref, r_ref, o_ref):
        o_ref[...] = (_dot_nt(a_ref[0], _vload(wv0, w0_ref)) + _dot_nt(a_ref[1], _vload(wv1, w1_ref))
                      + alpha * r_ref[...])

    (out,), couts = _call(
        body, grid=(S // tm,),
        in_specs=[pl.BlockSpec((2, tm, K), lambda i: (0, i, 0)), _vspec(wv0), _vspec(wv1),
                  pl.BlockSpec((tm, N), lambda i: (i, 0))],
        out_specs=[pl.BlockSpec((tm, N), lambda i: (i, 0))],
        out_shape=[jax.ShapeDtypeStruct((S, N), F32)],
        operands=(a3, wv0[0], wv1[0], r), name=name, sem=("parallel",), comm=comm)
    return out, couts


def mm_ln(a, wv, h, g, b, scale, name, tm=256, comm=None):
    S, K = a.shape
    tm = min(tm, S)

    def body(a_ref, w_ref, h_ref, g_ref, b_ref, z_ref, y_ref, yb_ref):
        z = ALPHA * h_ref[...] + scale * _dot(a_ref[...], _vload(wv, w_ref))
        y = _ln(z, g_ref[...], b_ref[...])
        z_ref[...] = z
        y_ref[...] = y
        yb_ref[...] = y.astype(BF16)

    row = pl.BlockSpec((tm, D_MODEL), lambda i: (i, 0))
    vec = pl.BlockSpec((1, D_MODEL), lambda i: (0, 0))
    return _call(
        body, grid=(S // tm,),
        in_specs=[pl.BlockSpec((tm, K), lambda i: (i, 0)), _vspec(wv), row, vec, vec],
        out_specs=[row, row, row],
        out_shape=[jax.ShapeDtypeStruct((S, D_MODEL), F32), jax.ShapeDtypeStruct((S, D_MODEL), F32),
                   jax.ShapeDtypeStruct((S, D_MODEL), BF16)],
        operands=(a, wv[0], h, g.reshape(1, -1), b.reshape(1, -1)), name=name, sem=("parallel",), comm=comm)


def ffn_up(xb, w_in, name, tm=512, tn=1408, comm=None):
    S = xb.shape[0]
    tm = min(tm, S)
    nj = D_FF // tn

    def body(x_ref, wg_ref, wu_ref, gu_ref, a_ref):
        x = x_ref[...]
        g = _dot(x, wg_ref[...])
        u = _dot(x, wu_ref[...])
        gu_ref[0] = g.astype(BF16)
        gu_ref[1] = u.astype(BF16)
        a_ref[...] = (g * jax.nn.sigmoid(g) * u).astype(BF16)

    return _call(
        body, grid=(nj, S // tm),
        in_specs=[pl.BlockSpec((tm, D_MODEL), lambda j, i: (i, 0)),
                  pl.BlockSpec((D_MODEL, tn), lambda j, i: (0, j)),
                  pl.BlockSpec((D_MODEL, tn), lambda j, i: (0, nj + j))],
        out_specs=[pl.BlockSpec((2, tm, tn), lambda j, i: (0, i, j)), pl.BlockSpec((tm, tn), lambda j, i: (i, j))],
        out_shape=[jax.ShapeDtypeStruct((2, S, D_FF), BF16), jax.ShapeDtypeStruct((S, D_FF), BF16)],
        operands=(xb, w_in, w_in), name=name, sem=("parallel", "parallel"), comm=comm)


def ffn_mid_bwd(dfb, r_ff, blk, gu, name, tm=512, comm=None):
    S = dfb.shape[0]
    tm = min(tm, S)
    tn = 4 * FF_SHARD

    def body(df_ref, w_ref, gu_ref, dh_ref):
        da = _dot_nt(df_ref[...], w_ref[...].reshape(tn, D_MODEL))
        g = gu_ref[0].astype(F32)
        u = gu_ref[1].astype(F32)
        sg = jax.nn.sigmoid(g)
        dh_ref[0] = (da * u * (sg * (1.0 + g * (1.0 - sg)))).astype(BF16)
        dh_ref[1] = (da * (g * sg)).astype(BF16)

    gspec = pl.BlockSpec((2, tm, tn), lambda j, i: (0, i, j))
    (out,), couts = _call(
        body, grid=(2, S // tm),
        in_specs=[pl.BlockSpec((tm, D_MODEL), lambda j, i: (i, 0)),
                  pl.BlockSpec((4, FF_SHARD, D_MODEL), lambda j, i: (j, blk, 0)), gspec],
        out_specs=[gspec],
        out_shape=[jax.ShapeDtypeStruct((2, S, D_FF), BF16)],
        operands=(dfb, r_ff, gu), name=name, sem=("parallel", "parallel"), comm=comm)
    return out, couts


def mm_tn(a, b, name, tm, tn, ts=1024, planes=1, comm=None):
    S, M = a.shape
    N = b.shape[-1] * planes
    ts = min(ts, S)
    per = b.shape[-1] // tn

    def body(a_ref, b_ref, o_ref):
        @pl.when(pl.program_id(2) == 0)
        def _():
            o_ref[...] = jnp.zeros_like(o_ref)

        o_ref[...] += _dot_tn(a_ref[...], b_ref[...])

    if planes == 1:
        bspec = pl.BlockSpec((ts, tn), lambda i, j, s: (s, j))
    else:
        bspec = pl.BlockSpec((None, ts, tn), lambda i, j, s: (j // per, s, j % per))
    (out,), couts = _call(
        body, grid=(M // tm, N // tn, S // ts),
        in_specs=[pl.BlockSpec((ts, tm), lambda i, j, s: (s, i)), bspec],
        out_specs=[pl.BlockSpec((tm, tn), lambda i, j, s: (i, j))],
        out_shape=[jax.ShapeDtypeStruct((M, N), F32)],
        operands=(a, b), name=name, sem=("parallel", "parallel", "arbitrary"), comm=comm)
    return out, couts


def ln_bwd(dy, z, g, out_scale, name, tm=512):
    S = dy.shape[0]
    tm = min(tm, S)

    def body(dy_ref, z_ref, g_ref, dz_ref, dzb_ref, dg_ref, db_ref):
        @pl.when(pl.program_id(0) == 0)
        def _():
            dg_ref[...] = jnp.zeros_like(dg_ref)
            db_ref[...] = jnp.zeros_like(db_ref)

        z = z_ref[...]
        dy_ = dy_ref[...]
        mu = jnp.mean(z, axis=-1, keepdims=True)
        xc = z - mu
        var = jnp.mean(xc * xc, axis=-1, keepdims=True)
        rstd = lax.rsqrt(var + LN_EPS)
        xh = xc * rstd
        dxh = dy_ * g_ref[...]
        m1 = jnp.mean(dxh, axis=-1, keepdims=True)
        m2 = jnp.mean(dxh * xh, axis=-1, keepdims=True)
        dz = rstd * (dxh - m1 - xh * m2)
        dz_ref[...] = dz
        dzb_ref[...] = (out_scale * dz).astype(BF16)
        dg_ref[...] += jnp.sum(dy_ * xh, axis=0, keepdims=True)
        db_ref[...] += jnp.sum(dy_, axis=0, keepdims=True)

    row = pl.BlockSpec((tm, D_MODEL), lambda i: (i, 0))
    vec = pl.BlockSpec((1, D_MODEL), lambda i: (0, 0))
    return pl.pallas_call(
        body, grid=(S // tm,),
        in_specs=[row, row, vec],
        out_specs=[row, row, vec, vec],
        out_shape=[jax.ShapeDtypeStruct((S, D_MODEL), F32), jax.ShapeDtypeStruct((S, D_MODEL), BF16),
                   jax.ShapeDtypeStruct((1, D_MODEL), F32), jax.ShapeDtypeStruct((1, D_MODEL), F32)],
        compiler_params=_cp("arbitrary"), name=name)(dy, z, g.reshape(1, -1))


def loss_head(y, t, name, tm=512):
    S = y.shape[0]
    tm = min(tm, S)
    nt = S // tm

    def body(y_ref, t_ref, dy_ref, l_ref):
        i = pl.program_id(0)

        @pl.when(i == 0)
        def _():
            l_ref[...] = jnp.zeros_like(l_ref)

        e = y_ref[...] - t_ref[...]
        dy_ref[...] = e * (1.0 / D_MODEL)
        l_ref[...] += jnp.sum(e * e, axis=0, keepdims=True)

        @pl.when(i == nt - 1)
        def _():
            tot = jnp.sum(l_ref[...], axis=1, keepdims=True) * (0.5 / D_MODEL)
            l_ref[...] = jnp.broadcast_to(tot, l_ref.shape)

    row = pl.BlockSpec((tm, D_MODEL), lambda i: (i, 0))
    vec = pl.BlockSpec((1, D_MODEL), lambda i: (0, 0))
    return pl.pallas_call(
        body, grid=(nt,), in_specs=[row, row], out_specs=[row, vec],
        out_shape=[jax.ShapeDtypeStruct((S, D_MODEL), F32), jax.ShapeDtypeStruct((1, D_MODEL), F32)],
        compiler_params=_cp("arbitrary"), name=name)(y, t)


def _rope_tables(S):
    pos = jnp.arange(S, dtype=F32)
    inv_freq = ROPE_THETA ** (-jnp.arange(0, HEAD_DIM, 2, dtype=F32) / HEAD_DIM)
    ang = pos[:, None] * inv_freq[None, :]
    cos, sin = jnp.cos(ang), jnp.sin(ang)
    cosf = jnp.concatenate([cos, cos, cos, cos], axis=1)
    sinf = jnp.concatenate([-sin, sin, -sin, sin], axis=1)
    return cosf, sinf


def _rot(t, c, s, first):
    sw = jnp.where(first, pltpu.roll(t, 96, 1), pltpu.roll(t, 32, 1))
    return t * c + sw * s


def rope_fwd(qkv, cosf, sinf, name, tm=512):
    S = qkv.shape[0]
    tm = min(tm, S)

    def body(x_ref, c_ref, s_ref, q_ref, k_ref, v_ref):
        c = c_ref[...]
        s = s_ref[...]
        first = (lax.broadcasted_iota(jnp.int32, (tm, 128), 1) % HEAD_DIM) < (HEAD_DIM // 2)
        for j in range(8):
            q_ref[:, 128 * j:128 * (j + 1)] = _rot(x_ref[:, 128 * j:128 * (j + 1)], c, s, first).astype(BF16)
        for j in range(2):
            k_ref[:, 128 * j:128 * (j + 1)] = _rot(x_ref[:, 1024 + 128 * j:1024 + 128 * (j + 1)], c, s, first).astype(BF16)
        v_ref[...] = x_ref[:, 1280:1536].astype(BF16)

    tab = pl.BlockSpec((tm, 128), lambda i: (i, 0))
    return pl.pallas_call(
        body, grid=(S // tm,),
        in_specs=[pl.BlockSpec((tm, QKV), lambda i: (i, 0)), tab, tab],
        out_specs=[pl.BlockSpec((tm, 1024), lambda i: (i, 0)), pl.BlockSpec((tm, 256), lambda i: (i, 0)),
                   pl.BlockSpec((tm, 256), lambda i: (i, 0))],
        out_shape=[jax.ShapeDtypeStruct((S, 1024), BF16), jax.ShapeDtypeStruct((S, 256), BF16),
                   jax.ShapeDtypeStruct((S, 256), BF16)],
        compiler_params=_cp("parallel"), name=name)(qkv, cosf, sinf)


def rope_bwd(dq, dkc, dkp, dvc, dvp, cosf, sinf, name, tq):
    S = dq.shape[0]
    nt = S // tq

    def body(dq_ref, dkc_ref, dkp_ref, dvc_ref, dvp_ref, c_ref, s_ref, o_ref):
        i = pl.program_id(0)
        c = c_ref[...]
        s = -s_ref[...]
        first = (lax.broadcasted_iota(jnp.int32, (tq, 128), 1) % HEAD_DIM) < (HEAD_DIM // 2)
        for j in range(8):
            o_ref[:, 128 * j:128 * (j + 1)] = _rot(dq_ref[:, 128 * j:128 * (j + 1)], c, s, first).astype(BF16)
        has_next = i < nt - 1
        rows = lax.broadcasted_iota(jnp.int32, (tq, 256), 0)
        pad = jnp.zeros((tq - BLOCK, 256), F32)
        halo_k = jnp.concatenate([pad, dkp_ref[...]], axis=0)
        halo_v = jnp.concatenate([pad, dvp_ref[...]], axis=0)
        use = jnp.logical_and(has_next, rows >= tq - BLOCK)
        dk = dkc_ref[...] + jnp.where(use, halo_k, 0.0)
        dv = dvc_ref[...] + jnp.where(use, halo_v, 0.0)
        for j in range(2):
            o_ref[:, 1024 + 128 * j:1024 + 128 * (j + 1)] = _rot(dk[:, 128 * j:128 * (j + 1)], c, s, first).astype(BF16)
        o_ref[:, 1280:1536] = dv.astype(BF16)

    tab = pl.BlockSpec((tq, 128), lambda i: (i, 0))
    cur = pl.BlockSpec((tq, 256), lambda i: (i, 0))
    nxt = pl.BlockSpec((BLOCK, 256), lambda i: (jnp.minimum(i + 1, nt - 1), 0))
    return pl.pallas_call(
        body, grid=(nt,),
        in_specs=[pl.BlockSpec((tq, 1024), lambda i: (i, 0)), cur, nxt, cur, nxt, tab, tab],
        out_specs=pl.BlockSpec((tq, QKV), lambda i: (i, 0)),
        out_shape=jax.ShapeDtypeStruct((S, QKV), BF16),
        compiler_params=_cp("parallel"), name=name)(dq, dkc, dkp, dvc, dvp, cosf, sinf)


def _attn_probs(qs, kw, sink_col, first_block):
    s = _dot_nt(qs, kw) * (HEAD_DIM ** -0.5)
    r = lax.broadcasted_iota(jnp.int32, s.shape, 0) % BLOCK
    c = lax.broadcasted_iota(jnp.int32, s.shape, 1)
    ok = jnp.logical_and(c > r, c <= r + BLOCK)
    ok = jnp.logical_and(ok, jnp.logical_or(c >= BLOCK, jnp.logical_not(first_block)))
    s = jnp.where(ok, s, NEG)
    m = jnp.maximum(jnp.max(s, axis=1, keepdims=True), sink_col)
    p = jnp.exp(s - m)
    es = jnp.exp(sink_col - m)
    den = jnp.sum(p, axis=1, keepdims=True) + es
    return p / den, es / den


def _sink_col(sink_ref, g):
    rid = lax.broadcasted_iota(jnp.int32, (GROUP * BLOCK, 1), 0) // BLOCK
    col = jnp.zeros((GROUP * BLOCK, 1), F32)
    for j in range(GROUP):
        col = jnp.where(rid == j, sink_ref[GROUP * g + j], col)
    return col


def _kv_window(kc_ref, kp_ref, b):
    if b == 0:
        return jnp.concatenate([kp_ref[...], kc_ref[0:BLOCK, :]], axis=0)
    return kc_ref[BLOCK * (b - 1):BLOCK * (b + 1), :]


def _attn_specs(tq):
    nsub = tq // BLOCK
    qspec = pl.BlockSpec((GROUP, tq, HEAD_DIM), lambda g, n: (g, n, 0))
    cur = pl.BlockSpec((None, tq, HEAD_DIM), lambda g, n: (g, n, 0))
    prev = pl.BlockSpec((None, BLOCK, HEAD_DIM), lambda g, n: (g, jnp.maximum(n * nsub - 1, 0), 0))
    return qspec, cur, prev


def attn_fwd(q, k, v, sinks, name, tq):
    S = q.shape[1]
    nsub = tq // BLOCK

    def body(sink_ref, q_ref, kc_ref, kp_ref, vc_ref, vp_ref, o_ref):
        g = pl.program_id(0)
        n = pl.program_id(1)
        sink_col = _sink_col(sink_ref, g)
        for b in range(nsub):
            lo = BLOCK * b
            qs = jnp.concatenate([q_ref[j, lo:lo + BLOCK, :] for j in range(GROUP)], axis=0)
            pn, _ = _attn_probs(qs, _kv_window(kc_ref, kp_ref, b), sink_col, jnp.logical_and(n == 0, b == 0))
            o = _dot(pn.astype(BF16), _kv_window(vc_ref, vp_ref, b))
            for j in range(GROUP):
                o_ref[j, lo:lo + BLOCK, :] = o[BLOCK * j:BLOCK * (j + 1)].astype(BF16)

    qspec, cur, prev = _attn_specs(tq)
    return pl.pallas_call(
        body, grid=(N_KV, S // tq),
        in_specs=[pl.BlockSpec(memory_space=pltpu.SMEM), qspec, cur, prev, cur, prev],
        out_specs=qspec,
        out_shape=jax.ShapeDtypeStruct((N_HEADS, S, HEAD_DIM), BF16),
        compiler_params=_cp("parallel", "parallel"), name=name)(sinks, q, k, k, v, v)


def attn_bwd(q, k, v, do, sinks, name, tq):
    S = q.shape[1]
    nsub = tq // BLOCK
    nt = S // tq

    def body(sink_ref, q_ref, kc_ref, kp_ref, vc_ref, vp_ref, do_ref, dq_ref, dkc_ref, dkp_ref, dvc_ref, dvp_ref, ds_ref):
        g = pl.program_id(0)
        n = pl.program_id(1)

        @pl.when(n == 0)
        def _():
            ds_ref[...] = jnp.zeros_like(ds_ref)

        dkc_ref[...] = jnp.zeros_like(dkc_ref)
        dvc_ref[...] = jnp.zeros_like(dvc_ref)
        sink_col = _sink_col(sink_ref, g)
        rid = lax.broadcasted_iota(jnp.int32, (GROUP * BLOCK, 1), 0) // BLOCK
        sub = lax.broadcasted_iota(jnp.int32, (8, 128), 0)
        dsink = jnp.zeros((8, 128), F32)
        for b in range(nsub):
            lo = BLOCK * b
            qs = jnp.concatenate([q_ref[j, lo:lo + BLOCK, :] for j in range(GROUP)], axis=0)
            dos = jnp.concatenate([do_ref[j, lo:lo + BLOCK, :] for j in range(GROUP)], axis=0)
            kw = _kv_window(kc_ref, kp_ref, b)
            vw = _kv_window(vc_ref, vp_ref, b)
            pn, ps = _attn_probs(qs, kw, sink_col, jnp.logical_and(n == 0, b == 0))
            dp = _dot_nt(dos, vw)
            delta = jnp.sum(pn * dp, axis=1, keepdims=True)
            dsb = (pn * (dp - delta) * (HEAD_DIM ** -0.5)).astype(BF16)
            dq = _dot(dsb, kw)
            for j in range(GROUP):
                dq_ref[j, lo:lo + BLOCK, :] = dq[BLOCK * j:BLOCK * (j + 1)]
            dkw = _dot_tn(dsb, qs)
            dvw = _dot_tn(pn.astype(BF16), dos)
            if b == 0:
                dkp_ref[...] = dkw[0:BLOCK]
                dvp_ref[...] = dvw[0:BLOCK]
            else:
                dkc_ref[lo - BLOCK:lo, :] += dkw[0:BLOCK]
                dvc_ref[lo - BLOCK:lo, :] += dvw[0:BLOCK]
            dkc_ref[lo:lo + BLOCK, :] += dkw[BLOCK:2 * BLOCK]
            dvc_ref[lo:lo + BLOCK, :] += dvw[BLOCK:2 * BLOCK]
            sd = ps * delta
            for j in range(GROUP):
                tot = -jnp.sum(jnp.where(rid == j, sd, 0.0))
                dsink = dsink + jnp.where(sub == j, tot, 0.0)
        ds_ref[...] += dsink

    qspec, cur, prev = _attn_specs(tq)
    halo = pl.BlockSpec((None, BLOCK, HEAD_DIM), lambda g, n: (g, n, 0))
    return pl.pallas_call(
        body, grid=(N_KV, nt),
        in_specs=[pl.BlockSpec(memory_space=pltpu.SMEM), qspec, cur, prev, cur, prev, qspec],
        out_specs=[qspec, cur, halo, cur, halo, pl.BlockSpec((None, 8, 128), lambda g, n: (g, 0, 0))],
        out_shape=[jax.ShapeDtypeStruct((N_HEADS, S, HEAD_DIM), F32),
                   jax.ShapeDtypeStruct((N_KV, S, HEAD_DIM), F32), jax.ShapeDtypeStruct((N_KV, nt * BLOCK, HEAD_DIM), F32),
                   jax.ShapeDtypeStruct((N_KV, S, HEAD_DIM), F32), jax.ShapeDtypeStruct((N_KV, nt * BLOCK, HEAD_DIM), F32),
                   jax.ShapeDtypeStruct((N_KV, 8, 128), F32)],
        compiler_params=_cp("parallel", "arbitrary"), name=name)(sinks, q, k, k, v, v, do)


def _rows_before(cur, prev8, k):
    if k == 0:
        return cur
    n = cur.shape[0]
    ext = jnp.concatenate([prev8, cur], axis=0)
    return ext[8 - k:8 - k + n]


def _rows_after(cur, next8, k):
    if k == 0:
        return cur
    n = cur.shape[0]
    ext = jnp.concatenate([cur, next8], axis=0)
    return ext[k:k + n]


def _gelu(x):
    c = math.sqrt(2.0 / math.pi)
    t = jnp.tanh(c * (x + 0.044715 * (x * x * x)))
    return 0.5 * (1.0 + t), t


def _neg_expm1(u):
    ser = 1.0 + u * (1.0 / 9.0)
    for k in range(8, 1, -1):
        ser = 1.0 + (u * (1.0 / k)) * ser
    return jnp.where(u > -0.5, -(u * ser), 1.0 - jnp.exp(u))


def _block_diag(xb16, w_ref):
    return jnp.concatenate([_dot(xb16[:, RNN_W * n:RNN_W * (n + 1)], w_ref[n]) for n in range(RNN_BLOCKS)], axis=1)


def _lru_gates(xb, prev8, cw_ref, cb_ref, wra_ref, wrx_ref, bra_ref, brx_ref, lsl_ref):
    xc = cb_ref[...] + cw_ref[3:4, :] * xb
    for w in range(CONV_W - 1):
        xc = xc + cw_ref[w:w + 1, :] * _rows_before(xb, prev8, CONV_W - 1 - w)
    xcb = xc.astype(BF16)
    r = jax.nn.sigmoid(_block_diag(xcb, wra_ref) + bra_ref[...])
    ig = jax.nn.sigmoid(_block_diag(xcb, wrx_ref) + brx_ref[...])
    la = LRU_C * r * lsl_ref[...]
    a = jnp.exp(la)
    sq = jnp.sqrt(_neg_expm1(2.0 * la))
    return xc, xcb, r, ig, a, sq


def lru_fwd(xg, p, name, tm=256):
    S = xg.shape[0]
    tm = min(tm, S)

    def body(xg_ref, xp_ref, cw_ref, cb_ref, wra_ref, wrx_ref, bra_ref, brx_ref, lsl_ref, y_ref, h_ref, hc_ref, a_s, b_s):
        i = pl.program_id(0)
        xb = xg_ref[:, 0:D_RNN]
        gb = xg_ref[:, D_RNN:2 * D_RNN]
        prev8 = jnp.where(i > 0, xp_ref[:, 0:D_RNN], 0.0)
        xc, _, r, ig, a, sq = _lru_gates(xb, prev8, cw_ref, cb_ref, wra_ref, wrx_ref, bra_ref, brx_ref, lsl_ref)
        a_s[...] = a
        b_s[...] = sq * (ig * xc)

        @pl.when(i == 0)
        def _():
            hc_ref[...] = jnp.zeros_like(hc_ref)

        def chunk(c, h):
            o = pl.multiple_of(c * 8, 8)
            av = a_s[pl.ds(o, 8), :]
            bv = b_s[pl.ds(o, 8), :]
            rows = []
            for t in range(8):
                h = av[t:t + 1, :] * h + bv[t:t + 1, :]
                rows.append(h)
            h_ref[pl.ds(o, 8), :] = jnp.concatenate(rows, axis=0)
            return h

        h_last = lax.fori_loop(0, tm // 8, chunk, hc_ref[0:1, :])
        hc_ref[0:1, :] = h_last
        cdf, _ = _gelu(gb)
        y_ref[...] = (h_ref[...] * (gb * cdf)).astype(BF16)

    vec = pl.BlockSpec((1, D_RNN), lambda i: (0, 0))
    wsp = pl.BlockSpec((RNN_BLOCKS, RNN_W, RNN_W), lambda i: (0, 0, 0))
    return pl.pallas_call(
        body, grid=(S // tm,),
        in_specs=[pl.BlockSpec((tm, 2 * D_RNN), lambda i: (i, 0)),
                  pl.BlockSpec((8, 2 * D_RNN), lambda i: (jnp.maximum(i * (tm // 8) - 1, 0), 0)),
                  pl.BlockSpec((CONV_W, D_RNN), lambda i: (0, 0)), vec, wsp, wsp, vec, vec, vec],
        out_specs=[pl.BlockSpec((tm, D_RNN), lambda i: (i, 0)), pl.BlockSpec((tm, D_RNN), lambda i: (i, 0))],
        out_shape=[jax.ShapeDtypeStruct((S, D_RNN), BF16), jax.ShapeDtypeStruct((S, D_RNN), F32)],
        scratch_shapes=[pltpu.VMEM((8, D_RNN), F32), pltpu.VMEM((tm, D_RNN), F32), pltpu.VMEM((tm, D_RNN), F32)],
        compiler_params=_cp("arbitrary"), name=name)(
            xg, xg, p["conv_w"], p["conv_b"], p["w_ra"], p["w_rx"], p["b_ra"], p["b_rx"], p["lsl"])


def lru_bwd(dy, xg, h, p, name, tm=256):
    S = xg.shape[0]
    tm = min(tm, S)
    nt = S // tm

    def body(dy_ref, xg_ref, xp_ref, h_ref, hp_ref, cw_ref, cb_ref, wra_ref, wrx_ref, bra_ref, brx_ref,
             lsl_ref, dxg_ref, dcw_ref, dcb_ref, dwra_ref, dwrx_ref, dbra_ref, dbrx_ref, dlam_ref,
             lc_ref, nx_ref, a_s, g_s, l_s):
        i = pl.program_id(0)
        ti = nt - 1 - i

        @pl.when(i == 0)
        def _():
            lc_ref[...] = jnp.zeros_like(lc_ref)
            nx_ref[...] = jnp.zeros_like(nx_ref)
            for ref in (dcw_ref, dcb_ref, dwra_ref, dwrx_ref, dbra_ref, dbrx_ref, dlam_ref):
                ref[...] = jnp.zeros_like(ref)

        xb = xg_ref[:, 0:D_RNN]
        gb = xg_ref[:, D_RNN:2 * D_RNN]
        prev8 = jnp.where(ti > 0, xp_ref[:, 0:D_RNN], 0.0)
        xc, xcb, r, ig, a, sq = _lru_gates(xb, prev8, cw_ref, cb_ref, wra_ref, wrx_ref, bra_ref, brx_ref, lsl_ref)
        hh = h_ref[...]
        hprev = _rows_before(hh, jnp.where(ti > 0, hp_ref[...], 0.0), 1)
        dy_ = dy_ref[...]
        cdf, th = _gelu(gb)
        c0 = math.sqrt(2.0 / math.pi)
        dgate = cdf + gb * (0.5 * (1.0 - th * th) * c0 * (1.0 + 3.0 * 0.044715 * gb * gb))
        dgb = dy_ * hh * dgate
        a_s[...] = a
        g_s[...] = dy_ * (gb * cdf)

        def chunk(cc, carry):
            o = pl.multiple_of((tm // 8 - 1 - cc) * 8, 8)
            av = a_s[pl.ds(o, 8), :]
            gv = g_s[pl.ds(o, 8), :]
            rows = [None] * 8
            for t in range(7, -1, -1):
                lam_t = gv[t:t + 1, :] + carry
                rows[t] = lam_t
                carry = av[t:t + 1, :] * lam_t
            l_s[pl.ds(o, 8), :] = jnp.concatenate(rows, axis=0)
            return carry

        carry = lax.fori_loop(0, tm // 8, chunk, lc_ref[0:1, :])
        lc_ref[0:1, :] = carry
        lam = l_s[...]
        da = lam * hprev
        dixc = lam * sq
        di = dixc * xc
        dxc = dixc * ig
        dsq = lam * (ig * xc)
        dla = da * a - dsq * (a * a / sq)
        dr = dla * (LRU_C * lsl_ref[...])
        dlam_ref[...] += jnp.sum(dla * (LRU_C * r), axis=0, keepdims=True)
        dpr = dr * r * (1.0 - r)
        dpi = di * ig * (1.0 - ig)
        dbra_ref[...] += jnp.sum(dpr, axis=0, keepdims=True)
        dbrx_ref[...] += jnp.sum(dpi, axis=0, keepdims=True)
        dprb = dpr.astype(BF16)
        dpib = dpi.astype(BF16)
        back = []
        for n in range(RNN_BLOCKS):
            sl = slice(RNN_W * n, RNN_W * (n + 1))
            dwra_ref[n] += _dot_tn(xcb[:, sl], dprb[:, sl])
            dwrx_ref[n] += _dot_tn(xcb[:, sl], dpib[:, sl])
            back.append(_dot_nt(dprb[:, sl], wra_ref[n]) + _dot_nt(dpib[:, sl], wrx_ref[n]))
        dxc = dxc + jnp.concatenate(back, axis=1)
        dcb_ref[...] += jnp.sum(dxc, axis=0, keepdims=True)
        next8 = nx_ref[...]
        dxb = cw_ref[3:4, :] * dxc
        dcw_ref[3:4, :] += jnp.sum(dxc * xb, axis=0, keepdims=True)
        for w in range(CONV_W - 1):
            k = CONV_W - 1 - w
            dcw_ref[w:w + 1, :] += jnp.sum(dxc * _rows_before(xb, prev8, k), axis=0, keepdims=True)
            dxb = dxb + cw_ref[w:w + 1, :] * _rows_after(dxc, next8, k)
        nx_ref[...] = dxc[0:8, :]
        dxg_ref[:, 0:D_RNN] = dxb.astype(BF16)
        dxg_ref[:, D_RNN:2 * D_RNN] = dgb.astype(BF16)

    rev = lambda i: (nt - 1 - i, 0)
    before = lambda i: (jnp.maximum((nt - 1 - i) * (tm // 8) - 1, 0), 0)
    vec = pl.BlockSpec((1, D_RNN), lambda i: (0, 0))
    wsp = pl.BlockSpec((RNN_BLOCKS, RNN_W, RNN_W), lambda i: (0, 0, 0))
    cwsp = pl.BlockSpec((CONV_W, D_RNN), lambda i: (0, 0))
    return pl.pallas_call(
        body, grid=(nt,),
        in_specs=[pl.BlockSpec((tm, D_RNN), rev), pl.BlockSpec((tm, 2 * D_RNN), rev), pl.BlockSpec((8, 2 * D_RNN), before),
                  pl.BlockSpec((tm, D_RNN), rev), pl.BlockSpec((8, D_RNN), before),
                  cwsp, vec, wsp, wsp, vec, vec, vec],
        out_specs=[pl.BlockSpec((tm, 2 * D_RNN), rev), cwsp, vec, wsp, wsp, vec, vec, vec],
        out_shape=[jax.ShapeDtypeStruct((S, 2 * D_RNN), BF16), jax.ShapeDtypeStruct((CONV_W, D_RNN), F32),
                   jax.ShapeDtypeStruct((1, D_RNN), F32), jax.ShapeDtypeStruct((RNN_BLOCKS, RNN_W, RNN_W), F32),
                   jax.ShapeDtypeStruct((RNN_BLOCKS, RNN_W, RNN_W), F32), jax.ShapeDtypeStruct((1, D_RNN), F32),
                   jax.ShapeDtypeStruct((1, D_RNN), F32), jax.ShapeDtypeStruct((1, D_RNN), F32)],
        scratch_shapes=[pltpu.VMEM((8, D_RNN), F32), pltpu.VMEM((8, D_RNN), F32), pltpu.VMEM((tm, D_RNN), F32),
                        pltpu.VMEM((tm, D_RNN), F32), pltpu.VMEM((tm, D_RNN), F32)],
        compiler_params=_cp("arbitrary"), name=name)(
            dy, xg, xg, h, h, p["conv_w"], p["conv_b"], p["w_ra"], p["w_rx"], p["b_ra"], p["b_rx"], p["lsl"])


def _place():
    x, y, c = lax.axis_index("x"), lax.axis_index("y"), lax.axis_index("c")
    chips = [(1 - x, y), (x, 1 - y), (1 - x, 1 - y)]
    return x, y, c, chips


def comm_ag(arrs):
    n = len(arrs)

    def copies(ins, outs, sems):
        send_sems, recv_sems, local_sems = sems
        x, y, c, chips = _place()
        me, sibling = (x, y, c), (x, y, 1 - c)

        def rows(a, px, py, pc):
            return outs[a].at[4 * px + 2 * py + pc]

        def copy(a, k, block, to, own=False):
            return pltpu.make_async_remote_copy(
                src_ref=ins[a] if own else rows(a, *block), dst_ref=rows(a, *block),
                send_sem=send_sems.at[k, a], recv_sem=recv_sems.at[k, a], device_id=to, device_id_type=MESH)

        mine = [pltpu.make_async_copy(ins[a], rows(a, *me), local_sems.at[a]) for a in range(n)]
        first = [copy(a, 1 + j, me, (*chip, c), own=True) for j, chip in enumerate(chips) for a in range(n)]
        first += [copy(a, 0, me, sibling, own=True) for a in range(n)]
        return copy, chips, c, me, sibling, mine, first

    def start(ins, outs, sems):
        _, _, _, _, _, mine, first = copies(ins, outs, sems)
        for cp in mine + first:
            cp.start()

    def finish(ins, outs, sems):
        copy, chips, c, me, sibling, mine, first = copies(ins, outs, sems)
        passed = []
        for j, chip in enumerate(chips):
            for a in range(n):
                copy(a, 1 + j, (*chip, c), me).wait_recv()
                fwd = copy(a, 4 + j, (*chip, c), sibling)
                fwd.start()
                passed.append(fwd)
        for a in range(n):
            copy(a, 0, sibling, me).wait_recv()
        for j, chip in enumerate(chips):
            for a in range(n):
                copy(a, 4 + j, (*chip, 1 - c), me).wait_recv()
        for cp in first + passed:
            cp.wait_send()
        for cp in mine:
            cp.wait()

    return Comm(arrs, [jax.ShapeDtypeStruct((N_DEV,) + p.shape, p.dtype) for p in arrs],
                [pltpu.SemaphoreType.DMA((7, n)), pltpu.SemaphoreType.DMA((7, n)), pltpu.SemaphoreType.DMA((n,))],
                start, finish)


def comm_pair(sends):
    n = len(sends)

    def copies(ins, outs, sems):
        send_sems, recv_sems = sems
        x, y, c, _ = _place()
        return [pltpu.make_async_remote_copy(
            src_ref=ins[a].at[k, 1 - c], dst_ref=outs[a].at[k], send_sem=send_sems.at[k, a], recv_sem=recv_sems.at[k, a],
            device_id=(x, y, 1 - c), device_id_type=MESH) for k in range(4) for a in range(n)]

    def start(ins, outs, sems):
        for cp in copies(ins, outs, sems):
            cp.start()

    def finish(ins, outs, sems):
        for cp in copies(ins, outs, sems):
            cp.wait()

    return Comm(sends, [jax.ShapeDtypeStruct((4,) + s.shape[2:], s.dtype) for s in sends],
                [pltpu.SemaphoreType.DMA((4, n)), pltpu.SemaphoreType.DMA((4, n))], start, finish)


def pair_sum(send, got, out_dtype, tr, name):
    _, _, R, C = send.shape
    c = lax.axis_index("c").astype(jnp.int32).reshape(1)

    def body(c_ref, s_ref, g_ref, o_ref):
        o_ref[...] = (s_ref[...] + g_ref[...]).astype(out_dtype)

    return pl.pallas_call(
        body,
        grid_spec=pltpu.PrefetchScalarGridSpec(
            num_scalar_prefetch=1, grid=(4, R // tr),
            in_specs=[pl.BlockSpec((None, None, tr, C), lambda k, i, cr: (k, cr[0], i, 0)),
                      pl.BlockSpec((None, tr, C), lambda k, i, cr: (k, i, 0))],
            out_specs=pl.BlockSpec((None, tr, C), lambda k, i, cr: (k, i, 0))),
        out_shape=jax.ShapeDtypeStruct((4, R, C), out_dtype),
        compiler_params=_cp("parallel", "parallel"), name=name)(c, send, got)


def comm_chip(items, bufs):
    ns = len(items)
    segs = [(i, b, s0, nr, d0) for i, (_, b, ranges) in enumerate(items) for (s0, nr, d0) in ranges]

    def copies(ins, outs, sems):
        send_sems, recv_sems, local_sems = sems
        x, y, c, chips = _place()
        mychip = 2 * x + y
        mine = [pltpu.make_async_copy(ins[i].at[mychip, pl.ds(s0, nr)], outs[b].at[mychip, pl.ds(d0, nr)], local_sems.at[q])
                for q, (i, b, s0, nr, d0) in enumerate(segs)]
        remote = [pltpu.make_async_remote_copy(
            src_ref=ins[i].at[2 * px + py, pl.ds(s0, nr)], dst_ref=outs[b].at[mychip, pl.ds(d0, nr)],
            send_sem=send_sems.at[j, q], recv_sem=recv_sems.at[j, q], device_id=(px, py, c), device_id_type=MESH)
            for j, (px, py) in enumerate(chips) for q, (i, b, s0, nr, d0) in enumerate(segs)]
        return mine, remote

    def start(ins, outs, sems):
        mine, remote = copies(ins, outs, sems)
        for cp in mine + remote:
            cp.start()

    def finish(ins, outs, sems):
        mine, remote = copies(ins, outs, sems)
        for cp in remote + mine:
            cp.wait()

    q = len(segs)
    return Comm([it[0] for it in items] + list(bufs), [jax.ShapeDtypeStruct(b.shape, b.dtype) for b in bufs],
                [pltpu.SemaphoreType.DMA((3, q)), pltpu.SemaphoreType.DMA((3, q)), pltpu.SemaphoreType.DMA((q,))],
                start, finish, aliases={ns + b: b for b in range(len(bufs))})


def adamw(parts, row0, w, m, v, tr, name):
    C = w.shape[-1]
    rows = w.size // C
    off = row0 // tr

    def body(p_ref, w_ref, m_ref, v_ref, g_ref, d_ref, nm_ref, nv_ref):
        g = ((p_ref[0].astype(F32) + p_ref[1].astype(F32)) + p_ref[2].astype(F32)) + p_ref[3].astype(F32)
        m2 = ADAM_B1 * m_ref[...] + (1.0 - ADAM_B1) * g
        v2 = ADAM_B2 * v_ref[...] + (1.0 - ADAM_B2) * (g * g)
        mh = m2 / (1.0 - ADAM_B1 ** ADAM_STEP)
        vh = v2 / (1.0 - ADAM_B2 ** ADAM_STEP)
        g_ref[...] = g
        d_ref[...] = -ADAM_LR * (mh / (jnp.sqrt(vh) + ADAM_EPS) + ADAM_WD * w_ref[...])
        nm_ref[...] = m2
        nv_ref[...] = v2

    row = pl.BlockSpec((tr, C), lambda i: (i, 0))
    shp = jax.ShapeDtypeStruct((rows, C), F32)
    outs = pl.pallas_call(
        body, grid=(rows // tr,),
        in_specs=[pl.BlockSpec((4, tr, C), lambda i: (0, off + i, 0)), row, row, row],
        out_specs=[row, row, row, row], out_shape=[shp, shp, shp, shp],
        compiler_params=_cp("parallel"), name=name)(parts, w.reshape(rows, C), m.reshape(rows, C), v.reshape(rows, C))
    return [o.reshape(w.shape) for o in outs]


def _full_from_gathered(flat, shape, axis):
    t = jnp.moveaxis(flat.reshape((N_DEV,) + shape), 0, axis)
    return t.reshape(shape[:axis] + (N_DEV * shape[axis],) + shape[axis + 1:])


def _shards_of_full(full, shape, axis):
    t = full.reshape(shape[:axis] + (N_DEV, shape[axis]) + shape[axis + 1:])
    return jnp.moveaxis(t, axis, 0).reshape(N_DEV, -1)


def _small_pack(args, prefix):
    flat = jnp.concatenate([args[prefix + n].reshape(-1) for n, _, _ in SMALL] + [args[prefix + "attn_sinks"].reshape(-1)])
    return jnp.pad(flat, (0, SM_ROWS * 1024 - flat.shape[0])).reshape(SM_ROWS, 1024)


def _small_unpack(pack):
    flat = pack.reshape(-1)
    out, off = {}, 0
    for n, shape, _ in SMALL:
        size = math.prod(shape)
        out[n] = flat[off:off + size].reshape(shape)
        off += size
    out["attn_sinks"] = flat[SINK_OFF:SINK_OFF + 32].reshape(2, 16)
    return out


def _tn_tile(n):
    return next(t for t in (1408, 1024, 768, 512, 256, 128) if n % t == 0)


def _dw(a, b, name, planes=1, comm=None):
    return mm_tn(a, b, name, _tn_tile(a.shape[1]), _tn_tile(b.shape[-1]), planes=planes, comm=comm)


class _NoExchange:
    def __init__(self, layers):
        self.layers, self.grads = layers, [None] * DEPTH

    def weights(self, l):
        return self.layers[l]

    def fwd_slot(self, l, slot):
        return None

    def fwd_done(self, l, slot, outs):
        pass

    def loss_ready(self, loss):
        pass

    def bwd_slot(self, slot):
        return None

    def bwd_done(self, slot, outs):
        pass

    def grads_ready(self, l, g):
        self.grads[l] = g


def _local_step(x, target, sched, sinks):
    S = x.shape[0]
    tq = min(512, S)
    cosf, sinf = _rope_tables(S)
    saved = []
    h, hb = x, x.astype(BF16)
    for l in range(DEPTH):
        j = l // 2
        wl = sched.weights(l)
        sv = {"h0b": hb, "wl": wl}
        (sv["gu1"], a1), co = ffn_up(hb, wl["w_in1"], f"ffn1_up_{l}", comm=sched.fwd_slot(l, "a"))
        sched.fwd_done(l, "a", co)
        sv["a1"] = a1
        (sv["z1"], h, hb), co = mm_ln(a1, _rows_view(wl["r_ff"], FF_SHARD, 0), h, wl["ln_g"][0], wl["ln_b"][0], 0.5,
                                      f"ffn1_down_ln_{l}", comm=sched.fwd_slot(l, "b"))
        sched.fwd_done(l, "b", co)
        sv["h1b"] = hb
        if l % 2 == 0:
            qkv = mm_plain(hb, _mat_view(wl["m_c"]), F32, f"attn_qkv_{l}")
            qr, kr, vv = rope_fwd(qkv, cosf, sinf, f"rope_{l}")
            qh = qr.reshape(S, N_HEADS, HEAD_DIM).transpose(1, 0, 2)
            kh = kr.reshape(S, N_KV, HEAD_DIM).transpose(1, 0, 2)
            vh = vv.reshape(S, N_KV, HEAD_DIM).transpose(1, 0, 2)
            oh = attn_fwd(qh, kh, vh, sinks[j], f"attn_core_{l}", tq)
            mix_in = oh.transpose(1, 0, 2).reshape(S, N_HEADS * HEAD_DIM)
            sv.update(qh=qh, kh=kh, vh=vh)
        else:
            xg = mm_plain(hb, _mat_view(wl["m_c"]), F32, f"lru_in_{l}")
            mix_in, hstate = lru_fwd(xg, wl["lru"], f"lru_core_{l}")
            sv.update(xg=xg, hstate=hstate)
        sv["mix_in"] = mix_in
        (sv["z2"], h, hb), _ = mm_ln(mix_in, _rows_view(wl["m_o"], MIX_SHARD, 0), h, wl["ln_g"][1], wl["ln_b"][1], 1.0,
                                     f"mix_out_ln_{l}")
        sv["h2b"] = hb
        (sv["gu2"], a2), co = ffn_up(hb, wl["w_in2"], f"ffn2_up_{l}", comm=sched.fwd_slot(l, "c"))
        sched.fwd_done(l, "c", co)
        sv["a2"] = a2
        (sv["z3"], h, hb), _ = mm_ln(a2, _rows_view(wl["r_ff"], FF_SHARD, FF_SHARD), h, wl["ln_g"][2], wl["ln_b"][2], 0.5,
                                     f"ffn2_down_ln_{l}")
        saved.append(sv)

    dy, lvec = loss_head(h, target, "loss_head")
    loss = lvec[0, 0]
    sched.loss_ready(loss)

    def hosted(slot, on, fn):
        comm = sched.bwd_slot(slot) if on else None
        out, co = fn(comm)
        if comm is not None:
            sched.bwd_done(slot, co)
        return out

    def ffn_bwd(dy, z, g, gu, a, xin_b, r_ff, blk, w_in, tag, host):
        dz, dzb, dg, db = ln_bwd(dy, z, g, 0.5, f"ln_bwd_{tag}")
        dh = hosted("pair", host, lambda cm: ffn_mid_bwd(dzb, r_ff, blk, gu, f"ffn_mid_bwd_{tag}", comm=cm))
        d_wout, _ = _dw(a, dzb, f"dw_out_{tag}")
        d_win = hosted("chip_x", host, lambda cm: _dw(xin_b, dh, f"dw_in_{tag}", planes=2, comm=cm))
        dx = hosted("chip_y", host, lambda cm: mm_res2_nt(dh, _mat_view(w_in, D_FF, 0), _mat_view(w_in, D_FF, 1), dz, ALPHA,
                                                          f"ffn_dx_{tag}", comm=cm))
        return dx, dg, db, d_wout, d_win

    for l in reversed(range(DEPTH)):
        j = l // 2
        sv = saved[l]
        wl = sv["wl"]
        gl = {}
        dg, db = [None] * 3, [None] * 3
        dy, dg[2], db[2], gl["w_out2"], gl["w_in2"] = ffn_bwd(
            dy, sv["z3"], wl["ln_g"][2], sv["gu2"], sv["a2"], sv["h2b"], wl["r_ff"], 1, wl["w_in2"], f"2_{l}", True)
        dz, dzb, dg[1], db[1] = ln_bwd(dy, sv["z2"], wl["ln_g"][1], 1.0, f"ln_bwd_mix_{l}")
        w_mix = _rows_view(wl["m_o"], MIX_SHARD, 0)
        if l % 2 == 0:
            gl["mix_out"], _ = _dw(sv["mix_in"], dzb, f"dw_o_{l}")
            do = mm_plain(dzb, w_mix, BF16, f"attn_do_{l}", nt=True)
            doh = do.reshape(S, N_HEADS, HEAD_DIM).transpose(1, 0, 2)
            dq, dkc, dkp, dvc, dvp, dsk = attn_bwd(sv["qh"], sv["kh"], sv["vh"], doh, sinks[j], f"attn_core_bwd_{l}", tq)
            gl["sinks"] = dsk[:, :GROUP, 0].reshape(N_HEADS)
            tok = lambda t: t.transpose(1, 0, 2).reshape(t.shape[1], -1)
            dmid = rope_bwd(tok(dq), tok(dkc), tok(dkp), tok(dvc), tok(dvp), cosf, sinf, f"rope_bwd_{l}", tq)
            gl["mix_in"], _ = _dw(sv["h1b"], dmid, f"dw_qkv_{l}")
        else:
            gl["mix_out"], _ = _dw(sv["mix_in"], dzb, f"dw_lru_out_{l}")
            dyl = mm_plain(dzb, w_mix, F32, f"lru_dy_{l}", nt=True)
            dmid, dcw, dcb, dwra, dwrx, dbra, dbrx, dlam = lru_bwd(dyl, sv["xg"], sv["hstate"], wl["lru"], f"lru_core_bwd_{l}")
            gl.update(conv_w=dcw, conv_b=dcb[0], w_ra=dwra, w_rx=dwrx, b_ra=dbra[0], b_rx=dbrx[0],
                      lam=dlam[0] * wl["lru"]["sig_neg"])
            gl["mix_in"], _ = _dw(sv["h1b"], dmid, f"dw_lru_in_{l}")
        dy = mm_res_nt(dmid, _mat_view(wl["m_c"]), dz, ALPHA, f"mix_dx_{l}")
        dy, dg[0], db[0], gl["w_out1"], gl["w_in1"] = ffn_bwd(
            dy, sv["z1"], wl["ln_g"][0], sv["gu1"], sv["a1"], sv["h0b"], wl["r_ff"], 0, wl["w_in1"], f"1_{l}", False)
        gl["ln_g"], gl["ln_b"] = jnp.concatenate(dg, axis=0), jnp.concatenate(db, axis=0)
        sched.grads_ready(l, gl)
    return loss, dy


def _lru_params(full, j, w_ra, w_rx):
    lam = full["lru_lambda"][j]
    return {
        "conv_w": full["lru_conv_w"][j], "conv_b": full["lru_conv_b"][j].reshape(1, -1),
        "w_ra": w_ra, "w_rx": w_rx,
        "b_ra": full["lru_b_ra"][j].reshape(1, -1), "b_rx": full["lru_b_rx"][j].reshape(1, -1),
        "lsl": jax.nn.log_sigmoid(lam).reshape(1, -1), "sig_neg": jax.nn.sigmoid(-lam),
    }


def _col_shards(f):
    return f.reshape(f.shape[0], N_DEV, -1).transpose(1, 0, 2)


def _row_shards(f):
    return f.reshape(N_DEV, -1, f.shape[1])


def _gate_shards(w):
    return w.reshape(RNN_BLOCKS, N_DEV, RNN_W // N_DEV, RNN_W).transpose(1, 0, 2, 3).reshape(N_DEV, -1, RNN_W)


class _Fsdp:
    def __init__(self, args):
        self.args = args
        self.b16 = lambda a: a.astype(BF16)
        self.sm = _small_pack(args, "")
        self.sent = {l: self._ag_arrays(l) for l in range(DEPTH)}
        first = self.sent[0]
        got = run_comm(comm_ag(first["a"] + first["b"] + first["c"] + [self.sm]), "all_gather_layer0")
        na, nb = len(first["a"]), len(first["b"])
        self.raw = {0: {"a": got[:na], "b": got[na:na + nb], "c": got[na + nb:na + nb + 1]}}
        gflat = got[-1].reshape(N_DEV, SM_ROWS * 1024)
        self.small, off = {}, 0
        for n, shape, axis in SMALL:
            size = math.prod(shape)
            self.small[n] = _full_from_gathered(gflat[:, off:off + size], shape, axis)
            off += size
        self.bufs = {"r": jnp.zeros((4, R_ROWS, D_MODEL), BF16), "c704": jnp.zeros((4, 2 * DEPTH * D_MODEL, 704), BF16),
                     "c192": jnp.zeros((4, 2 * D_MODEL, 192), BF16), "c256": jnp.zeros((4, C256_ROWS, RNN_W), BF16),
                     "sm": jnp.zeros((4, SM_ROWS, 1024), F32)}
        self.pending = None
        self.small_g = [None] * DEPTH
        self.loss = None

    def _ag_arrays(self, l):
        a, b16, j = self.args, self.b16, l // 2
        r_ff = jnp.concatenate([b16(a["ffn1_w_out"][l]), b16(a["ffn2_w_out"][l])], axis=0)
        if l % 2 == 0:
            return {"a": [b16(a["ffn1_w_in"][l]), b16(a["attn_w_qkv"][j])], "b": [r_ff, b16(a["attn_w_o"][j])],
                    "c": [b16(a["ffn2_w_in"][l])]}
        gates = jnp.concatenate([b16(a["lru_w_ra"][j]).reshape(-1, RNN_W), b16(a["lru_w_rx"][j]).reshape(-1, RNN_W)], axis=0)
        return {"a": [b16(a["ffn1_w_in"][l]), b16(a["lru_w_in"][j])], "b": [r_ff, b16(a["lru_w_out"][j]), gates],
                "c": [b16(a["ffn2_w_in"][l])]}

    def weights(self, l):
        raw, j = self.raw[l], l // 2
        full = lambda g: g.transpose(1, 0, 2).reshape(g.shape[1], -1)
        wl = {"w_in1": full(raw["a"][0]), "m_c": full(raw["a"][1]), "w_in2": full(raw["c"][0]),
              "r_ff": raw["b"][0], "m_o": raw["b"][1], "ln_g": self.small["ln_g"][l], "ln_b": self.small["ln_b"][l]}
        if l % 2:
            g = raw["b"][2].reshape(N_DEV, 2, RNN_BLOCKS, RNN_W // N_DEV, RNN_W).transpose(1, 2, 0, 3, 4)
            g = g.reshape(2, RNN_BLOCKS, RNN_W, RNN_W)
            wl["lru"] = _lru_params(self.small, j, g[0], g[1])
        return wl

    def fwd_slot(self, l, slot):
        return comm_ag(self.sent[l + 1][slot]) if l + 1 < DEPTH else None

    def fwd_done(self, l, slot, outs):
        if outs:
            self.raw.setdefault(l + 1, {})[slot] = outs

    def loss_ready(self, loss):
        self.loss = loss

    def _ranges(self, l):
        j = l // 2
        r = [(0, FF_SHARD, l * FF_SHARD), (FF_SHARD, FF_SHARD, R_FFN2 + l * FF_SHARD),
             (2 * FF_SHARD, MIX_SHARD, (R_LRU if l % 2 else R_ATTN) + j * MIX_SHARD)]
        c704 = [(0, D_MODEL, l * D_MODEL), (D_MODEL, D_MODEL, (DEPTH + l) * D_MODEL)]
        if l % 2:
            gr = RNN_BLOCKS * (RNN_W // N_DEV)
            mix = ("c256", [(0, D_MODEL, j * D_MODEL), (D_MODEL, gr, C256_RA + j * gr), (D_MODEL + gr, gr, C256_RX + j * gr)])
        else:
            mix = ("c192", [(0, D_MODEL, j * D_MODEL)])
        return r, c704, mix

    def grads_ready(self, l, g):
        self.small_g[l] = g
        t1024 = jnp.concatenate([_row_shards(g["w_out1"]), _row_shards(g["w_out2"]), _row_shards(g["mix_out"])], axis=1)
        t704 = jnp.concatenate([_col_shards(g["w_in1"]), _col_shards(g["w_in2"])], axis=1)
        tmix = _col_shards(g["mix_in"])
        if l % 2:
            tmix = jnp.concatenate([tmix, _gate_shards(g["w_ra"]), _gate_shards(g["w_rx"])], axis=1)
        sends = [t.reshape((4, 2) + t.shape[1:]) for t in (t1024, t704, tmix)]
        if l > 0:
            self.pending = {"l": l, "sends": sends}
        else:
            self._last(sends)

    def _pair_sums(self, l, sends, gots):
        tiles = (416, 512, 256)
        return [pair_sum(s, g, BF16, t, f"pair_sum_{nm}_{l}")
                for s, g, t, nm in zip(sends, gots, tiles, ("r", "c704", "mix"))]

    def bwd_slot(self, slot):
        p = self.pending
        if p is None:
            return None
        r, c704, (mixbuf, mix) = self._ranges(p["l"])
        if slot == "pair":
            return comm_pair(p["sends"])
        if slot == "chip_x":
            return comm_chip([(p["parts"][1], 0, c704)], [self.bufs["c704"]])
        return comm_chip([(p["parts"][0], 0, r), (p["parts"][2], 1, mix)], [self.bufs["r"], self.bufs[mixbuf]])

    def bwd_done(self, slot, outs):
        p = self.pending
        if slot == "pair":
            p["parts"] = self._pair_sums(p["l"], p["sends"], outs)
        elif slot == "chip_x":
            self.bufs["c704"] = outs[0]
        else:
            self.bufs["r"], self.bufs[self._ranges(p["l"])[2][0]] = outs
            self.pending = None

    def _last(self, sends):
        sg = self.small_g
        stack = lambda key, ls: jnp.stack([sg[l][key] for l in ls])
        every, lru = range(DEPTH), (1, 3)
        small = {"ln_g": stack("ln_g", every), "ln_b": stack("ln_b", every), "lru_conv_w": stack("conv_w", lru),
                 "lru_conv_b": stack("conv_b", lru), "lru_b_ra": stack("b_ra", lru), "lru_b_rx": stack("b_rx", lru),
                 "lru_lambda": stack("lam", lru)}
        tail = jnp.concatenate([stack("sinks", (0, 2)).reshape(-1), self.loss.reshape(1)])
        tail = jnp.pad(tail, (0, SM_ROWS * 1024 - SMALL_N - tail.shape[0]))
        s_sm = jnp.concatenate([_shards_of_full(small[n], shape, axis) for n, shape, axis in SMALL]
                               + [jnp.broadcast_to(tail, (N_DEV, tail.shape[0]))], axis=1).reshape(4, 2, SM_ROWS, 1024)
        gots = run_comm(comm_pair(sends + [s_sm]), "pair_exchange_layer0")
        parts = self._pair_sums(0, sends, gots[:3]) + [pair_sum(s_sm, gots[3], F32, SM_ROWS, "pair_sum_sm")]
        r, c704, (mixbuf, mix) = self._ranges(0)
        names = ["r", "c704", mixbuf, "sm"]
        outs = run_comm(comm_chip([(parts[0], 0, r), (parts[1], 1, c704), (parts[2], 2, mix), (parts[3], 3, [(0, SM_ROWS, 0)])],
                                  [self.bufs[n] for n in names]), "chip_exchange_layer0")
        for n, o in zip(names, outs):
            self.bufs[n] = o


def kernel(x, ffn1_w_in, ffn1_w_out, ffn2_w_in, ffn2_w_out, ln_g, ln_b, attn_w_qkv, attn_sinks, attn_w_o, lru_w_in, lru_conv_w, lru_conv_b, lru_w_ra, lru_b_ra, lru_w_rx, lru_b_rx, lru_lambda, lru_w_out, loss_target, m_ffn1_w_in, m_ffn1_w_out, m_ffn2_w_in, m_ffn2_w_out, m_ln_g, m_ln_b, m_attn_w_qkv, m_attn_sinks, m_attn_w_o, m_lru_w_in, m_lru_conv_w, m_lru_conv_b, m_lru_w_ra, m_lru_b_ra, m_lru_w_rx, m_lru_b_rx, m_lru_lambda, m_lru_w_out, v_ffn1_w_in, v_ffn1_w_out, v_ffn2_w_in, v_ffn2_w_out, v_ln_g, v_ln_b, v_attn_w_qkv, v_attn_sinks, v_attn_w_o, v_lru_w_in, v_lru_conv_w, v_lru_conv_b, v_lru_w_ra, v_lru_b_ra, v_lru_w_rx, v_lru_b_rx, v_lru_lambda, v_lru_w_out):
    args = dict(locals())

    sched = _Fsdp(args)
    _, dx = _local_step(x[0], loss_target[0], sched, attn_sinks)
    p_r, p704, p192, p256, p_sm = (sched.bufs[n] for n in ("r", "c704", "c192", "c256", "sm"))
    sm = sched.sm

    fam = lambda n: (args[n], args["m_" + n], args["v_" + n])
    res = {
        "ffn1_w_out": adamw(p_r, 0, *fam("ffn1_w_out"), FF_SHARD, "adamw_ffn1_w_out"),
        "ffn2_w_out": adamw(p_r, R_FFN2, *fam("ffn2_w_out"), FF_SHARD, "adamw_ffn2_w_out"),
        "attn_w_o": adamw(p_r, R_ATTN, *fam("attn_w_o"), 2 * MIX_SHARD, "adamw_attn_w_o"),
        "lru_w_out": adamw(p_r, R_LRU, *fam("lru_w_out"), 2 * MIX_SHARD, "adamw_lru_w_out"),
        "ffn1_w_in": adamw(p704, 0, *fam("ffn1_w_in"), 512, "adamw_ffn1_w_in"),
        "ffn2_w_in": adamw(p704, DEPTH * D_MODEL, *fam("ffn2_w_in"), 512, "adamw_ffn2_w_in"),
        "attn_w_qkv": adamw(p192, 0, *fam("attn_w_qkv"), 512, "adamw_attn_w_qkv"),
        "lru_w_in": adamw(p256, 0, *fam("lru_w_in"), 512, "adamw_lru_w_in"),
        "lru_w_ra": adamw(p256, C256_RA, *fam("lru_w_ra"), 256, "adamw_lru_w_ra"),
        "lru_w_rx": adamw(p256, C256_RX, *fam("lru_w_rx"), 256, "adamw_lru_w_rx"),
    }
    sm_out = adamw(p_sm, 0, sm, _small_pack(args, "m_"), _small_pack(args, "v_"), SM_ROWS, "adamw_small")
    for k, pack in enumerate(sm_out):
        for n, val in _small_unpack(pack).items():
            res.setdefault(n, [None] * 4)[k] = val
    loss_total = sm_out[0].reshape(-1)[LOSS_OFF]
    out = [loss_total, dx[None]]
    for k in range(4):
        out += [res[n][k] for n in WEIGHTS]
    return tuple(out)
```

```python
import math

import jax
import jax.numpy as jnp
import numpy as np
from jax import lax
from jax.experimental import pallas as pl
from jax.experimental.pallas import tpu as pltpu

F32 = jnp.float32
BF16 = jnp.bfloat16

D_MODEL = 1024
DEPTH = 4
N_HEADS = 16
N_KV = 4
HEAD_DIM = 64
GROUP = 4
BLOCK = 128
ROPE_THETA = 10000.0
D_RNN = 1024
RNN_BLOCKS = 4
RNN_W = 256
CONV_W = 4
LRU_C = 8.0
D_FF = 2816
ALPHA = (2.0 * DEPTH) ** 0.25
LN_EPS = 1e-5
QKV = (N_HEADS + 2 * N_KV) * HEAD_DIM
N_DEV = 8

ADAM_LR = 0.001
ADAM_B1 = 0.9
ADAM_B2 = 0.999
ADAM_EPS = 1e-08
ADAM_WD = 0.01
ADAM_STEP = 10

VMEM_LIMIT = 52 * 1024 * 1024
NEG = float(np.finfo(np.float32).min)

MESH = pl.DeviceIdType.MESH
ANY = pl.BlockSpec(memory_space=pl.ANY)

FF_SHARD = D_FF // N_DEV
MIX_SHARD = D_MODEL // N_DEV
R_FFN2 = DEPTH * FF_SHARD
R_ATTN = 2 * DEPTH * FF_SHARD
R_LRU = R_ATTN + 2 * MIX_SHARD
R_ROWS = R_LRU + 2 * MIX_SHARD
C256_RA = 2 * D_MODEL
C256_RX = C256_RA + 2 * RNN_BLOCKS * (RNN_W // N_DEV)
C256_ROWS = C256_RX + 2 * RNN_BLOCKS * (RNN_W // N_DEV)

SMALL = (
    ("ln_g", (4, 3, 128), 2),
    ("ln_b", (4, 3, 128), 2),
    ("lru_conv_w", (2, 4, 128), 2),
    ("lru_conv_b", (2, 128), 1),
    ("lru_b_ra", (2, 128), 1),
    ("lru_b_rx", (2, 128), 1),
    ("lru_lambda", (2, 128), 1),
)
WEIGHTS = ("ffn1_w_in", "ffn1_w_out", "ffn2_w_in", "ffn2_w_out", "ln_g", "ln_b", "attn_w_qkv", "attn_sinks",
           "attn_w_o", "lru_w_in", "lru_conv_w", "lru_conv_b", "lru_w_ra", "lru_b_ra", "lru_w_rx", "lru_b_rx",
           "lru_lambda", "lru_w_out")
SMALL_N = sum(math.prod(s) for _, s, _ in SMALL)
SINK_OFF = SMALL_N
LOSS_OFF = SMALL_N + 32
SM_ROWS = 8


def _cp(*sem):
    return pltpu.CompilerParams(dimension_semantics=sem, vmem_limit_bytes=VMEM_LIMIT)


def _dot(a, b):
    return jnp.dot(a, b, preferred_element_type=F32)


def _dot_tn(a, b):
    return lax.dot_general(a, b, (((0,), (0,)), ((), ())), preferred_element_type=F32)


def _dot_nt(a, b):
    return lax.dot_general(a, b, (((1,), (1,)), ((), ())), preferred_element_type=F32)


def _ln(z, g, b):
    mu = jnp.mean(z, axis=-1, keepdims=True)
    xc = z - mu
    var = jnp.mean(xc * xc, axis=-1, keepdims=True)
    return xc * lax.rsqrt(var + LN_EPS) * g + b


def _rows_view(pack, rows_per_dev, row0):
    return (pack, (N_DEV, rows_per_dev, D_MODEL), (0, row0 // rows_per_dev, 0), (N_DEV * rows_per_dev, D_MODEL))


def _mat_view(arr, cols=None, cblk=0):
    k, n = arr.shape
    cols = n if cols is None else cols
    return (arr, (k, cols), (0, cblk), (k, cols))


def _vspec(view):
    _, bshape, bidx, _ = view
    return pl.BlockSpec(bshape, lambda *_: bidx)


def _vload(view, ref):
    return ref[...].reshape(view[3])


class Comm:
    def __init__(self, ins, out_shapes, sems, start, finish, aliases=None):
        self.ins, self.out_shapes, self.sems = list(ins), list(out_shapes), list(sems)
        self.start, self.finish, self.aliases = start, finish, dict(aliases or {})


def _call(body, *, grid, in_specs, out_specs, out_shape, operands, name, sem, scratch=(), comm=None):
    n_in, n_out, n_scr = len(in_specs), len(out_specs), len(scratch)
    if comm is None:
        return pl.pallas_call(body, grid=grid, in_specs=list(in_specs), out_specs=list(out_specs),
                              out_shape=list(out_shape), scratch_shapes=list(scratch), compiler_params=_cp(*sem),
                              name=name)(*operands), []
    nci, nco = len(comm.ins), len(comm.out_shapes)

    def hosted(*refs):
        ins, cins = refs[:n_in], refs[n_in:n_in + nci]
        o0 = n_in + nci
        outs, couts = refs[o0:o0 + n_out], refs[o0 + n_out:o0 + n_out + nco]
        s0 = o0 + n_out + nco
        scr, csems = refs[s0:s0 + n_scr], refs[s0 + n_scr:]
        first = last = None
        for ax, size in enumerate(grid):
            pid = pl.program_id(ax)
            f, e = pid == 0, pid == size - 1
            first = f if first is None else jnp.logical_and(first, f)
            last = e if last is None else jnp.logical_and(last, e)

        @pl.when(first)
        def _():
            comm.start(cins, couts, csems)

        body(*ins, *outs, *scr)

        @pl.when(last)
        def _():
            comm.finish(cins, couts, csems)

    res = pl.pallas_call(
        hosted, grid=grid, in_specs=list(in_specs) + [ANY] * nci, out_specs=list(out_specs) + [ANY] * nco,
        out_shape=list(out_shape) + comm.out_shapes, scratch_shapes=list(scratch) + comm.sems,
        input_output_aliases={n_in + i: n_out + o for i, o in comm.aliases.items()},
        compiler_params=_cp(*(("arbitrary",) * len(grid))), name=name)(*operands, *comm.ins)
    return res[:n_out], res[n_out:]


def run_comm(comm, name):
    nci, nco = len(comm.ins), len(comm.out_shapes)

    def body(*refs):
        cins, couts, csems = refs[:nci], refs[nci:nci + nco], refs[nci + nco:]
        comm.start(cins, couts, csems)
        comm.finish(cins, couts, csems)

    return pl.pallas_call(
        body, in_specs=[ANY] * nci, out_specs=[ANY] * nco, out_shape=comm.out_shapes, scratch_shapes=comm.sems,
        input_output_aliases=dict(comm.aliases), name=name)(*comm.ins)


def mm_plain(a, wv, out_dtype, name, nt=False, tm=512):
    S, K = a.shape
    N = wv[3][0] if nt else wv[3][1]
    tm = min(tm, S)
    dot = _dot_nt if nt else _dot

    def body(a_ref, w_ref, o_ref):
        o_ref[...] = dot(a_ref[...], _vload(wv, w_ref)).astype(out_dtype)

    return pl.pallas_call(
        body, grid=(S // tm,),
        in_specs=[pl.BlockSpec((tm, K), lambda i: (i, 0)), _vspec(wv)],
        out_specs=pl.BlockSpec((tm, N), lambda i: (i, 0)),
        out_shape=jax.ShapeDtypeStruct((S, N), out_dtype),
        compiler_params=_cp("parallel"), name=name)(a, wv[0])


def mm_res_nt(a, wv, r, alpha, name, tm=256):
    S, K = a.shape
    N = wv[3][0]
    tm = min(tm, S)

    def body(a_ref, w_ref, r_ref, o_ref):
        o_ref[...] = _dot_nt(a_ref[...], _vload(wv, w_ref)) + alpha * r_ref[...]

    return pl.pallas_call(
        body, grid=(S // tm,),
        in_specs=[pl.BlockSpec((tm, K), lambda i: (i, 0)), _vspec(wv), pl.BlockSpec((tm, N), lambda i: (i, 0))],
        out_specs=pl.BlockSpec((tm, N), lambda i: (i, 0)),
        out_shape=jax.ShapeDtypeStruct((S, N), F32),
        compiler_params=_cp("parallel"), name=name)(a, wv[0], r)


def mm_res2_nt(a3, wv0, wv1, r, alpha, name, tm=256, comm=None):
    _, S, K = a3.shape
    N = wv0[3][0]
    tm = min(tm, S)

    def body(a_ref, w0_ref, w1_ref, r_ref, o_ref):
        o_ref[...] = (_dot_nt(a_ref[0], _vload(wv0, w0_ref)) + _dot_nt(a_ref[1], _vload(wv1, w1_ref))
                      + alpha * r_ref[...])

    (out,), couts = _call(
        body, grid=(S // tm,),
        in_specs=[pl.BlockSpec((2, tm, K), lambda i: (0, i, 0)), _vspec(wv0), _vspec(wv1),
                  pl.BlockSpec((tm, N), lambda i: (i, 0))],
        out_specs=[pl.BlockSpec((tm, N), lambda i: (i, 0))],
        out_shape=[jax.ShapeDtypeStruct((S, N), F32)],
        operands=(a3, wv0[0], wv1[0], r), name=name, sem=("parallel",), comm=comm)
    return out, couts


def mm_ln(a, wv, h, g, b, scale, name, tm=256, comm=None):
    S, K = a.shape
    tm = min(tm, S)

    def body(a_ref, w_ref, h_ref, g_ref, b_ref, z_ref, y_ref, yb_ref):
        z = ALPHA * h_ref[...] + scale * _dot(a_ref[...], _vload(wv, w_ref))
        y = _ln(z, g_ref[...], b_ref[...])
        z_ref[...] = z
        y_ref[...] = y
        yb_ref[...] = y.astype(BF16)

    row = pl.BlockSpec((tm, D_MODEL), lambda i: (i, 0))
    vec = pl.BlockSpec((1, D_MODEL), lambda i: (0, 0))
    return _call(
        body, grid=(S // tm,),
        in_specs=[pl.BlockSpec((tm, K), lambda i: (i, 0)), _vspec(wv), row, vec, vec],
        out_specs=[row, row, row],
        out_shape=[jax.ShapeDtypeStruct((S, D_MODEL), F32), jax.ShapeDtypeStruct((S, D_MODEL), F32),
                   jax.ShapeDtypeStruct((S, D_MODEL), BF16)],
        operands=(a, wv[0], h, g.reshape(1, -1), b.reshape(1, -1)), name=name, sem=("parallel",), comm=comm)


def ffn_up(xb, w_in, name, tm=512, tn=1408, comm=None):
    S = xb.shape[0]
    tm = min(tm, S)
    nj = D_FF // tn

    def body(x_ref, wg_ref, wu_ref, gu_ref, a_ref):
        x = x_ref[...]
        g = _dot(x, wg_ref[...])
        u = _dot(x, wu_ref[...])
        gu_ref[0] = g.astype(BF16)
        gu_ref[1] = u.astype(BF16)
        a_ref[...] = (g * jax.nn.sigmoid(g) * u).astype(BF16)

    return _call(
        body, grid=(nj, S // tm),
        in_specs=[pl.BlockSpec((tm, D_MODEL), lambda j, i: (i, 0)),
                  pl.BlockSpec((D_MODEL, tn), lambda j, i: (0, j)),
                  pl.BlockSpec((D_MODEL, tn), lambda j, i: (0, nj + j))],
        out_specs=[pl.BlockSpec((2, tm, tn), lambda j, i: (0, i, j)), pl.BlockSpec((tm, tn), lambda j, i: (i, j))],
        out_shape=[jax.ShapeDtypeStruct((2, S, D_FF), BF16), jax.ShapeDtypeStruct((S, D_FF), BF16)],
        operands=(xb, w_in, w_in), name=name, sem=("parallel", "parallel"), comm=comm)


def ffn_mid_bwd(dfb, r_ff, blk, gu, name, tm=512, comm=None):
    S = dfb.shape[0]
    tm = min(tm, S)
    tn = 4 * FF_SHARD

    def body(df_ref, w_ref, gu_ref, dh_ref):
        da = _dot_nt(df_ref[...], w_ref[...].reshape(tn, D_MODEL))
        g = gu_ref[0].astype(F32)
        u = gu_ref[1].astype(F32)
        sg = jax.nn.sigmoid(g)
        dh_ref[0] = (da * u * (sg * (1.0 + g * (1.0 - sg)))).astype(BF16)
        dh_ref[1] = (da * (g * sg)).astype(BF16)

    gspec = pl.BlockSpec((2, tm, tn), lambda j, i: (0, i, j))
    (out,), couts = _call(
        body, grid=(2, S // tm),
        in_specs=[pl.BlockSpec((tm, D_MODEL), lambda j, i: (i, 0)),
                  pl.BlockSpec((4, FF_SHARD, D_MODEL), lambda j, i: (j, blk, 0)), gspec],
        out_specs=[gspec],
        out_shape=[jax.ShapeDtypeStruct((2, S, D_FF), BF16)],
        operands=(dfb, r_ff, gu), name=name, sem=("parallel", "parallel"), comm=comm)
    return out, couts


def mm_tn(a, b, name, tm, tn, ts=1024, planes=1, comm=None):
    S, M = a.shape
    N = b.shape[-1] * planes
    ts = min(ts, S)
    per = b.shape[-1] // tn

    def body(a_ref, b_ref, o_ref):
        @pl.when(pl.program_id(2) == 0)
        def _():
            o_ref[...] = jnp.zeros_like(o_ref)

        o_ref[...] += _dot_tn(a_ref[...], b_ref[...])

    if planes == 1:
        bspec = pl.BlockSpec((ts, tn), lambda i, j, s: (s, j))
    else:
        bspec = pl.BlockSpec((None, ts, tn), lambda i, j, s: (j // per, s, j % per))
    (out,), couts = _call(
        body, grid=(M // tm, N // tn, S // ts),
        in_specs=[pl.BlockSpec((ts, tm), lambda i, j, s: (s, i)), bspec],
        out_specs=[pl.BlockSpec((tm, tn), lambda i, j, s: (i, j))],
        out_shape=[jax.ShapeDtypeStruct((M, N), F32)],
        operands=(a, b), name=name, sem=("parallel", "parallel", "arbitrary"), comm=comm)
    return out, couts


def ln_bwd(dy, z, g, out_scale, name, tm=512):
    S = dy.shape[0]
    tm = min(tm, S)

    def body(dy_ref, z_ref, g_ref, dz_ref, dzb_ref, dg_ref, db_ref):
        @pl.when(pl.program_id(0) == 0)
        def _():
            dg_ref[...] = jnp.zeros_like(dg_ref)
            db_ref[...] = jnp.zeros_like(db_ref)

        z = z_ref[...]
        dy_ = dy_ref[...]
        mu = jnp.mean(z, axis=-1, keepdims=True)
        xc = z - mu
        var = jnp.mean(xc * xc, axis=-1, keepdims=True)
        rstd = lax.rsqrt(var + LN_EPS)
        xh = xc * rstd
        dxh = dy_ * g_ref[...]
        m1 = jnp.mean(dxh, axis=-1, keepdims=True)
        m2 = jnp.mean(dxh * xh, axis=-1, keepdims=True)
        dz = rstd * (dxh - m1 - xh * m2)
        dz_ref[...] = dz
        dzb_ref[...] = (out_scale * dz).astype(BF16)
        dg_ref[...] += jnp.sum(dy_ * xh, axis=0, keepdims=True)
        db_ref[...] += jnp.sum(dy_, axis=0, keepdims=True)

    row = pl.BlockSpec((tm, D_MODEL), lambda i: (i, 0))
    vec = pl.BlockSpec((1, D_MODEL), lambda i: (0, 0))
    return pl.pallas_call(
        body, grid=(S // tm,),
        in_specs=[row, row, vec],
        out_specs=[row, row, vec, vec],
        out_shape=[jax.ShapeDtypeStruct((S, D_MODEL), F32), jax.ShapeDtypeStruct((S, D_MODEL), BF16),
                   jax.ShapeDtypeStruct((1, D_MODEL), F32), jax.ShapeDtypeStruct((1, D_MODEL), F32)],
        compiler_params=_cp("arbitrary"), name=name)(dy, z, g.reshape(1, -1))


def loss_head(y, t, name, tm=512):
    S = y.shape[0]
    tm = min(tm, S)
    nt = S // tm

    def body(y_ref, t_ref, dy_ref, l_ref):
        i = pl.program_id(0)

        @pl.when(i == 0)
        def _():
            l_ref[...] = jnp.zeros_like(l_ref)

        e = y_ref[...] - t_ref[...]
        dy_ref[...] = e * (1.0 / D_MODEL)
        l_ref[...] += jnp.sum(e * e, axis=0, keepdims=True)

        @pl.when(i == nt - 1)
        def _():
            tot = jnp.sum(l_ref[...], axis=1, keepdims=True) * (0.5 / D_MODEL)
            l_ref[...] = jnp.broadcast_to(tot, l_ref.shape)

    row = pl.BlockSpec((tm, D_MODEL), lambda i: (i, 0))
    vec = pl.BlockSpec((1, D_MODEL), lambda i: (0, 0))
    return pl.pallas_call(
        body, grid=(nt,), in_specs=[row, row], out_specs=[row, vec],
        out_shape=[jax.ShapeDtypeStruct((S, D_MODEL), F32), jax.ShapeDtypeStruct((1, D_MODEL), F32)],
        compiler_params=_cp("arbitrary"), name=name)(y, t)


def _rope_tables(S):
    pos = jnp.arange(S, dtype=F32)
    inv_freq = ROPE_THETA ** (-jnp.arange(0, HEAD_DIM, 2, dtype=F32) / HEAD_DIM)
    ang = pos[:, None] * inv_freq[None, :]
    cos, sin = jnp.cos(ang), jnp.sin(ang)
    cosf = jnp.concatenate([cos, cos, cos, cos], axis=1)
    sinf = jnp.concatenate([-sin, sin, -sin, sin], axis=1)
    return cosf, sinf


def _rot(t, c, s, first):
    sw = jnp.where(first, pltpu.roll(t, 96, 1), pltpu.roll(t, 32, 1))
    return t * c + sw * s


def rope_fwd(qkv, cosf, sinf, name, tm=512):
    S = qkv.shape[0]
    tm = min(tm, S)

    def body(x_ref, c_ref, s_ref, q_ref, k_ref, v_ref):
        c = c_ref[...]
        s = s_ref[...]
        first = (lax.broadcasted_iota(jnp.int32, (tm, 128), 1) % HEAD_DIM) < (HEAD_DIM // 2)
        for j in range(8):
            q_ref[:, 128 * j:128 * (j + 1)] = _rot(x_ref[:, 128 * j:128 * (j + 1)], c, s, first).astype(BF16)
        for j in range(2):
            k_ref[:, 128 * j:128 * (j + 1)] = _rot(x_ref[:, 1024 + 128 * j:1024 + 128 * (j + 1)], c, s, first).astype(BF16)
        v_ref[...] = x_ref[:, 1280:1536].astype(BF16)

    tab = pl.BlockSpec((tm, 128), lambda i: (i, 0))
    return pl.pallas_call(
        body, grid=(S // tm,),
        in_specs=[pl.BlockSpec((tm, QKV), lambda i: (i, 0)), tab, tab],
        out_specs=[pl.BlockSpec((tm, 1024), lambda i: (i, 0)), pl.BlockSpec((tm, 256), lambda i: (i, 0)),
                   pl.BlockSpec((tm, 256), lambda i: (i, 0))],
        out_shape=[jax.ShapeDtypeStruct((S, 1024), BF16), jax.ShapeDtypeStruct((S, 256), BF16),
                   jax.ShapeDtypeStruct((S, 256), BF16)],
        compiler_params=_cp("parallel"), name=name)(qkv, cosf, sinf)


def rope_bwd(dq, dkc, dkp, dvc, dvp, cosf, sinf, name, tq):
    S = dq.shape[0]
    nt = S // tq

    def body(dq_ref, dkc_ref, dkp_ref, dvc_ref, dvp_ref, c_ref, s_ref, o_ref):
        i = pl.program_id(0)
        c = c_ref[...]
        s = -s_ref[...]
        first = (lax.broadcasted_iota(jnp.int32, (tq, 128), 1) % HEAD_DIM) < (HEAD_DIM // 2)
        for j in range(8):
            o_ref[:, 128 * j:128 * (j + 1)] = _rot(dq_ref[:, 128 * j:128 * (j + 1)], c, s, first).astype(BF16)
        has_next = i < nt - 1
        rows = lax.broadcasted_iota(jnp.int32, (tq, 256), 0)
        pad = jnp.zeros((tq - BLOCK, 256), F32)
        halo_k = jnp.concatenate([pad, dkp_ref[...]], axis=0)
        halo_v = jnp.concatenate([pad, dvp_ref[...]], axis=0)
        use = jnp.logical_and(has_next, rows >= tq - BLOCK)
        dk = dkc_ref[...] + jnp.where(use, halo_k, 0.0)
        dv = dvc_ref[...] + jnp.where(use, halo_v, 0.0)
        for j in range(2):
            o_ref[:, 1024 + 128 * j:1024 + 128 * (j + 1)] = _rot(dk[:, 128 * j:128 * (j + 1)], c, s, first).astype(BF16)
        o_ref[:, 1280:1536] = dv.astype(BF16)

    tab = pl.BlockSpec((tq, 128), lambda i: (i, 0))
    cur = pl.BlockSpec((tq, 256), lambda i: (i, 0))
    nxt = pl.BlockSpec((BLOCK, 256), lambda i: (jnp.minimum(i + 1, nt - 1), 0))
    return pl.pallas_call(
        body, grid=(nt,),
        in_specs=[pl.BlockSpec((tq, 1024), lambda i: (i, 0)), cur, nxt, cur, nxt, tab, tab],
        out_specs=pl.BlockSpec((tq, QKV), lambda i: (i, 0)),
        out_shape=jax.ShapeDtypeStruct((S, QKV), BF16),
        compiler_params=_cp("parallel"), name=name)(dq, dkc, dkp, dvc, dvp, cosf, sinf)


def _attn_masks(n):
    shape = (GROUP * BLOCK, 2 * BLOCK)
    r = lax.broadcasted_iota(jnp.int32, shape, 0) % BLOCK
    c = lax.broadcasted_iota(jnp.int32, shape, 1)
    ok = jnp.logical_and(c > r, c <= r + BLOCK)
    return jnp.logical_and(ok, jnp.logical_or(c >= BLOCK, n > 0)), ok


def _attn_probs(qs, kw, sink_col, ok):
    s = jnp.where(ok, _dot_nt(qs, kw) * (HEAD_DIM ** -0.5), NEG)
    m = jnp.maximum(jnp.max(s, axis=1, keepdims=True), sink_col)
    p = jnp.exp(s - m)
    es = jnp.exp(sink_col - m)
    inv = 1.0 / (jnp.sum(p, axis=1, keepdims=True) + es)
    return p * inv, es * inv


def _sink_col(sink_ref, g):
    rid = lax.broadcasted_iota(jnp.int32, (GROUP * BLOCK, 1), 0) // BLOCK
    col = jnp.zeros((GROUP * BLOCK, 1), F32)
    for j in range(GROUP):
        col = jnp.where(rid == j, sink_ref[GROUP * g + j], col)
    return col


def _kv_window(kc_ref, kp_ref, b):
    if b == 0:
        return jnp.concatenate([kp_ref[...], kc_ref[0:BLOCK, :]], axis=0)
    return kc_ref[BLOCK * (b - 1):BLOCK * (b + 1), :]


def _attn_specs(tq):
    nsub = tq // BLOCK
    qspec = pl.BlockSpec((GROUP, tq, HEAD_DIM), lambda g, n: (g, n, 0))
    cur = pl.BlockSpec((None, tq, HEAD_DIM), lambda g, n: (g, n, 0))
    prev = pl.BlockSpec((None, BLOCK, HEAD_DIM), lambda g, n: (g, jnp.maximum(n * nsub - 1, 0), 0))
    return qspec, cur, prev


def attn_fwd(q, k, v, sinks, name, tq):
    S = q.shape[1]
    nsub = tq // BLOCK

    def body(sink_ref, q_ref, kc_ref, kp_ref, vc_ref, vp_ref, o_ref):
        g = pl.program_id(0)
        n = pl.program_id(1)
        sink_col = _sink_col(sink_ref, g)
        ok_first, ok_rest = _attn_masks(n)
        for b in range(nsub):
            lo = BLOCK * b
            qs = jnp.concatenate([q_ref[j, lo:lo + BLOCK, :] for j in range(GROUP)], axis=0)
            pn, _ = _attn_probs(qs, _kv_window(kc_ref, kp_ref, b), sink_col, ok_rest if b else ok_first)
            o = _dot(pn.astype(BF16), _kv_window(vc_ref, vp_ref, b))
            for j in range(GROUP):
                o_ref[j, lo:lo + BLOCK, :] = o[BLOCK * j:BLOCK * (j + 1)].astype(BF16)

    qspec, cur, prev = _attn_specs(tq)
    return pl.pallas_call(
        body, grid=(N_KV, S // tq),
        in_specs=[pl.BlockSpec(memory_space=pltpu.SMEM), qspec, cur, prev, cur, prev],
        out_specs=qspec,
        out_shape=jax.ShapeDtypeStruct((N_HEADS, S, HEAD_DIM), BF16),
        compiler_params=_cp("parallel", "parallel"), name=name)(sinks, q, k, k, v, v)


def attn_bwd(q, k, v, do, sinks, name, tq):
    S = q.shape[1]
    nsub = tq // BLOCK
    nt = S // tq

    def body(sink_ref, q_ref, kc_ref, kp_ref, vc_ref, vp_ref, do_ref, dq_ref, dkc_ref, dkp_ref, dvc_ref, dvp_ref, ds_ref):
        g = pl.program_id(0)
        n = pl.program_id(1)

        @pl.when(n == 0)
        def _():
            ds_ref[...] = jnp.zeros_like(ds_ref)

        dkc_ref[...] = jnp.zeros_like(dkc_ref)
        dvc_ref[...] = jnp.zeros_like(dvc_ref)
        sink_col = _sink_col(sink_ref, g)
        rid = lax.broadcasted_iota(jnp.int32, (GROUP * BLOCK, 1), 0) // BLOCK
        sub = lax.broadcasted_iota(jnp.int32, (8, 128), 0)
        dsink = jnp.zeros((8, 128), F32)
        ok_first, ok_rest = _attn_masks(n)
        for b in range(nsub):
            lo = BLOCK * b
            qs = jnp.concatenate([q_ref[j, lo:lo + BLOCK, :] for j in range(GROUP)], axis=0)
            dos = jnp.concatenate([do_ref[j, lo:lo + BLOCK, :] for j in range(GROUP)], axis=0)
            kw = _kv_window(kc_ref, kp_ref, b)
            vw = _kv_window(vc_ref, vp_ref, b)
            pn, ps = _attn_probs(qs, kw, sink_col, ok_rest if b else ok_first)
            dp = _dot_nt(dos, vw)
            delta = jnp.sum(pn * dp, axis=1, keepdims=True)
            dsb = (pn * (dp - delta) * (HEAD_DIM ** -0.5)).astype(BF16)
            dq = _dot(dsb, kw)
            for j in range(GROUP):
                dq_ref[j, lo:lo + BLOCK, :] = dq[BLOCK * j:BLOCK * (j + 1)]
            dkw = _dot_tn(dsb, qs)
            dvw = _dot_tn(pn.astype(BF16), dos)
            if b == 0:
                dkp_ref[...] = dkw[0:BLOCK]
                dvp_ref[...] = dvw[0:BLOCK]
            else:
                dkc_ref[lo - BLOCK:lo, :] += dkw[0:BLOCK]
                dvc_ref[lo - BLOCK:lo, :] += dvw[0:BLOCK]
            dkc_ref[lo:lo + BLOCK, :] += dkw[BLOCK:2 * BLOCK]
            dvc_ref[lo:lo + BLOCK, :] += dvw[BLOCK:2 * BLOCK]
            sd = ps * delta
            for j in range(GROUP):
                tot = -jnp.sum(jnp.where(rid == j, sd, 0.0))
                dsink = dsink + jnp.where(sub == j, tot, 0.0)
        ds_ref[...] += dsink

    qspec, cur, prev = _attn_specs(tq)
    halo = pl.BlockSpec((None, BLOCK, HEAD_DIM), lambda g, n: (g, n, 0))
    return pl.pallas_call(
        body, grid=(N_KV, nt),
        in_specs=[pl.BlockSpec(memory_space=pltpu.SMEM), qspec, cur, prev, cur, prev, qspec],
        out_specs=[qspec, cur, halo, cur, halo, pl.BlockSpec((None, 8, 128), lambda g, n: (g, 0, 0))],
        out_shape=[jax.ShapeDtypeStruct((N_HEADS, S, HEAD_DIM), F32),
                   jax.ShapeDtypeStruct((N_KV, S, HEAD_DIM), F32), jax.ShapeDtypeStruct((N_KV, nt * BLOCK, HEAD_DIM), F32),
                   jax.ShapeDtypeStruct((N_KV, S, HEAD_DIM), F32), jax.ShapeDtypeStruct((N_KV, nt * BLOCK, HEAD_DIM), F32),
                   jax.ShapeDtypeStruct((N_KV, 8, 128), F32)],
        compiler_params=_cp("parallel", "arbitrary"), name=name)(sinks, q, k, k, v, v, do)


def _rows_before(cur, prev8, k):
    if k == 0:
        return cur
    n = cur.shape[0]
    ext = jnp.concatenate([prev8, cur], axis=0)
    return ext[8 - k:8 - k + n]


def _rows_after(cur, next8, k):
    if k == 0:
        return cur
    n = cur.shape[0]
    ext = jnp.concatenate([cur, next8], axis=0)
    return ext[k:k + n]


def _gelu(x):
    c = math.sqrt(2.0 / math.pi)
    t = jnp.tanh(c * (x + 0.044715 * (x * x * x)))
    return 0.5 * (1.0 + t), t


def _neg_expm1(u):
    ser = 1.0 + u * (1.0 / 6.0)
    for k in range(5, 1, -1):
        ser = 1.0 + (u * (1.0 / k)) * ser
    return jnp.where(u > -0.125, -(u * ser), 1.0 - jnp.exp(u))


def _block_diag(xb16, w_ref):
    return jnp.concatenate([_dot(xb16[:, RNN_W * n:RNN_W * (n + 1)], w_ref[n]) for n in range(RNN_BLOCKS)], axis=1)


def _lru_gates(xb, prev8, cw_ref, cb_ref, wra_ref, wrx_ref, bra_ref, brx_ref, lsl_ref):
    xc = cb_ref[...] + cw_ref[3:4, :] * xb
    for w in range(CONV_W - 1):
        xc = xc + cw_ref[w:w + 1, :] * _rows_before(xb, prev8, CONV_W - 1 - w)
    xcb = xc.astype(BF16)
    r = jax.nn.sigmoid(_block_diag(xcb, wra_ref) + bra_ref[...])
    ig = jax.nn.sigmoid(_block_diag(xcb, wrx_ref) + brx_ref[...])
    la = LRU_C * r * lsl_ref[...]
    a = jnp.exp(la)
    sq = jnp.sqrt(_neg_expm1(2.0 * la))
    return xc, xcb, r, ig, a, sq


def lru_fwd(xg, p, name, tm=256):
    S = xg.shape[0]
    tm = min(tm, S)

    def body(xg_ref, xp_ref, cw_ref, cb_ref, wra_ref, wrx_ref, bra_ref, brx_ref, lsl_ref, y_ref, h_ref, hc_ref, a_s, b_s):
        i = pl.program_id(0)
        xb = xg_ref[:, 0:D_RNN]
        gb = xg_ref[:, D_RNN:2 * D_RNN]
        prev8 = jnp.where(i > 0, xp_ref[:, 0:D_RNN], 0.0)
        xc, _, r, ig, a, sq = _lru_gates(xb, prev8, cw_ref, cb_ref, wra_ref, wrx_ref, bra_ref, brx_ref, lsl_ref)
        a_s[...] = a
        b_s[...] = sq * (ig * xc)

        @pl.when(i == 0)
        def _():
            hc_ref[...] = jnp.zeros_like(hc_ref)

        def chunk(c, h):
            o = pl.multiple_of(c * 8, 8)
            av = a_s[pl.ds(o, 8), :]
            bv = b_s[pl.ds(o, 8), :]
            rows = []
            for t in range(8):
                h = av[t:t + 1, :] * h + bv[t:t + 1, :]
                rows.append(h)
            h_ref[pl.ds(o, 8), :] = jnp.concatenate(rows, axis=0)
            return h

        h_last = lax.fori_loop(0, tm // 8, chunk, hc_ref[0:1, :])
        hc_ref[0:1, :] = h_last
        cdf, _ = _gelu(gb)
        y_ref[...] = (h_ref[...] * (gb * cdf)).astype(BF16)

    vec = pl.BlockSpec((1, D_RNN), lambda i: (0, 0))
    wsp = pl.BlockSpec((RNN_BLOCKS, RNN_W, RNN_W), lambda i: (0, 0, 0))
    return pl.pallas_call(
        body, grid=(S // tm,),
        in_specs=[pl.BlockSpec((tm, 2 * D_RNN), lambda i: (i, 0)),
                  pl.BlockSpec((8, 2 * D_RNN), lambda i: (jnp.maximum(i * (tm // 8) - 1, 0), 0)),
                  pl.BlockSpec((CONV_W, D_RNN), lambda i: (0, 0)), vec, wsp, wsp, vec, vec, vec],
        out_specs=[pl.BlockSpec((tm, D_RNN), lambda i: (i, 0)), pl.BlockSpec((tm, D_RNN), lambda i: (i, 0))],
        out_shape=[jax.ShapeDtypeStruct((S, D_RNN), BF16), jax.ShapeDtypeStruct((S, D_RNN), F32)],
        scratch_shapes=[pltpu.VMEM((8, D_RNN), F32), pltpu.VMEM((tm, D_RNN), F32), pltpu.VMEM((tm, D_RNN), F32)],
        compiler_params=_cp("arbitrary"), name=name)(
            xg, xg, p["conv_w"], p["conv_b"], p["w_ra"], p["w_rx"], p["b_ra"], p["b_rx"], p["lsl"])


def lru_bwd(dy, xg, h, p, name, tm=256):
    S = xg.shape[0]
    tm = min(tm, S)
    nt = S // tm

    def body(dy_ref, xg_ref, xp_ref, h_ref, hp_ref, cw_ref, cb_ref, wra_ref, wrx_ref, bra_ref, brx_ref,
             lsl_ref, dxg_ref, dcw_ref, dcb_ref, dwra_ref, dwrx_ref, dbra_ref, dbrx_ref, dlam_ref,
             lc_ref, nx_ref, a_s, g_s, l_s):
        i = pl.program_id(0)
        ti = nt - 1 - i

        @pl.when(i == 0)
        def _():
            lc_ref[...] = jnp.zeros_like(lc_ref)
            nx_ref[...] = jnp.zeros_like(nx_ref)
            for ref in (dcw_ref, dcb_ref, dwra_ref, dwrx_ref, dbra_ref, dbrx_ref, dlam_ref):
                ref[...] = jnp.zeros_like(ref)

        xb = xg_ref[:, 0:D_RNN]
        gb = xg_ref[:, D_RNN:2 * D_RNN]
        prev8 = jnp.where(ti > 0, xp_ref[:, 0:D_RNN], 0.0)
        xc, xcb, r, ig, a, sq = _lru_gates(xb, prev8, cw_ref, cb_ref, wra_ref, wrx_ref, bra_ref, brx_ref, lsl_ref)
        hh = h_ref[...]
        hprev = _rows_before(hh, jnp.where(ti > 0, hp_ref[...], 0.0), 1)
        dy_ = dy_ref[...]
        cdf, th = _gelu(gb)
        c0 = math.sqrt(2.0 / math.pi)
        dgate = cdf + gb * (0.5 * (1.0 - th * th) * c0 * (1.0 + 3.0 * 0.044715 * gb * gb))
        dgb = dy_ * hh * dgate
        a_s[...] = a
        g_s[...] = dy_ * (gb * cdf)

        def chunk(cc, carry):
            o = pl.multiple_of((tm // 8 - 1 - cc) * 8, 8)
            av = a_s[pl.ds(o, 8), :]
            gv = g_s[pl.ds(o, 8), :]
            rows = [None] * 8
            for t in range(7, -1, -1):
                lam_t = gv[t:t + 1, :] + carry
                rows[t] = lam_t
                carry = av[t:t + 1, :] * lam_t
            l_s[pl.ds(o, 8), :] = jnp.concatenate(rows, axis=0)
            return carry

        carry = lax.fori_loop(0, tm // 8, chunk, lc_ref[0:1, :])
        lc_ref[0:1, :] = carry
        lam = l_s[...]
        da = lam * hprev
        dixc = lam * sq
        di = dixc * xc
        dxc = dixc * ig
        dsq = lam * (ig * xc)
        dla = da * a - dsq * (a * a / sq)
        dr = dla * (LRU_C * lsl_ref[...])
        dlam_ref[...] += jnp.sum(dla * (LRU_C * r), axis=0, keepdims=True)
        dpr = dr * r * (1.0 - r)
        dpi = di * ig * (1.0 - ig)
        dbra_ref[...] += jnp.sum(dpr, axis=0, keepdims=True)
        dbrx_ref[...] += jnp.sum(dpi, axis=0, keepdims=True)
        dprb = dpr.astype(BF16)
        dpib = dpi.astype(BF16)
        back = []
        for n in range(RNN_BLOCKS):
            sl = slice(RNN_W * n, RNN_W * (n + 1))
            dwra_ref[n] += _dot_tn(xcb[:, sl], dprb[:, sl])
            dwrx_ref[n] += _dot_tn(xcb[:, sl], dpib[:, sl])
            back.append(_dot_nt(dprb[:, sl], wra_ref[n]) + _dot_nt(dpib[:, sl], wrx_ref[n]))
        dxc = dxc + jnp.concatenate(back, axis=1)
        dcb_ref[...] += jnp.sum(dxc, axis=0, keepdims=True)
        next8 = nx_ref[...]
        dxb = cw_ref[3:4, :] * dxc
        dcw_ref[3:4, :] += jnp.sum(dxc * xb, axis=0, keepdims=True)
        for w in range(CONV_W - 1):
            k = CONV_W - 1 - w
            dcw_ref[w:w + 1, :] += jnp.sum(dxc * _rows_before(xb, prev8, k), axis=0, keepdims=True)
            dxb = dxb + cw_ref[w:w + 1, :] * _rows_after(dxc, next8, k)
        nx_ref[...] = dxc[0:8, :]
        dxg_ref[:, 0:D_RNN] = dxb.astype(BF16)
        dxg_ref[:, D_RNN:2 * D_RNN] = dgb.astype(BF16)

    rev = lambda i: (nt - 1 - i, 0)
    before = lambda i: (jnp.maximum((nt - 1 - i) * (tm // 8) - 1, 0), 0)
    vec = pl.BlockSpec((1, D_RNN), lambda i: (0, 0))
    wsp = pl.BlockSpec((RNN_BLOCKS, RNN_W, RNN_W), lambda i: (0, 0, 0))
    cwsp = pl.BlockSpec((CONV_W, D_RNN), lambda i: (0, 0))
    return pl.pallas_call(
        body, grid=(nt,),
        in_specs=[pl.BlockSpec((tm, D_RNN), rev), pl.BlockSpec((tm, 2 * D_RNN), rev), pl.BlockSpec((8, 2 * D_RNN), before),
                  pl.BlockSpec((tm, D_RNN), rev), pl.BlockSpec((8, D_RNN), before),
                  cwsp, vec, wsp, wsp, vec, vec, vec],
        out_specs=[pl.BlockSpec((tm, 2 * D_RNN), rev), cwsp, vec, wsp, wsp, vec, vec, vec],
        out_shape=[jax.ShapeDtypeStruct((S, 2 * D_RNN), BF16), jax.ShapeDtypeStruct((CONV_W, D_RNN), F32),
                   jax.ShapeDtypeStruct((1, D_RNN), F32), jax.ShapeDtypeStruct((RNN_BLOCKS, RNN_W, RNN_W), F32),
                   jax.ShapeDtypeStruct((RNN_BLOCKS, RNN_W, RNN_W), F32), jax.ShapeDtypeStruct((1, D_RNN), F32),
                   jax.ShapeDtypeStruct((1, D_RNN), F32), jax.ShapeDtypeStruct((1, D_RNN), F32)],
        scratch_shapes=[pltpu.VMEM((8, D_RNN), F32), pltpu.VMEM((8, D_RNN), F32), pltpu.VMEM((tm, D_RNN), F32),
                        pltpu.VMEM((tm, D_RNN), F32), pltpu.VMEM((tm, D_RNN), F32)],
        compiler_params=_cp("arbitrary"), name=name)(
            dy, xg, xg, h, h, p["conv_w"], p["conv_b"], p["w_ra"], p["w_rx"], p["b_ra"], p["b_rx"], p["lsl"])


def _place():
    x, y, c = lax.axis_index("x"), lax.axis_index("y"), lax.axis_index("c")
    chips = [(1 - x, y), (x, 1 - y), (1 - x, 1 - y)]
    return x, y, c, chips


def comm_ag(arrs):
    n = len(arrs)

    def copies(ins, outs, sems):
        send_sems, recv_sems, local_sems = sems
        x, y, c, chips = _place()
        me, sibling = (x, y, c), (x, y, 1 - c)

        def rows(a, px, py, pc):
            return outs[a].at[4 * px + 2 * py + pc]

        def copy(a, k, block, to, own=False):
            return pltpu.make_async_remote_copy(
                src_ref=ins[a] if own else rows(a, *block), dst_ref=rows(a, *block),
                send_sem=send_sems.at[k, a], recv_sem=recv_sems.at[k, a], device_id=to, device_id_type=MESH)

        mine = [pltpu.make_async_copy(ins[a], rows(a, *me), local_sems.at[a]) for a in range(n)]
        first = [copy(a, 1 + j, me, (*chip, c), own=True) for j, chip in enumerate(chips) for a in range(n)]
        first += [copy(a, 0, me, sibling, own=True) for a in range(n)]
        return copy, chips, c, me, sibling, mine, first

    def start(ins, outs, sems):
        _, _, _, _, _, mine, first = copies(ins, outs, sems)
        for cp in mine + first:
            cp.start()

    def finish(ins, outs, sems):
        copy, chips, c, me, sibling, mine, first = copies(ins, outs, sems)
        passed = []
        for j, chip in enumerate(chips):
            for a in range(n):
                copy(a, 1 + j, (*chip, c), me).wait_recv()
                fwd = copy(a, 4 + j, (*chip, c), sibling)
                fwd.start()
                passed.append(fwd)
        for a in range(n):
            copy(a, 0, sibling, me).wait_recv()
        for j, chip in enumerate(chips):
            for a in range(n):
                copy(a, 4 + j, (*chip, 1 - c), me).wait_recv()
        for cp in first + passed:
            cp.wait_send()
        for cp in mine:
            cp.wait()

    return Comm(arrs, [jax.ShapeDtypeStruct((N_DEV,) + p.shape, p.dtype) for p in arrs],
                [pltpu.SemaphoreType.DMA((7, n)), pltpu.SemaphoreType.DMA((7, n)), pltpu.SemaphoreType.DMA((n,))],
                start, finish)


def comm_pair(sends):
    n = len(sends)

    def copies(ins, outs, sems):
        send_sems, recv_sems = sems
        x, y, c, _ = _place()
        return [pltpu.make_async_remote_copy(
            src_ref=ins[a].at[k, 1 - c], dst_ref=outs[a].at[k], send_sem=send_sems.at[k, a], recv_sem=recv_sems.at[k, a],
            device_id=(x, y, 1 - c), device_id_type=MESH) for k in range(4) for a in range(n)]

    def start(ins, outs, sems):
        for cp in copies(ins, outs, sems):
            cp.start()

    def finish(ins, outs, sems):
        for cp in copies(ins, outs, sems):
            cp.wait()

    return Comm(sends, [jax.ShapeDtypeStruct((4,) + s.shape[2:], s.dtype) for s in sends],
                [pltpu.SemaphoreType.DMA((4, n)), pltpu.SemaphoreType.DMA((4, n))], start, finish)


def pair_sum(send, got, out_dtype, tr, name):
    _, _, R, C = send.shape
    c = lax.axis_index("c").astype(jnp.int32).reshape(1)

    def body(c_ref, s_ref, g_ref, o_ref):
        o_ref[...] = (s_ref[...] + g_ref[...]).astype(out_dtype)

    return pl.pallas_call(
        body,
        grid_spec=pltpu.PrefetchScalarGridSpec(
            num_scalar_prefetch=1, grid=(4, R // tr),
            in_specs=[pl.BlockSpec((None, None, tr, C), lambda k, i, cr: (k, cr[0], i, 0)),
                      pl.BlockSpec((None, tr, C), lambda k, i, cr: (k, i, 0))],
            out_specs=pl.BlockSpec((None, tr, C), lambda k, i, cr: (k, i, 0))),
        out_shape=jax.ShapeDtypeStruct((4, R, C), out_dtype),
        compiler_params=_cp("parallel", "parallel"), name=name)(c, send, got)


def comm_chip(items, bufs):
    ns = len(items)
    segs = [(i, b, s0, nr, d0) for i, (_, b, ranges) in enumerate(items) for (s0, nr, d0) in ranges]

    def copies(ins, outs, sems):
        send_sems, recv_sems, local_sems = sems
        x, y, c, chips = _place()
        mychip = 2 * x + y
        mine = [pltpu.make_async_copy(ins[i].at[mychip, pl.ds(s0, nr)], outs[b].at[mychip, pl.ds(d0, nr)], local_sems.at[q])
                for q, (i, b, s0, nr, d0) in enumerate(segs)]
        remote = [pltpu.make_async_remote_copy(
            src_ref=ins[i].at[2 * px + py, pl.ds(s0, nr)], dst_ref=outs[b].at[mychip, pl.ds(d0, nr)],
            send_sem=send_sems.at[j, q], recv_sem=recv_sems.at[j, q], device_id=(px, py, c), device_id_type=MESH)
            for j, (px, py) in enumerate(chips) for q, (i, b, s0, nr, d0) in enumerate(segs)]
        return mine, remote

    def start(ins, outs, sems):
        mine, remote = copies(ins, outs, sems)
        for cp in mine + remote:
            cp.start()

    def finish(ins, outs, sems):
        mine, remote = copies(ins, outs, sems)
        for cp in remote + mine:
            cp.wait()

    q = len(segs)
    return Comm([it[0] for it in items] + list(bufs), [jax.ShapeDtypeStruct(b.shape, b.dtype) for b in bufs],
                [pltpu.SemaphoreType.DMA((3, q)), pltpu.SemaphoreType.DMA((3, q)), pltpu.SemaphoreType.DMA((q,))],
                start, finish, aliases={ns + b: b for b in range(len(bufs))})


def adamw(parts, row0, w, m, v, tr, name):
    C = w.shape[-1]
    rows = w.size // C
    off = row0 // tr

    def body(p_ref, w_ref, m_ref, v_ref, g_ref, d_ref, nm_ref, nv_ref):
        g = ((p_ref[0].astype(F32) + p_ref[1].astype(F32)) + p_ref[2].astype(F32)) + p_ref[3].astype(F32)
        m2 = ADAM_B1 * m_ref[...] + (1.0 - ADAM_B1) * g
        v2 = ADAM_B2 * v_ref[...] + (1.0 - ADAM_B2) * (g * g)
        mh = m2 / (1.0 - ADAM_B1 ** ADAM_STEP)
        vh = v2 / (1.0 - ADAM_B2 ** ADAM_STEP)
        g_ref[...] = g
        d_ref[...] = -ADAM_LR * (mh / (jnp.sqrt(vh) + ADAM_EPS) + ADAM_WD * w_ref[...])
        nm_ref[...] = m2
        nv_ref[...] = v2

    row = pl.BlockSpec((tr, C), lambda i: (i, 0))
    shp = jax.ShapeDtypeStruct((rows, C), F32)
    outs = pl.pallas_call(
        body, grid=(rows // tr,),
        in_specs=[pl.BlockSpec((4, tr, C), lambda i: (0, off + i, 0)), row, row, row],
        out_specs=[row, row, row, row], out_shape=[shp, shp, shp, shp],
        compiler_params=_cp("parallel"), name=name)(parts, w.reshape(rows, C), m.reshape(rows, C), v.reshape(rows, C))
    return [o.reshape(w.shape) for o in outs]


def _full_from_gathered(flat, shape, axis):
    t = jnp.moveaxis(flat.reshape((N_DEV,) + shape), 0, axis)
    return t.reshape(shape[:axis] + (N_DEV * shape[axis],) + shape[axis + 1:])


def _shards_of_full(full, shape, axis):
    t = full.reshape(shape[:axis] + (N_DEV, shape[axis]) + shape[axis + 1:])
    return jnp.moveaxis(t, axis, 0).reshape(N_DEV, -1)


def _small_pack(args, prefix):
    flat = jnp.concatenate([args[prefix + n].reshape(-1) for n, _, _ in SMALL] + [args[prefix + "attn_sinks"].reshape(-1)])
    return jnp.pad(flat, (0, SM_ROWS * 1024 - flat.shape[0])).reshape(SM_ROWS, 1024)


def _small_unpack(pack):
    flat = pack.reshape(-1)
    out, off = {}, 0
    for n, shape, _ in SMALL:
        size = math.prod(shape)
        out[n] = flat[off:off + size].reshape(shape)
        off += size
    out["attn_sinks"] = flat[SINK_OFF:SINK_OFF + 32].reshape(2, 16)
    return out


def _tn_tile(n):
    return next(t for t in (1408, 1024, 768, 512, 256, 128) if n % t == 0)


def _dw(a, b, name, planes=1, comm=None):
    return mm_tn(a, b, name, _tn_tile(a.shape[1]), _tn_tile(b.shape[-1]), planes=planes, comm=comm)


class _NoExchange:
    def __init__(self, layers):
        self.layers, self.grads = layers, [{} for _ in range(DEPTH)]

    def weights(self, l, piece):
        return self.layers[l]

    def fwd_slot(self):
        return None

    def fwd_done(self, outs):
        pass

    def loss_ready(self, loss):
        pass

    def bwd_slot(self, slot):
        return None

    def bwd_done(self, slot, outs):
        pass

    def grads_ready(self, l, part, g):
        self.grads[l].update(g)


def _local_step(x, target, sched, sinks):
    S = x.shape[0]
    tq = min(512, S)
    cosf, sinf = _rope_tables(S)
    saved = []
    h, hb = x, x.astype(BF16)
    for l in range(DEPTH):
        j = l // 2
        wl = {}
        sv = {"h0b": hb, "wl": wl}
        wl.update(sched.weights(l, "a"))
        (sv["gu1"], a1), co = ffn_up(hb, wl["w_in1"], f"ffn1_up_{l}", comm=sched.fwd_slot())
        sched.fwd_done(co)
        sv["a1"] = a1
        wl.update(sched.weights(l, "b"))
        (sv["z1"], h, hb), co = mm_ln(a1, _rows_view(wl["r_ff"], FF_SHARD, 0), h, wl["ln_g"][0], wl["ln_b"][0], 0.5,
                                      f"ffn1_down_ln_{l}", comm=sched.fwd_slot())
        sched.fwd_done(co)
        sv["h1b"] = hb
        if l % 2 == 0:
            qkv = mm_plain(hb, _mat_view(wl["m_c"]), F32, f"attn_qkv_{l}")
            qr, kr, vv = rope_fwd(qkv, cosf, sinf, f"rope_{l}")
            qh = qr.reshape(S, N_HEADS, HEAD_DIM).transpose(1, 0, 2)
            kh = kr.reshape(S, N_KV, HEAD_DIM).transpose(1, 0, 2)
            vh = vv.reshape(S, N_KV, HEAD_DIM).transpose(1, 0, 2)
            oh = attn_fwd(qh, kh, vh, sinks[j], f"attn_core_{l}", tq)
            mix_in = oh.transpose(1, 0, 2).reshape(S, N_HEADS * HEAD_DIM)
            sv.update(qh=qh, kh=kh, vh=vh)
        else:
            xg = mm_plain(hb, _mat_view(wl["m_c"]), F32, f"lru_in_{l}")
            mix_in, hstate = lru_fwd(xg, wl["lru"], f"lru_core_{l}")
            sv.update(xg=xg, hstate=hstate)
        sv["mix_in"] = mix_in
        (sv["z2"], h, hb), _ = mm_ln(mix_in, _rows_view(wl["m_o"], MIX_SHARD, 0), h, wl["ln_g"][1], wl["ln_b"][1], 1.0,
                                     f"mix_out_ln_{l}")
        sv["h2b"] = hb
        wl.update(sched.weights(l, "c"))
        (sv["gu2"], a2), co = ffn_up(hb, wl["w_in2"], f"ffn2_up_{l}", comm=sched.fwd_slot())
        sched.fwd_done(co)
        sv["a2"] = a2
        (sv["z3"], h, hb), co = mm_ln(a2, _rows_view(wl["r_ff"], FF_SHARD, FF_SHARD), h, wl["ln_g"][2], wl["ln_b"][2], 0.5,
                                      f"ffn2_down_ln_{l}", comm=sched.fwd_slot())
        sched.fwd_done(co)
        saved.append(sv)

    dy, lvec = loss_head(h, target, "loss_head")
    loss = lvec[0, 0]
    sched.loss_ready(loss)

    def hosted(slot, on, fn):
        comm = sched.bwd_slot(slot) if on else None
        out, co = fn(comm)
        if comm is not None:
            sched.bwd_done(slot, co)
        return out

    def ffn_bwd(dy, z, g, gu, a, xin_b, r_ff, blk, w_in, tag, host):
        dz, dzb, dg, db = ln_bwd(dy, z, g, 0.5, f"ln_bwd_{tag}")
        dh = hosted("pair", host, lambda cm: ffn_mid_bwd(dzb, r_ff, blk, gu, f"ffn_mid_bwd_{tag}", comm=cm))
        d_wout, _ = _dw(a, dzb, f"dw_out_{tag}")
        d_win = hosted("chip_x", host, lambda cm: _dw(xin_b, dh, f"dw_in_{tag}", planes=2, comm=cm))
        dx = hosted("chip_y", host, lambda cm: mm_res2_nt(dh, _mat_view(w_in, D_FF, 0), _mat_view(w_in, D_FF, 1), dz, ALPHA,
                                                          f"ffn_dx_{tag}", comm=cm))
        return dx, dg, db, d_wout, d_win

    for l in reversed(range(DEPTH)):
        j = l // 2
        sv = saved[l]
        wl = sv["wl"]
        gl = {}
        dg, db = [None] * 3, [None] * 3
        dy, dg[2], db[2], gl["w_out2"], gl["w_in2"] = ffn_bwd(
            dy, sv["z3"], wl["ln_g"][2], sv["gu2"], sv["a2"], sv["h2b"], wl["r_ff"], 1, wl["w_in2"], f"2_{l}", True)
        dz, dzb, dg[1], db[1] = ln_bwd(dy, sv["z2"], wl["ln_g"][1], 1.0, f"ln_bwd_mix_{l}")
        w_mix = _rows_view(wl["m_o"], MIX_SHARD, 0)
        if l % 2 == 0:
            gl["mix_out"], _ = _dw(sv["mix_in"], dzb, f"dw_o_{l}")
            do = mm_plain(dzb, w_mix, BF16, f"attn_do_{l}", nt=True)
            doh = do.reshape(S, N_HEADS, HEAD_DIM).transpose(1, 0, 2)
            dq, dkc, dkp, dvc, dvp, dsk = attn_bwd(sv["qh"], sv["kh"], sv["vh"], doh, sinks[j], f"attn_core_bwd_{l}", tq)
            gl["sinks"] = dsk[:, :GROUP, 0].reshape(N_HEADS)
            tok = lambda t: t.transpose(1, 0, 2).reshape(t.shape[1], -1)
            dmid = rope_bwd(tok(dq), tok(dkc), tok(dkp), tok(dvc), tok(dvp), cosf, sinf, f"rope_bwd_{l}", tq)
            gl["mix_in"], _ = _dw(sv["h1b"], dmid, f"dw_qkv_{l}")
        else:
            gl["mix_out"], _ = _dw(sv["mix_in"], dzb, f"dw_lru_out_{l}")
            dyl = mm_plain(dzb, w_mix, F32, f"lru_dy_{l}", nt=True)
            dmid, dcw, dcb, dwra, dwrx, dbra, dbrx, dlam = lru_bwd(dyl, sv["xg"], sv["hstate"], wl["lru"], f"lru_core_bwd_{l}")
            gl.update(conv_w=dcw, conv_b=dcb[0], w_ra=dwra, w_rx=dwrx, b_ra=dbra[0], b_rx=dbrx[0],
                      lam=dlam[0] * wl["lru"]["sig_neg"])
            gl["mix_in"], _ = _dw(sv["h1b"], dmid, f"dw_lru_in_{l}")
        dy = mm_res_nt(dmid, _mat_view(wl["m_c"]), dz, ALPHA, f"mix_dx_{l}")
        sched.grads_ready(l, "hi", gl)
        lo = {}
        dy, dg[0], db[0], lo["w_out1"], lo["w_in1"] = ffn_bwd(
            dy, sv["z1"], wl["ln_g"][0], sv["gu1"], sv["a1"], sv["h0b"], wl["r_ff"], 0, wl["w_in1"], f"1_{l}", True)
        lo["ln_g"], lo["ln_b"] = jnp.concatenate(dg, axis=0), jnp.concatenate(db, axis=0)
        sched.grads_ready(l, "lo", lo)
    return loss, dy


def _lru_params(full, j, w_ra, w_rx):
    lam = full["lru_lambda"][j]
    return {
        "conv_w": full["lru_conv_w"][j], "conv_b": full["lru_conv_b"][j].reshape(1, -1),
        "w_ra": w_ra, "w_rx": w_rx,
        "b_ra": full["lru_b_ra"][j].reshape(1, -1), "b_rx": full["lru_b_rx"][j].reshape(1, -1),
        "lsl": jax.nn.log_sigmoid(lam).reshape(1, -1), "sig_neg": jax.nn.sigmoid(-lam),
    }


def _col_shards(f):
    return f.reshape(f.shape[0], N_DEV, -1).transpose(1, 0, 2)


def _row_shards(f):
    return f.reshape(N_DEV, -1, f.shape[1])


def _gate_shards(w):
    return w.reshape(RNN_BLOCKS, N_DEV, RNN_W // N_DEV, RNN_W).transpose(1, 0, 2, 3).reshape(N_DEV, -1, RNN_W)


class _Fsdp:
    def __init__(self, args):
        self.args = args
        self.b16 = lambda a: a.astype(BF16)
        self.sm = _small_pack(args, "")
        self.sent = {l: self._ag_arrays(l) for l in range(DEPTH)}
        self.queue = [(l, p) for l in range(DEPTH) for p in "abc"][1:]
        self.hosting = None
        got = run_comm(comm_ag(self.sent[0]["a"] + [self.sm]), "all_gather_first")
        self.raw = {0: {"a": got[:-1]}}
        gflat = got[-1].reshape(N_DEV, SM_ROWS * 1024)
        self.small, off = {}, 0
        for n, shape, axis in SMALL:
            size = math.prod(shape)
            self.small[n] = _full_from_gathered(gflat[:, off:off + size], shape, axis)
            off += size
        self.bufs = {"r": jnp.zeros((4, R_ROWS, D_MODEL), BF16), "c704": jnp.zeros((4, 2 * DEPTH * D_MODEL, 704), BF16),
                     "c192": jnp.zeros((4, 2 * D_MODEL, 192), BF16), "c256": jnp.zeros((4, C256_ROWS, RNN_W), BF16),
                     "sm": jnp.zeros((4, SM_ROWS, 1024), F32)}
        self.pending = None
        self.small_g = [{} for _ in range(DEPTH)]
        self.loss = None

    def _ag_arrays(self, l):
        a, b16, j = self.args, self.b16, l // 2
        r_ff = jnp.concatenate([b16(a["ffn1_w_out"][l]), b16(a["ffn2_w_out"][l])], axis=0)
        if l % 2 == 0:
            return {"a": [b16(a["ffn1_w_in"][l]), b16(a["attn_w_qkv"][j])], "b": [r_ff, b16(a["attn_w_o"][j])],
                    "c": [b16(a["ffn2_w_in"][l])]}
        gates = jnp.concatenate([b16(a["lru_w_ra"][j]).reshape(-1, RNN_W), b16(a["lru_w_rx"][j]).reshape(-1, RNN_W)], axis=0)
        return {"a": [b16(a["ffn1_w_in"][l]), b16(a["lru_w_in"][j])], "b": [r_ff, b16(a["lru_w_out"][j]), gates],
                "c": [b16(a["ffn2_w_in"][l])]}

    def weights(self, l, piece):
        raw, j = self.raw[l][piece], l // 2
        full = lambda g: g.transpose(1, 0, 2).reshape(g.shape[1], -1)
        if piece == "a":
            return {"w_in1": full(raw[0]), "m_c": full(raw[1]), "ln_g": self.small["ln_g"][l], "ln_b": self.small["ln_b"][l]}
        if piece == "c":
            return {"w_in2": full(raw[0])}
        wl = {"r_ff": raw[0], "m_o": raw[1]}
        if l % 2:
            g = raw[2].reshape(N_DEV, 2, RNN_BLOCKS, RNN_W // N_DEV, RNN_W).transpose(1, 2, 0, 3, 4)
            g = g.reshape(2, RNN_BLOCKS, RNN_W, RNN_W)
            wl["lru"] = _lru_params(self.small, j, g[0], g[1])
        return wl

    def fwd_slot(self):
        if not self.queue:
            return None
        self.hosting = self.queue.pop(0)
        l, piece = self.hosting
        return comm_ag(self.sent[l][piece])

    def fwd_done(self, outs):
        if self.hosting is not None:
            l, piece = self.hosting
            self.raw.setdefault(l, {})[piece] = outs
            self.hosting = None

    def loss_ready(self, loss):
        self.loss = loss

    def _ranges(self, l, part):
        j = l // 2
        if part == "lo":
            return [("r", [(0, FF_SHARD, l * FF_SHARD)]), ("c704", [(0, D_MODEL, l * D_MODEL)])]
        r = [(0, FF_SHARD, R_FFN2 + l * FF_SHARD), (FF_SHARD, MIX_SHARD, (R_LRU if l % 2 else R_ATTN) + j * MIX_SHARD)]
        c704 = [(0, D_MODEL, (DEPTH + l) * D_MODEL)]
        if l % 2:
            gr = RNN_BLOCKS * (RNN_W // N_DEV)
            mix = ("c256", [(0, D_MODEL, j * D_MODEL), (D_MODEL, gr, C256_RA + j * gr), (D_MODEL + gr, gr, C256_RX + j * gr)])
        else:
            mix = ("c192", [(0, D_MODEL, j * D_MODEL)])
        return [("r", r), ("c704", c704), mix]

    def grads_ready(self, l, part, g):
        self.small_g[l].update(g)
        if part == "lo":
            ts = [_row_shards(g["w_out1"]), _col_shards(g["w_in1"])]
        else:
            tmix = _col_shards(g["mix_in"])
            if l % 2:
                tmix = jnp.concatenate([tmix, _gate_shards(g["w_ra"]), _gate_shards(g["w_rx"])], axis=1)
            ts = [jnp.concatenate([_row_shards(g["w_out2"]), _row_shards(g["mix_out"])], axis=1), _col_shards(g["w_in2"]), tmix]
        sends = [t.reshape((4, 2) + t.shape[1:]) for t in ts]
        if (l, part) == (0, "lo"):
            self._last(sends)
        else:
            self.pending = {"tag": f"{l}{part}", "sends": sends, "where": self._ranges(l, part)}

    def _pair_sums(self, tag, sends, gots):
        tiles = (None, 512, 256)
        return [pair_sum(s, g, BF16, t or s.shape[2] // 2, f"pair_sum_{nm}_{tag}")
                for s, g, t, nm in zip(sends, gots, tiles, ("r", "c704", "mix"))]

    def _chip(self, parts, where, pick):
        names = [where[i][0] for i in pick]
        return names, comm_chip([(parts[i], k, where[i][1]) for k, i in enumerate(pick)], [self.bufs[n] for n in names])

    def bwd_slot(self, slot):
        p = self.pending
        if p is None:
            return None
        if slot == "pair":
            return comm_pair(p["sends"])
        pick = [1] if slot == "chip_x" else [i for i in range(len(p["sends"])) if i != 1]
        p["names"], comm = self._chip(p["parts"], p["where"], pick)
        return comm

    def bwd_done(self, slot, outs):
        p = self.pending
        if slot == "pair":
            p["parts"] = self._pair_sums(p["tag"], p["sends"], outs)
            return
        for n, o in zip(p["names"], outs):
            self.bufs[n] = o
        if slot == "chip_y":
            self.pending = None

    def _last(self, sends):
        sg = self.small_g
        stack = lambda key, ls: jnp.stack([sg[l][key] for l in ls])
        every, lru = range(DEPTH), (1, 3)
        small = {"ln_g": stack("ln_g", every), "ln_b": stack("ln_b", every), "lru_conv_w": stack("conv_w", lru),
                 "lru_conv_b": stack("conv_b", lru), "lru_b_ra": stack("b_ra", lru), "lru_b_rx": stack("b_rx", lru),
                 "lru_lambda": stack("lam", lru)}
        tail = jnp.concatenate([stack("sinks", (0, 2)).reshape(-1), self.loss.reshape(1)])
        tail = jnp.pad(tail, (0, SM_ROWS * 1024 - SMALL_N - tail.shape[0]))
        s_sm = jnp.concatenate([_shards_of_full(small[n], shape, axis) for n, shape, axis in SMALL]
                               + [jnp.broadcast_to(tail, (N_DEV, tail.shape[0]))], axis=1).reshape(4, 2, SM_ROWS, 1024)
        gots = run_comm(comm_pair(sends + [s_sm]), "pair_exchange_last")
        parts = self._pair_sums("0lo", sends, gots[:2]) + [pair_sum(s_sm, gots[2], F32, SM_ROWS, "pair_sum_sm")]
        names, comm = self._chip(parts, self._ranges(0, "lo") + [("sm", [(0, SM_ROWS, 0)])], [0, 1, 2])
        for n, o in zip(names, run_comm(comm, "chip_exchange_last")):
            self.bufs[n] = o


def kernel(x, ffn1_w_in, ffn1_w_out, ffn2_w_in, ffn2_w_out, ln_g, ln_b, attn_w_qkv, attn_sinks, attn_w_o, lru_w_in, lru_conv_w, lru_conv_b, lru_w_ra, lru_b_ra, lru_w_rx, lru_b_rx, lru_lambda, lru_w_out, loss_target, m_ffn1_w_in, m_ffn1_w_out, m_ffn2_w_in, m_ffn2_w_out, m_ln_g, m_ln_b, m_attn_w_qkv, m_attn_sinks, m_attn_w_o, m_lru_w_in, m_lru_conv_w, m_lru_conv_b, m_lru_w_ra, m_lru_b_ra, m_lru_w_rx, m_lru_b_rx, m_lru_lambda, m_lru_w_out, v_ffn1_w_in, v_ffn1_w_out, v_ffn2_w_in, v_ffn2_w_out, v_ln_g, v_ln_b, v_attn_w_qkv, v_attn_sinks, v_attn_w_o, v_lru_w_in, v_lru_conv_w, v_lru_conv_b, v_lru_w_ra, v_lru_b_ra, v_lru_w_rx, v_lru_b_rx, v_lru_lambda, v_lru_w_out):
    args = dict(locals())

    sched = _Fsdp(args)
    _, dx = _local_step(x[0], loss_target[0], sched, attn_sinks)
    p_r, p704, p192, p256, p_sm = (sched.bufs[n] for n in ("r", "c704", "c192", "c256", "sm"))
    sm = sched.sm

    fam = lambda n: (args[n], args["m_" + n], args["v_" + n])
    res = {
        "ffn1_w_out": adamw(p_r, 0, *fam("ffn1_w_out"), FF_SHARD, "adamw_ffn1_w_out"),
        "ffn2_w_out": adamw(p_r, R_FFN2, *fam("ffn2_w_out"), FF_SHARD, "adamw_ffn2_w_out"),
        "attn_w_o": adamw(p_r, R_ATTN, *fam("attn_w_o"), 2 * MIX_SHARD, "adamw_attn_w_o"),
        "lru_w_out": adamw(p_r, R_LRU, *fam("lru_w_out"), 2 * MIX_SHARD, "adamw_lru_w_out"),
        "ffn1_w_in": adamw(p704, 0, *fam("ffn1_w_in"), 512, "adamw_ffn1_w_in"),
        "ffn2_w_in": adamw(p704, DEPTH * D_MODEL, *fam("ffn2_w_in"), 512, "adamw_ffn2_w_in"),
        "attn_w_qkv": adamw(p192, 0, *fam("attn_w_qkv"), 512, "adamw_attn_w_qkv"),
        "lru_w_in": adamw(p256, 0, *fam("lru_w_in"), 512, "adamw_lru_w_in"),
        "lru_w_ra": adamw(p256, C256_RA, *fam("lru_w_ra"), 256, "adamw_lru_w_ra"),
        "lru_w_rx": adamw(p256, C256_RX, *fam("lru_w_rx"), 256, "adamw_lru_w_rx"),
    }
    sm_out = adamw(p_sm, 0, sm, _small_pack(args, "m_"), _small_pack(args, "v_"), SM_ROWS, "adamw_small")
    for k, pack in enumerate(sm_out):
        for n, val in _small_unpack(pack).items():
            res.setdefault(n, [None] * 4)[k] = val
    loss_total = sm_out[0].reshape(-1)[LOSS_OFF]
    out = [loss_total, dx[None]]
    for k in range(4):
        out += [res[n][k] for n in WEIGHTS]
    return tuple(out)
```

```python
import math

import jax
import jax.numpy as jnp
import numpy as np
from jax import lax
from jax.experimental import pallas as pl
from jax.experimental.pallas import tpu as pltpu

F32 = jnp.float32
BF16 = jnp.bfloat16

D_MODEL = 1024
DEPTH = 4
N_HEADS = 16
N_KV = 4
HEAD_DIM = 64
GROUP = 4
BLOCK = 128
ROPE_THETA = 10000.0
D_RNN = 1024
RNN_BLOCKS = 4
RNN_W = 256
CONV_W = 4
LRU_C = 8.0
D_FF = 2816
ALPHA = (2.0 * DEPTH) ** 0.25
LN_EPS = 1e-5
QKV = (N_HEADS + 2 * N_KV) * HEAD_DIM
N_DEV = 8

ADAM_LR = 0.001
ADAM_B1 = 0.9
ADAM_B2 = 0.999
ADAM_EPS = 1e-08
ADAM_WD = 0.01
ADAM_STEP = 10

VMEM_LIMIT = 52 * 1024 * 1024
NEG = float(np.finfo(np.float32).min)

MESH = pl.DeviceIdType.MESH
ANY = pl.BlockSpec(memory_space=pl.ANY)

FF_SHARD = D_FF // N_DEV
MIX_SHARD = D_MODEL // N_DEV
R_FFN2 = DEPTH * FF_SHARD
R_ATTN = 2 * DEPTH * FF_SHARD
R_LRU = R_ATTN + 2 * MIX_SHARD
R_ROWS = R_LRU + 2 * MIX_SHARD
C256_RA = 2 * D_MODEL
C256_RX = C256_RA + 2 * RNN_BLOCKS * (RNN_W // N_DEV)
C256_ROWS = C256_RX + 2 * RNN_BLOCKS * (RNN_W // N_DEV)

SMALL = (
    ("ln_g", (4, 3, 128), 2),
    ("ln_b", (4, 3, 128), 2),
    ("lru_conv_w", (2, 4, 128), 2),
    ("lru_conv_b", (2, 128), 1),
    ("lru_b_ra", (2, 128), 1),
    ("lru_b_rx", (2, 128), 1),
    ("lru_lambda", (2, 128), 1),
)
WEIGHTS = ("ffn1_w_in", "ffn1_w_out", "ffn2_w_in", "ffn2_w_out", "ln_g", "ln_b", "attn_w_qkv", "attn_sinks",
           "attn_w_o", "lru_w_in", "lru_conv_w", "lru_conv_b", "lru_w_ra", "lru_b_ra", "lru_w_rx", "lru_b_rx",
           "lru_lambda", "lru_w_out")
SMALL_N = sum(math.prod(s) for _, s, _ in SMALL)
SINK_OFF = SMALL_N
LOSS_OFF = SMALL_N + 32
SM_ROWS = 8


def _cp(*sem):
    return pltpu.CompilerParams(dimension_semantics=sem, vmem_limit_bytes=VMEM_LIMIT)


def _dot(a, b):
    return jnp.dot(a, b, preferred_element_type=F32)


def _dot_tn(a, b):
    return lax.dot_general(a, b, (((0,), (0,)), ((), ())), preferred_element_type=F32)


def _dot_nt(a, b):
    return lax.dot_general(a, b, (((1,), (1,)), ((), ())), preferred_element_type=F32)


def _ln(z, g, b):
    mu = jnp.mean(z, axis=-1, keepdims=True)
    xc = z - mu
    var = jnp.mean(xc * xc, axis=-1, keepdims=True)
    return xc * lax.rsqrt(var + LN_EPS) * g + b


def _rows_view(pack, rows_per_dev, row0):
    return (pack, (N_DEV, rows_per_dev, D_MODEL), (0, row0 // rows_per_dev, 0), (N_DEV * rows_per_dev, D_MODEL))


def _mat_view(arr, cols=None, cblk=0):
    k, n = arr.shape
    cols = n if cols is None else cols
    return (arr, (k, cols), (0, cblk), (k, cols))


def _vspec(view):
    _, bshape, bidx, _ = view
    return pl.BlockSpec(bshape, lambda *_: bidx)


def _vload(view, ref):
    return ref[...].reshape(view[3])


class Comm:
    def __init__(self, ins, out_shapes, sems, start, finish, aliases=None):
        self.ins, self.out_shapes, self.sems = list(ins), list(out_shapes), list(sems)
        self.start, self.finish, self.aliases = start, finish, dict(aliases or {})


def _call(body, *, grid, in_specs, out_specs, out_shape, operands, name, sem, scratch=(), comm=None):
    n_in, n_out, n_scr = len(in_specs), len(out_specs), len(scratch)
    if comm is None:
        return pl.pallas_call(body, grid=grid, in_specs=list(in_specs), out_specs=list(out_specs),
                              out_shape=list(out_shape), scratch_shapes=list(scratch), compiler_params=_cp(*sem),
                              name=name)(*operands), []
    nci, nco = len(comm.ins), len(comm.out_shapes)

    def hosted(*refs):
        ins, cins = refs[:n_in], refs[n_in:n_in + nci]
        o0 = n_in + nci
        outs, couts = refs[o0:o0 + n_out], refs[o0 + n_out:o0 + n_out + nco]
        s0 = o0 + n_out + nco
        scr, csems = refs[s0:s0 + n_scr], refs[s0 + n_scr:]
        first = last = None
        for ax, size in enumerate(grid):
            pid = pl.program_id(ax)
            f, e = pid == 0, pid == size - 1
            first = f if first is None else jnp.logical_and(first, f)
            last = e if last is None else jnp.logical_and(last, e)

        @pl.when(first)
        def _():
            comm.start(cins, couts, csems)

        body(*ins, *outs, *scr)

        @pl.when(last)
        def _():
            comm.finish(cins, couts, csems)

    res = pl.pallas_call(
        hosted, grid=grid, in_specs=list(in_specs) + [ANY] * nci, out_specs=list(out_specs) + [ANY] * nco,
        out_shape=list(out_shape) + comm.out_shapes, scratch_shapes=list(scratch) + comm.sems,
        input_output_aliases={n_in + i: n_out + o for i, o in comm.aliases.items()},
        compiler_params=_cp(*(("arbitrary",) * len(grid))), name=name)(*operands, *comm.ins)
    return res[:n_out], res[n_out:]


def run_comm(comm, name):
    nci, nco = len(comm.ins), len(comm.out_shapes)

    def body(*refs):
        cins, couts, csems = refs[:nci], refs[nci:nci + nco], refs[nci + nco:]
        comm.start(cins, couts, csems)
        comm.finish(cins, couts, csems)

    return pl.pallas_call(
        body, in_specs=[ANY] * nci, out_specs=[ANY] * nco, out_shape=comm.out_shapes, scratch_shapes=comm.sems,
        input_output_aliases=dict(comm.aliases), name=name)(*comm.ins)


def mm_plain(a, wv, out_dtype, name, nt=False, tm=512):
    S, K = a.shape
    N = wv[3][0] if nt else wv[3][1]
    tm = min(tm, S)
    dot = _dot_nt if nt else _dot

    def body(a_ref, w_ref, o_ref):
        o_ref[...] = dot(a_ref[...], _vload(wv, w_ref)).astype(out_dtype)

    return pl.pallas_call(
        body, grid=(S // tm,),
        in_specs=[pl.BlockSpec((tm, K), lambda i: (i, 0)), _vspec(wv)],
        out_specs=pl.BlockSpec((tm, N), lambda i: (i, 0)),
        out_shape=jax.ShapeDtypeStruct((S, N), out_dtype),
        compiler_params=_cp("parallel"), name=name)(a, wv[0])


def mm_res_nt(a, wv, r, alpha, name, tm=256):
    S, K = a.shape
    N = wv[3][0]
    tm = min(tm, S)

    def body(a_ref, w_ref, r_ref, o_ref):
        o_ref[...] = _dot_nt(a_ref[...], _vload(wv, w_ref)) + alpha * r_ref[...]

    return pl.pallas_call(
        body, grid=(S // tm,),
        in_specs=[pl.BlockSpec((tm, K), lambda i: (i, 0)), _vspec(wv), pl.BlockSpec((tm, N), lambda i: (i, 0))],
        out_specs=pl.BlockSpec((tm, N), lambda i: (i, 0)),
        out_shape=jax.ShapeDtypeStruct((S, N), F32),
        compiler_params=_cp("parallel"), name=name)(a, wv[0], r)


def mm_res2_nt(a3, wv0, wv1, r, alpha, name, tm=256, comm=None):
    _, S, K = a3.shape
    N = wv0[3][0]
    tm = min(tm, S)

    def body(a_ref, w0_ref, w1_ref, r_ref, o_ref):
        o_ref[...] = (_dot_nt(a_ref[0], _vload(wv0, w0_ref)) + _dot_nt(a_ref[1], _vload(wv1, w1_ref))
                      + alpha * r_ref[...])

    (out,), couts = _call(
        body, grid=(S // tm,),
        in_specs=[pl.BlockSpec((2, tm, K), lambda i: (0, i, 0)), _vspec(wv0), _vspec(wv1),
                  pl.BlockSpec((tm, N), lambda i: (i, 0))],
        out_specs=[pl.BlockSpec((tm, N), lambda i: (i, 0))],
        out_shape=[jax.ShapeDtypeStruct((S, N), F32)],
        operands=(a3, wv0[0], wv1[0], r), name=name, sem=("parallel",), comm=comm)
    return out, couts


def mm_ln(a, wv, h, g, b, scale, name, tm=256, comm=None):
    S, K = a.shape
    tm = min(tm, S)

    def body(a_ref, w_ref, h_ref, g_ref, b_ref, z_ref, y_ref, yb_ref):
        z = ALPHA * h_ref[...] + scale * _dot(a_ref[...], _vload(wv, w_ref))
        y = _ln(z, g_ref[...], b_ref[...])
        z_ref[...] = z
        y_ref[...] = y
        yb_ref[...] = y.astype(BF16)

    row = pl.BlockSpec((tm, D_MODEL), lambda i: (i, 0))
    vec = pl.BlockSpec((1, D_MODEL), lambda i: (0, 0))
    return _call(
        body, grid=(S // tm,),
        in_specs=[pl.BlockSpec((tm, K), lambda i: (i, 0)), _vspec(wv), row, vec, vec],
        out_specs=[row, row, row],
        out_shape=[jax.ShapeDtypeStruct((S, D_MODEL), F32), jax.ShapeDtypeStruct((S, D_MODEL), F32),
                   jax.ShapeDtypeStruct((S, D_MODEL), BF16)],
        operands=(a, wv[0], h, g.reshape(1, -1), b.reshape(1, -1)), name=name, sem=("parallel",), comm=comm)


def ffn_up(xb, w_in, name, tm=512, tn=1408, comm=None):
    S = xb.shape[0]
    tm = min(tm, S)
    nj = D_FF // tn

    def body(x_ref, wg_ref, wu_ref, gu_ref, a_ref):
        x = x_ref[...]
        g = _dot(x, wg_ref[...])
        u = _dot(x, wu_ref[...])
        gu_ref[0] = g.astype(BF16)
        gu_ref[1] = u.astype(BF16)
        a_ref[...] = (g * jax.nn.sigmoid(g) * u).astype(BF16)

    return _call(
        body, grid=(nj, S // tm),
        in_specs=[pl.BlockSpec((tm, D_MODEL), lambda j, i: (i, 0)),
                  pl.BlockSpec((D_MODEL, tn), lambda j, i: (0, j)),
                  pl.BlockSpec((D_MODEL, tn), lambda j, i: (0, nj + j))],
        out_specs=[pl.BlockSpec((2, tm, tn), lambda j, i: (0, i, j)), pl.BlockSpec((tm, tn), lambda j, i: (i, j))],
        out_shape=[jax.ShapeDtypeStruct((2, S, D_FF), BF16), jax.ShapeDtypeStruct((S, D_FF), BF16)],
        operands=(xb, w_in, w_in), name=name, sem=("parallel", "parallel"), comm=comm)


def ffn_mid_bwd(dfb, r_ff, blk, gu, name, tm=512, comm=None):
    S = dfb.shape[0]
    tm = min(tm, S)
    tn = 4 * FF_SHARD

    def body(df_ref, w_ref, gu_ref, dh_ref):
        da = _dot_nt(df_ref[...], w_ref[...].reshape(tn, D_MODEL))
        g = gu_ref[0].astype(F32)
        u = gu_ref[1].astype(F32)
        sg = jax.nn.sigmoid(g)
        dh_ref[0] = (da * u * (sg * (1.0 + g * (1.0 - sg)))).astype(BF16)
        dh_ref[1] = (da * (g * sg)).astype(BF16)

    gspec = pl.BlockSpec((2, tm, tn), lambda j, i: (0, i, j))
    (out,), couts = _call(
        body, grid=(2, S // tm),
        in_specs=[pl.BlockSpec((tm, D_MODEL), lambda j, i: (i, 0)),
                  pl.BlockSpec((4, FF_SHARD, D_MODEL), lambda j, i: (j, blk, 0)), gspec],
        out_specs=[gspec],
        out_shape=[jax.ShapeDtypeStruct((2, S, D_FF), BF16)],
        operands=(dfb, r_ff, gu), name=name, sem=("parallel", "parallel"), comm=comm)
    return out, couts


def mm_tn(a, b, name, tm, tn, ts=1024, planes=1, shard=None, comm=None):
    S, M = a.shape
    N = b.shape[-1] * planes
    ts = min(ts, S)
    per = b.shape[-1] // tn
    ns = S // ts

    if shard is None:
        def body(a_ref, b_ref, o_ref):
            @pl.when(pl.program_id(2) == 0)
            def _():
                o_ref[...] = jnp.zeros_like(o_ref)

            o_ref[...] += _dot_tn(a_ref[...], b_ref[...])

        out_spec = pl.BlockSpec((tm, tn), lambda i, j, s: (i, j))
        out_shape = jax.ShapeDtypeStruct((M, N), F32)
        scratch = ()
    else:
        def body(a_ref, b_ref, o_ref, acc_ref):
            s = pl.program_id(2)

            @pl.when(s == 0)
            def _():
                acc_ref[...] = jnp.zeros_like(acc_ref)

            acc_ref[...] += _dot_tn(a_ref[...], b_ref[...])

            @pl.when(s == ns - 1)
            def _():
                for q in range(tn // shard):
                    o_ref[q] = acc_ref[:, shard * q:shard * (q + 1)]

        out_spec = pl.BlockSpec((tn // shard, tm, shard), lambda i, j, s: (j, i, 0))
        out_shape = jax.ShapeDtypeStruct((N // shard, M, shard), F32)
        scratch = (pltpu.VMEM((tm, tn), F32),)

    if planes == 1:
        bspec = pl.BlockSpec((ts, tn), lambda i, j, s: (s, j))
    else:
        bspec = pl.BlockSpec((None, ts, tn), lambda i, j, s: (j // per, s, j % per))
    (out,), couts = _call(
        body, grid=(M // tm, N // tn, ns),
        in_specs=[pl.BlockSpec((ts, tm), lambda i, j, s: (s, i)), bspec],
        out_specs=[out_spec], out_shape=[out_shape], scratch=scratch,
        operands=(a, b), name=name, sem=("parallel", "parallel", "arbitrary"), comm=comm)
    return out, couts


def ln_bwd(dy, z, g, out_scale, name, tm=512):
    S = dy.shape[0]
    tm = min(tm, S)

    def body(dy_ref, z_ref, g_ref, dz_ref, dzb_ref, dg_ref, db_ref):
        @pl.when(pl.program_id(0) == 0)
        def _():
            dg_ref[...] = jnp.zeros_like(dg_ref)
            db_ref[...] = jnp.zeros_like(db_ref)

        z = z_ref[...]
        dy_ = dy_ref[...]
        mu = jnp.mean(z, axis=-1, keepdims=True)
        xc = z - mu
        var = jnp.mean(xc * xc, axis=-1, keepdims=True)
        rstd = lax.rsqrt(var + LN_EPS)
        xh = xc * rstd
        dxh = dy_ * g_ref[...]
        m1 = jnp.mean(dxh, axis=-1, keepdims=True)
        m2 = jnp.mean(dxh * xh, axis=-1, keepdims=True)
        dz = rstd * (dxh - m1 - xh * m2)
        dz_ref[...] = dz
        dzb_ref[...] = (out_scale * dz).astype(BF16)
        dg_ref[...] += jnp.sum(dy_ * xh, axis=0, keepdims=True)
        db_ref[...] += jnp.sum(dy_, axis=0, keepdims=True)

    row = pl.BlockSpec((tm, D_MODEL), lambda i: (i, 0))
    vec = pl.BlockSpec((1, D_MODEL), lambda i: (0, 0))
    return pl.pallas_call(
        body, grid=(S // tm,),
        in_specs=[row, row, vec],
        out_specs=[row, row, vec, vec],
        out_shape=[jax.ShapeDtypeStruct((S, D_MODEL), F32), jax.ShapeDtypeStruct((S, D_MODEL), BF16),
                   jax.ShapeDtypeStruct((1, D_MODEL), F32), jax.ShapeDtypeStruct((1, D_MODEL), F32)],
        compiler_params=_cp("arbitrary"), name=name)(dy, z, g.reshape(1, -1))


def loss_head(y, t, name, tm=512):
    S = y.shape[0]
    tm = min(tm, S)
    nt = S // tm

    def body(y_ref, t_ref, dy_ref, l_ref):
        i = pl.program_id(0)

        @pl.when(i == 0)
        def _():
            l_ref[...] = jnp.zeros_like(l_ref)

        e = y_ref[...] - t_ref[...]
        dy_ref[...] = e * (1.0 / D_MODEL)
        l_ref[...] += jnp.sum(e * e, axis=0, keepdims=True)

        @pl.when(i == nt - 1)
        def _():
            tot = jnp.sum(l_ref[...], axis=1, keepdims=True) * (0.5 / D_MODEL)
            l_ref[...] = jnp.broadcast_to(tot, l_ref.shape)

    row = pl.BlockSpec((tm, D_MODEL), lambda i: (i, 0))
    vec = pl.BlockSpec((1, D_MODEL), lambda i: (0, 0))
    return pl.pallas_call(
        body, grid=(nt,), in_specs=[row, row], out_specs=[row, vec],
        out_shape=[jax.ShapeDtypeStruct((S, D_MODEL), F32), jax.ShapeDtypeStruct((1, D_MODEL), F32)],
        compiler_params=_cp("arbitrary"), name=name)(y, t)


def _rope_tables(S):
    pos = jnp.arange(S, dtype=F32)
    inv_freq = ROPE_THETA ** (-jnp.arange(0, HEAD_DIM, 2, dtype=F32) / HEAD_DIM)
    ang = pos[:, None] * inv_freq[None, :]
    cos, sin = jnp.cos(ang), jnp.sin(ang)
    cosf = jnp.concatenate([cos, cos, cos, cos], axis=1)
    sinf = jnp.concatenate([-sin, sin, -sin, sin], axis=1)
    return cosf, sinf


def _rot(t, c, s, first):
    sw = jnp.where(first, pltpu.roll(t, 96, 1), pltpu.roll(t, 32, 1))
    return t * c + sw * s


def rope_fwd(qkv, cosf, sinf, name, tm=512):
    S = qkv.shape[0]
    tm = min(tm, S)

    def body(x_ref, c_ref, s_ref, q_ref, k_ref, v_ref):
        c = c_ref[...]
        s = s_ref[...]
        first = (lax.broadcasted_iota(jnp.int32, (tm, 128), 1) % HEAD_DIM) < (HEAD_DIM // 2)
        for j in range(8):
            q_ref[:, 128 * j:128 * (j + 1)] = _rot(x_ref[:, 128 * j:128 * (j + 1)], c, s, first).astype(BF16)
        for j in range(2):
            k_ref[:, 128 * j:128 * (j + 1)] = _rot(x_ref[:, 1024 + 128 * j:1024 + 128 * (j + 1)], c, s, first).astype(BF16)
        v_ref[...] = x_ref[:, 1280:1536].astype(BF16)

    tab = pl.BlockSpec((tm, 128), lambda i: (i, 0))
    return pl.pallas_call(
        body, grid=(S // tm,),
        in_specs=[pl.BlockSpec((tm, QKV), lambda i: (i, 0)), tab, tab],
        out_specs=[pl.BlockSpec((tm, 1024), lambda i: (i, 0)), pl.BlockSpec((tm, 256), lambda i: (i, 0)),
                   pl.BlockSpec((tm, 256), lambda i: (i, 0))],
        out_shape=[jax.ShapeDtypeStruct((S, 1024), BF16), jax.ShapeDtypeStruct((S, 256), BF16),
                   jax.ShapeDtypeStruct((S, 256), BF16)],
        compiler_params=_cp("parallel"), name=name)(qkv, cosf, sinf)


def rope_bwd(dq, dkc, dkp, dvc, dvp, cosf, sinf, name, tq):
    S = dq.shape[0]
    nt = S // tq

    def body(dq_ref, dkc_ref, dkp_ref, dvc_ref, dvp_ref, c_ref, s_ref, o_ref):
        i = pl.program_id(0)
        c = c_ref[...]
        s = -s_ref[...]
        first = (lax.broadcasted_iota(jnp.int32, (tq, 128), 1) % HEAD_DIM) < (HEAD_DIM // 2)
        for j in range(8):
            o_ref[:, 128 * j:128 * (j + 1)] = _rot(dq_ref[:, 128 * j:128 * (j + 1)], c, s, first).astype(BF16)
        has_next = i < nt - 1
        rows = lax.broadcasted_iota(jnp.int32, (tq, 256), 0)
        pad = jnp.zeros((tq - BLOCK, 256), F32)
        halo_k = jnp.concatenate([pad, dkp_ref[...]], axis=0)
        halo_v = jnp.concatenate([pad, dvp_ref[...]], axis=0)
        use = jnp.logical_and(has_next, rows >= tq - BLOCK)
        dk = dkc_ref[...] + jnp.where(use, halo_k, 0.0)
        dv = dvc_ref[...] + jnp.where(use, halo_v, 0.0)
        for j in range(2):
            o_ref[:, 1024 + 128 * j:1024 + 128 * (j + 1)] = _rot(dk[:, 128 * j:128 * (j + 1)], c, s, first).astype(BF16)
        o_ref[:, 1280:1536] = dv.astype(BF16)

    tab = pl.BlockSpec((tq, 128), lambda i: (i, 0))
    cur = pl.BlockSpec((tq, 256), lambda i: (i, 0))
    nxt = pl.BlockSpec((BLOCK, 256), lambda i: (jnp.minimum(i + 1, nt - 1), 0))
    return pl.pallas_call(
        body, grid=(nt,),
        in_specs=[pl.BlockSpec((tq, 1024), lambda i: (i, 0)), cur, nxt, cur, nxt, tab, tab],
        out_specs=pl.BlockSpec((tq, QKV), lambda i: (i, 0)),
        out_shape=jax.ShapeDtypeStruct((S, QKV), BF16),
        compiler_params=_cp("parallel"), name=name)(dq, dkc, dkp, dvc, dvp, cosf, sinf)


def _attn_masks(n):
    shape = (GROUP * BLOCK, 2 * BLOCK)
    r = lax.broadcasted_iota(jnp.int32, shape, 0) % BLOCK
    c = lax.broadcasted_iota(jnp.int32, shape, 1)
    ok = jnp.logical_and(c > r, c <= r + BLOCK)
    return jnp.logical_and(ok, jnp.logical_or(c >= BLOCK, n > 0)), ok


def _attn_probs(qs, kw, sink_col, ok):
    s = jnp.where(ok, _dot_nt(qs, kw) * (HEAD_DIM ** -0.5), NEG)
    m = jnp.maximum(jnp.max(s, axis=1, keepdims=True), sink_col)
    p = jnp.exp(s - m)
    es = jnp.exp(sink_col - m)
    inv = 1.0 / (jnp.sum(p, axis=1, keepdims=True) + es)
    return p * inv, es * inv


def _sink_col(sink_ref, g):
    rid = lax.broadcasted_iota(jnp.int32, (GROUP * BLOCK, 1), 0) // BLOCK
    col = jnp.zeros((GROUP * BLOCK, 1), F32)
    for j in range(GROUP):
        col = jnp.where(rid == j, sink_ref[GROUP * g + j], col)
    return col


KV_PAIR = 2
Q_LANES = KV_PAIR * GROUP * HEAD_DIM


def _head_rows(ref, lo, gi):
    base = gi * GROUP * HEAD_DIM
    return jnp.concatenate([ref[lo:lo + BLOCK, base + HEAD_DIM * j:base + HEAD_DIM * (j + 1)] for j in range(GROUP)], axis=0)


def _kv_window(kc_ref, kp_ref, b, gi):
    ks = slice(HEAD_DIM * gi, HEAD_DIM * (gi + 1))
    if b == 0:
        return jnp.concatenate([kp_ref[:, ks], kc_ref[0:BLOCK, ks]], axis=0)
    return kc_ref[BLOCK * (b - 1):BLOCK * (b + 1), ks]


def _attn_specs(tq):
    nsub = tq // BLOCK
    qspec = pl.BlockSpec((tq, Q_LANES), lambda p, n: (n, p))
    cur = pl.BlockSpec((tq, KV_PAIR * HEAD_DIM), lambda p, n: (n, p))
    prev = pl.BlockSpec((BLOCK, KV_PAIR * HEAD_DIM), lambda p, n: (jnp.maximum(n * nsub - 1, 0), p))
    return qspec, cur, prev


def attn_fwd(q, k, v, sinks, name, tq):
    S = q.shape[0]
    nsub = tq // BLOCK

    def body(sink_ref, q_ref, kc_ref, kp_ref, vc_ref, vp_ref, o_ref):
        p = pl.program_id(0)
        n = pl.program_id(1)
        ok_first, ok_rest = _attn_masks(n)
        for gi in range(KV_PAIR):
            sink_col = _sink_col(sink_ref, KV_PAIR * p + gi)
            base = gi * GROUP * HEAD_DIM
            for b in range(nsub):
                lo = BLOCK * b
                qs = _head_rows(q_ref, lo, gi)
                pn, _ = _attn_probs(qs, _kv_window(kc_ref, kp_ref, b, gi), sink_col, ok_rest if b else ok_first)
                o = _dot(pn.astype(BF16), _kv_window(vc_ref, vp_ref, b, gi))
                for j in range(GROUP):
                    o_ref[lo:lo + BLOCK, base + HEAD_DIM * j:base + HEAD_DIM * (j + 1)] = (
                        o[BLOCK * j:BLOCK * (j + 1)].astype(BF16))

    qspec, cur, prev = _attn_specs(tq)
    return pl.pallas_call(
        body, grid=(N_KV // KV_PAIR, S // tq),
        in_specs=[pl.BlockSpec(memory_space=pltpu.SMEM), qspec, cur, prev, cur, prev],
        out_specs=qspec,
        out_shape=jax.ShapeDtypeStruct((S, N_HEADS * HEAD_DIM), BF16),
        compiler_params=_cp("parallel", "parallel"), name=name)(sinks, q, k, k, v, v)


def attn_bwd(q, k, v, do, sinks, name, tq):
    S = q.shape[0]
    nsub = tq // BLOCK
    nt = S // tq

    def body(sink_ref, q_ref, kc_ref, kp_ref, vc_ref, vp_ref, do_ref, dq_ref, dkc_ref, dkp_ref, dvc_ref, dvp_ref, ds_ref):
        p = pl.program_id(0)
        n = pl.program_id(1)

        @pl.when(n == 0)
        def _():
            ds_ref[...] = jnp.zeros_like(ds_ref)

        dkc_ref[...] = jnp.zeros_like(dkc_ref)
        dvc_ref[...] = jnp.zeros_like(dvc_ref)
        rid = lax.broadcasted_iota(jnp.int32, (GROUP * BLOCK, 1), 0) // BLOCK
        sub = lax.broadcasted_iota(jnp.int32, (8, 128), 0)
        ok_first, ok_rest = _attn_masks(n)
        for gi in range(KV_PAIR):
            sink_col = _sink_col(sink_ref, KV_PAIR * p + gi)
            base = gi * GROUP * HEAD_DIM
            ks = slice(HEAD_DIM * gi, HEAD_DIM * (gi + 1))
            dsink = jnp.zeros((8, 128), F32)
            for b in range(nsub):
                lo = BLOCK * b
                qs = _head_rows(q_ref, lo, gi)
                dos = _head_rows(do_ref, lo, gi)
                kw = _kv_window(kc_ref, kp_ref, b, gi)
                vw = _kv_window(vc_ref, vp_ref, b, gi)
                pn, ps = _attn_probs(qs, kw, sink_col, ok_rest if b else ok_first)
                dp = _dot_nt(dos, vw)
                delta = jnp.sum(pn * dp, axis=1, keepdims=True)
                dsb = (pn * (dp - delta) * (HEAD_DIM ** -0.5)).astype(BF16)
                dq = _dot(dsb, kw)
                for j in range(GROUP):
                    dq_ref[lo:lo + BLOCK, base + HEAD_DIM * j:base + HEAD_DIM * (j + 1)] = dq[BLOCK * j:BLOCK * (j + 1)]
                dkw = _dot_tn(dsb, qs)
                dvw = _dot_tn(pn.astype(BF16), dos)
                if b == 0:
                    dkp_ref[:, ks] = dkw[0:BLOCK]
                    dvp_ref[:, ks] = dvw[0:BLOCK]
                else:
                    dkc_ref[lo - BLOCK:lo, ks] += dkw[0:BLOCK]
                    dvc_ref[lo - BLOCK:lo, ks] += dvw[0:BLOCK]
                dkc_ref[lo:lo + BLOCK, ks] += dkw[BLOCK:2 * BLOCK]
                dvc_ref[lo:lo + BLOCK, ks] += dvw[BLOCK:2 * BLOCK]
                sd = ps * delta
                for j in range(GROUP):
                    tot = -jnp.sum(jnp.where(rid == j, sd, 0.0))
                    dsink = dsink + jnp.where(sub == j, tot, 0.0)
            ds_ref[gi] += dsink

    qspec, cur, prev = _attn_specs(tq)
    halo = pl.BlockSpec((BLOCK, KV_PAIR * HEAD_DIM), lambda p, n: (n, p))
    kv_shape = jax.ShapeDtypeStruct((S, N_KV * HEAD_DIM), F32)
    halo_shape = jax.ShapeDtypeStruct((nt * BLOCK, N_KV * HEAD_DIM), F32)
    return pl.pallas_call(
        body, grid=(N_KV // KV_PAIR, nt),
        in_specs=[pl.BlockSpec(memory_space=pltpu.SMEM), qspec, cur, prev, cur, prev, qspec],
        out_specs=[qspec, cur, halo, cur, halo, pl.BlockSpec((KV_PAIR, 8, 128), lambda p, n: (p, 0, 0))],
        out_shape=[jax.ShapeDtypeStruct((S, N_HEADS * HEAD_DIM), F32), kv_shape, halo_shape, kv_shape, halo_shape,
                   jax.ShapeDtypeStruct((N_KV, 8, 128), F32)],
        compiler_params=_cp("parallel", "arbitrary"), name=name)(sinks, q, k, k, v, v, do)


def _rows_before(cur, prev8, k):
    if k == 0:
        return cur
    n = cur.shape[0]
    ext = jnp.concatenate([prev8, cur], axis=0)
    return ext[8 - k:8 - k + n]


def _rows_after(cur, next8, k):
    if k == 0:
        return cur
    n = cur.shape[0]
    ext = jnp.concatenate([cur, next8], axis=0)
    return ext[k:k + n]


def _gelu(x):
    c = math.sqrt(2.0 / math.pi)
    t = jnp.tanh(c * (x + 0.044715 * (x * x * x)))
    return 0.5 * (1.0 + t), t


def _neg_expm1(u):
    ser = 1.0 + u * (1.0 / 6.0)
    for k in range(5, 1, -1):
        ser = 1.0 + (u * (1.0 / k)) * ser
    return jnp.where(u > -0.125, -(u * ser), 1.0 - jnp.exp(u))


def _block_diag(xb16, w_ref):
    return jnp.concatenate([_dot(xb16[:, RNN_W * n:RNN_W * (n + 1)], w_ref[n]) for n in range(RNN_BLOCKS)], axis=1)


def _lru_gates(xb, prev8, cw_ref, cb_ref, wra_ref, wrx_ref, bra_ref, brx_ref, lsl_ref):
    xc = cb_ref[...] + cw_ref[3:4, :] * xb
    for w in range(CONV_W - 1):
        xc = xc + cw_ref[w:w + 1, :] * _rows_before(xb, prev8, CONV_W - 1 - w)
    xcb = xc.astype(BF16)
    r = jax.nn.sigmoid(_block_diag(xcb, wra_ref) + bra_ref[...])
    ig = jax.nn.sigmoid(_block_diag(xcb, wrx_ref) + brx_ref[...])
    la = LRU_C * r * lsl_ref[...]
    a = jnp.exp(la)
    sq = jnp.sqrt(_neg_expm1(2.0 * la))
    return xc, xcb, r, ig, a, sq


def lru_fwd(xg, p, name, tm=256):
    S = xg.shape[0]
    tm = min(tm, S)

    def body(xg_ref, xp_ref, cw_ref, cb_ref, wra_ref, wrx_ref, bra_ref, brx_ref, lsl_ref, y_ref, h_ref, hc_ref, a_s, b_s):
        i = pl.program_id(0)
        xb = xg_ref[:, 0:D_RNN]
        gb = xg_ref[:, D_RNN:2 * D_RNN]
        prev8 = jnp.where(i > 0, xp_ref[:, 0:D_RNN], 0.0)
        xc, _, r, ig, a, sq = _lru_gates(xb, prev8, cw_ref, cb_ref, wra_ref, wrx_ref, bra_ref, brx_ref, lsl_ref)
        a_s[...] = a
        b_s[...] = sq * (ig * xc)

        @pl.when(i == 0)
        def _():
            hc_ref[...] = jnp.zeros_like(hc_ref)

        def chunk(c, h):
            o = pl.multiple_of(c * 8, 8)
            av = a_s[pl.ds(o, 8), :]
            bv = b_s[pl.ds(o, 8), :]
            rows = []
            for t in range(8):
                h = av[t:t + 1, :] * h + bv[t:t + 1, :]
                rows.append(h)
            h_ref[pl.ds(o, 8), :] = jnp.concatenate(rows, axis=0)
            return h

        h_last = lax.fori_loop(0, tm // 8, chunk, hc_ref[0:1, :])
        hc_ref[0:1, :] = h_last
        cdf, _ = _gelu(gb)
        y_ref[...] = (h_ref[...] * (gb * cdf)).astype(BF16)

    vec = pl.BlockSpec((1, D_RNN), lambda i: (0, 0))
    wsp = pl.BlockSpec((RNN_BLOCKS, RNN_W, RNN_W), lambda i: (0, 0, 0))
    return pl.pallas_call(
        body, grid=(S // tm,),
        in_specs=[pl.BlockSpec((tm, 2 * D_RNN), lambda i: (i, 0)),
                  pl.BlockSpec((8, 2 * D_RNN), lambda i: (jnp.maximum(i * (tm // 8) - 1, 0), 0)),
                  pl.BlockSpec((CONV_W, D_RNN), lambda i: (0, 0)), vec, wsp, wsp, vec, vec, vec],
        out_specs=[pl.BlockSpec((tm, D_RNN), lambda i: (i, 0)), pl.BlockSpec((tm, D_RNN), lambda i: (i, 0))],
        out_shape=[jax.ShapeDtypeStruct((S, D_RNN), BF16), jax.ShapeDtypeStruct((S, D_RNN), F32)],
        scratch_shapes=[pltpu.VMEM((8, D_RNN), F32), pltpu.VMEM((tm, D_RNN), F32), pltpu.VMEM((tm, D_RNN), F32)],
        compiler_params=_cp("arbitrary"), name=name)(
            xg, xg, p["conv_w"], p["conv_b"], p["w_ra"], p["w_rx"], p["b_ra"], p["b_rx"], p["lsl"])


def lru_bwd(dy, xg, h, p, name, tm=256):
    S = xg.shape[0]
    tm = min(tm, S)
    nt = S // tm

    def body(dy_ref, xg_ref, xp_ref, h_ref, hp_ref, cw_ref, cb_ref, wra_ref, wrx_ref, bra_ref, brx_ref,
             lsl_ref, dxg_ref, dcw_ref, dcb_ref, dwra_ref, dwrx_ref, dbra_ref, dbrx_ref, dlam_ref,
             lc_ref, nx_ref, a_s, g_s, l_s):
        i = pl.program_id(0)
        ti = nt - 1 - i

        @pl.when(i == 0)
        def _():
            lc_ref[...] = jnp.zeros_like(lc_ref)
            nx_ref[...] = jnp.zeros_like(nx_ref)
            for ref in (dcw_ref, dcb_ref, dwra_ref, dwrx_ref, dbra_ref, dbrx_ref, dlam_ref):
                ref[...] = jnp.zeros_like(ref)

        xb = xg_ref[:, 0:D_RNN]
        gb = xg_ref[:, D_RNN:2 * D_RNN]
        prev8 = jnp.where(ti > 0, xp_ref[:, 0:D_RNN], 0.0)
        xc, xcb, r, ig, a, sq = _lru_gates(xb, prev8, cw_ref, cb_ref, wra_ref, wrx_ref, bra_ref, brx_ref, lsl_ref)
        hh = h_ref[...]
        hprev = _rows_before(hh, jnp.where(ti > 0, hp_ref[...], 0.0), 1)
        dy_ = dy_ref[...]
        cdf, th = _gelu(gb)
        c0 = math.sqrt(2.0 / math.pi)
        dgate = cdf + gb * (0.5 * (1.0 - th * th) * c0 * (1.0 + 3.0 * 0.044715 * gb * gb))
        dgb = dy_ * hh * dgate
        a_s[...] = a
        g_s[...] = dy_ * (gb * cdf)

        def chunk(cc, carry):
            o = pl.multiple_of((tm // 8 - 1 - cc) * 8, 8)
            av = a_s[pl.ds(o, 8), :]
            gv = g_s[pl.ds(o, 8), :]
            rows = [None] * 8
            for t in range(7, -1, -1):
                lam_t = gv[t:t + 1, :] + carry
                rows[t] = lam_t
                carry = av[t:t + 1, :] * lam_t
            l_s[pl.ds(o, 8), :] = jnp.concatenate(rows, axis=0)
            return carry

        carry = lax.fori_loop(0, tm // 8, chunk, lc_ref[0:1, :])
        lc_ref[0:1, :] = carry
        lam = l_s[...]
        da = lam * hprev
        dixc = lam * sq
        di = dixc * xc
        dxc = dixc * ig
        dsq = lam * (ig * xc)
        dla = da * a - dsq * (a * a / sq)
        dr = dla * (LRU_C * lsl_ref[...])
        dlam_ref[...] += jnp.sum(dla * (LRU_C * r), axis=0, keepdims=True)
        dpr = dr * r * (1.0 - r)
        dpi = di * ig * (1.0 - ig)
        dbra_ref[...] += jnp.sum(dpr, axis=0, keepdims=True)
        dbrx_ref[...] += jnp.sum(dpi, axis=0, keepdims=True)
        dprb = dpr.astype(BF16)
        dpib = dpi.astype(BF16)
        back = []
        for n in range(RNN_BLOCKS):
            sl = slice(RNN_W * n, RNN_W * (n + 1))
            dwra_ref[n] += _dot_tn(xcb[:, sl], dprb[:, sl])
            dwrx_ref[n] += _dot_tn(xcb[:, sl], dpib[:, sl])
            back.append(_dot_nt(dprb[:, sl], wra_ref[n]) + _dot_nt(dpib[:, sl], wrx_ref[n]))
        dxc = dxc + jnp.concatenate(back, axis=1)
        dcb_ref[...] += jnp.sum(dxc, axis=0, keepdims=True)
        next8 = nx_ref[...]
        dxb = cw_ref[3:4, :] * dxc
        dcw_ref[3:4, :] += jnp.sum(dxc * xb, axis=0, keepdims=True)
        for w in range(CONV_W - 1):
            k = CONV_W - 1 - w
            dcw_ref[w:w + 1, :] += jnp.sum(dxc * _rows_before(xb, prev8, k), axis=0, keepdims=True)
            dxb = dxb + cw_ref[w:w + 1, :] * _rows_after(dxc, next8, k)
        nx_ref[...] = dxc[0:8, :]
        dxg_ref[:, 0:D_RNN] = dxb.astype(BF16)
        dxg_ref[:, D_RNN:2 * D_RNN] = dgb.astype(BF16)

    rev = lambda i: (nt - 1 - i, 0)
    before = lambda i: (jnp.maximum((nt - 1 - i) * (tm // 8) - 1, 0), 0)
    vec = pl.BlockSpec((1, D_RNN), lambda i: (0, 0))
    wsp = pl.BlockSpec((RNN_BLOCKS, RNN_W, RNN_W), lambda i: (0, 0, 0))
    cwsp = pl.BlockSpec((CONV_W, D_RNN), lambda i: (0, 0))
    return pl.pallas_call(
        body, grid=(nt,),
        in_specs=[pl.BlockSpec((tm, D_RNN), rev), pl.BlockSpec((tm, 2 * D_RNN), rev), pl.BlockSpec((8, 2 * D_RNN), before),
                  pl.BlockSpec((tm, D_RNN), rev), pl.BlockSpec((8, D_RNN), before),
                  cwsp, vec, wsp, wsp, vec, vec, vec],
        out_specs=[pl.BlockSpec((tm, 2 * D_RNN), rev), cwsp, vec, wsp, wsp, vec, vec, vec],
        out_shape=[jax.ShapeDtypeStruct((S, 2 * D_RNN), BF16), jax.ShapeDtypeStruct((CONV_W, D_RNN), F32),
                   jax.ShapeDtypeStruct((1, D_RNN), F32), jax.ShapeDtypeStruct((RNN_BLOCKS, RNN_W, RNN_W), F32),
                   jax.ShapeDtypeStruct((RNN_BLOCKS, RNN_W, RNN_W), F32), jax.ShapeDtypeStruct((1, D_RNN), F32),
                   jax.ShapeDtypeStruct((1, D_RNN), F32), jax.ShapeDtypeStruct((1, D_RNN), F32)],
        scratch_shapes=[pltpu.VMEM((8, D_RNN), F32), pltpu.VMEM((8, D_RNN), F32), pltpu.VMEM((tm, D_RNN), F32),
                        pltpu.VMEM((tm, D_RNN), F32), pltpu.VMEM((tm, D_RNN), F32)],
        compiler_params=_cp("arbitrary"), name=name)(
            dy, xg, xg, h, h, p["conv_w"], p["conv_b"], p["w_ra"], p["w_rx"], p["b_ra"], p["b_rx"], p["lsl"])


def _place():
    x, y, c = lax.axis_index("x"), lax.axis_index("y"), lax.axis_index("c")
    chips = [(1 - x, y), (x, 1 - y), (1 - x, 1 - y)]
    return x, y, c, chips


def comm_ag(arrs):
    n = len(arrs)

    def copies(ins, outs, sems):
        send_sems, recv_sems, local_sems = sems
        x, y, c, chips = _place()
        me, sibling = (x, y, c), (x, y, 1 - c)

        def rows(a, px, py, pc):
            return outs[a].at[4 * px + 2 * py + pc]

        def copy(a, k, block, to, own=False):
            return pltpu.make_async_remote_copy(
                src_ref=ins[a] if own else rows(a, *block), dst_ref=rows(a, *block),
                send_sem=send_sems.at[k, a], recv_sem=recv_sems.at[k, a], device_id=to, device_id_type=MESH)

        mine = [pltpu.make_async_copy(ins[a], rows(a, *me), local_sems.at[a]) for a in range(n)]
        first = [copy(a, 1 + j, me, (*chip, c), own=True) for j, chip in enumerate(chips) for a in range(n)]
        first += [copy(a, 0, me, sibling, own=True) for a in range(n)]
        return copy, chips, c, me, sibling, mine, first

    def start(ins, outs, sems):
        _, _, _, _, _, mine, first = copies(ins, outs, sems)
        for cp in mine + first:
            cp.start()

    def finish(ins, outs, sems):
        copy, chips, c, me, sibling, mine, first = copies(ins, outs, sems)
        passed = []
        for j, chip in enumerate(chips):
            for a in range(n):
                copy(a, 1 + j, (*chip, c), me).wait_recv()
                fwd = copy(a, 4 + j, (*chip, c), sibling)
                fwd.start()
                passed.append(fwd)
        for a in range(n):
            copy(a, 0, sibling, me).wait_recv()
        for j, chip in enumerate(chips):
            for a in range(n):
                copy(a, 4 + j, (*chip, 1 - c), me).wait_recv()
        for cp in first + passed:
            cp.wait_send()
        for cp in mine:
            cp.wait()

    return Comm(arrs, [jax.ShapeDtypeStruct((N_DEV,) + p.shape, p.dtype) for p in arrs],
                [pltpu.SemaphoreType.DMA((7, n)), pltpu.SemaphoreType.DMA((7, n)), pltpu.SemaphoreType.DMA((n,))],
                start, finish)


def comm_pair(sends):
    n = len(sends)

    def copies(ins, outs, sems):
        send_sems, recv_sems = sems
        x, y, c, _ = _place()
        return [pltpu.make_async_remote_copy(
            src_ref=ins[a].at[k, 1 - c], dst_ref=outs[a].at[k], send_sem=send_sems.at[k, a], recv_sem=recv_sems.at[k, a],
            device_id=(x, y, 1 - c), device_id_type=MESH) for k in range(4) for a in range(n)]

    def start(ins, outs, sems):
        for cp in copies(ins, outs, sems):
            cp.start()

    def finish(ins, outs, sems):
        for cp in copies(ins, outs, sems):
            cp.wait()

    return Comm(sends, [jax.ShapeDtypeStruct((4,) + s.shape[2:], s.dtype) for s in sends],
                [pltpu.SemaphoreType.DMA((4, n)), pltpu.SemaphoreType.DMA((4, n))], start, finish)


def pair_sum(send, got, out_dtype, tr, name):
    _, _, R, C = send.shape
    c = lax.axis_index("c").astype(jnp.int32).reshape(1)

    def body(c_ref, s_ref, g_ref, o_ref):
        o_ref[...] = (s_ref[...] + g_ref[...]).astype(out_dtype)

    return pl.pallas_call(
        body,
        grid_spec=pltpu.PrefetchScalarGridSpec(
            num_scalar_prefetch=1, grid=(4, R // tr),
            in_specs=[pl.BlockSpec((None, None, tr, C), lambda k, i, cr: (k, cr[0], i, 0)),
                      pl.BlockSpec((None, tr, C), lambda k, i, cr: (k, i, 0))],
            out_specs=pl.BlockSpec((None, tr, C), lambda k, i, cr: (k, i, 0))),
        out_shape=jax.ShapeDtypeStruct((4, R, C), out_dtype),
        compiler_params=_cp("parallel", "parallel"), name=name)(c, send, got)


def comm_chip(items, bufs):
    ns = len(items)
    segs = [(i, b, s0, nr, d0) for i, (_, b, ranges) in enumerate(items) for (s0, nr, d0) in ranges]

    def copies(ins, outs, sems):
        send_sems, recv_sems, local_sems = sems
        x, y, c, chips = _place()
        mychip = 2 * x + y
        mine = [pltpu.make_async_copy(ins[i].at[mychip, pl.ds(s0, nr)], outs[b].at[mychip, pl.ds(d0, nr)], local_sems.at[q])
                for q, (i, b, s0, nr, d0) in enumerate(segs)]
        remote = [pltpu.make_async_remote_copy(
            src_ref=ins[i].at[2 * px + py, pl.ds(s0, nr)], dst_ref=outs[b].at[mychip, pl.ds(d0, nr)],
            send_sem=send_sems.at[j, q], recv_sem=recv_sems.at[j, q], device_id=(px, py, c), device_id_type=MESH)
            for j, (px, py) in enumerate(chips) for q, (i, b, s0, nr, d0) in enumerate(segs)]
        return mine, remote

    def start(ins, outs, sems):
        mine, remote = copies(ins, outs, sems)
        for cp in mine + remote:
            cp.start()

    def finish(ins, outs, sems):
        mine, remote = copies(ins, outs, sems)
        for cp in remote + mine:
            cp.wait()

    q = len(segs)
    return Comm([it[0] for it in items] + list(bufs), [jax.ShapeDtypeStruct(b.shape, b.dtype) for b in bufs],
                [pltpu.SemaphoreType.DMA((3, q)), pltpu.SemaphoreType.DMA((3, q)), pltpu.SemaphoreType.DMA((q,))],
                start, finish, aliases={ns + b: b for b in range(len(bufs))})


def adamw(parts, row0, w, m, v, tr, name):
    C = w.shape[-1]
    rows = w.size // C
    off = row0 // tr

    def body(p_ref, w_ref, m_ref, v_ref, g_ref, d_ref, nm_ref, nv_ref):
        g = ((p_ref[0].astype(F32) + p_ref[1].astype(F32)) + p_ref[2].astype(F32)) + p_ref[3].astype(F32)
        m2 = ADAM_B1 * m_ref[...] + (1.0 - ADAM_B1) * g
        v2 = ADAM_B2 * v_ref[...] + (1.0 - ADAM_B2) * (g * g)
        mh = m2 / (1.0 - ADAM_B1 ** ADAM_STEP)
        vh = v2 / (1.0 - ADAM_B2 ** ADAM_STEP)
        g_ref[...] = g
        d_ref[...] = -ADAM_LR * (mh / (jnp.sqrt(vh) + ADAM_EPS) + ADAM_WD * w_ref[...])
        nm_ref[...] = m2
        nv_ref[...] = v2

    row = pl.BlockSpec((tr, C), lambda i: (i, 0))
    shp = jax.ShapeDtypeStruct((rows, C), F32)
    outs = pl.pallas_call(
        body, grid=(rows // tr,),
        in_specs=[pl.BlockSpec((4, tr, C), lambda i: (0, off + i, 0)), row, row, row],
        out_specs=[row, row, row, row], out_shape=[shp, shp, shp, shp],
        compiler_params=_cp("parallel"), name=name)(parts, w.reshape(rows, C), m.reshape(rows, C), v.reshape(rows, C))
    return [o.reshape(w.shape) for o in outs]


def _full_from_gathered(flat, shape, axis):
    t = jnp.moveaxis(flat.reshape((N_DEV,) + shape), 0, axis)
    return t.reshape(shape[:axis] + (N_DEV * shape[axis],) + shape[axis + 1:])


def _shards_of_full(full, shape, axis):
    t = full.reshape(shape[:axis] + (N_DEV, shape[axis]) + shape[axis + 1:])
    return jnp.moveaxis(t, axis, 0).reshape(N_DEV, -1)


def _small_pack(args, prefix):
    flat = jnp.concatenate([args[prefix + n].reshape(-1) for n, _, _ in SMALL] + [args[prefix + "attn_sinks"].reshape(-1)])
    return jnp.pad(flat, (0, SM_ROWS * 1024 - flat.shape[0])).reshape(SM_ROWS, 1024)


def _small_unpack(pack):
    flat = pack.reshape(-1)
    out, off = {}, 0
    for n, shape, _ in SMALL:
        size = math.prod(shape)
        out[n] = flat[off:off + size].reshape(shape)
        off += size
    out["attn_sinks"] = flat[SINK_OFF:SINK_OFF + 32].reshape(2, 16)
    return out


def _tn_tile(n):
    return next(t for t in (1408, 1024, 768, 512, 256, 128) if n % t == 0)


def _dw(a, b, name, planes=1, shard=None, comm=None):
    return mm_tn(a, b, name, _tn_tile(a.shape[1]), _tn_tile(b.shape[-1]), planes=planes, shard=shard, comm=comm)


class _NoExchange:
    def __init__(self, layers):
        self.layers, self.grads = layers, [{} for _ in range(DEPTH)]

    def weights(self, l, piece):
        return self.layers[l]

    def fwd_slot(self):
        return None

    def fwd_done(self, outs):
        pass

    def loss_ready(self, loss):
        pass

    def bwd_slot(self, slot):
        return None

    def bwd_done(self, slot, outs):
        pass

    def grads_ready(self, l, part, g):
        self.grads[l].update(g)


def _local_step(x, target, sched, sinks):
    S = x.shape[0]
    tq = min(512, S)
    cosf, sinf = _rope_tables(S)
    saved = []
    h, hb = x, x.astype(BF16)
    for l in range(DEPTH):
        j = l // 2
        wl = {}
        sv = {"h0b": hb, "wl": wl}
        wl.update(sched.weights(l, "a"))
        (sv["gu1"], a1), co = ffn_up(hb, wl["w_in1"], f"ffn1_up_{l}", comm=sched.fwd_slot())
        sched.fwd_done(co)
        sv["a1"] = a1
        wl.update(sched.weights(l, "b"))
        (sv["z1"], h, hb), co = mm_ln(a1, _rows_view(wl["r_ff"], FF_SHARD, 0), h, wl["ln_g"][0], wl["ln_b"][0], 0.5,
                                      f"ffn1_down_ln_{l}", comm=sched.fwd_slot())
        sched.fwd_done(co)
        sv["h1b"] = hb
        if l % 2 == 0:
            qkv = mm_plain(hb, _mat_view(wl["m_c"]), F32, f"attn_qkv_{l}")
            qr, kr, vv = rope_fwd(qkv, cosf, sinf, f"rope_{l}")
            mix_in = attn_fwd(qr, kr, vv, sinks[j], f"attn_core_{l}", tq)
            sv.update(qr=qr, kr=kr, vv=vv)
        else:
            xg = mm_plain(hb, _mat_view(wl["m_c"]), F32, f"lru_in_{l}")
            mix_in, hstate = lru_fwd(xg, wl["lru"], f"lru_core_{l}")
            sv.update(xg=xg, hstate=hstate)
        sv["mix_in"] = mix_in
        (sv["z2"], h, hb), _ = mm_ln(mix_in, _rows_view(wl["m_o"], MIX_SHARD, 0), h, wl["ln_g"][1], wl["ln_b"][1], 1.0,
                                     f"mix_out_ln_{l}")
        sv["h2b"] = hb
        wl.update(sched.weights(l, "c"))
        (sv["gu2"], a2), co = ffn_up(hb, wl["w_in2"], f"ffn2_up_{l}", comm=sched.fwd_slot())
        sched.fwd_done(co)
        sv["a2"] = a2
        (sv["z3"], h, hb), co = mm_ln(a2, _rows_view(wl["r_ff"], FF_SHARD, FF_SHARD), h, wl["ln_g"][2], wl["ln_b"][2], 0.5,
                                      f"ffn2_down_ln_{l}", comm=sched.fwd_slot())
        sched.fwd_done(co)
        saved.append(sv)

    dy, lvec = loss_head(h, target, "loss_head")
    loss = lvec[0, 0]
    sched.loss_ready(loss)

    def hosted(slot, on, fn):
        comm = sched.bwd_slot(slot) if on else None
        out, co = fn(comm)
        if comm is not None:
            sched.bwd_done(slot, co)
        return out

    def ffn_bwd(dy, z, g, gu, a, xin_b, r_ff, blk, w_in, tag, host):
        dz, dzb, dg, db = ln_bwd(dy, z, g, 0.5, f"ln_bwd_{tag}")
        dh = hosted("pair", host, lambda cm: ffn_mid_bwd(dzb, r_ff, blk, gu, f"ffn_mid_bwd_{tag}", comm=cm))
        d_wout, _ = _dw(a, dzb, f"dw_out_{tag}")
        d_win = hosted("chip_x", host, lambda cm: _dw(xin_b, dh, f"dw_in_{tag}", planes=2, shard=2 * D_FF // N_DEV, comm=cm))
        dx = hosted("chip_y", host, lambda cm: mm_res2_nt(dh, _mat_view(w_in, D_FF, 0), _mat_view(w_in, D_FF, 1), dz, ALPHA,
                                                          f"ffn_dx_{tag}", comm=cm))
        return dx, dg, db, d_wout, d_win

    for l in reversed(range(DEPTH)):
        j = l // 2
        sv = saved[l]
        wl = sv["wl"]
        gl = {}
        dg, db = [None] * 3, [None] * 3
        dy, dg[2], db[2], gl["w_out2"], gl["w_in2"] = ffn_bwd(
            dy, sv["z3"], wl["ln_g"][2], sv["gu2"], sv["a2"], sv["h2b"], wl["r_ff"], 1, wl["w_in2"], f"2_{l}", True)
        dz, dzb, dg[1], db[1] = ln_bwd(dy, sv["z2"], wl["ln_g"][1], 1.0, f"ln_bwd_mix_{l}")
        w_mix = _rows_view(wl["m_o"], MIX_SHARD, 0)
        if l % 2 == 0:
            gl["mix_out"], _ = _dw(sv["mix_in"], dzb, f"dw_o_{l}")
            do = mm_plain(dzb, w_mix, BF16, f"attn_do_{l}", nt=True)
            dq, dkc, dkp, dvc, dvp, dsk = attn_bwd(sv["qr"], sv["kr"], sv["vv"], do, sinks[j], f"attn_core_bwd_{l}", tq)
            gl["sinks"] = dsk[:, :GROUP, 0].reshape(N_HEADS)
            dmid = rope_bwd(dq, dkc, dkp, dvc, dvp, cosf, sinf, f"rope_bwd_{l}", tq)
            gl["mix_in"], _ = _dw(sv["h1b"], dmid, f"dw_qkv_{l}", shard=QKV // N_DEV)
        else:
            gl["mix_out"], _ = _dw(sv["mix_in"], dzb, f"dw_lru_out_{l}")
            dyl = mm_plain(dzb, w_mix, F32, f"lru_dy_{l}", nt=True)
            dmid, dcw, dcb, dwra, dwrx, dbra, dbrx, dlam = lru_bwd(dyl, sv["xg"], sv["hstate"], wl["lru"], f"lru_core_bwd_{l}")
            gl.update(conv_w=dcw, conv_b=dcb[0], w_ra=dwra, w_rx=dwrx, b_ra=dbra[0], b_rx=dbrx[0],
                      lam=dlam[0] * wl["lru"]["sig_neg"])
            gl["mix_in"], _ = _dw(sv["h1b"], dmid, f"dw_lru_in_{l}", shard=2 * D_RNN // N_DEV)
        dy = mm_res_nt(dmid, _mat_view(wl["m_c"]), dz, ALPHA, f"mix_dx_{l}")
        sched.grads_ready(l, "hi", gl)
        lo = {}
        dy, dg[0], db[0], lo["w_out1"], lo["w_in1"] = ffn_bwd(
            dy, sv["z1"], wl["ln_g"][0], sv["gu1"], sv["a1"], sv["h0b"], wl["r_ff"], 0, wl["w_in1"], f"1_{l}", True)
        lo["ln_g"], lo["ln_b"] = jnp.concatenate(dg, axis=0), jnp.concatenate(db, axis=0)
        sched.grads_ready(l, "lo", lo)
    return loss, dy


def _lru_params(full, j, w_ra, w_rx):
    lam = full["lru_lambda"][j]
    return {
        "conv_w": full["lru_conv_w"][j], "conv_b": full["lru_conv_b"][j].reshape(1, -1),
        "w_ra": w_ra, "w_rx": w_rx,
        "b_ra": full["lru_b_ra"][j].reshape(1, -1), "b_rx": full["lru_b_rx"][j].reshape(1, -1),
        "lsl": jax.nn.log_sigmoid(lam).reshape(1, -1), "sig_neg": jax.nn.sigmoid(-lam),
    }


def _row_shards(f):
    return f.reshape(N_DEV, -1, f.shape[1])


def _gate_shards(w):
    return w.reshape(RNN_BLOCKS, N_DEV, RNN_W // N_DEV, RNN_W).transpose(1, 0, 2, 3).reshape(N_DEV, -1, RNN_W)


class _Fsdp:
    def __init__(self, args):
        self.args = args
        self.b16 = lambda a: a.astype(BF16)
        self.sm = _small_pack(args, "")
        self.sent = {l: self._ag_arrays(l) for l in range(DEPTH)}
        self.queue = [(l, p) for l in range(DEPTH) for p in "abc"][1:]
        self.hosting = None
        got = run_comm(comm_ag(self.sent[0]["a"] + [self.sm]), "all_gather_first")
        self.raw = {0: {"a": got[:-1]}}
        gflat = got[-1].reshape(N_DEV, SM_ROWS * 1024)
        self.small, off = {}, 0
        for n, shape, axis in SMALL:
            size = math.prod(shape)
            self.small[n] = _full_from_gathered(gflat[:, off:off + size], shape, axis)
            off += size
        self.bufs = {"r": jnp.zeros((4, R_ROWS, D_MODEL), BF16), "c704": jnp.zeros((4, 2 * DEPTH * D_MODEL, 704), BF16),
                     "c192": jnp.zeros((4, 2 * D_MODEL, 192), BF16), "c256": jnp.zeros((4, C256_ROWS, RNN_W), BF16),
                     "sm": jnp.zeros((4, SM_ROWS, 1024), F32)}
        self.pending = None
        self.small_g = [{} for _ in range(DEPTH)]
        self.loss = None

    def _ag_arrays(self, l):
        a, b16, j = self.args, self.b16, l // 2
        r_ff = jnp.concatenate([b16(a["ffn1_w_out"][l]), b16(a["ffn2_w_out"][l])], axis=0)
        if l % 2 == 0:
            return {"a": [b16(a["ffn1_w_in"][l]), b16(a["attn_w_qkv"][j])], "b": [r_ff, b16(a["attn_w_o"][j])],
                    "c": [b16(a["ffn2_w_in"][l])]}
        gates = jnp.concatenate([b16(a["lru_w_ra"][j]).reshape(-1, RNN_W), b16(a["lru_w_rx"][j]).reshape(-1, RNN_W)], axis=0)
        return {"a": [b16(a["ffn1_w_in"][l]), b16(a["lru_w_in"][j])], "b": [r_ff, b16(a["lru_w_out"][j]), gates],
                "c": [b16(a["ffn2_w_in"][l])]}

    def weights(self, l, piece):
        raw, j = self.raw[l][piece], l // 2
        full = lambda g: g.transpose(1, 0, 2).reshape(g.shape[1], -1)
        if piece == "a":
            return {"w_in1": full(raw[0]), "m_c": full(raw[1]), "ln_g": self.small["ln_g"][l], "ln_b": self.small["ln_b"][l]}
        if piece == "c":
            return {"w_in2": full(raw[0])}
        wl = {"r_ff": raw[0], "m_o": raw[1]}
        if l % 2:
            g = raw[2].reshape(N_DEV, 2, RNN_BLOCKS, RNN_W // N_DEV, RNN_W).transpose(1, 2, 0, 3, 4)
            g = g.reshape(2, RNN_BLOCKS, RNN_W, RNN_W)
            wl["lru"] = _lru_params(self.small, j, g[0], g[1])
        return wl

    def fwd_slot(self):
        if not self.queue:
            return None
        self.hosting = self.queue.pop(0)
        l, piece = self.hosting
        return comm_ag(self.sent[l][piece])

    def fwd_done(self, outs):
        if self.hosting is not None:
            l, piece = self.hosting
            self.raw.setdefault(l, {})[piece] = outs
            self.hosting = None

    def loss_ready(self, loss):
        self.loss = loss

    def _ranges(self, l, part):
        j = l // 2
        if part == "lo":
            return [("r", [(0, FF_SHARD, l * FF_SHARD)]), ("c704", [(0, D_MODEL, l * D_MODEL)])]
        r = [(0, FF_SHARD, R_FFN2 + l * FF_SHARD), (FF_SHARD, MIX_SHARD, (R_LRU if l % 2 else R_ATTN) + j * MIX_SHARD)]
        c704 = [(0, D_MODEL, (DEPTH + l) * D_MODEL)]
        if l % 2:
            gr = RNN_BLOCKS * (RNN_W // N_DEV)
            mix = ("c256", [(0, D_MODEL, j * D_MODEL), (D_MODEL, gr, C256_RA + j * gr), (D_MODEL + gr, gr, C256_RX + j * gr)])
        else:
            mix = ("c192", [(0, D_MODEL, j * D_MODEL)])
        return [("r", r), ("c704", c704), mix]

    def grads_ready(self, l, part, g):
        self.small_g[l].update(g)
        if part == "lo":
            ts = [_row_shards(g["w_out1"]), g["w_in1"]]
        else:
            tmix = g["mix_in"]
            if l % 2:
                tmix = jnp.concatenate([tmix, _gate_shards(g["w_ra"]), _gate_shards(g["w_rx"])], axis=1)
            ts = [jnp.concatenate([_row_shards(g["w_out2"]), _row_shards(g["mix_out"])], axis=1), g["w_in2"], tmix]
        sends = [t.reshape((4, 2) + t.shape[1:]) for t in ts]
        if (l, part) == (0, "lo"):
            self._last(sends)
        else:
            self.pending = {"tag": f"{l}{part}", "sends": sends, "where": self._ranges(l, part)}

    def _pair_sums(self, tag, sends, gots):
        tiles = (None, 512, 256)
        return [pair_sum(s, g, BF16, t or s.shape[2] // 2, f"pair_sum_{nm}_{tag}")
                for s, g, t, nm in zip(sends, gots, tiles, ("r", "c704", "mix"))]

    def _chip(self, parts, where, pick):
        names = [where[i][0] for i in pick]
        return names, comm_chip([(parts[i], k, where[i][1]) for k, i in enumerate(pick)], [self.bufs[n] for n in names])

    def bwd_slot(self, slot):
        p = self.pending
        if p is None:
            return None
        if slot == "pair":
            return comm_pair(p["sends"])
        pick = [1] if slot == "chip_x" else [i for i in range(len(p["sends"])) if i != 1]
        p["names"], comm = self._chip(p["parts"], p["where"], pick)
        return comm

    def bwd_done(self, slot, outs):
        p = self.pending
        if slot == "pair":
            p["parts"] = self._pair_sums(p["tag"], p["sends"], outs)
            return
        for n, o in zip(p["names"], outs):
            self.bufs[n] = o
        if slot == "chip_y":
            self.pending = None

    def _last(self, sends):
        sg = self.small_g
        stack = lambda key, ls: jnp.stack([sg[l][key] for l in ls])
        every, lru = range(DEPTH), (1, 3)
        small = {"ln_g": stack("ln_g", every), "ln_b": stack("ln_b", every), "lru_conv_w": stack("conv_w", lru),
                 "lru_conv_b": stack("conv_b", lru), "lru_b_ra": stack("b_ra", lru), "lru_b_rx": stack("b_rx", lru),
                 "lru_lambda": stack("lam", lru)}
        tail = jnp.concatenate([stack("sinks", (0, 2)).reshape(-1), self.loss.reshape(1)])
        tail = jnp.pad(tail, (0, SM_ROWS * 1024 - SMALL_N - tail.shape[0]))
        s_sm = jnp.concatenate([_shards_of_full(small[n], shape, axis) for n, shape, axis in SMALL]
                               + [jnp.broadcast_to(tail, (N_DEV, tail.shape[0]))], axis=1).reshape(4, 2, SM_ROWS, 1024)
        gots = run_comm(comm_pair(sends + [s_sm]), "pair_exchange_last")
        parts = self._pair_sums("0lo", sends, gots[:2]) + [pair_sum(s_sm, gots[2], F32, SM_ROWS, "pair_sum_sm")]
        names, comm = self._chip(parts, self._ranges(0, "lo") + [("sm", [(0, SM_ROWS, 0)])], [0, 1, 2])
        for n, o in zip(names, run_comm(comm, "chip_exchange_last")):
            self.bufs[n] = o


def kernel(x, ffn1_w_in, ffn1_w_out, ffn2_w_in, ffn2_w_out, ln_g, ln_b, attn_w_qkv, attn_sinks, attn_w_o, lru_w_in, lru_conv_w, lru_conv_b, lru_w_ra, lru_b_ra, lru_w_rx, lru_b_rx, lru_lambda, lru_w_out, loss_target, m_ffn1_w_in, m_ffn1_w_out, m_ffn2_w_in, m_ffn2_w_out, m_ln_g, m_ln_b, m_attn_w_qkv, m_attn_sinks, m_attn_w_o, m_lru_w_in, m_lru_conv_w, m_lru_conv_b, m_lru_w_ra, m_lru_b_ra, m_lru_w_rx, m_lru_b_rx, m_lru_lambda, m_lru_w_out, v_ffn1_w_in, v_ffn1_w_out, v_ffn2_w_in, v_ffn2_w_out, v_ln_g, v_ln_b, v_attn_w_qkv, v_attn_sinks, v_attn_w_o, v_lru_w_in, v_lru_conv_w, v_lru_conv_b, v_lru_w_ra, v_lru_b_ra, v_lru_w_rx, v_lru_b_rx, v_lru_lambda, v_lru_w_out):
    args = dict(locals())

    sched = _Fsdp(args)
    _, dx = _local_step(x[0], loss_target[0], sched, attn_sinks)
    p_r, p704, p192, p256, p_sm = (sched.bufs[n] for n in ("r", "c704", "c192", "c256", "sm"))
    sm = sched.sm

    fam = lambda n: (args[n], args["m_" + n], args["v_" + n])
    res = {
        "ffn1_w_out": adamw(p_r, 0, *fam("ffn1_w_out"), FF_SHARD, "adamw_ffn1_w_out"),
        "ffn2_w_out": adamw(p_r, R_FFN2, *fam("ffn2_w_out"), FF_SHARD, "adamw_ffn2_w_out"),
        "attn_w_o": adamw(p_r, R_ATTN, *fam("attn_w_o"), 2 * MIX_SHARD, "adamw_attn_w_o"),
        "lru_w_out": adamw(p_r, R_LRU, *fam("lru_w_out"), 2 * MIX_SHARD, "adamw_lru_w_out"),
        "ffn1_w_in": adamw(p704, 0, *fam("ffn1_w_in"), 512, "adamw_ffn1_w_in"),
        "ffn2_w_in": adamw(p704, DEPTH * D_MODEL, *fam("ffn2_w_in"), 512, "adamw_ffn2_w_in"),
        "attn_w_qkv": adamw(p192, 0, *fam("attn_w_qkv"), 512, "adamw_attn_w_qkv"),
        "lru_w_in": adamw(p256, 0, *fam("lru_w_in"), 512, "adamw_lru_w_in"),
        "lru_w_ra": adamw(p256, C256_RA, *fam("lru_w_ra"), 256, "adamw_lru_w_ra"),
        "lru_w_rx": adamw(p256, C256_RX, *fam("lru_w_rx"), 256, "adamw_lru_w_rx"),
    }
    sm_out = adamw(p_sm, 0, sm, _small_pack(args, "m_"), _small_pack(args, "v_"), SM_ROWS, "adamw_small")
    for k, pack in enumerate(sm_out):
        for n, val in _small_unpack(pack).items():
            res.setdefault(n, [None] * 4)[k] = val
    loss_total = sm_out[0].reshape(-1)[LOSS_OFF]
    out = [loss_total, dx[None]]
    for k in range(4):
        out += [res[n][k] for n in WEIGHTS]
    return tuple(out)
```

```python
import math

import jax
import jax.numpy as jnp
import numpy as np
from jax import lax
from jax.experimental import pallas as pl
from jax.experimental.pallas import tpu as pltpu

F32 = jnp.float32
BF16 = jnp.bfloat16

D_MODEL = 1024
DEPTH = 4
N_HEADS = 16
N_KV = 4
HEAD_DIM = 64
GROUP = 4
BLOCK = 128
ROPE_THETA = 10000.0
D_RNN = 1024
RNN_BLOCKS = 4
RNN_W = 256
CONV_W = 4
LRU_C = 8.0
D_FF = 2816
ALPHA = (2.0 * DEPTH) ** 0.25
LN_EPS = 1e-5
QKV = (N_HEADS + 2 * N_KV) * HEAD_DIM
N_DEV = 8

ADAM_LR = 0.001
ADAM_B1 = 0.9
ADAM_B2 = 0.999
ADAM_EPS = 1e-08
ADAM_WD = 0.01
ADAM_STEP = 10

VMEM_LIMIT = 52 * 1024 * 1024
NEG = float(np.finfo(np.float32).min)

MESH = pl.DeviceIdType.MESH
ANY = pl.BlockSpec(memory_space=pl.ANY)

FF_SHARD = D_FF // N_DEV
MIX_SHARD = D_MODEL // N_DEV
R_FFN2 = DEPTH * FF_SHARD
R_ATTN = 2 * DEPTH * FF_SHARD
R_LRU = R_ATTN + 2 * MIX_SHARD
R_ROWS = R_LRU + 2 * MIX_SHARD
C256_RA = 2 * D_MODEL
C256_RX = C256_RA + 2 * RNN_BLOCKS * (RNN_W // N_DEV)
C256_ROWS = C256_RX + 2 * RNN_BLOCKS * (RNN_W // N_DEV)

SMALL = (
    ("ln_g", (4, 3, 128), 2),
    ("ln_b", (4, 3, 128), 2),
    ("lru_conv_w", (2, 4, 128), 2),
    ("lru_conv_b", (2, 128), 1),
    ("lru_b_ra", (2, 128), 1),
    ("lru_b_rx", (2, 128), 1),
    ("lru_lambda", (2, 128), 1),
)
WEIGHTS = ("ffn1_w_in", "ffn1_w_out", "ffn2_w_in", "ffn2_w_out", "ln_g", "ln_b", "attn_w_qkv", "attn_sinks",
           "attn_w_o", "lru_w_in", "lru_conv_w", "lru_conv_b", "lru_w_ra", "lru_b_ra", "lru_w_rx", "lru_b_rx",
           "lru_lambda", "lru_w_out")
SMALL_N = sum(math.prod(s) for _, s, _ in SMALL)
SINK_OFF = SMALL_N
LOSS_OFF = SMALL_N + 32
SM_ROWS = 8


def _cp(*sem):
    return pltpu.CompilerParams(dimension_semantics=sem, vmem_limit_bytes=VMEM_LIMIT)


def _dot(a, b):
    return jnp.dot(a, b, preferred_element_type=F32)


def _dot_tn(a, b):
    return lax.dot_general(a, b, (((0,), (0,)), ((), ())), preferred_element_type=F32)


def _dot_nt(a, b):
    return lax.dot_general(a, b, (((1,), (1,)), ((), ())), preferred_element_type=F32)


def _col_chunks(cols, size=512):
    return [slice(c, min(c + size, cols)) for c in range(0, cols, size)]


def _ln(z, g, b):
    mu = jnp.mean(z, axis=-1, keepdims=True)
    xc = z - mu
    var = jnp.mean(xc * xc, axis=-1, keepdims=True)
    return xc * lax.rsqrt(var + LN_EPS) * g + b


def _rows_view(pack, rows_per_dev, row0):
    return (pack, (N_DEV, rows_per_dev, D_MODEL), (0, row0 // rows_per_dev, 0), (N_DEV * rows_per_dev, D_MODEL))


def _mat_view(arr, cols=None, cblk=0):
    k, n = arr.shape
    cols = n if cols is None else cols
    return (arr, (k, cols), (0, cblk), (k, cols))


def _vspec(view):
    _, bshape, bidx, _ = view
    return pl.BlockSpec(bshape, lambda *_: bidx)


def _vload(view, ref):
    return ref[...].reshape(view[3])


class Comm:
    def __init__(self, ins, out_shapes, sems, start, finish, aliases=None):
        self.ins, self.out_shapes, self.sems = list(ins), list(out_shapes), list(sems)
        self.start, self.finish, self.aliases = start, finish, dict(aliases or {})


def _call(body, *, grid, in_specs, out_specs, out_shape, operands, name, sem, scratch=(), comm=None):
    n_in, n_out, n_scr = len(in_specs), len(out_specs), len(scratch)
    if comm is None:
        return pl.pallas_call(body, grid=grid, in_specs=list(in_specs), out_specs=list(out_specs),
                              out_shape=list(out_shape), scratch_shapes=list(scratch), compiler_params=_cp(*sem),
                              name=name)(*operands), []
    nci, nco = len(comm.ins), len(comm.out_shapes)

    def hosted(*refs):
        ins, cins = refs[:n_in], refs[n_in:n_in + nci]
        o0 = n_in + nci
        outs, couts = refs[o0:o0 + n_out], refs[o0 + n_out:o0 + n_out + nco]
        s0 = o0 + n_out + nco
        scr, csems = refs[s0:s0 + n_scr], refs[s0 + n_scr:]
        first = last = None
        for ax, size in enumerate(grid):
            pid = pl.program_id(ax)
            f, e = pid == 0, pid == size - 1
            first = f if first is None else jnp.logical_and(first, f)
            last = e if last is None else jnp.logical_and(last, e)

        @pl.when(first)
        def _():
            comm.start(cins, couts, csems)

        body(*ins, *outs, *scr)

        @pl.when(last)
        def _():
            comm.finish(cins, couts, csems)

    res = pl.pallas_call(
        hosted, grid=grid, in_specs=list(in_specs) + [ANY] * nci, out_specs=list(out_specs) + [ANY] * nco,
        out_shape=list(out_shape) + comm.out_shapes, scratch_shapes=list(scratch) + comm.sems,
        input_output_aliases={n_in + i: n_out + o for i, o in comm.aliases.items()},
        compiler_params=_cp(*(("arbitrary",) * len(grid))), name=name)(*operands, *comm.ins)
    return res[:n_out], res[n_out:]


def run_comm(comm, name):
    nci, nco = len(comm.ins), len(comm.out_shapes)

    def body(*refs):
        cins, couts, csems = refs[:nci], refs[nci:nci + nco], refs[nci + nco:]
        comm.start(cins, couts, csems)
        comm.finish(cins, couts, csems)

    return pl.pallas_call(
        body, in_specs=[ANY] * nci, out_specs=[ANY] * nco, out_shape=comm.out_shapes, scratch_shapes=comm.sems,
        input_output_aliases=dict(comm.aliases), name=name)(*comm.ins)


def mm_plain(a, wv, out_dtype, name, nt=False, tm=512):
    S, K = a.shape
    N = wv[3][0] if nt else wv[3][1]
    tm = min(tm, S)
    dot = _dot_nt if nt else _dot

    def body(a_ref, w_ref, o_ref):
        o_ref[...] = dot(a_ref[...], _vload(wv, w_ref)).astype(out_dtype)

    return pl.pallas_call(
        body, grid=(S // tm,),
        in_specs=[pl.BlockSpec((tm, K), lambda i: (i, 0)), _vspec(wv)],
        out_specs=pl.BlockSpec((tm, N), lambda i: (i, 0)),
        out_shape=jax.ShapeDtypeStruct((S, N), out_dtype),
        compiler_params=_cp("parallel"), name=name)(a, wv[0])


def mm_res_nt(a, wv, r, alpha, name, tm=512):
    S, K = a.shape
    N = wv[3][0]
    tm = min(tm, S)

    def body(a_ref, w_ref, r_ref, o_ref):
        o_ref[...] = _dot_nt(a_ref[...], _vload(wv, w_ref)) + alpha * r_ref[...]

    return pl.pallas_call(
        body, grid=(S // tm,),
        in_specs=[pl.BlockSpec((tm, K), lambda i: (i, 0)), _vspec(wv), pl.BlockSpec((tm, N), lambda i: (i, 0))],
        out_specs=pl.BlockSpec((tm, N), lambda i: (i, 0)),
        out_shape=jax.ShapeDtypeStruct((S, N), F32),
        compiler_params=_cp("parallel"), name=name)(a, wv[0], r)


def mm_res2_nt(a3, wv0, wv1, r, alpha, name, tm=512, comm=None):
    _, S, K = a3.shape
    N = wv0[3][0]
    tm = min(tm, S)

    def body(a_ref, w0_ref, w1_ref, r_ref, o_ref):
        o_ref[...] = (_dot_nt(a_ref[0], _vload(wv0, w0_ref)) + _dot_nt(a_ref[1], _vload(wv1, w1_ref))
                      + alpha * r_ref[...])

    (out,), couts = _call(
        body, grid=(S // tm,),
        in_specs=[pl.BlockSpec((2, tm, K), lambda i: (0, i, 0)), _vspec(wv0), _vspec(wv1),
                  pl.BlockSpec((tm, N), lambda i: (i, 0))],
        out_specs=[pl.BlockSpec((tm, N), lambda i: (i, 0))],
        out_shape=[jax.ShapeDtypeStruct((S, N), F32)],
        operands=(a3, wv0[0], wv1[0], r), name=name, sem=("parallel",), comm=comm)
    return out, couts


def mm_ln(a, wv, h, g, b, scale, name, tm=512, comm=None):
    S, K = a.shape
    tm = min(tm, S)

    def body(a_ref, w_ref, h_ref, g_ref, b_ref, z_ref, y_ref, yb_ref):
        z = ALPHA * h_ref[...] + scale * _dot(a_ref[...], _vload(wv, w_ref))
        y = _ln(z, g_ref[...], b_ref[...])
        z_ref[...] = z
        y_ref[...] = y
        yb_ref[...] = y.astype(BF16)

    row = pl.BlockSpec((tm, D_MODEL), lambda i: (i, 0))
    vec = pl.BlockSpec((1, D_MODEL), lambda i: (0, 0))
    return _call(
        body, grid=(S // tm,),
        in_specs=[pl.BlockSpec((tm, K), lambda i: (i, 0)), _vspec(wv), row, vec, vec],
        out_specs=[row, row, row],
        out_shape=[jax.ShapeDtypeStruct((S, D_MODEL), F32), jax.ShapeDtypeStruct((S, D_MODEL), F32),
                   jax.ShapeDtypeStruct((S, D_MODEL), BF16)],
        operands=(a, wv[0], h, g.reshape(1, -1), b.reshape(1, -1)), name=name, sem=("parallel",), comm=comm)


def ffn_up(xb, w_in, name, tm=512, tn=1408, comm=None):
    S = xb.shape[0]
    tm = min(tm, S)
    nj = D_FF // tn

    def body(x_ref, wg_ref, wu_ref, gu_ref, a_ref):
        x = x_ref[...]
        g = _dot(x, wg_ref[...])
        u = _dot(x, wu_ref[...])
        gu_ref[0] = g.astype(BF16)
        gu_ref[1] = u.astype(BF16)
        a_ref[...] = (g * jax.nn.sigmoid(g) * u).astype(BF16)

    return _call(
        body, grid=(nj, S // tm),
        in_specs=[pl.BlockSpec((tm, D_MODEL), lambda j, i: (i, 0)),
                  pl.BlockSpec((D_MODEL, tn), lambda j, i: (0, j)),
                  pl.BlockSpec((D_MODEL, tn), lambda j, i: (0, nj + j))],
        out_specs=[pl.BlockSpec((2, tm, tn), lambda j, i: (0, i, j)), pl.BlockSpec((tm, tn), lambda j, i: (i, j))],
        out_shape=[jax.ShapeDtypeStruct((2, S, D_FF), BF16), jax.ShapeDtypeStruct((S, D_FF), BF16)],
        operands=(xb, w_in, w_in), name=name, sem=("parallel", "parallel"), comm=comm)


def ffn_mid_bwd(dfb, r_ff, blk, gu, name, tm=512, comm=None):
    S = dfb.shape[0]
    tm = min(tm, S)
    tn = 4 * FF_SHARD

    def body(df_ref, w_ref, gu_ref, dh_ref):
        w = w_ref[...].reshape(tn, D_MODEL)
        df = df_ref[...]
        for cols in _col_chunks(tn):
            da = _dot_nt(df, w[cols, :])
            g = gu_ref[0, :, cols].astype(F32)
            u = gu_ref[1, :, cols].astype(F32)
            sg = jax.nn.sigmoid(g)
            t = g * sg
            dh_ref[0, :, cols] = (da * u * (sg * ((g - t) + 1.0))).astype(BF16)
            dh_ref[1, :, cols] = (da * t).astype(BF16)

    gspec = pl.BlockSpec((2, tm, tn), lambda j, i: (0, i, j))
    (out,), couts = _call(
        body, grid=(2, S // tm),
        in_specs=[pl.BlockSpec((tm, D_MODEL), lambda j, i: (i, 0)),
                  pl.BlockSpec((4, FF_SHARD, D_MODEL), lambda j, i: (j, blk, 0)), gspec],
        out_specs=[gspec],
        out_shape=[jax.ShapeDtypeStruct((2, S, D_FF), BF16)],
        operands=(dfb, r_ff, gu), name=name, sem=("parallel", "parallel"), comm=comm)
    return out, couts


def mm_tn(a, b, name, tm, tn, ts=2048, planes=1, shard=None, comm=None):
    S, M = a.shape
    N = b.shape[-1] * planes
    ts = min(ts, S)
    per = b.shape[-1] // tn
    ns = S // ts

    if shard is None:
        def body(a_ref, b_ref, o_ref):
            @pl.when(pl.program_id(2) == 0)
            def _():
                o_ref[...] = jnp.zeros_like(o_ref)

            o_ref[...] += _dot_tn(a_ref[...], b_ref[...])

        out_spec = pl.BlockSpec((tm, tn), lambda i, j, s: (i, j))
        out_shape = jax.ShapeDtypeStruct((M, N), F32)
        scratch = ()
    else:
        def body(a_ref, b_ref, o_ref, acc_ref):
            s = pl.program_id(2)

            @pl.when(s == 0)
            def _():
                acc_ref[...] = jnp.zeros_like(acc_ref)

            acc_ref[...] += _dot_tn(a_ref[...], b_ref[...])

            @pl.when(s == ns - 1)
            def _():
                for q in range(tn // shard):
                    o_ref[q] = acc_ref[:, shard * q:shard * (q + 1)]

        out_spec = pl.BlockSpec((tn // shard, tm, shard), lambda i, j, s: (j, i, 0))
        out_shape = jax.ShapeDtypeStruct((N // shard, M, shard), F32)
        scratch = (pltpu.VMEM((tm, tn), F32),)

    if planes == 1:
        bspec = pl.BlockSpec((ts, tn), lambda i, j, s: (s, j))
    else:
        bspec = pl.BlockSpec((None, ts, tn), lambda i, j, s: (j // per, s, j % per))
    (out,), couts = _call(
        body, grid=(M // tm, N // tn, ns),
        in_specs=[pl.BlockSpec((ts, tm), lambda i, j, s: (s, i)), bspec],
        out_specs=[out_spec], out_shape=[out_shape], scratch=scratch,
        operands=(a, b), name=name, sem=("parallel", "parallel", "arbitrary"), comm=comm)
    return out, couts


def ln_bwd(dy, z, g, out_scale, name, tm=512):
    S = dy.shape[0]
    tm = min(tm, S)

    def body(dy_ref, z_ref, g_ref, dz_ref, dzb_ref, dg_ref, db_ref):
        @pl.when(pl.program_id(0) == 0)
        def _():
            dg_ref[...] = jnp.zeros_like(dg_ref)
            db_ref[...] = jnp.zeros_like(db_ref)

        z = z_ref[...]
        dy_ = dy_ref[...]
        mu = jnp.mean(z, axis=-1, keepdims=True)
        xc = z - mu
        var = jnp.mean(xc * xc, axis=-1, keepdims=True)
        rstd = lax.rsqrt(var + LN_EPS)
        xh = xc * rstd
        dxh = dy_ * g_ref[...]
        m1 = jnp.mean(dxh, axis=-1, keepdims=True)
        m2 = jnp.mean(dxh * xh, axis=-1, keepdims=True)
        dz = rstd * (dxh - m1 - xh * m2)
        dz_ref[...] = dz
        dzb_ref[...] = (out_scale * dz).astype(BF16)
        dg_ref[...] += jnp.sum(dy_ * xh, axis=0, keepdims=True)
        db_ref[...] += jnp.sum(dy_, axis=0, keepdims=True)

    row = pl.BlockSpec((tm, D_MODEL), lambda i: (i, 0))
    vec = pl.BlockSpec((1, D_MODEL), lambda i: (0, 0))
    return pl.pallas_call(
        body, grid=(S // tm,),
        in_specs=[row, row, vec],
        out_specs=[row, row, vec, vec],
        out_shape=[jax.ShapeDtypeStruct((S, D_MODEL), F32), jax.ShapeDtypeStruct((S, D_MODEL), BF16),
                   jax.ShapeDtypeStruct((1, D_MODEL), F32), jax.ShapeDtypeStruct((1, D_MODEL), F32)],
        compiler_params=_cp("arbitrary"), name=name)(dy, z, g.reshape(1, -1))


def loss_head(y, t, name, tm=512):
    S = y.shape[0]
    tm = min(tm, S)
    nt = S // tm

    def body(y_ref, t_ref, dy_ref, l_ref):
        i = pl.program_id(0)

        @pl.when(i == 0)
        def _():
            l_ref[...] = jnp.zeros_like(l_ref)

        e = y_ref[...] - t_ref[...]
        dy_ref[...] = e * (1.0 / D_MODEL)
        l_ref[...] += jnp.sum(e * e, axis=0, keepdims=True)

        @pl.when(i == nt - 1)
        def _():
            tot = jnp.sum(l_ref[...], axis=1, keepdims=True) * (0.5 / D_MODEL)
            l_ref[...] = jnp.broadcast_to(tot, l_ref.shape)

    row = pl.BlockSpec((tm, D_MODEL), lambda i: (i, 0))
    vec = pl.BlockSpec((1, D_MODEL), lambda i: (0, 0))
    return pl.pallas_call(
        body, grid=(nt,), in_specs=[row, row], out_specs=[row, vec],
        out_shape=[jax.ShapeDtypeStruct((S, D_MODEL), F32), jax.ShapeDtypeStruct((1, D_MODEL), F32)],
        compiler_params=_cp("arbitrary"), name=name)(y, t)


def _rope_tables(S):
    pos = jnp.arange(S, dtype=F32)
    inv_freq = ROPE_THETA ** (-jnp.arange(0, HEAD_DIM, 2, dtype=F32) / HEAD_DIM)
    ang = pos[:, None] * inv_freq[None, :]
    cos, sin = jnp.cos(ang), jnp.sin(ang)
    cosf = jnp.concatenate([cos, cos, cos, cos], axis=1)
    sinf = jnp.concatenate([-sin, sin, -sin, sin], axis=1)
    return cosf, sinf


def _rot(t, c, s, first):
    sw = jnp.where(first, pltpu.roll(t, 96, 1), pltpu.roll(t, 32, 1))
    return t * c + sw * s


def rope_fwd(qkv, cosf, sinf, name, tm=512):
    S = qkv.shape[0]
    tm = min(tm, S)

    def body(x_ref, c_ref, s_ref, q_ref, k_ref, v_ref):
        c = c_ref[...]
        s = s_ref[...]
        first = (lax.broadcasted_iota(jnp.int32, (tm, 128), 1) % HEAD_DIM) < (HEAD_DIM // 2)
        for j in range(8):
            q_ref[:, 128 * j:128 * (j + 1)] = _rot(x_ref[:, 128 * j:128 * (j + 1)], c, s, first).astype(BF16)
        for j in range(2):
            k_ref[:, 128 * j:128 * (j + 1)] = _rot(x_ref[:, 1024 + 128 * j:1024 + 128 * (j + 1)], c, s, first).astype(BF16)
        v_ref[...] = x_ref[:, 1280:1536].astype(BF16)

    tab = pl.BlockSpec((tm, 128), lambda i: (i, 0))
    return pl.pallas_call(
        body, grid=(S // tm,),
        in_specs=[pl.BlockSpec((tm, QKV), lambda i: (i, 0)), tab, tab],
        out_specs=[pl.BlockSpec((tm, 1024), lambda i: (i, 0)), pl.BlockSpec((tm, 256), lambda i: (i, 0)),
                   pl.BlockSpec((tm, 256), lambda i: (i, 0))],
        out_shape=[jax.ShapeDtypeStruct((S, 1024), BF16), jax.ShapeDtypeStruct((S, 256), BF16),
                   jax.ShapeDtypeStruct((S, 256), BF16)],
        compiler_params=_cp("parallel"), name=name)(qkv, cosf, sinf)


def rope_bwd(dq, dkc, dkp, dvc, dvp, cosf, sinf, name, tq):
    S = dq.shape[0]
    nt = S // tq

    def body(dq_ref, dkc_ref, dkp_ref, dvc_ref, dvp_ref, c_ref, s_ref, o_ref):
        i = pl.program_id(0)
        c = c_ref[...]
        s = -s_ref[...]
        first = (lax.broadcasted_iota(jnp.int32, (tq, 128), 1) % HEAD_DIM) < (HEAD_DIM // 2)
        for j in range(8):
            o_ref[:, 128 * j:128 * (j + 1)] = _rot(dq_ref[:, 128 * j:128 * (j + 1)], c, s, first).astype(BF16)
        has_next = i < nt - 1
        rows = lax.broadcasted_iota(jnp.int32, (tq, 256), 0)
        pad = jnp.zeros((tq - BLOCK, 256), F32)
        halo_k = jnp.concatenate([pad, dkp_ref[...]], axis=0)
        halo_v = jnp.concatenate([pad, dvp_ref[...]], axis=0)
        use = jnp.logical_and(has_next, rows >= tq - BLOCK)
        dk = dkc_ref[...] + jnp.where(use, halo_k, 0.0)
        dv = dvc_ref[...] + jnp.where(use, halo_v, 0.0)
        for j in range(2):
            o_ref[:, 1024 + 128 * j:1024 + 128 * (j + 1)] = _rot(dk[:, 128 * j:128 * (j + 1)], c, s, first).astype(BF16)
        o_ref[:, 1280:1536] = dv.astype(BF16)

    tab = pl.BlockSpec((tq, 128), lambda i: (i, 0))
    cur = pl.BlockSpec((tq, 256), lambda i: (i, 0))
    nxt = pl.BlockSpec((BLOCK, 256), lambda i: (jnp.minimum(i + 1, nt - 1), 0))
    return pl.pallas_call(
        body, grid=(nt,),
        in_specs=[pl.BlockSpec((tq, 1024), lambda i: (i, 0)), cur, nxt, cur, nxt, tab, tab],
        out_specs=pl.BlockSpec((tq, QKV), lambda i: (i, 0)),
        out_shape=jax.ShapeDtypeStruct((S, QKV), BF16),
        compiler_params=_cp("parallel"), name=name)(dq, dkc, dkp, dvc, dvp, cosf, sinf)


def _attn_masks(n):
    shape = (GROUP * BLOCK, 2 * BLOCK)
    r = lax.broadcasted_iota(jnp.int32, shape, 0) % BLOCK
    c = lax.broadcasted_iota(jnp.int32, shape, 1)
    ok = jnp.logical_and(c > r, c <= r + BLOCK)
    return jnp.logical_and(ok, jnp.logical_or(c >= BLOCK, n > 0)), ok


def _attn_probs(qs, kw, sink_col, ok):
    s = jnp.where(ok, _dot_nt(qs, kw) * (HEAD_DIM ** -0.5), NEG)
    m = jnp.maximum(jnp.max(s, axis=1, keepdims=True), sink_col)
    p = jnp.exp(s - m)
    es = jnp.exp(sink_col - m)
    inv = 1.0 / (jnp.sum(p, axis=1, keepdims=True) + es)
    return p * inv, es * inv


def _sink_col(sink_ref, g):
    rid = lax.broadcasted_iota(jnp.int32, (GROUP * BLOCK, 1), 0) // BLOCK
    col = jnp.zeros((GROUP * BLOCK, 1), F32)
    for j in range(GROUP):
        col = jnp.where(rid == j, sink_ref[GROUP * g + j], col)
    return col


KV_PAIR = 2
Q_LANES = KV_PAIR * GROUP * HEAD_DIM


def _head_rows(ref, lo, gi):
    base = gi * GROUP * HEAD_DIM
    return jnp.concatenate([ref[lo:lo + BLOCK, base + HEAD_DIM * j:base + HEAD_DIM * (j + 1)] for j in range(GROUP)], axis=0)


def _kv_window(kc_ref, kp_ref, b, gi):
    ks = slice(HEAD_DIM * gi, HEAD_DIM * (gi + 1))
    if b == 0:
        return jnp.concatenate([kp_ref[:, ks], kc_ref[0:BLOCK, ks]], axis=0)
    return kc_ref[BLOCK * (b - 1):BLOCK * (b + 1), ks]


def _attn_specs(tq):
    nsub = tq // BLOCK
    qspec = pl.BlockSpec((tq, Q_LANES), lambda p, n: (n, p))
    cur = pl.BlockSpec((tq, KV_PAIR * HEAD_DIM), lambda p, n: (n, p))
    prev = pl.BlockSpec((BLOCK, KV_PAIR * HEAD_DIM), lambda p, n: (jnp.maximum(n * nsub - 1, 0), p))
    return qspec, cur, prev


def attn_fwd(q, k, v, sinks, name, tq):
    S = q.shape[0]
    nsub = tq // BLOCK

    def body(sink_ref, q_ref, kc_ref, kp_ref, vc_ref, vp_ref, o_ref):
        p = pl.program_id(0)
        n = pl.program_id(1)
        ok_first, ok_rest = _attn_masks(n)
        for gi in range(KV_PAIR):
            sink_col = _sink_col(sink_ref, KV_PAIR * p + gi)
            base = gi * GROUP * HEAD_DIM
            for b in range(nsub):
                lo = BLOCK * b
                qs = _head_rows(q_ref, lo, gi)
                pn, _ = _attn_probs(qs, _kv_window(kc_ref, kp_ref, b, gi), sink_col, ok_rest if b else ok_first)
                o = _dot(pn.astype(BF16), _kv_window(vc_ref, vp_ref, b, gi))
                for j in range(GROUP):
                    o_ref[lo:lo + BLOCK, base + HEAD_DIM * j:base + HEAD_DIM * (j + 1)] = (
                        o[BLOCK * j:BLOCK * (j + 1)].astype(BF16))

    qspec, cur, prev = _attn_specs(tq)
    return pl.pallas_call(
        body, grid=(N_KV // KV_PAIR, S // tq),
        in_specs=[pl.BlockSpec(memory_space=pltpu.SMEM), qspec, cur, prev, cur, prev],
        out_specs=qspec,
        out_shape=jax.ShapeDtypeStruct((S, N_HEADS * HEAD_DIM), BF16),
        compiler_params=_cp("parallel", "parallel"), name=name)(sinks, q, k, k, v, v)


def attn_bwd(q, k, v, do, sinks, name, tq):
    S = q.shape[0]
    nsub = tq // BLOCK
    nt = S // tq

    def body(sink_ref, q_ref, kc_ref, kp_ref, vc_ref, vp_ref, do_ref, dq_ref, dkc_ref, dkp_ref, dvc_ref, dvp_ref, ds_ref):
        p = pl.program_id(0)
        n = pl.program_id(1)

        @pl.when(n == 0)
        def _():
            ds_ref[...] = jnp.zeros_like(ds_ref)

        dkc_ref[...] = jnp.zeros_like(dkc_ref)
        dvc_ref[...] = jnp.zeros_like(dvc_ref)
        rid = lax.broadcasted_iota(jnp.int32, (GROUP * BLOCK, 1), 0) // BLOCK
        sub = lax.broadcasted_iota(jnp.int32, (8, 128), 0)
        ok_first, ok_rest = _attn_masks(n)
        for gi in range(KV_PAIR):
            sink_col = _sink_col(sink_ref, KV_PAIR * p + gi)
            base = gi * GROUP * HEAD_DIM
            ks = slice(HEAD_DIM * gi, HEAD_DIM * (gi + 1))
            dsink = jnp.zeros((8, 128), F32)
            for b in range(nsub):
                lo = BLOCK * b
                qs = _head_rows(q_ref, lo, gi)
                dos = _head_rows(do_ref, lo, gi)
                kw = _kv_window(kc_ref, kp_ref, b, gi)
                vw = _kv_window(vc_ref, vp_ref, b, gi)
                pn, ps = _attn_probs(qs, kw, sink_col, ok_rest if b else ok_first)
                dp = _dot_nt(dos, vw)
                delta = jnp.sum(pn * dp, axis=1, keepdims=True)
                dsb = (pn * (dp - delta) * (HEAD_DIM ** -0.5)).astype(BF16)
                dq = _dot(dsb, kw)
                for j in range(GROUP):
                    dq_ref[lo:lo + BLOCK, base + HEAD_DIM * j:base + HEAD_DIM * (j + 1)] = dq[BLOCK * j:BLOCK * (j + 1)]
                dkw = _dot_tn(dsb, qs)
                dvw = _dot_tn(pn.astype(BF16), dos)
                if b == 0:
                    dkp_ref[:, ks] = dkw[0:BLOCK]
                    dvp_ref[:, ks] = dvw[0:BLOCK]
                else:
                    dkc_ref[lo - BLOCK:lo, ks] += dkw[0:BLOCK]
                    dvc_ref[lo - BLOCK:lo, ks] += dvw[0:BLOCK]
                dkc_ref[lo:lo + BLOCK, ks] += dkw[BLOCK:2 * BLOCK]
                dvc_ref[lo:lo + BLOCK, ks] += dvw[BLOCK:2 * BLOCK]
                sd = ps * delta
                for j in range(GROUP):
                    tot = -jnp.sum(jnp.where(rid == j, sd, 0.0))
                    dsink = dsink + jnp.where(sub == j, tot, 0.0)
            ds_ref[gi] += dsink

    qspec, cur, prev = _attn_specs(tq)
    halo = pl.BlockSpec((BLOCK, KV_PAIR * HEAD_DIM), lambda p, n: (n, p))
    kv_shape = jax.ShapeDtypeStruct((S, N_KV * HEAD_DIM), F32)
    halo_shape = jax.ShapeDtypeStruct((nt * BLOCK, N_KV * HEAD_DIM), F32)
    return pl.pallas_call(
        body, grid=(N_KV // KV_PAIR, nt),
        in_specs=[pl.BlockSpec(memory_space=pltpu.SMEM), qspec, cur, prev, cur, prev, qspec],
        out_specs=[qspec, cur, halo, cur, halo, pl.BlockSpec((KV_PAIR, 8, 128), lambda p, n: (p, 0, 0))],
        out_shape=[jax.ShapeDtypeStruct((S, N_HEADS * HEAD_DIM), F32), kv_shape, halo_shape, kv_shape, halo_shape,
                   jax.ShapeDtypeStruct((N_KV, 8, 128), F32)],
        compiler_params=_cp("parallel", "arbitrary"), name=name)(sinks, q, k, k, v, v, do)


def _rows_before(cur, prev8, k):
    if k == 0:
        return cur
    n = cur.shape[0]
    ext = jnp.concatenate([prev8, cur], axis=0)
    return ext[8 - k:8 - k + n]


def _rows_after(cur, next8, k):
    if k == 0:
        return cur
    n = cur.shape[0]
    ext = jnp.concatenate([cur, next8], axis=0)
    return ext[k:k + n]


def _gelu(x):
    c = math.sqrt(2.0 / math.pi)
    t = jnp.tanh(c * (x + 0.044715 * (x * x * x)))
    return 0.5 * (1.0 + t), t


def _neg_expm1(u):
    ser = 1.0 + u * (1.0 / 6.0)
    for k in range(5, 1, -1):
        ser = 1.0 + (u * (1.0 / k)) * ser
    return jnp.where(u > -0.125, -(u * ser), 1.0 - jnp.exp(u))


def _block_diag(xb16, w_ref):
    return jnp.concatenate([_dot(xb16[:, RNN_W * n:RNN_W * (n + 1)], w_ref[n]) for n in range(RNN_BLOCKS)], axis=1)


def _lru_gates(xb, prev8, cw_ref, cb_ref, wra_ref, wrx_ref, bra_ref, brx_ref, lsl_ref):
    xc = cb_ref[...] + cw_ref[3:4, :] * xb
    for w in range(CONV_W - 1):
        xc = xc + cw_ref[w:w + 1, :] * _rows_before(xb, prev8, CONV_W - 1 - w)
    xcb = xc.astype(BF16)
    r = jax.nn.sigmoid(_block_diag(xcb, wra_ref) + bra_ref[...])
    ig = jax.nn.sigmoid(_block_diag(xcb, wrx_ref) + brx_ref[...])
    la = LRU_C * r * lsl_ref[...]
    a = jnp.exp(la)
    sq = jnp.sqrt(_neg_expm1(2.0 * la))
    return xc, xcb, r, ig, a, sq


def lru_fwd(xg, p, name, tm=256):
    S = xg.shape[0]
    tm = min(tm, S)

    def body(xg_ref, xp_ref, cw_ref, cb_ref, wra_ref, wrx_ref, bra_ref, brx_ref, lsl_ref, y_ref, h_ref, hc_ref, a_s, b_s):
        i = pl.program_id(0)
        xb = xg_ref[:, 0:D_RNN]
        gb = xg_ref[:, D_RNN:2 * D_RNN]
        prev8 = jnp.where(i > 0, xp_ref[:, 0:D_RNN], 0.0)
        xc, _, r, ig, a, sq = _lru_gates(xb, prev8, cw_ref, cb_ref, wra_ref, wrx_ref, bra_ref, brx_ref, lsl_ref)
        a_s[...] = a
        b_s[...] = sq * (ig * xc)

        @pl.when(i == 0)
        def _():
            hc_ref[...] = jnp.zeros_like(hc_ref)

        def chunk(c, h):
            o = pl.multiple_of(c * 8, 8)
            av = a_s[pl.ds(o, 8), :]
            bv = b_s[pl.ds(o, 8), :]
            rows = []
            for t in range(8):
                h = av[t:t + 1, :] * h + bv[t:t + 1, :]
                rows.append(h)
            h_ref[pl.ds(o, 8), :] = jnp.concatenate(rows, axis=0)
            return h

        h_last = lax.fori_loop(0, tm // 8, chunk, hc_ref[0:1, :])
        hc_ref[0:1, :] = h_last
        cdf, _ = _gelu(gb)
        y_ref[...] = (h_ref[...] * (gb * cdf)).astype(BF16)

    vec = pl.BlockSpec((1, D_RNN), lambda i: (0, 0))
    wsp = pl.BlockSpec((RNN_BLOCKS, RNN_W, RNN_W), lambda i: (0, 0, 0))
    return pl.pallas_call(
        body, grid=(S // tm,),
        in_specs=[pl.BlockSpec((tm, 2 * D_RNN), lambda i: (i, 0)),
                  pl.BlockSpec((8, 2 * D_RNN), lambda i: (jnp.maximum(i * (tm // 8) - 1, 0), 0)),
                  pl.BlockSpec((CONV_W, D_RNN), lambda i: (0, 0)), vec, wsp, wsp, vec, vec, vec],
        out_specs=[pl.BlockSpec((tm, D_RNN), lambda i: (i, 0)), pl.BlockSpec((tm, D_RNN), lambda i: (i, 0))],
        out_shape=[jax.ShapeDtypeStruct((S, D_RNN), BF16), jax.ShapeDtypeStruct((S, D_RNN), F32)],
        scratch_shapes=[pltpu.VMEM((8, D_RNN), F32), pltpu.VMEM((tm, D_RNN), F32), pltpu.VMEM((tm, D_RNN), F32)],
        compiler_params=_cp("arbitrary"), name=name)(
            xg, xg, p["conv_w"], p["conv_b"], p["w_ra"], p["w_rx"], p["b_ra"], p["b_rx"], p["lsl"])


def lru_bwd(dy, xg, h, p, name, tm=256):
    S = xg.shape[0]
    tm = min(tm, S)
    nt = S // tm

    def body(dy_ref, xg_ref, xp_ref, h_ref, hp_ref, cw_ref, cb_ref, wra_ref, wrx_ref, bra_ref, brx_ref,
             lsl_ref, dxg_ref, dcw_ref, dcb_ref, dwra_ref, dwrx_ref, dbra_ref, dbrx_ref, dlam_ref,
             lc_ref, nx_ref, a_s, g_s, l_s):
        i = pl.program_id(0)
        ti = nt - 1 - i

        @pl.when(i == 0)
        def _():
            lc_ref[...] = jnp.zeros_like(lc_ref)
            nx_ref[...] = jnp.zeros_like(nx_ref)
            for ref in (dcw_ref, dcb_ref, dwra_ref, dwrx_ref, dbra_ref, dbrx_ref, dlam_ref):
                ref[...] = jnp.zeros_like(ref)

        xb = xg_ref[:, 0:D_RNN]
        gb = xg_ref[:, D_RNN:2 * D_RNN]
        prev8 = jnp.where(ti > 0, xp_ref[:, 0:D_RNN], 0.0)
        xc, xcb, r, ig, a, sq = _lru_gates(xb, prev8, cw_ref, cb_ref, wra_ref, wrx_ref, bra_ref, brx_ref, lsl_ref)
        hh = h_ref[...]
        hprev = _rows_before(hh, jnp.where(ti > 0, hp_ref[...], 0.0), 1)
        dy_ = dy_ref[...]
        cdf, th = _gelu(gb)
        c0 = math.sqrt(2.0 / math.pi)
        dgate = cdf + gb * (0.5 * (1.0 - th * th) * c0 * (1.0 + 3.0 * 0.044715 * gb * gb))
        dgb = dy_ * hh * dgate
        a_s[...] = a
        g_s[...] = dy_ * (gb * cdf)

        def chunk(cc, carry):
            o = pl.multiple_of((tm // 8 - 1 - cc) * 8, 8)
            av = a_s[pl.ds(o, 8), :]
            gv = g_s[pl.ds(o, 8), :]
            rows = [None] * 8
            for t in range(7, -1, -1):
                lam_t = gv[t:t + 1, :] + carry
                rows[t] = lam_t
                carry = av[t:t + 1, :] * lam_t
            l_s[pl.ds(o, 8), :] = jnp.concatenate(rows, axis=0)
            return carry

        carry = lax.fori_loop(0, tm // 8, chunk, lc_ref[0:1, :])
        lc_ref[0:1, :] = carry
        lam = l_s[...]
        da = lam * hprev
        dixc = lam * sq
        di = dixc * xc
        dxc = dixc * ig
        dsq = lam * (ig * xc)
        dla = da * a - dsq * (a * a / sq)
        dr = dla * (LRU_C * lsl_ref[...])
        dlam_ref[...] += jnp.sum(dla * (LRU_C * r), axis=0, keepdims=True)
        dpr = dr * r * (1.0 - r)
        dpi = di * ig * (1.0 - ig)
        dbra_ref[...] += jnp.sum(dpr, axis=0, keepdims=True)
        dbrx_ref[...] += jnp.sum(dpi, axis=0, keepdims=True)
        dprb = dpr.astype(BF16)
        dpib = dpi.astype(BF16)
        back = []
        for n in range(RNN_BLOCKS):
            sl = slice(RNN_W * n, RNN_W * (n + 1))
            dwra_ref[n] += _dot_tn(xcb[:, sl], dprb[:, sl])
            dwrx_ref[n] += _dot_tn(xcb[:, sl], dpib[:, sl])
            back.append(_dot_nt(dprb[:, sl], wra_ref[n]) + _dot_nt(dpib[:, sl], wrx_ref[n]))
        dxc = dxc + jnp.concatenate(back, axis=1)
        dcb_ref[...] += jnp.sum(dxc, axis=0, keepdims=True)
        next8 = nx_ref[...]
        dxb = cw_ref[3:4, :] * dxc
        dcw_ref[3:4, :] += jnp.sum(dxc * xb, axis=0, keepdims=True)
        for w in range(CONV_W - 1):
            k = CONV_W - 1 - w
            dcw_ref[w:w + 1, :] += jnp.sum(dxc * _rows_before(xb, prev8, k), axis=0, keepdims=True)
            dxb = dxb + cw_ref[w:w + 1, :] * _rows_after(dxc, next8, k)
        nx_ref[...] = dxc[0:8, :]
        dxg_ref[:, 0:D_RNN] = dxb.astype(BF16)
        dxg_ref[:, D_RNN:2 * D_RNN] = dgb.astype(BF16)

    rev = lambda i: (nt - 1 - i, 0)
    before = lambda i: (jnp.maximum((nt - 1 - i) * (tm // 8) - 1, 0), 0)
    vec = pl.BlockSpec((1, D_RNN), lambda i: (0, 0))
    wsp = pl.BlockSpec((RNN_BLOCKS, RNN_W, RNN_W), lambda i: (0, 0, 0))
    cwsp = pl.BlockSpec((CONV_W, D_RNN), lambda i: (0, 0))
    return pl.pallas_call(
        body, grid=(nt,),
        in_specs=[pl.BlockSpec((tm, D_RNN), rev), pl.BlockSpec((tm, 2 * D_RNN), rev), pl.BlockSpec((8, 2 * D_RNN), before),
                  pl.BlockSpec((tm, D_RNN), rev), pl.BlockSpec((8, D_RNN), before),
                  cwsp, vec, wsp, wsp, vec, vec, vec],
        out_specs=[pl.BlockSpec((tm, 2 * D_RNN), rev), cwsp, vec, wsp, wsp, vec, vec, vec],
        out_shape=[jax.ShapeDtypeStruct((S, 2 * D_RNN), BF16), jax.ShapeDtypeStruct((CONV_W, D_RNN), F32),
                   jax.ShapeDtypeStruct((1, D_RNN), F32), jax.ShapeDtypeStruct((RNN_BLOCKS, RNN_W, RNN_W), F32),
                   jax.ShapeDtypeStruct((RNN_BLOCKS, RNN_W, RNN_W), F32), jax.ShapeDtypeStruct((1, D_RNN), F32),
                   jax.ShapeDtypeStruct((1, D_RNN), F32), jax.ShapeDtypeStruct((1, D_RNN), F32)],
        scratch_shapes=[pltpu.VMEM((8, D_RNN), F32), pltpu.VMEM((8, D_RNN), F32), pltpu.VMEM((tm, D_RNN), F32),
                        pltpu.VMEM((tm, D_RNN), F32), pltpu.VMEM((tm, D_RNN), F32)],
        compiler_params=_cp("arbitrary"), name=name)(
            dy, xg, xg, h, h, p["conv_w"], p["conv_b"], p["w_ra"], p["w_rx"], p["b_ra"], p["b_rx"], p["lsl"])


def _place():
    x, y, c = lax.axis_index("x"), lax.axis_index("y"), lax.axis_index("c")
    chips = [(1 - x, y), (x, 1 - y), (1 - x, 1 - y)]
    return x, y, c, chips


def comm_ag(arrs):
    n = len(arrs)

    def copies(ins, outs, sems):
        send_sems, recv_sems, local_sems = sems
        x, y, c, chips = _place()
        me, sibling = (x, y, c), (x, y, 1 - c)

        def rows(a, px, py, pc):
            return outs[a].at[4 * px + 2 * py + pc]

        def copy(a, k, block, to, own=False):
            return pltpu.make_async_remote_copy(
                src_ref=ins[a] if own else rows(a, *block), dst_ref=rows(a, *block),
                send_sem=send_sems.at[k, a], recv_sem=recv_sems.at[k, a], device_id=to, device_id_type=MESH)

        mine = [pltpu.make_async_copy(ins[a], rows(a, *me), local_sems.at[a]) for a in range(n)]
        first = [copy(a, 1 + j, me, (*chip, c), own=True) for j, chip in enumerate(chips) for a in range(n)]
        first += [copy(a, 0, me, sibling, own=True) for a in range(n)]
        return copy, chips, c, me, sibling, mine, first

    def start(ins, outs, sems):
        _, _, _, _, _, mine, first = copies(ins, outs, sems)
        for cp in mine + first:
            cp.start()

    def finish(ins, outs, sems):
        copy, chips, c, me, sibling, mine, first = copies(ins, outs, sems)
        passed = []
        for j, chip in enumerate(chips):
            for a in range(n):
                copy(a, 1 + j, (*chip, c), me).wait_recv()
                fwd = copy(a, 4 + j, (*chip, c), sibling)
                fwd.start()
                passed.append(fwd)
        for a in range(n):
            copy(a, 0, sibling, me).wait_recv()
        for j, chip in enumerate(chips):
            for a in range(n):
                copy(a, 4 + j, (*chip, 1 - c), me).wait_recv()
        for cp in first + passed:
            cp.wait_send()
        for cp in mine:
            cp.wait()

    return Comm(arrs, [jax.ShapeDtypeStruct((N_DEV,) + p.shape, p.dtype) for p in arrs],
                [pltpu.SemaphoreType.DMA((7, n)), pltpu.SemaphoreType.DMA((7, n)), pltpu.SemaphoreType.DMA((n,))],
                start, finish)


def comm_pair(sends):
    n = len(sends)

    def copies(ins, outs, sems):
        send_sems, recv_sems = sems
        x, y, c, _ = _place()
        return [pltpu.make_async_remote_copy(
            src_ref=ins[a].at[k, 1 - c], dst_ref=outs[a].at[k], send_sem=send_sems.at[k, a], recv_sem=recv_sems.at[k, a],
            device_id=(x, y, 1 - c), device_id_type=MESH) for k in range(4) for a in range(n)]

    def start(ins, outs, sems):
        for cp in copies(ins, outs, sems):
            cp.start()

    def finish(ins, outs, sems):
        for cp in copies(ins, outs, sems):
            cp.wait()

    return Comm(sends, [jax.ShapeDtypeStruct((4,) + s.shape[2:], s.dtype) for s in sends],
                [pltpu.SemaphoreType.DMA((4, n)), pltpu.SemaphoreType.DMA((4, n))], start, finish)


def pair_sum(send, got, out_dtype, tr, name):
    _, _, R, C = send.shape
    c = lax.axis_index("c").astype(jnp.int32).reshape(1)

    def body(c_ref, s_ref, g_ref, o_ref):
        o_ref[...] = (s_ref[...] + g_ref[...]).astype(out_dtype)

    return pl.pallas_call(
        body,
        grid_spec=pltpu.PrefetchScalarGridSpec(
            num_scalar_prefetch=1, grid=(4, R // tr),
            in_specs=[pl.BlockSpec((None, None, tr, C), lambda k, i, cr: (k, cr[0], i, 0)),
                      pl.BlockSpec((None, tr, C), lambda k, i, cr: (k, i, 0))],
            out_specs=pl.BlockSpec((None, tr, C), lambda k, i, cr: (k, i, 0))),
        out_shape=jax.ShapeDtypeStruct((4, R, C), out_dtype),
        compiler_params=_cp("parallel", "parallel"), name=name)(c, send, got)


def comm_chip(items, bufs):
    ns = len(items)
    segs = [(i, b, s0, nr, d0) for i, (_, b, ranges) in enumerate(items) for (s0, nr, d0) in ranges]

    def copies(ins, outs, sems):
        send_sems, recv_sems, local_sems = sems
        x, y, c, chips = _place()
        mychip = 2 * x + y
        mine = [pltpu.make_async_copy(ins[i].at[mychip, pl.ds(s0, nr)], outs[b].at[mychip, pl.ds(d0, nr)], local_sems.at[q])
                for q, (i, b, s0, nr, d0) in enumerate(segs)]
        remote = [pltpu.make_async_remote_copy(
            src_ref=ins[i].at[2 * px + py, pl.ds(s0, nr)], dst_ref=outs[b].at[mychip, pl.ds(d0, nr)],
            send_sem=send_sems.at[j, q], recv_sem=recv_sems.at[j, q], device_id=(px, py, c), device_id_type=MESH)
            for j, (px, py) in enumerate(chips) for q, (i, b, s0, nr, d0) in enumerate(segs)]
        return mine, remote

    def start(ins, outs, sems):
        mine, remote = copies(ins, outs, sems)
        for cp in mine + remote:
            cp.start()

    def finish(ins, outs, sems):
        mine, remote = copies(ins, outs, sems)
        for cp in remote + mine:
            cp.wait()

    q = len(segs)
    return Comm([it[0] for it in items] + list(bufs), [jax.ShapeDtypeStruct(b.shape, b.dtype) for b in bufs],
                [pltpu.SemaphoreType.DMA((3, q)), pltpu.SemaphoreType.DMA((3, q)), pltpu.SemaphoreType.DMA((q,))],
                start, finish, aliases={ns + b: b for b in range(len(bufs))})


def adamw(parts, row0, w, m, v, tr, name):
    C = w.shape[-1]
    rows = w.size // C
    off = row0 // tr

    def body(p_ref, w_ref, m_ref, v_ref, g_ref, d_ref, nm_ref, nv_ref):
        g = ((p_ref[0].astype(F32) + p_ref[1].astype(F32)) + p_ref[2].astype(F32)) + p_ref[3].astype(F32)
        m2 = ADAM_B1 * m_ref[...] + (1.0 - ADAM_B1) * g
        v2 = ADAM_B2 * v_ref[...] + (1.0 - ADAM_B2) * (g * g)
        mh = m2 / (1.0 - ADAM_B1 ** ADAM_STEP)
        vh = v2 / (1.0 - ADAM_B2 ** ADAM_STEP)
        g_ref[...] = g
        d_ref[...] = -ADAM_LR * (mh / (jnp.sqrt(vh) + ADAM_EPS) + ADAM_WD * w_ref[...])
        nm_ref[...] = m2
        nv_ref[...] = v2

    row = pl.BlockSpec((tr, C), lambda i: (i, 0))
    shp = jax.ShapeDtypeStruct((rows, C), F32)
    outs = pl.pallas_call(
        body, grid=(rows // tr,),
        in_specs=[pl.BlockSpec((4, tr, C), lambda i: (0, off + i, 0)), row, row, row],
        out_specs=[row, row, row, row], out_shape=[shp, shp, shp, shp],
        compiler_params=_cp("parallel"), name=name)(parts, w.reshape(rows, C), m.reshape(rows, C), v.reshape(rows, C))
    return [o.reshape(w.shape) for o in outs]


def _full_from_gathered(flat, shape, axis):
    t = jnp.moveaxis(flat.reshape((N_DEV,) + shape), 0, axis)
    return t.reshape(shape[:axis] + (N_DEV * shape[axis],) + shape[axis + 1:])


def _shards_of_full(full, shape, axis):
    t = full.reshape(shape[:axis] + (N_DEV, shape[axis]) + shape[axis + 1:])
    return jnp.moveaxis(t, axis, 0).reshape(N_DEV, -1)


def _small_pack(args, prefix):
    flat = jnp.concatenate([args[prefix + n].reshape(-1) for n, _, _ in SMALL] + [args[prefix + "attn_sinks"].reshape(-1)])
    return jnp.pad(flat, (0, SM_ROWS * 1024 - flat.shape[0])).reshape(SM_ROWS, 1024)


def _small_unpack(pack):
    flat = pack.reshape(-1)
    out, off = {}, 0
    for n, shape, _ in SMALL:
        size = math.prod(shape)
        out[n] = flat[off:off + size].reshape(shape)
        off += size
    out["attn_sinks"] = flat[SINK_OFF:SINK_OFF + 32].reshape(2, 16)
    return out


def _tn_tile(n):
    return next(t for t in (1408, 1024, 768, 512, 256, 128) if n % t == 0)


def _dw(a, b, name, planes=1, shard=None, comm=None):
    return mm_tn(a, b, name, _tn_tile(a.shape[1]), _tn_tile(b.shape[-1]), planes=planes, shard=shard, comm=comm)


class _NoExchange:
    def __init__(self, layers):
        self.layers, self.grads = layers, [{} for _ in range(DEPTH)]

    def weights(self, l, piece):
        return self.layers[l]

    def fwd_slot(self):
        return None

    def fwd_done(self, outs):
        pass

    def loss_ready(self, loss):
        pass

    def bwd_slot(self, slot):
        return None

    def bwd_done(self, slot, outs):
        pass

    def grads_ready(self, l, part, g):
        self.grads[l].update(g)


def _local_step(x, target, sched, sinks):
    S = x.shape[0]
    tq = min(512, S)
    cosf, sinf = _rope_tables(S)
    saved = []
    h, hb = x, x.astype(BF16)
    for l in range(DEPTH):
        j = l // 2
        wl = {}
        sv = {"h0b": hb, "wl": wl}
        wl.update(sched.weights(l, "a"))
        (sv["gu1"], a1), co = ffn_up(hb, wl["w_in1"], f"ffn1_up_{l}", comm=sched.fwd_slot())
        sched.fwd_done(co)
        sv["a1"] = a1
        wl.update(sched.weights(l, "b"))
        (sv["z1"], h, hb), co = mm_ln(a1, _rows_view(wl["r_ff"], FF_SHARD, 0), h, wl["ln_g"][0], wl["ln_b"][0], 0.5,
                                      f"ffn1_down_ln_{l}", comm=sched.fwd_slot())
        sched.fwd_done(co)
        sv["h1b"] = hb
        if l % 2 == 0:
            qkv = mm_plain(hb, _mat_view(wl["m_c"]), F32, f"attn_qkv_{l}")
            qr, kr, vv = rope_fwd(qkv, cosf, sinf, f"rope_{l}")
            mix_in = attn_fwd(qr, kr, vv, sinks[j], f"attn_core_{l}", tq)
            sv.update(qr=qr, kr=kr, vv=vv)
        else:
            xg = mm_plain(hb, _mat_view(wl["m_c"]), F32, f"lru_in_{l}")
            mix_in, hstate = lru_fwd(xg, wl["lru"], f"lru_core_{l}")
            sv.update(xg=xg, hstate=hstate)
        sv["mix_in"] = mix_in
        (sv["z2"], h, hb), _ = mm_ln(mix_in, _rows_view(wl["m_o"], MIX_SHARD, 0), h, wl["ln_g"][1], wl["ln_b"][1], 1.0,
                                     f"mix_out_ln_{l}")
        sv["h2b"] = hb
        wl.update(sched.weights(l, "c"))
        (sv["gu2"], a2), co = ffn_up(hb, wl["w_in2"], f"ffn2_up_{l}", comm=sched.fwd_slot())
        sched.fwd_done(co)
        sv["a2"] = a2
        (sv["z3"], h, hb), co = mm_ln(a2, _rows_view(wl["r_ff"], FF_SHARD, FF_SHARD), h, wl["ln_g"][2], wl["ln_b"][2], 0.5,
                                      f"ffn2_down_ln_{l}", comm=sched.fwd_slot())
        sched.fwd_done(co)
        saved.append(sv)

    dy, lvec = loss_head(h, target, "loss_head")
    loss = lvec[0, 0]
    sched.loss_ready(loss)

    def hosted(slot, on, fn):
        comm = sched.bwd_slot(slot) if on else None
        out, co = fn(comm)
        if comm is not None:
            sched.bwd_done(slot, co)
        return out

    def ffn_bwd(dy, z, g, gu, a, xin_b, r_ff, blk, w_in, tag, host):
        dz, dzb, dg, db = ln_bwd(dy, z, g, 0.5, f"ln_bwd_{tag}")
        dh = hosted("pair", host, lambda cm: ffn_mid_bwd(dzb, r_ff, blk, gu, f"ffn_mid_bwd_{tag}", comm=cm))
        d_wout, _ = _dw(a, dzb, f"dw_out_{tag}")
        d_win = hosted("chip_x", host, lambda cm: _dw(xin_b, dh, f"dw_in_{tag}", planes=2, shard=2 * D_FF // N_DEV, comm=cm))
        dx = hosted("chip_y", host, lambda cm: mm_res2_nt(dh, _mat_view(w_in, D_FF, 0), _mat_view(w_in, D_FF, 1), dz, ALPHA,
                                                          f"ffn_dx_{tag}", comm=cm))
        return dx, dg, db, d_wout, d_win

    for l in reversed(range(DEPTH)):
        j = l // 2
        sv = saved[l]
        wl = sv["wl"]
        gl = {}
        dg, db = [None] * 3, [None] * 3
        dy, dg[2], db[2], gl["w_out2"], gl["w_in2"] = ffn_bwd(
            dy, sv["z3"], wl["ln_g"][2], sv["gu2"], sv["a2"], sv["h2b"], wl["r_ff"], 1, wl["w_in2"], f"2_{l}", True)
        dz, dzb, dg[1], db[1] = ln_bwd(dy, sv["z2"], wl["ln_g"][1], 1.0, f"ln_bwd_mix_{l}")
        w_mix = _rows_view(wl["m_o"], MIX_SHARD, 0)
        if l % 2 == 0:
            gl["mix_out"], _ = _dw(sv["mix_in"], dzb, f"dw_o_{l}")
            do = mm_plain(dzb, w_mix, BF16, f"attn_do_{l}", nt=True)
            dq, dkc, dkp, dvc, dvp, dsk = attn_bwd(sv["qr"], sv["kr"], sv["vv"], do, sinks[j], f"attn_core_bwd_{l}", tq)
            gl["sinks"] = dsk[:, :GROUP, 0].reshape(N_HEADS)
            dmid = rope_bwd(dq, dkc, dkp, dvc, dvp, cosf, sinf, f"rope_bwd_{l}", tq)
            gl["mix_in"], _ = _dw(sv["h1b"], dmid, f"dw_qkv_{l}", shard=QKV // N_DEV)
        else:
            gl["mix_out"], _ = _dw(sv["mix_in"], dzb, f"dw_lru_out_{l}")
            dyl = mm_plain(dzb, w_mix, F32, f"lru_dy_{l}", nt=True)
            dmid, dcw, dcb, dwra, dwrx, dbra, dbrx, dlam = lru_bwd(dyl, sv["xg"], sv["hstate"], wl["lru"], f"lru_core_bwd_{l}")
            gl.update(conv_w=dcw, conv_b=dcb[0], w_ra=dwra, w_rx=dwrx, b_ra=dbra[0], b_rx=dbrx[0],
                      lam=dlam[0] * wl["lru"]["sig_neg"])
            gl["mix_in"], _ = _dw(sv["h1b"], dmid, f"dw_lru_in_{l}", shard=2 * D_RNN // N_DEV)
        dy = mm_res_nt(dmid, _mat_view(wl["m_c"]), dz, ALPHA, f"mix_dx_{l}")
        sched.grads_ready(l, "hi", gl)
        lo = {}
        dy, dg[0], db[0], lo["w_out1"], lo["w_in1"] = ffn_bwd(
            dy, sv["z1"], wl["ln_g"][0], sv["gu1"], sv["a1"], sv["h0b"], wl["r_ff"], 0, wl["w_in1"], f"1_{l}", True)
        lo["ln_g"], lo["ln_b"] = jnp.concatenate(dg, axis=0), jnp.concatenate(db, axis=0)
        sched.grads_ready(l, "lo", lo)
    return loss, dy


def _lru_params(full, j, w_ra, w_rx):
    lam = full["lru_lambda"][j]
    return {
        "conv_w": full["lru_conv_w"][j], "conv_b": full["lru_conv_b"][j].reshape(1, -1),
        "w_ra": w_ra, "w_rx": w_rx,
        "b_ra": full["lru_b_ra"][j].reshape(1, -1), "b_rx": full["lru_b_rx"][j].reshape(1, -1),
        "lsl": jax.nn.log_sigmoid(lam).reshape(1, -1), "sig_neg": jax.nn.sigmoid(-lam),
    }


def _row_shards(f):
    return f.reshape(N_DEV, -1, f.shape[1])


def _gate_shards(w):
    return w.reshape(RNN_BLOCKS, N_DEV, RNN_W // N_DEV, RNN_W).transpose(1, 0, 2, 3).reshape(N_DEV, -1, RNN_W)


class _Fsdp:
    def __init__(self, args):
        self.args = args
        self.b16 = lambda a: a.astype(BF16)
        self.sm = _small_pack(args, "")
        self.sent = {l: self._ag_arrays(l) for l in range(DEPTH)}
        self.queue = [(l, p) for l in range(DEPTH) for p in "abc"][1:]
        self.hosting = None
        got = run_comm(comm_ag(self.sent[0]["a"] + [self.sm]), "all_gather_first")
        self.raw = {0: {"a": got[:-1]}}
        gflat = got[-1].reshape(N_DEV, SM_ROWS * 1024)
        self.small, off = {}, 0
        for n, shape, axis in SMALL:
            size = math.prod(shape)
            self.small[n] = _full_from_gathered(gflat[:, off:off + size], shape, axis)
            off += size
        self.bufs = {"r": jnp.zeros((4, R_ROWS, D_MODEL), BF16), "c704": jnp.zeros((4, 2 * DEPTH * D_MODEL, 704), BF16),
                     "c192": jnp.zeros((4, 2 * D_MODEL, 192), BF16), "c256": jnp.zeros((4, C256_ROWS, RNN_W), BF16),
                     "sm": jnp.zeros((4, SM_ROWS, 1024), F32)}
        self.pending = None
        self.small_g = [{} for _ in range(DEPTH)]
        self.loss = None

    def _ag_arrays(self, l):
        a, b16, j = self.args, self.b16, l // 2
        r_ff = jnp.concatenate([b16(a["ffn1_w_out"][l]), b16(a["ffn2_w_out"][l])], axis=0)
        if l % 2 == 0:
            return {"a": [b16(a["ffn1_w_in"][l]), b16(a["attn_w_qkv"][j])], "b": [r_ff, b16(a["attn_w_o"][j])],
                    "c": [b16(a["ffn2_w_in"][l])]}
        gates = jnp.concatenate([b16(a["lru_w_ra"][j]).reshape(-1, RNN_W), b16(a["lru_w_rx"][j]).reshape(-1, RNN_W)], axis=0)
        return {"a": [b16(a["ffn1_w_in"][l]), b16(a["lru_w_in"][j])], "b": [r_ff, b16(a["lru_w_out"][j]), gates],
                "c": [b16(a["ffn2_w_in"][l])]}

    def weights(self, l, piece):
        raw, j = self.raw[l][piece], l // 2
        full = lambda g: g.transpose(1, 0, 2).reshape(g.shape[1], -1)
        if piece == "a":
            return {"w_in1": full(raw[0]), "m_c": full(raw[1]), "ln_g": self.small["ln_g"][l], "ln_b": self.small["ln_b"][l]}
        if piece == "c":
            return {"w_in2": full(raw[0])}
        wl = {"r_ff": raw[0], "m_o": raw[1]}
        if l % 2:
            g = raw[2].reshape(N_DEV, 2, RNN_BLOCKS, RNN_W // N_DEV, RNN_W).transpose(1, 2, 0, 3, 4)
            g = g.reshape(2, RNN_BLOCKS, RNN_W, RNN_W)
            wl["lru"] = _lru_params(self.small, j, g[0], g[1])
        return wl

    def fwd_slot(self):
        if not self.queue:
            return None
        self.hosting = self.queue.pop(0)
        l, piece = self.hosting
        return comm_ag(self.sent[l][piece])

    def fwd_done(self, outs):
        if self.hosting is not None:
            l, piece = self.hosting
            self.raw.setdefault(l, {})[piece] = outs
            self.hosting = None

    def loss_ready(self, loss):
        self.loss = loss

    def _ranges(self, l, part):
        j = l // 2
        if part == "lo":
            return [("r", [(0, FF_SHARD, l * FF_SHARD)]), ("c704", [(0, D_MODEL, l * D_MODEL)])]
        r = [(0, FF_SHARD, R_FFN2 + l * FF_SHARD), (FF_SHARD, MIX_SHARD, (R_LRU if l % 2 else R_ATTN) + j * MIX_SHARD)]
        c704 = [(0, D_MODEL, (DEPTH + l) * D_MODEL)]
        if l % 2:
            gr = RNN_BLOCKS * (RNN_W // N_DEV)
            mix = ("c256", [(0, D_MODEL, j * D_MODEL), (D_MODEL, gr, C256_RA + j * gr), (D_MODEL + gr, gr, C256_RX + j * gr)])
        else:
            mix = ("c192", [(0, D_MODEL, j * D_MODEL)])
        return [("r", r), ("c704", c704), mix]

    def grads_ready(self, l, part, g):
        self.small_g[l].update(g)
        if part == "lo":
            ts = [_row_shards(g["w_out1"]), g["w_in1"]]
        else:
            tmix = g["mix_in"]
            if l % 2:
                tmix = jnp.concatenate([tmix, _gate_shards(g["w_ra"]), _gate_shards(g["w_rx"])], axis=1)
            ts = [jnp.concatenate([_row_shards(g["w_out2"]), _row_shards(g["mix_out"])], axis=1), g["w_in2"], tmix]
        sends = [t.reshape((4, 2) + t.shape[1:]) for t in ts]
        if (l, part) == (0, "lo"):
            self._last(sends)
        else:
            self.pending = {"tag": f"{l}{part}", "sends": sends, "where": self._ranges(l, part)}

    def _pair_sums(self, tag, sends, gots):
        tiles = (None, 512, 256)
        return [pair_sum(s, g, BF16, t or s.shape[2] // 2, f"pair_sum_{nm}_{tag}")
                for s, g, t, nm in zip(sends, gots, tiles, ("r", "c704", "mix"))]

    def _chip(self, parts, where, pick):
        names = [where[i][0] for i in pick]
        return names, comm_chip([(parts[i], k, where[i][1]) for k, i in enumerate(pick)], [self.bufs[n] for n in names])

    def bwd_slot(self, slot):
        p = self.pending
        if p is None:
            return None
        if slot == "pair":
            return comm_pair(p["sends"])
        pick = [1] if slot == "chip_x" else [i for i in range(len(p["sends"])) if i != 1]
        p["names"], comm = self._chip(p["parts"], p["where"], pick)
        return comm

    def bwd_done(self, slot, outs):
        p = self.pending
        if slot == "pair":
            p["parts"] = self._pair_sums(p["tag"], p["sends"], outs)
            return
        for n, o in zip(p["names"], outs):
            self.bufs[n] = o
        if slot == "chip_y":
            self.pending = None

    def _last(self, sends):
        sg = self.small_g
        stack = lambda key, ls: jnp.stack([sg[l][key] for l in ls])
        every, lru = range(DEPTH), (1, 3)
        small = {"ln_g": stack("ln_g", every), "ln_b": stack("ln_b", every), "lru_conv_w": stack("conv_w", lru),
                 "lru_conv_b": stack("conv_b", lru), "lru_b_ra": stack("b_ra", lru), "lru_b_rx": stack("b_rx", lru),
                 "lru_lambda": stack("lam", lru)}
        tail = jnp.concatenate([stack("sinks", (0, 2)).reshape(-1), self.loss.reshape(1)])
        tail = jnp.pad(tail, (0, SM_ROWS * 1024 - SMALL_N - tail.shape[0]))
        s_sm = jnp.concatenate([_shards_of_full(small[n], shape, axis) for n, shape, axis in SMALL]
                               + [jnp.broadcast_to(tail, (N_DEV, tail.shape[0]))], axis=1).reshape(4, 2, SM_ROWS, 1024)
        gots = run_comm(comm_pair(sends + [s_sm]), "pair_exchange_last")
        parts = self._pair_sums("0lo", sends, gots[:2]) + [pair_sum(s_sm, gots[2], F32, SM_ROWS, "pair_sum_sm")]
        names, comm = self._chip(parts, self._ranges(0, "lo") + [("sm", [(0, SM_ROWS, 0)])], [0, 1, 2])
        for n, o in zip(names, run_comm(comm, "chip_exchange_last")):
            self.bufs[n] = o


def kernel(x, ffn1_w_in, ffn1_w_out, ffn2_w_in, ffn2_w_out, ln_g, ln_b, attn_w_qkv, attn_sinks, attn_w_o, lru_w_in, lru_conv_w, lru_conv_b, lru_w_ra, lru_b_ra, lru_w_rx, lru_b_rx, lru_lambda, lru_w_out, loss_target, m_ffn1_w_in, m_ffn1_w_out, m_ffn2_w_in, m_ffn2_w_out, m_ln_g, m_ln_b, m_attn_w_qkv, m_attn_sinks, m_attn_w_o, m_lru_w_in, m_lru_conv_w, m_lru_conv_b, m_lru_w_ra, m_lru_b_ra, m_lru_w_rx, m_lru_b_rx, m_lru_lambda, m_lru_w_out, v_ffn1_w_in, v_ffn1_w_out, v_ffn2_w_in, v_ffn2_w_out, v_ln_g, v_ln_b, v_attn_w_qkv, v_attn_sinks, v_attn_w_o, v_lru_w_in, v_lru_conv_w, v_lru_conv_b, v_lru_w_ra, v_lru_b_ra, v_lru_w_rx, v_lru_b_rx, v_lru_lambda, v_lru_w_out):
    args = dict(locals())

    sched = _Fsdp(args)
    _, dx = _local_step(x[0], loss_target[0], sched, attn_sinks)
    p_r, p704, p192, p256, p_sm = (sched.bufs[n] for n in ("r", "c704", "c192", "c256", "sm"))
    sm = sched.sm

    fam = lambda n: (args[n], args["m_" + n], args["v_" + n])
    res = {
        "ffn1_w_out": adamw(p_r, 0, *fam("ffn1_w_out"), FF_SHARD, "adamw_ffn1_w_out"),
        "ffn2_w_out": adamw(p_r, R_FFN2, *fam("ffn2_w_out"), FF_SHARD, "adamw_ffn2_w_out"),
        "attn_w_o": adamw(p_r, R_ATTN, *fam("attn_w_o"), 2 * MIX_SHARD, "adamw_attn_w_o"),
        "lru_w_out": adamw(p_r, R_LRU, *fam("lru_w_out"), 2 * MIX_SHARD, "adamw_lru_w_out"),
        "ffn1_w_in": adamw(p704, 0, *fam("ffn1_w_in"), 512, "adamw_ffn1_w_in"),
        "ffn2_w_in": adamw(p704, DEPTH * D_MODEL, *fam("ffn2_w_in"), 512, "adamw_ffn2_w_in"),
        "attn_w_qkv": adamw(p192, 0, *fam("attn_w_qkv"), 512, "adamw_attn_w_qkv"),
        "lru_w_in": adamw(p256, 0, *fam("lru_w_in"), 512, "adamw_lru_w_in"),
        "lru_w_ra": adamw(p256, C256_RA, *fam("lru_w_ra"), 256, "adamw_lru_w_ra"),
        "lru_w_rx": adamw(p256, C256_RX, *fam("lru_w_rx"), 256, "adamw_lru_w_rx"),
    }
    sm_out = adamw(p_sm, 0, sm, _small_pack(args, "m_"), _small_pack(args, "v_"), SM_ROWS, "adamw_small")
    for k, pack in enumerate(sm_out):
        for n, val in _small_unpack(pack).items():
            res.setdefault(n, [None] * 4)[k] = val
    loss_total = sm_out[0].reshape(-1)[LOSS_OFF]
    out = [loss_total, dx[None]]
    for k in range(4):
        out += [res[n][k] for n in WEIGHTS]
    return tuple(out)
```

```python
import math

import jax
import jax.numpy as jnp
import numpy as np
from jax import lax
from jax.experimental import pallas as pl
from jax.experimental.pallas import tpu as pltpu

F32 = jnp.float32
BF16 = jnp.bfloat16

D_MODEL = 1024
DEPTH = 4
N_HEADS = 16
N_KV = 4
HEAD_DIM = 64
GROUP = 4
BLOCK = 128
ROPE_THETA = 10000.0
D_RNN = 1024
RNN_BLOCKS = 4
RNN_W = 256
CONV_W = 4
LRU_C = 8.0
D_FF = 2816
ALPHA = (2.0 * DEPTH) ** 0.25
LN_EPS = 1e-5
QKV = (N_HEADS + 2 * N_KV) * HEAD_DIM
N_DEV = 8

ADAM_LR = 0.001
ADAM_B1 = 0.9
ADAM_B2 = 0.999
ADAM_EPS = 1e-08
ADAM_WD = 0.01
ADAM_STEP = 10

VMEM_LIMIT = 52 * 1024 * 1024
NEG = float(np.finfo(np.float32).min)

MESH = pl.DeviceIdType.MESH
ANY = pl.BlockSpec(memory_space=pl.ANY)

FF_SHARD = D_FF // N_DEV
MIX_SHARD = D_MODEL // N_DEV
R_FFN2 = DEPTH * FF_SHARD
R_ATTN = 2 * DEPTH * FF_SHARD
R_LRU = R_ATTN + 2 * MIX_SHARD
R_ROWS = R_LRU + 2 * MIX_SHARD
C256_RA = 2 * D_MODEL
C256_RX = C256_RA + 2 * RNN_BLOCKS * (RNN_W // N_DEV)
C256_ROWS = C256_RX + 2 * RNN_BLOCKS * (RNN_W // N_DEV)

SMALL = (
    ("ln_g", (4, 3, 128), 2),
    ("ln_b", (4, 3, 128), 2),
    ("lru_conv_w", (2, 4, 128), 2),
    ("lru_conv_b", (2, 128), 1),
    ("lru_b_ra", (2, 128), 1),
    ("lru_b_rx", (2, 128), 1),
    ("lru_lambda", (2, 128), 1),
)
WEIGHTS = ("ffn1_w_in", "ffn1_w_out", "ffn2_w_in", "ffn2_w_out", "ln_g", "ln_b", "attn_w_qkv", "attn_sinks",
           "attn_w_o", "lru_w_in", "lru_conv_w", "lru_conv_b", "lru_w_ra", "lru_b_ra", "lru_w_rx", "lru_b_rx",
           "lru_lambda", "lru_w_out")
SMALL_N = sum(math.prod(s) for _, s, _ in SMALL)
SINK_OFF = SMALL_N
LOSS_OFF = SMALL_N + 32
SM_ROWS = 8


def _cp(*sem):
    return pltpu.CompilerParams(dimension_semantics=sem, vmem_limit_bytes=VMEM_LIMIT)


def _dot(a, b):
    return jnp.dot(a, b, preferred_element_type=F32)


def _dot_tn(a, b):
    return lax.dot_general(a, b, (((0,), (0,)), ((), ())), preferred_element_type=F32)


def _dot_nt(a, b):
    return lax.dot_general(a, b, (((1,), (1,)), ((), ())), preferred_element_type=F32)


def _col_chunks(cols, size=512):
    return [slice(c, min(c + size, cols)) for c in range(0, cols, size)]


def _ln(z, g, b):
    mu = jnp.mean(z, axis=-1, keepdims=True)
    xc = z - mu
    var = jnp.mean(xc * xc, axis=-1, keepdims=True)
    return xc * lax.rsqrt(var + LN_EPS) * g + b


def _rows_view(pack, rows_per_dev, row0):
    return (pack, (N_DEV, rows_per_dev, D_MODEL), (0, row0 // rows_per_dev, 0), (N_DEV * rows_per_dev, D_MODEL))


def _mat_view(arr, cols=None, cblk=0):
    k, n = arr.shape
    cols = n if cols is None else cols
    return (arr, (k, cols), (0, cblk), (k, cols))


def _vspec(view):
    _, bshape, bidx, _ = view
    return pl.BlockSpec(bshape, lambda *_: bidx)


def _vload(view, ref):
    return ref[...].reshape(view[3])


class Comm:
    def __init__(self, ins, out_shapes, sems, start, finish, aliases=None):
        self.ins, self.out_shapes, self.sems = list(ins), list(out_shapes), list(sems)
        self.start, self.finish, self.aliases = start, finish, dict(aliases or {})


def _call(body, *, grid, in_specs, out_specs, out_shape, operands, name, sem, scratch=(), comm=None):
    n_in, n_out, n_scr = len(in_specs), len(out_specs), len(scratch)
    if comm is None:
        return pl.pallas_call(body, grid=grid, in_specs=list(in_specs), out_specs=list(out_specs),
                              out_shape=list(out_shape), scratch_shapes=list(scratch), compiler_params=_cp(*sem),
                              name=name)(*operands), []
    nci, nco = len(comm.ins), len(comm.out_shapes)

    def hosted(*refs):
        ins, cins = refs[:n_in], refs[n_in:n_in + nci]
        o0 = n_in + nci
        outs, couts = refs[o0:o0 + n_out], refs[o0 + n_out:o0 + n_out + nco]
        s0 = o0 + n_out + nco
        scr, csems = refs[s0:s0 + n_scr], refs[s0 + n_scr:]
        first = last = None
        for ax, size in enumerate(grid):
            pid = pl.program_id(ax)
            f, e = pid == 0, pid == size - 1
            first = f if first is None else jnp.logical_and(first, f)
            last = e if last is None else jnp.logical_and(last, e)

        @pl.when(first)
        def _():
            comm.start(cins, couts, csems)

        body(*ins, *outs, *scr)

        @pl.when(last)
        def _():
            comm.finish(cins, couts, csems)

    res = pl.pallas_call(
        hosted, grid=grid, in_specs=list(in_specs) + [ANY] * nci, out_specs=list(out_specs) + [ANY] * nco,
        out_shape=list(out_shape) + comm.out_shapes, scratch_shapes=list(scratch) + comm.sems,
        input_output_aliases={n_in + i: n_out + o for i, o in comm.aliases.items()},
        compiler_params=_cp(*(("arbitrary",) * len(grid))), name=name)(*operands, *comm.ins)
    return res[:n_out], res[n_out:]


def run_comm(comm, name):
    nci, nco = len(comm.ins), len(comm.out_shapes)

    def body(*refs):
        cins, couts, csems = refs[:nci], refs[nci:nci + nco], refs[nci + nco:]
        comm.start(cins, couts, csems)
        comm.finish(cins, couts, csems)

    return pl.pallas_call(
        body, in_specs=[ANY] * nci, out_specs=[ANY] * nco, out_shape=comm.out_shapes, scratch_shapes=comm.sems,
        input_output_aliases=dict(comm.aliases), name=name)(*comm.ins)


def mm_plain(a, wv, out_dtype, name, nt=False, tm=512):
    S, K = a.shape
    N = wv[3][0] if nt else wv[3][1]
    tm = min(tm, S)
    dot = _dot_nt if nt else _dot

    def body(a_ref, w_ref, o_ref):
        o_ref[...] = dot(a_ref[...], _vload(wv, w_ref)).astype(out_dtype)

    return pl.pallas_call(
        body, grid=(S // tm,),
        in_specs=[pl.BlockSpec((tm, K), lambda i: (i, 0)), _vspec(wv)],
        out_specs=pl.BlockSpec((tm, N), lambda i: (i, 0)),
        out_shape=jax.ShapeDtypeStruct((S, N), out_dtype),
        compiler_params=_cp("parallel"), name=name)(a, wv[0])


def mm_res_nt(a, wv, r, alpha, name, tm=512):
    S, K = a.shape
    N = wv[3][0]
    tm = min(tm, S)

    def body(a_ref, w_ref, r_ref, o_ref):
        o_ref[...] = _dot_nt(a_ref[...], _vload(wv, w_ref)) + alpha * r_ref[...]

    return pl.pallas_call(
        body, grid=(S // tm,),
        in_specs=[pl.BlockSpec((tm, K), lambda i: (i, 0)), _vspec(wv), pl.BlockSpec((tm, N), lambda i: (i, 0))],
        out_specs=pl.BlockSpec((tm, N), lambda i: (i, 0)),
        out_shape=jax.ShapeDtypeStruct((S, N), F32),
        compiler_params=_cp("parallel"), name=name)(a, wv[0], r)


def mm_res2_nt(a3, wv0, wv1, r, alpha, name, tm=512, comm=None):
    _, S, K = a3.shape
    N = wv0[3][0]
    tm = min(tm, S)

    def body(a_ref, w0_ref, w1_ref, r_ref, o_ref):
        o_ref[...] = (_dot_nt(a_ref[0], _vload(wv0, w0_ref)) + _dot_nt(a_ref[1], _vload(wv1, w1_ref))
                      + alpha * r_ref[...])

    (out,), couts = _call(
        body, grid=(S // tm,),
        in_specs=[pl.BlockSpec((2, tm, K), lambda i: (0, i, 0)), _vspec(wv0), _vspec(wv1),
                  pl.BlockSpec((tm, N), lambda i: (i, 0))],
        out_specs=[pl.BlockSpec((tm, N), lambda i: (i, 0))],
        out_shape=[jax.ShapeDtypeStruct((S, N), F32)],
        operands=(a3, wv0[0], wv1[0], r), name=name, sem=("parallel",), comm=comm)
    return out, couts


def mm_ln(a, wv, h, g, b, scale, name, tm=512, comm=None):
    S, K = a.shape
    tm = min(tm, S)

    def body(a_ref, w_ref, h_ref, g_ref, b_ref, z_ref, y_ref, yb_ref):
        z = ALPHA * h_ref[...] + scale * _dot(a_ref[...], _vload(wv, w_ref))
        y = _ln(z, g_ref[...], b_ref[...])
        z_ref[...] = z
        y_ref[...] = y
        yb_ref[...] = y.astype(BF16)

    row = pl.BlockSpec((tm, D_MODEL), lambda i: (i, 0))
    vec = pl.BlockSpec((1, D_MODEL), lambda i: (0, 0))
    return _call(
        body, grid=(S // tm,),
        in_specs=[pl.BlockSpec((tm, K), lambda i: (i, 0)), _vspec(wv), row, vec, vec],
        out_specs=[row, row, row],
        out_shape=[jax.ShapeDtypeStruct((S, D_MODEL), F32), jax.ShapeDtypeStruct((S, D_MODEL), F32),
                   jax.ShapeDtypeStruct((S, D_MODEL), BF16)],
        operands=(a, wv[0], h, g.reshape(1, -1), b.reshape(1, -1)), name=name, sem=("parallel",), comm=comm)


def ffn_up(xb, w_in, name, tm=512, tn=1408, comm=None):
    S = xb.shape[0]
    tm = min(tm, S)
    nj = D_FF // tn

    def body(x_ref, wg_ref, wu_ref, jac_ref, a_ref):
        x = x_ref[...]
        g = _dot(x, wg_ref[...])
        u = _dot(x, wu_ref[...])
        sg = jax.nn.sigmoid(g)
        t = g * sg
        jac_ref[0] = (u * (sg * ((g - t) + 1.0))).astype(BF16)
        jac_ref[1] = t.astype(BF16)
        a_ref[...] = (t * u).astype(BF16)

    return _call(
        body, grid=(nj, S // tm),
        in_specs=[pl.BlockSpec((tm, D_MODEL), lambda j, i: (i, 0)),
                  pl.BlockSpec((D_MODEL, tn), lambda j, i: (0, j)),
                  pl.BlockSpec((D_MODEL, tn), lambda j, i: (0, nj + j))],
        out_specs=[pl.BlockSpec((2, tm, tn), lambda j, i: (0, i, j)), pl.BlockSpec((tm, tn), lambda j, i: (i, j))],
        out_shape=[jax.ShapeDtypeStruct((2, S, D_FF), BF16), jax.ShapeDtypeStruct((S, D_FF), BF16)],
        operands=(xb, w_in, w_in), name=name, sem=("parallel", "parallel"), comm=comm)


def ffn_mid_bwd(dfb, r_ff, blk, gu, name, tm=512, comm=None):
    S = dfb.shape[0]
    tm = min(tm, S)
    tn = 4 * FF_SHARD

    def body(df_ref, w_ref, gu_ref, dh_ref):
        w = w_ref[...].reshape(tn, D_MODEL)
        df = df_ref[...]
        for cols in _col_chunks(tn):
            da = _dot_nt(df, w[cols, :])
            dh_ref[0, :, cols] = (da * gu_ref[0, :, cols].astype(F32)).astype(BF16)
            dh_ref[1, :, cols] = (da * gu_ref[1, :, cols].astype(F32)).astype(BF16)

    gspec = pl.BlockSpec((2, tm, tn), lambda j, i: (0, i, j))
    (out,), couts = _call(
        body, grid=(2, S // tm),
        in_specs=[pl.BlockSpec((tm, D_MODEL), lambda j, i: (i, 0)),
                  pl.BlockSpec((4, FF_SHARD, D_MODEL), lambda j, i: (j, blk, 0)), gspec],
        out_specs=[gspec],
        out_shape=[jax.ShapeDtypeStruct((2, S, D_FF), BF16)],
        operands=(dfb, r_ff, gu), name=name, sem=("parallel", "parallel"), comm=comm)
    return out, couts


def mm_tn(a, b, name, tm, tn, ts=2048, planes=1, shard=None, comm=None):
    S, M = a.shape
    N = b.shape[-1] * planes
    ts = min(ts, S)
    per = b.shape[-1] // tn
    ns = S // ts

    if shard is None:
        def body(a_ref, b_ref, o_ref):
            @pl.when(pl.program_id(2) == 0)
            def _():
                o_ref[...] = jnp.zeros_like(o_ref)

            o_ref[...] += _dot_tn(a_ref[...], b_ref[...])

        out_spec = pl.BlockSpec((tm, tn), lambda i, j, s: (i, j))
        out_shape = jax.ShapeDtypeStruct((M, N), F32)
        scratch = ()
    else:
        def body(a_ref, b_ref, o_ref, acc_ref):
            s = pl.program_id(2)

            @pl.when(s == 0)
            def _():
                acc_ref[...] = jnp.zeros_like(acc_ref)

            acc_ref[...] += _dot_tn(a_ref[...], b_ref[...])

            @pl.when(s == ns - 1)
            def _():
                for q in range(tn // shard):
                    o_ref[q] = acc_ref[:, shard * q:shard * (q + 1)]

        out_spec = pl.BlockSpec((tn // shard, tm, shard), lambda i, j, s: (j, i, 0))
        out_shape = jax.ShapeDtypeStruct((N // shard, M, shard), F32)
        scratch = (pltpu.VMEM((tm, tn), F32),)

    if planes == 1:
        bspec = pl.BlockSpec((ts, tn), lambda i, j, s: (s, j))
    else:
        bspec = pl.BlockSpec((None, ts, tn), lambda i, j, s: (j // per, s, j % per))
    (out,), couts = _call(
        body, grid=(M // tm, N // tn, ns),
        in_specs=[pl.BlockSpec((ts, tm), lambda i, j, s: (s, i)), bspec],
        out_specs=[out_spec], out_shape=[out_shape], scratch=scratch,
        operands=(a, b), name=name, sem=("parallel", "parallel", "arbitrary"), comm=comm)
    return out, couts


def ln_bwd(dy, z, g, out_scale, name, tm=512):
    S = dy.shape[0]
    tm = min(tm, S)

    def body(dy_ref, z_ref, g_ref, dz_ref, dzb_ref, dg_ref, db_ref):
        @pl.when(pl.program_id(0) == 0)
        def _():
            dg_ref[...] = jnp.zeros_like(dg_ref)
            db_ref[...] = jnp.zeros_like(db_ref)

        z = z_ref[...]
        dy_ = dy_ref[...]
        mu = jnp.mean(z, axis=-1, keepdims=True)
        xc = z - mu
        var = jnp.mean(xc * xc, axis=-1, keepdims=True)
        rstd = lax.rsqrt(var + LN_EPS)
        xh = xc * rstd
        dxh = dy_ * g_ref[...]
        m1 = jnp.mean(dxh, axis=-1, keepdims=True)
        m2 = jnp.mean(dxh * xh, axis=-1, keepdims=True)
        dz = rstd * (dxh - m1 - xh * m2)
        dz_ref[...] = dz
        dzb_ref[...] = (out_scale * dz).astype(BF16)
        dg_ref[...] += jnp.sum(dy_ * xh, axis=0, keepdims=True)
        db_ref[...] += jnp.sum(dy_, axis=0, keepdims=True)

    row = pl.BlockSpec((tm, D_MODEL), lambda i: (i, 0))
    vec = pl.BlockSpec((1, D_MODEL), lambda i: (0, 0))
    return pl.pallas_call(
        body, grid=(S // tm,),
        in_specs=[row, row, vec],
        out_specs=[row, row, vec, vec],
        out_shape=[jax.ShapeDtypeStruct((S, D_MODEL), F32), jax.ShapeDtypeStruct((S, D_MODEL), BF16),
                   jax.ShapeDtypeStruct((1, D_MODEL), F32), jax.ShapeDtypeStruct((1, D_MODEL), F32)],
        compiler_params=_cp("arbitrary"), name=name)(dy, z, g.reshape(1, -1))


def loss_head(y, t, name, tm=512):
    S = y.shape[0]
    tm = min(tm, S)
    nt = S // tm

    def body(y_ref, t_ref, dy_ref, l_ref):
        i = pl.program_id(0)

        @pl.when(i == 0)
        def _():
            l_ref[...] = jnp.zeros_like(l_ref)

        e = y_ref[...] - t_ref[...]
        dy_ref[...] = e * (1.0 / D_MODEL)
        l_ref[...] += jnp.sum(e * e, axis=0, keepdims=True)

        @pl.when(i == nt - 1)
        def _():
            tot = jnp.sum(l_ref[...], axis=1, keepdims=True) * (0.5 / D_MODEL)
            l_ref[...] = jnp.broadcast_to(tot, l_ref.shape)

    row = pl.BlockSpec((tm, D_MODEL), lambda i: (i, 0))
    vec = pl.BlockSpec((1, D_MODEL), lambda i: (0, 0))
    return pl.pallas_call(
        body, grid=(nt,), in_specs=[row, row], out_specs=[row, vec],
        out_shape=[jax.ShapeDtypeStruct((S, D_MODEL), F32), jax.ShapeDtypeStruct((1, D_MODEL), F32)],
        compiler_params=_cp("arbitrary"), name=name)(y, t)


def _rope_tables(S):
    pos = jnp.arange(S, dtype=F32)
    inv_freq = ROPE_THETA ** (-jnp.arange(0, HEAD_DIM, 2, dtype=F32) / HEAD_DIM)
    ang = pos[:, None] * inv_freq[None, :]
    cos, sin = jnp.cos(ang), jnp.sin(ang)
    cosf = jnp.concatenate([cos, cos, cos, cos], axis=1)
    sinf = jnp.concatenate([-sin, sin, -sin, sin], axis=1)
    return cosf, sinf


def _rot(t, c, s, first):
    sw = jnp.where(first, pltpu.roll(t, 96, 1), pltpu.roll(t, 32, 1))
    return t * c + sw * s


def rope_fwd(qkv, cosf, sinf, name, tm=512):
    S = qkv.shape[0]
    tm = min(tm, S)

    def body(x_ref, c_ref, s_ref, q_ref, k_ref, v_ref):
        c = c_ref[...]
        s = s_ref[...]
        first = (lax.broadcasted_iota(jnp.int32, (tm, 128), 1) % HEAD_DIM) < (HEAD_DIM // 2)
        for j in range(8):
            q_ref[:, 128 * j:128 * (j + 1)] = _rot(x_ref[:, 128 * j:128 * (j + 1)], c, s, first).astype(BF16)
        for j in range(2):
            k_ref[:, 128 * j:128 * (j + 1)] = _rot(x_ref[:, 1024 + 128 * j:1024 + 128 * (j + 1)], c, s, first).astype(BF16)
        v_ref[...] = x_ref[:, 1280:1536].astype(BF16)

    tab = pl.BlockSpec((tm, 128), lambda i: (i, 0))
    return pl.pallas_call(
        body, grid=(S // tm,),
        in_specs=[pl.BlockSpec((tm, QKV), lambda i: (i, 0)), tab, tab],
        out_specs=[pl.BlockSpec((tm, 1024), lambda i: (i, 0)), pl.BlockSpec((tm, 256), lambda i: (i, 0)),
                   pl.BlockSpec((tm, 256), lambda i: (i, 0))],
        out_shape=[jax.ShapeDtypeStruct((S, 1024), BF16), jax.ShapeDtypeStruct((S, 256), BF16),
                   jax.ShapeDtypeStruct((S, 256), BF16)],
        compiler_params=_cp("parallel"), name=name)(qkv, cosf, sinf)


def rope_bwd(dq, dkc, dkp, dvc, dvp, cosf, sinf, name, tq):
    S = dq.shape[0]
    nt = S // tq

    def body(dq_ref, dkc_ref, dkp_ref, dvc_ref, dvp_ref, c_ref, s_ref, o_ref):
        i = pl.program_id(0)
        c = c_ref[...]
        s = -s_ref[...]
        first = (lax.broadcasted_iota(jnp.int32, (tq, 128), 1) % HEAD_DIM) < (HEAD_DIM // 2)
        for j in range(8):
            o_ref[:, 128 * j:128 * (j + 1)] = _rot(dq_ref[:, 128 * j:128 * (j + 1)], c, s, first).astype(BF16)
        has_next = i < nt - 1
        rows = lax.broadcasted_iota(jnp.int32, (tq, 256), 0)
        pad = jnp.zeros((tq - BLOCK, 256), F32)
        halo_k = jnp.concatenate([pad, dkp_ref[...]], axis=0)
        halo_v = jnp.concatenate([pad, dvp_ref[...]], axis=0)
        use = jnp.logical_and(has_next, rows >= tq - BLOCK)
        dk = dkc_ref[...] + jnp.where(use, halo_k, 0.0)
        dv = dvc_ref[...] + jnp.where(use, halo_v, 0.0)
        for j in range(2):
            o_ref[:, 1024 + 128 * j:1024 + 128 * (j + 1)] = _rot(dk[:, 128 * j:128 * (j + 1)], c, s, first).astype(BF16)
        o_ref[:, 1280:1536] = dv.astype(BF16)

    tab = pl.BlockSpec((tq, 128), lambda i: (i, 0))
    cur = pl.BlockSpec((tq, 256), lambda i: (i, 0))
    nxt = pl.BlockSpec((BLOCK, 256), lambda i: (jnp.minimum(i + 1, nt - 1), 0))
    return pl.pallas_call(
        body, grid=(nt,),
        in_specs=[pl.BlockSpec((tq, 1024), lambda i: (i, 0)), cur, nxt, cur, nxt, tab, tab],
        out_specs=pl.BlockSpec((tq, QKV), lambda i: (i, 0)),
        out_shape=jax.ShapeDtypeStruct((S, QKV), BF16),
        compiler_params=_cp("parallel"), name=name)(dq, dkc, dkp, dvc, dvp, cosf, sinf)


def _attn_masks(n):
    shape = (GROUP * BLOCK, 2 * BLOCK)
    r = lax.broadcasted_iota(jnp.int32, shape, 0) % BLOCK
    c = lax.broadcasted_iota(jnp.int32, shape, 1)
    ok = jnp.logical_and(c > r, c <= r + BLOCK)
    return jnp.logical_and(ok, jnp.logical_or(c >= BLOCK, n > 0)), ok


def _attn_probs(qs, kw, sink_col, ok):
    s = jnp.where(ok, _dot_nt(qs, kw) * (HEAD_DIM ** -0.5), NEG)
    m = jnp.maximum(jnp.max(s, axis=1, keepdims=True), sink_col)
    p = jnp.exp(s - m)
    es = jnp.exp(sink_col - m)
    inv = 1.0 / (jnp.sum(p, axis=1, keepdims=True) + es)
    return p * inv, es * inv


def _sink_col(sink_ref, g):
    rid = lax.broadcasted_iota(jnp.int32, (GROUP * BLOCK, 1), 0) // BLOCK
    col = jnp.zeros((GROUP * BLOCK, 1), F32)
    for j in range(GROUP):
        col = jnp.where(rid == j, sink_ref[GROUP * g + j], col)
    return col


KV_PAIR = 2
Q_LANES = KV_PAIR * GROUP * HEAD_DIM


def _low_half(shape):
    return lax.broadcasted_iota(jnp.int32, shape, 1) < HEAD_DIM


def _head_rows(ref, lo, gi):
    low = _low_half((BLOCK, 2 * HEAD_DIM))
    out = []
    for s in range(GROUP // 2):
        c0 = gi * GROUP * HEAD_DIM + 2 * HEAD_DIM * s
        x = ref[lo:lo + BLOCK, c0:c0 + 2 * HEAD_DIM]
        out += [jnp.where(low, x, jnp.zeros_like(x)), jnp.where(low, jnp.zeros_like(x), x)]
    return jnp.concatenate(out, axis=0)


def _head_slabs(x):
    low = _low_half((BLOCK, 2 * HEAD_DIM))
    return [jnp.where(low, x[2 * BLOCK * s:2 * BLOCK * s + BLOCK], x[2 * BLOCK * s + BLOCK:2 * BLOCK * (s + 1)])
            for s in range(GROUP // 2)]


def _kv_window(kc_ref, kp_ref, b, gi):
    if b == 0:
        x = jnp.concatenate([kp_ref[...], kc_ref[0:BLOCK, :]], axis=0)
    else:
        x = kc_ref[BLOCK * (b - 1):BLOCK * (b + 1), :]
    other = pltpu.roll(x, HEAD_DIM, 1)
    low = _low_half(x.shape)
    return jnp.where(low, x, other) if gi == 0 else jnp.where(low, other, x)


def _fold_halves(x, gi):
    tot = x + pltpu.roll(x, HEAD_DIM, 1)
    low = _low_half(x.shape)
    return jnp.where(low if gi == 0 else jnp.logical_not(low), tot, 0.0)


def _attn_specs(tq):
    nsub = tq // BLOCK
    qspec = pl.BlockSpec((tq, Q_LANES), lambda p, n: (n, p))
    cur = pl.BlockSpec((tq, KV_PAIR * HEAD_DIM), lambda p, n: (n, p))
    prev = pl.BlockSpec((BLOCK, KV_PAIR * HEAD_DIM), lambda p, n: (jnp.maximum(n * nsub - 1, 0), p))
    return qspec, cur, prev


def attn_fwd(q, k, v, sinks, name, tq):
    S = q.shape[0]
    nsub = tq // BLOCK

    def body(sink_ref, q_ref, kc_ref, kp_ref, vc_ref, vp_ref, o_ref):
        p = pl.program_id(0)
        n = pl.program_id(1)
        ok_first, ok_rest = _attn_masks(n)
        for gi in range(KV_PAIR):
            sink_col = _sink_col(sink_ref, KV_PAIR * p + gi)
            base = gi * GROUP * HEAD_DIM
            for b in range(nsub):
                lo = BLOCK * b
                qs = _head_rows(q_ref, lo, gi)
                pn, _ = _attn_probs(qs, _kv_window(kc_ref, kp_ref, b, gi), sink_col, ok_rest if b else ok_first)
                o = _dot(pn.astype(BF16), _kv_window(vc_ref, vp_ref, b, gi))
                for s, slab in enumerate(_head_slabs(o)):
                    c0 = base + 2 * HEAD_DIM * s
                    o_ref[lo:lo + BLOCK, c0:c0 + 2 * HEAD_DIM] = slab.astype(BF16)

    qspec, cur, prev = _attn_specs(tq)
    return pl.pallas_call(
        body, grid=(N_KV // KV_PAIR, S // tq),
        in_specs=[pl.BlockSpec(memory_space=pltpu.SMEM), qspec, cur, prev, cur, prev],
        out_specs=qspec,
        out_shape=jax.ShapeDtypeStruct((S, N_HEADS * HEAD_DIM), BF16),
        compiler_params=_cp("parallel", "parallel"), name=name)(sinks, q, k, k, v, v)


def attn_bwd(q, k, v, do, sinks, name, tq):
    S = q.shape[0]
    nsub = tq // BLOCK
    nt = S // tq

    def body(sink_ref, q_ref, kc_ref, kp_ref, vc_ref, vp_ref, do_ref, dq_ref, dkc_ref, dkp_ref, dvc_ref, dvp_ref, ds_ref):
        p = pl.program_id(0)
        n = pl.program_id(1)

        @pl.when(n == 0)
        def _():
            ds_ref[...] = jnp.zeros_like(ds_ref)

        for ref in (dkc_ref, dvc_ref, dkp_ref, dvp_ref):
            ref[...] = jnp.zeros_like(ref)
        rid = lax.broadcasted_iota(jnp.int32, (GROUP * BLOCK, 1), 0) // BLOCK
        sub = lax.broadcasted_iota(jnp.int32, (8, 128), 0)
        ok_first, ok_rest = _attn_masks(n)
        for gi in range(KV_PAIR):
            sink_col = _sink_col(sink_ref, KV_PAIR * p + gi)
            base = gi * GROUP * HEAD_DIM
            dsink = jnp.zeros((8, 128), F32)
            for b in range(nsub):
                lo = BLOCK * b
                qs = _head_rows(q_ref, lo, gi)
                dos = _head_rows(do_ref, lo, gi)
                kw = _kv_window(kc_ref, kp_ref, b, gi)
                vw = _kv_window(vc_ref, vp_ref, b, gi)
                pn, ps = _attn_probs(qs, kw, sink_col, ok_rest if b else ok_first)
                dp = _dot_nt(dos, vw)
                delta = jnp.sum(pn * dp, axis=1, keepdims=True)
                dsb = (pn * (dp - delta) * (HEAD_DIM ** -0.5)).astype(BF16)
                for s, slab in enumerate(_head_slabs(_dot(dsb, kw))):
                    c0 = base + 2 * HEAD_DIM * s
                    dq_ref[lo:lo + BLOCK, c0:c0 + 2 * HEAD_DIM] = slab
                dkw = _fold_halves(_dot_tn(dsb, qs), gi)
                dvw = _fold_halves(_dot_tn(pn.astype(BF16), dos), gi)
                if b == 0:
                    dkp_ref[...] += dkw[0:BLOCK]
                    dvp_ref[...] += dvw[0:BLOCK]
                else:
                    dkc_ref[lo - BLOCK:lo, :] += dkw[0:BLOCK]
                    dvc_ref[lo - BLOCK:lo, :] += dvw[0:BLOCK]
                dkc_ref[lo:lo + BLOCK, :] += dkw[BLOCK:2 * BLOCK]
                dvc_ref[lo:lo + BLOCK, :] += dvw[BLOCK:2 * BLOCK]
                sd = ps * delta
                for j in range(GROUP):
                    tot = -jnp.sum(jnp.where(rid == j, sd, 0.0))
                    dsink = dsink + jnp.where(sub == j, tot, 0.0)
            ds_ref[gi] += dsink

    qspec, cur, prev = _attn_specs(tq)
    halo = pl.BlockSpec((BLOCK, KV_PAIR * HEAD_DIM), lambda p, n: (n, p))
    kv_shape = jax.ShapeDtypeStruct((S, N_KV * HEAD_DIM), F32)
    halo_shape = jax.ShapeDtypeStruct((nt * BLOCK, N_KV * HEAD_DIM), F32)
    return pl.pallas_call(
        body, grid=(N_KV // KV_PAIR, nt),
        in_specs=[pl.BlockSpec(memory_space=pltpu.SMEM), qspec, cur, prev, cur, prev, qspec],
        out_specs=[qspec, cur, halo, cur, halo, pl.BlockSpec((KV_PAIR, 8, 128), lambda p, n: (p, 0, 0))],
        out_shape=[jax.ShapeDtypeStruct((S, N_HEADS * HEAD_DIM), F32), kv_shape, halo_shape, kv_shape, halo_shape,
                   jax.ShapeDtypeStruct((N_KV, 8, 128), F32)],
        compiler_params=_cp("parallel", "arbitrary"), name=name)(sinks, q, k, k, v, v, do)


def _rows_before(cur, prev8, k):
    if k == 0:
        return cur
    n = cur.shape[0]
    ext = jnp.concatenate([prev8, cur], axis=0)
    return ext[8 - k:8 - k + n]


def _rows_after(cur, next8, k):
    if k == 0:
        return cur
    n = cur.shape[0]
    ext = jnp.concatenate([cur, next8], axis=0)
    return ext[k:k + n]


def _gelu(x):
    c = math.sqrt(2.0 / math.pi)
    t = jnp.tanh(c * (x + 0.044715 * (x * x * x)))
    return 0.5 * (1.0 + t), t


def _neg_expm1(u):
    ser = 1.0 + u * (1.0 / 6.0)
    for k in range(5, 1, -1):
        ser = 1.0 + (u * (1.0 / k)) * ser
    return jnp.where(u > -0.125, -(u * ser), 1.0 - jnp.exp(u))


def _block_diag(xb16, w_ref):
    return jnp.concatenate([_dot(xb16[:, RNN_W * n:RNN_W * (n + 1)], w_ref[n]) for n in range(RNN_BLOCKS)], axis=1)


def _lru_gates(xb, prev8, cw_ref, cb_ref, wra_ref, wrx_ref, bra_ref, brx_ref, lsl_ref):
    xc = cb_ref[...] + cw_ref[3:4, :] * xb
    for w in range(CONV_W - 1):
        xc = xc + cw_ref[w:w + 1, :] * _rows_before(xb, prev8, CONV_W - 1 - w)
    xcb = xc.astype(BF16)
    r = jax.nn.sigmoid(_block_diag(xcb, wra_ref) + bra_ref[...])
    ig = jax.nn.sigmoid(_block_diag(xcb, wrx_ref) + brx_ref[...])
    la = LRU_C * r * lsl_ref[...]
    a = jnp.exp(la)
    sq = jnp.sqrt(_neg_expm1(2.0 * la))
    return xc, xcb, r, ig, a, sq


def lru_fwd(xg, p, name, tm=256):
    S = xg.shape[0]
    tm = min(tm, S)

    def body(xg_ref, xp_ref, cw_ref, cb_ref, wra_ref, wrx_ref, bra_ref, brx_ref, lsl_ref, y_ref, h_ref, hc_ref, a_s, b_s):
        i = pl.program_id(0)
        xb = xg_ref[:, 0:D_RNN]
        gb = xg_ref[:, D_RNN:2 * D_RNN]
        prev8 = jnp.where(i > 0, xp_ref[:, 0:D_RNN], 0.0)
        xc, _, r, ig, a, sq = _lru_gates(xb, prev8, cw_ref, cb_ref, wra_ref, wrx_ref, bra_ref, brx_ref, lsl_ref)
        a_s[...] = a
        b_s[...] = sq * (ig * xc)

        @pl.when(i == 0)
        def _():
            hc_ref[...] = jnp.zeros_like(hc_ref)

        def chunk(c, h):
            o = pl.multiple_of(c * 8, 8)
            av = a_s[pl.ds(o, 8), :]
            bv = b_s[pl.ds(o, 8), :]
            rows = []
            for t in range(8):
                h = av[t:t + 1, :] * h + bv[t:t + 1, :]
                rows.append(h)
            h_ref[pl.ds(o, 8), :] = jnp.concatenate(rows, axis=0)
            return h

        h_last = lax.fori_loop(0, tm // 8, chunk, hc_ref[0:1, :])
        hc_ref[0:1, :] = h_last
        cdf, _ = _gelu(gb)
        y_ref[...] = (h_ref[...] * (gb * cdf)).astype(BF16)

    vec = pl.BlockSpec((1, D_RNN), lambda i: (0, 0))
    wsp = pl.BlockSpec((RNN_BLOCKS, RNN_W, RNN_W), lambda i: (0, 0, 0))
    return pl.pallas_call(
        body, grid=(S // tm,),
        in_specs=[pl.BlockSpec((tm, 2 * D_RNN), lambda i: (i, 0)),
                  pl.BlockSpec((8, 2 * D_RNN), lambda i: (jnp.maximum(i * (tm // 8) - 1, 0), 0)),
                  pl.BlockSpec((CONV_W, D_RNN), lambda i: (0, 0)), vec, wsp, wsp, vec, vec, vec],
        out_specs=[pl.BlockSpec((tm, D_RNN), lambda i: (i, 0)), pl.BlockSpec((tm, D_RNN), lambda i: (i, 0))],
        out_shape=[jax.ShapeDtypeStruct((S, D_RNN), BF16), jax.ShapeDtypeStruct((S, D_RNN), F32)],
        scratch_shapes=[pltpu.VMEM((8, D_RNN), F32), pltpu.VMEM((tm, D_RNN), F32), pltpu.VMEM((tm, D_RNN), F32)],
        compiler_params=_cp("arbitrary"), name=name)(
            xg, xg, p["conv_w"], p["conv_b"], p["w_ra"], p["w_rx"], p["b_ra"], p["b_rx"], p["lsl"])


def lru_bwd(dy, xg, h, p, name, tm=256):
    S = xg.shape[0]
    tm = min(tm, S)
    nt = S // tm

    def body(dy_ref, xg_ref, xp_ref, h_ref, hp_ref, cw_ref, cb_ref, wra_ref, wrx_ref, bra_ref, brx_ref,
             lsl_ref, dxg_ref, dcw_ref, dcb_ref, dwra_ref, dwrx_ref, dbra_ref, dbrx_ref, dlam_ref,
             lc_ref, nx_ref, a_s, g_s, l_s):
        i = pl.program_id(0)
        ti = nt - 1 - i

        @pl.when(i == 0)
        def _():
            lc_ref[...] = jnp.zeros_like(lc_ref)
            nx_ref[...] = jnp.zeros_like(nx_ref)
            for ref in (dcw_ref, dcb_ref, dwra_ref, dwrx_ref, dbra_ref, dbrx_ref, dlam_ref):
                ref[...] = jnp.zeros_like(ref)

        xb = xg_ref[:, 0:D_RNN]
        gb = xg_ref[:, D_RNN:2 * D_RNN]
        prev8 = jnp.where(ti > 0, xp_ref[:, 0:D_RNN], 0.0)
        xc, xcb, r, ig, a, sq = _lru_gates(xb, prev8, cw_ref, cb_ref, wra_ref, wrx_ref, bra_ref, brx_ref, lsl_ref)
        hh = h_ref[...]
        hprev = _rows_before(hh, jnp.where(ti > 0, hp_ref[...], 0.0), 1)
        dy_ = dy_ref[...]
        cdf, th = _gelu(gb)
        c0 = math.sqrt(2.0 / math.pi)
        dgate = cdf + gb * (0.5 * (1.0 - th * th) * c0 * (1.0 + 3.0 * 0.044715 * gb * gb))
        dgb = dy_ * hh * dgate
        a_s[...] = a
        g_s[...] = dy_ * (gb * cdf)

        def chunk(cc, carry):
            o = pl.multiple_of((tm // 8 - 1 - cc) * 8, 8)
            av = a_s[pl.ds(o, 8), :]
            gv = g_s[pl.ds(o, 8), :]
            rows = [None] * 8
            for t in range(7, -1, -1):
                lam_t = gv[t:t + 1, :] + carry
                rows[t] = lam_t
                carry = av[t:t + 1, :] * lam_t
            l_s[pl.ds(o, 8), :] = jnp.concatenate(rows, axis=0)
            return carry

        carry = lax.fori_loop(0, tm // 8, chunk, lc_ref[0:1, :])
        lc_ref[0:1, :] = carry
        lam = l_s[...]
        da = lam * hprev
        dixc = lam * sq
        di = dixc * xc
        dxc = dixc * ig
        dsq = lam * (ig * xc)
        dla = da * a - dsq * (a * a / sq)
        dr = dla * (LRU_C * lsl_ref[...])
        dlam_ref[...] += jnp.sum(dla * (LRU_C * r), axis=0, keepdims=True)
        dpr = dr * r * (1.0 - r)
        dpi = di * ig * (1.0 - ig)
        dbra_ref[...] += jnp.sum(dpr, axis=0, keepdims=True)
        dbrx_ref[...] += jnp.sum(dpi, axis=0, keepdims=True)
        dprb = dpr.astype(BF16)
        dpib = dpi.astype(BF16)
        back = []
        for n in range(RNN_BLOCKS):
            sl = slice(RNN_W * n, RNN_W * (n + 1))
            dwra_ref[n] += _dot_tn(xcb[:, sl], dprb[:, sl])
            dwrx_ref[n] += _dot_tn(xcb[:, sl], dpib[:, sl])
            back.append(_dot_nt(dprb[:, sl], wra_ref[n]) + _dot_nt(dpib[:, sl], wrx_ref[n]))
        dxc = dxc + jnp.concatenate(back, axis=1)
        dcb_ref[...] += jnp.sum(dxc, axis=0, keepdims=True)
        next8 = nx_ref[...]
        dxb = cw_ref[3:4, :] * dxc
        dcw_ref[3:4, :] += jnp.sum(dxc * xb, axis=0, keepdims=True)
        for w in range(CONV_W - 1):
            k = CONV_W - 1 - w
            dcw_ref[w:w + 1, :] += jnp.sum(dxc * _rows_before(xb, prev8, k), axis=0, keepdims=True)
            dxb = dxb + cw_ref[w:w + 1, :] * _rows_after(dxc, next8, k)
        nx_ref[...] = dxc[0:8, :]
        dxg_ref[:, 0:D_RNN] = dxb.astype(BF16)
        dxg_ref[:, D_RNN:2 * D_RNN] = dgb.astype(BF16)

    rev = lambda i: (nt - 1 - i, 0)
    before = lambda i: (jnp.maximum((nt - 1 - i) * (tm // 8) - 1, 0), 0)
    vec = pl.BlockSpec((1, D_RNN), lambda i: (0, 0))
    wsp = pl.BlockSpec((RNN_BLOCKS, RNN_W, RNN_W), lambda i: (0, 0, 0))
    cwsp = pl.BlockSpec((CONV_W, D_RNN), lambda i: (0, 0))
    return pl.pallas_call(
        body, grid=(nt,),
        in_specs=[pl.BlockSpec((tm, D_RNN), rev), pl.BlockSpec((tm, 2 * D_RNN), rev), pl.BlockSpec((8, 2 * D_RNN), before),
                  pl.BlockSpec((tm, D_RNN), rev), pl.BlockSpec((8, D_RNN), before),
                  cwsp, vec, wsp, wsp, vec, vec, vec],
        out_specs=[pl.BlockSpec((tm, 2 * D_RNN), rev), cwsp, vec, wsp, wsp, vec, vec, vec],
        out_shape=[jax.ShapeDtypeStruct((S, 2 * D_RNN), BF16), jax.ShapeDtypeStruct((CONV_W, D_RNN), F32),
                   jax.ShapeDtypeStruct((1, D_RNN), F32), jax.ShapeDtypeStruct((RNN_BLOCKS, RNN_W, RNN_W), F32),
                   jax.ShapeDtypeStruct((RNN_BLOCKS, RNN_W, RNN_W), F32), jax.ShapeDtypeStruct((1, D_RNN), F32),
                   jax.ShapeDtypeStruct((1, D_RNN), F32), jax.ShapeDtypeStruct((1, D_RNN), F32)],
        scratch_shapes=[pltpu.VMEM((8, D_RNN), F32), pltpu.VMEM((8, D_RNN), F32), pltpu.VMEM((tm, D_RNN), F32),
                        pltpu.VMEM((tm, D_RNN), F32), pltpu.VMEM((tm, D_RNN), F32)],
        compiler_params=_cp("arbitrary"), name=name)(
            dy, xg, xg, h, h, p["conv_w"], p["conv_b"], p["w_ra"], p["w_rx"], p["b_ra"], p["b_rx"], p["lsl"])


def _place():
    x, y, c = lax.axis_index("x"), lax.axis_index("y"), lax.axis_index("c")
    chips = [(1 - x, y), (x, 1 - y), (1 - x, 1 - y)]
    return x, y, c, chips


def comm_ag(arrs):
    n = len(arrs)

    def copies(ins, outs, sems):
        send_sems, recv_sems, local_sems = sems
        x, y, c, chips = _place()
        me, sibling = (x, y, c), (x, y, 1 - c)

        def rows(a, px, py, pc):
            return outs[a].at[4 * px + 2 * py + pc]

        def copy(a, k, block, to, own=False):
            return pltpu.make_async_remote_copy(
                src_ref=ins[a] if own else rows(a, *block), dst_ref=rows(a, *block),
                send_sem=send_sems.at[k, a], recv_sem=recv_sems.at[k, a], device_id=to, device_id_type=MESH)

        mine = [pltpu.make_async_copy(ins[a], rows(a, *me), local_sems.at[a]) for a in range(n)]
        first = [copy(a, 1 + j, me, (*chip, c), own=True) for j, chip in enumerate(chips) for a in range(n)]
        first += [copy(a, 0, me, sibling, own=True) for a in range(n)]
        return copy, chips, c, me, sibling, mine, first

    def start(ins, outs, sems):
        _, _, _, _, _, mine, first = copies(ins, outs, sems)
        for cp in mine + first:
            cp.start()

    def finish(ins, outs, sems):
        copy, chips, c, me, sibling, mine, first = copies(ins, outs, sems)
        passed = []
        for j, chip in enumerate(chips):
            for a in range(n):
                copy(a, 1 + j, (*chip, c), me).wait_recv()
                fwd = copy(a, 4 + j, (*chip, c), sibling)
                fwd.start()
                passed.append(fwd)
        for a in range(n):
            copy(a, 0, sibling, me).wait_recv()
        for j, chip in enumerate(chips):
            for a in range(n):
                copy(a, 4 + j, (*chip, 1 - c), me).wait_recv()
        for cp in first + passed:
            cp.wait_send()
        for cp in mine:
            cp.wait()

    return Comm(arrs, [jax.ShapeDtypeStruct((N_DEV,) + p.shape, p.dtype) for p in arrs],
                [pltpu.SemaphoreType.DMA((7, n)), pltpu.SemaphoreType.DMA((7, n)), pltpu.SemaphoreType.DMA((n,))],
                start, finish)


def comm_pair(sends):
    n = len(sends)

    def copies(ins, outs, sems):
        send_sems, recv_sems = sems
        x, y, c, _ = _place()
        return [pltpu.make_async_remote_copy(
            src_ref=ins[a].at[k, 1 - c], dst_ref=outs[a].at[k], send_sem=send_sems.at[k, a], recv_sem=recv_sems.at[k, a],
            device_id=(x, y, 1 - c), device_id_type=MESH) for k in range(4) for a in range(n)]

    def start(ins, outs, sems):
        for cp in copies(ins, outs, sems):
            cp.start()

    def finish(ins, outs, sems):
        for cp in copies(ins, outs, sems):
            cp.wait()

    return Comm(sends, [jax.ShapeDtypeStruct((4,) + s.shape[2:], s.dtype) for s in sends],
                [pltpu.SemaphoreType.DMA((4, n)), pltpu.SemaphoreType.DMA((4, n))], start, finish)


def pair_sum(send, got, out_dtype, tr, name):
    _, _, R, C = send.shape
    c = lax.axis_index("c").astype(jnp.int32).reshape(1)

    def body(c_ref, s_ref, g_ref, o_ref):
        o_ref[...] = (s_ref[...] + g_ref[...]).astype(out_dtype)

    return pl.pallas_call(
        body,
        grid_spec=pltpu.PrefetchScalarGridSpec(
            num_scalar_prefetch=1, grid=(4, R // tr),
            in_specs=[pl.BlockSpec((None, None, tr, C), lambda k, i, cr: (k, cr[0], i, 0)),
                      pl.BlockSpec((None, tr, C), lambda k, i, cr: (k, i, 0))],
            out_specs=pl.BlockSpec((None, tr, C), lambda k, i, cr: (k, i, 0))),
        out_shape=jax.ShapeDtypeStruct((4, R, C), out_dtype),
        compiler_params=_cp("parallel", "parallel"), name=name)(c, send, got)


def comm_chip(items, bufs):
    ns = len(items)
    segs = [(i, b, s0, nr, d0) for i, (_, b, ranges) in enumerate(items) for (s0, nr, d0) in ranges]

    def copies(ins, outs, sems):
        send_sems, recv_sems, local_sems = sems
        x, y, c, chips = _place()
        mychip = 2 * x + y
        mine = [pltpu.make_async_copy(ins[i].at[mychip, pl.ds(s0, nr)], outs[b].at[mychip, pl.ds(d0, nr)], local_sems.at[q])
                for q, (i, b, s0, nr, d0) in enumerate(segs)]
        remote = [pltpu.make_async_remote_copy(
            src_ref=ins[i].at[2 * px + py, pl.ds(s0, nr)], dst_ref=outs[b].at[mychip, pl.ds(d0, nr)],
            send_sem=send_sems.at[j, q], recv_sem=recv_sems.at[j, q], device_id=(px, py, c), device_id_type=MESH)
            for j, (px, py) in enumerate(chips) for q, (i, b, s0, nr, d0) in enumerate(segs)]
        return mine, remote

    def start(ins, outs, sems):
        mine, remote = copies(ins, outs, sems)
        for cp in mine + remote:
            cp.start()

    def finish(ins, outs, sems):
        mine, remote = copies(ins, outs, sems)
        for cp in remote + mine:
            cp.wait()

    q = len(segs)
    return Comm([it[0] for it in items] + list(bufs), [jax.ShapeDtypeStruct(b.shape, b.dtype) for b in bufs],
                [pltpu.SemaphoreType.DMA((3, q)), pltpu.SemaphoreType.DMA((3, q)), pltpu.SemaphoreType.DMA((q,))],
                start, finish, aliases={ns + b: b for b in range(len(bufs))})


def adamw(parts, row0, w, m, v, tr, name):
    C = w.shape[-1]
    rows = w.size // C
    off = row0 // tr

    def body(p_ref, w_ref, m_ref, v_ref, g_ref, d_ref, nm_ref, nv_ref):
        g = ((p_ref[0].astype(F32) + p_ref[1].astype(F32)) + p_ref[2].astype(F32)) + p_ref[3].astype(F32)
        m2 = ADAM_B1 * m_ref[...] + (1.0 - ADAM_B1) * g
        v2 = ADAM_B2 * v_ref[...] + (1.0 - ADAM_B2) * (g * g)
        mh = m2 / (1.0 - ADAM_B1 ** ADAM_STEP)
        vh = v2 / (1.0 - ADAM_B2 ** ADAM_STEP)
        g_ref[...] = g
        d_ref[...] = -ADAM_LR * (mh / (jnp.sqrt(vh) + ADAM_EPS) + ADAM_WD * w_ref[...])
        nm_ref[...] = m2
        nv_ref[...] = v2

    row = pl.BlockSpec((tr, C), lambda i: (i, 0))
    shp = jax.ShapeDtypeStruct((rows, C), F32)
    outs = pl.pallas_call(
        body, grid=(rows // tr,),
        in_specs=[pl.BlockSpec((4, tr, C), lambda i: (0, off + i, 0)), row, row, row],
        out_specs=[row, row, row, row], out_shape=[shp, shp, shp, shp],
        compiler_params=_cp("parallel"), name=name)(parts, w.reshape(rows, C), m.reshape(rows, C), v.reshape(rows, C))
    return [o.reshape(w.shape) for o in outs]


def _full_from_gathered(flat, shape, axis):
    t = jnp.moveaxis(flat.reshape((N_DEV,) + shape), 0, axis)
    return t.reshape(shape[:axis] + (N_DEV * shape[axis],) + shape[axis + 1:])


def _shards_of_full(full, shape, axis):
    t = full.reshape(shape[:axis] + (N_DEV, shape[axis]) + shape[axis + 1:])
    return jnp.moveaxis(t, axis, 0).reshape(N_DEV, -1)


def _small_pack(args, prefix):
    flat = jnp.concatenate([args[prefix + n].reshape(-1) for n, _, _ in SMALL] + [args[prefix + "attn_sinks"].reshape(-1)])
    return jnp.pad(flat, (0, SM_ROWS * 1024 - flat.shape[0])).reshape(SM_ROWS, 1024)


def _small_unpack(pack):
    flat = pack.reshape(-1)
    out, off = {}, 0
    for n, shape, _ in SMALL:
        size = math.prod(shape)
        out[n] = flat[off:off + size].reshape(shape)
        off += size
    out["attn_sinks"] = flat[SINK_OFF:SINK_OFF + 32].reshape(2, 16)
    return out


def _tn_tile(n):
    return next(t for t in (1408, 1024, 768, 512, 256, 128) if n % t == 0)


def _dw(a, b, name, planes=1, shard=None, comm=None):
    return mm_tn(a, b, name, _tn_tile(a.shape[1]), _tn_tile(b.shape[-1]), planes=planes, shard=shard, comm=comm)


class _NoExchange:
    def __init__(self, layers):
        self.layers, self.grads = layers, [{} for _ in range(DEPTH)]

    def weights(self, l, piece):
        return self.layers[l]

    def fwd_slot(self):
        return None

    def fwd_done(self, outs):
        pass

    def loss_ready(self, loss):
        pass

    def bwd_slot(self, slot):
        return None

    def bwd_done(self, slot, outs):
        pass

    def grads_ready(self, l, part, g):
        self.grads[l].update(g)


def _local_step(x, target, sched, sinks):
    S = x.shape[0]
    tq = min(512, S)
    cosf, sinf = _rope_tables(S)
    saved = []
    h, hb = x, x.astype(BF16)
    for l in range(DEPTH):
        j = l // 2
        wl = {}
        sv = {"h0b": hb, "wl": wl}
        wl.update(sched.weights(l, "a"))
        (sv["gu1"], a1), co = ffn_up(hb, wl["w_in1"], f"ffn1_up_{l}", comm=sched.fwd_slot())
        sched.fwd_done(co)
        sv["a1"] = a1
        wl.update(sched.weights(l, "b"))
        (sv["z1"], h, hb), co = mm_ln(a1, _rows_view(wl["r_ff"], FF_SHARD, 0), h, wl["ln_g"][0], wl["ln_b"][0], 0.5,
                                      f"ffn1_down_ln_{l}", comm=sched.fwd_slot())
        sched.fwd_done(co)
        sv["h1b"] = hb
        if l % 2 == 0:
            qkv = mm_plain(hb, _mat_view(wl["m_c"]), F32, f"attn_qkv_{l}")
            qr, kr, vv = rope_fwd(qkv, cosf, sinf, f"rope_{l}")
            mix_in = attn_fwd(qr, kr, vv, sinks[j], f"attn_core_{l}", tq)
            sv.update(qr=qr, kr=kr, vv=vv)
        else:
            xg = mm_plain(hb, _mat_view(wl["m_c"]), F32, f"lru_in_{l}")
            mix_in, hstate = lru_fwd(xg, wl["lru"], f"lru_core_{l}")
            sv.update(xg=xg, hstate=hstate)
        sv["mix_in"] = mix_in
        (sv["z2"], h, hb), _ = mm_ln(mix_in, _rows_view(wl["m_o"], MIX_SHARD, 0), h, wl["ln_g"][1], wl["ln_b"][1], 1.0,
                                     f"mix_out_ln_{l}")
        sv["h2b"] = hb
        wl.update(sched.weights(l, "c"))
        (sv["gu2"], a2), co = ffn_up(hb, wl["w_in2"], f"ffn2_up_{l}", comm=sched.fwd_slot())
        sched.fwd_done(co)
        sv["a2"] = a2
        (sv["z3"], h, hb), co = mm_ln(a2, _rows_view(wl["r_ff"], FF_SHARD, FF_SHARD), h, wl["ln_g"][2], wl["ln_b"][2], 0.5,
                                      f"ffn2_down_ln_{l}", comm=sched.fwd_slot())
        sched.fwd_done(co)
        saved.append(sv)

    dy, lvec = loss_head(h, target, "loss_head")
    loss = lvec[0, 0]
    sched.loss_ready(loss)

    def hosted(slot, on, fn):
        comm = sched.bwd_slot(slot) if on else None
        out, co = fn(comm)
        if comm is not None:
            sched.bwd_done(slot, co)
        return out

    def ffn_bwd(dy, z, g, gu, a, xin_b, r_ff, blk, w_in, tag, host):
        dz, dzb, dg, db = ln_bwd(dy, z, g, 0.5, f"ln_bwd_{tag}")
        dh = hosted("pair", host, lambda cm: ffn_mid_bwd(dzb, r_ff, blk, gu, f"ffn_mid_bwd_{tag}", comm=cm))
        d_wout, _ = _dw(a, dzb, f"dw_out_{tag}")
        d_win = hosted("chip_x", host, lambda cm: _dw(xin_b, dh, f"dw_in_{tag}", planes=2, shard=2 * D_FF // N_DEV, comm=cm))
        dx = hosted("chip_y", host, lambda cm: mm_res2_nt(dh, _mat_view(w_in, D_FF, 0), _mat_view(w_in, D_FF, 1), dz, ALPHA,
                                                          f"ffn_dx_{tag}", comm=cm))
        return dx, dg, db, d_wout, d_win

    for l in reversed(range(DEPTH)):
        j = l // 2
        sv = saved[l]
        wl = sv["wl"]
        gl = {}
        dg, db = [None] * 3, [None] * 3
        dy, dg[2], db[2], gl["w_out2"], gl["w_in2"] = ffn_bwd(
            dy, sv["z3"], wl["ln_g"][2], sv["gu2"], sv["a2"], sv["h2b"], wl["r_ff"], 1, wl["w_in2"], f"2_{l}", True)
        dz, dzb, dg[1], db[1] = ln_bwd(dy, sv["z2"], wl["ln_g"][1], 1.0, f"ln_bwd_mix_{l}")
        w_mix = _rows_view(wl["m_o"], MIX_SHARD, 0)
        if l % 2 == 0:
            gl["mix_out"], _ = _dw(sv["mix_in"], dzb, f"dw_o_{l}")
            do = mm_plain(dzb, w_mix, BF16, f"attn_do_{l}", nt=True)
            dq, dkc, dkp, dvc, dvp, dsk = attn_bwd(sv["qr"], sv["kr"], sv["vv"], do, sinks[j], f"attn_core_bwd_{l}", tq)
            gl["sinks"] = dsk[:, :GROUP, 0].reshape(N_HEADS)
            dmid = rope_bwd(dq, dkc, dkp, dvc, dvp, cosf, sinf, f"rope_bwd_{l}", tq)
            gl["mix_in"], _ = _dw(sv["h1b"], dmid, f"dw_qkv_{l}", shard=QKV // N_DEV)
        else:
            gl["mix_out"], _ = _dw(sv["mix_in"], dzb, f"dw_lru_out_{l}")
            dyl = mm_plain(dzb, w_mix, F32, f"lru_dy_{l}", nt=True)
            dmid, dcw, dcb, dwra, dwrx, dbra, dbrx, dlam = lru_bwd(dyl, sv["xg"], sv["hstate"], wl["lru"], f"lru_core_bwd_{l}")
            gl.update(conv_w=dcw, conv_b=dcb[0], w_ra=dwra, w_rx=dwrx, b_ra=dbra[0], b_rx=dbrx[0],
                      lam=dlam[0] * wl["lru"]["sig_neg"])
            gl["mix_in"], _ = _dw(sv["h1b"], dmid, f"dw_lru_in_{l}", shard=2 * D_RNN // N_DEV)
        dy = mm_res_nt(dmid, _mat_view(wl["m_c"]), dz, ALPHA, f"mix_dx_{l}")
        sched.grads_ready(l, "hi", gl)
        lo = {}
        dy, dg[0], db[0], lo["w_out1"], lo["w_in1"] = ffn_bwd(
            dy, sv["z1"], wl["ln_g"][0], sv["gu1"], sv["a1"], sv["h0b"], wl["r_ff"], 0, wl["w_in1"], f"1_{l}", True)
        lo["ln_g"], lo["ln_b"] = jnp.concatenate(dg, axis=0), jnp.concatenate(db, axis=0)
        sched.grads_ready(l, "lo", lo)
    return loss, dy


def _lru_params(full, j, w_ra, w_rx):
    lam = full["lru_lambda"][j]
    return {
        "conv_w": full["lru_conv_w"][j], "conv_b": full["lru_conv_b"][j].reshape(1, -1),
        "w_ra": w_ra, "w_rx": w_rx,
        "b_ra": full["lru_b_ra"][j].reshape(1, -1), "b_rx": full["lru_b_rx"][j].reshape(1, -1),
        "lsl": jax.nn.log_sigmoid(lam).reshape(1, -1), "sig_neg": jax.nn.sigmoid(-lam),
    }


def _row_shards(f):
    return f.reshape(N_DEV, -1, f.shape[1])


def _gate_shards(w):
    return w.reshape(RNN_BLOCKS, N_DEV, RNN_W // N_DEV, RNN_W).transpose(1, 0, 2, 3).reshape(N_DEV, -1, RNN_W)


class _Fsdp:
    def __init__(self, args):
        self.args = args
        self.b16 = lambda a: a.astype(BF16)
        self.sm = _small_pack(args, "")
        self.sent = {l: self._ag_arrays(l) for l in range(DEPTH)}
        self.queue = [(l, p) for l in range(DEPTH) for p in "abc"][1:]
        self.hosting = None
        got = run_comm(comm_ag(self.sent[0]["a"] + [self.sm]), "all_gather_first")
        self.raw = {0: {"a": got[:-1]}}
        gflat = got[-1].reshape(N_DEV, SM_ROWS * 1024)
        self.small, off = {}, 0
        for n, shape, axis in SMALL:
            size = math.prod(shape)
            self.small[n] = _full_from_gathered(gflat[:, off:off + size], shape, axis)
            off += size
        self.bufs = {"r": jnp.zeros((4, R_ROWS, D_MODEL), BF16), "c704": jnp.zeros((4, 2 * DEPTH * D_MODEL, 704), BF16),
                     "c192": jnp.zeros((4, 2 * D_MODEL, 192), BF16), "c256": jnp.zeros((4, C256_ROWS, RNN_W), BF16),
                     "sm": jnp.zeros((4, SM_ROWS, 1024), F32)}
        self.pending = None
        self.small_g = [{} for _ in range(DEPTH)]
        self.loss = None

    def _ag_arrays(self, l):
        a, b16, j = self.args, self.b16, l // 2
        r_ff = jnp.concatenate([b16(a["ffn1_w_out"][l]), b16(a["ffn2_w_out"][l])], axis=0)
        if l % 2 == 0:
            return {"a": [b16(a["ffn1_w_in"][l]), b16(a["attn_w_qkv"][j])], "b": [r_ff, b16(a["attn_w_o"][j])],
                    "c": [b16(a["ffn2_w_in"][l])]}
        gates = jnp.concatenate([b16(a["lru_w_ra"][j]).reshape(-1, RNN_W), b16(a["lru_w_rx"][j]).reshape(-1, RNN_W)], axis=0)
        return {"a": [b16(a["ffn1_w_in"][l]), b16(a["lru_w_in"][j])], "b": [r_ff, b16(a["lru_w_out"][j]), gates],
                "c": [b16(a["ffn2_w_in"][l])]}

    def weights(self, l, piece):
        raw, j = self.raw[l][piece], l // 2
        full = lambda g: g.transpose(1, 0, 2).reshape(g.shape[1], -1)
        if piece == "a":
            return {"w_in1": full(raw[0]), "m_c": full(raw[1]), "ln_g": self.small["ln_g"][l], "ln_b": self.small["ln_b"][l]}
        if piece == "c":
            return {"w_in2": full(raw[0])}
        wl = {"r_ff": raw[0], "m_o": raw[1]}
        if l % 2:
            g = raw[2].reshape(N_DEV, 2, RNN_BLOCKS, RNN_W // N_DEV, RNN_W).transpose(1, 2, 0, 3, 4)
            g = g.reshape(2, RNN_BLOCKS, RNN_W, RNN_W)
            wl["lru"] = _lru_params(self.small, j, g[0], g[1])
        return wl

    def fwd_slot(self):
        if not self.queue:
            return None
        self.hosting = self.queue.pop(0)
        l, piece = self.hosting
        return comm_ag(self.sent[l][piece])

    def fwd_done(self, outs):
        if self.hosting is not None:
            l, piece = self.hosting
            self.raw.setdefault(l, {})[piece] = outs
            self.hosting = None

    def loss_ready(self, loss):
        self.loss = loss

    def _ranges(self, l, part):
        j = l // 2
        if part == "lo":
            return [("r", [(0, FF_SHARD, l * FF_SHARD)]), ("c704", [(0, D_MODEL, l * D_MODEL)])]
        r = [(0, FF_SHARD, R_FFN2 + l * FF_SHARD), (FF_SHARD, MIX_SHARD, (R_LRU if l % 2 else R_ATTN) + j * MIX_SHARD)]
        c704 = [(0, D_MODEL, (DEPTH + l) * D_MODEL)]
        if l % 2:
            gr = RNN_BLOCKS * (RNN_W // N_DEV)
            mix = ("c256", [(0, D_MODEL, j * D_MODEL), (D_MODEL, gr, C256_RA + j * gr), (D_MODEL + gr, gr, C256_RX + j * gr)])
        else:
            mix = ("c192", [(0, D_MODEL, j * D_MODEL)])
        return [("r", r), ("c704", c704), mix]

    def grads_ready(self, l, part, g):
        self.small_g[l].update(g)
        if part == "lo":
            ts = [_row_shards(g["w_out1"]), g["w_in1"]]
        else:
            tmix = g["mix_in"]
            if l % 2:
                tmix = jnp.concatenate([tmix, _gate_shards(g["w_ra"]), _gate_shards(g["w_rx"])], axis=1)
            ts = [jnp.concatenate([_row_shards(g["w_out2"]), _row_shards(g["mix_out"])], axis=1), g["w_in2"], tmix]
        sends = [t.reshape((4, 2) + t.shape[1:]) for t in ts]
        if (l, part) == (0, "lo"):
            self._last(sends)
        else:
            self.pending = {"tag": f"{l}{part}", "sends": sends, "where": self._ranges(l, part)}

    def _pair_sums(self, tag, sends, gots):
        tiles = (None, 512, 256)
        return [pair_sum(s, g, BF16, t or s.shape[2] // 2, f"pair_sum_{nm}_{tag}")
                for s, g, t, nm in zip(sends, gots, tiles, ("r", "c704", "mix"))]

    def _chip(self, parts, where, pick):
        names = [where[i][0] for i in pick]
        return names, comm_chip([(parts[i], k, where[i][1]) for k, i in enumerate(pick)], [self.bufs[n] for n in names])

    def bwd_slot(self, slot):
        p = self.pending
        if p is None:
            return None
        if slot == "pair":
            return comm_pair(p["sends"])
        pick = [1] if slot == "chip_x" else [i for i in range(len(p["sends"])) if i != 1]
        p["names"], comm = self._chip(p["parts"], p["where"], pick)
        return comm

    def bwd_done(self, slot, outs):
        p = self.pending
        if slot == "pair":
            p["parts"] = self._pair_sums(p["tag"], p["sends"], outs)
            return
        for n, o in zip(p["names"], outs):
            self.bufs[n] = o
        if slot == "chip_y":
            self.pending = None

    def _last(self, sends):
        sg = self.small_g
        stack = lambda key, ls: jnp.stack([sg[l][key] for l in ls])
        every, lru = range(DEPTH), (1, 3)
        small = {"ln_g": stack("ln_g", every), "ln_b": stack("ln_b", every), "lru_conv_w": stack("conv_w", lru),
                 "lru_conv_b": stack("conv_b", lru), "lru_b_ra": stack("b_ra", lru), "lru_b_rx": stack("b_rx", lru),
                 "lru_lambda": stack("lam", lru)}
        tail = jnp.concatenate([stack("sinks", (0, 2)).reshape(-1), self.loss.reshape(1)])
        tail = jnp.pad(tail, (0, SM_ROWS * 1024 - SMALL_N - tail.shape[0]))
        s_sm = jnp.concatenate([_shards_of_full(small[n], shape, axis) for n, shape, axis in SMALL]
                               + [jnp.broadcast_to(tail, (N_DEV, tail.shape[0]))], axis=1).reshape(4, 2, SM_ROWS, 1024)
        gots = run_comm(comm_pair(sends + [s_sm]), "pair_exchange_last")
        parts = self._pair_sums("0lo", sends, gots[:2]) + [pair_sum(s_sm, gots[2], F32, SM_ROWS, "pair_sum_sm")]
        names, comm = self._chip(parts, self._ranges(0, "lo") + [("sm", [(0, SM_ROWS, 0)])], [0, 1, 2])
        for n, o in zip(names, run_comm(comm, "chip_exchange_last")):
            self.bufs[n] = o


def kernel(x, ffn1_w_in, ffn1_w_out, ffn2_w_in, ffn2_w_out, ln_g, ln_b, attn_w_qkv, attn_sinks, attn_w_o, lru_w_in, lru_conv_w, lru_conv_b, lru_w_ra, lru_b_ra, lru_w_rx, lru_b_rx, lru_lambda, lru_w_out, loss_target, m_ffn1_w_in, m_ffn1_w_out, m_ffn2_w_in, m_ffn2_w_out, m_ln_g, m_ln_b, m_attn_w_qkv, m_attn_sinks, m_attn_w_o, m_lru_w_in, m_lru_conv_w, m_lru_conv_b, m_lru_w_ra, m_lru_b_ra, m_lru_w_rx, m_lru_b_rx, m_lru_lambda, m_lru_w_out, v_ffn1_w_in, v_ffn1_w_out, v_ffn2_w_in, v_ffn2_w_out, v_ln_g, v_ln_b, v_attn_w_qkv, v_attn_sinks, v_attn_w_o, v_lru_w_in, v_lru_conv_w, v_lru_conv_b, v_lru_w_ra, v_lru_b_ra, v_lru_w_rx, v_lru_b_rx, v_lru_lambda, v_lru_w_out):
    args = dict(locals())

    sched = _Fsdp(args)
    _, dx = _local_step(x[0], loss_target[0], sched, attn_sinks)
    p_r, p704, p192, p256, p_sm = (sched.bufs[n] for n in ("r", "c704", "c192", "c256", "sm"))
    sm = sched.sm

    fam = lambda n: (args[n], args["m_" + n], args["v_" + n])
    res = {
        "ffn1_w_out": adamw(p_r, 0, *fam("ffn1_w_out"), FF_SHARD, "adamw_ffn1_w_out"),
        "ffn2_w_out": adamw(p_r, R_FFN2, *fam("ffn2_w_out"), FF_SHARD, "adamw_ffn2_w_out"),
        "attn_w_o": adamw(p_r, R_ATTN, *fam("attn_w_o"), 2 * MIX_SHARD, "adamw_attn_w_o"),
        "lru_w_out": adamw(p_r, R_LRU, *fam("lru_w_out"), 2 * MIX_SHARD, "adamw_lru_w_out"),
        "ffn1_w_in": adamw(p704, 0, *fam("ffn1_w_in"), 512, "adamw_ffn1_w_in"),
        "ffn2_w_in": adamw(p704, DEPTH * D_MODEL, *fam("ffn2_w_in"), 512, "adamw_ffn2_w_in"),
        "attn_w_qkv": adamw(p192, 0, *fam("attn_w_qkv"), 512, "adamw_attn_w_qkv"),
        "lru_w_in": adamw(p256, 0, *fam("lru_w_in"), 512, "adamw_lru_w_in"),
        "lru_w_ra": adamw(p256, C256_RA, *fam("lru_w_ra"), 256, "adamw_lru_w_ra"),
        "lru_w_rx": adamw(p256, C256_RX, *fam("lru_w_rx"), 256, "adamw_lru_w_rx"),
    }
    sm_out = adamw(p_sm, 0, sm, _small_pack(args, "m_"), _small_pack(args, "v_"), SM_ROWS, "adamw_small")
    for k, pack in enumerate(sm_out):
        for n, val in _small_unpack(pack).items():
            res.setdefault(n, [None] * 4)[k] = val
    loss_total = sm_out[0].reshape(-1)[LOSS_OFF]
    out = [loss_total, dx[None]]
    for k in range(4):
        out += [res[n][k] for n in WEIGHTS]
    return tuple(out)
```

```python
import math

import jax
import jax.numpy as jnp
import numpy as np
from jax import lax
from jax.experimental import pallas as pl
from jax.experimental.pallas import tpu as pltpu

F32 = jnp.float32
BF16 = jnp.bfloat16

D_MODEL = 1024
DEPTH = 4
N_HEADS = 16
N_KV = 4
HEAD_DIM = 64
GROUP = 4
BLOCK = 128
ROPE_THETA = 10000.0
D_RNN = 1024
RNN_BLOCKS = 4
RNN_W = 256
CONV_W = 4
LRU_C = 8.0
D_FF = 2816
ALPHA = (2.0 * DEPTH) ** 0.25
LN_EPS = 1e-5
QKV = (N_HEADS + 2 * N_KV) * HEAD_DIM
N_DEV = 8

ADAM_LR = 0.001
ADAM_B1 = 0.9
ADAM_B2 = 0.999
ADAM_EPS = 1e-08
ADAM_WD = 0.01
ADAM_STEP = 10

VMEM_LIMIT = 52 * 1024 * 1024
NEG = float(np.finfo(np.float32).min)

MESH = pl.DeviceIdType.MESH
ANY = pl.BlockSpec(memory_space=pl.ANY)

FF_SHARD = D_FF // N_DEV
MIX_SHARD = D_MODEL // N_DEV
R_FFN2 = DEPTH * FF_SHARD
R_ATTN = 2 * DEPTH * FF_SHARD
R_LRU = R_ATTN + 2 * MIX_SHARD
R_ROWS = R_LRU + 2 * MIX_SHARD
C256_RA = 2 * D_MODEL
C256_RX = C256_RA + 2 * RNN_BLOCKS * (RNN_W // N_DEV)
C256_ROWS = C256_RX + 2 * RNN_BLOCKS * (RNN_W // N_DEV)

SMALL = (
    ("ln_g", (4, 3, 128), 2),
    ("ln_b", (4, 3, 128), 2),
    ("lru_conv_w", (2, 4, 128), 2),
    ("lru_conv_b", (2, 128), 1),
    ("lru_b_ra", (2, 128), 1),
    ("lru_b_rx", (2, 128), 1),
    ("lru_lambda", (2, 128), 1),
)
WEIGHTS = ("ffn1_w_in", "ffn1_w_out", "ffn2_w_in", "ffn2_w_out", "ln_g", "ln_b", "attn_w_qkv", "attn_sinks",
           "attn_w_o", "lru_w_in", "lru_conv_w", "lru_conv_b", "lru_w_ra", "lru_b_ra", "lru_w_rx", "lru_b_rx",
           "lru_lambda", "lru_w_out")
SMALL_N = sum(math.prod(s) for _, s, _ in SMALL)
SINK_OFF = SMALL_N
LOSS_OFF = SMALL_N + 32
SM_ROWS = 8


def _cp(*sem):
    return pltpu.CompilerParams(dimension_semantics=sem, vmem_limit_bytes=VMEM_LIMIT)


def _dot(a, b):
    return jnp.dot(a, b, preferred_element_type=F32)


def _dot_tn(a, b):
    return lax.dot_general(a, b, (((0,), (0,)), ((), ())), preferred_element_type=F32)


def _dot_nt(a, b):
    return lax.dot_general(a, b, (((1,), (1,)), ((), ())), preferred_element_type=F32)


def _col_chunks(cols, size=512):
    return [slice(c, min(c + size, cols)) for c in range(0, cols, size)]


def _ln(z, g, b):
    mu = jnp.mean(z, axis=-1, keepdims=True)
    xc = z - mu
    var = jnp.mean(xc * xc, axis=-1, keepdims=True)
    return xc * lax.rsqrt(var + LN_EPS) * g + b


def _rows_view(pack, rows_per_dev, row0):
    return (pack, (N_DEV, rows_per_dev, D_MODEL), (0, row0 // rows_per_dev, 0), (N_DEV * rows_per_dev, D_MODEL))


def _mat_view(arr, cols=None, cblk=0):
    k, n = arr.shape
    cols = n if cols is None else cols
    return (arr, (k, cols), (0, cblk), (k, cols))


def _vspec(view):
    _, bshape, bidx, _ = view
    return pl.BlockSpec(bshape, lambda *_: bidx)


def _vload(view, ref):
    return ref[...].reshape(view[3])


class Comm:
    def __init__(self, ins, out_shapes, sems, start, finish, aliases=None):
        self.ins, self.out_shapes, self.sems = list(ins), list(out_shapes), list(sems)
        self.start, self.finish, self.aliases = start, finish, dict(aliases or {})


def _call(body, *, grid, in_specs, out_specs, out_shape, operands, name, sem, scratch=(), comm=None):
    n_in, n_out, n_scr = len(in_specs), len(out_specs), len(scratch)
    if comm is None:
        return pl.pallas_call(body, grid=grid, in_specs=list(in_specs), out_specs=list(out_specs),
                              out_shape=list(out_shape), scratch_shapes=list(scratch), compiler_params=_cp(*sem),
                              name=name)(*operands), []
    nci, nco = len(comm.ins), len(comm.out_shapes)

    def hosted(*refs):
        ins, cins = refs[:n_in], refs[n_in:n_in + nci]
        o0 = n_in + nci
        outs, couts = refs[o0:o0 + n_out], refs[o0 + n_out:o0 + n_out + nco]
        s0 = o0 + n_out + nco
        scr, csems = refs[s0:s0 + n_scr], refs[s0 + n_scr:]
        first = last = None
        for ax, size in enumerate(grid):
            pid = pl.program_id(ax)
            f, e = pid == 0, pid == size - 1
            first = f if first is None else jnp.logical_and(first, f)
            last = e if last is None else jnp.logical_and(last, e)

        @pl.when(first)
        def _():
            comm.start(cins, couts, csems)

        body(*ins, *outs, *scr)

        @pl.when(last)
        def _():
            comm.finish(cins, couts, csems)

    res = pl.pallas_call(
        hosted, grid=grid, in_specs=list(in_specs) + [ANY] * nci, out_specs=list(out_specs) + [ANY] * nco,
        out_shape=list(out_shape) + comm.out_shapes, scratch_shapes=list(scratch) + comm.sems,
        input_output_aliases={n_in + i: n_out + o for i, o in comm.aliases.items()},
        compiler_params=_cp(*(("arbitrary",) * len(grid))), name=name)(*operands, *comm.ins)
    return res[:n_out], res[n_out:]


def run_comm(comm, name):
    nci, nco = len(comm.ins), len(comm.out_shapes)

    def body(*refs):
        cins, couts, csems = refs[:nci], refs[nci:nci + nco], refs[nci + nco:]
        comm.start(cins, couts, csems)
        comm.finish(cins, couts, csems)

    return pl.pallas_call(
        body, in_specs=[ANY] * nci, out_specs=[ANY] * nco, out_shape=comm.out_shapes, scratch_shapes=comm.sems,
        input_output_aliases=dict(comm.aliases), name=name)(*comm.ins)


def mm_plain(a, wv, out_dtype, name, nt=False, tm=512):
    S, K = a.shape
    N = wv[3][0] if nt else wv[3][1]
    tm = min(tm, S)
    dot = _dot_nt if nt else _dot

    def body(a_ref, w_ref, o_ref):
        o_ref[...] = dot(a_ref[...], _vload(wv, w_ref)).astype(out_dtype)

    return pl.pallas_call(
        body, grid=(S // tm,),
        in_specs=[pl.BlockSpec((tm, K), lambda i: (i, 0)), _vspec(wv)],
        out_specs=pl.BlockSpec((tm, N), lambda i: (i, 0)),
        out_shape=jax.ShapeDtypeStruct((S, N), out_dtype),
        compiler_params=_cp("parallel"), name=name)(a, wv[0])


def mm_res_nt(a, wv, r, alpha, name, tm=512):
    S, K = a.shape
    N = wv[3][0]
    tm = min(tm, S)

    def body(a_ref, w_ref, r_ref, o_ref):
        o_ref[...] = _dot_nt(a_ref[...], _vload(wv, w_ref)) + alpha * r_ref[...]

    return pl.pallas_call(
        body, grid=(S // tm,),
        in_specs=[pl.BlockSpec((tm, K), lambda i: (i, 0)), _vspec(wv), pl.BlockSpec((tm, N), lambda i: (i, 0))],
        out_specs=pl.BlockSpec((tm, N), lambda i: (i, 0)),
        out_shape=jax.ShapeDtypeStruct((S, N), F32),
        compiler_params=_cp("parallel"), name=name)(a, wv[0], r)


def mm_res2_nt(a3, wv0, wv1, r, alpha, name, tm=512, comm=None):
    _, S, K = a3.shape
    N = wv0[3][0]
    tm = min(tm, S)

    def body(a_ref, w0_ref, w1_ref, r_ref, o_ref):
        o_ref[...] = (_dot_nt(a_ref[0], _vload(wv0, w0_ref)) + _dot_nt(a_ref[1], _vload(wv1, w1_ref))
                      + alpha * r_ref[...])

    (out,), couts = _call(
        body, grid=(S // tm,),
        in_specs=[pl.BlockSpec((2, tm, K), lambda i: (0, i, 0)), _vspec(wv0), _vspec(wv1),
                  pl.BlockSpec((tm, N), lambda i: (i, 0))],
        out_specs=[pl.BlockSpec((tm, N), lambda i: (i, 0))],
        out_shape=[jax.ShapeDtypeStruct((S, N), F32)],
        operands=(a3, wv0[0], wv1[0], r), name=name, sem=("parallel",), comm=comm)
    return out, couts


def mm_ln(a, wv, h, g, b, scale, name, tm=512, comm=None):
    S, K = a.shape
    tm = min(tm, S)

    def body(a_ref, w_ref, h_ref, g_ref, b_ref, z_ref, y_ref, yb_ref):
        z = ALPHA * h_ref[...] + scale * _dot(a_ref[...], _vload(wv, w_ref))
        y = _ln(z, g_ref[...], b_ref[...])
        z_ref[...] = z
        y_ref[...] = y
        yb_ref[...] = y.astype(BF16)

    row = pl.BlockSpec((tm, D_MODEL), lambda i: (i, 0))
    vec = pl.BlockSpec((1, D_MODEL), lambda i: (0, 0))
    return _call(
        body, grid=(S // tm,),
        in_specs=[pl.BlockSpec((tm, K), lambda i: (i, 0)), _vspec(wv), row, vec, vec],
        out_specs=[row, row, row],
        out_shape=[jax.ShapeDtypeStruct((S, D_MODEL), F32), jax.ShapeDtypeStruct((S, D_MODEL), F32),
                   jax.ShapeDtypeStruct((S, D_MODEL), BF16)],
        operands=(a, wv[0], h, g.reshape(1, -1), b.reshape(1, -1)), name=name, sem=("parallel",), comm=comm)


def ffn_up(xb, w_in, name, tm=512, tn=1408, comm=None):
    S = xb.shape[0]
    tm = min(tm, S)
    nj = D_FF // tn

    def body(x_ref, wg_ref, wu_ref, jac_ref, a_ref):
        x = x_ref[...]
        g = _dot(x, wg_ref[...])
        u = _dot(x, wu_ref[...])
        sg = jax.nn.sigmoid(g)
        t = g * sg
        jac_ref[0] = (u * (sg * ((g - t) + 1.0))).astype(BF16)
        jac_ref[1] = t.astype(BF16)
        a_ref[...] = (t * u).astype(BF16)

    return _call(
        body, grid=(nj, S // tm),
        in_specs=[pl.BlockSpec((tm, D_MODEL), lambda j, i: (i, 0)),
                  pl.BlockSpec((D_MODEL, tn), lambda j, i: (0, j)),
                  pl.BlockSpec((D_MODEL, tn), lambda j, i: (0, nj + j))],
        out_specs=[pl.BlockSpec((2, tm, tn), lambda j, i: (0, i, j)), pl.BlockSpec((tm, tn), lambda j, i: (i, j))],
        out_shape=[jax.ShapeDtypeStruct((2, S, D_FF), BF16), jax.ShapeDtypeStruct((S, D_FF), BF16)],
        operands=(xb, w_in, w_in), name=name, sem=("parallel", "parallel"), comm=comm)


def ffn_mid_bwd(dfb, r_ff, blk, gu, name, tm=512, comm=None):
    S = dfb.shape[0]
    tm = min(tm, S)
    tn = 4 * FF_SHARD

    def body(df_ref, w_ref, gu_ref, dh_ref):
        w = w_ref[...].reshape(tn, D_MODEL)
        df = df_ref[...]
        for cols in _col_chunks(tn):
            da = _dot_nt(df, w[cols, :])
            dh_ref[0, :, cols] = (da * gu_ref[0, :, cols].astype(F32)).astype(BF16)
            dh_ref[1, :, cols] = (da * gu_ref[1, :, cols].astype(F32)).astype(BF16)

    gspec = pl.BlockSpec((2, tm, tn), lambda j, i: (0, i, j))
    (out,), couts = _call(
        body, grid=(2, S // tm),
        in_specs=[pl.BlockSpec((tm, D_MODEL), lambda j, i: (i, 0)),
                  pl.BlockSpec((4, FF_SHARD, D_MODEL), lambda j, i: (j, blk, 0)), gspec],
        out_specs=[gspec],
        out_shape=[jax.ShapeDtypeStruct((2, S, D_FF), BF16)],
        operands=(dfb, r_ff, gu), name=name, sem=("parallel", "parallel"), comm=comm)
    return out, couts


def mm_tn(a, b, name, tm, tn, ts=2048, planes=1, shard=None, comm=None):
    S, M = a.shape
    N = b.shape[-1] * planes
    ts = min(ts, S)
    per = b.shape[-1] // tn
    ns = S // ts

    if shard is None:
        def body(a_ref, b_ref, o_ref):
            @pl.when(pl.program_id(2) == 0)
            def _():
                o_ref[...] = jnp.zeros_like(o_ref)

            o_ref[...] += _dot_tn(a_ref[...], b_ref[...])

        out_spec = pl.BlockSpec((tm, tn), lambda i, j, s: (i, j))
        out_shape = jax.ShapeDtypeStruct((M, N), F32)
        scratch = ()
    else:
        def body(a_ref, b_ref, o_ref, acc_ref):
            s = pl.program_id(2)

            @pl.when(s == 0)
            def _():
                acc_ref[...] = jnp.zeros_like(acc_ref)

            acc_ref[...] += _dot_tn(a_ref[...], b_ref[...])

            @pl.when(s == ns - 1)
            def _():
                for q in range(tn // shard):
                    o_ref[q] = acc_ref[:, shard * q:shard * (q + 1)]

        out_spec = pl.BlockSpec((tn // shard, tm, shard), lambda i, j, s: (j, i, 0))
        out_shape = jax.ShapeDtypeStruct((N // shard, M, shard), F32)
        scratch = (pltpu.VMEM((tm, tn), F32),)

    if planes == 1:
        bspec = pl.BlockSpec((ts, tn), lambda i, j, s: (s, j))
    else:
        bspec = pl.BlockSpec((None, ts, tn), lambda i, j, s: (j // per, s, j % per))
    (out,), couts = _call(
        body, grid=(M // tm, N // tn, ns),
        in_specs=[pl.BlockSpec((ts, tm), lambda i, j, s: (s, i)), bspec],
        out_specs=[out_spec], out_shape=[out_shape], scratch=scratch,
        operands=(a, b), name=name, sem=("parallel", "parallel", "arbitrary"), comm=comm)
    return out, couts


def ln_bwd(dy, z, g, out_scale, name, tm=512):
    S = dy.shape[0]
    tm = min(tm, S)

    def body(dy_ref, z_ref, g_ref, dz_ref, dzb_ref, dg_ref, db_ref):
        @pl.when(pl.program_id(0) == 0)
        def _():
            dg_ref[...] = jnp.zeros_like(dg_ref)
            db_ref[...] = jnp.zeros_like(db_ref)

        z = z_ref[...]
        dy_ = dy_ref[...]
        mu = jnp.mean(z, axis=-1, keepdims=True)
        xc = z - mu
        var = jnp.mean(xc * xc, axis=-1, keepdims=True)
        rstd = lax.rsqrt(var + LN_EPS)
        xh = xc * rstd
        dxh = dy_ * g_ref[...]
        m1 = jnp.mean(dxh, axis=-1, keepdims=True)
        m2 = jnp.mean(dxh * xh, axis=-1, keepdims=True)
        dz = rstd * (dxh - m1 - xh * m2)
        dz_ref[...] = dz
        dzb_ref[...] = (out_scale * dz).astype(BF16)
        dg_ref[...] += jnp.sum(dy_ * xh, axis=0, keepdims=True)
        db_ref[...] += jnp.sum(dy_, axis=0, keepdims=True)

    row = pl.BlockSpec((tm, D_MODEL), lambda i: (i, 0))
    vec = pl.BlockSpec((1, D_MODEL), lambda i: (0, 0))
    return pl.pallas_call(
        body, grid=(S // tm,),
        in_specs=[row, row, vec],
        out_specs=[row, row, vec, vec],
        out_shape=[jax.ShapeDtypeStruct((S, D_MODEL), F32), jax.ShapeDtypeStruct((S, D_MODEL), BF16),
                   jax.ShapeDtypeStruct((1, D_MODEL), F32), jax.ShapeDtypeStruct((1, D_MODEL), F32)],
        compiler_params=_cp("arbitrary"), name=name)(dy, z, g.reshape(1, -1))


def loss_head(y, t, name, tm=512):
    S = y.shape[0]
    tm = min(tm, S)
    nt = S // tm

    def body(y_ref, t_ref, dy_ref, l_ref):
        i = pl.program_id(0)

        @pl.when(i == 0)
        def _():
            l_ref[...] = jnp.zeros_like(l_ref)

        e = y_ref[...] - t_ref[...]
        dy_ref[...] = e * (1.0 / D_MODEL)
        l_ref[...] += jnp.sum(e * e, axis=0, keepdims=True)

        @pl.when(i == nt - 1)
        def _():
            tot = jnp.sum(l_ref[...], axis=1, keepdims=True) * (0.5 / D_MODEL)
            l_ref[...] = jnp.broadcast_to(tot, l_ref.shape)

    row = pl.BlockSpec((tm, D_MODEL), lambda i: (i, 0))
    vec = pl.BlockSpec((1, D_MODEL), lambda i: (0, 0))
    return pl.pallas_call(
        body, grid=(nt,), in_specs=[row, row], out_specs=[row, vec],
        out_shape=[jax.ShapeDtypeStruct((S, D_MODEL), F32), jax.ShapeDtypeStruct((1, D_MODEL), F32)],
        compiler_params=_cp("arbitrary"), name=name)(y, t)


def _rope_tables(S):
    pos = jnp.arange(S, dtype=F32)
    inv_freq = ROPE_THETA ** (-jnp.arange(0, HEAD_DIM, 2, dtype=F32) / HEAD_DIM)
    ang = pos[:, None] * inv_freq[None, :]
    cos, sin = jnp.cos(ang), jnp.sin(ang)
    cosf = jnp.concatenate([cos, cos, cos, cos], axis=1)
    sinf = jnp.concatenate([-sin, sin, -sin, sin], axis=1)
    return cosf, sinf


def _rot(t, c, s, first):
    sw = jnp.where(first, pltpu.roll(t, 96, 1), pltpu.roll(t, 32, 1))
    return t * c + sw * s


def rope_fwd(qkv, cosf, sinf, name, tm=512):
    S = qkv.shape[0]
    tm = min(tm, S)

    def body(x_ref, c_ref, s_ref, q_ref, k_ref, v_ref):
        c = c_ref[...]
        s = s_ref[...]
        first = (lax.broadcasted_iota(jnp.int32, (tm, 128), 1) % HEAD_DIM) < (HEAD_DIM // 2)
        for j in range(8):
            q_ref[:, 128 * j:128 * (j + 1)] = _rot(x_ref[:, 128 * j:128 * (j + 1)], c, s, first).astype(BF16)
        for j in range(2):
            k_ref[:, 128 * j:128 * (j + 1)] = _rot(x_ref[:, 1024 + 128 * j:1024 + 128 * (j + 1)], c, s, first).astype(BF16)
        v_ref[...] = x_ref[:, 1280:1536].astype(BF16)

    tab = pl.BlockSpec((tm, 128), lambda i: (i, 0))
    return pl.pallas_call(
        body, grid=(S // tm,),
        in_specs=[pl.BlockSpec((tm, QKV), lambda i: (i, 0)), tab, tab],
        out_specs=[pl.BlockSpec((tm, 1024), lambda i: (i, 0)), pl.BlockSpec((tm, 256), lambda i: (i, 0)),
                   pl.BlockSpec((tm, 256), lambda i: (i, 0))],
        out_shape=[jax.ShapeDtypeStruct((S, 1024), BF16), jax.ShapeDtypeStruct((S, 256), BF16),
                   jax.ShapeDtypeStruct((S, 256), BF16)],
        compiler_params=_cp("parallel"), name=name)(qkv, cosf, sinf)


def rope_bwd(dq, dkc, dkp, dvc, dvp, cosf, sinf, name, tq):
    S = dq.shape[0]
    nt = S // tq

    def body(dq_ref, dkc_ref, dkp_ref, dvc_ref, dvp_ref, c_ref, s_ref, o_ref):
        i = pl.program_id(0)
        c = c_ref[...]
        s = -s_ref[...]
        first = (lax.broadcasted_iota(jnp.int32, (tq, 128), 1) % HEAD_DIM) < (HEAD_DIM // 2)
        for j in range(8):
            o_ref[:, 128 * j:128 * (j + 1)] = _rot(dq_ref[:, 128 * j:128 * (j + 1)], c, s, first).astype(BF16)
        has_next = i < nt - 1
        rows = lax.broadcasted_iota(jnp.int32, (tq, 256), 0)
        pad = jnp.zeros((tq - BLOCK, 256), F32)
        halo_k = jnp.concatenate([pad, dkp_ref[...]], axis=0)
        halo_v = jnp.concatenate([pad, dvp_ref[...]], axis=0)
        use = jnp.logical_and(has_next, rows >= tq - BLOCK)
        dk = dkc_ref[...] + jnp.where(use, halo_k, 0.0)
        dv = dvc_ref[...] + jnp.where(use, halo_v, 0.0)
        for j in range(2):
            o_ref[:, 1024 + 128 * j:1024 + 128 * (j + 1)] = _rot(dk[:, 128 * j:128 * (j + 1)], c, s, first).astype(BF16)
        o_ref[:, 1280:1536] = dv.astype(BF16)

    tab = pl.BlockSpec((tq, 128), lambda i: (i, 0))
    cur = pl.BlockSpec((tq, 256), lambda i: (i, 0))
    nxt = pl.BlockSpec((BLOCK, 256), lambda i: (jnp.minimum(i + 1, nt - 1), 0))
    return pl.pallas_call(
        body, grid=(nt,),
        in_specs=[pl.BlockSpec((tq, 1024), lambda i: (i, 0)), cur, nxt, cur, nxt, tab, tab],
        out_specs=pl.BlockSpec((tq, QKV), lambda i: (i, 0)),
        out_shape=jax.ShapeDtypeStruct((S, QKV), BF16),
        compiler_params=_cp("parallel"), name=name)(dq, dkc, dkp, dvc, dvp, cosf, sinf)


def _attn_masks(n):
    shape = (GROUP * BLOCK, 2 * BLOCK)
    r = lax.broadcasted_iota(jnp.int32, shape, 0) % BLOCK
    c = lax.broadcasted_iota(jnp.int32, shape, 1)
    ok = jnp.logical_and(c > r, c <= r + BLOCK)
    return jnp.logical_and(ok, jnp.logical_or(c >= BLOCK, n > 0)), ok


def _attn_probs(qs, kw, sink_col, ok):
    s = jnp.where(ok, _dot_nt(qs, kw) * (HEAD_DIM ** -0.5), NEG)
    m = jnp.maximum(jnp.max(s, axis=1, keepdims=True), sink_col)
    p = jnp.exp(s - m)
    es = jnp.exp(sink_col - m)
    inv = 1.0 / (jnp.sum(p, axis=1, keepdims=True) + es)
    return p * inv, es * inv


def _sink_col(sink_ref, g):
    rid = lax.broadcasted_iota(jnp.int32, (GROUP * BLOCK, 1), 0) // BLOCK
    col = jnp.zeros((GROUP * BLOCK, 1), F32)
    for j in range(GROUP):
        col = jnp.where(rid == j, sink_ref[GROUP * g + j], col)
    return col


KV_PAIR = 2
Q_LANES = KV_PAIR * GROUP * HEAD_DIM


def _low_half(shape):
    return lax.broadcasted_iota(jnp.int32, shape, 1) < HEAD_DIM


def _head_rows(ref, lo, gi):
    low = _low_half((BLOCK, 2 * HEAD_DIM))
    out = []
    for s in range(GROUP // 2):
        c0 = gi * GROUP * HEAD_DIM + 2 * HEAD_DIM * s
        x = ref[lo:lo + BLOCK, c0:c0 + 2 * HEAD_DIM]
        out += [jnp.where(low, x, jnp.zeros_like(x)), jnp.where(low, jnp.zeros_like(x), x)]
    return jnp.concatenate(out, axis=0)


def _head_slabs(x):
    low = _low_half((BLOCK, 2 * HEAD_DIM))
    return [jnp.where(low, x[2 * BLOCK * s:2 * BLOCK * s + BLOCK], x[2 * BLOCK * s + BLOCK:2 * BLOCK * (s + 1)])
            for s in range(GROUP // 2)]


def _kv_window(kc_ref, kp_ref, b, gi):
    if b == 0:
        x = jnp.concatenate([kp_ref[...], kc_ref[0:BLOCK, :]], axis=0)
    else:
        x = kc_ref[BLOCK * (b - 1):BLOCK * (b + 1), :]
    other = pltpu.roll(x, HEAD_DIM, 1)
    low = _low_half(x.shape)
    return jnp.where(low, x, other) if gi == 0 else jnp.where(low, other, x)


def _fold_halves(x, gi):
    tot = x + pltpu.roll(x, HEAD_DIM, 1)
    low = _low_half(x.shape)
    return jnp.where(low if gi == 0 else jnp.logical_not(low), tot, 0.0)


def _attn_specs(tq):
    nsub = tq // BLOCK
    qspec = pl.BlockSpec((tq, Q_LANES), lambda p, n: (n, p))
    cur = pl.BlockSpec((tq, KV_PAIR * HEAD_DIM), lambda p, n: (n, p))
    prev = pl.BlockSpec((BLOCK, KV_PAIR * HEAD_DIM), lambda p, n: (jnp.maximum(n * nsub - 1, 0), p))
    return qspec, cur, prev


def attn_fwd(q, k, v, sinks, name, tq, comm=None):
    S = q.shape[0]
    nsub = tq // BLOCK

    def body(sink_ref, q_ref, kc_ref, kp_ref, vc_ref, vp_ref, o_ref):
        p = pl.program_id(0)
        n = pl.program_id(1)
        ok_first, ok_rest = _attn_masks(n)
        for gi in range(KV_PAIR):
            sink_col = _sink_col(sink_ref, KV_PAIR * p + gi)
            base = gi * GROUP * HEAD_DIM
            for b in range(nsub):
                lo = BLOCK * b
                qs = _head_rows(q_ref, lo, gi)
                pn, _ = _attn_probs(qs, _kv_window(kc_ref, kp_ref, b, gi), sink_col, ok_rest if b else ok_first)
                o = _dot(pn.astype(BF16), _kv_window(vc_ref, vp_ref, b, gi))
                for s, slab in enumerate(_head_slabs(o)):
                    c0 = base + 2 * HEAD_DIM * s
                    o_ref[lo:lo + BLOCK, c0:c0 + 2 * HEAD_DIM] = slab.astype(BF16)

    qspec, cur, prev = _attn_specs(tq)
    (out,), couts = _call(
        body, grid=(N_KV // KV_PAIR, S // tq),
        in_specs=[pl.BlockSpec(memory_space=pltpu.SMEM), qspec, cur, prev, cur, prev],
        out_specs=[qspec],
        out_shape=[jax.ShapeDtypeStruct((S, N_HEADS * HEAD_DIM), BF16)],
        operands=(sinks, q, k, k, v, v), name=name, sem=("parallel", "parallel"), comm=comm)
    return out, couts


def attn_bwd(q, k, v, do, sinks, name, tq):
    S = q.shape[0]
    nsub = tq // BLOCK
    nt = S // tq

    def body(sink_ref, q_ref, kc_ref, kp_ref, vc_ref, vp_ref, do_ref, dq_ref, dkc_ref, dkp_ref, dvc_ref, dvp_ref, ds_ref):
        p = pl.program_id(0)
        n = pl.program_id(1)

        @pl.when(n == 0)
        def _():
            ds_ref[...] = jnp.zeros_like(ds_ref)

        for ref in (dkc_ref, dvc_ref, dkp_ref, dvp_ref):
            ref[...] = jnp.zeros_like(ref)
        rid = lax.broadcasted_iota(jnp.int32, (GROUP * BLOCK, 1), 0) // BLOCK
        sub = lax.broadcasted_iota(jnp.int32, (8, 128), 0)
        ok_first, ok_rest = _attn_masks(n)
        for gi in range(KV_PAIR):
            sink_col = _sink_col(sink_ref, KV_PAIR * p + gi)
            base = gi * GROUP * HEAD_DIM
            dsink = jnp.zeros((8, 128), F32)
            for b in range(nsub):
                lo = BLOCK * b
                qs = _head_rows(q_ref, lo, gi)
                dos = _head_rows(do_ref, lo, gi)
                kw = _kv_window(kc_ref, kp_ref, b, gi)
                vw = _kv_window(vc_ref, vp_ref, b, gi)
                pn, ps = _attn_probs(qs, kw, sink_col, ok_rest if b else ok_first)
                dp = _dot_nt(dos, vw)
                delta = jnp.sum(pn * dp, axis=1, keepdims=True)
                dsb = (pn * (dp - delta) * (HEAD_DIM ** -0.5)).astype(BF16)
                for s, slab in enumerate(_head_slabs(_dot(dsb, kw))):
                    c0 = base + 2 * HEAD_DIM * s
                    dq_ref[lo:lo + BLOCK, c0:c0 + 2 * HEAD_DIM] = slab
                dkw = _fold_halves(_dot_tn(dsb, qs), gi)
                dvw = _fold_halves(_dot_tn(pn.astype(BF16), dos), gi)
                if b == 0:
                    dkp_ref[...] += dkw[0:BLOCK]
                    dvp_ref[...] += dvw[0:BLOCK]
                else:
                    dkc_ref[lo - BLOCK:lo, :] += dkw[0:BLOCK]
                    dvc_ref[lo - BLOCK:lo, :] += dvw[0:BLOCK]
                dkc_ref[lo:lo + BLOCK, :] += dkw[BLOCK:2 * BLOCK]
                dvc_ref[lo:lo + BLOCK, :] += dvw[BLOCK:2 * BLOCK]
                sd = ps * delta
                for j in range(GROUP):
                    tot = -jnp.sum(jnp.where(rid == j, sd, 0.0))
                    dsink = dsink + jnp.where(sub == j, tot, 0.0)
            ds_ref[gi] += dsink

    qspec, cur, prev = _attn_specs(tq)
    halo = pl.BlockSpec((BLOCK, KV_PAIR * HEAD_DIM), lambda p, n: (n, p))
    kv_shape = jax.ShapeDtypeStruct((S, N_KV * HEAD_DIM), F32)
    halo_shape = jax.ShapeDtypeStruct((nt * BLOCK, N_KV * HEAD_DIM), F32)
    return pl.pallas_call(
        body, grid=(N_KV // KV_PAIR, nt),
        in_specs=[pl.BlockSpec(memory_space=pltpu.SMEM), qspec, cur, prev, cur, prev, qspec],
        out_specs=[qspec, cur, halo, cur, halo, pl.BlockSpec((KV_PAIR, 8, 128), lambda p, n: (p, 0, 0))],
        out_shape=[jax.ShapeDtypeStruct((S, N_HEADS * HEAD_DIM), F32), kv_shape, halo_shape, kv_shape, halo_shape,
                   jax.ShapeDtypeStruct((N_KV, 8, 128), F32)],
        compiler_params=_cp("parallel", "arbitrary"), name=name)(sinks, q, k, k, v, v, do)


def _rows_before(cur, prev8, k):
    if k == 0:
        return cur
    n = cur.shape[0]
    ext = jnp.concatenate([prev8, cur], axis=0)
    return ext[8 - k:8 - k + n]


def _rows_after(cur, next8, k):
    if k == 0:
        return cur
    n = cur.shape[0]
    ext = jnp.concatenate([cur, next8], axis=0)
    return ext[k:k + n]


def _gelu(x):
    c = math.sqrt(2.0 / math.pi)
    t = jnp.tanh(c * (x + 0.044715 * (x * x * x)))
    return 0.5 * (1.0 + t), t


def _neg_expm1(u):
    ser = 1.0 + u * (1.0 / 6.0)
    for k in range(5, 1, -1):
        ser = 1.0 + (u * (1.0 / k)) * ser
    return jnp.where(u > -0.125, -(u * ser), 1.0 - jnp.exp(u))


def _block_diag(xb16, w_ref):
    return jnp.concatenate([_dot(xb16[:, RNN_W * n:RNN_W * (n + 1)], w_ref[n]) for n in range(RNN_BLOCKS)], axis=1)


def _lru_gates(xb, prev8, cw_ref, cb_ref, wra_ref, wrx_ref, bra_ref, brx_ref, lsl_ref):
    xc = cb_ref[...] + cw_ref[3:4, :] * xb
    for w in range(CONV_W - 1):
        xc = xc + cw_ref[w:w + 1, :] * _rows_before(xb, prev8, CONV_W - 1 - w)
    xcb = xc.astype(BF16)
    r = jax.nn.sigmoid(_block_diag(xcb, wra_ref) + bra_ref[...])
    ig = jax.nn.sigmoid(_block_diag(xcb, wrx_ref) + brx_ref[...])
    la = LRU_C * r * lsl_ref[...]
    a = jnp.exp(la)
    sq = jnp.sqrt(_neg_expm1(2.0 * la))
    return xc, xcb, r, ig, a, sq


def lru_fwd(xg, p, name, tm=256, comm=None):
    S = xg.shape[0]
    tm = min(tm, S)

    def body(xg_ref, xp_ref, cw_ref, cb_ref, wra_ref, wrx_ref, bra_ref, brx_ref, lsl_ref, y_ref, h_ref, hc_ref, a_s, b_s):
        i = pl.program_id(0)
        xb = xg_ref[:, 0:D_RNN]
        gb = xg_ref[:, D_RNN:2 * D_RNN]
        prev8 = jnp.where(i > 0, xp_ref[:, 0:D_RNN], 0.0)
        xc, _, r, ig, a, sq = _lru_gates(xb, prev8, cw_ref, cb_ref, wra_ref, wrx_ref, bra_ref, brx_ref, lsl_ref)
        a_s[...] = a
        b_s[...] = sq * (ig * xc)

        @pl.when(i == 0)
        def _():
            hc_ref[...] = jnp.zeros_like(hc_ref)

        def chunk(c, h):
            o = pl.multiple_of(c * 8, 8)
            av = a_s[pl.ds(o, 8), :]
            bv = b_s[pl.ds(o, 8), :]
            rows = []
            for t in range(8):
                h = av[t:t + 1, :] * h + bv[t:t + 1, :]
                rows.append(h)
            h_ref[pl.ds(o, 8), :] = jnp.concatenate(rows, axis=0)
            return h

        h_last = lax.fori_loop(0, tm // 8, chunk, hc_ref[0:1, :])
        hc_ref[0:1, :] = h_last
        cdf, _ = _gelu(gb)
        y_ref[...] = (h_ref[...] * (gb * cdf)).astype(BF16)

    vec = pl.BlockSpec((1, D_RNN), lambda i: (0, 0))
    wsp = pl.BlockSpec((RNN_BLOCKS, RNN_W, RNN_W), lambda i: (0, 0, 0))
    return _call(
        body, grid=(S // tm,),
        in_specs=[pl.BlockSpec((tm, 2 * D_RNN), lambda i: (i, 0)),
                  pl.BlockSpec((8, 2 * D_RNN), lambda i: (jnp.maximum(i * (tm // 8) - 1, 0), 0)),
                  pl.BlockSpec((CONV_W, D_RNN), lambda i: (0, 0)), vec, wsp, wsp, vec, vec, vec],
        out_specs=[pl.BlockSpec((tm, D_RNN), lambda i: (i, 0)), pl.BlockSpec((tm, D_RNN), lambda i: (i, 0))],
        out_shape=[jax.ShapeDtypeStruct((S, D_RNN), BF16), jax.ShapeDtypeStruct((S, D_RNN), F32)],
        scratch=[pltpu.VMEM((8, D_RNN), F32), pltpu.VMEM((tm, D_RNN), F32), pltpu.VMEM((tm, D_RNN), F32)],
        operands=(xg, xg, p["conv_w"], p["conv_b"], p["w_ra"], p["w_rx"], p["b_ra"], p["b_rx"], p["lsl"]),
        name=name, sem=("arbitrary",), comm=comm)


def lru_bwd(dy, xg, h, p, name, tm=256):
    S = xg.shape[0]
    tm = min(tm, S)
    nt = S // tm

    def body(dy_ref, xg_ref, xp_ref, h_ref, hp_ref, cw_ref, cb_ref, wra_ref, wrx_ref, bra_ref, brx_ref,
             lsl_ref, dxg_ref, dcw_ref, dcb_ref, dwra_ref, dwrx_ref, dbra_ref, dbrx_ref, dlam_ref,
             lc_ref, nx_ref, a_s, g_s, l_s):
        i = pl.program_id(0)
        ti = nt - 1 - i

        @pl.when(i == 0)
        def _():
            lc_ref[...] = jnp.zeros_like(lc_ref)
            nx_ref[...] = jnp.zeros_like(nx_ref)
            for ref in (dcw_ref, dcb_ref, dwra_ref, dwrx_ref, dbra_ref, dbrx_ref, dlam_ref):
                ref[...] = jnp.zeros_like(ref)

        xb = xg_ref[:, 0:D_RNN]
        gb = xg_ref[:, D_RNN:2 * D_RNN]
        prev8 = jnp.where(ti > 0, xp_ref[:, 0:D_RNN], 0.0)
        xc, xcb, r, ig, a, sq = _lru_gates(xb, prev8, cw_ref, cb_ref, wra_ref, wrx_ref, bra_ref, brx_ref, lsl_ref)
        hh = h_ref[...]
        hprev = _rows_before(hh, jnp.where(ti > 0, hp_ref[...], 0.0), 1)
        dy_ = dy_ref[...]
        cdf, th = _gelu(gb)
        c0 = math.sqrt(2.0 / math.pi)
        dgate = cdf + gb * (0.5 * (1.0 - th * th) * c0 * (1.0 + 3.0 * 0.044715 * gb * gb))
        dgb = dy_ * hh * dgate
        a_s[...] = a
        g_s[...] = dy_ * (gb * cdf)

        def chunk(cc, carry):
            o = pl.multiple_of((tm // 8 - 1 - cc) * 8, 8)
            av = a_s[pl.ds(o, 8), :]
            gv = g_s[pl.ds(o, 8), :]
            rows = [None] * 8
            for t in range(7, -1, -1):
                lam_t = gv[t:t + 1, :] + carry
                rows[t] = lam_t
                carry = av[t:t + 1, :] * lam_t
            l_s[pl.ds(o, 8), :] = jnp.concatenate(rows, axis=0)
            return carry

        carry = lax.fori_loop(0, tm // 8, chunk, lc_ref[0:1, :])
        lc_ref[0:1, :] = carry
        lam = l_s[...]
        da = lam * hprev
        dixc = lam * sq
        di = dixc * xc
        dxc = dixc * ig
        dsq = lam * (ig * xc)
        dla = da * a - dsq * (a * a / sq)
        dr = dla * (LRU_C * lsl_ref[...])
        dlam_ref[...] += jnp.sum(dla * (LRU_C * r), axis=0, keepdims=True)
        dpr = dr * r * (1.0 - r)
        dpi = di * ig * (1.0 - ig)
        dbra_ref[...] += jnp.sum(dpr, axis=0, keepdims=True)
        dbrx_ref[...] += jnp.sum(dpi, axis=0, keepdims=True)
        dprb = dpr.astype(BF16)
        dpib = dpi.astype(BF16)
        back = []
        for n in range(RNN_BLOCKS):
            sl = slice(RNN_W * n, RNN_W * (n + 1))
            dwra_ref[n] += _dot_tn(xcb[:, sl], dprb[:, sl])
            dwrx_ref[n] += _dot_tn(xcb[:, sl], dpib[:, sl])
            back.append(_dot_nt(dprb[:, sl], wra_ref[n]) + _dot_nt(dpib[:, sl], wrx_ref[n]))
        dxc = dxc + jnp.concatenate(back, axis=1)
        dcb_ref[...] += jnp.sum(dxc, axis=0, keepdims=True)
        next8 = nx_ref[...]
        dxb = cw_ref[3:4, :] * dxc
        dcw_ref[3:4, :] += jnp.sum(dxc * xb, axis=0, keepdims=True)
        for w in range(CONV_W - 1):
            k = CONV_W - 1 - w
            dcw_ref[w:w + 1, :] += jnp.sum(dxc * _rows_before(xb, prev8, k), axis=0, keepdims=True)
            dxb = dxb + cw_ref[w:w + 1, :] * _rows_after(dxc, next8, k)
        nx_ref[...] = dxc[0:8, :]
        dxg_ref[:, 0:D_RNN] = dxb.astype(BF16)
        dxg_ref[:, D_RNN:2 * D_RNN] = dgb.astype(BF16)

    rev = lambda i: (nt - 1 - i, 0)
    before = lambda i: (jnp.maximum((nt - 1 - i) * (tm // 8) - 1, 0), 0)
    vec = pl.BlockSpec((1, D_RNN), lambda i: (0, 0))
    wsp = pl.BlockSpec((RNN_BLOCKS, RNN_W, RNN_W), lambda i: (0, 0, 0))
    cwsp = pl.BlockSpec((CONV_W, D_RNN), lambda i: (0, 0))
    return pl.pallas_call(
        body, grid=(nt,),
        in_specs=[pl.BlockSpec((tm, D_RNN), rev), pl.BlockSpec((tm, 2 * D_RNN), rev), pl.BlockSpec((8, 2 * D_RNN), before),
                  pl.BlockSpec((tm, D_RNN), rev), pl.BlockSpec((8, D_RNN), before),
                  cwsp, vec, wsp, wsp, vec, vec, vec],
        out_specs=[pl.BlockSpec((tm, 2 * D_RNN), rev), cwsp, vec, wsp, wsp, vec, vec, vec],
        out_shape=[jax.ShapeDtypeStruct((S, 2 * D_RNN), BF16), jax.ShapeDtypeStruct((CONV_W, D_RNN), F32),
                   jax.ShapeDtypeStruct((1, D_RNN), F32), jax.ShapeDtypeStruct((RNN_BLOCKS, RNN_W, RNN_W), F32),
                   jax.ShapeDtypeStruct((RNN_BLOCKS, RNN_W, RNN_W), F32), jax.ShapeDtypeStruct((1, D_RNN), F32),
                   jax.ShapeDtypeStruct((1, D_RNN), F32), jax.ShapeDtypeStruct((1, D_RNN), F32)],
        scratch_shapes=[pltpu.VMEM((8, D_RNN), F32), pltpu.VMEM((8, D_RNN), F32), pltpu.VMEM((tm, D_RNN), F32),
                        pltpu.VMEM((tm, D_RNN), F32), pltpu.VMEM((tm, D_RNN), F32)],
        compiler_params=_cp("arbitrary"), name=name)(
            dy, xg, xg, h, h, p["conv_w"], p["conv_b"], p["w_ra"], p["w_rx"], p["b_ra"], p["b_rx"], p["lsl"])


def _place():
    x, y, c = lax.axis_index("x"), lax.axis_index("y"), lax.axis_index("c")
    chips = [(1 - x, y), (x, 1 - y), (1 - x, 1 - y)]
    return x, y, c, chips


def comm_ag(arrs):
    n = len(arrs)

    def copies(ins, outs, sems):
        send_sems, recv_sems, local_sems = sems
        x, y, c, chips = _place()
        me, sibling = (x, y, c), (x, y, 1 - c)

        def rows(a, px, py, pc):
            return outs[a].at[4 * px + 2 * py + pc]

        def copy(a, k, block, to, own=False):
            return pltpu.make_async_remote_copy(
                src_ref=ins[a] if own else rows(a, *block), dst_ref=rows(a, *block),
                send_sem=send_sems.at[k, a], recv_sem=recv_sems.at[k, a], device_id=to, device_id_type=MESH)

        mine = [pltpu.make_async_copy(ins[a], rows(a, *me), local_sems.at[a]) for a in range(n)]
        first = [copy(a, 1 + j, me, (*chip, c), own=True) for j, chip in enumerate(chips) for a in range(n)]
        first += [copy(a, 0, me, sibling, own=True) for a in range(n)]
        return copy, chips, c, me, sibling, mine, first

    def start(ins, outs, sems):
        _, _, _, _, _, mine, first = copies(ins, outs, sems)
        for cp in mine + first:
            cp.start()

    def finish(ins, outs, sems):
        copy, chips, c, me, sibling, mine, first = copies(ins, outs, sems)
        passed = []
        for j, chip in enumerate(chips):
            for a in range(n):
                copy(a, 1 + j, (*chip, c), me).wait_recv()
                fwd = copy(a, 4 + j, (*chip, c), sibling)
                fwd.start()
                passed.append(fwd)
        for a in range(n):
            copy(a, 0, sibling, me).wait_recv()
        for j, chip in enumerate(chips):
            for a in range(n):
                copy(a, 4 + j, (*chip, 1 - c), me).wait_recv()
        for cp in first + passed:
            cp.wait_send()
        for cp in mine:
            cp.wait()

    return Comm(arrs, [jax.ShapeDtypeStruct((N_DEV,) + p.shape, p.dtype) for p in arrs],
                [pltpu.SemaphoreType.DMA((7, n)), pltpu.SemaphoreType.DMA((7, n)), pltpu.SemaphoreType.DMA((n,))],
                start, finish)


def comm_pair(sends):
    n = len(sends)

    def copies(ins, outs, sems):
        send_sems, recv_sems = sems
        x, y, c, _ = _place()
        return [pltpu.make_async_remote_copy(
            src_ref=ins[a].at[k, 1 - c], dst_ref=outs[a].at[k], send_sem=send_sems.at[k, a], recv_sem=recv_sems.at[k, a],
            device_id=(x, y, 1 - c), device_id_type=MESH) for k in range(4) for a in range(n)]

    def start(ins, outs, sems):
        for cp in copies(ins, outs, sems):
            cp.start()

    def finish(ins, outs, sems):
        for cp in copies(ins, outs, sems):
            cp.wait()

    return Comm(sends, [jax.ShapeDtypeStruct((4,) + s.shape[2:], s.dtype) for s in sends],
                [pltpu.SemaphoreType.DMA((4, n)), pltpu.SemaphoreType.DMA((4, n))], start, finish)


def pair_sum(send, got, out_dtype, tr, name):
    _, _, R, C = send.shape
    c = lax.axis_index("c").astype(jnp.int32).reshape(1)

    def body(c_ref, s_ref, g_ref, o_ref):
        o_ref[...] = (s_ref[...] + g_ref[...]).astype(out_dtype)

    return pl.pallas_call(
        body,
        grid_spec=pltpu.PrefetchScalarGridSpec(
            num_scalar_prefetch=1, grid=(4, R // tr),
            in_specs=[pl.BlockSpec((None, None, tr, C), lambda k, i, cr: (k, cr[0], i, 0)),
                      pl.BlockSpec((None, tr, C), lambda k, i, cr: (k, i, 0))],
            out_specs=pl.BlockSpec((None, tr, C), lambda k, i, cr: (k, i, 0))),
        out_shape=jax.ShapeDtypeStruct((4, R, C), out_dtype),
        compiler_params=_cp("parallel", "parallel"), name=name)(c, send, got)


def comm_chip(items, bufs):
    ns = len(items)
    segs = [(i, b, s0, nr, d0) for i, (_, b, ranges) in enumerate(items) for (s0, nr, d0) in ranges]

    def copies(ins, outs, sems):
        send_sems, recv_sems, local_sems = sems
        x, y, c, chips = _place()
        mychip = 2 * x + y
        mine = [pltpu.make_async_copy(ins[i].at[mychip, pl.ds(s0, nr)], outs[b].at[mychip, pl.ds(d0, nr)], local_sems.at[q])
                for q, (i, b, s0, nr, d0) in enumerate(segs)]
        remote = [pltpu.make_async_remote_copy(
            src_ref=ins[i].at[2 * px + py, pl.ds(s0, nr)], dst_ref=outs[b].at[mychip, pl.ds(d0, nr)],
            send_sem=send_sems.at[j, q], recv_sem=recv_sems.at[j, q], device_id=(px, py, c), device_id_type=MESH)
            for j, (px, py) in enumerate(chips) for q, (i, b, s0, nr, d0) in enumerate(segs)]
        return mine, remote

    def start(ins, outs, sems):
        mine, remote = copies(ins, outs, sems)
        for cp in mine + remote:
            cp.start()

    def finish(ins, outs, sems):
        mine, remote = copies(ins, outs, sems)
        for cp in remote + mine:
            cp.wait()

    q = len(segs)
    return Comm([it[0] for it in items] + list(bufs), [jax.ShapeDtypeStruct(b.shape, b.dtype) for b in bufs],
                [pltpu.SemaphoreType.DMA((3, q)), pltpu.SemaphoreType.DMA((3, q)), pltpu.SemaphoreType.DMA((q,))],
                start, finish, aliases={ns + b: b for b in range(len(bufs))})


def adamw(parts, row0, w, m, v, tr, name):
    C = w.shape[-1]
    rows = w.size // C
    off = row0 // tr

    def body(p_ref, w_ref, m_ref, v_ref, g_ref, d_ref, nm_ref, nv_ref):
        g = ((p_ref[0].astype(F32) + p_ref[1].astype(F32)) + p_ref[2].astype(F32)) + p_ref[3].astype(F32)
        m2 = ADAM_B1 * m_ref[...] + (1.0 - ADAM_B1) * g
        v2 = ADAM_B2 * v_ref[...] + (1.0 - ADAM_B2) * (g * g)
        mh = m2 / (1.0 - ADAM_B1 ** ADAM_STEP)
        vh = v2 / (1.0 - ADAM_B2 ** ADAM_STEP)
        g_ref[...] = g
        d_ref[...] = -ADAM_LR * (mh / (jnp.sqrt(vh) + ADAM_EPS) + ADAM_WD * w_ref[...])
        nm_ref[...] = m2
        nv_ref[...] = v2

    row = pl.BlockSpec((tr, C), lambda i: (i, 0))
    shp = jax.ShapeDtypeStruct((rows, C), F32)
    outs = pl.pallas_call(
        body, grid=(rows // tr,),
        in_specs=[pl.BlockSpec((4, tr, C), lambda i: (0, off + i, 0)), row, row, row],
        out_specs=[row, row, row, row], out_shape=[shp, shp, shp, shp],
        compiler_params=_cp("parallel"), name=name)(parts, w.reshape(rows, C), m.reshape(rows, C), v.reshape(rows, C))
    return [o.reshape(w.shape) for o in outs]


def _full_from_gathered(flat, shape, axis):
    t = jnp.moveaxis(flat.reshape((N_DEV,) + shape), 0, axis)
    return t.reshape(shape[:axis] + (N_DEV * shape[axis],) + shape[axis + 1:])


def _shards_of_full(full, shape, axis):
    t = full.reshape(shape[:axis] + (N_DEV, shape[axis]) + shape[axis + 1:])
    return jnp.moveaxis(t, axis, 0).reshape(N_DEV, -1)


def _small_pack(args, prefix):
    flat = jnp.concatenate([args[prefix + n].reshape(-1) for n, _, _ in SMALL] + [args[prefix + "attn_sinks"].reshape(-1)])
    return jnp.pad(flat, (0, SM_ROWS * 1024 - flat.shape[0])).reshape(SM_ROWS, 1024)


def _small_unpack(pack):
    flat = pack.reshape(-1)
    out, off = {}, 0
    for n, shape, _ in SMALL:
        size = math.prod(shape)
        out[n] = flat[off:off + size].reshape(shape)
        off += size
    out["attn_sinks"] = flat[SINK_OFF:SINK_OFF + 32].reshape(2, 16)
    return out


def _tn_tile(n):
    return next(t for t in (1408, 1024, 768, 512, 256, 128) if n % t == 0)


def _dw(a, b, name, planes=1, shard=None, comm=None):
    return mm_tn(a, b, name, _tn_tile(a.shape[1]), _tn_tile(b.shape[-1]), planes=planes, shard=shard, comm=comm)


class _NoExchange:
    def __init__(self, layers):
        self.layers, self.grads = layers, [{} for _ in range(DEPTH)]

    def weights(self, l, piece):
        return self.layers[l]

    def fwd_slot(self):
        return None

    def fwd_done(self, outs):
        pass

    def loss_ready(self, loss):
        pass

    def bwd_slot(self, slot):
        return None

    def bwd_done(self, slot, outs):
        pass

    def grads_ready(self, l, part, g):
        self.grads[l].update(g)


def _local_step(x, target, sched, sinks):
    S = x.shape[0]
    tq = min(512, S)
    cosf, sinf = _rope_tables(S)
    saved = []
    h, hb = x, x.astype(BF16)
    for l in range(DEPTH):
        j = l // 2
        wl = {}
        sv = {"h0b": hb, "wl": wl}
        wl.update(sched.weights(l, "a"))
        (sv["gu1"], a1), co = ffn_up(hb, wl["w_in1"], f"ffn1_up_{l}", comm=sched.fwd_slot())
        sched.fwd_done(co)
        sv["a1"] = a1
        wl.update(sched.weights(l, "b"))
        (sv["z1"], h, hb), _ = mm_ln(a1, _rows_view(wl["r_ff"], FF_SHARD, 0), h, wl["ln_g"][0], wl["ln_b"][0], 0.5,
                                     f"ffn1_down_ln_{l}")
        sv["h1b"] = hb
        if l % 2 == 0:
            qkv = mm_plain(hb, _mat_view(wl["m_c"]), F32, f"attn_qkv_{l}")
            qr, kr, vv = rope_fwd(qkv, cosf, sinf, f"rope_{l}")
            mix_in, co = attn_fwd(qr, kr, vv, sinks[j], f"attn_core_{l}", tq, comm=sched.fwd_slot())
            sv.update(qr=qr, kr=kr, vv=vv)
        else:
            xg = mm_plain(hb, _mat_view(wl["m_c"]), F32, f"lru_in_{l}")
            (mix_in, hstate), co = lru_fwd(xg, wl["lru"], f"lru_core_{l}", comm=sched.fwd_slot())
            sv.update(xg=xg, hstate=hstate)
        sched.fwd_done(co)
        sv["mix_in"] = mix_in
        (sv["z2"], h, hb), _ = mm_ln(mix_in, _rows_view(wl["m_o"], MIX_SHARD, 0), h, wl["ln_g"][1], wl["ln_b"][1], 1.0,
                                     f"mix_out_ln_{l}")
        sv["h2b"] = hb
        wl.update(sched.weights(l, "c"))
        (sv["gu2"], a2), co = ffn_up(hb, wl["w_in2"], f"ffn2_up_{l}", comm=sched.fwd_slot())
        sched.fwd_done(co)
        sv["a2"] = a2
        (sv["z3"], h, hb), _ = mm_ln(a2, _rows_view(wl["r_ff"], FF_SHARD, FF_SHARD), h, wl["ln_g"][2], wl["ln_b"][2], 0.5,
                                     f"ffn2_down_ln_{l}")
        saved.append(sv)

    dy, lvec = loss_head(h, target, "loss_head")
    loss = lvec[0, 0]
    sched.loss_ready(loss)

    def hosted(slot, on, fn):
        comm = sched.bwd_slot(slot) if on else None
        out, co = fn(comm)
        if comm is not None:
            sched.bwd_done(slot, co)
        return out

    def ffn_bwd(dy, z, g, gu, a, xin_b, r_ff, blk, w_in, tag, host):
        dz, dzb, dg, db = ln_bwd(dy, z, g, 0.5, f"ln_bwd_{tag}")
        dh = hosted("pair", host, lambda cm: ffn_mid_bwd(dzb, r_ff, blk, gu, f"ffn_mid_bwd_{tag}", comm=cm))
        d_wout, _ = _dw(a, dzb, f"dw_out_{tag}")
        d_win = hosted("chip_x", host, lambda cm: _dw(xin_b, dh, f"dw_in_{tag}", planes=2, shard=2 * D_FF // N_DEV, comm=cm))
        dx = hosted("chip_y", host, lambda cm: mm_res2_nt(dh, _mat_view(w_in, D_FF, 0), _mat_view(w_in, D_FF, 1), dz, ALPHA,
                                                          f"ffn_dx_{tag}", comm=cm))
        return dx, dg, db, d_wout, d_win

    for l in reversed(range(DEPTH)):
        j = l // 2
        sv = saved[l]
        wl = sv["wl"]
        gl = {}
        dg, db = [None] * 3, [None] * 3
        dy, dg[2], db[2], gl["w_out2"], gl["w_in2"] = ffn_bwd(
            dy, sv["z3"], wl["ln_g"][2], sv["gu2"], sv["a2"], sv["h2b"], wl["r_ff"], 1, wl["w_in2"], f"2_{l}", True)
        dz, dzb, dg[1], db[1] = ln_bwd(dy, sv["z2"], wl["ln_g"][1], 1.0, f"ln_bwd_mix_{l}")
        w_mix = _rows_view(wl["m_o"], MIX_SHARD, 0)
        if l % 2 == 0:
            gl["mix_out"], _ = _dw(sv["mix_in"], dzb, f"dw_o_{l}")
            do = mm_plain(dzb, w_mix, BF16, f"attn_do_{l}", nt=True)
            dq, dkc, dkp, dvc, dvp, dsk = attn_bwd(sv["qr"], sv["kr"], sv["vv"], do, sinks[j], f"attn_core_bwd_{l}", tq)
            gl["sinks"] = dsk[:, :GROUP, 0].reshape(N_HEADS)
            dmid = rope_bwd(dq, dkc, dkp, dvc, dvp, cosf, sinf, f"rope_bwd_{l}", tq)
            gl["mix_in"], _ = _dw(sv["h1b"], dmid, f"dw_qkv_{l}", shard=QKV // N_DEV)
        else:
            gl["mix_out"], _ = _dw(sv["mix_in"], dzb, f"dw_lru_out_{l}")
            dyl = mm_plain(dzb, w_mix, F32, f"lru_dy_{l}", nt=True)
            dmid, dcw, dcb, dwra, dwrx, dbra, dbrx, dlam = lru_bwd(dyl, sv["xg"], sv["hstate"], wl["lru"], f"lru_core_bwd_{l}")
            gl.update(conv_w=dcw, conv_b=dcb[0], w_ra=dwra, w_rx=dwrx, b_ra=dbra[0], b_rx=dbrx[0],
                      lam=dlam[0] * wl["lru"]["sig_neg"])
            gl["mix_in"], _ = _dw(sv["h1b"], dmid, f"dw_lru_in_{l}", shard=2 * D_RNN // N_DEV)
        dy = mm_res_nt(dmid, _mat_view(wl["m_c"]), dz, ALPHA, f"mix_dx_{l}")
        sched.grads_ready(l, "hi", gl)
        lo = {}
        dy, dg[0], db[0], lo["w_out1"], lo["w_in1"] = ffn_bwd(
            dy, sv["z1"], wl["ln_g"][0], sv["gu1"], sv["a1"], sv["h0b"], wl["r_ff"], 0, wl["w_in1"], f"1_{l}", True)
        lo["ln_g"], lo["ln_b"] = jnp.concatenate(dg, axis=0), jnp.concatenate(db, axis=0)
        sched.grads_ready(l, "lo", lo)
    return loss, dy


def _lru_params(full, j, w_ra, w_rx):
    lam = full["lru_lambda"][j]
    return {
        "conv_w": full["lru_conv_w"][j], "conv_b": full["lru_conv_b"][j].reshape(1, -1),
        "w_ra": w_ra, "w_rx": w_rx,
        "b_ra": full["lru_b_ra"][j].reshape(1, -1), "b_rx": full["lru_b_rx"][j].reshape(1, -1),
        "lsl": jax.nn.log_sigmoid(lam).reshape(1, -1), "sig_neg": jax.nn.sigmoid(-lam),
    }


def _row_shards(f):
    return f.reshape(N_DEV, -1, f.shape[1])


def _gate_shards(w):
    return w.reshape(RNN_BLOCKS, N_DEV, RNN_W // N_DEV, RNN_W).transpose(1, 0, 2, 3).reshape(N_DEV, -1, RNN_W)


class _Fsdp:
    def __init__(self, args):
        self.args = args
        self.b16 = lambda a: a.astype(BF16)
        self.sm = _small_pack(args, "")
        self.sent = {l: self._ag_arrays(l) for l in range(DEPTH)}
        self.queue = [(l, p) for l in range(DEPTH) for p in "abc"][1:]
        self.hosting = None
        got = run_comm(comm_ag(self.sent[0]["a"] + [self.sm]), "all_gather_first")
        self.raw = {0: {"a": got[:-1]}}
        gflat = got[-1].reshape(N_DEV, SM_ROWS * 1024)
        self.small, off = {}, 0
        for n, shape, axis in SMALL:
            size = math.prod(shape)
            self.small[n] = _full_from_gathered(gflat[:, off:off + size], shape, axis)
            off += size
        self.bufs = {"r": jnp.zeros((4, R_ROWS, D_MODEL), BF16), "c704": jnp.zeros((4, 2 * DEPTH * D_MODEL, 704), BF16),
                     "c192": jnp.zeros((4, 2 * D_MODEL, 192), BF16), "c256": jnp.zeros((4, C256_ROWS, RNN_W), BF16),
                     "sm": jnp.zeros((4, SM_ROWS, 1024), F32)}
        self.pending = None
        self.small_g = [{} for _ in range(DEPTH)]
        self.loss = None

    def _ag_arrays(self, l):
        a, b16, j = self.args, self.b16, l // 2
        r_ff = jnp.concatenate([b16(a["ffn1_w_out"][l]), b16(a["ffn2_w_out"][l])], axis=0)
        if l % 2 == 0:
            return {"a": [b16(a["ffn1_w_in"][l]), b16(a["attn_w_qkv"][j])], "b": [r_ff, b16(a["attn_w_o"][j])],
                    "c": [b16(a["ffn2_w_in"][l])]}
        gates = jnp.concatenate([b16(a["lru_w_ra"][j]).reshape(-1, RNN_W), b16(a["lru_w_rx"][j]).reshape(-1, RNN_W)], axis=0)
        return {"a": [b16(a["ffn1_w_in"][l]), b16(a["lru_w_in"][j])], "b": [r_ff, b16(a["lru_w_out"][j]), gates],
                "c": [b16(a["ffn2_w_in"][l])]}

    def weights(self, l, piece):
        raw, j = self.raw[l][piece], l // 2
        full = lambda g: g.transpose(1, 0, 2).reshape(g.shape[1], -1)
        if piece == "a":
            return {"w_in1": full(raw[0]), "m_c": full(raw[1]), "ln_g": self.small["ln_g"][l], "ln_b": self.small["ln_b"][l]}
        if piece == "c":
            return {"w_in2": full(raw[0])}
        wl = {"r_ff": raw[0], "m_o": raw[1]}
        if l % 2:
            g = raw[2].reshape(N_DEV, 2, RNN_BLOCKS, RNN_W // N_DEV, RNN_W).transpose(1, 2, 0, 3, 4)
            g = g.reshape(2, RNN_BLOCKS, RNN_W, RNN_W)
            wl["lru"] = _lru_params(self.small, j, g[0], g[1])
        return wl

    def fwd_slot(self):
        if not self.queue:
            return None
        self.hosting = self.queue.pop(0)
        l, piece = self.hosting
        return comm_ag(self.sent[l][piece])

    def fwd_done(self, outs):
        if self.hosting is not None:
            l, piece = self.hosting
            self.raw.setdefault(l, {})[piece] = outs
            self.hosting = None

    def loss_ready(self, loss):
        self.loss = loss

    def _ranges(self, l, part):
        j = l // 2
        if part == "lo":
            return [("r", [(0, FF_SHARD, l * FF_SHARD)]), ("c704", [(0, D_MODEL, l * D_MODEL)])]
        r = [(0, FF_SHARD, R_FFN2 + l * FF_SHARD), (FF_SHARD, MIX_SHARD, (R_LRU if l % 2 else R_ATTN) + j * MIX_SHARD)]
        c704 = [(0, D_MODEL, (DEPTH + l) * D_MODEL)]
        if l % 2:
            gr = RNN_BLOCKS * (RNN_W // N_DEV)
            mix = ("c256", [(0, D_MODEL, j * D_MODEL), (D_MODEL, gr, C256_RA + j * gr), (D_MODEL + gr, gr, C256_RX + j * gr)])
        else:
            mix = ("c192", [(0, D_MODEL, j * D_MODEL)])
        return [("r", r), ("c704", c704), mix]

    def grads_ready(self, l, part, g):
        self.small_g[l].update(g)
        if part == "lo":
            ts = [_row_shards(g["w_out1"]), g["w_in1"]]
        else:
            tmix = g["mix_in"]
            if l % 2:
                tmix = jnp.concatenate([tmix, _gate_shards(g["w_ra"]), _gate_shards(g["w_rx"])], axis=1)
            ts = [jnp.concatenate([_row_shards(g["w_out2"]), _row_shards(g["mix_out"])], axis=1), g["w_in2"], tmix]
        sends = [t.reshape((4, 2) + t.shape[1:]) for t in ts]
        if (l, part) == (0, "lo"):
            self._last(sends)
        else:
            self.pending = {"tag": f"{l}{part}", "sends": sends, "where": self._ranges(l, part)}

    def _pair_sums(self, tag, sends, gots):
        tiles = (None, 512, 256)
        return [pair_sum(s, g, BF16, t or s.shape[2] // 2, f"pair_sum_{nm}_{tag}")
                for s, g, t, nm in zip(sends, gots, tiles, ("r", "c704", "mix"))]

    def _chip(self, parts, where, pick):
        names = [where[i][0] for i in pick]
        return names, comm_chip([(parts[i], k, where[i][1]) for k, i in enumerate(pick)], [self.bufs[n] for n in names])

    def bwd_slot(self, slot):
        p = self.pending
        if p is None:
            return None
        if slot == "pair":
            return comm_pair(p["sends"])
        pick = [1] if slot == "chip_x" else [i for i in range(len(p["sends"])) if i != 1]
        p["names"], comm = self._chip(p["parts"], p["where"], pick)
        return comm

    def bwd_done(self, slot, outs):
        p = self.pending
        if slot == "pair":
            p["parts"] = self._pair_sums(p["tag"], p["sends"], outs)
            return
        for n, o in zip(p["names"], outs):
            self.bufs[n] = o
        if slot == "chip_y":
            self.pending = None

    def _last(self, sends):
        sg = self.small_g
        stack = lambda key, ls: jnp.stack([sg[l][key] for l in ls])
        every, lru = range(DEPTH), (1, 3)
        small = {"ln_g": stack("ln_g", every), "ln_b": stack("ln_b", every), "lru_conv_w": stack("conv_w", lru),
                 "lru_conv_b": stack("conv_b", lru), "lru_b_ra": stack("b_ra", lru), "lru_b_rx": stack("b_rx", lru),
                 "lru_lambda": stack("lam", lru)}
        tail = jnp.concatenate([stack("sinks", (0, 2)).reshape(-1), self.loss.reshape(1)])
        tail = jnp.pad(tail, (0, SM_ROWS * 1024 - SMALL_N - tail.shape[0]))
        s_sm = jnp.concatenate([_shards_of_full(small[n], shape, axis) for n, shape, axis in SMALL]
                               + [jnp.broadcast_to(tail, (N_DEV, tail.shape[0]))], axis=1).reshape(4, 2, SM_ROWS, 1024)
        gots = run_comm(comm_pair(sends + [s_sm]), "pair_exchange_last")
        parts = self._pair_sums("0lo", sends, gots[:2]) + [pair_sum(s_sm, gots[2], F32, SM_ROWS, "pair_sum_sm")]
        names, comm = self._chip(parts, self._ranges(0, "lo") + [("sm", [(0, SM_ROWS, 0)])], [0, 1, 2])
        for n, o in zip(names, run_comm(comm, "chip_exchange_last")):
            self.bufs[n] = o


def kernel(x, ffn1_w_in, ffn1_w_out, ffn2_w_in, ffn2_w_out, ln_g, ln_b, attn_w_qkv, attn_sinks, attn_w_o, lru_w_in, lru_conv_w, lru_conv_b, lru_w_ra, lru_b_ra, lru_w_rx, lru_b_rx, lru_lambda, lru_w_out, loss_target, m_ffn1_w_in, m_ffn1_w_out, m_ffn2_w_in, m_ffn2_w_out, m_ln_g, m_ln_b, m_attn_w_qkv, m_attn_sinks, m_attn_w_o, m_lru_w_in, m_lru_conv_w, m_lru_conv_b, m_lru_w_ra, m_lru_b_ra, m_lru_w_rx, m_lru_b_rx, m_lru_lambda, m_lru_w_out, v_ffn1_w_in, v_ffn1_w_out, v_ffn2_w_in, v_ffn2_w_out, v_ln_g, v_ln_b, v_attn_w_qkv, v_attn_sinks, v_attn_w_o, v_lru_w_in, v_lru_conv_w, v_lru_conv_b, v_lru_w_ra, v_lru_b_ra, v_lru_w_rx, v_lru_b_rx, v_lru_lambda, v_lru_w_out):
    args = dict(locals())

    sched = _Fsdp(args)
    _, dx = _local_step(x[0], loss_target[0], sched, attn_sinks)
    p_r, p704, p192, p256, p_sm = (sched.bufs[n] for n in ("r", "c704", "c192", "c256", "sm"))
    sm = sched.sm

    fam = lambda n: (args[n], args["m_" + n], args["v_" + n])
    res = {
        "ffn1_w_out": adamw(p_r, 0, *fam("ffn1_w_out"), FF_SHARD, "adamw_ffn1_w_out"),
        "ffn2_w_out": adamw(p_r, R_FFN2, *fam("ffn2_w_out"), FF_SHARD, "adamw_ffn2_w_out"),
        "attn_w_o": adamw(p_r, R_ATTN, *fam("attn_w_o"), 2 * MIX_SHARD, "adamw_attn_w_o"),
        "lru_w_out": adamw(p_r, R_LRU, *fam("lru_w_out"), 2 * MIX_SHARD, "adamw_lru_w_out"),
        "ffn1_w_in": adamw(p704, 0, *fam("ffn1_w_in"), 512, "adamw_ffn1_w_in"),
        "ffn2_w_in": adamw(p704, DEPTH * D_MODEL, *fam("ffn2_w_in"), 512, "adamw_ffn2_w_in"),
        "attn_w_qkv": adamw(p192, 0, *fam("attn_w_qkv"), 512, "adamw_attn_w_qkv"),
        "lru_w_in": adamw(p256, 0, *fam("lru_w_in"), 512, "adamw_lru_w_in"),
        "lru_w_ra": adamw(p256, C256_RA, *fam("lru_w_ra"), 256, "adamw_lru_w_ra"),
        "lru_w_rx": adamw(p256, C256_RX, *fam("lru_w_rx"), 256, "adamw_lru_w_rx"),
    }
    sm_out = adamw(p_sm, 0, sm, _small_pack(args, "m_"), _small_pack(args, "v_"), SM_ROWS, "adamw_small")
    for k, pack in enumerate(sm_out):
        for n, val in _small_unpack(pack).items():
            res.setdefault(n, [None] * 4)[k] = val
    loss_total = sm_out[0].reshape(-1)[LOSS_OFF]
    out = [loss_total, dx[None]]
    for k in range(4):
        out += [res[n][k] for n in WEIGHTS]
    return tuple(out)
```

```python
import math

import jax
import jax.numpy as jnp
import numpy as np
from jax import lax
from jax.experimental import pallas as pl
from jax.experimental.pallas import tpu as pltpu

F32 = jnp.float32
BF16 = jnp.bfloat16

D_MODEL = 1024
DEPTH = 4
N_HEADS = 16
N_KV = 4
HEAD_DIM = 64
GROUP = 4
BLOCK = 128
ROPE_THETA = 10000.0
D_RNN = 1024
RNN_BLOCKS = 4
RNN_W = 256
CONV_W = 4
LRU_C = 8.0
D_FF = 2816
ALPHA = (2.0 * DEPTH) ** 0.25
LN_EPS = 1e-5
QKV = (N_HEADS + 2 * N_KV) * HEAD_DIM
N_DEV = 8

ADAM_LR = 0.001
ADAM_B1 = 0.9
ADAM_B2 = 0.999
ADAM_EPS = 1e-08
ADAM_WD = 0.01
ADAM_STEP = 10

VMEM_LIMIT = 52 * 1024 * 1024
NEG = float(np.finfo(np.float32).min)

MESH = pl.DeviceIdType.MESH
ANY = pl.BlockSpec(memory_space=pl.ANY)

FF_SHARD = D_FF // N_DEV
MIX_SHARD = D_MODEL // N_DEV
R_FFN2 = DEPTH * FF_SHARD
R_ATTN = 2 * DEPTH * FF_SHARD
R_LRU = R_ATTN + 2 * MIX_SHARD
R_ROWS = R_LRU + 2 * MIX_SHARD
C256_RA = 2 * D_MODEL
C256_RX = C256_RA + 2 * RNN_BLOCKS * (RNN_W // N_DEV)
C256_ROWS = C256_RX + 2 * RNN_BLOCKS * (RNN_W // N_DEV)

SMALL = (
    ("ln_g", (4, 3, 128), 2),
    ("ln_b", (4, 3, 128), 2),
    ("lru_conv_w", (2, 4, 128), 2),
    ("lru_conv_b", (2, 128), 1),
    ("lru_b_ra", (2, 128), 1),
    ("lru_b_rx", (2, 128), 1),
    ("lru_lambda", (2, 128), 1),
)
WEIGHTS = ("ffn1_w_in", "ffn1_w_out", "ffn2_w_in", "ffn2_w_out", "ln_g", "ln_b", "attn_w_qkv", "attn_sinks",
           "attn_w_o", "lru_w_in", "lru_conv_w", "lru_conv_b", "lru_w_ra", "lru_b_ra", "lru_w_rx", "lru_b_rx",
           "lru_lambda", "lru_w_out")
SMALL_N = sum(math.prod(s) for _, s, _ in SMALL)
SINK_OFF = SMALL_N
LOSS_OFF = SMALL_N + 32
SM_ROWS = 8


def _cp(*sem):
    return pltpu.CompilerParams(dimension_semantics=sem, vmem_limit_bytes=VMEM_LIMIT)


def _dot(a, b):
    return jnp.dot(a, b, preferred_element_type=F32)


def _dot_tn(a, b):
    return lax.dot_general(a, b, (((0,), (0,)), ((), ())), preferred_element_type=F32)


def _dot_nt(a, b):
    return lax.dot_general(a, b, (((1,), (1,)), ((), ())), preferred_element_type=F32)


def _col_chunks(cols, size=512):
    return [slice(c, min(c + size, cols)) for c in range(0, cols, size)]


def _ln(z, g, b):
    mu = jnp.mean(z, axis=-1, keepdims=True)
    xc = z - mu
    var = jnp.mean(xc * xc, axis=-1, keepdims=True)
    return xc * lax.rsqrt(var + LN_EPS) * g + b


def _rows_view(pack, rows_per_dev, row0):
    return (pack, (N_DEV, rows_per_dev, D_MODEL), (0, row0 // rows_per_dev, 0), (N_DEV * rows_per_dev, D_MODEL))


def _mat_view(arr, cols=None, cblk=0):
    k, n = arr.shape
    cols = n if cols is None else cols
    return (arr, (k, cols), (0, cblk), (k, cols))


def _vspec(view):
    _, bshape, bidx, _ = view
    return pl.BlockSpec(bshape, lambda *_: bidx)


def _vload(view, ref):
    return ref[...].reshape(view[3])


class Comm:
    def __init__(self, ins, out_shapes, sems, start, finish, aliases=None, relay=None):
        self.ins, self.out_shapes, self.sems = list(ins), list(out_shapes), list(sems)
        self.start, self.finish, self.aliases = start, finish, dict(aliases or {})
        self.relay = relay


def _call(body, *, grid, in_specs, out_specs, out_shape, operands, name, sem, scratch=(), comm=None):
    n_in, n_out, n_scr = len(in_specs), len(out_specs), len(scratch)
    if comm is None:
        return pl.pallas_call(body, grid=grid, in_specs=list(in_specs), out_specs=list(out_specs),
                              out_shape=list(out_shape), scratch_shapes=list(scratch), compiler_params=_cp(*sem),
                              name=name)(*operands), []
    nci, nco = len(comm.ins), len(comm.out_shapes)

    def hosted(*refs):
        ins, cins = refs[:n_in], refs[n_in:n_in + nci]
        o0 = n_in + nci
        outs, couts = refs[o0:o0 + n_out], refs[o0 + n_out:o0 + n_out + nco]
        s0 = o0 + n_out + nco
        scr, csems = refs[s0:s0 + n_scr], refs[s0 + n_scr:]
        step = 0
        for ax, size in enumerate(grid):
            step = step * size + pl.program_id(ax)
        steps = math.prod(grid)

        @pl.when(step == 0)
        def _():
            comm.start(cins, couts, csems)

        if comm.relay is not None and steps >= 4:
            @pl.when(step == (3 * steps) // 4)
            def _():
                comm.relay(cins, couts, csems)

        body(*ins, *outs, *scr)

        @pl.when(step == steps - 1)
        def _():
            if comm.relay is not None and steps < 4:
                comm.relay(cins, couts, csems)
            comm.finish(cins, couts, csems)

    res = pl.pallas_call(
        hosted, grid=grid, in_specs=list(in_specs) + [ANY] * nci, out_specs=list(out_specs) + [ANY] * nco,
        out_shape=list(out_shape) + comm.out_shapes, scratch_shapes=list(scratch) + comm.sems,
        input_output_aliases={n_in + i: n_out + o for i, o in comm.aliases.items()},
        compiler_params=_cp(*(("arbitrary",) * len(grid))), name=name)(*operands, *comm.ins)
    return res[:n_out], res[n_out:]


def run_comm(comm, name):
    nci, nco = len(comm.ins), len(comm.out_shapes)

    def body(*refs):
        cins, couts, csems = refs[:nci], refs[nci:nci + nco], refs[nci + nco:]
        comm.start(cins, couts, csems)
        if comm.relay is not None:
            comm.relay(cins, couts, csems)
        comm.finish(cins, couts, csems)

    return pl.pallas_call(
        body, in_specs=[ANY] * nci, out_specs=[ANY] * nco, out_shape=comm.out_shapes, scratch_shapes=comm.sems,
        input_output_aliases=dict(comm.aliases), name=name)(*comm.ins)


def mm_plain(a, wv, out_dtype, name, nt=False, tm=512):
    S, K = a.shape
    N = wv[3][0] if nt else wv[3][1]
    tm = min(tm, S)
    dot = _dot_nt if nt else _dot

    def body(a_ref, w_ref, o_ref):
        o_ref[...] = dot(a_ref[...], _vload(wv, w_ref)).astype(out_dtype)

    return pl.pallas_call(
        body, grid=(S // tm,),
        in_specs=[pl.BlockSpec((tm, K), lambda i: (i, 0)), _vspec(wv)],
        out_specs=pl.BlockSpec((tm, N), lambda i: (i, 0)),
        out_shape=jax.ShapeDtypeStruct((S, N), out_dtype),
        compiler_params=_cp("parallel"), name=name)(a, wv[0])


def mm_res_nt(a, wv, r, alpha, name, tm=512):
    S, K = a.shape
    N = wv[3][0]
    tm = min(tm, S)

    def body(a_ref, w_ref, r_ref, o_ref):
        o_ref[...] = _dot_nt(a_ref[...], _vload(wv, w_ref)) + alpha * r_ref[...]

    return pl.pallas_call(
        body, grid=(S // tm,),
        in_specs=[pl.BlockSpec((tm, K), lambda i: (i, 0)), _vspec(wv), pl.BlockSpec((tm, N), lambda i: (i, 0))],
        out_specs=pl.BlockSpec((tm, N), lambda i: (i, 0)),
        out_shape=jax.ShapeDtypeStruct((S, N), F32),
        compiler_params=_cp("parallel"), name=name)(a, wv[0], r)


def mm_res2_nt(a3, wv0, wv1, r, alpha, name, tm=512, comm=None):
    _, S, K = a3.shape
    N = wv0[3][0]
    tm = min(tm, S)

    def body(a_ref, w0_ref, w1_ref, r_ref, o_ref):
        o_ref[...] = (_dot_nt(a_ref[0], _vload(wv0, w0_ref)) + _dot_nt(a_ref[1], _vload(wv1, w1_ref))
                      + alpha * r_ref[...])

    (out,), couts = _call(
        body, grid=(S // tm,),
        in_specs=[pl.BlockSpec((2, tm, K), lambda i: (0, i, 0)), _vspec(wv0), _vspec(wv1),
                  pl.BlockSpec((tm, N), lambda i: (i, 0))],
        out_specs=[pl.BlockSpec((tm, N), lambda i: (i, 0))],
        out_shape=[jax.ShapeDtypeStruct((S, N), F32)],
        operands=(a3, wv0[0], wv1[0], r), name=name, sem=("parallel",), comm=comm)
    return out, couts


def mm_ln(a, wv, h, g, b, scale, name, tm=512, comm=None):
    S, K = a.shape
    tm = min(tm, S)

    def body(a_ref, w_ref, h_ref, g_ref, b_ref, z_ref, y_ref, yb_ref):
        z = ALPHA * h_ref[...] + scale * _dot(a_ref[...], _vload(wv, w_ref))
        y = _ln(z, g_ref[...], b_ref[...])
        z_ref[...] = z
        y_ref[...] = y
        yb_ref[...] = y.astype(BF16)

    row = pl.BlockSpec((tm, D_MODEL), lambda i: (i, 0))
    vec = pl.BlockSpec((1, D_MODEL), lambda i: (0, 0))
    return _call(
        body, grid=(S // tm,),
        in_specs=[pl.BlockSpec((tm, K), lambda i: (i, 0)), _vspec(wv), row, vec, vec],
        out_specs=[row, row, row],
        out_shape=[jax.ShapeDtypeStruct((S, D_MODEL), F32), jax.ShapeDtypeStruct((S, D_MODEL), F32),
                   jax.ShapeDtypeStruct((S, D_MODEL), BF16)],
        operands=(a, wv[0], h, g.reshape(1, -1), b.reshape(1, -1)), name=name, sem=("parallel",), comm=comm)


def ffn_up(xb, w_in, name, tm=512, tn=1408, comm=None):
    S = xb.shape[0]
    tm = min(tm, S)
    nj = D_FF // tn

    def body(x_ref, wg_ref, wu_ref, jac_ref, a_ref):
        x = x_ref[...]
        g = _dot(x, wg_ref[...])
        u = _dot(x, wu_ref[...])
        sg = jax.nn.sigmoid(g)
        t = g * sg
        jac_ref[0] = (u * (sg * ((g - t) + 1.0))).astype(BF16)
        jac_ref[1] = t.astype(BF16)
        a_ref[...] = (t * u).astype(BF16)

    return _call(
        body, grid=(nj, S // tm),
        in_specs=[pl.BlockSpec((tm, D_MODEL), lambda j, i: (i, 0)),
                  pl.BlockSpec((D_MODEL, tn), lambda j, i: (0, j)),
                  pl.BlockSpec((D_MODEL, tn), lambda j, i: (0, nj + j))],
        out_specs=[pl.BlockSpec((2, tm, tn), lambda j, i: (0, i, j)), pl.BlockSpec((tm, tn), lambda j, i: (i, j))],
        out_shape=[jax.ShapeDtypeStruct((2, S, D_FF), BF16), jax.ShapeDtypeStruct((S, D_FF), BF16)],
        operands=(xb, w_in, w_in), name=name, sem=("parallel", "parallel"), comm=comm)


def ffn_mid_bwd(dfb, r_ff, blk, gu, name, tm=512, comm=None):
    S = dfb.shape[0]
    tm = min(tm, S)
    tn = 4 * FF_SHARD

    def body(df_ref, w_ref, gu_ref, dh_ref):
        w = w_ref[...].reshape(tn, D_MODEL)
        df = df_ref[...]
        for cols in _col_chunks(tn):
            da = _dot_nt(df, w[cols, :])
            dh_ref[0, :, cols] = (da * gu_ref[0, :, cols].astype(F32)).astype(BF16)
            dh_ref[1, :, cols] = (da * gu_ref[1, :, cols].astype(F32)).astype(BF16)

    gspec = pl.BlockSpec((2, tm, tn), lambda j, i: (0, i, j))
    (out,), couts = _call(
        body, grid=(2, S // tm),
        in_specs=[pl.BlockSpec((tm, D_MODEL), lambda j, i: (i, 0)),
                  pl.BlockSpec((4, FF_SHARD, D_MODEL), lambda j, i: (j, blk, 0)), gspec],
        out_specs=[gspec],
        out_shape=[jax.ShapeDtypeStruct((2, S, D_FF), BF16)],
        operands=(dfb, r_ff, gu), name=name, sem=("parallel", "parallel"), comm=comm)
    return out, couts


def mm_tn(a, b, name, tm, tn, ts=2048, planes=1, shard=None, comm=None):
    S, M = a.shape
    N = b.shape[-1] * planes
    ts = min(ts, S)
    per = b.shape[-1] // tn
    ns = S // ts

    if shard is None:
        def body(a_ref, b_ref, o_ref):
            @pl.when(pl.program_id(2) == 0)
            def _():
                o_ref[...] = jnp.zeros_like(o_ref)

            o_ref[...] += _dot_tn(a_ref[...], b_ref[...])

        out_spec = pl.BlockSpec((tm, tn), lambda i, j, s: (i, j))
        out_shape = jax.ShapeDtypeStruct((M, N), F32)
        scratch = ()
    else:
        def body(a_ref, b_ref, o_ref, acc_ref):
            s = pl.program_id(2)

            @pl.when(s == 0)
            def _():
                acc_ref[...] = jnp.zeros_like(acc_ref)

            acc_ref[...] += _dot_tn(a_ref[...], b_ref[...])

            @pl.when(s == ns - 1)
            def _():
                for q in range(tn // shard):
                    o_ref[q] = acc_ref[:, shard * q:shard * (q + 1)]

        out_spec = pl.BlockSpec((tn // shard, tm, shard), lambda i, j, s: (j, i, 0))
        out_shape = jax.ShapeDtypeStruct((N // shard, M, shard), F32)
        scratch = (pltpu.VMEM((tm, tn), F32),)

    if planes == 1:
        bspec = pl.BlockSpec((ts, tn), lambda i, j, s: (s, j))
    else:
        bspec = pl.BlockSpec((None, ts, tn), lambda i, j, s: (j // per, s, j % per))
    (out,), couts = _call(
        body, grid=(M // tm, N // tn, ns),
        in_specs=[pl.BlockSpec((ts, tm), lambda i, j, s: (s, i)), bspec],
        out_specs=[out_spec], out_shape=[out_shape], scratch=scratch,
        operands=(a, b), name=name, sem=("parallel", "parallel", "arbitrary"), comm=comm)
    return out, couts


def ln_bwd(dy, z, g, out_scale, name, tm=512):
    S = dy.shape[0]
    tm = min(tm, S)

    def body(dy_ref, z_ref, g_ref, dz_ref, dzb_ref, dg_ref, db_ref):
        @pl.when(pl.program_id(0) == 0)
        def _():
            dg_ref[...] = jnp.zeros_like(dg_ref)
            db_ref[...] = jnp.zeros_like(db_ref)

        z = z_ref[...]
        dy_ = dy_ref[...]
        mu = jnp.mean(z, axis=-1, keepdims=True)
        xc = z - mu
        var = jnp.mean(xc * xc, axis=-1, keepdims=True)
        rstd = lax.rsqrt(var + LN_EPS)
        xh = xc * rstd
        dxh = dy_ * g_ref[...]
        m1 = jnp.mean(dxh, axis=-1, keepdims=True)
        m2 = jnp.mean(dxh * xh, axis=-1, keepdims=True)
        dz = rstd * (dxh - m1 - xh * m2)
        dz_ref[...] = dz
        dzb_ref[...] = (out_scale * dz).astype(BF16)
        dg_ref[...] += jnp.sum(dy_ * xh, axis=0, keepdims=True)
        db_ref[...] += jnp.sum(dy_, axis=0, keepdims=True)

    row = pl.BlockSpec((tm, D_MODEL), lambda i: (i, 0))
    vec = pl.BlockSpec((1, D_MODEL), lambda i: (0, 0))
    return pl.pallas_call(
        body, grid=(S // tm,),
        in_specs=[row, row, vec],
        out_specs=[row, row, vec, vec],
        out_shape=[jax.ShapeDtypeStruct((S, D_MODEL), F32), jax.ShapeDtypeStruct((S, D_MODEL), BF16),
                   jax.ShapeDtypeStruct((1, D_MODEL), F32), jax.ShapeDtypeStruct((1, D_MODEL), F32)],
        compiler_params=_cp("arbitrary"), name=name)(dy, z, g.reshape(1, -1))


def loss_head(y, t, name, tm=512):
    S = y.shape[0]
    tm = min(tm, S)
    nt = S // tm

    def body(y_ref, t_ref, dy_ref, l_ref):
        i = pl.program_id(0)

        @pl.when(i == 0)
        def _():
            l_ref[...] = jnp.zeros_like(l_ref)

        e = y_ref[...] - t_ref[...]
        dy_ref[...] = e * (1.0 / D_MODEL)
        l_ref[...] += jnp.sum(e * e, axis=0, keepdims=True)

        @pl.when(i == nt - 1)
        def _():
            tot = jnp.sum(l_ref[...], axis=1, keepdims=True) * (0.5 / D_MODEL)
            l_ref[...] = jnp.broadcast_to(tot, l_ref.shape)

    row = pl.BlockSpec((tm, D_MODEL), lambda i: (i, 0))
    vec = pl.BlockSpec((1, D_MODEL), lambda i: (0, 0))
    return pl.pallas_call(
        body, grid=(nt,), in_specs=[row, row], out_specs=[row, vec],
        out_shape=[jax.ShapeDtypeStruct((S, D_MODEL), F32), jax.ShapeDtypeStruct((1, D_MODEL), F32)],
        compiler_params=_cp("arbitrary"), name=name)(y, t)


def _rope_tables(S):
    pos = jnp.arange(S, dtype=F32)
    inv_freq = ROPE_THETA ** (-jnp.arange(0, HEAD_DIM, 2, dtype=F32) / HEAD_DIM)
    ang = pos[:, None] * inv_freq[None, :]
    cos, sin = jnp.cos(ang), jnp.sin(ang)
    cosf = jnp.concatenate([cos, cos, cos, cos], axis=1)
    sinf = jnp.concatenate([-sin, sin, -sin, sin], axis=1)
    return cosf, sinf


def _rot(t, c, s, first):
    sw = jnp.where(first, pltpu.roll(t, 96, 1), pltpu.roll(t, 32, 1))
    return t * c + sw * s


def rope_fwd(qkv, cosf, sinf, name, tm=512):
    S = qkv.shape[0]
    tm = min(tm, S)

    def body(x_ref, c_ref, s_ref, q_ref, k_ref, v_ref):
        c = c_ref[...]
        s = s_ref[...]
        first = (lax.broadcasted_iota(jnp.int32, (tm, 128), 1) % HEAD_DIM) < (HEAD_DIM // 2)
        for j in range(8):
            q_ref[:, 128 * j:128 * (j + 1)] = _rot(x_ref[:, 128 * j:128 * (j + 1)], c, s, first).astype(BF16)
        for j in range(2):
            k_ref[:, 128 * j:128 * (j + 1)] = _rot(x_ref[:, 1024 + 128 * j:1024 + 128 * (j + 1)], c, s, first).astype(BF16)
        v_ref[...] = x_ref[:, 1280:1536].astype(BF16)

    tab = pl.BlockSpec((tm, 128), lambda i: (i, 0))
    return pl.pallas_call(
        body, grid=(S // tm,),
        in_specs=[pl.BlockSpec((tm, QKV), lambda i: (i, 0)), tab, tab],
        out_specs=[pl.BlockSpec((tm, 1024), lambda i: (i, 0)), pl.BlockSpec((tm, 256), lambda i: (i, 0)),
                   pl.BlockSpec((tm, 256), lambda i: (i, 0))],
        out_shape=[jax.ShapeDtypeStruct((S, 1024), BF16), jax.ShapeDtypeStruct((S, 256), BF16),
                   jax.ShapeDtypeStruct((S, 256), BF16)],
        compiler_params=_cp("parallel"), name=name)(qkv, cosf, sinf)


def rope_bwd(dq, dkc, dkp, dvc, dvp, cosf, sinf, name, tq):
    S = dq.shape[0]
    nt = S // tq

    def body(dq_ref, dkc_ref, dkp_ref, dvc_ref, dvp_ref, c_ref, s_ref, o_ref):
        i = pl.program_id(0)
        c = c_ref[...]
        s = -s_ref[...]
        first = (lax.broadcasted_iota(jnp.int32, (tq, 128), 1) % HEAD_DIM) < (HEAD_DIM // 2)
        for j in range(8):
            o_ref[:, 128 * j:128 * (j + 1)] = _rot(dq_ref[:, 128 * j:128 * (j + 1)], c, s, first).astype(BF16)
        has_next = i < nt - 1
        rows = lax.broadcasted_iota(jnp.int32, (tq, 256), 0)
        pad = jnp.zeros((tq - BLOCK, 256), F32)
        halo_k = jnp.concatenate([pad, dkp_ref[...]], axis=0)
        halo_v = jnp.concatenate([pad, dvp_ref[...]], axis=0)
        use = jnp.logical_and(has_next, rows >= tq - BLOCK)
        dk = dkc_ref[...] + jnp.where(use, halo_k, 0.0)
        dv = dvc_ref[...] + jnp.where(use, halo_v, 0.0)
        for j in range(2):
            o_ref[:, 1024 + 128 * j:1024 + 128 * (j + 1)] = _rot(dk[:, 128 * j:128 * (j + 1)], c, s, first).astype(BF16)
        o_ref[:, 1280:1536] = dv.astype(BF16)

    tab = pl.BlockSpec((tq, 128), lambda i: (i, 0))
    cur = pl.BlockSpec((tq, 256), lambda i: (i, 0))
    nxt = pl.BlockSpec((BLOCK, 256), lambda i: (jnp.minimum(i + 1, nt - 1), 0))
    return pl.pallas_call(
        body, grid=(nt,),
        in_specs=[pl.BlockSpec((tq, 1024), lambda i: (i, 0)), cur, nxt, cur, nxt, tab, tab],
        out_specs=pl.BlockSpec((tq, QKV), lambda i: (i, 0)),
        out_shape=jax.ShapeDtypeStruct((S, QKV), BF16),
        compiler_params=_cp("parallel"), name=name)(dq, dkc, dkp, dvc, dvp, cosf, sinf)


def _attn_masks(n):
    shape = (GROUP * BLOCK, 2 * BLOCK)
    r = lax.broadcasted_iota(jnp.int32, shape, 0) % BLOCK
    c = lax.broadcasted_iota(jnp.int32, shape, 1)
    ok = jnp.logical_and(c > r, c <= r + BLOCK)
    return jnp.logical_and(ok, jnp.logical_or(c >= BLOCK, n > 0)), ok


def _attn_probs(qs, kw, sink_col, ok):
    s = jnp.where(ok, _dot_nt(qs, kw) * (HEAD_DIM ** -0.5), NEG)
    m = jnp.maximum(jnp.max(s, axis=1, keepdims=True), sink_col)
    p = jnp.exp(s - m)
    es = jnp.exp(sink_col - m)
    inv = 1.0 / (jnp.sum(p, axis=1, keepdims=True) + es)
    return p * inv, es * inv


def _sink_col(sink_ref, g):
    rid = lax.broadcasted_iota(jnp.int32, (GROUP * BLOCK, 1), 0) // BLOCK
    col = jnp.zeros((GROUP * BLOCK, 1), F32)
    for j in range(GROUP):
        col = jnp.where(rid == j, sink_ref[GROUP * g + j], col)
    return col


KV_PAIR = 2
Q_LANES = KV_PAIR * GROUP * HEAD_DIM


def _low_half(shape):
    return lax.broadcasted_iota(jnp.int32, shape, 1) < HEAD_DIM


def _head_rows(ref, lo, gi):
    low = _low_half((BLOCK, 2 * HEAD_DIM))
    out = []
    for s in range(GROUP // 2):
        c0 = gi * GROUP * HEAD_DIM + 2 * HEAD_DIM * s
        x = ref[lo:lo + BLOCK, c0:c0 + 2 * HEAD_DIM]
        out += [jnp.where(low, x, jnp.zeros_like(x)), jnp.where(low, jnp.zeros_like(x), x)]
    return jnp.concatenate(out, axis=0)


def _head_slabs(x):
    low = _low_half((BLOCK, 2 * HEAD_DIM))
    return [jnp.where(low, x[2 * BLOCK * s:2 * BLOCK * s + BLOCK], x[2 * BLOCK * s + BLOCK:2 * BLOCK * (s + 1)])
            for s in range(GROUP // 2)]


def _kv_window(kc_ref, kp_ref, b, gi):
    if b == 0:
        x = jnp.concatenate([kp_ref[...], kc_ref[0:BLOCK, :]], axis=0)
    else:
        x = kc_ref[BLOCK * (b - 1):BLOCK * (b + 1), :]
    other = pltpu.roll(x, HEAD_DIM, 1)
    low = _low_half(x.shape)
    return jnp.where(low, x, other) if gi == 0 else jnp.where(low, other, x)


def _fold_halves(x, gi):
    tot = x + pltpu.roll(x, HEAD_DIM, 1)
    low = _low_half(x.shape)
    return jnp.where(low if gi == 0 else jnp.logical_not(low), tot, 0.0)


def _attn_specs(tq):
    nsub = tq // BLOCK
    qspec = pl.BlockSpec((tq, Q_LANES), lambda p, n: (n, p))
    cur = pl.BlockSpec((tq, KV_PAIR * HEAD_DIM), lambda p, n: (n, p))
    prev = pl.BlockSpec((BLOCK, KV_PAIR * HEAD_DIM), lambda p, n: (jnp.maximum(n * nsub - 1, 0), p))
    return qspec, cur, prev


def attn_fwd(q, k, v, sinks, name, tq, comm=None):
    S = q.shape[0]
    nsub = tq // BLOCK

    def body(sink_ref, q_ref, kc_ref, kp_ref, vc_ref, vp_ref, o_ref):
        p = pl.program_id(0)
        n = pl.program_id(1)
        ok_first, ok_rest = _attn_masks(n)
        for gi in range(KV_PAIR):
            sink_col = _sink_col(sink_ref, KV_PAIR * p + gi)
            base = gi * GROUP * HEAD_DIM
            for b in range(nsub):
                lo = BLOCK * b
                qs = _head_rows(q_ref, lo, gi)
                pn, _ = _attn_probs(qs, _kv_window(kc_ref, kp_ref, b, gi), sink_col, ok_rest if b else ok_first)
                o = _dot(pn.astype(BF16), _kv_window(vc_ref, vp_ref, b, gi))
                for s, slab in enumerate(_head_slabs(o)):
                    c0 = base + 2 * HEAD_DIM * s
                    o_ref[lo:lo + BLOCK, c0:c0 + 2 * HEAD_DIM] = slab.astype(BF16)

    qspec, cur, prev = _attn_specs(tq)
    (out,), couts = _call(
        body, grid=(N_KV // KV_PAIR, S // tq),
        in_specs=[pl.BlockSpec(memory_space=pltpu.SMEM), qspec, cur, prev, cur, prev],
        out_specs=[qspec],
        out_shape=[jax.ShapeDtypeStruct((S, N_HEADS * HEAD_DIM), BF16)],
        operands=(sinks, q, k, k, v, v), name=name, sem=("parallel", "parallel"), comm=comm)
    return out, couts


def attn_bwd(q, k, v, do, sinks, name, tq):
    S = q.shape[0]
    nsub = tq // BLOCK
    nt = S // tq

    def body(sink_ref, q_ref, kc_ref, kp_ref, vc_ref, vp_ref, do_ref, dq_ref, dkc_ref, dkp_ref, dvc_ref, dvp_ref, ds_ref):
        p = pl.program_id(0)
        n = pl.program_id(1)

        @pl.when(n == 0)
        def _():
            ds_ref[...] = jnp.zeros_like(ds_ref)

        for ref in (dkc_ref, dvc_ref, dkp_ref, dvp_ref):
            ref[...] = jnp.zeros_like(ref)
        rid = lax.broadcasted_iota(jnp.int32, (GROUP * BLOCK, 1), 0) // BLOCK
        sub = lax.broadcasted_iota(jnp.int32, (8, 128), 0)
        ok_first, ok_rest = _attn_masks(n)
        for gi in range(KV_PAIR):
            sink_col = _sink_col(sink_ref, KV_PAIR * p + gi)
            base = gi * GROUP * HEAD_DIM
            dsink = jnp.zeros((8, 128), F32)
            for b in range(nsub):
                lo = BLOCK * b
                qs = _head_rows(q_ref, lo, gi)
                dos = _head_rows(do_ref, lo, gi)
                kw = _kv_window(kc_ref, kp_ref, b, gi)
                vw = _kv_window(vc_ref, vp_ref, b, gi)
                pn, ps = _attn_probs(qs, kw, sink_col, ok_rest if b else ok_first)
                dp = _dot_nt(dos, vw)
                delta = jnp.sum(pn * dp, axis=1, keepdims=True)
                dsb = (pn * (dp - delta) * (HEAD_DIM ** -0.5)).astype(BF16)
                for s, slab in enumerate(_head_slabs(_dot(dsb, kw))):
                    c0 = base + 2 * HEAD_DIM * s
                    dq_ref[lo:lo + BLOCK, c0:c0 + 2 * HEAD_DIM] = slab
                dkw = _fold_halves(_dot_tn(dsb, qs), gi)
                dvw = _fold_halves(_dot_tn(pn.astype(BF16), dos), gi)
                if b == 0:
                    dkp_ref[...] += dkw[0:BLOCK]
                    dvp_ref[...] += dvw[0:BLOCK]
                else:
                    dkc_ref[lo - BLOCK:lo, :] += dkw[0:BLOCK]
                    dvc_ref[lo - BLOCK:lo, :] += dvw[0:BLOCK]
                dkc_ref[lo:lo + BLOCK, :] += dkw[BLOCK:2 * BLOCK]
                dvc_ref[lo:lo + BLOCK, :] += dvw[BLOCK:2 * BLOCK]
                sd = ps * delta
                for j in range(GROUP):
                    tot = -jnp.sum(jnp.where(rid == j, sd, 0.0))
                    dsink = dsink + jnp.where(sub == j, tot, 0.0)
            ds_ref[gi] += dsink

    qspec, cur, prev = _attn_specs(tq)
    halo = pl.BlockSpec((BLOCK, KV_PAIR * HEAD_DIM), lambda p, n: (n, p))
    kv_shape = jax.ShapeDtypeStruct((S, N_KV * HEAD_DIM), F32)
    halo_shape = jax.ShapeDtypeStruct((nt * BLOCK, N_KV * HEAD_DIM), F32)
    return pl.pallas_call(
        body, grid=(N_KV // KV_PAIR, nt),
        in_specs=[pl.BlockSpec(memory_space=pltpu.SMEM), qspec, cur, prev, cur, prev, qspec],
        out_specs=[qspec, cur, halo, cur, halo, pl.BlockSpec((KV_PAIR, 8, 128), lambda p, n: (p, 0, 0))],
        out_shape=[jax.ShapeDtypeStruct((S, N_HEADS * HEAD_DIM), F32), kv_shape, halo_shape, kv_shape, halo_shape,
                   jax.ShapeDtypeStruct((N_KV, 8, 128), F32)],
        compiler_params=_cp("parallel", "arbitrary"), name=name)(sinks, q, k, k, v, v, do)


def _rows_before(cur, prev8, k):
    if k == 0:
        return cur
    n = cur.shape[0]
    ext = jnp.concatenate([prev8, cur], axis=0)
    return ext[8 - k:8 - k + n]


def _rows_after(cur, next8, k):
    if k == 0:
        return cur
    n = cur.shape[0]
    ext = jnp.concatenate([cur, next8], axis=0)
    return ext[k:k + n]


def _gelu(x):
    c = math.sqrt(2.0 / math.pi)
    t = jnp.tanh(c * (x + 0.044715 * (x * x * x)))
    return 0.5 * (1.0 + t), t


def _neg_expm1(u):
    ser = 1.0 + u * (1.0 / 6.0)
    for k in range(5, 1, -1):
        ser = 1.0 + (u * (1.0 / k)) * ser
    return jnp.where(u > -0.125, -(u * ser), 1.0 - jnp.exp(u))


def _block_diag(xb16, w_ref):
    return jnp.concatenate([_dot(xb16[:, RNN_W * n:RNN_W * (n + 1)], w_ref[n]) for n in range(RNN_BLOCKS)], axis=1)


def _lru_gates(xb, prev8, cw_ref, cb_ref, wra_ref, wrx_ref, bra_ref, brx_ref, lsl_ref):
    xc = cb_ref[...] + cw_ref[3:4, :] * xb
    for w in range(CONV_W - 1):
        xc = xc + cw_ref[w:w + 1, :] * _rows_before(xb, prev8, CONV_W - 1 - w)
    xcb = xc.astype(BF16)
    r = jax.nn.sigmoid(_block_diag(xcb, wra_ref) + bra_ref[...])
    ig = jax.nn.sigmoid(_block_diag(xcb, wrx_ref) + brx_ref[...])
    la = LRU_C * r * lsl_ref[...]
    a = jnp.exp(la)
    sq = jnp.sqrt(_neg_expm1(2.0 * la))
    return xc, xcb, r, ig, a, sq


def lru_fwd(xg, p, name, tm=256, comm=None):
    S = xg.shape[0]
    tm = min(tm, S)

    def body(xg_ref, xp_ref, cw_ref, cb_ref, wra_ref, wrx_ref, bra_ref, brx_ref, lsl_ref, y_ref, h_ref, hc_ref, a_s, b_s):
        i = pl.program_id(0)
        xb = xg_ref[:, 0:D_RNN]
        gb = xg_ref[:, D_RNN:2 * D_RNN]
        prev8 = jnp.where(i > 0, xp_ref[:, 0:D_RNN], 0.0)
        xc, _, r, ig, a, sq = _lru_gates(xb, prev8, cw_ref, cb_ref, wra_ref, wrx_ref, bra_ref, brx_ref, lsl_ref)
        a_s[...] = a
        b_s[...] = sq * (ig * xc)

        @pl.when(i == 0)
        def _():
            hc_ref[...] = jnp.zeros_like(hc_ref)

        def chunk(c, h):
            o = pl.multiple_of(c * 8, 8)
            av = a_s[pl.ds(o, 8), :]
            bv = b_s[pl.ds(o, 8), :]
            rows = []
            for t in range(8):
                h = av[t:t + 1, :] * h + bv[t:t + 1, :]
                rows.append(h)
            h_ref[pl.ds(o, 8), :] = jnp.concatenate(rows, axis=0)
            return h

        h_last = lax.fori_loop(0, tm // 8, chunk, hc_ref[0:1, :])
        hc_ref[0:1, :] = h_last
        cdf, _ = _gelu(gb)
        y_ref[...] = (h_ref[...] * (gb * cdf)).astype(BF16)

    vec = pl.BlockSpec((1, D_RNN), lambda i: (0, 0))
    wsp = pl.BlockSpec((RNN_BLOCKS, RNN_W, RNN_W), lambda i: (0, 0, 0))
    return _call(
        body, grid=(S // tm,),
        in_specs=[pl.BlockSpec((tm, 2 * D_RNN), lambda i: (i, 0)),
                  pl.BlockSpec((8, 2 * D_RNN), lambda i: (jnp.maximum(i * (tm // 8) - 1, 0), 0)),
                  pl.BlockSpec((CONV_W, D_RNN), lambda i: (0, 0)), vec, wsp, wsp, vec, vec, vec],
        out_specs=[pl.BlockSpec((tm, D_RNN), lambda i: (i, 0)), pl.BlockSpec((tm, D_RNN), lambda i: (i, 0))],
        out_shape=[jax.ShapeDtypeStruct((S, D_RNN), BF16), jax.ShapeDtypeStruct((S, D_RNN), F32)],
        scratch=[pltpu.VMEM((8, D_RNN), F32), pltpu.VMEM((tm, D_RNN), F32), pltpu.VMEM((tm, D_RNN), F32)],
        operands=(xg, xg, p["conv_w"], p["conv_b"], p["w_ra"], p["w_rx"], p["b_ra"], p["b_rx"], p["lsl"]),
        name=name, sem=("arbitrary",), comm=comm)


def lru_bwd(dy, xg, h, p, name, tm=256):
    S = xg.shape[0]
    tm = min(tm, S)
    nt = S // tm

    def body(dy_ref, xg_ref, xp_ref, h_ref, hp_ref, cw_ref, cb_ref, wra_ref, wrx_ref, bra_ref, brx_ref,
             lsl_ref, dxg_ref, dcw_ref, dcb_ref, dwra_ref, dwrx_ref, dbra_ref, dbrx_ref, dlam_ref,
             lc_ref, nx_ref, a_s, g_s, l_s):
        i = pl.program_id(0)
        ti = nt - 1 - i

        @pl.when(i == 0)
        def _():
            lc_ref[...] = jnp.zeros_like(lc_ref)
            nx_ref[...] = jnp.zeros_like(nx_ref)
            for ref in (dcw_ref, dcb_ref, dwra_ref, dwrx_ref, dbra_ref, dbrx_ref, dlam_ref):
                ref[...] = jnp.zeros_like(ref)

        xb = xg_ref[:, 0:D_RNN]
        gb = xg_ref[:, D_RNN:2 * D_RNN]
        prev8 = jnp.where(ti > 0, xp_ref[:, 0:D_RNN], 0.0)
        xc, xcb, r, ig, a, sq = _lru_gates(xb, prev8, cw_ref, cb_ref, wra_ref, wrx_ref, bra_ref, brx_ref, lsl_ref)
        hh = h_ref[...]
        hprev = _rows_before(hh, jnp.where(ti > 0, hp_ref[...], 0.0), 1)
        dy_ = dy_ref[...]
        cdf, th = _gelu(gb)
        c0 = math.sqrt(2.0 / math.pi)
        dgate = cdf + gb * (0.5 * (1.0 - th * th) * c0 * (1.0 + 3.0 * 0.044715 * gb * gb))
        dgb = dy_ * hh * dgate
        a_s[...] = a
        g_s[...] = dy_ * (gb * cdf)

        def chunk(cc, carry):
            o = pl.multiple_of((tm // 8 - 1 - cc) * 8, 8)
            av = a_s[pl.ds(o, 8), :]
            gv = g_s[pl.ds(o, 8), :]
            rows = [None] * 8
            for t in range(7, -1, -1):
                lam_t = gv[t:t + 1, :] + carry
                rows[t] = lam_t
                carry = av[t:t + 1, :] * lam_t
            l_s[pl.ds(o, 8), :] = jnp.concatenate(rows, axis=0)
            return carry

        carry = lax.fori_loop(0, tm // 8, chunk, lc_ref[0:1, :])
        lc_ref[0:1, :] = carry
        lam = l_s[...]
        da = lam * hprev
        dixc = lam * sq
        di = dixc * xc
        dxc = dixc * ig
        dsq = lam * (ig * xc)
        dla = da * a - dsq * (a * a / sq)
        dr = dla * (LRU_C * lsl_ref[...])
        dlam_ref[...] += jnp.sum(dla * (LRU_C * r), axis=0, keepdims=True)
        dpr = dr * r * (1.0 - r)
        dpi = di * ig * (1.0 - ig)
        dbra_ref[...] += jnp.sum(dpr, axis=0, keepdims=True)
        dbrx_ref[...] += jnp.sum(dpi, axis=0, keepdims=True)
        dprb = dpr.astype(BF16)
        dpib = dpi.astype(BF16)
        back = []
        for n in range(RNN_BLOCKS):
            sl = slice(RNN_W * n, RNN_W * (n + 1))
            dwra_ref[n] += _dot_tn(xcb[:, sl], dprb[:, sl])
            dwrx_ref[n] += _dot_tn(xcb[:, sl], dpib[:, sl])
            back.append(_dot_nt(dprb[:, sl], wra_ref[n]) + _dot_nt(dpib[:, sl], wrx_ref[n]))
        dxc = dxc + jnp.concatenate(back, axis=1)
        dcb_ref[...] += jnp.sum(dxc, axis=0, keepdims=True)
        next8 = nx_ref[...]
        dxb = cw_ref[3:4, :] * dxc
        dcw_ref[3:4, :] += jnp.sum(dxc * xb, axis=0, keepdims=True)
        for w in range(CONV_W - 1):
            k = CONV_W - 1 - w
            dcw_ref[w:w + 1, :] += jnp.sum(dxc * _rows_before(xb, prev8, k), axis=0, keepdims=True)
            dxb = dxb + cw_ref[w:w + 1, :] * _rows_after(dxc, next8, k)
        nx_ref[...] = dxc[0:8, :]
        dxg_ref[:, 0:D_RNN] = dxb.astype(BF16)
        dxg_ref[:, D_RNN:2 * D_RNN] = dgb.astype(BF16)

    rev = lambda i: (nt - 1 - i, 0)
    before = lambda i: (jnp.maximum((nt - 1 - i) * (tm // 8) - 1, 0), 0)
    vec = pl.BlockSpec((1, D_RNN), lambda i: (0, 0))
    wsp = pl.BlockSpec((RNN_BLOCKS, RNN_W, RNN_W), lambda i: (0, 0, 0))
    cwsp = pl.BlockSpec((CONV_W, D_RNN), lambda i: (0, 0))
    return pl.pallas_call(
        body, grid=(nt,),
        in_specs=[pl.BlockSpec((tm, D_RNN), rev), pl.BlockSpec((tm, 2 * D_RNN), rev), pl.BlockSpec((8, 2 * D_RNN), before),
                  pl.BlockSpec((tm, D_RNN), rev), pl.BlockSpec((8, D_RNN), before),
                  cwsp, vec, wsp, wsp, vec, vec, vec],
        out_specs=[pl.BlockSpec((tm, 2 * D_RNN), rev), cwsp, vec, wsp, wsp, vec, vec, vec],
        out_shape=[jax.ShapeDtypeStruct((S, 2 * D_RNN), BF16), jax.ShapeDtypeStruct((CONV_W, D_RNN), F32),
                   jax.ShapeDtypeStruct((1, D_RNN), F32), jax.ShapeDtypeStruct((RNN_BLOCKS, RNN_W, RNN_W), F32),
                   jax.ShapeDtypeStruct((RNN_BLOCKS, RNN_W, RNN_W), F32), jax.ShapeDtypeStruct((1, D_RNN), F32),
                   jax.ShapeDtypeStruct((1, D_RNN), F32), jax.ShapeDtypeStruct((1, D_RNN), F32)],
        scratch_shapes=[pltpu.VMEM((8, D_RNN), F32), pltpu.VMEM((8, D_RNN), F32), pltpu.VMEM((tm, D_RNN), F32),
                        pltpu.VMEM((tm, D_RNN), F32), pltpu.VMEM((tm, D_RNN), F32)],
        compiler_params=_cp("arbitrary"), name=name)(
            dy, xg, xg, h, h, p["conv_w"], p["conv_b"], p["w_ra"], p["w_rx"], p["b_ra"], p["b_rx"], p["lsl"])


def _place():
    x, y, c = lax.axis_index("x"), lax.axis_index("y"), lax.axis_index("c")
    chips = [(1 - x, y), (x, 1 - y), (1 - x, 1 - y)]
    return x, y, c, chips


def comm_ag(arrs):
    n = len(arrs)

    def copies(ins, outs, sems):
        send_sems, recv_sems, local_sems = sems
        x, y, c, chips = _place()
        me, sibling = (x, y, c), (x, y, 1 - c)

        def rows(a, px, py, pc):
            return outs[a].at[4 * px + 2 * py + pc]

        def copy(a, k, block, to, own=False):
            return pltpu.make_async_remote_copy(
                src_ref=ins[a] if own else rows(a, *block), dst_ref=rows(a, *block),
                send_sem=send_sems.at[k, a], recv_sem=recv_sems.at[k, a], device_id=to, device_id_type=MESH)

        mine = [pltpu.make_async_copy(ins[a], rows(a, *me), local_sems.at[a]) for a in range(n)]
        first = [copy(a, 1 + j, me, (*chip, c), own=True) for j, chip in enumerate(chips) for a in range(n)]
        first += [copy(a, 0, me, sibling, own=True) for a in range(n)]
        return copy, chips, c, me, sibling, mine, first

    def start(ins, outs, sems):
        _, _, _, _, _, mine, first = copies(ins, outs, sems)
        for cp in mine + first:
            cp.start()

    def relay(ins, outs, sems):
        copy, chips, c, me, sibling, _, _ = copies(ins, outs, sems)
        for j, chip in enumerate(chips):
            for a in range(n):
                copy(a, 1 + j, (*chip, c), me).wait_recv()
                copy(a, 4 + j, (*chip, c), sibling).start()

    def finish(ins, outs, sems):
        copy, chips, c, me, sibling, mine, first = copies(ins, outs, sems)
        passed = [copy(a, 4 + j, (*chip, c), sibling) for j, chip in enumerate(chips) for a in range(n)]
        for a in range(n):
            copy(a, 0, sibling, me).wait_recv()
        for j, chip in enumerate(chips):
            for a in range(n):
                copy(a, 4 + j, (*chip, 1 - c), me).wait_recv()
        for cp in first + passed:
            cp.wait_send()
        for cp in mine:
            cp.wait()

    return Comm(arrs, [jax.ShapeDtypeStruct((N_DEV,) + p.shape, p.dtype) for p in arrs],
                [pltpu.SemaphoreType.DMA((7, n)), pltpu.SemaphoreType.DMA((7, n)), pltpu.SemaphoreType.DMA((n,))],
                start, finish, relay=relay)


def comm_pair(sends):
    n = len(sends)

    def copies(ins, outs, sems):
        send_sems, recv_sems = sems
        x, y, c, _ = _place()
        return [pltpu.make_async_remote_copy(
            src_ref=ins[a].at[k, 1 - c], dst_ref=outs[a].at[k], send_sem=send_sems.at[k, a], recv_sem=recv_sems.at[k, a],
            device_id=(x, y, 1 - c), device_id_type=MESH) for k in range(4) for a in range(n)]

    def start(ins, outs, sems):
        for cp in copies(ins, outs, sems):
            cp.start()

    def finish(ins, outs, sems):
        for cp in copies(ins, outs, sems):
            cp.wait()

    return Comm(sends, [jax.ShapeDtypeStruct((4,) + s.shape[2:], s.dtype) for s in sends],
                [pltpu.SemaphoreType.DMA((4, n)), pltpu.SemaphoreType.DMA((4, n))], start, finish)


def pair_sum(send, got, out_dtype, tr, name):
    _, _, R, C = send.shape
    c = lax.axis_index("c").astype(jnp.int32).reshape(1)

    def body(c_ref, s_ref, g_ref, o_ref):
        o_ref[...] = (s_ref[...] + g_ref[...]).astype(out_dtype)

    return pl.pallas_call(
        body,
        grid_spec=pltpu.PrefetchScalarGridSpec(
            num_scalar_prefetch=1, grid=(4, R // tr),
            in_specs=[pl.BlockSpec((None, None, tr, C), lambda k, i, cr: (k, cr[0], i, 0)),
                      pl.BlockSpec((None, tr, C), lambda k, i, cr: (k, i, 0))],
            out_specs=pl.BlockSpec((None, tr, C), lambda k, i, cr: (k, i, 0))),
        out_shape=jax.ShapeDtypeStruct((4, R, C), out_dtype),
        compiler_params=_cp("parallel", "parallel"), name=name)(c, send, got)


def comm_chip(items, bufs):
    ns = len(items)
    segs = [(i, b, s0, nr, d0) for i, (_, b, ranges) in enumerate(items) for (s0, nr, d0) in ranges]

    def copies(ins, outs, sems):
        send_sems, recv_sems, local_sems = sems
        x, y, c, chips = _place()
        mychip = 2 * x + y
        mine = [pltpu.make_async_copy(ins[i].at[mychip, pl.ds(s0, nr)], outs[b].at[mychip, pl.ds(d0, nr)], local_sems.at[q])
                for q, (i, b, s0, nr, d0) in enumerate(segs)]
        remote = [pltpu.make_async_remote_copy(
            src_ref=ins[i].at[2 * px + py, pl.ds(s0, nr)], dst_ref=outs[b].at[mychip, pl.ds(d0, nr)],
            send_sem=send_sems.at[j, q], recv_sem=recv_sems.at[j, q], device_id=(px, py, c), device_id_type=MESH)
            for j, (px, py) in enumerate(chips) for q, (i, b, s0, nr, d0) in enumerate(segs)]
        return mine, remote

    def start(ins, outs, sems):
        mine, remote = copies(ins, outs, sems)
        for cp in mine + remote:
            cp.start()

    def finish(ins, outs, sems):
        mine, remote = copies(ins, outs, sems)
        for cp in remote + mine:
            cp.wait()

    q = len(segs)
    return Comm([it[0] for it in items] + list(bufs), [jax.ShapeDtypeStruct(b.shape, b.dtype) for b in bufs],
                [pltpu.SemaphoreType.DMA((3, q)), pltpu.SemaphoreType.DMA((3, q)), pltpu.SemaphoreType.DMA((q,))],
                start, finish, aliases={ns + b: b for b in range(len(bufs))})


def adamw(parts, row0, w, m, v, tr, name):
    C = w.shape[-1]
    rows = w.size // C
    off = row0 // tr

    def body(p_ref, w_ref, m_ref, v_ref, g_ref, d_ref, nm_ref, nv_ref):
        g = ((p_ref[0].astype(F32) + p_ref[1].astype(F32)) + p_ref[2].astype(F32)) + p_ref[3].astype(F32)
        m2 = ADAM_B1 * m_ref[...] + (1.0 - ADAM_B1) * g
        v2 = ADAM_B2 * v_ref[...] + (1.0 - ADAM_B2) * (g * g)
        mh = m2 / (1.0 - ADAM_B1 ** ADAM_STEP)
        vh = v2 / (1.0 - ADAM_B2 ** ADAM_STEP)
        g_ref[...] = g
        d_ref[...] = -ADAM_LR * (mh / (jnp.sqrt(vh) + ADAM_EPS) + ADAM_WD * w_ref[...])
        nm_ref[...] = m2
        nv_ref[...] = v2

    row = pl.BlockSpec((tr, C), lambda i: (i, 0))
    shp = jax.ShapeDtypeStruct((rows, C), F32)
    outs = pl.pallas_call(
        body, grid=(rows // tr,),
        in_specs=[pl.BlockSpec((4, tr, C), lambda i: (0, off + i, 0)), row, row, row],
        out_specs=[row, row, row, row], out_shape=[shp, shp, shp, shp],
        compiler_params=_cp("parallel"), name=name)(parts, w.reshape(rows, C), m.reshape(rows, C), v.reshape(rows, C))
    return [o.reshape(w.shape) for o in outs]


def _full_from_gathered(flat, shape, axis):
    t = jnp.moveaxis(flat.reshape((N_DEV,) + shape), 0, axis)
    return t.reshape(shape[:axis] + (N_DEV * shape[axis],) + shape[axis + 1:])


def _shards_of_full(full, shape, axis):
    t = full.reshape(shape[:axis] + (N_DEV, shape[axis]) + shape[axis + 1:])
    return jnp.moveaxis(t, axis, 0).reshape(N_DEV, -1)


def _small_pack(args, prefix):
    flat = jnp.concatenate([args[prefix + n].reshape(-1) for n, _, _ in SMALL] + [args[prefix + "attn_sinks"].reshape(-1)])
    return jnp.pad(flat, (0, SM_ROWS * 1024 - flat.shape[0])).reshape(SM_ROWS, 1024)


def _small_unpack(pack):
    flat = pack.reshape(-1)
    out, off = {}, 0
    for n, shape, _ in SMALL:
        size = math.prod(shape)
        out[n] = flat[off:off + size].reshape(shape)
        off += size
    out["attn_sinks"] = flat[SINK_OFF:SINK_OFF + 32].reshape(2, 16)
    return out


def _tn_tile(n):
    return next(t for t in (1408, 1024, 768, 512, 256, 128) if n % t == 0)


def _dw(a, b, name, planes=1, shard=None, comm=None):
    return mm_tn(a, b, name, _tn_tile(a.shape[1]), _tn_tile(b.shape[-1]), planes=planes, shard=shard, comm=comm)


class _NoExchange:
    def __init__(self, layers):
        self.layers, self.grads = layers, [{} for _ in range(DEPTH)]

    def weights(self, l, piece):
        return self.layers[l]

    def fwd_slot(self):
        return None

    def fwd_done(self, outs):
        pass

    def loss_ready(self, loss):
        pass

    def bwd_slot(self, slot):
        return None

    def bwd_done(self, slot, outs):
        pass

    def grads_ready(self, l, part, g):
        self.grads[l].update(g)


def _local_step(x, target, sched, sinks):
    S = x.shape[0]
    tq = min(512, S)
    cosf, sinf = _rope_tables(S)
    saved = []
    h, hb = x, x.astype(BF16)
    for l in range(DEPTH):
        j = l // 2
        wl = {}
        sv = {"h0b": hb, "wl": wl}
        wl.update(sched.weights(l, "a"))
        (sv["gu1"], a1), co = ffn_up(hb, wl["w_in1"], f"ffn1_up_{l}", comm=sched.fwd_slot())
        sched.fwd_done(co)
        sv["a1"] = a1
        wl.update(sched.weights(l, "b"))
        (sv["z1"], h, hb), _ = mm_ln(a1, _rows_view(wl["r_ff"], FF_SHARD, 0), h, wl["ln_g"][0], wl["ln_b"][0], 0.5,
                                     f"ffn1_down_ln_{l}")
        sv["h1b"] = hb
        if l % 2 == 0:
            qkv = mm_plain(hb, _mat_view(wl["m_c"]), F32, f"attn_qkv_{l}")
            qr, kr, vv = rope_fwd(qkv, cosf, sinf, f"rope_{l}")
            mix_in, co = attn_fwd(qr, kr, vv, sinks[j], f"attn_core_{l}", tq, comm=sched.fwd_slot())
            sv.update(qr=qr, kr=kr, vv=vv)
        else:
            xg = mm_plain(hb, _mat_view(wl["m_c"]), F32, f"lru_in_{l}")
            (mix_in, hstate), co = lru_fwd(xg, wl["lru"], f"lru_core_{l}", comm=sched.fwd_slot())
            sv.update(xg=xg, hstate=hstate)
        sched.fwd_done(co)
        sv["mix_in"] = mix_in
        (sv["z2"], h, hb), _ = mm_ln(mix_in, _rows_view(wl["m_o"], MIX_SHARD, 0), h, wl["ln_g"][1], wl["ln_b"][1], 1.0,
                                     f"mix_out_ln_{l}")
        sv["h2b"] = hb
        wl.update(sched.weights(l, "c"))
        (sv["gu2"], a2), co = ffn_up(hb, wl["w_in2"], f"ffn2_up_{l}", comm=sched.fwd_slot())
        sched.fwd_done(co)
        sv["a2"] = a2
        (sv["z3"], h, hb), _ = mm_ln(a2, _rows_view(wl["r_ff"], FF_SHARD, FF_SHARD), h, wl["ln_g"][2], wl["ln_b"][2], 0.5,
                                     f"ffn2_down_ln_{l}")
        saved.append(sv)

    dy, lvec = loss_head(h, target, "loss_head")
    loss = lvec[0, 0]
    sched.loss_ready(loss)

    def hosted(slot, on, fn):
        comm = sched.bwd_slot(slot) if on else None
        out, co = fn(comm)
        if comm is not None:
            sched.bwd_done(slot, co)
        return out

    def ffn_bwd(dy, z, g, gu, a, xin_b, r_ff, blk, w_in, tag, host):
        dz, dzb, dg, db = ln_bwd(dy, z, g, 0.5, f"ln_bwd_{tag}")
        dh, _ = ffn_mid_bwd(dzb, r_ff, blk, gu, f"ffn_mid_bwd_{tag}")
        d_wout = hosted("pair", host, lambda cm: _dw(a, dzb, f"dw_out_{tag}", comm=cm))
        d_win = hosted("chip_x", host, lambda cm: _dw(xin_b, dh, f"dw_in_{tag}", planes=2, shard=2 * D_FF // N_DEV, comm=cm))
        dx = hosted("chip_y", host, lambda cm: mm_res2_nt(dh, _mat_view(w_in, D_FF, 0), _mat_view(w_in, D_FF, 1), dz, ALPHA,
                                                          f"ffn_dx_{tag}", comm=cm))
        return dx, dg, db, d_wout, d_win

    for l in reversed(range(DEPTH)):
        j = l // 2
        sv = saved[l]
        wl = sv["wl"]
        gl = {}
        dg, db = [None] * 3, [None] * 3
        dy, dg[2], db[2], gl["w_out2"], gl["w_in2"] = ffn_bwd(
            dy, sv["z3"], wl["ln_g"][2], sv["gu2"], sv["a2"], sv["h2b"], wl["r_ff"], 1, wl["w_in2"], f"2_{l}", True)
        dz, dzb, dg[1], db[1] = ln_bwd(dy, sv["z2"], wl["ln_g"][1], 1.0, f"ln_bwd_mix_{l}")
        w_mix = _rows_view(wl["m_o"], MIX_SHARD, 0)
        if l % 2 == 0:
            gl["mix_out"], _ = _dw(sv["mix_in"], dzb, f"dw_o_{l}")
            do = mm_plain(dzb, w_mix, BF16, f"attn_do_{l}", nt=True)
            dq, dkc, dkp, dvc, dvp, dsk = attn_bwd(sv["qr"], sv["kr"], sv["vv"], do, sinks[j], f"attn_core_bwd_{l}", tq)
            gl["sinks"] = dsk[:, :GROUP, 0].reshape(N_HEADS)
            dmid = rope_bwd(dq, dkc, dkp, dvc, dvp, cosf, sinf, f"rope_bwd_{l}", tq)
            gl["mix_in"], _ = _dw(sv["h1b"], dmid, f"dw_qkv_{l}", shard=QKV // N_DEV)
        else:
            gl["mix_out"], _ = _dw(sv["mix_in"], dzb, f"dw_lru_out_{l}")
            dyl = mm_plain(dzb, w_mix, F32, f"lru_dy_{l}", nt=True)
            dmid, dcw, dcb, dwra, dwrx, dbra, dbrx, dlam = lru_bwd(dyl, sv["xg"], sv["hstate"], wl["lru"], f"lru_core_bwd_{l}")
            gl.update(conv_w=dcw, conv_b=dcb[0], w_ra=dwra, w_rx=dwrx, b_ra=dbra[0], b_rx=dbrx[0],
                      lam=dlam[0] * wl["lru"]["sig_neg"])
            gl["mix_in"], _ = _dw(sv["h1b"], dmid, f"dw_lru_in_{l}", shard=2 * D_RNN // N_DEV)
        dy = mm_res_nt(dmid, _mat_view(wl["m_c"]), dz, ALPHA, f"mix_dx_{l}")
        sched.grads_ready(l, "hi", gl)
        lo = {}
        dy, dg[0], db[0], lo["w_out1"], lo["w_in1"] = ffn_bwd(
            dy, sv["z1"], wl["ln_g"][0], sv["gu1"], sv["a1"], sv["h0b"], wl["r_ff"], 0, wl["w_in1"], f"1_{l}", True)
        lo["ln_g"], lo["ln_b"] = jnp.concatenate(dg, axis=0), jnp.concatenate(db, axis=0)
        sched.grads_ready(l, "lo", lo)
    return loss, dy


def _lru_params(full, j, w_ra, w_rx):
    lam = full["lru_lambda"][j]
    return {
        "conv_w": full["lru_conv_w"][j], "conv_b": full["lru_conv_b"][j].reshape(1, -1),
        "w_ra": w_ra, "w_rx": w_rx,
        "b_ra": full["lru_b_ra"][j].reshape(1, -1), "b_rx": full["lru_b_rx"][j].reshape(1, -1),
        "lsl": jax.nn.log_sigmoid(lam).reshape(1, -1), "sig_neg": jax.nn.sigmoid(-lam),
    }


def _row_shards(f):
    return f.reshape(N_DEV, -1, f.shape[1])


def _gate_shards(w):
    return w.reshape(RNN_BLOCKS, N_DEV, RNN_W // N_DEV, RNN_W).transpose(1, 0, 2, 3).reshape(N_DEV, -1, RNN_W)


class _Fsdp:
    def __init__(self, args):
        self.args = args
        self.b16 = lambda a: a.astype(BF16)
        self.sm = _small_pack(args, "")
        self.sent = {l: self._ag_arrays(l) for l in range(DEPTH)}
        self.queue = [(l, p) for l in range(DEPTH) for p in "abc"][1:]
        self.hosting = None
        got = run_comm(comm_ag(self.sent[0]["a"] + [self.sm]), "all_gather_first")
        self.raw = {0: {"a": got[:-1]}}
        gflat = got[-1].reshape(N_DEV, SM_ROWS * 1024)
        self.small, off = {}, 0
        for n, shape, axis in SMALL:
            size = math.prod(shape)
            self.small[n] = _full_from_gathered(gflat[:, off:off + size], shape, axis)
            off += size
        self.bufs = {"r": jnp.zeros((4, R_ROWS, D_MODEL), BF16), "c704": jnp.zeros((4, 2 * DEPTH * D_MODEL, 704), BF16),
                     "c192": jnp.zeros((4, 2 * D_MODEL, 192), BF16), "c256": jnp.zeros((4, C256_ROWS, RNN_W), BF16),
                     "sm": jnp.zeros((4, SM_ROWS, 1024), F32)}
        self.pending = None
        self.small_g = [{} for _ in range(DEPTH)]
        self.loss = None

    def _ag_arrays(self, l):
        a, b16, j = self.args, self.b16, l // 2
        r_ff = jnp.concatenate([b16(a["ffn1_w_out"][l]), b16(a["ffn2_w_out"][l])], axis=0)
        if l % 2 == 0:
            return {"a": [b16(a["ffn1_w_in"][l]), b16(a["attn_w_qkv"][j])], "b": [r_ff, b16(a["attn_w_o"][j])],
                    "c": [b16(a["ffn2_w_in"][l])]}
        gates = jnp.concatenate([b16(a["lru_w_ra"][j]).reshape(-1, RNN_W), b16(a["lru_w_rx"][j]).reshape(-1, RNN_W)], axis=0)
        return {"a": [b16(a["ffn1_w_in"][l]), b16(a["lru_w_in"][j])], "b": [r_ff, b16(a["lru_w_out"][j]), gates],
                "c": [b16(a["ffn2_w_in"][l])]}

    def weights(self, l, piece):
        raw, j = self.raw[l][piece], l // 2
        full = lambda g: g.transpose(1, 0, 2).reshape(g.shape[1], -1)
        if piece == "a":
            return {"w_in1": full(raw[0]), "m_c": full(raw[1]), "ln_g": self.small["ln_g"][l], "ln_b": self.small["ln_b"][l]}
        if piece == "c":
            return {"w_in2": full(raw[0])}
        wl = {"r_ff": raw[0], "m_o": raw[1]}
        if l % 2:
            g = raw[2].reshape(N_DEV, 2, RNN_BLOCKS, RNN_W // N_DEV, RNN_W).transpose(1, 2, 0, 3, 4)
            g = g.reshape(2, RNN_BLOCKS, RNN_W, RNN_W)
            wl["lru"] = _lru_params(self.small, j, g[0], g[1])
        return wl

    def fwd_slot(self):
        if not self.queue:
            return None
        self.hosting = self.queue.pop(0)
        l, piece = self.hosting
        return comm_ag(self.sent[l][piece])

    def fwd_done(self, outs):
        if self.hosting is not None:
            l, piece = self.hosting
            self.raw.setdefault(l, {})[piece] = outs
            self.hosting = None

    def loss_ready(self, loss):
        self.loss = loss

    def _ranges(self, l, part):
        j = l // 2
        if part == "lo":
            return [("r", [(0, FF_SHARD, l * FF_SHARD)]), ("c704", [(0, D_MODEL, l * D_MODEL)])]
        r = [(0, FF_SHARD, R_FFN2 + l * FF_SHARD), (FF_SHARD, MIX_SHARD, (R_LRU if l % 2 else R_ATTN) + j * MIX_SHARD)]
        c704 = [(0, D_MODEL, (DEPTH + l) * D_MODEL)]
        if l % 2:
            gr = RNN_BLOCKS * (RNN_W // N_DEV)
            mix = ("c256", [(0, D_MODEL, j * D_MODEL), (D_MODEL, gr, C256_RA + j * gr), (D_MODEL + gr, gr, C256_RX + j * gr)])
        else:
            mix = ("c192", [(0, D_MODEL, j * D_MODEL)])
        return [("r", r), ("c704", c704), mix]

    def grads_ready(self, l, part, g):
        self.small_g[l].update(g)
        if part == "lo":
            ts = [_row_shards(g["w_out1"]), g["w_in1"]]
        else:
            tmix = g["mix_in"]
            if l % 2:
                tmix = jnp.concatenate([tmix, _gate_shards(g["w_ra"]), _gate_shards(g["w_rx"])], axis=1)
            ts = [jnp.concatenate([_row_shards(g["w_out2"]), _row_shards(g["mix_out"])], axis=1), g["w_in2"], tmix]
        sends = [t.reshape((4, 2) + t.shape[1:]) for t in ts]
        if (l, part) == (0, "lo"):
            self._last(sends)
        else:
            self.pending = {"tag": f"{l}{part}", "sends": sends, "where": self._ranges(l, part)}

    def _pair_sums(self, tag, sends, gots):
        tiles = (None, 512, 256)
        return [pair_sum(s, g, BF16, t or s.shape[2] // 2, f"pair_sum_{nm}_{tag}")
                for s, g, t, nm in zip(sends, gots, tiles, ("r", "c704", "mix"))]

    def _chip(self, parts, where, pick):
        names = [where[i][0] for i in pick]
        return names, comm_chip([(parts[i], k, where[i][1]) for k, i in enumerate(pick)], [self.bufs[n] for n in names])

    def bwd_slot(self, slot):
        p = self.pending
        if p is None:
            return None
        if slot == "pair":
            return comm_pair(p["sends"])
        pick = [1] if slot == "chip_x" else [i for i in range(len(p["sends"])) if i != 1]
        p["names"], comm = self._chip(p["parts"], p["where"], pick)
        return comm

    def bwd_done(self, slot, outs):
        p = self.pending
        if slot == "pair":
            p["parts"] = self._pair_sums(p["tag"], p["sends"], outs)
            return
        for n, o in zip(p["names"], outs):
            self.bufs[n] = o
        if slot == "chip_y":
            self.pending = None

    def _last(self, sends):
        sg = self.small_g
        stack = lambda key, ls: jnp.stack([sg[l][key] for l in ls])
        every, lru = range(DEPTH), (1, 3)
        small = {"ln_g": stack("ln_g", every), "ln_b": stack("ln_b", every), "lru_conv_w": stack("conv_w", lru),
                 "lru_conv_b": stack("conv_b", lru), "lru_b_ra": stack("b_ra", lru), "lru_b_rx": stack("b_rx", lru),
                 "lru_lambda": stack("lam", lru)}
        tail = jnp.concatenate([stack("sinks", (0, 2)).reshape(-1), self.loss.reshape(1)])
        tail = jnp.pad(tail, (0, SM_ROWS * 1024 - SMALL_N - tail.shape[0]))
        s_sm = jnp.concatenate([_shards_of_full(small[n], shape, axis) for n, shape, axis in SMALL]
                               + [jnp.broadcast_to(tail, (N_DEV, tail.shape[0]))], axis=1).reshape(4, 2, SM_ROWS, 1024)
        gots = run_comm(comm_pair(sends + [s_sm]), "pair_exchange_last")
        parts = self._pair_sums("0lo", sends, gots[:2]) + [pair_sum(s_sm, gots[2], F32, SM_ROWS, "pair_sum_sm")]
        names, comm = self._chip(parts, self._ranges(0, "lo") + [("sm", [(0, SM_ROWS, 0)])], [0, 1, 2])
        for n, o in zip(names, run_comm(comm, "chip_exchange_last")):
            self.bufs[n] = o


def kernel(x, ffn1_w_in, ffn1_w_out, ffn2_w_in, ffn2_w_out, ln_g, ln_b, attn_w_qkv, attn_sinks, attn_w_o, lru_w_in, lru_conv_w, lru_conv_b, lru_w_ra, lru_b_ra, lru_w_rx, lru_b_rx, lru_lambda, lru_w_out, loss_target, m_ffn1_w_in, m_ffn1_w_out, m_ffn2_w_in, m_ffn2_w_out, m_ln_g, m_ln_b, m_attn_w_qkv, m_attn_sinks, m_attn_w_o, m_lru_w_in, m_lru_conv_w, m_lru_conv_b, m_lru_w_ra, m_lru_b_ra, m_lru_w_rx, m_lru_b_rx, m_lru_lambda, m_lru_w_out, v_ffn1_w_in, v_ffn1_w_out, v_ffn2_w_in, v_ffn2_w_out, v_ln_g, v_ln_b, v_attn_w_qkv, v_attn_sinks, v_attn_w_o, v_lru_w_in, v_lru_conv_w, v_lru_conv_b, v_lru_w_ra, v_lru_b_ra, v_lru_w_rx, v_lru_b_rx, v_lru_lambda, v_lru_w_out):
    args = dict(locals())

    sched = _Fsdp(args)
    _, dx = _local_step(x[0], loss_target[0], sched, attn_sinks)
    p_r, p704, p192, p256, p_sm = (sched.bufs[n] for n in ("r", "c704", "c192", "c256", "sm"))
    sm = sched.sm

    fam = lambda n: (args[n], args["m_" + n], args["v_" + n])
    res = {
        "ffn1_w_out": adamw(p_r, 0, *fam("ffn1_w_out"), FF_SHARD, "adamw_ffn1_w_out"),
        "ffn2_w_out": adamw(p_r, R_FFN2, *fam("ffn2_w_out"), FF_SHARD, "adamw_ffn2_w_out"),
        "attn_w_o": adamw(p_r, R_ATTN, *fam("attn_w_o"), 2 * MIX_SHARD, "adamw_attn_w_o"),
        "lru_w_out": adamw(p_r, R_LRU, *fam("lru_w_out"), 2 * MIX_SHARD, "adamw_lru_w_out"),
        "ffn1_w_in": adamw(p704, 0, *fam("ffn1_w_in"), 512, "adamw_ffn1_w_in"),
        "ffn2_w_in": adamw(p704, DEPTH * D_MODEL, *fam("ffn2_w_in"), 512, "adamw_ffn2_w_in"),
        "attn_w_qkv": adamw(p192, 0, *fam("attn_w_qkv"), 512, "adamw_attn_w_qkv"),
        "lru_w_in": adamw(p256, 0, *fam("lru_w_in"), 512, "adamw_lru_w_in"),
        "lru_w_ra": adamw(p256, C256_RA, *fam("lru_w_ra"), 256, "adamw_lru_w_ra"),
        "lru_w_rx": adamw(p256, C256_RX, *fam("lru_w_rx"), 256, "adamw_lru_w_rx"),
    }
    sm_out = adamw(p_sm, 0, sm, _small_pack(args, "m_"), _small_pack(args, "v_"), SM_ROWS, "adamw_small")
    for k, pack in enumerate(sm_out):
        for n, val in _small_unpack(pack).items():
            res.setdefault(n, [None] * 4)[k] = val
    loss_total = sm_out[0].reshape(-1)[LOSS_OFF]
    out = [loss_total, dx[None]]
    for k in range(4):
        out += [res[n][k] for n in WEIGHTS]
    return tuple(out)
```

```python
import math

import jax
import jax.numpy as jnp
import numpy as np
from jax import lax
from jax.experimental import pallas as pl
from jax.experimental.pallas import tpu as pltpu

F32 = jnp.float32
BF16 = jnp.bfloat16

D_MODEL = 1024
DEPTH = 4
N_HEADS = 16
N_KV = 4
HEAD_DIM = 64
GROUP = 4
BLOCK = 128
ROPE_THETA = 10000.0
D_RNN = 1024
RNN_BLOCKS = 4
RNN_W = 256
CONV_W = 4
LRU_C = 8.0
D_FF = 2816
ALPHA = (2.0 * DEPTH) ** 0.25
LN_EPS = 1e-5
QKV = (N_HEADS + 2 * N_KV) * HEAD_DIM
N_DEV = 8

ADAM_LR = 0.001
ADAM_B1 = 0.9
ADAM_B2 = 0.999
ADAM_EPS = 1e-08
ADAM_WD = 0.01
ADAM_STEP = 10

VMEM_LIMIT = 52 * 1024 * 1024
NEG = float(np.finfo(np.float32).min)

MESH = pl.DeviceIdType.MESH
ANY = pl.BlockSpec(memory_space=pl.ANY)

FF_SHARD = D_FF // N_DEV
MIX_SHARD = D_MODEL // N_DEV
R_FFN2 = DEPTH * FF_SHARD
R_ATTN = 2 * DEPTH * FF_SHARD
R_LRU = R_ATTN + 2 * MIX_SHARD
R_ROWS = R_LRU + 2 * MIX_SHARD
C256_RA = 2 * D_MODEL
C256_RX = C256_RA + 2 * RNN_BLOCKS * (RNN_W // N_DEV)
C256_ROWS = C256_RX + 2 * RNN_BLOCKS * (RNN_W // N_DEV)

SMALL = (
    ("ln_g", (4, 3, 128), 2),
    ("ln_b", (4, 3, 128), 2),
    ("lru_conv_w", (2, 4, 128), 2),
    ("lru_conv_b", (2, 128), 1),
    ("lru_b_ra", (2, 128), 1),
    ("lru_b_rx", (2, 128), 1),
    ("lru_lambda", (2, 128), 1),
)
WEIGHTS = ("ffn1_w_in", "ffn1_w_out", "ffn2_w_in", "ffn2_w_out", "ln_g", "ln_b", "attn_w_qkv", "attn_sinks",
           "attn_w_o", "lru_w_in", "lru_conv_w", "lru_conv_b", "lru_w_ra", "lru_b_ra", "lru_w_rx", "lru_b_rx",
           "lru_lambda", "lru_w_out")
SMALL_N = sum(math.prod(s) for _, s, _ in SMALL)
SINK_OFF = SMALL_N
LOSS_OFF = SMALL_N + 32
SM_ROWS = 8


def _cp(*sem):
    return pltpu.CompilerParams(dimension_semantics=sem, vmem_limit_bytes=VMEM_LIMIT)


def _dot(a, b):
    return jnp.dot(a, b, preferred_element_type=F32)


def _dot_tn(a, b):
    return lax.dot_general(a, b, (((0,), (0,)), ((), ())), preferred_element_type=F32)


def _dot_nt(a, b):
    return lax.dot_general(a, b, (((1,), (1,)), ((), ())), preferred_element_type=F32)


def _col_chunks(cols, size=512):
    return [slice(c, min(c + size, cols)) for c in range(0, cols, size)]


def _ln(z, g, b):
    mu = jnp.mean(z, axis=-1, keepdims=True)
    xc = z - mu
    var = jnp.mean(xc * xc, axis=-1, keepdims=True)
    return xc * lax.rsqrt(var + LN_EPS) * g + b


def _rows_view(pack, rows_per_dev, row0):
    return (pack, (N_DEV, rows_per_dev, D_MODEL), (0, row0 // rows_per_dev, 0), (N_DEV * rows_per_dev, D_MODEL))


def _mat_view(arr, cols=None, cblk=0):
    k, n = arr.shape
    cols = n if cols is None else cols
    return (arr, (k, cols), (0, cblk), (k, cols))


def _vspec(view):
    _, bshape, bidx, _ = view
    return pl.BlockSpec(bshape, lambda *_: bidx)


def _vload(view, ref):
    return ref[...].reshape(view[3])


class Comm:
    def __init__(self, ins, out_shapes, sems, start, finish, aliases=None, relay=None):
        self.ins, self.out_shapes, self.sems = list(ins), list(out_shapes), list(sems)
        self.start, self.finish, self.aliases = start, finish, dict(aliases or {})
        self.relay = relay


def _call(body, *, grid, in_specs, out_specs, out_shape, operands, name, sem, scratch=(), comm=None):
    n_in, n_out, n_scr = len(in_specs), len(out_specs), len(scratch)
    if comm is None:
        return pl.pallas_call(body, grid=grid, in_specs=list(in_specs), out_specs=list(out_specs),
                              out_shape=list(out_shape), scratch_shapes=list(scratch), compiler_params=_cp(*sem),
                              name=name)(*operands), []
    nci, nco = len(comm.ins), len(comm.out_shapes)

    def hosted(*refs):
        ins, cins = refs[:n_in], refs[n_in:n_in + nci]
        o0 = n_in + nci
        outs, couts = refs[o0:o0 + n_out], refs[o0 + n_out:o0 + n_out + nco]
        s0 = o0 + n_out + nco
        scr, csems = refs[s0:s0 + n_scr], refs[s0 + n_scr:]
        step = 0
        for ax, size in enumerate(grid):
            step = step * size + pl.program_id(ax)
        steps = math.prod(grid)

        @pl.when(step == 0)
        def _():
            comm.start(cins, couts, csems)

        if comm.relay is not None and steps >= 4:
            @pl.when(step == (3 * steps) // 4)
            def _():
                comm.relay(cins, couts, csems)

        body(*ins, *outs, *scr)

        @pl.when(step == steps - 1)
        def _():
            if comm.relay is not None and steps < 4:
                comm.relay(cins, couts, csems)
            comm.finish(cins, couts, csems)

    res = pl.pallas_call(
        hosted, grid=grid, in_specs=list(in_specs) + [ANY] * nci, out_specs=list(out_specs) + [ANY] * nco,
        out_shape=list(out_shape) + comm.out_shapes, scratch_shapes=list(scratch) + comm.sems,
        input_output_aliases={n_in + i: n_out + o for i, o in comm.aliases.items()},
        compiler_params=_cp(*(("arbitrary",) * len(grid))), name=name)(*operands, *comm.ins)
    return res[:n_out], res[n_out:]


def run_comm(comm, name):
    nci, nco = len(comm.ins), len(comm.out_shapes)

    def body(*refs):
        cins, couts, csems = refs[:nci], refs[nci:nci + nco], refs[nci + nco:]
        comm.start(cins, couts, csems)
        if comm.relay is not None:
            comm.relay(cins, couts, csems)
        comm.finish(cins, couts, csems)

    return pl.pallas_call(
        body, in_specs=[ANY] * nci, out_specs=[ANY] * nco, out_shape=comm.out_shapes, scratch_shapes=comm.sems,
        input_output_aliases=dict(comm.aliases), name=name)(*comm.ins)


def mm_plain(a, wv, out_dtype, name, nt=False, tm=512):
    S, K = a.shape
    N = wv[3][0] if nt else wv[3][1]
    tm = min(tm, S)
    dot = _dot_nt if nt else _dot

    def body(a_ref, w_ref, o_ref):
        o_ref[...] = dot(a_ref[...], _vload(wv, w_ref)).astype(out_dtype)

    return pl.pallas_call(
        body, grid=(S // tm,),
        in_specs=[pl.BlockSpec((tm, K), lambda i: (i, 0)), _vspec(wv)],
        out_specs=pl.BlockSpec((tm, N), lambda i: (i, 0)),
        out_shape=jax.ShapeDtypeStruct((S, N), out_dtype),
        compiler_params=_cp("parallel"), name=name)(a, wv[0])


def mm_res_nt(a, wv, r, alpha, name, tm=512):
    S, K = a.shape
    N = wv[3][0]
    tm = min(tm, S)

    def body(a_ref, w_ref, r_ref, o_ref):
        o_ref[...] = _dot_nt(a_ref[...], _vload(wv, w_ref)) + alpha * r_ref[...]

    return pl.pallas_call(
        body, grid=(S // tm,),
        in_specs=[pl.BlockSpec((tm, K), lambda i: (i, 0)), _vspec(wv), pl.BlockSpec((tm, N), lambda i: (i, 0))],
        out_specs=pl.BlockSpec((tm, N), lambda i: (i, 0)),
        out_shape=jax.ShapeDtypeStruct((S, N), F32),
        compiler_params=_cp("parallel"), name=name)(a, wv[0], r)


def mm_res2_nt(a3, wv0, wv1, r, alpha, name, tm=512, comm=None):
    _, S, K = a3.shape
    N = wv0[3][0]
    tm = min(tm, S)

    def body(a_ref, w0_ref, w1_ref, r_ref, o_ref):
        o_ref[...] = (_dot_nt(a_ref[0], _vload(wv0, w0_ref)) + _dot_nt(a_ref[1], _vload(wv1, w1_ref))
                      + alpha * r_ref[...])

    (out,), couts = _call(
        body, grid=(S // tm,),
        in_specs=[pl.BlockSpec((2, tm, K), lambda i: (0, i, 0)), _vspec(wv0), _vspec(wv1),
                  pl.BlockSpec((tm, N), lambda i: (i, 0))],
        out_specs=[pl.BlockSpec((tm, N), lambda i: (i, 0))],
        out_shape=[jax.ShapeDtypeStruct((S, N), F32)],
        operands=(a3, wv0[0], wv1[0], r), name=name, sem=("parallel",), comm=comm)
    return out, couts


def mm_ln(a, wv, h, g, b, scale, name, tm=512, comm=None):
    S, K = a.shape
    tm = min(tm, S)

    def body(a_ref, w_ref, h_ref, g_ref, b_ref, z_ref, y_ref, yb_ref):
        z = ALPHA * h_ref[...] + scale * _dot(a_ref[...], _vload(wv, w_ref))
        y = _ln(z, g_ref[...], b_ref[...])
        z_ref[...] = z
        y_ref[...] = y
        yb_ref[...] = y.astype(BF16)

    row = pl.BlockSpec((tm, D_MODEL), lambda i: (i, 0))
    vec = pl.BlockSpec((1, D_MODEL), lambda i: (0, 0))
    return _call(
        body, grid=(S // tm,),
        in_specs=[pl.BlockSpec((tm, K), lambda i: (i, 0)), _vspec(wv), row, vec, vec],
        out_specs=[row, row, row],
        out_shape=[jax.ShapeDtypeStruct((S, D_MODEL), F32), jax.ShapeDtypeStruct((S, D_MODEL), F32),
                   jax.ShapeDtypeStruct((S, D_MODEL), BF16)],
        operands=(a, wv[0], h, g.reshape(1, -1), b.reshape(1, -1)), name=name, sem=("parallel",), comm=comm)


def ffn_up(xb, w_in, name, tm=512, tn=1408, comm=None):
    S = xb.shape[0]
    tm = min(tm, S)
    nj = D_FF // tn

    def body(x_ref, wg_ref, wu_ref, jac_ref, a_ref):
        x = x_ref[...]
        g = _dot(x, wg_ref[...])
        u = _dot(x, wu_ref[...])
        sg = jax.nn.sigmoid(g)
        t = g * sg
        jac_ref[0] = (u * (sg * ((g - t) + 1.0))).astype(BF16)
        jac_ref[1] = t.astype(BF16)
        a_ref[...] = (t * u).astype(BF16)

    return _call(
        body, grid=(nj, S // tm),
        in_specs=[pl.BlockSpec((tm, D_MODEL), lambda j, i: (i, 0)),
                  pl.BlockSpec((D_MODEL, tn), lambda j, i: (0, j)),
                  pl.BlockSpec((D_MODEL, tn), lambda j, i: (0, nj + j))],
        out_specs=[pl.BlockSpec((2, tm, tn), lambda j, i: (0, i, j)), pl.BlockSpec((tm, tn), lambda j, i: (i, j))],
        out_shape=[jax.ShapeDtypeStruct((2, S, D_FF), BF16), jax.ShapeDtypeStruct((S, D_FF), BF16)],
        operands=(xb, w_in, w_in), name=name, sem=("parallel", "parallel"), comm=comm)


def ffn_mid_bwd(dfb, r_ff, blk, gu, name, tm=512, comm=None):
    S = dfb.shape[0]
    tm = min(tm, S)
    tn = 4 * FF_SHARD

    def body(df_ref, w_ref, gu_ref, dh_ref):
        w = w_ref[...].reshape(tn, D_MODEL)
        df = df_ref[...]
        for cols in _col_chunks(tn):
            da = _dot_nt(df, w[cols, :])
            dh_ref[0, :, cols] = (da * gu_ref[0, :, cols].astype(F32)).astype(BF16)
            dh_ref[1, :, cols] = (da * gu_ref[1, :, cols].astype(F32)).astype(BF16)

    gspec = pl.BlockSpec((2, tm, tn), lambda j, i: (0, i, j))
    (out,), couts = _call(
        body, grid=(2, S // tm),
        in_specs=[pl.BlockSpec((tm, D_MODEL), lambda j, i: (i, 0)),
                  pl.BlockSpec((4, FF_SHARD, D_MODEL), lambda j, i: (j, blk, 0)), gspec],
        out_specs=[gspec],
        out_shape=[jax.ShapeDtypeStruct((2, S, D_FF), BF16)],
        operands=(dfb, r_ff, gu), name=name, sem=("parallel", "parallel"), comm=comm)
    return out, couts


def mm_tn(a, b, name, tm, tn, ts=2048, planes=1, shard=None, comm=None):
    S, M = a.shape
    N = b.shape[-1] * planes
    ts = min(ts, S)
    per = b.shape[-1] // tn
    ns = S // ts

    if shard is None:
        def body(a_ref, b_ref, o_ref):
            @pl.when(pl.program_id(2) == 0)
            def _():
                o_ref[...] = jnp.zeros_like(o_ref)

            o_ref[...] += _dot_tn(a_ref[...], b_ref[...])

        out_spec = pl.BlockSpec((tm, tn), lambda i, j, s: (i, j))
        out_shape = jax.ShapeDtypeStruct((M, N), F32)
        scratch = ()
    else:
        def body(a_ref, b_ref, o_ref, acc_ref):
            s = pl.program_id(2)

            @pl.when(s == 0)
            def _():
                acc_ref[...] = jnp.zeros_like(acc_ref)

            acc_ref[...] += _dot_tn(a_ref[...], b_ref[...])

            @pl.when(s == ns - 1)
            def _():
                for q in range(tn // shard):
                    o_ref[q] = acc_ref[:, shard * q:shard * (q + 1)]

        out_spec = pl.BlockSpec((tn // shard, tm, shard), lambda i, j, s: (j, i, 0))
        out_shape = jax.ShapeDtypeStruct((N // shard, M, shard), F32)
        scratch = (pltpu.VMEM((tm, tn), F32),)

    if planes == 1:
        bspec = pl.BlockSpec((ts, tn), lambda i, j, s: (s, j))
    else:
        bspec = pl.BlockSpec((None, ts, tn), lambda i, j, s: (j // per, s, j % per))
    (out,), couts = _call(
        body, grid=(M // tm, N // tn, ns),
        in_specs=[pl.BlockSpec((ts, tm), lambda i, j, s: (s, i)), bspec],
        out_specs=[out_spec], out_shape=[out_shape], scratch=scratch,
        operands=(a, b), name=name, sem=("parallel", "parallel", "arbitrary"), comm=comm)
    return out, couts


def ln_bwd(dy, z, g, out_scale, name, tm=512):
    S = dy.shape[0]
    tm = min(tm, S)

    def body(dy_ref, z_ref, g_ref, dz_ref, dzb_ref, dg_ref, db_ref):
        @pl.when(pl.program_id(0) == 0)
        def _():
            dg_ref[...] = jnp.zeros_like(dg_ref)
            db_ref[...] = jnp.zeros_like(db_ref)

        z = z_ref[...]
        dy_ = dy_ref[...]
        mu = jnp.mean(z, axis=-1, keepdims=True)
        xc = z - mu
        var = jnp.mean(xc * xc, axis=-1, keepdims=True)
        rstd = lax.rsqrt(var + LN_EPS)
        xh = xc * rstd
        dxh = dy_ * g_ref[...]
        m1 = jnp.mean(dxh, axis=-1, keepdims=True)
        m2 = jnp.mean(dxh * xh, axis=-1, keepdims=True)
        dz = rstd * (dxh - m1 - xh * m2)
        dz_ref[...] = dz
        dzb_ref[...] = (out_scale * dz).astype(BF16)
        dg_ref[...] += jnp.sum(dy_ * xh, axis=0, keepdims=True)
        db_ref[...] += jnp.sum(dy_, axis=0, keepdims=True)

    row = pl.BlockSpec((tm, D_MODEL), lambda i: (i, 0))
    vec = pl.BlockSpec((1, D_MODEL), lambda i: (0, 0))
    return pl.pallas_call(
        body, grid=(S // tm,),
        in_specs=[row, row, vec],
        out_specs=[row, row, vec, vec],
        out_shape=[jax.ShapeDtypeStruct((S, D_MODEL), F32), jax.ShapeDtypeStruct((S, D_MODEL), BF16),
                   jax.ShapeDtypeStruct((1, D_MODEL), F32), jax.ShapeDtypeStruct((1, D_MODEL), F32)],
        compiler_params=_cp("arbitrary"), name=name)(dy, z, g.reshape(1, -1))


def loss_head(y, t, name, tm=512):
    S = y.shape[0]
    tm = min(tm, S)
    nt = S // tm

    def body(y_ref, t_ref, dy_ref, l_ref):
        i = pl.program_id(0)

        @pl.when(i == 0)
        def _():
            l_ref[...] = jnp.zeros_like(l_ref)

        e = y_ref[...] - t_ref[...]
        dy_ref[...] = e * (1.0 / D_MODEL)
        l_ref[...] += jnp.sum(e * e, axis=0, keepdims=True)

        @pl.when(i == nt - 1)
        def _():
            tot = jnp.sum(l_ref[...], axis=1, keepdims=True) * (0.5 / D_MODEL)
            l_ref[...] = jnp.broadcast_to(tot, l_ref.shape)

    row = pl.BlockSpec((tm, D_MODEL), lambda i: (i, 0))
    vec = pl.BlockSpec((1, D_MODEL), lambda i: (0, 0))
    return pl.pallas_call(
        body, grid=(nt,), in_specs=[row, row], out_specs=[row, vec],
        out_shape=[jax.ShapeDtypeStruct((S, D_MODEL), F32), jax.ShapeDtypeStruct((1, D_MODEL), F32)],
        compiler_params=_cp("arbitrary"), name=name)(y, t)


def _rope_tables(S):
    pos = jnp.arange(S, dtype=F32)
    inv_freq = ROPE_THETA ** (-jnp.arange(0, HEAD_DIM, 2, dtype=F32) / HEAD_DIM)
    ang = pos[:, None] * inv_freq[None, :]
    cos, sin = jnp.cos(ang), jnp.sin(ang)
    cosf = jnp.concatenate([cos, cos, cos, cos], axis=1)
    sinf = jnp.concatenate([-sin, sin, -sin, sin], axis=1)
    return cosf, sinf


def _rot(t, c, s, first):
    sw = jnp.where(first, pltpu.roll(t, 96, 1), pltpu.roll(t, 32, 1))
    return t * c + sw * s


def rope_fwd(qkv, cosf, sinf, name, tm=512):
    S = qkv.shape[0]
    tm = min(tm, S)

    def body(x_ref, c_ref, s_ref, q_ref, k_ref, v_ref):
        c = c_ref[...]
        s = s_ref[...]
        first = (lax.broadcasted_iota(jnp.int32, (tm, 128), 1) % HEAD_DIM) < (HEAD_DIM // 2)
        for j in range(8):
            q_ref[:, 128 * j:128 * (j + 1)] = _rot(x_ref[:, 128 * j:128 * (j + 1)], c, s, first).astype(BF16)
        for j in range(2):
            k_ref[:, 128 * j:128 * (j + 1)] = _rot(x_ref[:, 1024 + 128 * j:1024 + 128 * (j + 1)], c, s, first).astype(BF16)
        v_ref[...] = x_ref[:, 1280:1536].astype(BF16)

    tab = pl.BlockSpec((tm, 128), lambda i: (i, 0))
    return pl.pallas_call(
        body, grid=(S // tm,),
        in_specs=[pl.BlockSpec((tm, QKV), lambda i: (i, 0)), tab, tab],
        out_specs=[pl.BlockSpec((tm, 1024), lambda i: (i, 0)), pl.BlockSpec((tm, 256), lambda i: (i, 0)),
                   pl.BlockSpec((tm, 256), lambda i: (i, 0))],
        out_shape=[jax.ShapeDtypeStruct((S, 1024), BF16), jax.ShapeDtypeStruct((S, 256), BF16),
                   jax.ShapeDtypeStruct((S, 256), BF16)],
        compiler_params=_cp("parallel"), name=name)(qkv, cosf, sinf)


def rope_bwd(dq, dkc, dkp, dvc, dvp, cosf, sinf, name, tq):
    S = dq.shape[0]
    nt = S // tq

    def body(dq_ref, dkc_ref, dkp_ref, dvc_ref, dvp_ref, c_ref, s_ref, o_ref):
        i = pl.program_id(0)
        c = c_ref[...]
        s = -s_ref[...]
        first = (lax.broadcasted_iota(jnp.int32, (tq, 128), 1) % HEAD_DIM) < (HEAD_DIM // 2)
        for j in range(8):
            o_ref[:, 128 * j:128 * (j + 1)] = _rot(dq_ref[:, 128 * j:128 * (j + 1)], c, s, first).astype(BF16)
        has_next = i < nt - 1
        rows = lax.broadcasted_iota(jnp.int32, (tq, 256), 0)
        pad = jnp.zeros((tq - BLOCK, 256), F32)
        halo_k = jnp.concatenate([pad, dkp_ref[...]], axis=0)
        halo_v = jnp.concatenate([pad, dvp_ref[...]], axis=0)
        use = jnp.logical_and(has_next, rows >= tq - BLOCK)
        dk = dkc_ref[...] + jnp.where(use, halo_k, 0.0)
        dv = dvc_ref[...] + jnp.where(use, halo_v, 0.0)
        for j in range(2):
            o_ref[:, 1024 + 128 * j:1024 + 128 * (j + 1)] = _rot(dk[:, 128 * j:128 * (j + 1)], c, s, first).astype(BF16)
        o_ref[:, 1280:1536] = dv.astype(BF16)

    tab = pl.BlockSpec((tq, 128), lambda i: (i, 0))
    cur = pl.BlockSpec((tq, 256), lambda i: (i, 0))
    nxt = pl.BlockSpec((BLOCK, 256), lambda i: (jnp.minimum(i + 1, nt - 1), 0))
    return pl.pallas_call(
        body, grid=(nt,),
        in_specs=[pl.BlockSpec((tq, 1024), lambda i: (i, 0)), cur, nxt, cur, nxt, tab, tab],
        out_specs=pl.BlockSpec((tq, QKV), lambda i: (i, 0)),
        out_shape=jax.ShapeDtypeStruct((S, QKV), BF16),
        compiler_params=_cp("parallel"), name=name)(dq, dkc, dkp, dvc, dvp, cosf, sinf)


def _attn_masks(n):
    shape = (GROUP * BLOCK, 2 * BLOCK)
    r = lax.broadcasted_iota(jnp.int32, shape, 0) % BLOCK
    c = lax.broadcasted_iota(jnp.int32, shape, 1)
    ok = jnp.logical_and(c > r, c <= r + BLOCK)
    return jnp.logical_and(ok, jnp.logical_or(c >= BLOCK, n > 0)), ok


def _attn_probs(qs, kw, sink_col, ok):
    s = jnp.where(ok, _dot_nt(qs, kw) * (HEAD_DIM ** -0.5), NEG)
    m = jnp.maximum(jnp.max(s, axis=1, keepdims=True), sink_col)
    p = jnp.exp(s - m)
    es = jnp.exp(sink_col - m)
    inv = 1.0 / (jnp.sum(p, axis=1, keepdims=True) + es)
    return p * inv, es * inv


def _sink_col(sink_ref, g):
    rid = lax.broadcasted_iota(jnp.int32, (GROUP * BLOCK, 1), 0) // BLOCK
    col = jnp.zeros((GROUP * BLOCK, 1), F32)
    for j in range(GROUP):
        col = jnp.where(rid == j, sink_ref[GROUP * g + j], col)
    return col


KV_PAIR = 2
Q_LANES = KV_PAIR * GROUP * HEAD_DIM


def _low_half(shape):
    return lax.broadcasted_iota(jnp.int32, shape, 1) < HEAD_DIM


def _head_rows(ref, lo, gi):
    low = _low_half((BLOCK, 2 * HEAD_DIM))
    out = []
    for s in range(GROUP // 2):
        c0 = gi * GROUP * HEAD_DIM + 2 * HEAD_DIM * s
        x = ref[lo:lo + BLOCK, c0:c0 + 2 * HEAD_DIM]
        out += [jnp.where(low, x, jnp.zeros_like(x)), jnp.where(low, jnp.zeros_like(x), x)]
    return jnp.concatenate(out, axis=0)


def _head_slabs(x):
    low = _low_half((BLOCK, 2 * HEAD_DIM))
    return [jnp.where(low, x[2 * BLOCK * s:2 * BLOCK * s + BLOCK], x[2 * BLOCK * s + BLOCK:2 * BLOCK * (s + 1)])
            for s in range(GROUP // 2)]


def _kv_window(kc_ref, kp_ref, b, gi):
    if b == 0:
        x = jnp.concatenate([kp_ref[...], kc_ref[0:BLOCK, :]], axis=0)
    else:
        x = kc_ref[BLOCK * (b - 1):BLOCK * (b + 1), :]
    other = pltpu.roll(x, HEAD_DIM, 1)
    low = _low_half(x.shape)
    return jnp.where(low, x, other) if gi == 0 else jnp.where(low, other, x)


def _fold_halves(x, gi):
    tot = x + pltpu.roll(x, HEAD_DIM, 1)
    low = _low_half(x.shape)
    return jnp.where(low if gi == 0 else jnp.logical_not(low), tot, 0.0)


def _attn_specs(tq):
    nsub = tq // BLOCK
    qspec = pl.BlockSpec((tq, Q_LANES), lambda p, n: (n, p))
    cur = pl.BlockSpec((tq, KV_PAIR * HEAD_DIM), lambda p, n: (n, p))
    prev = pl.BlockSpec((BLOCK, KV_PAIR * HEAD_DIM), lambda p, n: (jnp.maximum(n * nsub - 1, 0), p))
    return qspec, cur, prev


def attn_fwd(q, k, v, sinks, name, tq, comm=None):
    S = q.shape[0]
    nsub = tq // BLOCK

    def body(sink_ref, q_ref, kc_ref, kp_ref, vc_ref, vp_ref, o_ref):
        p = pl.program_id(0)
        n = pl.program_id(1)
        ok_first, ok_rest = _attn_masks(n)
        for gi in range(KV_PAIR):
            sink_col = _sink_col(sink_ref, KV_PAIR * p + gi)
            base = gi * GROUP * HEAD_DIM
            for b in range(nsub):
                lo = BLOCK * b
                qs = _head_rows(q_ref, lo, gi)
                pn, _ = _attn_probs(qs, _kv_window(kc_ref, kp_ref, b, gi), sink_col, ok_rest if b else ok_first)
                o = _dot(pn.astype(BF16), _kv_window(vc_ref, vp_ref, b, gi))
                for s, slab in enumerate(_head_slabs(o)):
                    c0 = base + 2 * HEAD_DIM * s
                    o_ref[lo:lo + BLOCK, c0:c0 + 2 * HEAD_DIM] = slab.astype(BF16)

    qspec, cur, prev = _attn_specs(tq)
    (out,), couts = _call(
        body, grid=(N_KV // KV_PAIR, S // tq),
        in_specs=[pl.BlockSpec(memory_space=pltpu.SMEM), qspec, cur, prev, cur, prev],
        out_specs=[qspec],
        out_shape=[jax.ShapeDtypeStruct((S, N_HEADS * HEAD_DIM), BF16)],
        operands=(sinks, q, k, k, v, v), name=name, sem=("parallel", "parallel"), comm=comm)
    return out, couts


def attn_bwd(q, k, v, do, sinks, name, tq):
    S = q.shape[0]
    nsub = tq // BLOCK
    nt = S // tq

    def body(sink_ref, q_ref, kc_ref, kp_ref, vc_ref, vp_ref, do_ref, dq_ref, dkc_ref, dkp_ref, dvc_ref, dvp_ref, ds_ref):
        p = pl.program_id(0)
        n = pl.program_id(1)

        @pl.when(n == 0)
        def _():
            ds_ref[...] = jnp.zeros_like(ds_ref)

        for ref in (dkc_ref, dvc_ref, dkp_ref, dvp_ref):
            ref[...] = jnp.zeros_like(ref)
        rid = lax.broadcasted_iota(jnp.int32, (GROUP * BLOCK, 1), 0) // BLOCK
        sub = lax.broadcasted_iota(jnp.int32, (8, 128), 0)
        ok_first, ok_rest = _attn_masks(n)
        for gi in range(KV_PAIR):
            sink_col = _sink_col(sink_ref, KV_PAIR * p + gi)
            base = gi * GROUP * HEAD_DIM
            dsink = jnp.zeros((8, 128), F32)
            for b in range(nsub):
                lo = BLOCK * b
                qs = _head_rows(q_ref, lo, gi)
                dos = _head_rows(do_ref, lo, gi)
                kw = _kv_window(kc_ref, kp_ref, b, gi)
                vw = _kv_window(vc_ref, vp_ref, b, gi)
                pn, ps = _attn_probs(qs, kw, sink_col, ok_rest if b else ok_first)
                dp = _dot_nt(dos, vw)
                delta = jnp.sum(pn * dp, axis=1, keepdims=True)
                dsb = (pn * (dp - delta) * (HEAD_DIM ** -0.5)).astype(BF16)
                for s, slab in enumerate(_head_slabs(_dot(dsb, kw))):
                    c0 = base + 2 * HEAD_DIM * s
                    dq_ref[lo:lo + BLOCK, c0:c0 + 2 * HEAD_DIM] = slab
                dkw = _fold_halves(_dot_tn(dsb, qs), gi)
                dvw = _fold_halves(_dot_tn(pn.astype(BF16), dos), gi)
                if b == 0:
                    dkp_ref[...] += dkw[0:BLOCK]
                    dvp_ref[...] += dvw[0:BLOCK]
                else:
                    dkc_ref[lo - BLOCK:lo, :] += dkw[0:BLOCK]
                    dvc_ref[lo - BLOCK:lo, :] += dvw[0:BLOCK]
                dkc_ref[lo:lo + BLOCK, :] += dkw[BLOCK:2 * BLOCK]
                dvc_ref[lo:lo + BLOCK, :] += dvw[BLOCK:2 * BLOCK]
                sd = ps * delta
                for j in range(GROUP):
                    tot = -jnp.sum(jnp.where(rid == j, sd, 0.0))
                    dsink = dsink + jnp.where(sub == j, tot, 0.0)
            ds_ref[gi] += dsink

    qspec, cur, prev = _attn_specs(tq)
    halo = pl.BlockSpec((BLOCK, KV_PAIR * HEAD_DIM), lambda p, n: (n, p))
    kv_shape = jax.ShapeDtypeStruct((S, N_KV * HEAD_DIM), F32)
    halo_shape = jax.ShapeDtypeStruct((nt * BLOCK, N_KV * HEAD_DIM), F32)
    return pl.pallas_call(
        body, grid=(N_KV // KV_PAIR, nt),
        in_specs=[pl.BlockSpec(memory_space=pltpu.SMEM), qspec, cur, prev, cur, prev, qspec],
        out_specs=[qspec, cur, halo, cur, halo, pl.BlockSpec((KV_PAIR, 8, 128), lambda p, n: (p, 0, 0))],
        out_shape=[jax.ShapeDtypeStruct((S, N_HEADS * HEAD_DIM), F32), kv_shape, halo_shape, kv_shape, halo_shape,
                   jax.ShapeDtypeStruct((N_KV, 8, 128), F32)],
        compiler_params=_cp("parallel", "arbitrary"), name=name)(sinks, q, k, k, v, v, do)


def _rows_before(cur, prev8, k):
    if k == 0:
        return cur
    n = cur.shape[0]
    ext = jnp.concatenate([prev8, cur], axis=0)
    return ext[8 - k:8 - k + n]


def _rows_after(cur, next8, k):
    if k == 0:
        return cur
    n = cur.shape[0]
    ext = jnp.concatenate([cur, next8], axis=0)
    return ext[k:k + n]


def _gelu(x):
    c = math.sqrt(2.0 / math.pi)
    t = jnp.tanh(c * (x + 0.044715 * (x * x * x)))
    return 0.5 * (1.0 + t), t


def _neg_expm1(u):
    ser = 1.0 + u * (1.0 / 6.0)
    for k in range(5, 1, -1):
        ser = 1.0 + (u * (1.0 / k)) * ser
    return jnp.where(u > -0.125, -(u * ser), 1.0 - jnp.exp(u))


def _block_diag(xb16, w_ref):
    return jnp.concatenate([_dot(xb16[:, RNN_W * n:RNN_W * (n + 1)], w_ref[n]) for n in range(RNN_BLOCKS)], axis=1)


def _lru_gates(xb, prev8, cw_ref, cb_ref, wra_ref, wrx_ref, bra_ref, brx_ref, lsl_ref):
    xc = cb_ref[...] + cw_ref[3:4, :] * xb
    for w in range(CONV_W - 1):
        xc = xc + cw_ref[w:w + 1, :] * _rows_before(xb, prev8, CONV_W - 1 - w)
    xcb = xc.astype(BF16)
    r = jax.nn.sigmoid(_block_diag(xcb, wra_ref) + bra_ref[...])
    ig = jax.nn.sigmoid(_block_diag(xcb, wrx_ref) + brx_ref[...])
    la = LRU_C * r * lsl_ref[...]
    a = jnp.exp(la)
    sq = jnp.sqrt(_neg_expm1(2.0 * la))
    return xc, xcb, r, ig, a, sq


def lru_fwd(xg, p, name, tm=256, comm=None):
    S = xg.shape[0]
    tm = min(tm, S)

    def body(xg_ref, xp_ref, cw_ref, cb_ref, wra_ref, wrx_ref, bra_ref, brx_ref, lsl_ref, y_ref, h_ref, hc_ref, a_s, b_s):
        i = pl.program_id(0)
        xb = xg_ref[:, 0:D_RNN]
        gb = xg_ref[:, D_RNN:2 * D_RNN]
        prev8 = jnp.where(i > 0, xp_ref[:, 0:D_RNN], 0.0)
        xc, _, r, ig, a, sq = _lru_gates(xb, prev8, cw_ref, cb_ref, wra_ref, wrx_ref, bra_ref, brx_ref, lsl_ref)
        a_s[...] = a
        b_s[...] = sq * (ig * xc)

        @pl.when(i == 0)
        def _():
            hc_ref[...] = jnp.zeros_like(hc_ref)

        def chunk(c, h):
            o = pl.multiple_of(c * 8, 8)
            av = a_s[pl.ds(o, 8), :]
            bv = b_s[pl.ds(o, 8), :]
            rows = []
            for t in range(8):
                h = av[t:t + 1, :] * h + bv[t:t + 1, :]
                rows.append(h)
            h_ref[pl.ds(o, 8), :] = jnp.concatenate(rows, axis=0)
            return h

        h_last = lax.fori_loop(0, tm // 8, chunk, hc_ref[0:1, :])
        hc_ref[0:1, :] = h_last
        cdf, _ = _gelu(gb)
        y_ref[...] = (h_ref[...] * (gb * cdf)).astype(BF16)

    vec = pl.BlockSpec((1, D_RNN), lambda i: (0, 0))
    wsp = pl.BlockSpec((RNN_BLOCKS, RNN_W, RNN_W), lambda i: (0, 0, 0))
    return _call(
        body, grid=(S // tm,),
        in_specs=[pl.BlockSpec((tm, 2 * D_RNN), lambda i: (i, 0)),
                  pl.BlockSpec((8, 2 * D_RNN), lambda i: (jnp.maximum(i * (tm // 8) - 1, 0), 0)),
                  pl.BlockSpec((CONV_W, D_RNN), lambda i: (0, 0)), vec, wsp, wsp, vec, vec, vec],
        out_specs=[pl.BlockSpec((tm, D_RNN), lambda i: (i, 0)), pl.BlockSpec((tm, D_RNN), lambda i: (i, 0))],
        out_shape=[jax.ShapeDtypeStruct((S, D_RNN), BF16), jax.ShapeDtypeStruct((S, D_RNN), F32)],
        scratch=[pltpu.VMEM((8, D_RNN), F32), pltpu.VMEM((tm, D_RNN), F32), pltpu.VMEM((tm, D_RNN), F32)],
        operands=(xg, xg, p["conv_w"], p["conv_b"], p["w_ra"], p["w_rx"], p["b_ra"], p["b_rx"], p["lsl"]),
        name=name, sem=("arbitrary",), comm=comm)


def lru_bwd(dy, xg, h, p, name, tm=256):
    S = xg.shape[0]
    tm = min(tm, S)
    nt = S // tm

    def body(dy_ref, xg_ref, xp_ref, h_ref, hp_ref, cw_ref, cb_ref, wra_ref, wrx_ref, bra_ref, brx_ref,
             lsl_ref, dxg_ref, dcw_ref, dcb_ref, dwra_ref, dwrx_ref, dbra_ref, dbrx_ref, dlam_ref,
             lc_ref, nx_ref, a_s, g_s, l_s):
        i = pl.program_id(0)
        ti = nt - 1 - i

        @pl.when(i == 0)
        def _():
            lc_ref[...] = jnp.zeros_like(lc_ref)
            nx_ref[...] = jnp.zeros_like(nx_ref)
            for ref in (dcw_ref, dcb_ref, dwra_ref, dwrx_ref, dbra_ref, dbrx_ref, dlam_ref):
                ref[...] = jnp.zeros_like(ref)

        xb = xg_ref[:, 0:D_RNN]
        gb = xg_ref[:, D_RNN:2 * D_RNN]
        prev8 = jnp.where(ti > 0, xp_ref[:, 0:D_RNN], 0.0)
        xc, xcb, r, ig, a, sq = _lru_gates(xb, prev8, cw_ref, cb_ref, wra_ref, wrx_ref, bra_ref, brx_ref, lsl_ref)
        hh = h_ref[...]
        hprev = _rows_before(hh, jnp.where(ti > 0, hp_ref[...], 0.0), 1)
        dy_ = dy_ref[...]
        cdf, th = _gelu(gb)
        c0 = math.sqrt(2.0 / math.pi)
        dgate = cdf + gb * (0.5 * (1.0 - th * th) * c0 * (1.0 + 3.0 * 0.044715 * gb * gb))
        dgb = dy_ * hh * dgate
        a_s[...] = a
        g_s[...] = dy_ * (gb * cdf)

        def chunk(cc, carry):
            o = pl.multiple_of((tm // 8 - 1 - cc) * 8, 8)
            av = a_s[pl.ds(o, 8), :]
            gv = g_s[pl.ds(o, 8), :]
            rows = [None] * 8
            for t in range(7, -1, -1):
                lam_t = gv[t:t + 1, :] + carry
                rows[t] = lam_t
                carry = av[t:t + 1, :] * lam_t
            l_s[pl.ds(o, 8), :] = jnp.concatenate(rows, axis=0)
            return carry

        carry = lax.fori_loop(0, tm // 8, chunk, lc_ref[0:1, :])
        lc_ref[0:1, :] = carry
        lam = l_s[...]
        da = lam * hprev
        dixc = lam * sq
        di = dixc * xc
        dxc = dixc * ig
        dsq = lam * (ig * xc)
        dla = da * a - dsq * (a * a / sq)
        dr = dla * (LRU_C * lsl_ref[...])
        dlam_ref[...] += jnp.sum(dla * (LRU_C * r), axis=0, keepdims=True)
        dpr = dr * r * (1.0 - r)
        dpi = di * ig * (1.0 - ig)
        dbra_ref[...] += jnp.sum(dpr, axis=0, keepdims=True)
        dbrx_ref[...] += jnp.sum(dpi, axis=0, keepdims=True)
        dprb = dpr.astype(BF16)
        dpib = dpi.astype(BF16)
        back = []
        for n in range(RNN_BLOCKS):
            sl = slice(RNN_W * n, RNN_W * (n + 1))
            dwra_ref[n] += _dot_tn(xcb[:, sl], dprb[:, sl])
            dwrx_ref[n] += _dot_tn(xcb[:, sl], dpib[:, sl])
            back.append(_dot_nt(dprb[:, sl], wra_ref[n]) + _dot_nt(dpib[:, sl], wrx_ref[n]))
        dxc = dxc + jnp.concatenate(back, axis=1)
        dcb_ref[...] += jnp.sum(dxc, axis=0, keepdims=True)
        next8 = nx_ref[...]
        dxb = cw_ref[3:4, :] * dxc
        dcw_ref[3:4, :] += jnp.sum(dxc * xb, axis=0, keepdims=True)
        for w in range(CONV_W - 1):
            k = CONV_W - 1 - w
            dcw_ref[w:w + 1, :] += jnp.sum(dxc * _rows_before(xb, prev8, k), axis=0, keepdims=True)
            dxb = dxb + cw_ref[w:w + 1, :] * _rows_after(dxc, next8, k)
        nx_ref[...] = dxc[0:8, :]
        dxg_ref[:, 0:D_RNN] = dxb.astype(BF16)
        dxg_ref[:, D_RNN:2 * D_RNN] = dgb.astype(BF16)

    rev = lambda i: (nt - 1 - i, 0)
    before = lambda i: (jnp.maximum((nt - 1 - i) * (tm // 8) - 1, 0), 0)
    vec = pl.BlockSpec((1, D_RNN), lambda i: (0, 0))
    wsp = pl.BlockSpec((RNN_BLOCKS, RNN_W, RNN_W), lambda i: (0, 0, 0))
    cwsp = pl.BlockSpec((CONV_W, D_RNN), lambda i: (0, 0))
    return pl.pallas_call(
        body, grid=(nt,),
        in_specs=[pl.BlockSpec((tm, D_RNN), rev), pl.BlockSpec((tm, 2 * D_RNN), rev), pl.BlockSpec((8, 2 * D_RNN), before),
                  pl.BlockSpec((tm, D_RNN), rev), pl.BlockSpec((8, D_RNN), before),
                  cwsp, vec, wsp, wsp, vec, vec, vec],
        out_specs=[pl.BlockSpec((tm, 2 * D_RNN), rev), cwsp, vec, wsp, wsp, vec, vec, vec],
        out_shape=[jax.ShapeDtypeStruct((S, 2 * D_RNN), BF16), jax.ShapeDtypeStruct((CONV_W, D_RNN), F32),
                   jax.ShapeDtypeStruct((1, D_RNN), F32), jax.ShapeDtypeStruct((RNN_BLOCKS, RNN_W, RNN_W), F32),
                   jax.ShapeDtypeStruct((RNN_BLOCKS, RNN_W, RNN_W), F32), jax.ShapeDtypeStruct((1, D_RNN), F32),
                   jax.ShapeDtypeStruct((1, D_RNN), F32), jax.ShapeDtypeStruct((1, D_RNN), F32)],
        scratch_shapes=[pltpu.VMEM((8, D_RNN), F32), pltpu.VMEM((8, D_RNN), F32), pltpu.VMEM((tm, D_RNN), F32),
                        pltpu.VMEM((tm, D_RNN), F32), pltpu.VMEM((tm, D_RNN), F32)],
        compiler_params=_cp("arbitrary"), name=name)(
            dy, xg, xg, h, h, p["conv_w"], p["conv_b"], p["w_ra"], p["w_rx"], p["b_ra"], p["b_rx"], p["lsl"])


def _place():
    x, y, c = lax.axis_index("x"), lax.axis_index("y"), lax.axis_index("c")
    chips = [(1 - x, y), (x, 1 - y), (1 - x, 1 - y)]
    return x, y, c, chips


def comm_ag(arrs):
    n = len(arrs)

    def copies(ins, outs, sems):
        send_sems, recv_sems, local_sems = sems
        x, y, c, chips = _place()
        me, sibling = (x, y, c), (x, y, 1 - c)

        def rows(a, px, py, pc):
            return outs[a].at[4 * px + 2 * py + pc]

        def copy(a, k, block, to, own=False):
            return pltpu.make_async_remote_copy(
                src_ref=ins[a] if own else rows(a, *block), dst_ref=rows(a, *block),
                send_sem=send_sems.at[k, a], recv_sem=recv_sems.at[k, a], device_id=to, device_id_type=MESH)

        mine = [pltpu.make_async_copy(ins[a], rows(a, *me), local_sems.at[a]) for a in range(n)]
        first = [copy(a, 1 + j, me, (*chip, c), own=True) for j, chip in enumerate(chips) for a in range(n)]
        first += [copy(a, 0, me, sibling, own=True) for a in range(n)]
        return copy, chips, c, me, sibling, mine, first

    def start(ins, outs, sems):
        _, _, _, _, _, mine, first = copies(ins, outs, sems)
        for cp in mine + first:
            cp.start()

    def relay(ins, outs, sems):
        copy, chips, c, me, sibling, _, _ = copies(ins, outs, sems)
        for j, chip in enumerate(chips):
            for a in range(n):
                copy(a, 1 + j, (*chip, c), me).wait_recv()
                copy(a, 4 + j, (*chip, c), sibling).start()

    def finish(ins, outs, sems):
        copy, chips, c, me, sibling, mine, first = copies(ins, outs, sems)
        passed = [copy(a, 4 + j, (*chip, c), sibling) for j, chip in enumerate(chips) for a in range(n)]
        for a in range(n):
            copy(a, 0, sibling, me).wait_recv()
        for j, chip in enumerate(chips):
            for a in range(n):
                copy(a, 4 + j, (*chip, 1 - c), me).wait_recv()
        for cp in first + passed:
            cp.wait_send()
        for cp in mine:
            cp.wait()

    return Comm(arrs, [jax.ShapeDtypeStruct((N_DEV,) + p.shape, p.dtype) for p in arrs],
                [pltpu.SemaphoreType.DMA((7, n)), pltpu.SemaphoreType.DMA((7, n)), pltpu.SemaphoreType.DMA((n,))],
                start, finish, relay=relay)


def comm_pair(sends):
    n = len(sends)

    def copies(ins, outs, sems):
        send_sems, recv_sems = sems
        x, y, c, _ = _place()
        return [pltpu.make_async_remote_copy(
            src_ref=ins[a].at[k, 1 - c], dst_ref=outs[a].at[k], send_sem=send_sems.at[k, a], recv_sem=recv_sems.at[k, a],
            device_id=(x, y, 1 - c), device_id_type=MESH) for k in range(4) for a in range(n)]

    def start(ins, outs, sems):
        for cp in copies(ins, outs, sems):
            cp.start()

    def finish(ins, outs, sems):
        for cp in copies(ins, outs, sems):
            cp.wait()

    return Comm(sends, [jax.ShapeDtypeStruct((4,) + s.shape[2:], s.dtype) for s in sends],
                [pltpu.SemaphoreType.DMA((4, n)), pltpu.SemaphoreType.DMA((4, n))], start, finish)


def pair_sum(sends, gots, out_dtypes, name, steps=2):
    n = len(sends)
    c = lax.axis_index("c").astype(jnp.int32).reshape(1)
    steps = [steps if s.shape[2] % (16 * steps) == 0 else 1 for s in sends]

    def body(c_ref, *refs):
        for a in range(n):
            refs[2 * n + a][...] = (refs[a][...] + refs[n + a][...]).astype(out_dtypes[a])

    def specs(a, own):
        _, _, R, C = sends[a].shape
        tr, last = R // steps[a], steps[a] - 1
        if own:
            return pl.BlockSpec((None, None, tr, C), lambda k, i, cr: (k, cr[0], jnp.minimum(i, last), 0))
        return pl.BlockSpec((None, tr, C), lambda k, i, cr: (k, jnp.minimum(i, last), 0))

    return pl.pallas_call(
        body,
        grid_spec=pltpu.PrefetchScalarGridSpec(
            num_scalar_prefetch=1, grid=(4, max(steps)),
            in_specs=[specs(a, True) for a in range(n)] + [specs(a, False) for a in range(n)],
            out_specs=[specs(a, False) for a in range(n)]),
        out_shape=[jax.ShapeDtypeStruct((4,) + s.shape[2:], d) for s, d in zip(sends, out_dtypes)],
        compiler_params=_cp("parallel", "arbitrary"), name=name)(c, *sends, *gots)


def comm_chip(items, bufs):
    ns = len(items)
    segs = [(i, b, s0, nr, d0) for i, (_, b, ranges) in enumerate(items) for (s0, nr, d0) in ranges]

    def copies(ins, outs, sems):
        send_sems, recv_sems, local_sems = sems
        x, y, c, chips = _place()
        mychip = 2 * x + y
        mine = [pltpu.make_async_copy(ins[i].at[mychip, pl.ds(s0, nr)], outs[b].at[mychip, pl.ds(d0, nr)], local_sems.at[q])
                for q, (i, b, s0, nr, d0) in enumerate(segs)]
        remote = [pltpu.make_async_remote_copy(
            src_ref=ins[i].at[2 * px + py, pl.ds(s0, nr)], dst_ref=outs[b].at[mychip, pl.ds(d0, nr)],
            send_sem=send_sems.at[j, q], recv_sem=recv_sems.at[j, q], device_id=(px, py, c), device_id_type=MESH)
            for j, (px, py) in enumerate(chips) for q, (i, b, s0, nr, d0) in enumerate(segs)]
        return mine, remote

    def start(ins, outs, sems):
        mine, remote = copies(ins, outs, sems)
        for cp in mine + remote:
            cp.start()

    def finish(ins, outs, sems):
        mine, remote = copies(ins, outs, sems)
        for cp in remote + mine:
            cp.wait()

    q = len(segs)
    return Comm([it[0] for it in items] + list(bufs), [jax.ShapeDtypeStruct(b.shape, b.dtype) for b in bufs],
                [pltpu.SemaphoreType.DMA((3, q)), pltpu.SemaphoreType.DMA((3, q)), pltpu.SemaphoreType.DMA((q,))],
                start, finish, aliases={ns + b: b for b in range(len(bufs))})


def adamw(parts, row0, w, m, v, tr, name):
    C = w.shape[-1]
    rows = w.size // C
    off = row0 // tr

    def body(p_ref, w_ref, m_ref, v_ref, g_ref, d_ref, nm_ref, nv_ref):
        g = ((p_ref[0].astype(F32) + p_ref[1].astype(F32)) + p_ref[2].astype(F32)) + p_ref[3].astype(F32)
        m2 = ADAM_B1 * m_ref[...] + (1.0 - ADAM_B1) * g
        v2 = ADAM_B2 * v_ref[...] + (1.0 - ADAM_B2) * (g * g)
        mh = m2 / (1.0 - ADAM_B1 ** ADAM_STEP)
        vh = v2 / (1.0 - ADAM_B2 ** ADAM_STEP)
        g_ref[...] = g
        d_ref[...] = -ADAM_LR * (mh / (jnp.sqrt(vh) + ADAM_EPS) + ADAM_WD * w_ref[...])
        nm_ref[...] = m2
        nv_ref[...] = v2

    row = pl.BlockSpec((tr, C), lambda i: (i, 0))
    shp = jax.ShapeDtypeStruct((rows, C), F32)
    outs = pl.pallas_call(
        body, grid=(rows // tr,),
        in_specs=[pl.BlockSpec((4, tr, C), lambda i: (0, off + i, 0)), row, row, row],
        out_specs=[row, row, row, row], out_shape=[shp, shp, shp, shp],
        compiler_params=_cp("parallel"), name=name)(parts, w.reshape(rows, C), m.reshape(rows, C), v.reshape(rows, C))
    return [o.reshape(w.shape) for o in outs]


def _full_from_gathered(flat, shape, axis):
    t = jnp.moveaxis(flat.reshape((N_DEV,) + shape), 0, axis)
    return t.reshape(shape[:axis] + (N_DEV * shape[axis],) + shape[axis + 1:])


def _shards_of_full(full, shape, axis):
    t = full.reshape(shape[:axis] + (N_DEV, shape[axis]) + shape[axis + 1:])
    return jnp.moveaxis(t, axis, 0).reshape(N_DEV, -1)


def _small_pack(args, prefix):
    flat = jnp.concatenate([args[prefix + n].reshape(-1) for n, _, _ in SMALL] + [args[prefix + "attn_sinks"].reshape(-1)])
    return jnp.pad(flat, (0, SM_ROWS * 1024 - flat.shape[0])).reshape(SM_ROWS, 1024)


def _small_unpack(pack):
    flat = pack.reshape(-1)
    out, off = {}, 0
    for n, shape, _ in SMALL:
        size = math.prod(shape)
        out[n] = flat[off:off + size].reshape(shape)
        off += size
    out["attn_sinks"] = flat[SINK_OFF:SINK_OFF + 32].reshape(2, 16)
    return out


def _tn_tile(n):
    return next(t for t in (1408, 1024, 768, 512, 256, 128) if n % t == 0)


def _dw(a, b, name, planes=1, shard=None, comm=None):
    return mm_tn(a, b, name, _tn_tile(a.shape[1]), _tn_tile(b.shape[-1]), planes=planes, shard=shard, comm=comm)


class _NoExchange:
    def __init__(self, layers):
        self.layers, self.grads = layers, [{} for _ in range(DEPTH)]

    def weights(self, l, piece):
        return self.layers[l]

    def fwd_slot(self):
        return None

    def fwd_done(self, outs):
        pass

    def loss_ready(self, loss):
        pass

    def bwd_slot(self, slot):
        return None

    def bwd_done(self, slot, outs):
        pass

    def grads_ready(self, l, part, g):
        self.grads[l].update(g)


def _local_step(x, target, sched, sinks):
    S = x.shape[0]
    tq = min(512, S)
    cosf, sinf = _rope_tables(S)
    saved = []
    h, hb = x, x.astype(BF16)
    for l in range(DEPTH):
        j = l // 2
        wl = {}
        sv = {"h0b": hb, "wl": wl}
        wl.update(sched.weights(l, "a"))
        (sv["gu1"], a1), co = ffn_up(hb, wl["w_in1"], f"ffn1_up_{l}", comm=sched.fwd_slot())
        sched.fwd_done(co)
        sv["a1"] = a1
        wl.update(sched.weights(l, "b"))
        (sv["z1"], h, hb), _ = mm_ln(a1, _rows_view(wl["r_ff"], FF_SHARD, 0), h, wl["ln_g"][0], wl["ln_b"][0], 0.5,
                                     f"ffn1_down_ln_{l}")
        sv["h1b"] = hb
        if l % 2 == 0:
            qkv = mm_plain(hb, _mat_view(wl["m_c"]), F32, f"attn_qkv_{l}")
            qr, kr, vv = rope_fwd(qkv, cosf, sinf, f"rope_{l}")
            mix_in, co = attn_fwd(qr, kr, vv, sinks[j], f"attn_core_{l}", tq, comm=sched.fwd_slot())
            sv.update(qr=qr, kr=kr, vv=vv)
        else:
            xg = mm_plain(hb, _mat_view(wl["m_c"]), F32, f"lru_in_{l}")
            (mix_in, hstate), co = lru_fwd(xg, wl["lru"], f"lru_core_{l}", comm=sched.fwd_slot())
            sv.update(xg=xg, hstate=hstate)
        sched.fwd_done(co)
        sv["mix_in"] = mix_in
        (sv["z2"], h, hb), _ = mm_ln(mix_in, _rows_view(wl["m_o"], MIX_SHARD, 0), h, wl["ln_g"][1], wl["ln_b"][1], 1.0,
                                     f"mix_out_ln_{l}")
        sv["h2b"] = hb
        wl.update(sched.weights(l, "c"))
        (sv["gu2"], a2), co = ffn_up(hb, wl["w_in2"], f"ffn2_up_{l}", comm=sched.fwd_slot())
        sched.fwd_done(co)
        sv["a2"] = a2
        (sv["z3"], h, hb), _ = mm_ln(a2, _rows_view(wl["r_ff"], FF_SHARD, FF_SHARD), h, wl["ln_g"][2], wl["ln_b"][2], 0.5,
                                     f"ffn2_down_ln_{l}")
        saved.append(sv)

    dy, lvec = loss_head(h, target, "loss_head")
    loss = lvec[0, 0]
    sched.loss_ready(loss)

    def hosted(slot, on, fn):
        comm = sched.bwd_slot(slot) if on else None
        out, co = fn(comm)
        if comm is not None:
            sched.bwd_done(slot, co)
        return out

    def ffn_bwd(dy, z, g, gu, a, xin_b, r_ff, blk, w_in, tag, host):
        dz, dzb, dg, db = ln_bwd(dy, z, g, 0.5, f"ln_bwd_{tag}")
        dh, _ = ffn_mid_bwd(dzb, r_ff, blk, gu, f"ffn_mid_bwd_{tag}")
        d_wout = hosted("pair", host, lambda cm: _dw(a, dzb, f"dw_out_{tag}", comm=cm))
        d_win = hosted("chip_x", host, lambda cm: _dw(xin_b, dh, f"dw_in_{tag}", planes=2, shard=2 * D_FF // N_DEV, comm=cm))
        dx = hosted("chip_y", host, lambda cm: mm_res2_nt(dh, _mat_view(w_in, D_FF, 0), _mat_view(w_in, D_FF, 1), dz, ALPHA,
                                                          f"ffn_dx_{tag}", comm=cm))
        return dx, dg, db, d_wout, d_win

    for l in reversed(range(DEPTH)):
        j = l // 2
        sv = saved[l]
        wl = sv["wl"]
        gl = {}
        dg, db = [None] * 3, [None] * 3
        dy, dg[2], db[2], gl["w_out2"], gl["w_in2"] = ffn_bwd(
            dy, sv["z3"], wl["ln_g"][2], sv["gu2"], sv["a2"], sv["h2b"], wl["r_ff"], 1, wl["w_in2"], f"2_{l}", True)
        dz, dzb, dg[1], db[1] = ln_bwd(dy, sv["z2"], wl["ln_g"][1], 1.0, f"ln_bwd_mix_{l}")
        w_mix = _rows_view(wl["m_o"], MIX_SHARD, 0)
        if l % 2 == 0:
            gl["mix_out"], _ = _dw(sv["mix_in"], dzb, f"dw_o_{l}")
            do = mm_plain(dzb, w_mix, BF16, f"attn_do_{l}", nt=True)
            dq, dkc, dkp, dvc, dvp, dsk = attn_bwd(sv["qr"], sv["kr"], sv["vv"], do, sinks[j], f"attn_core_bwd_{l}", tq)
            gl["sinks"] = dsk[:, :GROUP, 0].reshape(N_HEADS)
            dmid = rope_bwd(dq, dkc, dkp, dvc, dvp, cosf, sinf, f"rope_bwd_{l}", tq)
            gl["mix_in"], _ = _dw(sv["h1b"], dmid, f"dw_qkv_{l}", shard=QKV // N_DEV)
        else:
            gl["mix_out"], _ = _dw(sv["mix_in"], dzb, f"dw_lru_out_{l}")
            dyl = mm_plain(dzb, w_mix, F32, f"lru_dy_{l}", nt=True)
            dmid, dcw, dcb, dwra, dwrx, dbra, dbrx, dlam = lru_bwd(dyl, sv["xg"], sv["hstate"], wl["lru"], f"lru_core_bwd_{l}")
            gl.update(conv_w=dcw, conv_b=dcb[0], w_ra=dwra, w_rx=dwrx, b_ra=dbra[0], b_rx=dbrx[0],
                      lam=dlam[0] * wl["lru"]["sig_neg"])
            gl["mix_in"], _ = _dw(sv["h1b"], dmid, f"dw_lru_in_{l}", shard=2 * D_RNN // N_DEV)
        dy = mm_res_nt(dmid, _mat_view(wl["m_c"]), dz, ALPHA, f"mix_dx_{l}")
        sched.grads_ready(l, "hi", gl)
        lo = {}
        dy, dg[0], db[0], lo["w_out1"], lo["w_in1"] = ffn_bwd(
            dy, sv["z1"], wl["ln_g"][0], sv["gu1"], sv["a1"], sv["h0b"], wl["r_ff"], 0, wl["w_in1"], f"1_{l}", True)
        lo["ln_g"], lo["ln_b"] = jnp.concatenate(dg, axis=0), jnp.concatenate(db, axis=0)
        sched.grads_ready(l, "lo", lo)
    return loss, dy


def _lru_params(full, j, w_ra, w_rx):
    lam = full["lru_lambda"][j]
    return {
        "conv_w": full["lru_conv_w"][j], "conv_b": full["lru_conv_b"][j].reshape(1, -1),
        "w_ra": w_ra, "w_rx": w_rx,
        "b_ra": full["lru_b_ra"][j].reshape(1, -1), "b_rx": full["lru_b_rx"][j].reshape(1, -1),
        "lsl": jax.nn.log_sigmoid(lam).reshape(1, -1), "sig_neg": jax.nn.sigmoid(-lam),
    }


def _row_shards(f):
    return f.reshape(N_DEV, -1, f.shape[1])


def _gate_shards(w):
    return w.reshape(RNN_BLOCKS, N_DEV, RNN_W // N_DEV, RNN_W).transpose(1, 0, 2, 3).reshape(N_DEV, -1, RNN_W)


class _Fsdp:
    def __init__(self, args):
        self.args = args
        self.b16 = lambda a: a.astype(BF16)
        self.sm = _small_pack(args, "")
        self.sent = {l: self._ag_arrays(l) for l in range(DEPTH)}
        self.queue = [(l, p) for l in range(DEPTH) for p in "abc"][1:]
        self.hosting = None
        got = run_comm(comm_ag(self.sent[0]["a"] + [self.sm]), "all_gather_first")
        self.raw = {0: {"a": got[:-1]}}
        gflat = got[-1].reshape(N_DEV, SM_ROWS * 1024)
        self.small, off = {}, 0
        for n, shape, axis in SMALL:
            size = math.prod(shape)
            self.small[n] = _full_from_gathered(gflat[:, off:off + size], shape, axis)
            off += size
        self.bufs = {"r": jnp.zeros((4, R_ROWS, D_MODEL), BF16), "c704": jnp.zeros((4, 2 * DEPTH * D_MODEL, 704), BF16),
                     "c192": jnp.zeros((4, 2 * D_MODEL, 192), BF16), "c256": jnp.zeros((4, C256_ROWS, RNN_W), BF16),
                     "sm": jnp.zeros((4, SM_ROWS, 1024), F32)}
        self.pending = None
        self.small_g = [{} for _ in range(DEPTH)]
        self.loss = None

    def _ag_arrays(self, l):
        a, b16, j = self.args, self.b16, l // 2
        r_ff = jnp.concatenate([b16(a["ffn1_w_out"][l]), b16(a["ffn2_w_out"][l])], axis=0)
        if l % 2 == 0:
            return {"a": [b16(a["ffn1_w_in"][l]), b16(a["attn_w_qkv"][j])], "b": [r_ff, b16(a["attn_w_o"][j])],
                    "c": [b16(a["ffn2_w_in"][l])]}
        gates = jnp.concatenate([b16(a["lru_w_ra"][j]).reshape(-1, RNN_W), b16(a["lru_w_rx"][j]).reshape(-1, RNN_W)], axis=0)
        return {"a": [b16(a["ffn1_w_in"][l]), b16(a["lru_w_in"][j])], "b": [r_ff, b16(a["lru_w_out"][j]), gates],
                "c": [b16(a["ffn2_w_in"][l])]}

    def weights(self, l, piece):
        raw, j = self.raw[l][piece], l // 2
        full = lambda g: g.transpose(1, 0, 2).reshape(g.shape[1], -1)
        if piece == "a":
            return {"w_in1": full(raw[0]), "m_c": full(raw[1]), "ln_g": self.small["ln_g"][l], "ln_b": self.small["ln_b"][l]}
        if piece == "c":
            return {"w_in2": full(raw[0])}
        wl = {"r_ff": raw[0], "m_o": raw[1]}
        if l % 2:
            g = raw[2].reshape(N_DEV, 2, RNN_BLOCKS, RNN_W // N_DEV, RNN_W).transpose(1, 2, 0, 3, 4)
            g = g.reshape(2, RNN_BLOCKS, RNN_W, RNN_W)
            wl["lru"] = _lru_params(self.small, j, g[0], g[1])
        return wl

    def fwd_slot(self):
        if not self.queue:
            return None
        self.hosting = self.queue.pop(0)
        l, piece = self.hosting
        return comm_ag(self.sent[l][piece])

    def fwd_done(self, outs):
        if self.hosting is not None:
            l, piece = self.hosting
            self.raw.setdefault(l, {})[piece] = outs
            self.hosting = None

    def loss_ready(self, loss):
        self.loss = loss

    def _ranges(self, l, part):
        j = l // 2
        if part == "lo":
            return [("r", [(0, FF_SHARD, l * FF_SHARD)]), ("c704", [(0, D_MODEL, l * D_MODEL)])]
        r = [(0, FF_SHARD, R_FFN2 + l * FF_SHARD), (FF_SHARD, MIX_SHARD, (R_LRU if l % 2 else R_ATTN) + j * MIX_SHARD)]
        c704 = [(0, D_MODEL, (DEPTH + l) * D_MODEL)]
        if l % 2:
            gr = RNN_BLOCKS * (RNN_W // N_DEV)
            mix = ("c256", [(0, D_MODEL, j * D_MODEL), (D_MODEL, gr, C256_RA + j * gr), (D_MODEL + gr, gr, C256_RX + j * gr)])
        else:
            mix = ("c192", [(0, D_MODEL, j * D_MODEL)])
        return [("r", r), ("c704", c704), mix]

    def grads_ready(self, l, part, g):
        self.small_g[l].update(g)
        if part == "lo":
            ts = [_row_shards(g["w_out1"]), g["w_in1"]]
        else:
            tmix = g["mix_in"]
            if l % 2:
                tmix = jnp.concatenate([tmix, _gate_shards(g["w_ra"]), _gate_shards(g["w_rx"])], axis=1)
            ts = [jnp.concatenate([_row_shards(g["w_out2"]), _row_shards(g["mix_out"])], axis=1), g["w_in2"], tmix]
        sends = [t.reshape((4, 2) + t.shape[1:]) for t in ts]
        if (l, part) == (0, "lo"):
            self._last(sends)
        else:
            self.pending = {"tag": f"{l}{part}", "sends": sends, "where": self._ranges(l, part)}

    def _pair_sums(self, tag, sends, gots):
        return pair_sum(sends, gots, [BF16] * len(sends), f"pair_sum_{tag}")

    def _chip(self, parts, where, pick):
        names = [where[i][0] for i in pick]
        return names, comm_chip([(parts[i], k, where[i][1]) for k, i in enumerate(pick)], [self.bufs[n] for n in names])

    def bwd_slot(self, slot):
        p = self.pending
        if p is None:
            return None
        if slot == "pair":
            return comm_pair(p["sends"])
        pick = [1] if slot == "chip_x" else [i for i in range(len(p["sends"])) if i != 1]
        p["names"], comm = self._chip(p["parts"], p["where"], pick)
        return comm

    def bwd_done(self, slot, outs):
        p = self.pending
        if slot == "pair":
            p["parts"] = self._pair_sums(p["tag"], p["sends"], outs)
            return
        for n, o in zip(p["names"], outs):
            self.bufs[n] = o
        if slot == "chip_y":
            self.pending = None

    def _last(self, sends):
        sg = self.small_g
        stack = lambda key, ls: jnp.stack([sg[l][key] for l in ls])
        every, lru = range(DEPTH), (1, 3)
        small = {"ln_g": stack("ln_g", every), "ln_b": stack("ln_b", every), "lru_conv_w": stack("conv_w", lru),
                 "lru_conv_b": stack("conv_b", lru), "lru_b_ra": stack("b_ra", lru), "lru_b_rx": stack("b_rx", lru),
                 "lru_lambda": stack("lam", lru)}
        tail = jnp.concatenate([stack("sinks", (0, 2)).reshape(-1), self.loss.reshape(1)])
        tail = jnp.pad(tail, (0, SM_ROWS * 1024 - SMALL_N - tail.shape[0]))
        s_sm = jnp.concatenate([_shards_of_full(small[n], shape, axis) for n, shape, axis in SMALL]
                               + [jnp.broadcast_to(tail, (N_DEV, tail.shape[0]))], axis=1).reshape(4, 2, SM_ROWS, 1024)
        gots = run_comm(comm_pair(sends + [s_sm]), "pair_exchange_last")
        parts = pair_sum(sends + [s_sm], gots, [BF16, BF16, F32], "pair_sum_last")
        names, comm = self._chip(parts, self._ranges(0, "lo") + [("sm", [(0, SM_ROWS, 0)])], [0, 1, 2])
        for n, o in zip(names, run_comm(comm, "chip_exchange_last")):
            self.bufs[n] = o


def kernel(x, ffn1_w_in, ffn1_w_out, ffn2_w_in, ffn2_w_out, ln_g, ln_b, attn_w_qkv, attn_sinks, attn_w_o, lru_w_in, lru_conv_w, lru_conv_b, lru_w_ra, lru_b_ra, lru_w_rx, lru_b_rx, lru_lambda, lru_w_out, loss_target, m_ffn1_w_in, m_ffn1_w_out, m_ffn2_w_in, m_ffn2_w_out, m_ln_g, m_ln_b, m_attn_w_qkv, m_attn_sinks, m_attn_w_o, m_lru_w_in, m_lru_conv_w, m_lru_conv_b, m_lru_w_ra, m_lru_b_ra, m_lru_w_rx, m_lru_b_rx, m_lru_lambda, m_lru_w_out, v_ffn1_w_in, v_ffn1_w_out, v_ffn2_w_in, v_ffn2_w_out, v_ln_g, v_ln_b, v_attn_w_qkv, v_attn_sinks, v_attn_w_o, v_lru_w_in, v_lru_conv_w, v_lru_conv_b, v_lru_w_ra, v_lru_b_ra, v_lru_w_rx, v_lru_b_rx, v_lru_lambda, v_lru_w_out):
    args = dict(locals())

    sched = _Fsdp(args)
    _, dx = _local_step(x[0], loss_target[0], sched, attn_sinks)
    p_r, p704, p192, p256, p_sm = (sched.bufs[n] for n in ("r", "c704", "c192", "c256", "sm"))
    sm = sched.sm

    fam = lambda n: (args[n], args["m_" + n], args["v_" + n])
    res = {
        "ffn1_w_out": adamw(p_r, 0, *fam("ffn1_w_out"), FF_SHARD, "adamw_ffn1_w_out"),
        "ffn2_w_out": adamw(p_r, R_FFN2, *fam("ffn2_w_out"), FF_SHARD, "adamw_ffn2_w_out"),
        "attn_w_o": adamw(p_r, R_ATTN, *fam("attn_w_o"), 2 * MIX_SHARD, "adamw_attn_w_o"),
        "lru_w_out": adamw(p_r, R_LRU, *fam("lru_w_out"), 2 * MIX_SHARD, "adamw_lru_w_out"),
        "ffn1_w_in": adamw(p704, 0, *fam("ffn1_w_in"), 512, "adamw_ffn1_w_in"),
        "ffn2_w_in": adamw(p704, DEPTH * D_MODEL, *fam("ffn2_w_in"), 512, "adamw_ffn2_w_in"),
        "attn_w_qkv": adamw(p192, 0, *fam("attn_w_qkv"), 512, "adamw_attn_w_qkv"),
        "lru_w_in": adamw(p256, 0, *fam("lru_w_in"), 512, "adamw_lru_w_in"),
        "lru_w_ra": adamw(p256, C256_RA, *fam("lru_w_ra"), 256, "adamw_lru_w_ra"),
        "lru_w_rx": adamw(p256, C256_RX, *fam("lru_w_rx"), 256, "adamw_lru_w_rx"),
    }
    sm_out = adamw(p_sm, 0, sm, _small_pack(args, "m_"), _small_pack(args, "v_"), SM_ROWS, "adamw_small")
    for k, pack in enumerate(sm_out):
        for n, val in _small_unpack(pack).items():
            res.setdefault(n, [None] * 4)[k] = val
    loss_total = sm_out[0].reshape(-1)[LOSS_OFF]
    out = [loss_total, dx[None]]
    for k in range(4):
        out += [res[n][k] for n in WEIGHTS]
    return tuple(out)
```

```python
import math

import jax
import jax.numpy as jnp
import numpy as np
from jax import lax
from jax.experimental import pallas as pl
from jax.experimental.pallas import tpu as pltpu

F32 = jnp.float32
BF16 = jnp.bfloat16

D_MODEL = 1024
DEPTH = 4
N_HEADS = 16
N_KV = 4
HEAD_DIM = 64
GROUP = 4
BLOCK = 128
ROPE_THETA = 10000.0
D_RNN = 1024
RNN_BLOCKS = 4
RNN_W = 256
CONV_W = 4
LRU_C = 8.0
D_FF = 2816
ALPHA = (2.0 * DEPTH) ** 0.25
LN_EPS = 1e-5
QKV = (N_HEADS + 2 * N_KV) * HEAD_DIM
N_DEV = 8

ADAM_LR = 0.001
ADAM_B1 = 0.9
ADAM_B2 = 0.999
ADAM_EPS = 1e-08
ADAM_WD = 0.01
ADAM_STEP = 10

VMEM_LIMIT = 52 * 1024 * 1024
NEG = float(np.finfo(np.float32).min)

MESH = pl.DeviceIdType.MESH
ANY = pl.BlockSpec(memory_space=pl.ANY)

FF_SHARD = D_FF // N_DEV
MIX_SHARD = D_MODEL // N_DEV
R_FFN2 = DEPTH * FF_SHARD
R_ATTN = 2 * DEPTH * FF_SHARD
R_LRU = R_ATTN + 2 * MIX_SHARD
R_ROWS = R_LRU + 2 * MIX_SHARD
C256_RA = 2 * D_MODEL
C256_RX = C256_RA + 2 * RNN_BLOCKS * (RNN_W // N_DEV)
C256_ROWS = C256_RX + 2 * RNN_BLOCKS * (RNN_W // N_DEV)

SMALL = (
    ("ln_g", (4, 3, 128), 2),
    ("ln_b", (4, 3, 128), 2),
    ("lru_conv_w", (2, 4, 128), 2),
    ("lru_conv_b", (2, 128), 1),
    ("lru_b_ra", (2, 128), 1),
    ("lru_b_rx", (2, 128), 1),
    ("lru_lambda", (2, 128), 1),
)
WEIGHTS = ("ffn1_w_in", "ffn1_w_out", "ffn2_w_in", "ffn2_w_out", "ln_g", "ln_b", "attn_w_qkv", "attn_sinks",
           "attn_w_o", "lru_w_in", "lru_conv_w", "lru_conv_b", "lru_w_ra", "lru_b_ra", "lru_w_rx", "lru_b_rx",
           "lru_lambda", "lru_w_out")
SMALL_N = sum(math.prod(s) for _, s, _ in SMALL)
SINK_OFF = SMALL_N
LOSS_OFF = SMALL_N + 32
SM_ROWS = 8


def _cp(*sem):
    return pltpu.CompilerParams(dimension_semantics=sem, vmem_limit_bytes=VMEM_LIMIT)


def _dot(a, b):
    return jnp.dot(a, b, preferred_element_type=F32)


def _dot_tn(a, b):
    return lax.dot_general(a, b, (((0,), (0,)), ((), ())), preferred_element_type=F32)


def _dot_nt(a, b):
    return lax.dot_general(a, b, (((1,), (1,)), ((), ())), preferred_element_type=F32)


def _col_chunks(cols, size=512):
    return [slice(c, min(c + size, cols)) for c in range(0, cols, size)]


def _ln(z, g, b):
    mu = jnp.mean(z, axis=-1, keepdims=True)
    xc = z - mu
    var = jnp.mean(xc * xc, axis=-1, keepdims=True)
    return xc * lax.rsqrt(var + LN_EPS) * g + b


def _rows_view(pack, rows_per_dev, row0):
    return (pack, (N_DEV, rows_per_dev, D_MODEL), (0, row0 // rows_per_dev, 0), (N_DEV * rows_per_dev, D_MODEL))


def _mat_view(arr, cols=None, cblk=0):
    k, n = arr.shape
    cols = n if cols is None else cols
    return (arr, (k, cols), (0, cblk), (k, cols))


def _vspec(view):
    _, bshape, bidx, _ = view
    return pl.BlockSpec(bshape, lambda *_: bidx)


def _vload(view, ref):
    return ref[...].reshape(view[3])


class Comm:
    def __init__(self, ins, out_shapes, sems, start, finish, aliases=None, relay=None):
        self.ins, self.out_shapes, self.sems = list(ins), list(out_shapes), list(sems)
        self.start, self.finish, self.aliases = start, finish, dict(aliases or {})
        self.relay = relay


def _call(body, *, grid, in_specs, out_specs, out_shape, operands, name, sem, scratch=(), comm=None):
    n_in, n_out, n_scr = len(in_specs), len(out_specs), len(scratch)
    if comm is None:
        return pl.pallas_call(body, grid=grid, in_specs=list(in_specs), out_specs=list(out_specs),
                              out_shape=list(out_shape), scratch_shapes=list(scratch), compiler_params=_cp(*sem),
                              name=name)(*operands), []
    nci, nco = len(comm.ins), len(comm.out_shapes)

    def hosted(*refs):
        ins, cins = refs[:n_in], refs[n_in:n_in + nci]
        o0 = n_in + nci
        outs, couts = refs[o0:o0 + n_out], refs[o0 + n_out:o0 + n_out + nco]
        s0 = o0 + n_out + nco
        scr, csems = refs[s0:s0 + n_scr], refs[s0 + n_scr:]
        step = 0
        for ax, size in enumerate(grid):
            step = step * size + pl.program_id(ax)
        steps = math.prod(grid)

        @pl.when(step == 0)
        def _():
            comm.start(cins, couts, csems)

        if comm.relay is not None and steps >= 4:
            @pl.when(step == (3 * steps) // 4)
            def _():
                comm.relay(cins, couts, csems)

        body(*ins, *outs, *scr)

        @pl.when(step == steps - 1)
        def _():
            if comm.relay is not None and steps < 4:
                comm.relay(cins, couts, csems)
            comm.finish(cins, couts, csems)

    res = pl.pallas_call(
        hosted, grid=grid, in_specs=list(in_specs) + [ANY] * nci, out_specs=list(out_specs) + [ANY] * nco,
        out_shape=list(out_shape) + comm.out_shapes, scratch_shapes=list(scratch) + comm.sems,
        input_output_aliases={n_in + i: n_out + o for i, o in comm.aliases.items()},
        compiler_params=_cp(*(("arbitrary",) * len(grid))), name=name)(*operands, *comm.ins)
    return res[:n_out], res[n_out:]


def run_comm(comm, name):
    nci, nco = len(comm.ins), len(comm.out_shapes)

    def body(*refs):
        cins, couts, csems = refs[:nci], refs[nci:nci + nco], refs[nci + nco:]
        comm.start(cins, couts, csems)
        if comm.relay is not None:
            comm.relay(cins, couts, csems)
        comm.finish(cins, couts, csems)

    return pl.pallas_call(
        body, in_specs=[ANY] * nci, out_specs=[ANY] * nco, out_shape=comm.out_shapes, scratch_shapes=comm.sems,
        input_output_aliases=dict(comm.aliases), name=name)(*comm.ins)


def mm_plain(a, wv, out_dtype, name, nt=False, tm=512):
    S, K = a.shape
    N = wv[3][0] if nt else wv[3][1]
    tm = min(tm, S)
    dot = _dot_nt if nt else _dot

    def body(a_ref, w_ref, o_ref):
        o_ref[...] = dot(a_ref[...], _vload(wv, w_ref)).astype(out_dtype)

    return pl.pallas_call(
        body, grid=(S // tm,),
        in_specs=[pl.BlockSpec((tm, K), lambda i: (i, 0)), _vspec(wv)],
        out_specs=pl.BlockSpec((tm, N), lambda i: (i, 0)),
        out_shape=jax.ShapeDtypeStruct((S, N), out_dtype),
        compiler_params=_cp("parallel"), name=name)(a, wv[0])


def mm_res_nt(a, wv, r, alpha, name, tm=512):
    S, K = a.shape
    N = wv[3][0]
    tm = min(tm, S)

    def body(a_ref, w_ref, r_ref, o_ref):
        o_ref[...] = _dot_nt(a_ref[...], _vload(wv, w_ref)) + alpha * r_ref[...]

    return pl.pallas_call(
        body, grid=(S // tm,),
        in_specs=[pl.BlockSpec((tm, K), lambda i: (i, 0)), _vspec(wv), pl.BlockSpec((tm, N), lambda i: (i, 0))],
        out_specs=pl.BlockSpec((tm, N), lambda i: (i, 0)),
        out_shape=jax.ShapeDtypeStruct((S, N), F32),
        compiler_params=_cp("parallel"), name=name)(a, wv[0], r)


def mm_res2_nt(a3, wv0, wv1, r, alpha, name, tm=512, comm=None):
    _, S, K = a3.shape
    N = wv0[3][0]
    tm = min(tm, S)

    def body(a_ref, w0_ref, w1_ref, r_ref, o_ref):
        o_ref[...] = (_dot_nt(a_ref[0], _vload(wv0, w0_ref)) + _dot_nt(a_ref[1], _vload(wv1, w1_ref))
                      + alpha * r_ref[...])

    (out,), couts = _call(
        body, grid=(S // tm,),
        in_specs=[pl.BlockSpec((2, tm, K), lambda i: (0, i, 0)), _vspec(wv0), _vspec(wv1),
                  pl.BlockSpec((tm, N), lambda i: (i, 0))],
        out_specs=[pl.BlockSpec((tm, N), lambda i: (i, 0))],
        out_shape=[jax.ShapeDtypeStruct((S, N), F32)],
        operands=(a3, wv0[0], wv1[0], r), name=name, sem=("parallel",), comm=comm)
    return out, couts


def mm_ln(a, wv, h, g, b, scale, name, tm=512, comm=None):
    S, K = a.shape
    tm = min(tm, S)

    def body(a_ref, w_ref, h_ref, g_ref, b_ref, z_ref, y_ref, yb_ref):
        z = ALPHA * h_ref[...] + scale * _dot(a_ref[...], _vload(wv, w_ref))
        y = _ln(z, g_ref[...], b_ref[...])
        z_ref[...] = z
        y_ref[...] = y
        yb_ref[...] = y.astype(BF16)

    row = pl.BlockSpec((tm, D_MODEL), lambda i: (i, 0))
    vec = pl.BlockSpec((1, D_MODEL), lambda i: (0, 0))
    return _call(
        body, grid=(S // tm,),
        in_specs=[pl.BlockSpec((tm, K), lambda i: (i, 0)), _vspec(wv), row, vec, vec],
        out_specs=[row, row, row],
        out_shape=[jax.ShapeDtypeStruct((S, D_MODEL), F32), jax.ShapeDtypeStruct((S, D_MODEL), F32),
                   jax.ShapeDtypeStruct((S, D_MODEL), BF16)],
        operands=(a, wv[0], h, g.reshape(1, -1), b.reshape(1, -1)), name=name, sem=("parallel",), comm=comm)


def ffn_up(xb, w_in, name, tm=512, tn=1408, comm=None):
    S = xb.shape[0]
    tm = min(tm, S)
    nj = D_FF // tn

    def body(x_ref, wg_ref, wu_ref, jac_ref, a_ref):
        x = x_ref[...]
        g = _dot(x, wg_ref[...])
        u = _dot(x, wu_ref[...])
        sg = jax.nn.sigmoid(g)
        t = g * sg
        jac_ref[0] = (u * (sg * ((g - t) + 1.0))).astype(BF16)
        jac_ref[1] = t.astype(BF16)
        a_ref[...] = (t * u).astype(BF16)

    return _call(
        body, grid=(nj, S // tm),
        in_specs=[pl.BlockSpec((tm, D_MODEL), lambda j, i: (i, 0)),
                  pl.BlockSpec((D_MODEL, tn), lambda j, i: (0, j)),
                  pl.BlockSpec((D_MODEL, tn), lambda j, i: (0, nj + j))],
        out_specs=[pl.BlockSpec((2, tm, tn), lambda j, i: (0, i, j)), pl.BlockSpec((tm, tn), lambda j, i: (i, j))],
        out_shape=[jax.ShapeDtypeStruct((2, S, D_FF), BF16), jax.ShapeDtypeStruct((S, D_FF), BF16)],
        operands=(xb, w_in, w_in), name=name, sem=("parallel", "parallel"), comm=comm)


def ffn_mid_bwd(dfb, r_ff, blk, gu, name, tm=512, comm=None):
    S = dfb.shape[0]
    tm = min(tm, S)
    tn = 4 * FF_SHARD

    def body(df_ref, w_ref, gu_ref, dh_ref):
        w = w_ref[...].reshape(tn, D_MODEL)
        df = df_ref[...]
        for cols in _col_chunks(tn):
            da = _dot_nt(df, w[cols, :])
            dh_ref[0, :, cols] = (da * gu_ref[0, :, cols].astype(F32)).astype(BF16)
            dh_ref[1, :, cols] = (da * gu_ref[1, :, cols].astype(F32)).astype(BF16)

    gspec = pl.BlockSpec((2, tm, tn), lambda j, i: (0, i, j))
    (out,), couts = _call(
        body, grid=(2, S // tm),
        in_specs=[pl.BlockSpec((tm, D_MODEL), lambda j, i: (i, 0)),
                  pl.BlockSpec((4, FF_SHARD, D_MODEL), lambda j, i: (j, blk, 0)), gspec],
        out_specs=[gspec],
        out_shape=[jax.ShapeDtypeStruct((2, S, D_FF), BF16)],
        operands=(dfb, r_ff, gu), name=name, sem=("parallel", "parallel"), comm=comm)
    return out, couts


def mm_tn(a, b, name, tm, tn, ts=2048, planes=1, shard=None, comm=None):
    S, M = a.shape
    N = b.shape[-1] * planes
    ts = min(ts, S)
    per = b.shape[-1] // tn
    ns = S // ts

    if shard is None:
        def body(a_ref, b_ref, o_ref):
            @pl.when(pl.program_id(2) == 0)
            def _():
                o_ref[...] = jnp.zeros_like(o_ref)

            o_ref[...] += _dot_tn(a_ref[...], b_ref[...])

        out_spec = pl.BlockSpec((tm, tn), lambda i, j, s: (i, j))
        out_shape = jax.ShapeDtypeStruct((M, N), F32)
        scratch = ()
    else:
        def body(a_ref, b_ref, o_ref, acc_ref):
            s = pl.program_id(2)

            @pl.when(s == 0)
            def _():
                acc_ref[...] = jnp.zeros_like(acc_ref)

            acc_ref[...] += _dot_tn(a_ref[...], b_ref[...])

            @pl.when(s == ns - 1)
            def _():
                for q in range(tn // shard):
                    o_ref[q] = acc_ref[:, shard * q:shard * (q + 1)]

        out_spec = pl.BlockSpec((tn // shard, tm, shard), lambda i, j, s: (j, i, 0))
        out_shape = jax.ShapeDtypeStruct((N // shard, M, shard), F32)
        scratch = (pltpu.VMEM((tm, tn), F32),)

    if planes == 1:
        bspec = pl.BlockSpec((ts, tn), lambda i, j, s: (s, j))
    else:
        bspec = pl.BlockSpec((None, ts, tn), lambda i, j, s: (j // per, s, j % per))
    (out,), couts = _call(
        body, grid=(M // tm, N // tn, ns),
        in_specs=[pl.BlockSpec((ts, tm), lambda i, j, s: (s, i)), bspec],
        out_specs=[out_spec], out_shape=[out_shape], scratch=scratch,
        operands=(a, b), name=name, sem=("parallel", "parallel", "arbitrary"), comm=comm)
    return out, couts


def ln_bwd(dy, z, g, out_scale, name, tm=512):
    S = dy.shape[0]
    tm = min(tm, S)

    def body(dy_ref, z_ref, g_ref, dz_ref, dzb_ref, dg_ref, db_ref):
        @pl.when(pl.program_id(0) == 0)
        def _():
            dg_ref[...] = jnp.zeros_like(dg_ref)
            db_ref[...] = jnp.zeros_like(db_ref)

        z = z_ref[...]
        dy_ = dy_ref[...]
        mu = jnp.mean(z, axis=-1, keepdims=True)
        xc = z - mu
        var = jnp.mean(xc * xc, axis=-1, keepdims=True)
        rstd = lax.rsqrt(var + LN_EPS)
        xh = xc * rstd
        dxh = dy_ * g_ref[...]
        m1 = jnp.mean(dxh, axis=-1, keepdims=True)
        m2 = jnp.mean(dxh * xh, axis=-1, keepdims=True)
        dz = rstd * (dxh - m1 - xh * m2)
        dz_ref[...] = dz
        dzb_ref[...] = (out_scale * dz).astype(BF16)
        dg_ref[...] += jnp.sum(dy_ * xh, axis=0, keepdims=True)
        db_ref[...] += jnp.sum(dy_, axis=0, keepdims=True)

    row = pl.BlockSpec((tm, D_MODEL), lambda i: (i, 0))
    vec = pl.BlockSpec((1, D_MODEL), lambda i: (0, 0))
    return pl.pallas_call(
        body, grid=(S // tm,),
        in_specs=[row, row, vec],
        out_specs=[row, row, vec, vec],
        out_shape=[jax.ShapeDtypeStruct((S, D_MODEL), F32), jax.ShapeDtypeStruct((S, D_MODEL), BF16),
                   jax.ShapeDtypeStruct((1, D_MODEL), F32), jax.ShapeDtypeStruct((1, D_MODEL), F32)],
        compiler_params=_cp("arbitrary"), name=name)(dy, z, g.reshape(1, -1))


def loss_head(y, t, name, tm=512):
    S = y.shape[0]
    tm = min(tm, S)
    nt = S // tm

    def body(y_ref, t_ref, dy_ref, l_ref):
        i = pl.program_id(0)

        @pl.when(i == 0)
        def _():
            l_ref[...] = jnp.zeros_like(l_ref)

        e = y_ref[...] - t_ref[...]
        dy_ref[...] = e * (1.0 / D_MODEL)
        l_ref[...] += jnp.sum(e * e, axis=0, keepdims=True)

        @pl.when(i == nt - 1)
        def _():
            tot = jnp.sum(l_ref[...], axis=1, keepdims=True) * (0.5 / D_MODEL)
            l_ref[...] = jnp.broadcast_to(tot, l_ref.shape)

    row = pl.BlockSpec((tm, D_MODEL), lambda i: (i, 0))
    vec = pl.BlockSpec((1, D_MODEL), lambda i: (0, 0))
    return pl.pallas_call(
        body, grid=(nt,), in_specs=[row, row], out_specs=[row, vec],
        out_shape=[jax.ShapeDtypeStruct((S, D_MODEL), F32), jax.ShapeDtypeStruct((1, D_MODEL), F32)],
        compiler_params=_cp("arbitrary"), name=name)(y, t)


def _rope_tables(S):
    pos = jnp.arange(S, dtype=F32)
    inv_freq = ROPE_THETA ** (-jnp.arange(0, HEAD_DIM, 2, dtype=F32) / HEAD_DIM)
    ang = pos[:, None] * inv_freq[None, :]
    cos, sin = jnp.cos(ang), jnp.sin(ang)
    cosf = jnp.concatenate([cos, cos, cos, cos], axis=1)
    sinf = jnp.concatenate([-sin, sin, -sin, sin], axis=1)
    return cosf, sinf


def _rot(t, c, s, first):
    sw = jnp.where(first, pltpu.roll(t, 96, 1), pltpu.roll(t, 32, 1))
    return t * c + sw * s


def rope_fwd(qkv, cosf, sinf, name, tm=512):
    S = qkv.shape[0]
    tm = min(tm, S)

    def body(x_ref, c_ref, s_ref, q_ref, k_ref, v_ref):
        c = c_ref[...]
        s = s_ref[...]
        first = (lax.broadcasted_iota(jnp.int32, (tm, 128), 1) % HEAD_DIM) < (HEAD_DIM // 2)
        for j in range(8):
            q_ref[:, 128 * j:128 * (j + 1)] = _rot(x_ref[:, 128 * j:128 * (j + 1)], c, s, first).astype(BF16)
        for j in range(2):
            k_ref[:, 128 * j:128 * (j + 1)] = _rot(x_ref[:, 1024 + 128 * j:1024 + 128 * (j + 1)], c, s, first).astype(BF16)
        v_ref[...] = x_ref[:, 1280:1536].astype(BF16)

    tab = pl.BlockSpec((tm, 128), lambda i: (i, 0))
    return pl.pallas_call(
        body, grid=(S // tm,),
        in_specs=[pl.BlockSpec((tm, QKV), lambda i: (i, 0)), tab, tab],
        out_specs=[pl.BlockSpec((tm, 1024), lambda i: (i, 0)), pl.BlockSpec((tm, 256), lambda i: (i, 0)),
                   pl.BlockSpec((tm, 256), lambda i: (i, 0))],
        out_shape=[jax.ShapeDtypeStruct((S, 1024), BF16), jax.ShapeDtypeStruct((S, 256), BF16),
                   jax.ShapeDtypeStruct((S, 256), BF16)],
        compiler_params=_cp("parallel"), name=name)(qkv, cosf, sinf)


def rope_bwd(dq, dkc, dkp, dvc, dvp, cosf, sinf, name, tq):
    S = dq.shape[0]
    nt = S // tq

    def body(dq_ref, dkc_ref, dkp_ref, dvc_ref, dvp_ref, c_ref, s_ref, o_ref):
        i = pl.program_id(0)
        c = c_ref[...]
        s = -s_ref[...]
        first = (lax.broadcasted_iota(jnp.int32, (tq, 128), 1) % HEAD_DIM) < (HEAD_DIM // 2)
        for j in range(8):
            o_ref[:, 128 * j:128 * (j + 1)] = _rot(dq_ref[:, 128 * j:128 * (j + 1)], c, s, first).astype(BF16)
        has_next = i < nt - 1
        rows = lax.broadcasted_iota(jnp.int32, (tq, 256), 0)
        pad = jnp.zeros((tq - BLOCK, 256), F32)
        halo_k = jnp.concatenate([pad, dkp_ref[...]], axis=0)
        halo_v = jnp.concatenate([pad, dvp_ref[...]], axis=0)
        use = jnp.logical_and(has_next, rows >= tq - BLOCK)
        dk = dkc_ref[...] + jnp.where(use, halo_k, 0.0)
        dv = dvc_ref[...] + jnp.where(use, halo_v, 0.0)
        for j in range(2):
            o_ref[:, 1024 + 128 * j:1024 + 128 * (j + 1)] = _rot(dk[:, 128 * j:128 * (j + 1)], c, s, first).astype(BF16)
        o_ref[:, 1280:1536] = dv.astype(BF16)

    tab = pl.BlockSpec((tq, 128), lambda i: (i, 0))
    cur = pl.BlockSpec((tq, 256), lambda i: (i, 0))
    nxt = pl.BlockSpec((BLOCK, 256), lambda i: (jnp.minimum(i + 1, nt - 1), 0))
    return pl.pallas_call(
        body, grid=(nt,),
        in_specs=[pl.BlockSpec((tq, 1024), lambda i: (i, 0)), cur, nxt, cur, nxt, tab, tab],
        out_specs=pl.BlockSpec((tq, QKV), lambda i: (i, 0)),
        out_shape=jax.ShapeDtypeStruct((S, QKV), BF16),
        compiler_params=_cp("parallel"), name=name)(dq, dkc, dkp, dvc, dvp, cosf, sinf)


def _attn_masks(n):
    shape = (GROUP * BLOCK, 2 * BLOCK)
    r = lax.broadcasted_iota(jnp.int32, shape, 0) % BLOCK
    c = lax.broadcasted_iota(jnp.int32, shape, 1)
    ok = jnp.logical_and(c > r, c <= r + BLOCK)
    return jnp.logical_and(ok, jnp.logical_or(c >= BLOCK, n > 0)), ok


def _attn_probs(qs, kw, sink_col, ok):
    s = jnp.where(ok, _dot_nt(qs, kw) * (HEAD_DIM ** -0.5), NEG)
    m = jnp.maximum(jnp.max(s, axis=1, keepdims=True), sink_col)
    p = jnp.exp(s - m)
    es = jnp.exp(sink_col - m)
    inv = 1.0 / (jnp.sum(p, axis=1, keepdims=True) + es)
    return p * inv, es * inv


def _sink_col(sink_ref, g):
    rid = lax.broadcasted_iota(jnp.int32, (GROUP * BLOCK, 1), 0) // BLOCK
    col = jnp.zeros((GROUP * BLOCK, 1), F32)
    for j in range(GROUP):
        col = jnp.where(rid == j, sink_ref[GROUP * g + j], col)
    return col


KV_PAIR = 2
Q_LANES = KV_PAIR * GROUP * HEAD_DIM


def _low_half(shape):
    return lax.broadcasted_iota(jnp.int32, shape, 1) < HEAD_DIM


def _head_rows(ref, lo, gi):
    low = _low_half((BLOCK, 2 * HEAD_DIM))
    out = []
    for s in range(GROUP // 2):
        c0 = gi * GROUP * HEAD_DIM + 2 * HEAD_DIM * s
        x = ref[lo:lo + BLOCK, c0:c0 + 2 * HEAD_DIM]
        out += [jnp.where(low, x, jnp.zeros_like(x)), jnp.where(low, jnp.zeros_like(x), x)]
    return jnp.concatenate(out, axis=0)


def _head_slabs(x):
    low = _low_half((BLOCK, 2 * HEAD_DIM))
    return [jnp.where(low, x[2 * BLOCK * s:2 * BLOCK * s + BLOCK], x[2 * BLOCK * s + BLOCK:2 * BLOCK * (s + 1)])
            for s in range(GROUP // 2)]


def _kv_window(kc_ref, kp_ref, b, gi):
    if b == 0:
        x = jnp.concatenate([kp_ref[...], kc_ref[0:BLOCK, :]], axis=0)
    else:
        x = kc_ref[BLOCK * (b - 1):BLOCK * (b + 1), :]
    other = pltpu.roll(x, HEAD_DIM, 1)
    low = _low_half(x.shape)
    return jnp.where(low, x, other) if gi == 0 else jnp.where(low, other, x)


def _fold_halves(x, gi):
    tot = x + pltpu.roll(x, HEAD_DIM, 1)
    low = _low_half(x.shape)
    return jnp.where(low if gi == 0 else jnp.logical_not(low), tot, 0.0)


def _attn_specs(tq):
    nsub = tq // BLOCK
    qspec = pl.BlockSpec((tq, Q_LANES), lambda p, n: (n, p))
    cur = pl.BlockSpec((tq, KV_PAIR * HEAD_DIM), lambda p, n: (n, p))
    prev = pl.BlockSpec((BLOCK, KV_PAIR * HEAD_DIM), lambda p, n: (jnp.maximum(n * nsub - 1, 0), p))
    return qspec, cur, prev


def attn_fwd(q, k, v, sinks, name, tq, comm=None):
    S = q.shape[0]
    nsub = tq // BLOCK

    def body(sink_ref, q_ref, kc_ref, kp_ref, vc_ref, vp_ref, o_ref):
        p = pl.program_id(0)
        n = pl.program_id(1)
        ok_first, ok_rest = _attn_masks(n)
        for gi in range(KV_PAIR):
            sink_col = _sink_col(sink_ref, KV_PAIR * p + gi)
            base = gi * GROUP * HEAD_DIM
            for b in range(nsub):
                lo = BLOCK * b
                qs = _head_rows(q_ref, lo, gi)
                pn, _ = _attn_probs(qs, _kv_window(kc_ref, kp_ref, b, gi), sink_col, ok_rest if b else ok_first)
                o = _dot(pn.astype(BF16), _kv_window(vc_ref, vp_ref, b, gi))
                for s, slab in enumerate(_head_slabs(o)):
                    c0 = base + 2 * HEAD_DIM * s
                    o_ref[lo:lo + BLOCK, c0:c0 + 2 * HEAD_DIM] = slab.astype(BF16)

    qspec, cur, prev = _attn_specs(tq)
    (out,), couts = _call(
        body, grid=(N_KV // KV_PAIR, S // tq),
        in_specs=[pl.BlockSpec(memory_space=pltpu.SMEM), qspec, cur, prev, cur, prev],
        out_specs=[qspec],
        out_shape=[jax.ShapeDtypeStruct((S, N_HEADS * HEAD_DIM), BF16)],
        operands=(sinks, q, k, k, v, v), name=name, sem=("parallel", "parallel"), comm=comm)
    return out, couts


def attn_bwd(q, k, v, do, sinks, name, tq):
    S = q.shape[0]
    nsub = tq // BLOCK
    nt = S // tq

    def body(sink_ref, q_ref, kc_ref, kp_ref, vc_ref, vp_ref, do_ref, dq_ref, dkc_ref, dkp_ref, dvc_ref, dvp_ref, ds_ref):
        p = pl.program_id(0)
        n = pl.program_id(1)

        @pl.when(n == 0)
        def _():
            ds_ref[...] = jnp.zeros_like(ds_ref)

        for ref in (dkc_ref, dvc_ref, dkp_ref, dvp_ref):
            ref[...] = jnp.zeros_like(ref)
        rid = lax.broadcasted_iota(jnp.int32, (GROUP * BLOCK, 1), 0) // BLOCK
        sub = lax.broadcasted_iota(jnp.int32, (8, 128), 0)
        ok_first, ok_rest = _attn_masks(n)
        for gi in range(KV_PAIR):
            sink_col = _sink_col(sink_ref, KV_PAIR * p + gi)
            base = gi * GROUP * HEAD_DIM
            dsink = jnp.zeros((8, 128), F32)
            for b in range(nsub):
                lo = BLOCK * b
                qs = _head_rows(q_ref, lo, gi)
                dos = _head_rows(do_ref, lo, gi)
                kw = _kv_window(kc_ref, kp_ref, b, gi)
                vw = _kv_window(vc_ref, vp_ref, b, gi)
                pn, ps = _attn_probs(qs, kw, sink_col, ok_rest if b else ok_first)
                dp = _dot_nt(dos, vw)
                delta = jnp.sum(pn * dp, axis=1, keepdims=True)
                dsb = (pn * (dp - delta) * (HEAD_DIM ** -0.5)).astype(BF16)
                for s, slab in enumerate(_head_slabs(_dot(dsb, kw))):
                    c0 = base + 2 * HEAD_DIM * s
                    dq_ref[lo:lo + BLOCK, c0:c0 + 2 * HEAD_DIM] = slab
                dkw = _fold_halves(_dot_tn(dsb, qs), gi)
                dvw = _fold_halves(_dot_tn(pn.astype(BF16), dos), gi)
                if b == 0:
                    dkp_ref[...] += dkw[0:BLOCK]
                    dvp_ref[...] += dvw[0:BLOCK]
                else:
                    dkc_ref[lo - BLOCK:lo, :] += dkw[0:BLOCK]
                    dvc_ref[lo - BLOCK:lo, :] += dvw[0:BLOCK]
                dkc_ref[lo:lo + BLOCK, :] += dkw[BLOCK:2 * BLOCK]
                dvc_ref[lo:lo + BLOCK, :] += dvw[BLOCK:2 * BLOCK]
                sd = ps * delta
                for j in range(GROUP):
                    tot = -jnp.sum(jnp.where(rid == j, sd, 0.0))
                    dsink = dsink + jnp.where(sub == j, tot, 0.0)
            ds_ref[gi] += dsink

    qspec, cur, prev = _attn_specs(tq)
    halo = pl.BlockSpec((BLOCK, KV_PAIR * HEAD_DIM), lambda p, n: (n, p))
    kv_shape = jax.ShapeDtypeStruct((S, N_KV * HEAD_DIM), F32)
    halo_shape = jax.ShapeDtypeStruct((nt * BLOCK, N_KV * HEAD_DIM), F32)
    return pl.pallas_call(
        body, grid=(N_KV // KV_PAIR, nt),
        in_specs=[pl.BlockSpec(memory_space=pltpu.SMEM), qspec, cur, prev, cur, prev, qspec],
        out_specs=[qspec, cur, halo, cur, halo, pl.BlockSpec((KV_PAIR, 8, 128), lambda p, n: (p, 0, 0))],
        out_shape=[jax.ShapeDtypeStruct((S, N_HEADS * HEAD_DIM), F32), kv_shape, halo_shape, kv_shape, halo_shape,
                   jax.ShapeDtypeStruct((N_KV, 8, 128), F32)],
        compiler_params=_cp("parallel", "arbitrary"), name=name)(sinks, q, k, k, v, v, do)


def _rows_before(cur, prev8, k):
    if k == 0:
        return cur
    n = cur.shape[0]
    ext = jnp.concatenate([prev8, cur], axis=0)
    return ext[8 - k:8 - k + n]


def _rows_after(cur, next8, k):
    if k == 0:
        return cur
    n = cur.shape[0]
    ext = jnp.concatenate([cur, next8], axis=0)
    return ext[k:k + n]


def _gelu(x):
    c = math.sqrt(2.0 / math.pi)
    t = jnp.tanh(c * (x + 0.044715 * (x * x * x)))
    return 0.5 * (1.0 + t), t


def _neg_expm1(u):
    ser = 1.0 + u * (1.0 / 6.0)
    for k in range(5, 1, -1):
        ser = 1.0 + (u * (1.0 / k)) * ser
    return jnp.where(u > -0.125, -(u * ser), 1.0 - jnp.exp(u))


def _block_diag(xb16, w_ref):
    return jnp.concatenate([_dot(xb16[:, RNN_W * n:RNN_W * (n + 1)], w_ref[n]) for n in range(RNN_BLOCKS)], axis=1)


def _lru_gates(xb, prev8, cw_ref, cb_ref, wra_ref, wrx_ref, bra_ref, brx_ref, lsl_ref):
    xc = cb_ref[...] + cw_ref[3:4, :] * xb
    for w in range(CONV_W - 1):
        xc = xc + cw_ref[w:w + 1, :] * _rows_before(xb, prev8, CONV_W - 1 - w)
    xcb = xc.astype(BF16)
    r = jax.nn.sigmoid(_block_diag(xcb, wra_ref) + bra_ref[...])
    ig = jax.nn.sigmoid(_block_diag(xcb, wrx_ref) + brx_ref[...])
    la = LRU_C * r * lsl_ref[...]
    a = jnp.exp(la)
    sq = jnp.sqrt(_neg_expm1(2.0 * la))
    return xc, xcb, r, ig, a, sq


def lru_fwd(xg, p, name, tm=256, comm=None):
    S = xg.shape[0]
    tm = min(tm, S)

    def body(xg_ref, xp_ref, cw_ref, cb_ref, wra_ref, wrx_ref, bra_ref, brx_ref, lsl_ref, y_ref, h_ref, hc_ref, a_s, b_s):
        i = pl.program_id(0)
        xb = xg_ref[:, 0:D_RNN]
        gb = xg_ref[:, D_RNN:2 * D_RNN]
        prev8 = jnp.where(i > 0, xp_ref[:, 0:D_RNN], 0.0)
        xc, _, r, ig, a, sq = _lru_gates(xb, prev8, cw_ref, cb_ref, wra_ref, wrx_ref, bra_ref, brx_ref, lsl_ref)
        a_s[...] = a
        b_s[...] = sq * (ig * xc)

        @pl.when(i == 0)
        def _():
            hc_ref[...] = jnp.zeros_like(hc_ref)

        def chunk(c, h):
            o = pl.multiple_of(c * 8, 8)
            av = a_s[pl.ds(o, 8), :]
            bv = b_s[pl.ds(o, 8), :]
            rows = []
            for t in range(8):
                h = av[t:t + 1, :] * h + bv[t:t + 1, :]
                rows.append(h)
            h_ref[pl.ds(o, 8), :] = jnp.concatenate(rows, axis=0)
            return h

        h_last = lax.fori_loop(0, tm // 8, chunk, hc_ref[0:1, :])
        hc_ref[0:1, :] = h_last
        cdf, _ = _gelu(gb)
        y_ref[...] = (h_ref[...] * (gb * cdf)).astype(BF16)

    vec = pl.BlockSpec((1, D_RNN), lambda i: (0, 0))
    wsp = pl.BlockSpec((RNN_BLOCKS, RNN_W, RNN_W), lambda i: (0, 0, 0))
    return _call(
        body, grid=(S // tm,),
        in_specs=[pl.BlockSpec((tm, 2 * D_RNN), lambda i: (i, 0)),
                  pl.BlockSpec((8, 2 * D_RNN), lambda i: (jnp.maximum(i * (tm // 8) - 1, 0), 0)),
                  pl.BlockSpec((CONV_W, D_RNN), lambda i: (0, 0)), vec, wsp, wsp, vec, vec, vec],
        out_specs=[pl.BlockSpec((tm, D_RNN), lambda i: (i, 0)), pl.BlockSpec((tm, D_RNN), lambda i: (i, 0))],
        out_shape=[jax.ShapeDtypeStruct((S, D_RNN), BF16), jax.ShapeDtypeStruct((S, D_RNN), F32)],
        scratch=[pltpu.VMEM((8, D_RNN), F32), pltpu.VMEM((tm, D_RNN), F32), pltpu.VMEM((tm, D_RNN), F32)],
        operands=(xg, xg, p["conv_w"], p["conv_b"], p["w_ra"], p["w_rx"], p["b_ra"], p["b_rx"], p["lsl"]),
        name=name, sem=("arbitrary",), comm=comm)


def lru_bwd(dy, xg, h, p, name, tm=256):
    S = xg.shape[0]
    tm = min(tm, S)
    nt = S // tm

    def body(dy_ref, xg_ref, xp_ref, h_ref, hp_ref, cw_ref, cb_ref, wra_ref, wrx_ref, bra_ref, brx_ref,
             lsl_ref, dxg_ref, dcw_ref, dcb_ref, dwra_ref, dwrx_ref, dbra_ref, dbrx_ref, dlam_ref,
             lc_ref, nx_ref, a_s, g_s, l_s):
        i = pl.program_id(0)
        ti = nt - 1 - i

        @pl.when(i == 0)
        def _():
            lc_ref[...] = jnp.zeros_like(lc_ref)
            nx_ref[...] = jnp.zeros_like(nx_ref)
            for ref in (dcw_ref, dcb_ref, dwra_ref, dwrx_ref, dbra_ref, dbrx_ref, dlam_ref):
                ref[...] = jnp.zeros_like(ref)

        xb = xg_ref[:, 0:D_RNN]
        gb = xg_ref[:, D_RNN:2 * D_RNN]
        prev8 = jnp.where(ti > 0, xp_ref[:, 0:D_RNN], 0.0)
        xc, xcb, r, ig, a, sq = _lru_gates(xb, prev8, cw_ref, cb_ref, wra_ref, wrx_ref, bra_ref, brx_ref, lsl_ref)
        hh = h_ref[...]
        hprev = _rows_before(hh, jnp.where(ti > 0, hp_ref[...], 0.0), 1)
        dy_ = dy_ref[...]
        cdf, th = _gelu(gb)
        c0 = math.sqrt(2.0 / math.pi)
        dgate = cdf + gb * (0.5 * (1.0 - th * th) * c0 * (1.0 + 3.0 * 0.044715 * gb * gb))
        dgb = dy_ * hh * dgate
        a_s[...] = a
        g_s[...] = dy_ * (gb * cdf)

        def chunk(cc, carry):
            o = pl.multiple_of((tm // 8 - 1 - cc) * 8, 8)
            av = a_s[pl.ds(o, 8), :]
            gv = g_s[pl.ds(o, 8), :]
            rows = [None] * 8
            for t in range(7, -1, -1):
                lam_t = gv[t:t + 1, :] + carry
                rows[t] = lam_t
                carry = av[t:t + 1, :] * lam_t
            l_s[pl.ds(o, 8), :] = jnp.concatenate(rows, axis=0)
            return carry

        carry = lax.fori_loop(0, tm // 8, chunk, lc_ref[0:1, :])
        lc_ref[0:1, :] = carry
        lam = l_s[...]
        da = lam * hprev
        dixc = lam * sq
        di = dixc * xc
        dxc = dixc * ig
        dsq = lam * (ig * xc)
        dla = da * a - dsq * (a * a / sq)
        dr = dla * (LRU_C * lsl_ref[...])
        dlam_ref[...] += jnp.sum(dla * (LRU_C * r), axis=0, keepdims=True)
        dpr = dr * r * (1.0 - r)
        dpi = di * ig * (1.0 - ig)
        dbra_ref[...] += jnp.sum(dpr, axis=0, keepdims=True)
        dbrx_ref[...] += jnp.sum(dpi, axis=0, keepdims=True)
        dprb = dpr.astype(BF16)
        dpib = dpi.astype(BF16)
        back = []
        for n in range(RNN_BLOCKS):
            sl = slice(RNN_W * n, RNN_W * (n + 1))
            dwra_ref[n] += _dot_tn(xcb[:, sl], dprb[:, sl])
            dwrx_ref[n] += _dot_tn(xcb[:, sl], dpib[:, sl])
            back.append(_dot_nt(dprb[:, sl], wra_ref[n]) + _dot_nt(dpib[:, sl], wrx_ref[n]))
        dxc = dxc + jnp.concatenate(back, axis=1)
        dcb_ref[...] += jnp.sum(dxc, axis=0, keepdims=True)
        next8 = nx_ref[...]
        dxb = cw_ref[3:4, :] * dxc
        dcw_ref[3:4, :] += jnp.sum(dxc * xb, axis=0, keepdims=True)
        for w in range(CONV_W - 1):
            k = CONV_W - 1 - w
            dcw_ref[w:w + 1, :] += jnp.sum(dxc * _rows_before(xb, prev8, k), axis=0, keepdims=True)
            dxb = dxb + cw_ref[w:w + 1, :] * _rows_after(dxc, next8, k)
        nx_ref[...] = dxc[0:8, :]
        dxg_ref[:, 0:D_RNN] = dxb.astype(BF16)
        dxg_ref[:, D_RNN:2 * D_RNN] = dgb.astype(BF16)

    rev = lambda i: (nt - 1 - i, 0)
    before = lambda i: (jnp.maximum((nt - 1 - i) * (tm // 8) - 1, 0), 0)
    vec = pl.BlockSpec((1, D_RNN), lambda i: (0, 0))
    wsp = pl.BlockSpec((RNN_BLOCKS, RNN_W, RNN_W), lambda i: (0, 0, 0))
    cwsp = pl.BlockSpec((CONV_W, D_RNN), lambda i: (0, 0))
    return pl.pallas_call(
        body, grid=(nt,),
        in_specs=[pl.BlockSpec((tm, D_RNN), rev), pl.BlockSpec((tm, 2 * D_RNN), rev), pl.BlockSpec((8, 2 * D_RNN), before),
                  pl.BlockSpec((tm, D_RNN), rev), pl.BlockSpec((8, D_RNN), before),
                  cwsp, vec, wsp, wsp, vec, vec, vec],
        out_specs=[pl.BlockSpec((tm, 2 * D_RNN), rev), cwsp, vec, wsp, wsp, vec, vec, vec],
        out_shape=[jax.ShapeDtypeStruct((S, 2 * D_RNN), BF16), jax.ShapeDtypeStruct((CONV_W, D_RNN), F32),
                   jax.ShapeDtypeStruct((1, D_RNN), F32), jax.ShapeDtypeStruct((RNN_BLOCKS, RNN_W, RNN_W), F32),
                   jax.ShapeDtypeStruct((RNN_BLOCKS, RNN_W, RNN_W), F32), jax.ShapeDtypeStruct((1, D_RNN), F32),
                   jax.ShapeDtypeStruct((1, D_RNN), F32), jax.ShapeDtypeStruct((1, D_RNN), F32)],
        scratch_shapes=[pltpu.VMEM((8, D_RNN), F32), pltpu.VMEM((8, D_RNN), F32), pltpu.VMEM((tm, D_RNN), F32),
                        pltpu.VMEM((tm, D_RNN), F32), pltpu.VMEM((tm, D_RNN), F32)],
        compiler_params=_cp("arbitrary"), name=name)(
            dy, xg, xg, h, h, p["conv_w"], p["conv_b"], p["w_ra"], p["w_rx"], p["b_ra"], p["b_rx"], p["lsl"])


def _place():
    x, y, c = lax.axis_index("x"), lax.axis_index("y"), lax.axis_index("c")
    chips = [(1 - x, y), (x, 1 - y), (1 - x, 1 - y)]
    return x, y, c, chips


def comm_ag(arrs):
    n = len(arrs)

    def copies(ins, outs, sems):
        send_sems, recv_sems, local_sems = sems
        x, y, c, chips = _place()
        me, sibling = (x, y, c), (x, y, 1 - c)

        def rows(a, px, py, pc):
            return outs[a].at[4 * px + 2 * py + pc]

        def copy(a, k, block, to, own=False):
            return pltpu.make_async_remote_copy(
                src_ref=ins[a] if own else rows(a, *block), dst_ref=rows(a, *block),
                send_sem=send_sems.at[k, a], recv_sem=recv_sems.at[k, a], device_id=to, device_id_type=MESH)

        mine = [pltpu.make_async_copy(ins[a], rows(a, *me), local_sems.at[a]) for a in range(n)]
        first = [copy(a, 1 + j, me, (*chip, c), own=True) for j, chip in enumerate(chips) for a in range(n)]
        first += [copy(a, 0, me, sibling, own=True) for a in range(n)]
        return copy, chips, c, me, sibling, mine, first

    def start(ins, outs, sems):
        _, _, _, _, _, mine, first = copies(ins, outs, sems)
        for cp in mine + first:
            cp.start()

    def relay(ins, outs, sems):
        copy, chips, c, me, sibling, _, _ = copies(ins, outs, sems)
        for j, chip in enumerate(chips):
            for a in range(n):
                copy(a, 1 + j, (*chip, c), me).wait_recv()
                copy(a, 4 + j, (*chip, c), sibling).start()

    def finish(ins, outs, sems):
        copy, chips, c, me, sibling, mine, first = copies(ins, outs, sems)
        passed = [copy(a, 4 + j, (*chip, c), sibling) for j, chip in enumerate(chips) for a in range(n)]
        for a in range(n):
            copy(a, 0, sibling, me).wait_recv()
        for j, chip in enumerate(chips):
            for a in range(n):
                copy(a, 4 + j, (*chip, 1 - c), me).wait_recv()
        for cp in first + passed:
            cp.wait_send()
        for cp in mine:
            cp.wait()

    return Comm(arrs, [jax.ShapeDtypeStruct((N_DEV,) + p.shape, p.dtype) for p in arrs],
                [pltpu.SemaphoreType.DMA((7, n)), pltpu.SemaphoreType.DMA((7, n)), pltpu.SemaphoreType.DMA((n,))],
                start, finish, relay=relay)


def comm_pair(sends):
    n = len(sends)

    def copies(ins, outs, sems):
        send_sems, recv_sems = sems
        x, y, c, _ = _place()
        return [pltpu.make_async_remote_copy(
            src_ref=ins[a].at[k, 1 - c], dst_ref=outs[a].at[k], send_sem=send_sems.at[k, a], recv_sem=recv_sems.at[k, a],
            device_id=(x, y, 1 - c), device_id_type=MESH) for k in range(4) for a in range(n)]

    def start(ins, outs, sems):
        for cp in copies(ins, outs, sems):
            cp.start()

    def finish(ins, outs, sems):
        for cp in copies(ins, outs, sems):
            cp.wait()

    return Comm(sends, [jax.ShapeDtypeStruct((4,) + s.shape[2:], s.dtype) for s in sends],
                [pltpu.SemaphoreType.DMA((4, n)), pltpu.SemaphoreType.DMA((4, n))], start, finish)


def pair_sum(sends, gots, out_dtypes, name, steps=2):
    n = len(sends)
    c = lax.axis_index("c").astype(jnp.int32).reshape(1)
    steps = [steps if s.shape[2] % (16 * steps) == 0 else 1 for s in sends]

    def body(c_ref, *refs):
        for a in range(n):
            refs[2 * n + a][...] = (refs[a][...] + refs[n + a][...]).astype(out_dtypes[a])

    def specs(a, own):
        _, _, R, C = sends[a].shape
        tr, last = R // steps[a], steps[a] - 1
        if own:
            return pl.BlockSpec((None, None, tr, C), lambda k, i, cr: (k, cr[0], jnp.minimum(i, last), 0))
        return pl.BlockSpec((None, tr, C), lambda k, i, cr: (k, jnp.minimum(i, last), 0))

    return pl.pallas_call(
        body,
        grid_spec=pltpu.PrefetchScalarGridSpec(
            num_scalar_prefetch=1, grid=(4, max(steps)),
            in_specs=[specs(a, True) for a in range(n)] + [specs(a, False) for a in range(n)],
            out_specs=[specs(a, False) for a in range(n)]),
        out_shape=[jax.ShapeDtypeStruct((4,) + s.shape[2:], d) for s, d in zip(sends, out_dtypes)],
        compiler_params=_cp("parallel", "arbitrary"), name=name)(c, *sends, *gots)


def comm_chip(items, bufs):
    ns = len(items)
    segs = [(i, b, s0, nr, d0) for i, (_, b, ranges) in enumerate(items) for (s0, nr, d0) in ranges]

    def copies(ins, outs, sems):
        send_sems, recv_sems, local_sems = sems
        x, y, c, chips = _place()
        mychip = 2 * x + y
        mine = [pltpu.make_async_copy(ins[i].at[mychip, pl.ds(s0, nr)], outs[b].at[mychip, pl.ds(d0, nr)], local_sems.at[q])
                for q, (i, b, s0, nr, d0) in enumerate(segs)]
        remote = [pltpu.make_async_remote_copy(
            src_ref=ins[i].at[2 * px + py, pl.ds(s0, nr)], dst_ref=outs[b].at[mychip, pl.ds(d0, nr)],
            send_sem=send_sems.at[j, q], recv_sem=recv_sems.at[j, q], device_id=(px, py, c), device_id_type=MESH)
            for j, (px, py) in enumerate(chips) for q, (i, b, s0, nr, d0) in enumerate(segs)]
        return mine, remote

    def start(ins, outs, sems):
        mine, remote = copies(ins, outs, sems)
        for cp in mine + remote:
            cp.start()

    def finish(ins, outs, sems):
        mine, remote = copies(ins, outs, sems)
        for cp in remote + mine:
            cp.wait()

    q = len(segs)
    return Comm([it[0] for it in items] + list(bufs), [jax.ShapeDtypeStruct(b.shape, b.dtype) for b in bufs],
                [pltpu.SemaphoreType.DMA((3, q)), pltpu.SemaphoreType.DMA((3, q)), pltpu.SemaphoreType.DMA((q,))],
                start, finish, aliases={ns + b: b for b in range(len(bufs))})


def adamw(parts, row0, w, m, v, tr, name):
    C = w.shape[-1]
    rows = w.size // C
    off = row0 // tr

    def body(p_ref, w_ref, m_ref, v_ref, g_ref, d_ref, nm_ref, nv_ref):
        g = ((p_ref[0].astype(F32) + p_ref[1].astype(F32)) + p_ref[2].astype(F32)) + p_ref[3].astype(F32)
        m2 = ADAM_B1 * m_ref[...] + (1.0 - ADAM_B1) * g
        v2 = ADAM_B2 * v_ref[...] + (1.0 - ADAM_B2) * (g * g)
        mh = m2 / (1.0 - ADAM_B1 ** ADAM_STEP)
        vh = v2 / (1.0 - ADAM_B2 ** ADAM_STEP)
        g_ref[...] = g
        d_ref[...] = -ADAM_LR * (mh / (jnp.sqrt(vh) + ADAM_EPS) + ADAM_WD * w_ref[...])
        nm_ref[...] = m2
        nv_ref[...] = v2

    row = pl.BlockSpec((tr, C), lambda i: (i, 0))
    shp = jax.ShapeDtypeStruct((rows, C), F32)
    outs = pl.pallas_call(
        body, grid=(rows // tr,),
        in_specs=[pl.BlockSpec((4, tr, C), lambda i: (0, off + i, 0)), row, row, row],
        out_specs=[row, row, row, row], out_shape=[shp, shp, shp, shp],
        compiler_params=_cp("parallel"), name=name)(parts, w.reshape(rows, C), m.reshape(rows, C), v.reshape(rows, C))
    return [o.reshape(w.shape) for o in outs]


def _full_from_gathered(flat, shape, axis):
    t = jnp.moveaxis(flat.reshape((N_DEV,) + shape), 0, axis)
    return t.reshape(shape[:axis] + (N_DEV * shape[axis],) + shape[axis + 1:])


def _shards_of_full(full, shape, axis):
    t = full.reshape(shape[:axis] + (N_DEV, shape[axis]) + shape[axis + 1:])
    return jnp.moveaxis(t, axis, 0).reshape(N_DEV, -1)


def _small_pack(args, prefix):
    flat = jnp.concatenate([args[prefix + n].reshape(-1) for n, _, _ in SMALL] + [args[prefix + "attn_sinks"].reshape(-1)])
    return jnp.pad(flat, (0, SM_ROWS * 1024 - flat.shape[0])).reshape(SM_ROWS, 1024)


def _small_unpack(pack):
    flat = pack.reshape(-1)
    out, off = {}, 0
    for n, shape, _ in SMALL:
        size = math.prod(shape)
        out[n] = flat[off:off + size].reshape(shape)
        off += size
    out["attn_sinks"] = flat[SINK_OFF:SINK_OFF + 32].reshape(2, 16)
    return out


def _tn_tile(n):
    return next(t for t in (1408, 1024, 768, 512, 256, 128) if n % t == 0)


def _dw(a, b, name, planes=1, shard=None, comm=None):
    return mm_tn(a, b, name, _tn_tile(a.shape[1]), _tn_tile(b.shape[-1]), planes=planes, shard=shard, comm=comm)


class _NoExchange:
    def __init__(self, layers):
        self.layers, self.grads = layers, [{} for _ in range(DEPTH)]

    def weights(self, l, piece):
        return self.layers[l]

    def fwd_slot(self):
        return None

    def fwd_done(self, outs):
        pass

    def loss_ready(self, loss):
        pass

    def bwd_slot(self, slot):
        return None

    def bwd_done(self, slot, outs):
        pass

    def grads_ready(self, l, part, g):
        self.grads[l].update(g)


def _local_step(x, target, sched, sinks):
    S = x.shape[0]
    tq = min(512, S)
    cosf, sinf = _rope_tables(S)
    saved = []
    h, hb = x, x.astype(BF16)
    for l in range(DEPTH):
        j = l // 2
        wl = {}
        sv = {"h0b": hb, "wl": wl}
        wl.update(sched.weights(l, "a"))
        (sv["gu1"], a1), co = ffn_up(hb, wl["w_in1"], f"ffn1_up_{l}", comm=sched.fwd_slot())
        sched.fwd_done(co)
        sv["a1"] = a1
        wl.update(sched.weights(l, "b"))
        (sv["z1"], h, hb), _ = mm_ln(a1, _rows_view(wl["r_ff"], FF_SHARD, 0), h, wl["ln_g"][0], wl["ln_b"][0], 0.5,
                                     f"ffn1_down_ln_{l}")
        sv["h1b"] = hb
        if l % 2 == 0:
            qkv = mm_plain(hb, _mat_view(wl["m_c"]), F32, f"attn_qkv_{l}")
            qr, kr, vv = rope_fwd(qkv, cosf, sinf, f"rope_{l}")
            mix_in, co = attn_fwd(qr, kr, vv, sinks[j], f"attn_core_{l}", tq, comm=sched.fwd_slot())
            sv.update(qr=qr, kr=kr, vv=vv)
        else:
            xg = mm_plain(hb, _mat_view(wl["m_c"]), F32, f"lru_in_{l}")
            (mix_in, hstate), co = lru_fwd(xg, wl["lru"], f"lru_core_{l}", comm=sched.fwd_slot())
            sv.update(xg=xg, hstate=hstate)
        sched.fwd_done(co)
        sv["mix_in"] = mix_in
        (sv["z2"], h, hb), _ = mm_ln(mix_in, _rows_view(wl["m_o"], MIX_SHARD, 0), h, wl["ln_g"][1], wl["ln_b"][1], 1.0,
                                     f"mix_out_ln_{l}")
        sv["h2b"] = hb
        wl.update(sched.weights(l, "c"))
        (sv["gu2"], a2), co = ffn_up(hb, wl["w_in2"], f"ffn2_up_{l}", comm=sched.fwd_slot())
        sched.fwd_done(co)
        sv["a2"] = a2
        (sv["z3"], h, hb), _ = mm_ln(a2, _rows_view(wl["r_ff"], FF_SHARD, FF_SHARD), h, wl["ln_g"][2], wl["ln_b"][2], 0.5,
                                     f"ffn2_down_ln_{l}")
        saved.append(sv)

    dy, lvec = loss_head(h, target, "loss_head")
    loss = lvec[0, 0]
    sched.loss_ready(loss)

    def hosted(slot, on, fn):
        comm = sched.bwd_slot(slot) if on else None
        out, co = fn(comm)
        if comm is not None:
            sched.bwd_done(slot, co)
        return out

    def ffn_bwd(dy, z, g, gu, a, xin_b, r_ff, blk, w_in, tag, host):
        dz, dzb, dg, db = ln_bwd(dy, z, g, 0.5, f"ln_bwd_{tag}")
        dh, _ = ffn_mid_bwd(dzb, r_ff, blk, gu, f"ffn_mid_bwd_{tag}")
        d_wout = hosted("pair", host, lambda cm: _dw(a, dzb, f"dw_out_{tag}", comm=cm))
        d_win = hosted("chip_x", host, lambda cm: _dw(xin_b, dh, f"dw_in_{tag}", planes=2, shard=2 * D_FF // N_DEV, comm=cm))
        dx = hosted("chip_y", host, lambda cm: mm_res2_nt(dh, _mat_view(w_in, D_FF, 0), _mat_view(w_in, D_FF, 1), dz, ALPHA,
                                                          f"ffn_dx_{tag}", comm=cm))
        return dx, dg, db, d_wout, d_win

    for l in reversed(range(DEPTH)):
        j = l // 2
        sv = saved[l]
        wl = sv["wl"]
        gl = {}
        dg, db = [None] * 3, [None] * 3
        dy, dg[2], db[2], gl["w_out2"], gl["w_in2"] = ffn_bwd(
            dy, sv["z3"], wl["ln_g"][2], sv["gu2"], sv["a2"], sv["h2b"], wl["r_ff"], 1, wl["w_in2"], f"2_{l}", True)
        dz, dzb, dg[1], db[1] = ln_bwd(dy, sv["z2"], wl["ln_g"][1], 1.0, f"ln_bwd_mix_{l}")
        w_mix = _rows_view(wl["m_o"], MIX_SHARD, 0)
        if l % 2 == 0:
            gl["mix_out"], _ = _dw(sv["mix_in"], dzb, f"dw_o_{l}")
            do = mm_plain(dzb, w_mix, BF16, f"attn_do_{l}", nt=True)
            dq, dkc, dkp, dvc, dvp, dsk = attn_bwd(sv["qr"], sv["kr"], sv["vv"], do, sinks[j], f"attn_core_bwd_{l}", tq)
            gl["sinks"] = dsk[:, :GROUP, 0].reshape(N_HEADS)
            dmid = rope_bwd(dq, dkc, dkp, dvc, dvp, cosf, sinf, f"rope_bwd_{l}", tq)
            gl["mix_in"], _ = _dw(sv["h1b"], dmid, f"dw_qkv_{l}", shard=QKV // N_DEV)
        else:
            gl["mix_out"], _ = _dw(sv["mix_in"], dzb, f"dw_lru_out_{l}")
            dyl = mm_plain(dzb, w_mix, F32, f"lru_dy_{l}", nt=True)
            dmid, dcw, dcb, dwra, dwrx, dbra, dbrx, dlam = lru_bwd(dyl, sv["xg"], sv["hstate"], wl["lru"], f"lru_core_bwd_{l}")
            gl.update(conv_w=dcw, conv_b=dcb[0], w_ra=dwra, w_rx=dwrx, b_ra=dbra[0], b_rx=dbrx[0],
                      lam=dlam[0] * wl["lru"]["sig_neg"])
            gl["mix_in"], _ = _dw(sv["h1b"], dmid, f"dw_lru_in_{l}", shard=2 * D_RNN // N_DEV)
        dy = mm_res_nt(dmid, _mat_view(wl["m_c"]), dz, ALPHA, f"mix_dx_{l}")
        sched.grads_ready(l, "hi", gl)
        lo = {}
        dy, dg[0], db[0], lo["w_out1"], lo["w_in1"] = ffn_bwd(
            dy, sv["z1"], wl["ln_g"][0], sv["gu1"], sv["a1"], sv["h0b"], wl["r_ff"], 0, wl["w_in1"], f"1_{l}", True)
        lo["ln_g"], lo["ln_b"] = jnp.concatenate(dg, axis=0), jnp.concatenate(db, axis=0)
        sched.grads_ready(l, "lo", lo)
    return loss, dy


def _lru_params(full, j, w_ra, w_rx):
    lam = full["lru_lambda"][j]
    return {
        "conv_w": full["lru_conv_w"][j], "conv_b": full["lru_conv_b"][j].reshape(1, -1),
        "w_ra": w_ra, "w_rx": w_rx,
        "b_ra": full["lru_b_ra"][j].reshape(1, -1), "b_rx": full["lru_b_rx"][j].reshape(1, -1),
        "lsl": jax.nn.log_sigmoid(lam).reshape(1, -1), "sig_neg": jax.nn.sigmoid(-lam),
    }


def _row_shards(f):
    return f.reshape(N_DEV, -1, f.shape[1])


def _gate_shards(w):
    return w.reshape(RNN_BLOCKS, N_DEV, RNN_W // N_DEV, RNN_W).transpose(1, 0, 2, 3).reshape(N_DEV, -1, RNN_W)


class _Fsdp:
    def __init__(self, args):
        self.args = args
        self.b16 = lambda a: a.astype(BF16)
        self.sm = _small_pack(args, "")
        self.sent = {l: self._ag_arrays(l) for l in range(DEPTH)}
        self.queue = [(l, p) for l in range(DEPTH) for p in "abc"][1:]
        self.hosting = None
        got = run_comm(comm_ag(self.sent[0]["a"] + [self.sm]), "all_gather_first")
        self.raw = {0: {"a": got[:-1]}}
        gflat = got[-1].reshape(N_DEV, SM_ROWS * 1024)
        self.small, off = {}, 0
        for n, shape, axis in SMALL:
            size = math.prod(shape)
            self.small[n] = _full_from_gathered(gflat[:, off:off + size], shape, axis)
            off += size
        self.bufs = {"r": jnp.zeros((4, R_ROWS, D_MODEL), BF16), "c704": jnp.zeros((4, 2 * DEPTH * D_MODEL, 704), BF16),
                     "c192": jnp.zeros((4, 2 * D_MODEL, 192), BF16), "c256": jnp.zeros((4, C256_ROWS, RNN_W), BF16),
                     "sm": jnp.zeros((4, SM_ROWS, 1024), F32)}
        self.pending = None
        self.small_g = [{} for _ in range(DEPTH)]
        self.loss = None

    def _ag_arrays(self, l):
        a, b16, j = self.args, self.b16, l // 2
        r_ff = jnp.concatenate([b16(a["ffn1_w_out"][l]), b16(a["ffn2_w_out"][l])], axis=0)
        if l % 2 == 0:
            return {"a": [b16(a["ffn1_w_in"][l])], "b": [r_ff, b16(a["attn_w_o"][j]), b16(a["attn_w_qkv"][j])],
                    "c": [b16(a["ffn2_w_in"][l])]}
        gates = jnp.concatenate([b16(a["lru_w_ra"][j]).reshape(-1, RNN_W), b16(a["lru_w_rx"][j]).reshape(-1, RNN_W)], axis=0)
        return {"a": [b16(a["ffn1_w_in"][l])], "b": [r_ff, b16(a["lru_w_out"][j]), b16(a["lru_w_in"][j]), gates],
                "c": [b16(a["ffn2_w_in"][l])]}

    def weights(self, l, piece):
        raw, j = self.raw[l][piece], l // 2
        full = lambda g: g.transpose(1, 0, 2).reshape(g.shape[1], -1)
        if piece == "a":
            return {"w_in1": full(raw[0]), "ln_g": self.small["ln_g"][l], "ln_b": self.small["ln_b"][l]}
        if piece == "c":
            return {"w_in2": full(raw[0])}
        wl = {"r_ff": raw[0], "m_o": raw[1], "m_c": full(raw[2])}
        if l % 2:
            g = raw[3].reshape(N_DEV, 2, RNN_BLOCKS, RNN_W // N_DEV, RNN_W).transpose(1, 2, 0, 3, 4)
            g = g.reshape(2, RNN_BLOCKS, RNN_W, RNN_W)
            wl["lru"] = _lru_params(self.small, j, g[0], g[1])
        return wl

    def fwd_slot(self):
        if not self.queue:
            return None
        self.hosting = self.queue.pop(0)
        l, piece = self.hosting
        return comm_ag(self.sent[l][piece])

    def fwd_done(self, outs):
        if self.hosting is not None:
            l, piece = self.hosting
            self.raw.setdefault(l, {})[piece] = outs
            self.hosting = None

    def loss_ready(self, loss):
        self.loss = loss

    def _ranges(self, l, part):
        j = l // 2
        if part == "lo":
            return [("r", [(0, FF_SHARD, l * FF_SHARD)]), ("c704", [(0, D_MODEL, l * D_MODEL)])]
        where = [("r", [(0, FF_SHARD, R_FFN2 + l * FF_SHARD)]),
                 ("r", [(0, MIX_SHARD, (R_LRU if l % 2 else R_ATTN) + j * MIX_SHARD)]),
                 ("c704", [(0, D_MODEL, (DEPTH + l) * D_MODEL)])]
        if l % 2:
            gr = RNN_BLOCKS * (RNN_W // N_DEV)
            return where + [("c256", [(0, D_MODEL, j * D_MODEL)]), ("c256", [(0, gr, C256_RA + j * gr)]),
                            ("c256", [(0, gr, C256_RX + j * gr)])]
        return where + [("c192", [(0, D_MODEL, j * D_MODEL)])]

    def grads_ready(self, l, part, g):
        self.small_g[l].update(g)
        if part == "lo":
            ts = [_row_shards(g["w_out1"]), g["w_in1"]]
        else:
            ts = [_row_shards(g["w_out2"]), _row_shards(g["mix_out"]), g["w_in2"], g["mix_in"]]
            if l % 2:
                ts += [_gate_shards(g["w_ra"]), _gate_shards(g["w_rx"])]
        sends = [t.reshape((4, 2) + t.shape[1:]) for t in ts]
        if (l, part) == (0, "lo"):
            self._last(sends)
        else:
            self.pending = {"tag": f"{l}{part}", "sends": sends, "where": self._ranges(l, part)}

    def _pair_sums(self, tag, sends, gots):
        return pair_sum(sends, gots, [BF16] * len(sends), f"pair_sum_{tag}")

    def _chip(self, parts, where, pick):
        names = list(dict.fromkeys(where[i][0] for i in pick))
        return names, comm_chip([(parts[i], names.index(where[i][0]), where[i][1]) for i in pick], [self.bufs[n] for n in names])

    def bwd_slot(self, slot):
        p = self.pending
        if p is None:
            return None
        if slot == "pair":
            return comm_pair(p["sends"])
        wide = [i for i, (name, _) in enumerate(p["where"]) if name == "c704"]
        pick = wide if slot == "chip_x" else [i for i in range(len(p["sends"])) if i not in wide]
        p["names"], comm = self._chip(p["parts"], p["where"], pick)
        return comm

    def bwd_done(self, slot, outs):
        p = self.pending
        if slot == "pair":
            p["parts"] = self._pair_sums(p["tag"], p["sends"], outs)
            return
        for n, o in zip(p["names"], outs):
            self.bufs[n] = o
        if slot == "chip_y":
            self.pending = None

    def _last(self, sends):
        sg = self.small_g
        stack = lambda key, ls: jnp.stack([sg[l][key] for l in ls])
        every, lru = range(DEPTH), (1, 3)
        small = {"ln_g": stack("ln_g", every), "ln_b": stack("ln_b", every), "lru_conv_w": stack("conv_w", lru),
                 "lru_conv_b": stack("conv_b", lru), "lru_b_ra": stack("b_ra", lru), "lru_b_rx": stack("b_rx", lru),
                 "lru_lambda": stack("lam", lru)}
        tail = jnp.concatenate([stack("sinks", (0, 2)).reshape(-1), self.loss.reshape(1)])
        tail = jnp.pad(tail, (0, SM_ROWS * 1024 - SMALL_N - tail.shape[0]))
        s_sm = jnp.concatenate([_shards_of_full(small[n], shape, axis) for n, shape, axis in SMALL]
                               + [jnp.broadcast_to(tail, (N_DEV, tail.shape[0]))], axis=1).reshape(4, 2, SM_ROWS, 1024)
        gots = run_comm(comm_pair(sends + [s_sm]), "pair_exchange_last")
        parts = pair_sum(sends + [s_sm], gots, [BF16, BF16, F32], "pair_sum_last")
        names, comm = self._chip(parts, self._ranges(0, "lo") + [("sm", [(0, SM_ROWS, 0)])], [0, 1, 2])
        for n, o in zip(names, run_comm(comm, "chip_exchange_last")):
            self.bufs[n] = o


def kernel(x, ffn1_w_in, ffn1_w_out, ffn2_w_in, ffn2_w_out, ln_g, ln_b, attn_w_qkv, attn_sinks, attn_w_o, lru_w_in, lru_conv_w, lru_conv_b, lru_w_ra, lru_b_ra, lru_w_rx, lru_b_rx, lru_lambda, lru_w_out, loss_target, m_ffn1_w_in, m_ffn1_w_out, m_ffn2_w_in, m_ffn2_w_out, m_ln_g, m_ln_b, m_attn_w_qkv, m_attn_sinks, m_attn_w_o, m_lru_w_in, m_lru_conv_w, m_lru_conv_b, m_lru_w_ra, m_lru_b_ra, m_lru_w_rx, m_lru_b_rx, m_lru_lambda, m_lru_w_out, v_ffn1_w_in, v_ffn1_w_out, v_ffn2_w_in, v_ffn2_w_out, v_ln_g, v_ln_b, v_attn_w_qkv, v_attn_sinks, v_attn_w_o, v_lru_w_in, v_lru_conv_w, v_lru_conv_b, v_lru_w_ra, v_lru_b_ra, v_lru_w_rx, v_lru_b_rx, v_lru_lambda, v_lru_w_out):
    args = dict(locals())

    sched = _Fsdp(args)
    _, dx = _local_step(x[0], loss_target[0], sched, attn_sinks)
    p_r, p704, p192, p256, p_sm = (sched.bufs[n] for n in ("r", "c704", "c192", "c256", "sm"))
    sm = sched.sm

    fam = lambda n: (args[n], args["m_" + n], args["v_" + n])
    res = {
        "ffn1_w_out": adamw(p_r, 0, *fam("ffn1_w_out"), FF_SHARD, "adamw_ffn1_w_out"),
        "ffn2_w_out": adamw(p_r, R_FFN2, *fam("ffn2_w_out"), FF_SHARD, "adamw_ffn2_w_out"),
        "attn_w_o": adamw(p_r, R_ATTN, *fam("attn_w_o"), 2 * MIX_SHARD, "adamw_attn_w_o"),
        "lru_w_out": adamw(p_r, R_LRU, *fam("lru_w_out"), 2 * MIX_SHARD, "adamw_lru_w_out"),
        "ffn1_w_in": adamw(p704, 0, *fam("ffn1_w_in"), 512, "adamw_ffn1_w_in"),
        "ffn2_w_in": adamw(p704, DEPTH * D_MODEL, *fam("ffn2_w_in"), 512, "adamw_ffn2_w_in"),
        "attn_w_qkv": adamw(p192, 0, *fam("attn_w_qkv"), 512, "adamw_attn_w_qkv"),
        "lru_w_in": adamw(p256, 0, *fam("lru_w_in"), 512, "adamw_lru_w_in"),
        "lru_w_ra": adamw(p256, C256_RA, *fam("lru_w_ra"), 256, "adamw_lru_w_ra"),
        "lru_w_rx": adamw(p256, C256_RX, *fam("lru_w_rx"), 256, "adamw_lru_w_rx"),
    }
    sm_out = adamw(p_sm, 0, sm, _small_pack(args, "m_"), _small_pack(args, "v_"), SM_ROWS, "adamw_small")
    for k, pack in enumerate(sm_out):
        for n, val in _small_unpack(pack).items():
            res.setdefault(n, [None] * 4)[k] = val
    loss_total = sm_out[0].reshape(-1)[LOSS_OFF]
    out = [loss_total, dx[None]]
    for k in range(4):
        out += [res[n][k] for n in WEIGHTS]
    return tuple(out)
```

```python
import math

import jax
import jax.numpy as jnp
import numpy as np
from jax import lax
from jax.experimental import pallas as pl
from jax.experimental.pallas import tpu as pltpu

F32 = jnp.float32
BF16 = jnp.bfloat16

D_MODEL = 1024
DEPTH = 4
N_HEADS = 16
N_KV = 4
HEAD_DIM = 64
GROUP = 4
BLOCK = 128
ROPE_THETA = 10000.0
D_RNN = 1024
RNN_BLOCKS = 4
RNN_W = 256
CONV_W = 4
LRU_C = 8.0
D_FF = 2816
ALPHA = (2.0 * DEPTH) ** 0.25
LN_EPS = 1e-5
QKV = (N_HEADS + 2 * N_KV) * HEAD_DIM
N_DEV = 8

ADAM_LR = 0.001
ADAM_B1 = 0.9
ADAM_B2 = 0.999
ADAM_EPS = 1e-08
ADAM_WD = 0.01
ADAM_STEP = 10

VMEM_LIMIT = 52 * 1024 * 1024
NEG = float(np.finfo(np.float32).min)

MESH = pl.DeviceIdType.MESH
ANY = pl.BlockSpec(memory_space=pl.ANY)

FF_SHARD = D_FF // N_DEV
MIX_SHARD = D_MODEL // N_DEV
R_FFN2 = DEPTH * FF_SHARD
R_ATTN = 2 * DEPTH * FF_SHARD
R_LRU = R_ATTN + 2 * MIX_SHARD
R_ROWS = R_LRU + 2 * MIX_SHARD
C256_RA = 2 * D_MODEL
C256_RX = C256_RA + 2 * RNN_BLOCKS * (RNN_W // N_DEV)
C256_ROWS = C256_RX + 2 * RNN_BLOCKS * (RNN_W // N_DEV)

SMALL = (
    ("ln_g", (4, 3, 128), 2),
    ("ln_b", (4, 3, 128), 2),
    ("lru_conv_w", (2, 4, 128), 2),
    ("lru_conv_b", (2, 128), 1),
    ("lru_b_ra", (2, 128), 1),
    ("lru_b_rx", (2, 128), 1),
    ("lru_lambda", (2, 128), 1),
)
WEIGHTS = ("ffn1_w_in", "ffn1_w_out", "ffn2_w_in", "ffn2_w_out", "ln_g", "ln_b", "attn_w_qkv", "attn_sinks",
           "attn_w_o", "lru_w_in", "lru_conv_w", "lru_conv_b", "lru_w_ra", "lru_b_ra", "lru_w_rx", "lru_b_rx",
           "lru_lambda", "lru_w_out")
SMALL_N = sum(math.prod(s) for _, s, _ in SMALL)
SINK_OFF = SMALL_N
LOSS_OFF = SMALL_N + 32
SM_ROWS = 8


def _cp(*sem):
    return pltpu.CompilerParams(dimension_semantics=sem, vmem_limit_bytes=VMEM_LIMIT)


def _dot(a, b):
    return jnp.dot(a, b, preferred_element_type=F32)


def _dot_tn(a, b):
    return lax.dot_general(a, b, (((0,), (0,)), ((), ())), preferred_element_type=F32)


def _dot_nt(a, b):
    return lax.dot_general(a, b, (((1,), (1,)), ((), ())), preferred_element_type=F32)


def _col_chunks(cols, size=512):
    return [slice(c, min(c + size, cols)) for c in range(0, cols, size)]


def _ln(z, g, b):
    mu = jnp.mean(z, axis=-1, keepdims=True)
    xc = z - mu
    var = jnp.mean(xc * xc, axis=-1, keepdims=True)
    return xc * lax.rsqrt(var + LN_EPS) * g + b


def _rows_view(pack, rows_per_dev, row0):
    return (pack, (N_DEV, rows_per_dev, D_MODEL), (0, row0 // rows_per_dev, 0), (N_DEV * rows_per_dev, D_MODEL))


def _mat_view(arr, cols=None, cblk=0):
    k, n = arr.shape
    cols = n if cols is None else cols
    return (arr, (k, cols), (0, cblk), (k, cols))


def _vspec(view):
    _, bshape, bidx, _ = view
    return pl.BlockSpec(bshape, lambda *_: bidx)


def _vload(view, ref):
    return ref[...].reshape(view[3])


class Comm:
    def __init__(self, ins, out_shapes, sems, start, finish, aliases=None, relay=None):
        self.ins, self.out_shapes, self.sems = list(ins), list(out_shapes), list(sems)
        self.start, self.finish, self.aliases = start, finish, dict(aliases or {})
        self.relay = relay


def _call(body, *, grid, in_specs, out_specs, out_shape, operands, name, sem, scratch=(), comm=None):
    n_in, n_out, n_scr = len(in_specs), len(out_specs), len(scratch)
    if comm is None:
        return pl.pallas_call(body, grid=grid, in_specs=list(in_specs), out_specs=list(out_specs),
                              out_shape=list(out_shape), scratch_shapes=list(scratch), compiler_params=_cp(*sem),
                              name=name)(*operands), []
    nci, nco = len(comm.ins), len(comm.out_shapes)

    def hosted(*refs):
        ins, cins = refs[:n_in], refs[n_in:n_in + nci]
        o0 = n_in + nci
        outs, couts = refs[o0:o0 + n_out], refs[o0 + n_out:o0 + n_out + nco]
        s0 = o0 + n_out + nco
        scr, csems = refs[s0:s0 + n_scr], refs[s0 + n_scr:]
        step = 0
        for ax, size in enumerate(grid):
            step = step * size + pl.program_id(ax)
        steps = math.prod(grid)

        @pl.when(step == 0)
        def _():
            comm.start(cins, couts, csems)

        if comm.relay is not None and steps >= 4:
            @pl.when(step == (3 * steps) // 4)
            def _():
                comm.relay(cins, couts, csems)

        body(*ins, *outs, *scr)

        @pl.when(step == steps - 1)
        def _():
            if comm.relay is not None and steps < 4:
                comm.relay(cins, couts, csems)
            comm.finish(cins, couts, csems)

    res = pl.pallas_call(
        hosted, grid=grid, in_specs=list(in_specs) + [ANY] * nci, out_specs=list(out_specs) + [ANY] * nco,
        out_shape=list(out_shape) + comm.out_shapes, scratch_shapes=list(scratch) + comm.sems,
        input_output_aliases={n_in + i: n_out + o for i, o in comm.aliases.items()},
        compiler_params=_cp(*(("arbitrary",) * len(grid))), name=name)(*operands, *comm.ins)
    return res[:n_out], res[n_out:]


def run_comm(comm, name):
    nci, nco = len(comm.ins), len(comm.out_shapes)

    def body(*refs):
        cins, couts, csems = refs[:nci], refs[nci:nci + nco], refs[nci + nco:]
        comm.start(cins, couts, csems)
        if comm.relay is not None:
            comm.relay(cins, couts, csems)
        comm.finish(cins, couts, csems)

    return pl.pallas_call(
        body, in_specs=[ANY] * nci, out_specs=[ANY] * nco, out_shape=comm.out_shapes, scratch_shapes=comm.sems,
        input_output_aliases=dict(comm.aliases), name=name)(*comm.ins)


def mm_plain(a, wv, out_dtype, name, nt=False, tm=512):
    S, K = a.shape
    N = wv[3][0] if nt else wv[3][1]
    tm = min(tm, S)
    dot = _dot_nt if nt else _dot

    def body(a_ref, w_ref, o_ref):
        o_ref[...] = dot(a_ref[...], _vload(wv, w_ref)).astype(out_dtype)

    return pl.pallas_call(
        body, grid=(S // tm,),
        in_specs=[pl.BlockSpec((tm, K), lambda i: (i, 0)), _vspec(wv)],
        out_specs=pl.BlockSpec((tm, N), lambda i: (i, 0)),
        out_shape=jax.ShapeDtypeStruct((S, N), out_dtype),
        compiler_params=_cp("parallel"), name=name)(a, wv[0])


def mm_res_nt(a, wv, r, alpha, name, tm=512):
    S, K = a.shape
    N = wv[3][0]
    tm = min(tm, S)

    def body(a_ref, w_ref, r_ref, o_ref):
        o_ref[...] = _dot_nt(a_ref[...], _vload(wv, w_ref)) + alpha * r_ref[...]

    return pl.pallas_call(
        body, grid=(S // tm,),
        in_specs=[pl.BlockSpec((tm, K), lambda i: (i, 0)), _vspec(wv), pl.BlockSpec((tm, N), lambda i: (i, 0))],
        out_specs=pl.BlockSpec((tm, N), lambda i: (i, 0)),
        out_shape=jax.ShapeDtypeStruct((S, N), F32),
        compiler_params=_cp("parallel"), name=name)(a, wv[0], r)


def mm_res2_nt(a3, wv0, wv1, r, alpha, name, tm=512, comm=None):
    _, S, K = a3.shape
    N = wv0[3][0]
    tm = min(tm, S)

    def body(a_ref, w0_ref, w1_ref, r_ref, o_ref):
        o_ref[...] = (_dot_nt(a_ref[0], _vload(wv0, w0_ref)) + _dot_nt(a_ref[1], _vload(wv1, w1_ref))
                      + alpha * r_ref[...])

    (out,), couts = _call(
        body, grid=(S // tm,),
        in_specs=[pl.BlockSpec((2, tm, K), lambda i: (0, i, 0)), _vspec(wv0), _vspec(wv1),
                  pl.BlockSpec((tm, N), lambda i: (i, 0))],
        out_specs=[pl.BlockSpec((tm, N), lambda i: (i, 0))],
        out_shape=[jax.ShapeDtypeStruct((S, N), F32)],
        operands=(a3, wv0[0], wv1[0], r), name=name, sem=("parallel",), comm=comm)
    return out, couts


def mm_ln(a, wv, h, g, b, scale, name, tm=512, comm=None):
    S, K = a.shape
    tm = min(tm, S)

    def body(a_ref, w_ref, h_ref, g_ref, b_ref, z_ref, y_ref, yb_ref):
        z = ALPHA * h_ref[...] + scale * _dot(a_ref[...], _vload(wv, w_ref))
        y = _ln(z, g_ref[...], b_ref[...])
        z_ref[...] = z
        y_ref[...] = y
        yb_ref[...] = y.astype(BF16)

    row = pl.BlockSpec((tm, D_MODEL), lambda i: (i, 0))
    vec = pl.BlockSpec((1, D_MODEL), lambda i: (0, 0))
    return _call(
        body, grid=(S // tm,),
        in_specs=[pl.BlockSpec((tm, K), lambda i: (i, 0)), _vspec(wv), row, vec, vec],
        out_specs=[row, row, row],
        out_shape=[jax.ShapeDtypeStruct((S, D_MODEL), F32), jax.ShapeDtypeStruct((S, D_MODEL), F32),
                   jax.ShapeDtypeStruct((S, D_MODEL), BF16)],
        operands=(a, wv[0], h, g.reshape(1, -1), b.reshape(1, -1)), name=name, sem=("parallel",), comm=comm)


def ffn_up(xb, w_in, name, tm=512, tn=1408, comm=None):
    S = xb.shape[0]
    tm = min(tm, S)
    nj = D_FF // tn

    def body(x_ref, wg_ref, wu_ref, jac_ref, a_ref):
        x = x_ref[...]
        g = _dot(x, wg_ref[...])
        u = _dot(x, wu_ref[...])
        sg = jax.nn.sigmoid(g)
        t = g * sg
        jac_ref[0] = (u * (sg * ((g - t) + 1.0))).astype(BF16)
        jac_ref[1] = t.astype(BF16)
        a_ref[...] = (t * u).astype(BF16)

    return _call(
        body, grid=(nj, S // tm),
        in_specs=[pl.BlockSpec((tm, D_MODEL), lambda j, i: (i, 0)),
                  pl.BlockSpec((D_MODEL, tn), lambda j, i: (0, j)),
                  pl.BlockSpec((D_MODEL, tn), lambda j, i: (0, nj + j))],
        out_specs=[pl.BlockSpec((2, tm, tn), lambda j, i: (0, i, j)), pl.BlockSpec((tm, tn), lambda j, i: (i, j))],
        out_shape=[jax.ShapeDtypeStruct((2, S, D_FF), BF16), jax.ShapeDtypeStruct((S, D_FF), BF16)],
        operands=(xb, w_in, w_in), name=name, sem=("parallel", "parallel"), comm=comm)


def ffn_mid_bwd(dfb, r_ff, blk, gu, name, tm=512, comm=None):
    S = dfb.shape[0]
    tm = min(tm, S)
    tn = 4 * FF_SHARD

    def body(df_ref, w_ref, gu_ref, dh_ref):
        w = w_ref[...].reshape(tn, D_MODEL)
        df = df_ref[...]
        for cols in _col_chunks(tn):
            da = _dot_nt(df, w[cols, :])
            dh_ref[0, :, cols] = (da * gu_ref[0, :, cols].astype(F32)).astype(BF16)
            dh_ref[1, :, cols] = (da * gu_ref[1, :, cols].astype(F32)).astype(BF16)

    gspec = pl.BlockSpec((2, tm, tn), lambda j, i: (0, i, j))
    (out,), couts = _call(
        body, grid=(2, S // tm),
        in_specs=[pl.BlockSpec((tm, D_MODEL), lambda j, i: (i, 0)),
                  pl.BlockSpec((4, FF_SHARD, D_MODEL), lambda j, i: (j, blk, 0)), gspec],
        out_specs=[gspec],
        out_shape=[jax.ShapeDtypeStruct((2, S, D_FF), BF16)],
        operands=(dfb, r_ff, gu), name=name, sem=("parallel", "parallel"), comm=comm)
    return out, couts


def mm_tn(a, b, name, tm, tn, ts=2048, planes=1, shard=None, comm=None):
    S, M = a.shape
    N = b.shape[-1] * planes
    ts = min(ts, S)
    per = b.shape[-1] // tn
    ns = S // ts

    if shard is None:
        def body(a_ref, b_ref, o_ref):
            @pl.when(pl.program_id(2) == 0)
            def _():
                o_ref[...] = jnp.zeros_like(o_ref)

            o_ref[...] += _dot_tn(a_ref[...], b_ref[...])

        out_spec = pl.BlockSpec((tm, tn), lambda i, j, s: (i, j))
        out_shape = jax.ShapeDtypeStruct((M, N), F32)
        scratch = ()
    else:
        def body(a_ref, b_ref, o_ref, acc_ref):
            s = pl.program_id(2)

            @pl.when(s == 0)
            def _():
                acc_ref[...] = jnp.zeros_like(acc_ref)

            acc_ref[...] += _dot_tn(a_ref[...], b_ref[...])

            @pl.when(s == ns - 1)
            def _():
                for q in range(tn // shard):
                    o_ref[q] = acc_ref[:, shard * q:shard * (q + 1)]

        out_spec = pl.BlockSpec((tn // shard, tm, shard), lambda i, j, s: (j, i, 0))
        out_shape = jax.ShapeDtypeStruct((N // shard, M, shard), F32)
        scratch = (pltpu.VMEM((tm, tn), F32),)

    if planes == 1:
        bspec = pl.BlockSpec((ts, tn), lambda i, j, s: (s, j))
    else:
        bspec = pl.BlockSpec((None, ts, tn), lambda i, j, s: (j // per, s, j % per))
    (out,), couts = _call(
        body, grid=(M // tm, N // tn, ns),
        in_specs=[pl.BlockSpec((ts, tm), lambda i, j, s: (s, i)), bspec],
        out_specs=[out_spec], out_shape=[out_shape], scratch=scratch,
        operands=(a, b), name=name, sem=("parallel", "parallel", "arbitrary"), comm=comm)
    return out, couts


def ln_bwd(dy, z, g, out_scale, name, tm=512):
    S = dy.shape[0]
    tm = min(tm, S)

    def body(dy_ref, z_ref, g_ref, dz_ref, dzb_ref, dg_ref, db_ref):
        @pl.when(pl.program_id(0) == 0)
        def _():
            dg_ref[...] = jnp.zeros_like(dg_ref)
            db_ref[...] = jnp.zeros_like(db_ref)

        z = z_ref[...]
        dy_ = dy_ref[...]
        mu = jnp.mean(z, axis=-1, keepdims=True)
        xc = z - mu
        var = jnp.mean(xc * xc, axis=-1, keepdims=True)
        rstd = lax.rsqrt(var + LN_EPS)
        xh = xc * rstd
        dxh = dy_ * g_ref[...]
        m1 = jnp.mean(dxh, axis=-1, keepdims=True)
        m2 = jnp.mean(dxh * xh, axis=-1, keepdims=True)
        dz = rstd * (dxh - m1 - xh * m2)
        dz_ref[...] = dz
        dzb_ref[...] = (out_scale * dz).astype(BF16)
        dg_ref[...] += jnp.sum(dy_ * xh, axis=0, keepdims=True)
        db_ref[...] += jnp.sum(dy_, axis=0, keepdims=True)

    row = pl.BlockSpec((tm, D_MODEL), lambda i: (i, 0))
    vec = pl.BlockSpec((1, D_MODEL), lambda i: (0, 0))
    return pl.pallas_call(
        body, grid=(S // tm,),
        in_specs=[row, row, vec],
        out_specs=[row, row, vec, vec],
        out_shape=[jax.ShapeDtypeStruct((S, D_MODEL), F32), jax.ShapeDtypeStruct((S, D_MODEL), BF16),
                   jax.ShapeDtypeStruct((1, D_MODEL), F32), jax.ShapeDtypeStruct((1, D_MODEL), F32)],
        compiler_params=_cp("arbitrary"), name=name)(dy, z, g.reshape(1, -1))


def loss_head(y, t, name, tm=512):
    S = y.shape[0]
    tm = min(tm, S)
    nt = S // tm

    def body(y_ref, t_ref, dy_ref, l_ref):
        i = pl.program_id(0)

        @pl.when(i == 0)
        def _():
            l_ref[...] = jnp.zeros_like(l_ref)

        e = y_ref[...] - t_ref[...]
        dy_ref[...] = e * (1.0 / D_MODEL)
        l_ref[...] += jnp.sum(e * e, axis=0, keepdims=True)

        @pl.when(i == nt - 1)
        def _():
            tot = jnp.sum(l_ref[...], axis=1, keepdims=True) * (0.5 / D_MODEL)
            l_ref[...] = jnp.broadcast_to(tot, l_ref.shape)

    row = pl.BlockSpec((tm, D_MODEL), lambda i: (i, 0))
    vec = pl.BlockSpec((1, D_MODEL), lambda i: (0, 0))
    return pl.pallas_call(
        body, grid=(nt,), in_specs=[row, row], out_specs=[row, vec],
        out_shape=[jax.ShapeDtypeStruct((S, D_MODEL), F32), jax.ShapeDtypeStruct((1, D_MODEL), F32)],
        compiler_params=_cp("arbitrary"), name=name)(y, t)


def _rope_tables(S):
    pos = jnp.arange(S, dtype=F32)
    inv_freq = ROPE_THETA ** (-jnp.arange(0, HEAD_DIM, 2, dtype=F32) / HEAD_DIM)
    ang = pos[:, None] * inv_freq[None, :]
    cos, sin = jnp.cos(ang), jnp.sin(ang)
    cosf = jnp.concatenate([cos, cos, cos, cos], axis=1)
    sinf = jnp.concatenate([-sin, sin, -sin, sin], axis=1)
    return cosf, sinf


def _rot(t, c, s, first):
    sw = jnp.where(first, pltpu.roll(t, 96, 1), pltpu.roll(t, 32, 1))
    return t * c + sw * s


def rope_fwd(qkv, cosf, sinf, name, tm=512):
    S = qkv.shape[0]
    tm = min(tm, S)

    def body(x_ref, c_ref, s_ref, q_ref, k_ref, v_ref):
        c = c_ref[...]
        s = s_ref[...]
        first = (lax.broadcasted_iota(jnp.int32, (tm, 128), 1) % HEAD_DIM) < (HEAD_DIM // 2)
        for j in range(8):
            q_ref[:, 128 * j:128 * (j + 1)] = _rot(x_ref[:, 128 * j:128 * (j + 1)], c, s, first).astype(BF16)
        for j in range(2):
            k_ref[:, 128 * j:128 * (j + 1)] = _rot(x_ref[:, 1024 + 128 * j:1024 + 128 * (j + 1)], c, s, first).astype(BF16)
        v_ref[...] = x_ref[:, 1280:1536].astype(BF16)

    tab = pl.BlockSpec((tm, 128), lambda i: (i, 0))
    return pl.pallas_call(
        body, grid=(S // tm,),
        in_specs=[pl.BlockSpec((tm, QKV), lambda i: (i, 0)), tab, tab],
        out_specs=[pl.BlockSpec((tm, 1024), lambda i: (i, 0)), pl.BlockSpec((tm, 256), lambda i: (i, 0)),
                   pl.BlockSpec((tm, 256), lambda i: (i, 0))],
        out_shape=[jax.ShapeDtypeStruct((S, 1024), BF16), jax.ShapeDtypeStruct((S, 256), BF16),
                   jax.ShapeDtypeStruct((S, 256), BF16)],
        compiler_params=_cp("parallel"), name=name)(qkv, cosf, sinf)


def rope_bwd(dq, dkc, dkp, dvc, dvp, cosf, sinf, name, tq):
    S = dq.shape[0]
    nt = S // tq

    def body(dq_ref, dkc_ref, dkp_ref, dvc_ref, dvp_ref, c_ref, s_ref, o_ref):
        i = pl.program_id(0)
        c = c_ref[...]
        s = -s_ref[...]
        first = (lax.broadcasted_iota(jnp.int32, (tq, 128), 1) % HEAD_DIM) < (HEAD_DIM // 2)
        for j in range(8):
            o_ref[:, 128 * j:128 * (j + 1)] = _rot(dq_ref[:, 128 * j:128 * (j + 1)], c, s, first).astype(BF16)
        has_next = i < nt - 1
        rows = lax.broadcasted_iota(jnp.int32, (tq, 256), 0)
        pad = jnp.zeros((tq - BLOCK, 256), F32)
        halo_k = jnp.concatenate([pad, dkp_ref[...]], axis=0)
        halo_v = jnp.concatenate([pad, dvp_ref[...]], axis=0)
        use = jnp.logical_and(has_next, rows >= tq - BLOCK)
        dk = dkc_ref[...] + jnp.where(use, halo_k, 0.0)
        dv = dvc_ref[...] + jnp.where(use, halo_v, 0.0)
        for j in range(2):
            o_ref[:, 1024 + 128 * j:1024 + 128 * (j + 1)] = _rot(dk[:, 128 * j:128 * (j + 1)], c, s, first).astype(BF16)
        o_ref[:, 1280:1536] = dv.astype(BF16)

    tab = pl.BlockSpec((tq, 128), lambda i: (i, 0))
    cur = pl.BlockSpec((tq, 256), lambda i: (i, 0))
    nxt = pl.BlockSpec((BLOCK, 256), lambda i: (jnp.minimum(i + 1, nt - 1), 0))
    return pl.pallas_call(
        body, grid=(nt,),
        in_specs=[pl.BlockSpec((tq, 1024), lambda i: (i, 0)), cur, nxt, cur, nxt, tab, tab],
        out_specs=pl.BlockSpec((tq, QKV), lambda i: (i, 0)),
        out_shape=jax.ShapeDtypeStruct((S, QKV), BF16),
        compiler_params=_cp("parallel"), name=name)(dq, dkc, dkp, dvc, dvp, cosf, sinf)


def _attn_masks(n):
    shape = (GROUP * BLOCK, 2 * BLOCK)
    r = lax.broadcasted_iota(jnp.int32, shape, 0) % BLOCK
    c = lax.broadcasted_iota(jnp.int32, shape, 1)
    ok = jnp.logical_and(c > r, c <= r + BLOCK)
    return jnp.logical_and(ok, jnp.logical_or(c >= BLOCK, n > 0)), ok


def _attn_probs(qs, kw, sink_col, ok):
    s = jnp.where(ok, _dot_nt(qs, kw) * (HEAD_DIM ** -0.5), NEG)
    m = jnp.maximum(jnp.max(s, axis=1, keepdims=True), sink_col)
    p = jnp.exp(s - m)
    es = jnp.exp(sink_col - m)
    inv = 1.0 / (jnp.sum(p, axis=1, keepdims=True) + es)
    return p * inv, es * inv


def _sink_col(sink_ref, g):
    rid = lax.broadcasted_iota(jnp.int32, (GROUP * BLOCK, 1), 0) // BLOCK
    col = jnp.zeros((GROUP * BLOCK, 1), F32)
    for j in range(GROUP):
        col = jnp.where(rid == j, sink_ref[GROUP * g + j], col)
    return col


KV_PAIR = 2
Q_LANES = KV_PAIR * GROUP * HEAD_DIM


def _low_half(shape):
    return lax.broadcasted_iota(jnp.int32, shape, 1) < HEAD_DIM


def _head_rows(ref, lo, gi):
    low = _low_half((BLOCK, 2 * HEAD_DIM))
    out = []
    for s in range(GROUP // 2):
        c0 = gi * GROUP * HEAD_DIM + 2 * HEAD_DIM * s
        x = ref[lo:lo + BLOCK, c0:c0 + 2 * HEAD_DIM]
        out += [jnp.where(low, x, jnp.zeros_like(x)), jnp.where(low, jnp.zeros_like(x), x)]
    return jnp.concatenate(out, axis=0)


def _head_slabs(x):
    low = _low_half((BLOCK, 2 * HEAD_DIM))
    return [jnp.where(low, x[2 * BLOCK * s:2 * BLOCK * s + BLOCK], x[2 * BLOCK * s + BLOCK:2 * BLOCK * (s + 1)])
            for s in range(GROUP // 2)]


def _kv_window(kc_ref, kp_ref, b, gi):
    if b == 0:
        x = jnp.concatenate([kp_ref[...], kc_ref[0:BLOCK, :]], axis=0)
    else:
        x = kc_ref[BLOCK * (b - 1):BLOCK * (b + 1), :]
    other = pltpu.roll(x, HEAD_DIM, 1)
    low = _low_half(x.shape)
    return jnp.where(low, x, other) if gi == 0 else jnp.where(low, other, x)


def _fold_halves(x, gi):
    tot = x + pltpu.roll(x, HEAD_DIM, 1)
    low = _low_half(x.shape)
    return jnp.where(low if gi == 0 else jnp.logical_not(low), tot, 0.0)


def _attn_specs(tq):
    nsub = tq // BLOCK
    qspec = pl.BlockSpec((tq, Q_LANES), lambda p, n: (n, p))
    cur = pl.BlockSpec((tq, KV_PAIR * HEAD_DIM), lambda p, n: (n, p))
    prev = pl.BlockSpec((BLOCK, KV_PAIR * HEAD_DIM), lambda p, n: (jnp.maximum(n * nsub - 1, 0), p))
    return qspec, cur, prev


def attn_fwd(q, k, v, sinks, name, tq, comm=None):
    S = q.shape[0]
    nsub = tq // BLOCK

    def body(sink_ref, q_ref, kc_ref, kp_ref, vc_ref, vp_ref, o_ref):
        p = pl.program_id(0)
        n = pl.program_id(1)
        ok_first, ok_rest = _attn_masks(n)
        for gi in range(KV_PAIR):
            sink_col = _sink_col(sink_ref, KV_PAIR * p + gi)
            base = gi * GROUP * HEAD_DIM
            for b in range(nsub):
                lo = BLOCK * b
                qs = _head_rows(q_ref, lo, gi)
                pn, _ = _attn_probs(qs, _kv_window(kc_ref, kp_ref, b, gi), sink_col, ok_rest if b else ok_first)
                o = _dot(pn.astype(BF16), _kv_window(vc_ref, vp_ref, b, gi))
                for s, slab in enumerate(_head_slabs(o)):
                    c0 = base + 2 * HEAD_DIM * s
                    o_ref[lo:lo + BLOCK, c0:c0 + 2 * HEAD_DIM] = slab.astype(BF16)

    qspec, cur, prev = _attn_specs(tq)
    (out,), couts = _call(
        body, grid=(N_KV // KV_PAIR, S // tq),
        in_specs=[pl.BlockSpec(memory_space=pltpu.SMEM), qspec, cur, prev, cur, prev],
        out_specs=[qspec],
        out_shape=[jax.ShapeDtypeStruct((S, N_HEADS * HEAD_DIM), BF16)],
        operands=(sinks, q, k, k, v, v), name=name, sem=("parallel", "parallel"), comm=comm)
    return out, couts


def attn_bwd(q, k, v, do, sinks, name, tq):
    S = q.shape[0]
    nsub = tq // BLOCK
    nt = S // tq

    def body(sink_ref, q_ref, kc_ref, kp_ref, vc_ref, vp_ref, do_ref, dq_ref, dkc_ref, dkp_ref, dvc_ref, dvp_ref, ds_ref):
        p = pl.program_id(0)
        n = pl.program_id(1)

        @pl.when(n == 0)
        def _():
            ds_ref[...] = jnp.zeros_like(ds_ref)

        for ref in (dkc_ref, dvc_ref, dkp_ref, dvp_ref):
            ref[...] = jnp.zeros_like(ref)
        rid = lax.broadcasted_iota(jnp.int32, (GROUP * BLOCK, 1), 0) // BLOCK
        sub = lax.broadcasted_iota(jnp.int32, (8, 128), 0)
        ok_first, ok_rest = _attn_masks(n)
        for gi in range(KV_PAIR):
            sink_col = _sink_col(sink_ref, KV_PAIR * p + gi)
            base = gi * GROUP * HEAD_DIM
            dsink = jnp.zeros((8, 128), F32)
            for b in range(nsub):
                lo = BLOCK * b
                qs = _head_rows(q_ref, lo, gi)
                dos = _head_rows(do_ref, lo, gi)
                kw = _kv_window(kc_ref, kp_ref, b, gi)
                vw = _kv_window(vc_ref, vp_ref, b, gi)
                pn, ps = _attn_probs(qs, kw, sink_col, ok_rest if b else ok_first)
                dp = _dot_nt(dos, vw)
                delta = jnp.sum(pn * dp, axis=1, keepdims=True)
                dsb = (pn * (dp - delta) * (HEAD_DIM ** -0.5)).astype(BF16)
                for s, slab in enumerate(_head_slabs(_dot(dsb, kw))):
                    c0 = base + 2 * HEAD_DIM * s
                    dq_ref[lo:lo + BLOCK, c0:c0 + 2 * HEAD_DIM] = slab
                dkw = _fold_halves(_dot_tn(dsb, qs), gi)
                dvw = _fold_halves(_dot_tn(pn.astype(BF16), dos), gi)
                if b == 0:
                    dkp_ref[...] += dkw[0:BLOCK]
                    dvp_ref[...] += dvw[0:BLOCK]
                else:
                    dkc_ref[lo - BLOCK:lo, :] += dkw[0:BLOCK]
                    dvc_ref[lo - BLOCK:lo, :] += dvw[0:BLOCK]
                dkc_ref[lo:lo + BLOCK, :] += dkw[BLOCK:2 * BLOCK]
                dvc_ref[lo:lo + BLOCK, :] += dvw[BLOCK:2 * BLOCK]
                sd = ps * delta
                for j in range(GROUP):
                    tot = -jnp.sum(jnp.where(rid == j, sd, 0.0))
                    dsink = dsink + jnp.where(sub == j, tot, 0.0)
            ds_ref[gi] += dsink

    qspec, cur, prev = _attn_specs(tq)
    halo = pl.BlockSpec((BLOCK, KV_PAIR * HEAD_DIM), lambda p, n: (n, p))
    kv_shape = jax.ShapeDtypeStruct((S, N_KV * HEAD_DIM), F32)
    halo_shape = jax.ShapeDtypeStruct((nt * BLOCK, N_KV * HEAD_DIM), F32)
    return pl.pallas_call(
        body, grid=(N_KV // KV_PAIR, nt),
        in_specs=[pl.BlockSpec(memory_space=pltpu.SMEM), qspec, cur, prev, cur, prev, qspec],
        out_specs=[qspec, cur, halo, cur, halo, pl.BlockSpec((KV_PAIR, 8, 128), lambda p, n: (p, 0, 0))],
        out_shape=[jax.ShapeDtypeStruct((S, N_HEADS * HEAD_DIM), F32), kv_shape, halo_shape, kv_shape, halo_shape,
                   jax.ShapeDtypeStruct((N_KV, 8, 128), F32)],
        compiler_params=_cp("parallel", "arbitrary"), name=name)(sinks, q, k, k, v, v, do)


def _rows_before(cur, prev8, k):
    if k == 0:
        return cur
    n = cur.shape[0]
    ext = jnp.concatenate([prev8, cur], axis=0)
    return ext[8 - k:8 - k + n]


def _rows_after(cur, next8, k):
    if k == 0:
        return cur
    n = cur.shape[0]
    ext = jnp.concatenate([cur, next8], axis=0)
    return ext[k:k + n]


def _gelu(x):
    c = math.sqrt(2.0 / math.pi)
    t = jnp.tanh(c * (x + 0.044715 * (x * x * x)))
    return 0.5 * (1.0 + t), t


def _neg_expm1(u):
    ser = 1.0 + u * (1.0 / 6.0)
    for k in range(5, 1, -1):
        ser = 1.0 + (u * (1.0 / k)) * ser
    return jnp.where(u > -0.125, -(u * ser), 1.0 - jnp.exp(u))


def _block_diag(xb16, w_ref):
    return jnp.concatenate([_dot(xb16[:, RNN_W * n:RNN_W * (n + 1)], w_ref[n]) for n in range(RNN_BLOCKS)], axis=1)


def _lru_gates(xb, prev8, cw_ref, cb_ref, wra_ref, wrx_ref, bra_ref, brx_ref, lsl_ref):
    xc = cb_ref[...] + cw_ref[3:4, :] * xb
    for w in range(CONV_W - 1):
        xc = xc + cw_ref[w:w + 1, :] * _rows_before(xb, prev8, CONV_W - 1 - w)
    xcb = xc.astype(BF16)
    r = jax.nn.sigmoid(_block_diag(xcb, wra_ref) + bra_ref[...])
    ig = jax.nn.sigmoid(_block_diag(xcb, wrx_ref) + brx_ref[...])
    la = LRU_C * r * lsl_ref[...]
    a = jnp.exp(la)
    sq = jnp.sqrt(_neg_expm1(2.0 * la))
    return xc, xcb, r, ig, a, sq


def lru_fwd(xg, p, name, tm=256, comm=None):
    S = xg.shape[0]
    tm = min(tm, S)

    def body(xg_ref, xp_ref, cw_ref, cb_ref, wra_ref, wrx_ref, bra_ref, brx_ref, lsl_ref, y_ref, h_ref, hc_ref, a_s, b_s):
        i = pl.program_id(0)
        xb = xg_ref[:, 0:D_RNN]
        gb = xg_ref[:, D_RNN:2 * D_RNN]
        prev8 = jnp.where(i > 0, xp_ref[:, 0:D_RNN], 0.0)
        xc, _, r, ig, a, sq = _lru_gates(xb, prev8, cw_ref, cb_ref, wra_ref, wrx_ref, bra_ref, brx_ref, lsl_ref)
        a_s[...] = a
        b_s[...] = sq * (ig * xc)

        @pl.when(i == 0)
        def _():
            hc_ref[...] = jnp.zeros_like(hc_ref)

        def chunk(c, h):
            o = pl.multiple_of(c * 8, 8)
            av = a_s[pl.ds(o, 8), :]
            bv = b_s[pl.ds(o, 8), :]
            rows = []
            for t in range(8):
                h = av[t:t + 1, :] * h + bv[t:t + 1, :]
                rows.append(h)
            h_ref[pl.ds(o, 8), :] = jnp.concatenate(rows, axis=0)
            return h

        h_last = lax.fori_loop(0, tm // 8, chunk, hc_ref[0:1, :])
        hc_ref[0:1, :] = h_last
        cdf, _ = _gelu(gb)
        y_ref[...] = (h_ref[...] * (gb * cdf)).astype(BF16)

    vec = pl.BlockSpec((1, D_RNN), lambda i: (0, 0))
    wsp = pl.BlockSpec((RNN_BLOCKS, RNN_W, RNN_W), lambda i: (0, 0, 0))
    return _call(
        body, grid=(S // tm,),
        in_specs=[pl.BlockSpec((tm, 2 * D_RNN), lambda i: (i, 0)),
                  pl.BlockSpec((8, 2 * D_RNN), lambda i: (jnp.maximum(i * (tm // 8) - 1, 0), 0)),
                  pl.BlockSpec((CONV_W, D_RNN), lambda i: (0, 0)), vec, wsp, wsp, vec, vec, vec],
        out_specs=[pl.BlockSpec((tm, D_RNN), lambda i: (i, 0)), pl.BlockSpec((tm, D_RNN), lambda i: (i, 0))],
        out_shape=[jax.ShapeDtypeStruct((S, D_RNN), BF16), jax.ShapeDtypeStruct((S, D_RNN), F32)],
        scratch=[pltpu.VMEM((8, D_RNN), F32), pltpu.VMEM((tm, D_RNN), F32), pltpu.VMEM((tm, D_RNN), F32)],
        operands=(xg, xg, p["conv_w"], p["conv_b"], p["w_ra"], p["w_rx"], p["b_ra"], p["b_rx"], p["lsl"]),
        name=name, sem=("arbitrary",), comm=comm)


def lru_bwd(dy, xg, h, p, name, tm=256):
    S = xg.shape[0]
    tm = min(tm, S)
    nt = S // tm

    def body(dy_ref, xg_ref, xp_ref, h_ref, hp_ref, cw_ref, cb_ref, wra_ref, wrx_ref, bra_ref, brx_ref,
             lsl_ref, dxg_ref, dcw_ref, dcb_ref, dwra_ref, dwrx_ref, dbra_ref, dbrx_ref, dlam_ref,
             lc_ref, nx_ref, a_s, g_s, l_s):
        i = pl.program_id(0)
        ti = nt - 1 - i

        @pl.when(i == 0)
        def _():
            lc_ref[...] = jnp.zeros_like(lc_ref)
            nx_ref[...] = jnp.zeros_like(nx_ref)
            for ref in (dcw_ref, dcb_ref, dwra_ref, dwrx_ref, dbra_ref, dbrx_ref, dlam_ref):
                ref[...] = jnp.zeros_like(ref)

        xb = xg_ref[:, 0:D_RNN]
        gb = xg_ref[:, D_RNN:2 * D_RNN]
        prev8 = jnp.where(ti > 0, xp_ref[:, 0:D_RNN], 0.0)
        xc, xcb, r, ig, a, sq = _lru_gates(xb, prev8, cw_ref, cb_ref, wra_ref, wrx_ref, bra_ref, brx_ref, lsl_ref)
        hh = h_ref[...]
        hprev = _rows_before(hh, jnp.where(ti > 0, hp_ref[...], 0.0), 1)
        dy_ = dy_ref[...]
        cdf, th = _gelu(gb)
        c0 = math.sqrt(2.0 / math.pi)
        dgate = cdf + gb * (0.5 * (1.0 - th * th) * c0 * (1.0 + 3.0 * 0.044715 * gb * gb))
        dgb = dy_ * hh * dgate
        a_s[...] = a
        g_s[...] = dy_ * (gb * cdf)

        def chunk(cc, carry):
            o = pl.multiple_of((tm // 8 - 1 - cc) * 8, 8)
            av = a_s[pl.ds(o, 8), :]
            gv = g_s[pl.ds(o, 8), :]
            rows = [None] * 8
            for t in range(7, -1, -1):
                lam_t = gv[t:t + 1, :] + carry
                rows[t] = lam_t
                carry = av[t:t + 1, :] * lam_t
            l_s[pl.ds(o, 8), :] = jnp.concatenate(rows, axis=0)
            return carry

        carry = lax.fori_loop(0, tm // 8, chunk, lc_ref[0:1, :])
        lc_ref[0:1, :] = carry
        lam = l_s[...]
        da = lam * hprev
        dixc = lam * sq
        di = dixc * xc
        dxc = dixc * ig
        dsq = lam * (ig * xc)
        dla = da * a - dsq * (a * a / sq)
        dr = dla * (LRU_C * lsl_ref[...])
        dlam_ref[...] += jnp.sum(dla * (LRU_C * r), axis=0, keepdims=True)
        dpr = dr * r * (1.0 - r)
        dpi = di * ig * (1.0 - ig)
        dbra_ref[...] += jnp.sum(dpr, axis=0, keepdims=True)
        dbrx_ref[...] += jnp.sum(dpi, axis=0, keepdims=True)
        dprb = dpr.astype(BF16)
        dpib = dpi.astype(BF16)
        back = []
        for n in range(RNN_BLOCKS):
            sl = slice(RNN_W * n, RNN_W * (n + 1))
            dwra_ref[n] += _dot_tn(xcb[:, sl], dprb[:, sl])
            dwrx_ref[n] += _dot_tn(xcb[:, sl], dpib[:, sl])
            back.append(_dot_nt(dprb[:, sl], wra_ref[n]) + _dot_nt(dpib[:, sl], wrx_ref[n]))
        dxc = dxc + jnp.concatenate(back, axis=1)
        dcb_ref[...] += jnp.sum(dxc, axis=0, keepdims=True)
        next8 = nx_ref[...]
        dxb = cw_ref[3:4, :] * dxc
        dcw_ref[3:4, :] += jnp.sum(dxc * xb, axis=0, keepdims=True)
        for w in range(CONV_W - 1):
            k = CONV_W - 1 - w
            dcw_ref[w:w + 1, :] += jnp.sum(dxc * _rows_before(xb, prev8, k), axis=0, keepdims=True)
            dxb = dxb + cw_ref[w:w + 1, :] * _rows_after(dxc, next8, k)
        nx_ref[...] = dxc[0:8, :]
        dxg_ref[:, 0:D_RNN] = dxb.astype(BF16)
        dxg_ref[:, D_RNN:2 * D_RNN] = dgb.astype(BF16)

    rev = lambda i: (nt - 1 - i, 0)
    before = lambda i: (jnp.maximum((nt - 1 - i) * (tm // 8) - 1, 0), 0)
    vec = pl.BlockSpec((1, D_RNN), lambda i: (0, 0))
    wsp = pl.BlockSpec((RNN_BLOCKS, RNN_W, RNN_W), lambda i: (0, 0, 0))
    cwsp = pl.BlockSpec((CONV_W, D_RNN), lambda i: (0, 0))
    return pl.pallas_call(
        body, grid=(nt,),
        in_specs=[pl.BlockSpec((tm, D_RNN), rev), pl.BlockSpec((tm, 2 * D_RNN), rev), pl.BlockSpec((8, 2 * D_RNN), before),
                  pl.BlockSpec((tm, D_RNN), rev), pl.BlockSpec((8, D_RNN), before),
                  cwsp, vec, wsp, wsp, vec, vec, vec],
        out_specs=[pl.BlockSpec((tm, 2 * D_RNN), rev), cwsp, vec, wsp, wsp, vec, vec, vec],
        out_shape=[jax.ShapeDtypeStruct((S, 2 * D_RNN), BF16), jax.ShapeDtypeStruct((CONV_W, D_RNN), F32),
                   jax.ShapeDtypeStruct((1, D_RNN), F32), jax.ShapeDtypeStruct((RNN_BLOCKS, RNN_W, RNN_W), F32),
                   jax.ShapeDtypeStruct((RNN_BLOCKS, RNN_W, RNN_W), F32), jax.ShapeDtypeStruct((1, D_RNN), F32),
                   jax.ShapeDtypeStruct((1, D_RNN), F32), jax.ShapeDtypeStruct((1, D_RNN), F32)],
        scratch_shapes=[pltpu.VMEM((8, D_RNN), F32), pltpu.VMEM((8, D_RNN), F32), pltpu.VMEM((tm, D_RNN), F32),
                        pltpu.VMEM((tm, D_RNN), F32), pltpu.VMEM((tm, D_RNN), F32)],
        compiler_params=_cp("arbitrary"), name=name)(
            dy, xg, xg, h, h, p["conv_w"], p["conv_b"], p["w_ra"], p["w_rx"], p["b_ra"], p["b_rx"], p["lsl"])


def _place():
    x, y, c = lax.axis_index("x"), lax.axis_index("y"), lax.axis_index("c")
    chips = [(1 - x, y), (x, 1 - y), (1 - x, 1 - y)]
    return x, y, c, chips


def comm_ag(arrs):
    n = len(arrs)

    def copies(ins, outs, sems):
        send_sems, recv_sems, local_sems = sems
        x, y, c, chips = _place()
        me, sibling = (x, y, c), (x, y, 1 - c)

        def rows(a, px, py, pc):
            return outs[a].at[4 * px + 2 * py + pc]

        def copy(a, k, block, to, own=False):
            return pltpu.make_async_remote_copy(
                src_ref=ins[a] if own else rows(a, *block), dst_ref=rows(a, *block),
                send_sem=send_sems.at[k, a], recv_sem=recv_sems.at[k, a], device_id=to, device_id_type=MESH)

        mine = [pltpu.make_async_copy(ins[a], rows(a, *me), local_sems.at[a]) for a in range(n)]
        first = [copy(a, 1 + j, me, (*chip, c), own=True) for j, chip in enumerate(chips) for a in range(n)]
        first += [copy(a, 0, me, sibling, own=True) for a in range(n)]
        return copy, chips, c, me, sibling, mine, first

    def start(ins, outs, sems):
        _, _, _, _, _, mine, first = copies(ins, outs, sems)
        for cp in mine + first:
            cp.start()

    def relay(ins, outs, sems):
        copy, chips, c, me, sibling, _, _ = copies(ins, outs, sems)
        for j, chip in enumerate(chips):
            for a in range(n):
                copy(a, 1 + j, (*chip, c), me).wait_recv()
                copy(a, 4 + j, (*chip, c), sibling).start()

    def finish(ins, outs, sems):
        copy, chips, c, me, sibling, mine, first = copies(ins, outs, sems)
        passed = [copy(a, 4 + j, (*chip, c), sibling) for j, chip in enumerate(chips) for a in range(n)]
        for a in range(n):
            copy(a, 0, sibling, me).wait_recv()
        for j, chip in enumerate(chips):
            for a in range(n):
                copy(a, 4 + j, (*chip, 1 - c), me).wait_recv()
        for cp in first + passed:
            cp.wait_send()
        for cp in mine:
            cp.wait()

    return Comm(arrs, [jax.ShapeDtypeStruct((N_DEV,) + p.shape, p.dtype) for p in arrs],
                [pltpu.SemaphoreType.DMA((7, n)), pltpu.SemaphoreType.DMA((7, n)), pltpu.SemaphoreType.DMA((n,))],
                start, finish, relay=relay)


def comm_pair(sends):
    n = len(sends)

    def copies(ins, outs, sems):
        send_sems, recv_sems = sems
        x, y, c, _ = _place()
        return [pltpu.make_async_remote_copy(
            src_ref=ins[a].at[k, 1 - c], dst_ref=outs[a].at[k], send_sem=send_sems.at[k, a], recv_sem=recv_sems.at[k, a],
            device_id=(x, y, 1 - c), device_id_type=MESH) for k in range(4) for a in range(n)]

    def start(ins, outs, sems):
        for cp in copies(ins, outs, sems):
            cp.start()

    def finish(ins, outs, sems):
        for cp in copies(ins, outs, sems):
            cp.wait()

    return Comm(sends, [jax.ShapeDtypeStruct((4,) + s.shape[2:], s.dtype) for s in sends],
                [pltpu.SemaphoreType.DMA((4, n)), pltpu.SemaphoreType.DMA((4, n))], start, finish)


def pair_sum(sends, gots, out_dtypes, name, steps=2):
    n = len(sends)
    c = lax.axis_index("c").astype(jnp.int32).reshape(1)
    steps = [steps if s.shape[2] % (16 * steps) == 0 else 1 for s in sends]

    def body(c_ref, *refs):
        for a in range(n):
            refs[2 * n + a][...] = (refs[a][...] + refs[n + a][...]).astype(out_dtypes[a])

    def specs(a, own):
        _, _, R, C = sends[a].shape
        tr, last = R // steps[a], steps[a] - 1
        if own:
            return pl.BlockSpec((None, None, tr, C), lambda k, i, cr: (k, cr[0], jnp.minimum(i, last), 0))
        return pl.BlockSpec((None, tr, C), lambda k, i, cr: (k, jnp.minimum(i, last), 0))

    return pl.pallas_call(
        body,
        grid_spec=pltpu.PrefetchScalarGridSpec(
            num_scalar_prefetch=1, grid=(4, max(steps)),
            in_specs=[specs(a, True) for a in range(n)] + [specs(a, False) for a in range(n)],
            out_specs=[specs(a, False) for a in range(n)]),
        out_shape=[jax.ShapeDtypeStruct((4,) + s.shape[2:], d) for s, d in zip(sends, out_dtypes)],
        compiler_params=_cp("parallel", "arbitrary"), name=name)(c, *sends, *gots)


def comm_chip(items, bufs):
    ns = len(items)
    segs = [(i, b, s0, nr, d0) for i, (_, b, ranges) in enumerate(items) for (s0, nr, d0) in ranges]

    def copies(ins, outs, sems):
        send_sems, recv_sems, local_sems = sems
        x, y, c, chips = _place()
        mychip = 2 * x + y
        mine = [pltpu.make_async_copy(ins[i].at[mychip, pl.ds(s0, nr)], outs[b].at[mychip, pl.ds(d0, nr)], local_sems.at[q])
                for q, (i, b, s0, nr, d0) in enumerate(segs)]
        remote = [pltpu.make_async_remote_copy(
            src_ref=ins[i].at[2 * px + py, pl.ds(s0, nr)], dst_ref=outs[b].at[mychip, pl.ds(d0, nr)],
            send_sem=send_sems.at[j, q], recv_sem=recv_sems.at[j, q], device_id=(px, py, c), device_id_type=MESH)
            for j, (px, py) in enumerate(chips) for q, (i, b, s0, nr, d0) in enumerate(segs)]
        return mine, remote

    def start(ins, outs, sems):
        mine, remote = copies(ins, outs, sems)
        for cp in mine + remote:
            cp.start()

    def finish(ins, outs, sems):
        mine, remote = copies(ins, outs, sems)
        for cp in remote + mine:
            cp.wait()

    q = len(segs)
    old = [b for b, buf in enumerate(bufs) if not isinstance(buf, jax.ShapeDtypeStruct)]
    return Comm([it[0] for it in items] + [bufs[b] for b in old], [jax.ShapeDtypeStruct(b.shape, b.dtype) for b in bufs],
                [pltpu.SemaphoreType.DMA((3, q)), pltpu.SemaphoreType.DMA((3, q)), pltpu.SemaphoreType.DMA((q,))],
                start, finish, aliases={ns + k: b for k, b in enumerate(old)})


def adamw(parts, row0, w, m, v, tr, name):
    C = w.shape[-1]
    rows = w.size // C
    off = row0 // tr

    def body(p_ref, w_ref, m_ref, v_ref, g_ref, d_ref, nm_ref, nv_ref):
        g = ((p_ref[0].astype(F32) + p_ref[1].astype(F32)) + p_ref[2].astype(F32)) + p_ref[3].astype(F32)
        m2 = ADAM_B1 * m_ref[...] + (1.0 - ADAM_B1) * g
        v2 = ADAM_B2 * v_ref[...] + (1.0 - ADAM_B2) * (g * g)
        mh = m2 / (1.0 - ADAM_B1 ** ADAM_STEP)
        vh = v2 / (1.0 - ADAM_B2 ** ADAM_STEP)
        g_ref[...] = g
        d_ref[...] = -ADAM_LR * (mh / (jnp.sqrt(vh) + ADAM_EPS) + ADAM_WD * w_ref[...])
        nm_ref[...] = m2
        nv_ref[...] = v2

    row = pl.BlockSpec((tr, C), lambda i: (i, 0))
    shp = jax.ShapeDtypeStruct((rows, C), F32)
    outs = pl.pallas_call(
        body, grid=(rows // tr,),
        in_specs=[pl.BlockSpec((4, tr, C), lambda i: (0, off + i, 0)), row, row, row],
        out_specs=[row, row, row, row], out_shape=[shp, shp, shp, shp],
        compiler_params=_cp("parallel"), name=name)(parts, w.reshape(rows, C), m.reshape(rows, C), v.reshape(rows, C))
    return [o.reshape(w.shape) for o in outs]


def _full_from_gathered(flat, shape, axis):
    t = jnp.moveaxis(flat.reshape((N_DEV,) + shape), 0, axis)
    return t.reshape(shape[:axis] + (N_DEV * shape[axis],) + shape[axis + 1:])


def _shards_of_full(full, shape, axis):
    t = full.reshape(shape[:axis] + (N_DEV, shape[axis]) + shape[axis + 1:])
    return jnp.moveaxis(t, axis, 0).reshape(N_DEV, -1)


def _small_pack(args, prefix):
    flat = jnp.concatenate([args[prefix + n].reshape(-1) for n, _, _ in SMALL] + [args[prefix + "attn_sinks"].reshape(-1)])
    return jnp.pad(flat, (0, SM_ROWS * 1024 - flat.shape[0])).reshape(SM_ROWS, 1024)


def _small_unpack(pack):
    flat = pack.reshape(-1)
    out, off = {}, 0
    for n, shape, _ in SMALL:
        size = math.prod(shape)
        out[n] = flat[off:off + size].reshape(shape)
        off += size
    out["attn_sinks"] = flat[SINK_OFF:SINK_OFF + 32].reshape(2, 16)
    return out


def _tn_tile(n):
    return next(t for t in (1408, 1024, 768, 512, 256, 128) if n % t == 0)


def _dw(a, b, name, planes=1, shard=None, comm=None):
    return mm_tn(a, b, name, _tn_tile(a.shape[1]), _tn_tile(b.shape[-1]), planes=planes, shard=shard, comm=comm)


class _NoExchange:
    def __init__(self, layers):
        self.layers, self.grads = layers, [{} for _ in range(DEPTH)]

    def weights(self, l, piece):
        return self.layers[l]

    def fwd_slot(self):
        return None

    def fwd_done(self, outs):
        pass

    def loss_ready(self, loss):
        pass

    def bwd_slot(self, slot):
        return None

    def bwd_done(self, slot, outs):
        pass

    def grads_ready(self, l, part, g):
        self.grads[l].update(g)


def _local_step(x, target, sched, sinks):
    S = x.shape[0]
    tq = min(512, S)
    cosf, sinf = _rope_tables(S)
    saved = []
    h, hb = x, x.astype(BF16)
    for l in range(DEPTH):
        j = l // 2
        wl = {}
        sv = {"h0b": hb, "wl": wl}
        wl.update(sched.weights(l, "a"))
        (sv["gu1"], a1), co = ffn_up(hb, wl["w_in1"], f"ffn1_up_{l}", comm=sched.fwd_slot())
        sched.fwd_done(co)
        sv["a1"] = a1
        wl.update(sched.weights(l, "b"))
        (sv["z1"], h, hb), _ = mm_ln(a1, _rows_view(wl["r_ff"], FF_SHARD, 0), h, wl["ln_g"][0], wl["ln_b"][0], 0.5,
                                     f"ffn1_down_ln_{l}")
        sv["h1b"] = hb
        if l % 2 == 0:
            qkv = mm_plain(hb, _mat_view(wl["m_c"]), F32, f"attn_qkv_{l}")
            qr, kr, vv = rope_fwd(qkv, cosf, sinf, f"rope_{l}")
            mix_in, co = attn_fwd(qr, kr, vv, sinks[j], f"attn_core_{l}", tq, comm=sched.fwd_slot())
            sv.update(qr=qr, kr=kr, vv=vv)
        else:
            xg = mm_plain(hb, _mat_view(wl["m_c"]), F32, f"lru_in_{l}")
            (mix_in, hstate), co = lru_fwd(xg, wl["lru"], f"lru_core_{l}", comm=sched.fwd_slot())
            sv.update(xg=xg, hstate=hstate)
        sched.fwd_done(co)
        sv["mix_in"] = mix_in
        (sv["z2"], h, hb), _ = mm_ln(mix_in, _rows_view(wl["m_o"], MIX_SHARD, 0), h, wl["ln_g"][1], wl["ln_b"][1], 1.0,
                                     f"mix_out_ln_{l}")
        sv["h2b"] = hb
        wl.update(sched.weights(l, "c"))
        (sv["gu2"], a2), co = ffn_up(hb, wl["w_in2"], f"ffn2_up_{l}", comm=sched.fwd_slot())
        sched.fwd_done(co)
        sv["a2"] = a2
        (sv["z3"], h, hb), _ = mm_ln(a2, _rows_view(wl["r_ff"], FF_SHARD, FF_SHARD), h, wl["ln_g"][2], wl["ln_b"][2], 0.5,
                                     f"ffn2_down_ln_{l}")
        saved.append(sv)

    dy, lvec = loss_head(h, target, "loss_head")
    loss = lvec[0, 0]
    sched.loss_ready(loss)

    def hosted(slot, on, fn):
        comm = sched.bwd_slot(slot) if on else None
        out, co = fn(comm)
        if comm is not None:
            sched.bwd_done(slot, co)
        return out

    def ffn_bwd(dy, z, g, gu, a, xin_b, r_ff, blk, w_in, tag, host):
        dz, dzb, dg, db = ln_bwd(dy, z, g, 0.5, f"ln_bwd_{tag}")
        dh, _ = ffn_mid_bwd(dzb, r_ff, blk, gu, f"ffn_mid_bwd_{tag}")
        d_wout = hosted("pair", host, lambda cm: _dw(a, dzb, f"dw_out_{tag}", comm=cm))
        d_win = hosted("chip_x", host, lambda cm: _dw(xin_b, dh, f"dw_in_{tag}", planes=2, shard=2 * D_FF // N_DEV, comm=cm))
        dx = hosted("chip_y", host, lambda cm: mm_res2_nt(dh, _mat_view(w_in, D_FF, 0), _mat_view(w_in, D_FF, 1), dz, ALPHA,
                                                          f"ffn_dx_{tag}", comm=cm))
        return dx, dg, db, d_wout, d_win

    for l in reversed(range(DEPTH)):
        j = l // 2
        sv = saved[l]
        wl = sv["wl"]
        gl = {}
        dg, db = [None] * 3, [None] * 3
        dy, dg[2], db[2], gl["w_out2"], gl["w_in2"] = ffn_bwd(
            dy, sv["z3"], wl["ln_g"][2], sv["gu2"], sv["a2"], sv["h2b"], wl["r_ff"], 1, wl["w_in2"], f"2_{l}", True)
        dz, dzb, dg[1], db[1] = ln_bwd(dy, sv["z2"], wl["ln_g"][1], 1.0, f"ln_bwd_mix_{l}")
        w_mix = _rows_view(wl["m_o"], MIX_SHARD, 0)
        if l % 2 == 0:
            gl["mix_out"], _ = _dw(sv["mix_in"], dzb, f"dw_o_{l}")
            do = mm_plain(dzb, w_mix, BF16, f"attn_do_{l}", nt=True)
            dq, dkc, dkp, dvc, dvp, dsk = attn_bwd(sv["qr"], sv["kr"], sv["vv"], do, sinks[j], f"attn_core_bwd_{l}", tq)
            gl["sinks"] = dsk[:, :GROUP, 0].reshape(N_HEADS)
            dmid = rope_bwd(dq, dkc, dkp, dvc, dvp, cosf, sinf, f"rope_bwd_{l}", tq)
            gl["mix_in"], _ = _dw(sv["h1b"], dmid, f"dw_qkv_{l}", shard=QKV // N_DEV)
        else:
            gl["mix_out"], _ = _dw(sv["mix_in"], dzb, f"dw_lru_out_{l}")
            dyl = mm_plain(dzb, w_mix, F32, f"lru_dy_{l}", nt=True)
            dmid, dcw, dcb, dwra, dwrx, dbra, dbrx, dlam = lru_bwd(dyl, sv["xg"], sv["hstate"], wl["lru"], f"lru_core_bwd_{l}")
            gl.update(conv_w=dcw, conv_b=dcb[0], w_ra=dwra, w_rx=dwrx, b_ra=dbra[0], b_rx=dbrx[0],
                      lam=dlam[0] * wl["lru"]["sig_neg"])
            gl["mix_in"], _ = _dw(sv["h1b"], dmid, f"dw_lru_in_{l}", shard=2 * D_RNN // N_DEV)
        dy = mm_res_nt(dmid, _mat_view(wl["m_c"]), dz, ALPHA, f"mix_dx_{l}")
        sched.grads_ready(l, "hi", gl)
        lo = {}
        dy, dg[0], db[0], lo["w_out1"], lo["w_in1"] = ffn_bwd(
            dy, sv["z1"], wl["ln_g"][0], sv["gu1"], sv["a1"], sv["h0b"], wl["r_ff"], 0, wl["w_in1"], f"1_{l}", True)
        lo["ln_g"], lo["ln_b"] = jnp.concatenate(dg, axis=0), jnp.concatenate(db, axis=0)
        sched.grads_ready(l, "lo", lo)
    return loss, dy


def _lru_params(full, j, w_ra, w_rx):
    lam = full["lru_lambda"][j]
    return {
        "conv_w": full["lru_conv_w"][j], "conv_b": full["lru_conv_b"][j].reshape(1, -1),
        "w_ra": w_ra, "w_rx": w_rx,
        "b_ra": full["lru_b_ra"][j].reshape(1, -1), "b_rx": full["lru_b_rx"][j].reshape(1, -1),
        "lsl": jax.nn.log_sigmoid(lam).reshape(1, -1), "sig_neg": jax.nn.sigmoid(-lam),
    }


def _row_shards(f):
    return f.reshape(N_DEV, -1, f.shape[1])


def _gate_shards(w):
    return w.reshape(RNN_BLOCKS, N_DEV, RNN_W // N_DEV, RNN_W).transpose(1, 0, 2, 3).reshape(N_DEV, -1, RNN_W)


class _Fsdp:
    def __init__(self, args):
        self.args = args
        self.b16 = lambda a: a.astype(BF16)
        self.sm = _small_pack(args, "")
        self.sent = {l: self._ag_arrays(l) for l in range(DEPTH)}
        self.queue = [(l, p) for l in range(DEPTH) for p in "abc"][1:]
        self.hosting = None
        got = run_comm(comm_ag(self.sent[0]["a"] + [self.sm]), "all_gather_first")
        self.raw = {0: {"a": got[:-1]}}
        gflat = got[-1].reshape(N_DEV, SM_ROWS * 1024)
        self.small, off = {}, 0
        for n, shape, axis in SMALL:
            size = math.prod(shape)
            self.small[n] = _full_from_gathered(gflat[:, off:off + size], shape, axis)
            off += size
        self.bufs = {"r": jax.ShapeDtypeStruct((4, R_ROWS, D_MODEL), BF16),
                     "c704": jax.ShapeDtypeStruct((4, 2 * DEPTH * D_MODEL, 704), BF16),
                     "c192": jax.ShapeDtypeStruct((4, 2 * D_MODEL, 192), BF16),
                     "c256": jax.ShapeDtypeStruct((4, C256_ROWS, RNN_W), BF16),
                     "sm": jax.ShapeDtypeStruct((4, SM_ROWS, 1024), F32)}
        self.pending = None
        self.small_g = [{} for _ in range(DEPTH)]
        self.loss = None

    def _ag_arrays(self, l):
        a, b16, j = self.args, self.b16, l // 2
        r_ff = jnp.concatenate([b16(a["ffn1_w_out"][l]), b16(a["ffn2_w_out"][l])], axis=0)
        if l % 2 == 0:
            return {"a": [b16(a["ffn1_w_in"][l])], "b": [r_ff, b16(a["attn_w_o"][j]), b16(a["attn_w_qkv"][j])],
                    "c": [b16(a["ffn2_w_in"][l])]}
        gates = jnp.concatenate([b16(a["lru_w_ra"][j]).reshape(-1, RNN_W), b16(a["lru_w_rx"][j]).reshape(-1, RNN_W)], axis=0)
        return {"a": [b16(a["ffn1_w_in"][l])], "b": [r_ff, b16(a["lru_w_out"][j]), b16(a["lru_w_in"][j]), gates],
                "c": [b16(a["ffn2_w_in"][l])]}

    def weights(self, l, piece):
        raw, j = self.raw[l][piece], l // 2
        full = lambda g: g.transpose(1, 0, 2).reshape(g.shape[1], -1)
        if piece == "a":
            return {"w_in1": full(raw[0]), "ln_g": self.small["ln_g"][l], "ln_b": self.small["ln_b"][l]}
        if piece == "c":
            return {"w_in2": full(raw[0])}
        wl = {"r_ff": raw[0], "m_o": raw[1], "m_c": full(raw[2])}
        if l % 2:
            g = raw[3].reshape(N_DEV, 2, RNN_BLOCKS, RNN_W // N_DEV, RNN_W).transpose(1, 2, 0, 3, 4)
            g = g.reshape(2, RNN_BLOCKS, RNN_W, RNN_W)
            wl["lru"] = _lru_params(self.small, j, g[0], g[1])
        return wl

    def fwd_slot(self):
        if not self.queue:
            return None
        self.hosting = self.queue.pop(0)
        l, piece = self.hosting
        return comm_ag(self.sent[l][piece])

    def fwd_done(self, outs):
        if self.hosting is not None:
            l, piece = self.hosting
            self.raw.setdefault(l, {})[piece] = outs
            self.hosting = None

    def loss_ready(self, loss):
        self.loss = loss

    def _ranges(self, l, part):
        j = l // 2
        if part == "lo":
            return [("r", [(0, FF_SHARD, l * FF_SHARD)]), ("c704", [(0, D_MODEL, l * D_MODEL)])]
        where = [("r", [(0, FF_SHARD, R_FFN2 + l * FF_SHARD)]),
                 ("r", [(0, MIX_SHARD, (R_LRU if l % 2 else R_ATTN) + j * MIX_SHARD)]),
                 ("c704", [(0, D_MODEL, (DEPTH + l) * D_MODEL)])]
        if l % 2:
            gr = RNN_BLOCKS * (RNN_W // N_DEV)
            return where + [("c256", [(0, D_MODEL, j * D_MODEL)]), ("c256", [(0, gr, C256_RA + j * gr)]),
                            ("c256", [(0, gr, C256_RX + j * gr)])]
        return where + [("c192", [(0, D_MODEL, j * D_MODEL)])]

    def grads_ready(self, l, part, g):
        self.small_g[l].update(g)
        if part == "lo":
            ts = [_row_shards(g["w_out1"]), g["w_in1"]]
        else:
            ts = [_row_shards(g["w_out2"]), _row_shards(g["mix_out"]), g["w_in2"], g["mix_in"]]
            if l % 2:
                ts += [_gate_shards(g["w_ra"]), _gate_shards(g["w_rx"])]
        sends = [t.reshape((4, 2) + t.shape[1:]) for t in ts]
        if (l, part) == (0, "lo"):
            self._last(sends)
        else:
            self.pending = {"tag": f"{l}{part}", "sends": sends, "where": self._ranges(l, part)}

    def _pair_sums(self, tag, sends, gots):
        return pair_sum(sends, gots, [BF16] * len(sends), f"pair_sum_{tag}")

    def _chip(self, parts, where, pick):
        names = list(dict.fromkeys(where[i][0] for i in pick))
        return names, comm_chip([(parts[i], names.index(where[i][0]), where[i][1]) for i in pick], [self.bufs[n] for n in names])

    def bwd_slot(self, slot):
        p = self.pending
        if p is None:
            return None
        if slot == "pair":
            return comm_pair(p["sends"])
        wide = [i for i, (name, _) in enumerate(p["where"]) if name == "c704"]
        pick = wide if slot == "chip_x" else [i for i in range(len(p["sends"])) if i not in wide]
        p["names"], comm = self._chip(p["parts"], p["where"], pick)
        return comm

    def bwd_done(self, slot, outs):
        p = self.pending
        if slot == "pair":
            p["parts"] = self._pair_sums(p["tag"], p["sends"], outs)
            return
        for n, o in zip(p["names"], outs):
            self.bufs[n] = o
        if slot == "chip_y":
            self.pending = None

    def _last(self, sends):
        sg = self.small_g
        stack = lambda key, ls: jnp.stack([sg[l][key] for l in ls])
        every, lru = range(DEPTH), (1, 3)
        small = {"ln_g": stack("ln_g", every), "ln_b": stack("ln_b", every), "lru_conv_w": stack("conv_w", lru),
                 "lru_conv_b": stack("conv_b", lru), "lru_b_ra": stack("b_ra", lru), "lru_b_rx": stack("b_rx", lru),
                 "lru_lambda": stack("lam", lru)}
        tail = jnp.concatenate([stack("sinks", (0, 2)).reshape(-1), self.loss.reshape(1)])
        tail = jnp.pad(tail, (0, SM_ROWS * 1024 - SMALL_N - tail.shape[0]))
        s_sm = jnp.concatenate([_shards_of_full(small[n], shape, axis) for n, shape, axis in SMALL]
                               + [jnp.broadcast_to(tail, (N_DEV, tail.shape[0]))], axis=1).reshape(4, 2, SM_ROWS, 1024)
        gots = run_comm(comm_pair(sends + [s_sm]), "pair_exchange_last")
        parts = pair_sum(sends + [s_sm], gots, [BF16, BF16, F32], "pair_sum_last")
        names, comm = self._chip(parts, self._ranges(0, "lo") + [("sm", [(0, SM_ROWS, 0)])], [0, 1, 2])
        for n, o in zip(names, run_comm(comm, "chip_exchange_last")):
            self.bufs[n] = o


def kernel(x, ffn1_w_in, ffn1_w_out, ffn2_w_in, ffn2_w_out, ln_g, ln_b, attn_w_qkv, attn_sinks, attn_w_o, lru_w_in, lru_conv_w, lru_conv_b, lru_w_ra, lru_b_ra, lru_w_rx, lru_b_rx, lru_lambda, lru_w_out, loss_target, m_ffn1_w_in, m_ffn1_w_out, m_ffn2_w_in, m_ffn2_w_out, m_ln_g, m_ln_b, m_attn_w_qkv, m_attn_sinks, m_attn_w_o, m_lru_w_in, m_lru_conv_w, m_lru_conv_b, m_lru_w_ra, m_lru_b_ra, m_lru_w_rx, m_lru_b_rx, m_lru_lambda, m_lru_w_out, v_ffn1_w_in, v_ffn1_w_out, v_ffn2_w_in, v_ffn2_w_out, v_ln_g, v_ln_b, v_attn_w_qkv, v_attn_sinks, v_attn_w_o, v_lru_w_in, v_lru_conv_w, v_lru_conv_b, v_lru_w_ra, v_lru_b_ra, v_lru_w_rx, v_lru_b_rx, v_lru_lambda, v_lru_w_out):
    args = dict(locals())

    sched = _Fsdp(args)
    _, dx = _local_step(x[0], loss_target[0], sched, attn_sinks)
    p_r, p704, p192, p256, p_sm = (sched.bufs[n] for n in ("r", "c704", "c192", "c256", "sm"))
    sm = sched.sm

    fam = lambda n: (args[n], args["m_" + n], args["v_" + n])
    res = {
        "ffn1_w_out": adamw(p_r, 0, *fam("ffn1_w_out"), FF_SHARD, "adamw_ffn1_w_out"),
        "ffn2_w_out": adamw(p_r, R_FFN2, *fam("ffn2_w_out"), FF_SHARD, "adamw_ffn2_w_out"),
        "attn_w_o": adamw(p_r, R_ATTN, *fam("attn_w_o"), 2 * MIX_SHARD, "adamw_attn_w_o"),
        "lru_w_out": adamw(p_r, R_LRU, *fam("lru_w_out"), 2 * MIX_SHARD, "adamw_lru_w_out"),
        "ffn1_w_in": adamw(p704, 0, *fam("ffn1_w_in"), 512, "adamw_ffn1_w_in"),
        "ffn2_w_in": adamw(p704, DEPTH * D_MODEL, *fam("ffn2_w_in"), 512, "adamw_ffn2_w_in"),
        "attn_w_qkv": adamw(p192, 0, *fam("attn_w_qkv"), 512, "adamw_attn_w_qkv"),
        "lru_w_in": adamw(p256, 0, *fam("lru_w_in"), 512, "adamw_lru_w_in"),
        "lru_w_ra": adamw(p256, C256_RA, *fam("lru_w_ra"), 256, "adamw_lru_w_ra"),
        "lru_w_rx": adamw(p256, C256_RX, *fam("lru_w_rx"), 256, "adamw_lru_w_rx"),
    }
    sm_out = adamw(p_sm, 0, sm, _small_pack(args, "m_"), _small_pack(args, "v_"), SM_ROWS, "adamw_small")
    for k, pack in enumerate(sm_out):
        for n, val in _small_unpack(pack).items():
            res.setdefault(n, [None] * 4)[k] = val
    loss_total = sm_out[0].reshape(-1)[LOSS_OFF]
    out = [loss_total, dx[None]]
    for k in range(4):
        out += [res[n][k] for n in WEIGHTS]
    return tuple(out)
```

```python
import math

import jax
import jax.numpy as jnp
import numpy as np
from jax import lax
from jax.experimental import pallas as pl
from jax.experimental.pallas import tpu as pltpu

F32 = jnp.float32
BF16 = jnp.bfloat16

D_MODEL = 1024
DEPTH = 4
N_HEADS = 16
N_KV = 4
HEAD_DIM = 64
GROUP = 4
BLOCK = 128
ROPE_THETA = 10000.0
D_RNN = 1024
RNN_BLOCKS = 4
RNN_W = 256
CONV_W = 4
LRU_C = 8.0
D_FF = 2816
ALPHA = (2.0 * DEPTH) ** 0.25
LN_EPS = 1e-5
QKV = (N_HEADS + 2 * N_KV) * HEAD_DIM
N_DEV = 8

ADAM_LR = 0.001
ADAM_B1 = 0.9
ADAM_B2 = 0.999
ADAM_EPS = 1e-08
ADAM_WD = 0.01
ADAM_STEP = 10

VMEM_LIMIT = 52 * 1024 * 1024
NEG = float(np.finfo(np.float32).min)

MESH = pl.DeviceIdType.MESH
ANY = pl.BlockSpec(memory_space=pl.ANY)

FF_SHARD = D_FF // N_DEV
MIX_SHARD = D_MODEL // N_DEV
R_FFN2 = DEPTH * FF_SHARD
R_ATTN = 2 * DEPTH * FF_SHARD
R_LRU = R_ATTN + 2 * MIX_SHARD
R_ROWS = R_LRU + 2 * MIX_SHARD
C256_RA = 2 * D_MODEL
C256_RX = C256_RA + 2 * RNN_BLOCKS * (RNN_W // N_DEV)
C256_ROWS = C256_RX + 2 * RNN_BLOCKS * (RNN_W // N_DEV)

SMALL = (
    ("ln_g", (4, 3, 128), 2),
    ("ln_b", (4, 3, 128), 2),
    ("lru_conv_w", (2, 4, 128), 2),
    ("lru_conv_b", (2, 128), 1),
    ("lru_b_ra", (2, 128), 1),
    ("lru_b_rx", (2, 128), 1),
    ("lru_lambda", (2, 128), 1),
)
WEIGHTS = ("ffn1_w_in", "ffn1_w_out", "ffn2_w_in", "ffn2_w_out", "ln_g", "ln_b", "attn_w_qkv", "attn_sinks",
           "attn_w_o", "lru_w_in", "lru_conv_w", "lru_conv_b", "lru_w_ra", "lru_b_ra", "lru_w_rx", "lru_b_rx",
           "lru_lambda", "lru_w_out")
SMALL_N = sum(math.prod(s) for _, s, _ in SMALL)
SINK_OFF = SMALL_N
LOSS_OFF = SMALL_N + 32
SM_ROWS = 8


def _cp(*sem):
    return pltpu.CompilerParams(dimension_semantics=sem, vmem_limit_bytes=VMEM_LIMIT)


def _dot(a, b):
    return jnp.dot(a, b, preferred_element_type=F32)


def _dot_tn(a, b):
    return lax.dot_general(a, b, (((0,), (0,)), ((), ())), preferred_element_type=F32)


def _dot_nt(a, b):
    return lax.dot_general(a, b, (((1,), (1,)), ((), ())), preferred_element_type=F32)


def _col_chunks(cols, size=512):
    return [slice(c, min(c + size, cols)) for c in range(0, cols, size)]


def _ln(z, g, b):
    mu = jnp.mean(z, axis=-1, keepdims=True)
    xc = z - mu
    var = jnp.mean(xc * xc, axis=-1, keepdims=True)
    return xc * lax.rsqrt(var + LN_EPS) * g + b


def _rows_view(pack, rows_per_dev, row0):
    return (pack, (N_DEV, rows_per_dev, D_MODEL), (0, row0 // rows_per_dev, 0), (N_DEV * rows_per_dev, D_MODEL))


def _mat_view(arr, cols=None, cblk=0):
    k, n = arr.shape
    cols = n if cols is None else cols
    return (arr, (k, cols), (0, cblk), (k, cols))


def _vspec(view):
    _, bshape, bidx, _ = view
    return pl.BlockSpec(bshape, lambda *_: bidx)


def _vload(view, ref):
    return ref[...].reshape(view[3])


class Comm:
    def __init__(self, ins, out_shapes, sems, start, finish, aliases=None, relay=None):
        self.ins, self.out_shapes, self.sems = list(ins), list(out_shapes), list(sems)
        self.start, self.finish, self.aliases = start, finish, dict(aliases or {})
        self.relay = relay


def _call(body, *, grid, in_specs, out_specs, out_shape, operands, name, sem, scratch=(), comm=None):
    n_in, n_out, n_scr = len(in_specs), len(out_specs), len(scratch)
    if comm is None:
        return pl.pallas_call(body, grid=grid, in_specs=list(in_specs), out_specs=list(out_specs),
                              out_shape=list(out_shape), scratch_shapes=list(scratch), compiler_params=_cp(*sem),
                              name=name)(*operands), []
    nci, nco = len(comm.ins), len(comm.out_shapes)

    def hosted(*refs):
        ins, cins = refs[:n_in], refs[n_in:n_in + nci]
        o0 = n_in + nci
        outs, couts = refs[o0:o0 + n_out], refs[o0 + n_out:o0 + n_out + nco]
        s0 = o0 + n_out + nco
        scr, csems = refs[s0:s0 + n_scr], refs[s0 + n_scr:]
        step = 0
        for ax, size in enumerate(grid):
            step = step * size + pl.program_id(ax)
        steps = math.prod(grid)

        @pl.when(step == 0)
        def _():
            comm.start(cins, couts, csems)

        if comm.relay is not None and steps >= 4:
            @pl.when(step == (3 * steps) // 4)
            def _():
                comm.relay(cins, couts, csems)

        body(*ins, *outs, *scr)

        @pl.when(step == steps - 1)
        def _():
            if comm.relay is not None and steps < 4:
                comm.relay(cins, couts, csems)
            comm.finish(cins, couts, csems)

    res = pl.pallas_call(
        hosted, grid=grid, in_specs=list(in_specs) + [ANY] * nci, out_specs=list(out_specs) + [ANY] * nco,
        out_shape=list(out_shape) + comm.out_shapes, scratch_shapes=list(scratch) + comm.sems,
        input_output_aliases={n_in + i: n_out + o for i, o in comm.aliases.items()},
        compiler_params=_cp(*(("arbitrary",) * len(grid))), name=name)(*operands, *comm.ins)
    return res[:n_out], res[n_out:]


def run_comm(comm, name):
    nci, nco = len(comm.ins), len(comm.out_shapes)

    def body(*refs):
        cins, couts, csems = refs[:nci], refs[nci:nci + nco], refs[nci + nco:]
        comm.start(cins, couts, csems)
        if comm.relay is not None:
            comm.relay(cins, couts, csems)
        comm.finish(cins, couts, csems)

    return pl.pallas_call(
        body, in_specs=[ANY] * nci, out_specs=[ANY] * nco, out_shape=comm.out_shapes, scratch_shapes=comm.sems,
        input_output_aliases=dict(comm.aliases), name=name)(*comm.ins)


def mm_plain(a, wv, out_dtype, name, nt=False, tm=512):
    S, K = a.shape
    N = wv[3][0] if nt else wv[3][1]
    tm = min(tm, S)
    dot = _dot_nt if nt else _dot

    def body(a_ref, w_ref, o_ref):
        o_ref[...] = dot(a_ref[...], _vload(wv, w_ref)).astype(out_dtype)

    return pl.pallas_call(
        body, grid=(S // tm,),
        in_specs=[pl.BlockSpec((tm, K), lambda i: (i, 0)), _vspec(wv)],
        out_specs=pl.BlockSpec((tm, N), lambda i: (i, 0)),
        out_shape=jax.ShapeDtypeStruct((S, N), out_dtype),
        compiler_params=_cp("parallel"), name=name)(a, wv[0])


def mm_res_nt(a, wv, r, alpha, name, tm=512):
    S, K = a.shape
    N = wv[3][0]
    tm = min(tm, S)

    def body(a_ref, w_ref, r_ref, o_ref):
        o_ref[...] = _dot_nt(a_ref[...], _vload(wv, w_ref)) + alpha * r_ref[...]

    return pl.pallas_call(
        body, grid=(S // tm,),
        in_specs=[pl.BlockSpec((tm, K), lambda i: (i, 0)), _vspec(wv), pl.BlockSpec((tm, N), lambda i: (i, 0))],
        out_specs=pl.BlockSpec((tm, N), lambda i: (i, 0)),
        out_shape=jax.ShapeDtypeStruct((S, N), F32),
        compiler_params=_cp("parallel"), name=name)(a, wv[0], r)


def mm_res2_nt(a3, wv0, wv1, r, alpha, name, tm=512, comm=None):
    _, S, K = a3.shape
    N = wv0[3][0]
    tm = min(tm, S)

    def body(a_ref, w0_ref, w1_ref, r_ref, o_ref):
        o_ref[...] = (_dot_nt(a_ref[0], _vload(wv0, w0_ref)) + _dot_nt(a_ref[1], _vload(wv1, w1_ref))
                      + alpha * r_ref[...])

    (out,), couts = _call(
        body, grid=(S // tm,),
        in_specs=[pl.BlockSpec((2, tm, K), lambda i: (0, i, 0)), _vspec(wv0), _vspec(wv1),
                  pl.BlockSpec((tm, N), lambda i: (i, 0))],
        out_specs=[pl.BlockSpec((tm, N), lambda i: (i, 0))],
        out_shape=[jax.ShapeDtypeStruct((S, N), F32)],
        operands=(a3, wv0[0], wv1[0], r), name=name, sem=("parallel",), comm=comm)
    return out, couts


def mm_ln(a, wv, h, g, b, scale, name, tm=512, comm=None):
    S, K = a.shape
    tm = min(tm, S)

    def body(a_ref, w_ref, h_ref, g_ref, b_ref, z_ref, y_ref, yb_ref):
        z = ALPHA * h_ref[...] + scale * _dot(a_ref[...], _vload(wv, w_ref))
        y = _ln(z, g_ref[...], b_ref[...])
        z_ref[...] = z
        y_ref[...] = y
        yb_ref[...] = y.astype(BF16)

    row = pl.BlockSpec((tm, D_MODEL), lambda i: (i, 0))
    vec = pl.BlockSpec((1, D_MODEL), lambda i: (0, 0))
    return _call(
        body, grid=(S // tm,),
        in_specs=[pl.BlockSpec((tm, K), lambda i: (i, 0)), _vspec(wv), row, vec, vec],
        out_specs=[row, row, row],
        out_shape=[jax.ShapeDtypeStruct((S, D_MODEL), F32), jax.ShapeDtypeStruct((S, D_MODEL), F32),
                   jax.ShapeDtypeStruct((S, D_MODEL), BF16)],
        operands=(a, wv[0], h, g.reshape(1, -1), b.reshape(1, -1)), name=name, sem=("parallel",), comm=comm)


def ffn_up(xb, w_in, name, tm=512, tn=1408, comm=None):
    S = xb.shape[0]
    tm = min(tm, S)
    nj = D_FF // tn

    def body(x_ref, wg_ref, wu_ref, jac_ref, a_ref):
        x = x_ref[...]
        g = _dot(x, wg_ref[...])
        u = _dot(x, wu_ref[...])
        sg = jax.nn.sigmoid(g)
        t = g * sg
        jac_ref[0] = (u * (sg * ((g - t) + 1.0))).astype(BF16)
        jac_ref[1] = t.astype(BF16)
        a_ref[...] = (t * u).astype(BF16)

    return _call(
        body, grid=(nj, S // tm),
        in_specs=[pl.BlockSpec((tm, D_MODEL), lambda j, i: (i, 0)),
                  pl.BlockSpec((D_MODEL, tn), lambda j, i: (0, j)),
                  pl.BlockSpec((D_MODEL, tn), lambda j, i: (0, nj + j))],
        out_specs=[pl.BlockSpec((2, tm, tn), lambda j, i: (0, i, j)), pl.BlockSpec((tm, tn), lambda j, i: (i, j))],
        out_shape=[jax.ShapeDtypeStruct((2, S, D_FF), BF16), jax.ShapeDtypeStruct((S, D_FF), BF16)],
        operands=(xb, w_in, w_in), name=name, sem=("parallel", "parallel"), comm=comm)


def ffn_mid_bwd(dfb, r_ff, blk, gu, name, tm=1024, comm=None):
    S = dfb.shape[0]
    tm = min(tm, S)
    tn = 4 * FF_SHARD

    def body(df_ref, w_ref, gu_ref, dh_ref):
        w = w_ref[...].reshape(tn, D_MODEL)
        df = df_ref[...]
        for cols in _col_chunks(tn):
            da = _dot_nt(df, w[cols, :])
            dh_ref[0, :, cols] = (da * gu_ref[0, :, cols].astype(F32)).astype(BF16)
            dh_ref[1, :, cols] = (da * gu_ref[1, :, cols].astype(F32)).astype(BF16)

    gspec = pl.BlockSpec((2, tm, tn), lambda j, i: (0, i, j))
    (out,), couts = _call(
        body, grid=(2, S // tm),
        in_specs=[pl.BlockSpec((tm, D_MODEL), lambda j, i: (i, 0)),
                  pl.BlockSpec((4, FF_SHARD, D_MODEL), lambda j, i: (j, blk, 0)), gspec],
        out_specs=[gspec],
        out_shape=[jax.ShapeDtypeStruct((2, S, D_FF), BF16)],
        operands=(dfb, r_ff, gu), name=name, sem=("parallel", "parallel"), comm=comm)
    return out, couts


def mm_tn(a, b, name, tm, tn, ts=2048, planes=1, shard=None, comm=None):
    S, M = a.shape
    N = b.shape[-1] * planes
    ts = min(ts, S)
    per = b.shape[-1] // tn
    ns = S // ts

    if shard is None:
        def body(a_ref, b_ref, o_ref):
            @pl.when(pl.program_id(2) == 0)
            def _():
                o_ref[...] = jnp.zeros_like(o_ref)

            o_ref[...] += _dot_tn(a_ref[...], b_ref[...])

        out_spec = pl.BlockSpec((tm, tn), lambda i, j, s: (i, j))
        out_shape = jax.ShapeDtypeStruct((M, N), F32)
        scratch = ()
    else:
        def body(a_ref, b_ref, o_ref, acc_ref):
            s = pl.program_id(2)

            @pl.when(s == 0)
            def _():
                acc_ref[...] = jnp.zeros_like(acc_ref)

            acc_ref[...] += _dot_tn(a_ref[...], b_ref[...])

            @pl.when(s == ns - 1)
            def _():
                for q in range(tn // shard):
                    o_ref[q] = acc_ref[:, shard * q:shard * (q + 1)]

        out_spec = pl.BlockSpec((tn // shard, tm, shard), lambda i, j, s: (j, i, 0))
        out_shape = jax.ShapeDtypeStruct((N // shard, M, shard), F32)
        scratch = (pltpu.VMEM((tm, tn), F32),)

    if planes == 1:
        bspec = pl.BlockSpec((ts, tn), lambda i, j, s: (s, j))
    else:
        bspec = pl.BlockSpec((None, ts, tn), lambda i, j, s: (j // per, s, j % per))
    (out,), couts = _call(
        body, grid=(M // tm, N // tn, ns),
        in_specs=[pl.BlockSpec((ts, tm), lambda i, j, s: (s, i)), bspec],
        out_specs=[out_spec], out_shape=[out_shape], scratch=scratch,
        operands=(a, b), name=name, sem=("parallel", "parallel", "arbitrary"), comm=comm)
    return out, couts


def ln_bwd(dy, z, g, out_scale, name, tm=512):
    S = dy.shape[0]
    tm = min(tm, S)

    def body(dy_ref, z_ref, g_ref, dz_ref, dzb_ref, dg_ref, db_ref):
        @pl.when(pl.program_id(0) == 0)
        def _():
            dg_ref[...] = jnp.zeros_like(dg_ref)
            db_ref[...] = jnp.zeros_like(db_ref)

        z = z_ref[...]
        dy_ = dy_ref[...]
        mu = jnp.mean(z, axis=-1, keepdims=True)
        xc = z - mu
        var = jnp.mean(xc * xc, axis=-1, keepdims=True)
        rstd = lax.rsqrt(var + LN_EPS)
        xh = xc * rstd
        dxh = dy_ * g_ref[...]
        m1 = jnp.mean(dxh, axis=-1, keepdims=True)
        m2 = jnp.mean(dxh * xh, axis=-1, keepdims=True)
        dz = rstd * (dxh - m1 - xh * m2)
        dz_ref[...] = dz
        dzb_ref[...] = (out_scale * dz).astype(BF16)
        dg_ref[...] += jnp.sum(dy_ * xh, axis=0, keepdims=True)
        db_ref[...] += jnp.sum(dy_, axis=0, keepdims=True)

    row = pl.BlockSpec((tm, D_MODEL), lambda i: (i, 0))
    vec = pl.BlockSpec((1, D_MODEL), lambda i: (0, 0))
    return pl.pallas_call(
        body, grid=(S // tm,),
        in_specs=[row, row, vec],
        out_specs=[row, row, vec, vec],
        out_shape=[jax.ShapeDtypeStruct((S, D_MODEL), F32), jax.ShapeDtypeStruct((S, D_MODEL), BF16),
                   jax.ShapeDtypeStruct((1, D_MODEL), F32), jax.ShapeDtypeStruct((1, D_MODEL), F32)],
        compiler_params=_cp("arbitrary"), name=name)(dy, z, g.reshape(1, -1))


def loss_head(y, t, name, tm=512):
    S = y.shape[0]
    tm = min(tm, S)
    nt = S // tm

    def body(y_ref, t_ref, dy_ref, l_ref):
        i = pl.program_id(0)

        @pl.when(i == 0)
        def _():
            l_ref[...] = jnp.zeros_like(l_ref)

        e = y_ref[...] - t_ref[...]
        dy_ref[...] = e * (1.0 / D_MODEL)
        l_ref[...] += jnp.sum(e * e, axis=0, keepdims=True)

        @pl.when(i == nt - 1)
        def _():
            tot = jnp.sum(l_ref[...], axis=1, keepdims=True) * (0.5 / D_MODEL)
            l_ref[...] = jnp.broadcast_to(tot, l_ref.shape)

    row = pl.BlockSpec((tm, D_MODEL), lambda i: (i, 0))
    vec = pl.BlockSpec((1, D_MODEL), lambda i: (0, 0))
    return pl.pallas_call(
        body, grid=(nt,), in_specs=[row, row], out_specs=[row, vec],
        out_shape=[jax.ShapeDtypeStruct((S, D_MODEL), F32), jax.ShapeDtypeStruct((1, D_MODEL), F32)],
        compiler_params=_cp("arbitrary"), name=name)(y, t)


def _rope_tables(S):
    pos = jnp.arange(S, dtype=F32)
    inv_freq = ROPE_THETA ** (-jnp.arange(0, HEAD_DIM, 2, dtype=F32) / HEAD_DIM)
    ang = pos[:, None] * inv_freq[None, :]
    cos, sin = jnp.cos(ang), jnp.sin(ang)
    cosf = jnp.concatenate([cos, cos, cos, cos], axis=1)
    sinf = jnp.concatenate([-sin, sin, -sin, sin], axis=1)
    return cosf, sinf


def _rot(t, c, s, first):
    sw = jnp.where(first, pltpu.roll(t, 96, 1), pltpu.roll(t, 32, 1))
    return t * c + sw * s


def rope_fwd(qkv, cosf, sinf, name, tm=512):
    S = qkv.shape[0]
    tm = min(tm, S)

    def body(x_ref, c_ref, s_ref, q_ref, k_ref, v_ref):
        c = c_ref[...]
        s = s_ref[...]
        first = (lax.broadcasted_iota(jnp.int32, (tm, 128), 1) % HEAD_DIM) < (HEAD_DIM // 2)
        for j in range(8):
            q_ref[:, 128 * j:128 * (j + 1)] = _rot(x_ref[:, 128 * j:128 * (j + 1)], c, s, first).astype(BF16)
        for j in range(2):
            k_ref[:, 128 * j:128 * (j + 1)] = _rot(x_ref[:, 1024 + 128 * j:1024 + 128 * (j + 1)], c, s, first).astype(BF16)
        v_ref[...] = x_ref[:, 1280:1536].astype(BF16)

    tab = pl.BlockSpec((tm, 128), lambda i: (i, 0))
    return pl.pallas_call(
        body, grid=(S // tm,),
        in_specs=[pl.BlockSpec((tm, QKV), lambda i: (i, 0)), tab, tab],
        out_specs=[pl.BlockSpec((tm, 1024), lambda i: (i, 0)), pl.BlockSpec((tm, 256), lambda i: (i, 0)),
                   pl.BlockSpec((tm, 256), lambda i: (i, 0))],
        out_shape=[jax.ShapeDtypeStruct((S, 1024), BF16), jax.ShapeDtypeStruct((S, 256), BF16),
                   jax.ShapeDtypeStruct((S, 256), BF16)],
        compiler_params=_cp("parallel"), name=name)(qkv, cosf, sinf)


def rope_bwd(dq, dkc, dkp, dvc, dvp, cosf, sinf, name, tq):
    S = dq.shape[0]
    nt = S // tq

    def body(dq_ref, dkc_ref, dkp_ref, dvc_ref, dvp_ref, c_ref, s_ref, o_ref):
        i = pl.program_id(0)
        c = c_ref[...]
        s = -s_ref[...]
        first = (lax.broadcasted_iota(jnp.int32, (tq, 128), 1) % HEAD_DIM) < (HEAD_DIM // 2)
        for j in range(8):
            o_ref[:, 128 * j:128 * (j + 1)] = _rot(dq_ref[:, 128 * j:128 * (j + 1)], c, s, first).astype(BF16)
        has_next = i < nt - 1
        rows = lax.broadcasted_iota(jnp.int32, (tq, 256), 0)
        pad = jnp.zeros((tq - BLOCK, 256), F32)
        halo_k = jnp.concatenate([pad, dkp_ref[...]], axis=0)
        halo_v = jnp.concatenate([pad, dvp_ref[...]], axis=0)
        use = jnp.logical_and(has_next, rows >= tq - BLOCK)
        dk = dkc_ref[...] + jnp.where(use, halo_k, 0.0)
        dv = dvc_ref[...] + jnp.where(use, halo_v, 0.0)
        for j in range(2):
            o_ref[:, 1024 + 128 * j:1024 + 128 * (j + 1)] = _rot(dk[:, 128 * j:128 * (j + 1)], c, s, first).astype(BF16)
        o_ref[:, 1280:1536] = dv.astype(BF16)

    tab = pl.BlockSpec((tq, 128), lambda i: (i, 0))
    cur = pl.BlockSpec((tq, 256), lambda i: (i, 0))
    nxt = pl.BlockSpec((BLOCK, 256), lambda i: (jnp.minimum(i + 1, nt - 1), 0))
    return pl.pallas_call(
        body, grid=(nt,),
        in_specs=[pl.BlockSpec((tq, 1024), lambda i: (i, 0)), cur, nxt, cur, nxt, tab, tab],
        out_specs=pl.BlockSpec((tq, QKV), lambda i: (i, 0)),
        out_shape=jax.ShapeDtypeStruct((S, QKV), BF16),
        compiler_params=_cp("parallel"), name=name)(dq, dkc, dkp, dvc, dvp, cosf, sinf)


def _attn_masks(n):
    shape = (GROUP * BLOCK, 2 * BLOCK)
    r = lax.broadcasted_iota(jnp.int32, shape, 0) % BLOCK
    c = lax.broadcasted_iota(jnp.int32, shape, 1)
    ok = jnp.logical_and(c > r, c <= r + BLOCK)
    return jnp.logical_and(ok, jnp.logical_or(c >= BLOCK, n > 0)), ok


def _attn_probs(qs, kw, sink_col, ok):
    s = jnp.where(ok, _dot_nt(qs, kw) * (HEAD_DIM ** -0.5), NEG)
    m = jnp.maximum(jnp.max(s, axis=1, keepdims=True), sink_col)
    p = jnp.exp(s - m)
    es = jnp.exp(sink_col - m)
    inv = 1.0 / (jnp.sum(p, axis=1, keepdims=True) + es)
    return p * inv, es * inv


def _sink_col(sink_ref, g):
    rid = lax.broadcasted_iota(jnp.int32, (GROUP * BLOCK, 1), 0) // BLOCK
    col = jnp.zeros((GROUP * BLOCK, 1), F32)
    for j in range(GROUP):
        col = jnp.where(rid == j, sink_ref[GROUP * g + j], col)
    return col


KV_PAIR = 2
Q_LANES = KV_PAIR * GROUP * HEAD_DIM


def _low_half(shape):
    return lax.broadcasted_iota(jnp.int32, shape, 1) < HEAD_DIM


def _head_rows(ref, lo, gi):
    low = _low_half((BLOCK, 2 * HEAD_DIM))
    out = []
    for s in range(GROUP // 2):
        c0 = gi * GROUP * HEAD_DIM + 2 * HEAD_DIM * s
        x = ref[lo:lo + BLOCK, c0:c0 + 2 * HEAD_DIM]
        out += [jnp.where(low, x, jnp.zeros_like(x)), jnp.where(low, jnp.zeros_like(x), x)]
    return jnp.concatenate(out, axis=0)


def _head_slabs(x):
    low = _low_half((BLOCK, 2 * HEAD_DIM))
    return [jnp.where(low, x[2 * BLOCK * s:2 * BLOCK * s + BLOCK], x[2 * BLOCK * s + BLOCK:2 * BLOCK * (s + 1)])
            for s in range(GROUP // 2)]


def _kv_window(kc_ref, kp_ref, b, gi):
    if b == 0:
        x = jnp.concatenate([kp_ref[...], kc_ref[0:BLOCK, :]], axis=0)
    else:
        x = kc_ref[BLOCK * (b - 1):BLOCK * (b + 1), :]
    other = pltpu.roll(x, HEAD_DIM, 1)
    low = _low_half(x.shape)
    return jnp.where(low, x, other) if gi == 0 else jnp.where(low, other, x)


def _fold_halves(x, gi):
    tot = x + pltpu.roll(x, HEAD_DIM, 1)
    low = _low_half(x.shape)
    return jnp.where(low if gi == 0 else jnp.logical_not(low), tot, 0.0)


def _attn_specs(tq):
    nsub = tq // BLOCK
    qspec = pl.BlockSpec((tq, Q_LANES), lambda p, n: (n, p))
    cur = pl.BlockSpec((tq, KV_PAIR * HEAD_DIM), lambda p, n: (n, p))
    prev = pl.BlockSpec((BLOCK, KV_PAIR * HEAD_DIM), lambda p, n: (jnp.maximum(n * nsub - 1, 0), p))
    return qspec, cur, prev


def attn_fwd(q, k, v, sinks, name, tq, comm=None):
    S = q.shape[0]
    nsub = tq // BLOCK

    def body(sink_ref, q_ref, kc_ref, kp_ref, vc_ref, vp_ref, o_ref):
        p = pl.program_id(0)
        n = pl.program_id(1)
        ok_first, ok_rest = _attn_masks(n)
        for gi in range(KV_PAIR):
            sink_col = _sink_col(sink_ref, KV_PAIR * p + gi)
            base = gi * GROUP * HEAD_DIM
            for b in range(nsub):
                lo = BLOCK * b
                qs = _head_rows(q_ref, lo, gi)
                pn, _ = _attn_probs(qs, _kv_window(kc_ref, kp_ref, b, gi), sink_col, ok_rest if b else ok_first)
                o = _dot(pn.astype(BF16), _kv_window(vc_ref, vp_ref, b, gi))
                for s, slab in enumerate(_head_slabs(o)):
                    c0 = base + 2 * HEAD_DIM * s
                    o_ref[lo:lo + BLOCK, c0:c0 + 2 * HEAD_DIM] = slab.astype(BF16)

    qspec, cur, prev = _attn_specs(tq)
    (out,), couts = _call(
        body, grid=(N_KV // KV_PAIR, S // tq),
        in_specs=[pl.BlockSpec(memory_space=pltpu.SMEM), qspec, cur, prev, cur, prev],
        out_specs=[qspec],
        out_shape=[jax.ShapeDtypeStruct((S, N_HEADS * HEAD_DIM), BF16)],
        operands=(sinks, q, k, k, v, v), name=name, sem=("parallel", "parallel"), comm=comm)
    return out, couts


def attn_bwd(q, k, v, do, sinks, name, tq):
    S = q.shape[0]
    nsub = tq // BLOCK
    nt = S // tq

    def body(sink_ref, q_ref, kc_ref, kp_ref, vc_ref, vp_ref, do_ref, dq_ref, dkc_ref, dkp_ref, dvc_ref, dvp_ref, ds_ref):
        p = pl.program_id(0)
        n = pl.program_id(1)

        @pl.when(n == 0)
        def _():
            ds_ref[...] = jnp.zeros_like(ds_ref)

        for ref in (dkc_ref, dvc_ref, dkp_ref, dvp_ref):
            ref[...] = jnp.zeros_like(ref)
        rid = lax.broadcasted_iota(jnp.int32, (GROUP * BLOCK, 1), 0) // BLOCK
        sub = lax.broadcasted_iota(jnp.int32, (8, 128), 0)
        ok_first, ok_rest = _attn_masks(n)
        for gi in range(KV_PAIR):
            sink_col = _sink_col(sink_ref, KV_PAIR * p + gi)
            base = gi * GROUP * HEAD_DIM
            dsink = jnp.zeros((8, 128), F32)
            for b in range(nsub):
                lo = BLOCK * b
                qs = _head_rows(q_ref, lo, gi)
                dos = _head_rows(do_ref, lo, gi)
                kw = _kv_window(kc_ref, kp_ref, b, gi)
                vw = _kv_window(vc_ref, vp_ref, b, gi)
                pn, ps = _attn_probs(qs, kw, sink_col, ok_rest if b else ok_first)
                dp = _dot_nt(dos, vw)
                delta = jnp.sum(pn * dp, axis=1, keepdims=True)
                dsb = (pn * (dp - delta) * (HEAD_DIM ** -0.5)).astype(BF16)
                for s, slab in enumerate(_head_slabs(_dot(dsb, kw))):
                    c0 = base + 2 * HEAD_DIM * s
                    dq_ref[lo:lo + BLOCK, c0:c0 + 2 * HEAD_DIM] = slab
                dkw = _fold_halves(_dot_tn(dsb, qs), gi)
                dvw = _fold_halves(_dot_tn(pn.astype(BF16), dos), gi)
                if b == 0:
                    dkp_ref[...] += dkw[0:BLOCK]
                    dvp_ref[...] += dvw[0:BLOCK]
                else:
                    dkc_ref[lo - BLOCK:lo, :] += dkw[0:BLOCK]
                    dvc_ref[lo - BLOCK:lo, :] += dvw[0:BLOCK]
                dkc_ref[lo:lo + BLOCK, :] += dkw[BLOCK:2 * BLOCK]
                dvc_ref[lo:lo + BLOCK, :] += dvw[BLOCK:2 * BLOCK]
                sd = ps * delta
                for j in range(GROUP):
                    tot = -jnp.sum(jnp.where(rid == j, sd, 0.0))
                    dsink = dsink + jnp.where(sub == j, tot, 0.0)
            ds_ref[gi] += dsink

    qspec, cur, prev = _attn_specs(tq)
    halo = pl.BlockSpec((BLOCK, KV_PAIR * HEAD_DIM), lambda p, n: (n, p))
    kv_shape = jax.ShapeDtypeStruct((S, N_KV * HEAD_DIM), F32)
    halo_shape = jax.ShapeDtypeStruct((nt * BLOCK, N_KV * HEAD_DIM), F32)
    return pl.pallas_call(
        body, grid=(N_KV // KV_PAIR, nt),
        in_specs=[pl.BlockSpec(memory_space=pltpu.SMEM), qspec, cur, prev, cur, prev, qspec],
        out_specs=[qspec, cur, halo, cur, halo, pl.BlockSpec((KV_PAIR, 8, 128), lambda p, n: (p, 0, 0))],
        out_shape=[jax.ShapeDtypeStruct((S, N_HEADS * HEAD_DIM), F32), kv_shape, halo_shape, kv_shape, halo_shape,
                   jax.ShapeDtypeStruct((N_KV, 8, 128), F32)],
        compiler_params=_cp("parallel", "arbitrary"), name=name)(sinks, q, k, k, v, v, do)


def _rows_before(cur, prev8, k):
    if k == 0:
        return cur
    n = cur.shape[0]
    ext = jnp.concatenate([prev8, cur], axis=0)
    return ext[8 - k:8 - k + n]


def _rows_after(cur, next8, k):
    if k == 0:
        return cur
    n = cur.shape[0]
    ext = jnp.concatenate([cur, next8], axis=0)
    return ext[k:k + n]


def _gelu(x):
    c = math.sqrt(2.0 / math.pi)
    t = jnp.tanh(c * (x + 0.044715 * (x * x * x)))
    return 0.5 * (1.0 + t), t


def _neg_expm1(u):
    ser = 1.0 + u * (1.0 / 6.0)
    for k in range(5, 1, -1):
        ser = 1.0 + (u * (1.0 / k)) * ser
    return jnp.where(u > -0.125, -(u * ser), 1.0 - jnp.exp(u))


def _block_diag(xb16, w_ref):
    return jnp.concatenate([_dot(xb16[:, RNN_W * n:RNN_W * (n + 1)], w_ref[n]) for n in range(RNN_BLOCKS)], axis=1)


def _lru_gates(xb, prev8, cw_ref, cb_ref, wra_ref, wrx_ref, bra_ref, brx_ref, lsl_ref):
    xc = cb_ref[...] + cw_ref[3:4, :] * xb
    for w in range(CONV_W - 1):
        xc = xc + cw_ref[w:w + 1, :] * _rows_before(xb, prev8, CONV_W - 1 - w)
    xcb = xc.astype(BF16)
    r = jax.nn.sigmoid(_block_diag(xcb, wra_ref) + bra_ref[...])
    ig = jax.nn.sigmoid(_block_diag(xcb, wrx_ref) + brx_ref[...])
    la = LRU_C * r * lsl_ref[...]
    a = jnp.exp(la)
    sq = jnp.sqrt(_neg_expm1(2.0 * la))
    return xc, xcb, r, ig, a, sq


def lru_fwd(xg, p, name, tm=256, comm=None):
    S = xg.shape[0]
    tm = min(tm, S)

    def body(xg_ref, xp_ref, cw_ref, cb_ref, wra_ref, wrx_ref, bra_ref, brx_ref, lsl_ref, y_ref, h_ref, hc_ref, a_s, b_s):
        i = pl.program_id(0)
        xb = xg_ref[:, 0:D_RNN]
        gb = xg_ref[:, D_RNN:2 * D_RNN]
        prev8 = jnp.where(i > 0, xp_ref[:, 0:D_RNN], 0.0)
        xc, _, r, ig, a, sq = _lru_gates(xb, prev8, cw_ref, cb_ref, wra_ref, wrx_ref, bra_ref, brx_ref, lsl_ref)
        a_s[...] = a
        b_s[...] = sq * (ig * xc)

        @pl.when(i == 0)
        def _():
            hc_ref[...] = jnp.zeros_like(hc_ref)

        def chunk(c, h):
            o = pl.multiple_of(c * 8, 8)
            av = a_s[pl.ds(o, 8), :]
            bv = b_s[pl.ds(o, 8), :]
            rows = []
            for t in range(8):
                h = av[t:t + 1, :] * h + bv[t:t + 1, :]
                rows.append(h)
            h_ref[pl.ds(o, 8), :] = jnp.concatenate(rows, axis=0)
            return h

        h_last = lax.fori_loop(0, tm // 8, chunk, hc_ref[0:1, :])
        hc_ref[0:1, :] = h_last
        cdf, _ = _gelu(gb)
        y_ref[...] = (h_ref[...] * (gb * cdf)).astype(BF16)

    vec = pl.BlockSpec((1, D_RNN), lambda i: (0, 0))
    wsp = pl.BlockSpec((RNN_BLOCKS, RNN_W, RNN_W), lambda i: (0, 0, 0))
    return _call(
        body, grid=(S // tm,),
        in_specs=[pl.BlockSpec((tm, 2 * D_RNN), lambda i: (i, 0)),
                  pl.BlockSpec((8, 2 * D_RNN), lambda i: (jnp.maximum(i * (tm // 8) - 1, 0), 0)),
                  pl.BlockSpec((CONV_W, D_RNN), lambda i: (0, 0)), vec, wsp, wsp, vec, vec, vec],
        out_specs=[pl.BlockSpec((tm, D_RNN), lambda i: (i, 0)), pl.BlockSpec((tm, D_RNN), lambda i: (i, 0))],
        out_shape=[jax.ShapeDtypeStruct((S, D_RNN), BF16), jax.ShapeDtypeStruct((S, D_RNN), F32)],
        scratch=[pltpu.VMEM((8, D_RNN), F32), pltpu.VMEM((tm, D_RNN), F32), pltpu.VMEM((tm, D_RNN), F32)],
        operands=(xg, xg, p["conv_w"], p["conv_b"], p["w_ra"], p["w_rx"], p["b_ra"], p["b_rx"], p["lsl"]),
        name=name, sem=("arbitrary",), comm=comm)


def lru_bwd(dy, xg, h, p, name, tm=256):
    S = xg.shape[0]
    tm = min(tm, S)
    nt = S // tm

    def body(dy_ref, xg_ref, xp_ref, h_ref, hp_ref, cw_ref, cb_ref, wra_ref, wrx_ref, bra_ref, brx_ref,
             lsl_ref, dxg_ref, dcw_ref, dcb_ref, dwra_ref, dwrx_ref, dbra_ref, dbrx_ref, dlam_ref,
             lc_ref, nx_ref, a_s, g_s, l_s):
        i = pl.program_id(0)
        ti = nt - 1 - i

        @pl.when(i == 0)
        def _():
            lc_ref[...] = jnp.zeros_like(lc_ref)
            nx_ref[...] = jnp.zeros_like(nx_ref)
            for ref in (dcw_ref, dcb_ref, dwra_ref, dwrx_ref, dbra_ref, dbrx_ref, dlam_ref):
                ref[...] = jnp.zeros_like(ref)

        xb = xg_ref[:, 0:D_RNN]
        gb = xg_ref[:, D_RNN:2 * D_RNN]
        prev8 = jnp.where(ti > 0, xp_ref[:, 0:D_RNN], 0.0)
        xc, xcb, r, ig, a, sq = _lru_gates(xb, prev8, cw_ref, cb_ref, wra_ref, wrx_ref, bra_ref, brx_ref, lsl_ref)
        hh = h_ref[...]
        hprev = _rows_before(hh, jnp.where(ti > 0, hp_ref[...], 0.0), 1)
        dy_ = dy_ref[...]
        cdf, th = _gelu(gb)
        c0 = math.sqrt(2.0 / math.pi)
        dgate = cdf + gb * (0.5 * (1.0 - th * th) * c0 * (1.0 + 3.0 * 0.044715 * gb * gb))
        dgb = dy_ * hh * dgate
        a_s[...] = a
        g_s[...] = dy_ * (gb * cdf)

        def chunk(cc, carry):
            o = pl.multiple_of((tm // 8 - 1 - cc) * 8, 8)
            av = a_s[pl.ds(o, 8), :]
            gv = g_s[pl.ds(o, 8), :]
            rows = [None] * 8
            for t in range(7, -1, -1):
                lam_t = gv[t:t + 1, :] + carry
                rows[t] = lam_t
                carry = av[t:t + 1, :] * lam_t
            l_s[pl.ds(o, 8), :] = jnp.concatenate(rows, axis=0)
            return carry

        carry = lax.fori_loop(0, tm // 8, chunk, lc_ref[0:1, :])
        lc_ref[0:1, :] = carry
        lam = l_s[...]
        da = lam * hprev
        dixc = lam * sq
        di = dixc * xc
        dxc = dixc * ig
        dsq = lam * (ig * xc)
        dla = da * a - dsq * (a * a / sq)
        dr = dla * (LRU_C * lsl_ref[...])
        dlam_ref[...] += jnp.sum(dla * (LRU_C * r), axis=0, keepdims=True)
        dpr = dr * r * (1.0 - r)
        dpi = di * ig * (1.0 - ig)
        dbra_ref[...] += jnp.sum(dpr, axis=0, keepdims=True)
        dbrx_ref[...] += jnp.sum(dpi, axis=0, keepdims=True)
        dprb = dpr.astype(BF16)
        dpib = dpi.astype(BF16)
        back = []
        for n in range(RNN_BLOCKS):
            sl = slice(RNN_W * n, RNN_W * (n + 1))
            dwra_ref[n] += _dot_tn(xcb[:, sl], dprb[:, sl])
            dwrx_ref[n] += _dot_tn(xcb[:, sl], dpib[:, sl])
            back.append(_dot_nt(dprb[:, sl], wra_ref[n]) + _dot_nt(dpib[:, sl], wrx_ref[n]))
        dxc = dxc + jnp.concatenate(back, axis=1)
        dcb_ref[...] += jnp.sum(dxc, axis=0, keepdims=True)
        next8 = nx_ref[...]
        dxb = cw_ref[3:4, :] * dxc
        dcw_ref[3:4, :] += jnp.sum(dxc * xb, axis=0, keepdims=True)
        for w in range(CONV_W - 1):
            k = CONV_W - 1 - w
            dcw_ref[w:w + 1, :] += jnp.sum(dxc * _rows_before(xb, prev8, k), axis=0, keepdims=True)
            dxb = dxb + cw_ref[w:w + 1, :] * _rows_after(dxc, next8, k)
        nx_ref[...] = dxc[0:8, :]
        dxg_ref[:, 0:D_RNN] = dxb.astype(BF16)
        dxg_ref[:, D_RNN:2 * D_RNN] = dgb.astype(BF16)

    rev = lambda i: (nt - 1 - i, 0)
    before = lambda i: (jnp.maximum((nt - 1 - i) * (tm // 8) - 1, 0), 0)
    vec = pl.BlockSpec((1, D_RNN), lambda i: (0, 0))
    wsp = pl.BlockSpec((RNN_BLOCKS, RNN_W, RNN_W), lambda i: (0, 0, 0))
    cwsp = pl.BlockSpec((CONV_W, D_RNN), lambda i: (0, 0))
    return pl.pallas_call(
        body, grid=(nt,),
        in_specs=[pl.BlockSpec((tm, D_RNN), rev), pl.BlockSpec((tm, 2 * D_RNN), rev), pl.BlockSpec((8, 2 * D_RNN), before),
                  pl.BlockSpec((tm, D_RNN), rev), pl.BlockSpec((8, D_RNN), before),
                  cwsp, vec, wsp, wsp, vec, vec, vec],
        out_specs=[pl.BlockSpec((tm, 2 * D_RNN), rev), cwsp, vec, wsp, wsp, vec, vec, vec],
        out_shape=[jax.ShapeDtypeStruct((S, 2 * D_RNN), BF16), jax.ShapeDtypeStruct((CONV_W, D_RNN), F32),
                   jax.ShapeDtypeStruct((1, D_RNN), F32), jax.ShapeDtypeStruct((RNN_BLOCKS, RNN_W, RNN_W), F32),
                   jax.ShapeDtypeStruct((RNN_BLOCKS, RNN_W, RNN_W), F32), jax.ShapeDtypeStruct((1, D_RNN), F32),
                   jax.ShapeDtypeStruct((1, D_RNN), F32), jax.ShapeDtypeStruct((1, D_RNN), F32)],
        scratch_shapes=[pltpu.VMEM((8, D_RNN), F32), pltpu.VMEM((8, D_RNN), F32), pltpu.VMEM((tm, D_RNN), F32),
                        pltpu.VMEM((tm, D_RNN), F32), pltpu.VMEM((tm, D_RNN), F32)],
        compiler_params=_cp("arbitrary"), name=name)(
            dy, xg, xg, h, h, p["conv_w"], p["conv_b"], p["w_ra"], p["w_rx"], p["b_ra"], p["b_rx"], p["lsl"])


def _place():
    x, y, c = lax.axis_index("x"), lax.axis_index("y"), lax.axis_index("c")
    chips = [(1 - x, y), (x, 1 - y), (1 - x, 1 - y)]
    return x, y, c, chips


def comm_ag(arrs):
    n = len(arrs)

    def copies(ins, outs, sems):
        send_sems, recv_sems, local_sems = sems
        x, y, c, chips = _place()
        me, sibling = (x, y, c), (x, y, 1 - c)

        def rows(a, px, py, pc):
            return outs[a].at[4 * px + 2 * py + pc]

        def copy(a, k, block, to, own=False):
            return pltpu.make_async_remote_copy(
                src_ref=ins[a] if own else rows(a, *block), dst_ref=rows(a, *block),
                send_sem=send_sems.at[k, a], recv_sem=recv_sems.at[k, a], device_id=to, device_id_type=MESH)

        mine = [pltpu.make_async_copy(ins[a], rows(a, *me), local_sems.at[a]) for a in range(n)]
        first = [copy(a, 1 + j, me, (*chip, c), own=True) for j, chip in enumerate(chips) for a in range(n)]
        first += [copy(a, 0, me, sibling, own=True) for a in range(n)]
        return copy, chips, c, me, sibling, mine, first

    def start(ins, outs, sems):
        _, _, _, _, _, mine, first = copies(ins, outs, sems)
        for cp in mine + first:
            cp.start()

    def relay(ins, outs, sems):
        copy, chips, c, me, sibling, _, _ = copies(ins, outs, sems)
        for j, chip in enumerate(chips):
            for a in range(n):
                copy(a, 1 + j, (*chip, c), me).wait_recv()
                copy(a, 4 + j, (*chip, c), sibling).start()

    def finish(ins, outs, sems):
        copy, chips, c, me, sibling, mine, first = copies(ins, outs, sems)
        passed = [copy(a, 4 + j, (*chip, c), sibling) for j, chip in enumerate(chips) for a in range(n)]
        for a in range(n):
            copy(a, 0, sibling, me).wait_recv()
        for j, chip in enumerate(chips):
            for a in range(n):
                copy(a, 4 + j, (*chip, 1 - c), me).wait_recv()
        for cp in first + passed:
            cp.wait_send()
        for cp in mine:
            cp.wait()

    return Comm(arrs, [jax.ShapeDtypeStruct((N_DEV,) + p.shape, p.dtype) for p in arrs],
                [pltpu.SemaphoreType.DMA((7, n)), pltpu.SemaphoreType.DMA((7, n)), pltpu.SemaphoreType.DMA((n,))],
                start, finish, relay=relay)


def comm_pair(sends):
    n = len(sends)

    def copies(ins, outs, sems):
        send_sems, recv_sems = sems
        x, y, c, _ = _place()
        return [pltpu.make_async_remote_copy(
            src_ref=ins[a].at[k, 1 - c], dst_ref=outs[a].at[k], send_sem=send_sems.at[k, a], recv_sem=recv_sems.at[k, a],
            device_id=(x, y, 1 - c), device_id_type=MESH) for k in range(4) for a in range(n)]

    def start(ins, outs, sems):
        for cp in copies(ins, outs, sems):
            cp.start()

    def finish(ins, outs, sems):
        for cp in copies(ins, outs, sems):
            cp.wait()

    return Comm(sends, [jax.ShapeDtypeStruct((4,) + s.shape[2:], s.dtype) for s in sends],
                [pltpu.SemaphoreType.DMA((4, n)), pltpu.SemaphoreType.DMA((4, n))], start, finish)


def pair_sum(sends, gots, out_dtypes, name, steps=2):
    n = len(sends)
    c = lax.axis_index("c").astype(jnp.int32).reshape(1)
    steps = [steps if s.shape[2] % (16 * steps) == 0 else 1 for s in sends]

    def body(c_ref, *refs):
        for a in range(n):
            refs[2 * n + a][...] = (refs[a][...] + refs[n + a][...]).astype(out_dtypes[a])

    def specs(a, own):
        _, _, R, C = sends[a].shape
        tr, last = R // steps[a], steps[a] - 1
        if own:
            return pl.BlockSpec((None, None, tr, C), lambda k, i, cr: (k, cr[0], jnp.minimum(i, last), 0))
        return pl.BlockSpec((None, tr, C), lambda k, i, cr: (k, jnp.minimum(i, last), 0))

    return pl.pallas_call(
        body,
        grid_spec=pltpu.PrefetchScalarGridSpec(
            num_scalar_prefetch=1, grid=(4, max(steps)),
            in_specs=[specs(a, True) for a in range(n)] + [specs(a, False) for a in range(n)],
            out_specs=[specs(a, False) for a in range(n)]),
        out_shape=[jax.ShapeDtypeStruct((4,) + s.shape[2:], d) for s, d in zip(sends, out_dtypes)],
        compiler_params=_cp("parallel", "arbitrary"), name=name)(c, *sends, *gots)


def comm_chip(items, bufs):
    ns = len(items)
    segs = [(i, b, s0, nr, d0) for i, (_, b, ranges) in enumerate(items) for (s0, nr, d0) in ranges]

    def copies(ins, outs, sems):
        send_sems, recv_sems, local_sems = sems
        x, y, c, chips = _place()
        mychip = 2 * x + y
        mine = [pltpu.make_async_copy(ins[i].at[mychip, pl.ds(s0, nr)], outs[b].at[mychip, pl.ds(d0, nr)], local_sems.at[q])
                for q, (i, b, s0, nr, d0) in enumerate(segs)]
        remote = [pltpu.make_async_remote_copy(
            src_ref=ins[i].at[2 * px + py, pl.ds(s0, nr)], dst_ref=outs[b].at[mychip, pl.ds(d0, nr)],
            send_sem=send_sems.at[j, q], recv_sem=recv_sems.at[j, q], device_id=(px, py, c), device_id_type=MESH)
            for j, (px, py) in enumerate(chips) for q, (i, b, s0, nr, d0) in enumerate(segs)]
        return mine, remote

    def start(ins, outs, sems):
        mine, remote = copies(ins, outs, sems)
        for cp in mine + remote:
            cp.start()

    def finish(ins, outs, sems):
        mine, remote = copies(ins, outs, sems)
        for cp in remote + mine:
            cp.wait()

    q = len(segs)
    old = [b for b, buf in enumerate(bufs) if not isinstance(buf, jax.ShapeDtypeStruct)]
    return Comm([it[0] for it in items] + [bufs[b] for b in old], [jax.ShapeDtypeStruct(b.shape, b.dtype) for b in bufs],
                [pltpu.SemaphoreType.DMA((3, q)), pltpu.SemaphoreType.DMA((3, q)), pltpu.SemaphoreType.DMA((q,))],
                start, finish, aliases={ns + k: b for k, b in enumerate(old)})


def adamw(parts, row0, w, m, v, tr, name):
    C = w.shape[-1]
    rows = w.size // C
    off = row0 // tr

    def body(p_ref, w_ref, m_ref, v_ref, g_ref, d_ref, nm_ref, nv_ref):
        g = ((p_ref[0].astype(F32) + p_ref[1].astype(F32)) + p_ref[2].astype(F32)) + p_ref[3].astype(F32)
        m2 = ADAM_B1 * m_ref[...] + (1.0 - ADAM_B1) * g
        v2 = ADAM_B2 * v_ref[...] + (1.0 - ADAM_B2) * (g * g)
        mh = m2 / (1.0 - ADAM_B1 ** ADAM_STEP)
        vh = v2 / (1.0 - ADAM_B2 ** ADAM_STEP)
        g_ref[...] = g
        d_ref[...] = -ADAM_LR * (mh / (jnp.sqrt(vh) + ADAM_EPS) + ADAM_WD * w_ref[...])
        nm_ref[...] = m2
        nv_ref[...] = v2

    row = pl.BlockSpec((tr, C), lambda i: (i, 0))
    shp = jax.ShapeDtypeStruct((rows, C), F32)
    outs = pl.pallas_call(
        body, grid=(rows // tr,),
        in_specs=[pl.BlockSpec((4, tr, C), lambda i: (0, off + i, 0)), row, row, row],
        out_specs=[row, row, row, row], out_shape=[shp, shp, shp, shp],
        compiler_params=_cp("parallel"), name=name)(parts, w.reshape(rows, C), m.reshape(rows, C), v.reshape(rows, C))
    return [o.reshape(w.shape) for o in outs]


def _full_from_gathered(flat, shape, axis):
    t = jnp.moveaxis(flat.reshape((N_DEV,) + shape), 0, axis)
    return t.reshape(shape[:axis] + (N_DEV * shape[axis],) + shape[axis + 1:])


def _shards_of_full(full, shape, axis):
    t = full.reshape(shape[:axis] + (N_DEV, shape[axis]) + shape[axis + 1:])
    return jnp.moveaxis(t, axis, 0).reshape(N_DEV, -1)


def _small_pack(args, prefix):
    flat = jnp.concatenate([args[prefix + n].reshape(-1) for n, _, _ in SMALL] + [args[prefix + "attn_sinks"].reshape(-1)])
    return jnp.pad(flat, (0, SM_ROWS * 1024 - flat.shape[0])).reshape(SM_ROWS, 1024)


def _small_unpack(pack):
    flat = pack.reshape(-1)
    out, off = {}, 0
    for n, shape, _ in SMALL:
        size = math.prod(shape)
        out[n] = flat[off:off + size].reshape(shape)
        off += size
    out["attn_sinks"] = flat[SINK_OFF:SINK_OFF + 32].reshape(2, 16)
    return out


def _tn_tile(n):
    return next(t for t in (1408, 1024, 768, 512, 256, 128) if n % t == 0)


def _dw(a, b, name, planes=1, shard=None, comm=None):
    return mm_tn(a, b, name, _tn_tile(a.shape[1]), _tn_tile(b.shape[-1]), planes=planes, shard=shard, comm=comm)


class _NoExchange:
    def __init__(self, layers):
        self.layers, self.grads = layers, [{} for _ in range(DEPTH)]

    def weights(self, l, piece):
        return self.layers[l]

    def fwd_slot(self):
        return None

    def fwd_done(self, outs):
        pass

    def loss_ready(self, loss):
        pass

    def bwd_slot(self, slot):
        return None

    def bwd_done(self, slot, outs):
        pass

    def grads_ready(self, l, part, g):
        self.grads[l].update(g)


def _local_step(x, target, sched, sinks):
    S = x.shape[0]
    tq = min(512, S)
    cosf, sinf = _rope_tables(S)
    saved = []
    h, hb = x, x.astype(BF16)
    for l in range(DEPTH):
        j = l // 2
        wl = {}
        sv = {"h0b": hb, "wl": wl}
        wl.update(sched.weights(l, "a"))
        (sv["gu1"], a1), co = ffn_up(hb, wl["w_in1"], f"ffn1_up_{l}", comm=sched.fwd_slot())
        sched.fwd_done(co)
        sv["a1"] = a1
        wl.update(sched.weights(l, "b"))
        (sv["z1"], h, hb), _ = mm_ln(a1, _rows_view(wl["r_ff"], FF_SHARD, 0), h, wl["ln_g"][0], wl["ln_b"][0], 0.5,
                                     f"ffn1_down_ln_{l}")
        sv["h1b"] = hb
        if l % 2 == 0:
            qkv = mm_plain(hb, _mat_view(wl["m_c"]), F32, f"attn_qkv_{l}")
            qr, kr, vv = rope_fwd(qkv, cosf, sinf, f"rope_{l}")
            mix_in, co = attn_fwd(qr, kr, vv, sinks[j], f"attn_core_{l}", tq, comm=sched.fwd_slot())
            sv.update(qr=qr, kr=kr, vv=vv)
        else:
            xg = mm_plain(hb, _mat_view(wl["m_c"]), F32, f"lru_in_{l}")
            (mix_in, hstate), co = lru_fwd(xg, wl["lru"], f"lru_core_{l}", comm=sched.fwd_slot())
            sv.update(xg=xg, hstate=hstate)
        sched.fwd_done(co)
        sv["mix_in"] = mix_in
        (sv["z2"], h, hb), _ = mm_ln(mix_in, _rows_view(wl["m_o"], MIX_SHARD, 0), h, wl["ln_g"][1], wl["ln_b"][1], 1.0,
                                     f"mix_out_ln_{l}")
        sv["h2b"] = hb
        wl.update(sched.weights(l, "c"))
        (sv["gu2"], a2), co = ffn_up(hb, wl["w_in2"], f"ffn2_up_{l}", comm=sched.fwd_slot())
        sched.fwd_done(co)
        sv["a2"] = a2
        (sv["z3"], h, hb), _ = mm_ln(a2, _rows_view(wl["r_ff"], FF_SHARD, FF_SHARD), h, wl["ln_g"][2], wl["ln_b"][2], 0.5,
                                     f"ffn2_down_ln_{l}")
        saved.append(sv)

    dy, lvec = loss_head(h, target, "loss_head")
    loss = lvec[0, 0]
    sched.loss_ready(loss)

    def hosted(slot, on, fn):
        comm = sched.bwd_slot(slot) if on else None
        out, co = fn(comm)
        if comm is not None:
            sched.bwd_done(slot, co)
        return out

    def ffn_bwd(dy, z, g, gu, a, xin_b, r_ff, blk, w_in, tag, host):
        dz, dzb, dg, db = ln_bwd(dy, z, g, 0.5, f"ln_bwd_{tag}")
        dh, _ = ffn_mid_bwd(dzb, r_ff, blk, gu, f"ffn_mid_bwd_{tag}")
        d_wout = hosted("pair", host, lambda cm: _dw(a, dzb, f"dw_out_{tag}", comm=cm))
        d_win = hosted("chip_x", host, lambda cm: _dw(xin_b, dh, f"dw_in_{tag}", planes=2, shard=2 * D_FF // N_DEV, comm=cm))
        dx = hosted("chip_y", host, lambda cm: mm_res2_nt(dh, _mat_view(w_in, D_FF, 0), _mat_view(w_in, D_FF, 1), dz, ALPHA,
                                                          f"ffn_dx_{tag}", comm=cm))
        return dx, dg, db, d_wout, d_win

    for l in reversed(range(DEPTH)):
        j = l // 2
        sv = saved[l]
        wl = sv["wl"]
        gl = {}
        dg, db = [None] * 3, [None] * 3
        dy, dg[2], db[2], gl["w_out2"], gl["w_in2"] = ffn_bwd(
            dy, sv["z3"], wl["ln_g"][2], sv["gu2"], sv["a2"], sv["h2b"], wl["r_ff"], 1, wl["w_in2"], f"2_{l}", True)
        dz, dzb, dg[1], db[1] = ln_bwd(dy, sv["z2"], wl["ln_g"][1], 1.0, f"ln_bwd_mix_{l}")
        w_mix = _rows_view(wl["m_o"], MIX_SHARD, 0)
        if l % 2 == 0:
            gl["mix_out"], _ = _dw(sv["mix_in"], dzb, f"dw_o_{l}")
            do = mm_plain(dzb, w_mix, BF16, f"attn_do_{l}", nt=True)
            dq, dkc, dkp, dvc, dvp, dsk = attn_bwd(sv["qr"], sv["kr"], sv["vv"], do, sinks[j], f"attn_core_bwd_{l}", tq)
            gl["sinks"] = dsk[:, :GROUP, 0].reshape(N_HEADS)
            dmid = rope_bwd(dq, dkc, dkp, dvc, dvp, cosf, sinf, f"rope_bwd_{l}", tq)
            gl["mix_in"], _ = _dw(sv["h1b"], dmid, f"dw_qkv_{l}", shard=QKV // N_DEV)
        else:
            gl["mix_out"], _ = _dw(sv["mix_in"], dzb, f"dw_lru_out_{l}")
            dyl = mm_plain(dzb, w_mix, F32, f"lru_dy_{l}", nt=True)
            dmid, dcw, dcb, dwra, dwrx, dbra, dbrx, dlam = lru_bwd(dyl, sv["xg"], sv["hstate"], wl["lru"], f"lru_core_bwd_{l}")
            gl.update(conv_w=dcw, conv_b=dcb[0], w_ra=dwra, w_rx=dwrx, b_ra=dbra[0], b_rx=dbrx[0],
                      lam=dlam[0] * wl["lru"]["sig_neg"])
            gl["mix_in"], _ = _dw(sv["h1b"], dmid, f"dw_lru_in_{l}", shard=2 * D_RNN // N_DEV)
        dy = mm_res_nt(dmid, _mat_view(wl["m_c"]), dz, ALPHA, f"mix_dx_{l}")
        sched.grads_ready(l, "hi", gl)
        lo = {}
        dy, dg[0], db[0], lo["w_out1"], lo["w_in1"] = ffn_bwd(
            dy, sv["z1"], wl["ln_g"][0], sv["gu1"], sv["a1"], sv["h0b"], wl["r_ff"], 0, wl["w_in1"], f"1_{l}", True)
        lo["ln_g"], lo["ln_b"] = jnp.concatenate(dg, axis=0), jnp.concatenate(db, axis=0)
        sched.grads_ready(l, "lo", lo)
    return loss, dy


def _lru_params(full, j, w_ra, w_rx):
    lam = full["lru_lambda"][j]
    return {
        "conv_w": full["lru_conv_w"][j], "conv_b": full["lru_conv_b"][j].reshape(1, -1),
        "w_ra": w_ra, "w_rx": w_rx,
        "b_ra": full["lru_b_ra"][j].reshape(1, -1), "b_rx": full["lru_b_rx"][j].reshape(1, -1),
        "lsl": jax.nn.log_sigmoid(lam).reshape(1, -1), "sig_neg": jax.nn.sigmoid(-lam),
    }


def _row_shards(f):
    return f.reshape(N_DEV, -1, f.shape[1])


def _gate_shards(w):
    return w.reshape(RNN_BLOCKS, N_DEV, RNN_W // N_DEV, RNN_W).transpose(1, 0, 2, 3).reshape(N_DEV, -1, RNN_W)


class _Fsdp:
    def __init__(self, args):
        self.args = args
        self.b16 = lambda a: a.astype(BF16)
        self.sm = _small_pack(args, "")
        self.sent = {l: self._ag_arrays(l) for l in range(DEPTH)}
        self.queue = [(l, p) for l in range(DEPTH) for p in "abc"][1:]
        self.hosting = None
        got = run_comm(comm_ag(self.sent[0]["a"] + [self.sm]), "all_gather_first")
        self.raw = {0: {"a": got[:-1]}}
        gflat = got[-1].reshape(N_DEV, SM_ROWS * 1024)
        self.small, off = {}, 0
        for n, shape, axis in SMALL:
            size = math.prod(shape)
            self.small[n] = _full_from_gathered(gflat[:, off:off + size], shape, axis)
            off += size
        self.bufs = {"r": jax.ShapeDtypeStruct((4, R_ROWS, D_MODEL), BF16),
                     "c704": jax.ShapeDtypeStruct((4, 2 * DEPTH * D_MODEL, 704), BF16),
                     "c192": jax.ShapeDtypeStruct((4, 2 * D_MODEL, 192), BF16),
                     "c256": jax.ShapeDtypeStruct((4, C256_ROWS, RNN_W), BF16),
                     "sm": jax.ShapeDtypeStruct((4, SM_ROWS, 1024), F32)}
        self.pending = None
        self.small_g = [{} for _ in range(DEPTH)]
        self.loss = None

    def _ag_arrays(self, l):
        a, b16, j = self.args, self.b16, l // 2
        r_ff = jnp.concatenate([b16(a["ffn1_w_out"][l]), b16(a["ffn2_w_out"][l])], axis=0)
        if l % 2 == 0:
            return {"a": [b16(a["ffn1_w_in"][l])], "b": [r_ff, b16(a["attn_w_o"][j]), b16(a["attn_w_qkv"][j])],
                    "c": [b16(a["ffn2_w_in"][l])]}
        gates = jnp.concatenate([b16(a["lru_w_ra"][j]).reshape(-1, RNN_W), b16(a["lru_w_rx"][j]).reshape(-1, RNN_W)], axis=0)
        return {"a": [b16(a["ffn1_w_in"][l])], "b": [r_ff, b16(a["lru_w_out"][j]), b16(a["lru_w_in"][j]), gates],
                "c": [b16(a["ffn2_w_in"][l])]}

    def weights(self, l, piece):
        raw, j = self.raw[l][piece], l // 2
        full = lambda g: g.transpose(1, 0, 2).reshape(g.shape[1], -1)
        if piece == "a":
            return {"w_in1": full(raw[0]), "ln_g": self.small["ln_g"][l], "ln_b": self.small["ln_b"][l]}
        if piece == "c":
            return {"w_in2": full(raw[0])}
        wl = {"r_ff": raw[0], "m_o": raw[1], "m_c": full(raw[2])}
        if l % 2:
            g = raw[3].reshape(N_DEV, 2, RNN_BLOCKS, RNN_W // N_DEV, RNN_W).transpose(1, 2, 0, 3, 4)
            g = g.reshape(2, RNN_BLOCKS, RNN_W, RNN_W)
            wl["lru"] = _lru_params(self.small, j, g[0], g[1])
        return wl

    def fwd_slot(self):
        if not self.queue:
            return None
        self.hosting = self.queue.pop(0)
        l, piece = self.hosting
        return comm_ag(self.sent[l][piece])

    def fwd_done(self, outs):
        if self.hosting is not None:
            l, piece = self.hosting
            self.raw.setdefault(l, {})[piece] = outs
            self.hosting = None

    def loss_ready(self, loss):
        self.loss = loss

    def _ranges(self, l, part):
        j = l // 2
        if part == "lo":
            return [("r", [(0, FF_SHARD, l * FF_SHARD)]), ("c704", [(0, D_MODEL, l * D_MODEL)])]
        where = [("r", [(0, FF_SHARD, R_FFN2 + l * FF_SHARD)]),
                 ("r", [(0, MIX_SHARD, (R_LRU if l % 2 else R_ATTN) + j * MIX_SHARD)]),
                 ("c704", [(0, D_MODEL, (DEPTH + l) * D_MODEL)])]
        if l % 2:
            gr = RNN_BLOCKS * (RNN_W // N_DEV)
            return where + [("c256", [(0, D_MODEL, j * D_MODEL)]), ("c256", [(0, gr, C256_RA + j * gr)]),
                            ("c256", [(0, gr, C256_RX + j * gr)])]
        return where + [("c192", [(0, D_MODEL, j * D_MODEL)])]

    def grads_ready(self, l, part, g):
        self.small_g[l].update(g)
        if part == "lo":
            ts = [_row_shards(g["w_out1"]), g["w_in1"]]
        else:
            ts = [_row_shards(g["w_out2"]), _row_shards(g["mix_out"]), g["w_in2"], g["mix_in"]]
            if l % 2:
                ts += [_gate_shards(g["w_ra"]), _gate_shards(g["w_rx"])]
        sends = [t.reshape((4, 2) + t.shape[1:]) for t in ts]
        if (l, part) == (0, "lo"):
            self._last(sends)
        else:
            self.pending = {"tag": f"{l}{part}", "sends": sends, "where": self._ranges(l, part)}

    def _pair_sums(self, tag, sends, gots):
        return pair_sum(sends, gots, [BF16] * len(sends), f"pair_sum_{tag}")

    def _chip(self, parts, where, pick):
        names = list(dict.fromkeys(where[i][0] for i in pick))
        return names, comm_chip([(parts[i], names.index(where[i][0]), where[i][1]) for i in pick], [self.bufs[n] for n in names])

    def bwd_slot(self, slot):
        p = self.pending
        if p is None:
            return None
        if slot == "pair":
            return comm_pair(p["sends"])
        wide = [i for i, (name, _) in enumerate(p["where"]) if name == "c704"]
        pick = wide if slot == "chip_x" else [i for i in range(len(p["sends"])) if i not in wide]
        p["names"], comm = self._chip(p["parts"], p["where"], pick)
        return comm

    def bwd_done(self, slot, outs):
        p = self.pending
        if slot == "pair":
            p["parts"] = self._pair_sums(p["tag"], p["sends"], outs)
            return
        for n, o in zip(p["names"], outs):
            self.bufs[n] = o
        if slot == "chip_y":
            self.pending = None

    def _last(self, sends):
        sg = self.small_g
        stack = lambda key, ls: jnp.stack([sg[l][key] for l in ls])
        every, lru = range(DEPTH), (1, 3)
        small = {"ln_g": stack("ln_g", every), "ln_b": stack("ln_b", every), "lru_conv_w": stack("conv_w", lru),
                 "lru_conv_b": stack("conv_b", lru), "lru_b_ra": stack("b_ra", lru), "lru_b_rx": stack("b_rx", lru),
                 "lru_lambda": stack("lam", lru)}
        tail = jnp.concatenate([stack("sinks", (0, 2)).reshape(-1), self.loss.reshape(1)])
        tail = jnp.pad(tail, (0, SM_ROWS * 1024 - SMALL_N - tail.shape[0]))
        s_sm = jnp.concatenate([_shards_of_full(small[n], shape, axis) for n, shape, axis in SMALL]
                               + [jnp.broadcast_to(tail, (N_DEV, tail.shape[0]))], axis=1).reshape(4, 2, SM_ROWS, 1024)
        gots = run_comm(comm_pair(sends + [s_sm]), "pair_exchange_last")
        parts = pair_sum(sends + [s_sm], gots, [BF16, BF16, F32], "pair_sum_last")
        names, comm = self._chip(parts, self._ranges(0, "lo") + [("sm", [(0, SM_ROWS, 0)])], [0, 1, 2])
        for n, o in zip(names, run_comm(comm, "chip_exchange_last")):
            self.bufs[n] = o


def kernel(x, ffn1_w_in, ffn1_w_out, ffn2_w_in, ffn2_w_out, ln_g, ln_b, attn_w_qkv, attn_sinks, attn_w_o, lru_w_in, lru_conv_w, lru_conv_b, lru_w_ra, lru_b_ra, lru_w_rx, lru_b_rx, lru_lambda, lru_w_out, loss_target, m_ffn1_w_in, m_ffn1_w_out, m_ffn2_w_in, m_ffn2_w_out, m_ln_g, m_ln_b, m_attn_w_qkv, m_attn_sinks, m_attn_w_o, m_lru_w_in, m_lru_conv_w, m_lru_conv_b, m_lru_w_ra, m_lru_b_ra, m_lru_w_rx, m_lru_b_rx, m_lru_lambda, m_lru_w_out, v_ffn1_w_in, v_ffn1_w_out, v_ffn2_w_in, v_ffn2_w_out, v_ln_g, v_ln_b, v_attn_w_qkv, v_attn_sinks, v_attn_w_o, v_lru_w_in, v_lru_conv_w, v_lru_conv_b, v_lru_w_ra, v_lru_b_ra, v_lru_w_rx, v_lru_b_rx, v_lru_lambda, v_lru_w_out):
    args = dict(locals())

    sched = _Fsdp(args)
    _, dx = _local_step(x[0], loss_target[0], sched, attn_sinks)
    p_r, p704, p192, p256, p_sm = (sched.bufs[n] for n in ("r", "c704", "c192", "c256", "sm"))
    sm = sched.sm

    fam = lambda n: (args[n], args["m_" + n], args["v_" + n])
    res = {
        "ffn1_w_out": adamw(p_r, 0, *fam("ffn1_w_out"), FF_SHARD, "adamw_ffn1_w_out"),
        "ffn2_w_out": adamw(p_r, R_FFN2, *fam("ffn2_w_out"), FF_SHARD, "adamw_ffn2_w_out"),
        "attn_w_o": adamw(p_r, R_ATTN, *fam("attn_w_o"), 2 * MIX_SHARD, "adamw_attn_w_o"),
        "lru_w_out": adamw(p_r, R_LRU, *fam("lru_w_out"), 2 * MIX_SHARD, "adamw_lru_w_out"),
        "ffn1_w_in": adamw(p704, 0, *fam("ffn1_w_in"), 512, "adamw_ffn1_w_in"),
        "ffn2_w_in": adamw(p704, DEPTH * D_MODEL, *fam("ffn2_w_in"), 512, "adamw_ffn2_w_in"),
        "attn_w_qkv": adamw(p192, 0, *fam("attn_w_qkv"), 512, "adamw_attn_w_qkv"),
        "lru_w_in": adamw(p256, 0, *fam("lru_w_in"), 512, "adamw_lru_w_in"),
        "lru_w_ra": adamw(p256, C256_RA, *fam("lru_w_ra"), 256, "adamw_lru_w_ra"),
        "lru_w_rx": adamw(p256, C256_RX, *fam("lru_w_rx"), 256, "adamw_lru_w_rx"),
    }
    sm_out = adamw(p_sm, 0, sm, _small_pack(args, "m_"), _small_pack(args, "v_"), SM_ROWS, "adamw_small")
    for k, pack in enumerate(sm_out):
        for n, val in _small_unpack(pack).items():
            res.setdefault(n, [None] * 4)[k] = val
    loss_total = sm_out[0].reshape(-1)[LOSS_OFF]
    out = [loss_total, dx[None]]
    for k in range(4):
        out += [res[n][k] for n in WEIGHTS]
    return tuple(out)
```
